```python
import math
import jax, jax.numpy as jnp
from jax import lax
import numpy as np

D_MODEL = 2048
BATCH = 1
SEQ = 16384
DEPTH = 4

POOL_WIDTH = D_MODEL // 2
POOL_WINDOWS = (2, 4, 8, 16)
POOL_GROUP = POOL_WIDTH // len(POOL_WINDOWS)
DIFF_WIDTH = D_MODEL // 2
DIFF_HEAD_DIM = 64
DIFF_HEADS = DIFF_WIDTH // (2 * DIFF_HEAD_DIM)
AB_WIDTH = POOL_WIDTH + DIFF_WIDTH
AB_IN = POOL_WIDTH + 3 * DIFF_WIDTH + AB_WIDTH
FOURIER_WIDTH = D_MODEL
FOURIER_GROUPS = 4
FOURIER_GROUP = FOURIER_WIDTH // FOURIER_GROUPS
C_IN = 2 * FOURIER_WIDTH
Q_BLOCK = 128
NORM_EPS = 1e-6
N_AB_LAYERS = (DEPTH + 1) // 2
N_C_LAYERS = DEPTH // 2

kernel_name = "hybrid_pool_diffattn_fourier_encoder"


def _alibi_slopes(n_heads):
    return jnp.asarray([2.0 ** (-8.0 * (h + 1) / n_heads) for h in range(n_heads)], dtype=jnp.float32)


def _rms(x, w):
    xf = x.astype(jnp.float32)
    y = xf * lax.rsqrt(jnp.mean(xf * xf, axis=-1, keepdims=True) + NORM_EPS)
    return (y * w.astype(jnp.float32)).astype(x.dtype)


def _modulate(x, norm_w, shift, scale):
    xf = x.astype(jnp.float32)
    y = xf * lax.rsqrt(jnp.mean(xf * xf, axis=-1, keepdims=True) + NORM_EPS) * norm_w.astype(jnp.float32)
    y = y * (1.0 + scale[:, None, :].astype(jnp.float32)) + shift[:, None, :].astype(jnp.float32)
    return y.astype(x.dtype)


def _pool_mix(u, w_pool, pool_scale):
    B, S, _ = u.shape
    t = jnp.arange(S)
    groups = u.reshape(B, S, len(POOL_WINDOWS), POOL_GROUP)
    outs = []
    for g, w in enumerate(POOL_WINDOWS):
        ug = groups[:, :, g].astype(jnp.float32)
        cs = jnp.pad(jnp.cumsum(ug, axis=1), ((0, 0), (1, 0), (0, 0)))
        lo = jnp.clip(t - w // 2, 0, S - 1)
        hi = jnp.clip(t + (w - w // 2) - 1, 0, S - 1)
        win_sum = cs[:, hi + 1] - cs[:, lo]
        cnt = (hi - lo + 1).astype(jnp.float32)
        pooled = (win_sum / cnt[None, :, None] - ug).astype(u.dtype)
        outs.append(jnp.einsum('bsc,cd->bsd', pooled, w_pool[g]))
    return jnp.concatenate(outs, axis=-1) * pool_scale


def _diff_attn(q, k, v, q_norm_w, k_norm_w, lq1, lk1, lq2, lk2, subln_w, lambda_init):
    B, S, H, _, d = q.shape
    q = _rms(q, q_norm_w)
    k = _rms(k, k_norm_w)
    lam = (jnp.exp(jnp.sum(lq1.astype(jnp.float32) * lk1.astype(jnp.float32)))
           - jnp.exp(jnp.sum(lq2.astype(jnp.float32) * lk2.astype(jnp.float32))) + lambda_init)
    slopes = _alibi_slopes(H)
    scale = d ** -0.5
    n_blk = S // Q_BLOCK
    pos = jnp.arange(S)
    q_blocks = q.reshape(B, n_blk, Q_BLOCK, H, 2, d).transpose(1, 0, 2, 3, 4, 5)
    pos_blocks = pos.reshape(n_blk, Q_BLOCK)

    def block(args):
        qb, qpos = args
        s = jnp.einsum('bqhjd,bkhjd->bhjqk', qb, k).astype(jnp.float32) * scale
        dist = jnp.abs(qpos[:, None] - pos[None, :]).astype(jnp.float32)
        s = s - slopes[None, :, None, None, None] * dist[None, None, None]
        p = jax.nn.softmax(s, axis=-1)
        a = (p[:, :, 0] - lam * p[:, :, 1]).astype(v.dtype)
        return jnp.einsum('bhqk,bkhe->bqhe', a, v)

    o = lax.map(block, (q_blocks, pos_blocks))
    o = o.transpose(1, 0, 2, 3, 4).reshape(B, S, H, 2 * d)
    o = _rms(o, subln_w) * (1.0 - lambda_init)
    return o.reshape(B, S, H * 2 * d)


def _fourier_mix(u, w_fourier):
    B, S, _ = u.shape
    ug = u.reshape(B, S, FOURIER_GROUPS, FOURIER_GROUP).astype(jnp.float32)
    f = jnp.fft.fft2(ug, axes=(1, 3), norm='ortho').real.astype(u.dtype)
    return jnp.einsum('bsgc,gcd->bsgd', f, w_fourier).reshape(B, S, FOURIER_WIDTH)


def setup_inputs(seed: int = 0) -> dict:
    key = jax.random.key(seed)
    ks = jax.random.split(key, 20)
    f32 = jnp.float32
    nrm = lambda k, shape, s: jax.random.normal(k, shape, f32) * s
    return {
        "x": nrm(ks[0], (BATCH, SEQ, D_MODEL), 1.0),
        "c": nrm(ks[1], (BATCH, D_MODEL), 1.0),
        "norm_w": 1.0 + nrm(ks[2], (DEPTH, D_MODEL), 0.02),
        "ada_w": nrm(ks[3], (DEPTH, D_MODEL, 3 * D_MODEL), D_MODEL ** -0.5),
        "ada_b": nrm(ks[4], (DEPTH, 3 * D_MODEL), 0.02),
        "w_in_ab": nrm(ks[5], (N_AB_LAYERS, D_MODEL, AB_IN), D_MODEL ** -0.5),
        "w_pool": nrm(ks[6], (N_AB_LAYERS, len(POOL_WINDOWS), POOL_GROUP, POOL_GROUP), POOL_GROUP ** -0.5),
        "pool_scale": 1.0 + nrm(ks[7], (N_AB_LAYERS, POOL_WIDTH), 0.02),
        "q_norm_w": 1.0 + nrm(ks[8], (N_AB_LAYERS, DIFF_HEAD_DIM), 0.02),
        "k_norm_w": 1.0 + nrm(ks[9], (N_AB_LAYERS, DIFF_HEAD_DIM), 0.02),
        "lambda_q1": nrm(ks[10], (N_AB_LAYERS, DIFF_HEAD_DIM), 0.1),
        "lambda_k1": nrm(ks[11], (N_AB_LAYERS, DIFF_HEAD_DIM), 0.1),
        "lambda_q2": nrm(ks[12], (N_AB_LAYERS, DIFF_HEAD_DIM), 0.1),
        "lambda_k2": nrm(ks[13], (N_AB_LAYERS, DIFF_HEAD_DIM), 0.1),
        "subln_w": 1.0 + nrm(ks[14], (N_AB_LAYERS, 2 * DIFF_HEAD_DIM), 0.02),
        "w_out_ab": nrm(ks[15], (N_AB_LAYERS, AB_WIDTH, D_MODEL), AB_WIDTH ** -0.5),
        "w_in_c": nrm(ks[16], (N_C_LAYERS, D_MODEL, C_IN), D_MODEL ** -0.5),
        "w_fourier": nrm(ks[17], (N_C_LAYERS, FOURIER_GROUPS, FOURIER_GROUP, FOURIER_GROUP), FOURIER_GROUP ** -0.5),
        "w_out_c": nrm(ks[18], (N_C_LAYERS, FOURIER_WIDTH, D_MODEL), FOURIER_WIDTH ** -0.5),
    }


def reference(x, c, norm_w, ada_w, ada_b, w_in_ab, w_pool, pool_scale, q_norm_w, k_norm_w,
              lambda_q1, lambda_k1, lambda_q2, lambda_k2, subln_w, w_out_ab, w_in_c, w_fourier, w_out_c):
    B, S, _ = x.shape
    H, d = DIFF_HEADS, DIFF_HEAD_DIM
    c_act = jax.nn.silu(c)
    for i in range(DEPTH):
        mod = c_act @ ada_w[i] + ada_b[i]
        shift, scale, gate = jnp.split(mod, 3, axis=-1)
        h = _modulate(x, norm_w[i], shift, scale)
        j = i // 2
        if i % 2 == 0:
            z = h @ w_in_ab[j]
            o1 = POOL_WIDTH
            o2 = o1 + DIFF_WIDTH
            o3 = o2 + DIFF_WIDTH
            o4 = o3 + DIFF_WIDTH
            u_pool = z[..., :o1]
            q = z[..., o1:o2].reshape(B, S, H, 2, d)
            k = z[..., o2:o3].reshape(B, S, H, 2, d)
            v = z[..., o3:o4].reshape(B, S, H, 2 * d)
            g = z[..., o4:]
            lambda_init = 0.8 - 0.6 * math.exp(-0.3 * i)
            y_a = _pool_mix(u_pool, w_pool[j], pool_scale[j])
            y_b = _diff_attn(q, k, v, q_norm_w[j], k_norm_w[j], lambda_q1[j], lambda_k1[j],
                             lambda_q2[j], lambda_k2[j], subln_w[j], lambda_init)
            y = jnp.concatenate([y_a, y_b], axis=-1) * jax.nn.silu(g)
            out = y @ w_out_ab[j]
        else:
            z = h @ w_in_c[j]
            u, g = z[..., :FOURIER_WIDTH], z[..., FOURIER_WIDTH:]
            y = _fourier_mix(u, w_fourier[j]) * jax.nn.silu(g)
            out = y @ w_out_c[j]
        x = x + gate[:, None, :] * out
    return x
```

```python
import functools
import math

import jax
import jax.numpy as jnp
from jax import lax
from jax.experimental import pallas as pl
from jax.experimental.pallas import tpu as pltpu

F32 = jnp.float32
BF16 = jnp.bfloat16

D_MODEL = 2048
DEPTH = 4
NORM_EPS = 1e-6

POOL_WINDOWS = (2, 4, 8, 16)
POOL_WIDTH = D_MODEL // 2
POOL_GROUP = POOL_WIDTH // len(POOL_WINDOWS)
POOL_HALO = 8

HEADS = 8
HEAD_DIM = 64
HEAD_V = 2 * HEAD_DIM
DIFF_WIDTH = HEADS * HEAD_V
AB_WIDTH = POOL_WIDTH + DIFF_WIDTH

FOURIER_GROUPS = 4
FOURIER_GROUP = D_MODEL // FOURIER_GROUPS

LANES = 128
SUBLANES = 8
BF16_ROWS = 16
POS_RADIX = 256

ATT_TQ = 512
ATT_TK = 512
V_ROWS = HEAD_V + BF16_ROWS
NEG_BIG = -1e30

VMEM_LIMIT = 56 * 1024 * 1024


def _params(n_axes):
    return pltpu.CompilerParams(dimension_semantics=("arbitrary",) * n_axes,
                                vmem_limit_bytes=VMEM_LIMIT)


def _mod_kernel(c_ref, w_ref, b_ref, o_ref):
    c = c_ref[...]
    o_ref[...] = jnp.sum(jax.nn.silu(c) * w_ref[...], axis=0, keepdims=True) + b_ref[...]


def _ada_mod(c, ada_w, ada_b):
    depth, d, n = ada_w.shape
    tn = 512
    return pl.pallas_call(
        _mod_kernel,
        grid=(depth, n // tn),
        in_specs=[pl.BlockSpec((d, 1), lambda i, j: (0, 0)),
                  pl.BlockSpec((None, d, tn), lambda i, j: (i, 0, j)),
                  pl.BlockSpec((None, 1, tn), lambda i, j: (i, 0, j))],
        out_specs=pl.BlockSpec((None, 1, tn), lambda i, j: (i, 0, j)),
        out_shape=jax.ShapeDtypeStruct((depth, 1, n), F32),
        compiler_params=_params(2),
        name="ada_mod",
    )(c.reshape(d, 1), ada_w, ada_b.reshape(depth, 1, n))


def _modulate_kernel(x_ref, nw_ref, shift_ref, scale_ref, o_ref):
    xf = x_ref[...]
    r = lax.rsqrt(jnp.mean(xf * xf, axis=-1, keepdims=True) + NORM_EPS)
    y = xf * r * nw_ref[...]
    y = y * (1.0 + scale_ref[...]) + shift_ref[...]
    o_ref[...] = y.astype(o_ref.dtype)


def _modulate(x, nw, shift, scale):
    s, d = x.shape
    tm = 512
    vec = pl.BlockSpec((1, d), lambda i: (0, 0))
    return pl.pallas_call(
        _modulate_kernel,
        grid=(s // tm,),
        in_specs=[pl.BlockSpec((tm, d), lambda i: (i, 0)), vec, vec, vec],
        out_specs=pl.BlockSpec((tm, d), lambda i: (i, 0)),
        out_shape=jax.ShapeDtypeStruct((s, d), BF16),
        compiler_params=_params(1),
        name="modulate",
    )(x, nw, shift, scale)


def _mm_kernel(a_ref, w_ref, o_ref, *, precision):
    o_ref[...] = jnp.dot(a_ref[...], w_ref[...], preferred_element_type=F32,
                         precision=precision).astype(o_ref.dtype)


def _matmul(a, w, out_dtype, tm, tn, precision=None, name="matmul"):
    m, k = a.shape
    n = w.shape[1]
    tm, tn = min(tm, m), min(tn, n)
    return pl.pallas_call(
        functools.partial(_mm_kernel, precision=precision),
        grid=(m // tm, n // tn),
        in_specs=[pl.BlockSpec((tm, k), lambda i, j: (i, 0)),
                  pl.BlockSpec((k, tn), lambda i, j: (0, j))],
        out_specs=pl.BlockSpec((tm, tn), lambda i, j: (i, j)),
        out_shape=jax.ShapeDtypeStruct((m, n), out_dtype),
        compiler_params=_params(2),
        name=name,
    )(a, w)


def _qkv_kernel(h_ref, wq_ref, wk_ref, wv_ref, qnw_ref, knw_ref, qt_ref, ka_ref, vt_ref, *, tm):
    i = pl.program_id(0)
    h = h_ref[...]
    lane = lax.broadcasted_iota(jnp.int32, (tm, LANES), 1)
    first = lane < HEAD_DIM
    pos = lax.broadcasted_iota(jnp.int32, (tm, LANES), 0) + i * tm
    pos_lo = (pos % POS_RADIX).astype(F32)
    pos_hi = ((pos % ATT_TK) // POS_RADIX).astype(F32)
    pos_cols = jnp.where(lane == HEAD_DIM, pos_lo, jnp.where(lane == HEAD_DIM + 1, pos_hi, 0.0))
    ones_rows = (lax.broadcasted_iota(jnp.int32, (BF16_ROWS, ATT_TK), 0) == 0).astype(BF16)

    def halves_rms(z, w):
        sq = z * z
        ss1 = jnp.sum(jnp.where(first, sq, 0.0), axis=-1, keepdims=True)
        ss2 = jnp.sum(jnp.where(first, 0.0, sq), axis=-1, keepdims=True)
        r = jnp.where(first, lax.rsqrt(ss1 / HEAD_DIM + NORM_EPS), lax.rsqrt(ss2 / HEAD_DIM + NORM_EPS))
        return z * r * w

    zq = jnp.dot(h, wq_ref[...], preferred_element_type=F32)
    zk = jnp.dot(h, wk_ref[...], preferred_element_type=F32)
    zv = jnp.dot(h, wv_ref[...], preferred_element_type=F32)
    for hh in range(2):
        cols = slice(hh * LANES, (hh + 1) * LANES)
        qn = halves_rms(zq[:, cols], qnw_ref[...]) * (HEAD_DIM ** -0.5)
        kn = halves_rms(zk[:, cols], knw_ref[...])
        for m in range(2):
            qm = qn if m == 0 else pltpu.roll(qn, HEAD_DIM, axis=1)
            km = kn if m == 0 else pltpu.roll(kn, HEAD_DIM, axis=1)
            qt_ref[hh, m] = jnp.where(first, qm, 0.0).T.astype(BF16)
            ka = jnp.where(first, km, pos_cols).astype(BF16)
            for cc in range(tm // ATT_TK):
                ka_ref[hh, m, cc] = ka[cc * ATT_TK:(cc + 1) * ATT_TK]
        v = zv[:, cols]
        for cc in range(tm // ATT_TK):
            vt_ref[hh, cc, 0:HEAD_V, :] = v[cc * ATT_TK:(cc + 1) * ATT_TK].T.astype(BF16)
            vt_ref[hh, cc, HEAD_V:V_ROWS, :] = ones_rows


def _qkv_proj(h, wq, wk, wv, qnw, knw):
    s, d = h.shape
    tm = min(1024, s)
    nchunk = s // ATT_TK
    cpt = tm // ATT_TK
    tn = 2 * LANES
    wspec = pl.BlockSpec((d, tn), lambda i, j: (0, j))
    vec = pl.BlockSpec((1, LANES), lambda i, j: (0, 0))
    return pl.pallas_call(
        functools.partial(_qkv_kernel, tm=tm),
        grid=(s // tm, HEADS // 2),
        in_specs=[pl.BlockSpec((tm, d), lambda i, j: (i, 0)), wspec, wspec, wspec, vec, vec],
        out_specs=[pl.BlockSpec((2, 2, LANES, tm), lambda i, j: (j, 0, 0, i)),
                   pl.BlockSpec((2, 2, cpt, ATT_TK, LANES), lambda i, j: (j, 0, i, 0, 0)),
                   pl.BlockSpec((2, cpt, V_ROWS, ATT_TK), lambda i, j: (j, i, 0, 0))],
        out_shape=[jax.ShapeDtypeStruct((HEADS, 2, LANES, s), BF16),
                   jax.ShapeDtypeStruct((HEADS, 2, nchunk, ATT_TK, LANES), BF16),
                   jax.ShapeDtypeStruct((HEADS, nchunk, V_ROWS, ATT_TK), BF16)],
        compiler_params=_params(2),
        name="qkv_proj",
    )(h, wq, wk, wv, jnp.tile(qnw, (1, 2)), jnp.tile(knw, (1, 2)))


def _attn_kernel(slopes_ref, qt_ref, ka_ref, vt_ref, lq1_ref, lk1_ref, lq2_ref, lk2_ref, subw_ref,
                 o_ref, acc_ref, m_ref, *, lambda_init, nchunk):
    hd = pl.program_id(0)
    i = pl.program_id(1)
    slope = slopes_ref[hd]
    rowq = lax.broadcasted_iota(jnp.int32, (LANES, ATT_TQ), 0)
    q_off = lax.broadcasted_iota(jnp.int32, (1, ATT_TQ), 1).astype(F32)

    def with_pos_rows(q, c_lo):
        qf = q.astype(F32)
        return jnp.where(rowq == HEAD_DIM, c_lo, jnp.where(rowq == HEAD_DIM + 1, c_lo * POS_RADIX, qf)).astype(BF16)

    q_diag = [qt_ref[0, m] for m in range(2)]
    q_before = [with_pos_rows(q, slope) for q in q_diag]
    q_after = [with_pos_rows(q, -slope) for q in q_diag]

    acc_ref[...] = jnp.zeros(acc_ref.shape, F32)
    m_ref[...] = jnp.full(m_ref.shape, NEG_BIG, F32)

    def step(c, qs, cvec, bias):
        for m in range(2):
            st = jnp.dot(ka_ref[0, m, c], qs[m], preferred_element_type=F32)
            if bias is not None:
                st = st - bias
            m_old = m_ref[m]
            m_new = jnp.maximum(m_old, jnp.max(st, axis=0, keepdims=True) + cvec)
            p = jnp.exp(st - (m_new - cvec))
            pv = jnp.dot(vt_ref[0, c], p.astype(BF16), preferred_element_type=F32)
            acc_ref[m] = acc_ref[m] * jnp.exp(m_old - m_new) + pv
            m_ref[m] = m_new

    def before(c, carry):
        gap = ((i - c) * ATT_TK).astype(F32)
        step(c, q_before, -slope * (gap + q_off), None)
        return carry

    def after(c, carry):
        gap = ((c - i) * ATT_TK).astype(F32)
        step(c, q_after, -slope * (gap - q_off), None)
        return carry

    lax.fori_loop(0, i, before, 0)
    key_off = lax.broadcasted_iota(jnp.int32, (ATT_TK, ATT_TQ), 0)
    qry_off = lax.broadcasted_iota(jnp.int32, (ATT_TK, ATT_TQ), 1)
    step(i, q_diag, jnp.zeros((1, ATT_TQ), F32), slope * jnp.abs(key_off - qry_off).astype(F32))
    lax.fori_loop(i + 1, nchunk, after, 0)

    lam = (jnp.exp(jnp.sum(lq1_ref[...] * lk1_ref[...], axis=-1, keepdims=True))
           - jnp.exp(jnp.sum(lq2_ref[...] * lk2_ref[...], axis=-1, keepdims=True)) + lambda_init)
    a1 = acc_ref[0]
    a2 = acc_ref[1]
    o = a1[0:HEAD_V] / a1[HEAD_V:HEAD_V + 1] - lam * (a2[0:HEAD_V] / a2[HEAD_V:HEAD_V + 1])
    r = lax.rsqrt(jnp.mean(o * o, axis=0, keepdims=True) + NORM_EPS)
    y = (o * r * subw_ref[...]) * (1.0 - lambda_init)
    o_ref[...] = y.T


def _diff_attn(qt, ka, vt, lq1, lk1, lq2, lk2, subw, lambda_init):
    s = qt.shape[-1]
    nchunk = s // ATT_TK
    slopes = jnp.asarray([2.0 ** (-8.0 * (h + 1) / HEADS) for h in range(HEADS)], F32)
    vec = pl.BlockSpec((1, HEAD_DIM), lambda h, i: (0, 0))
    return pl.pallas_call(
        functools.partial(_attn_kernel, lambda_init=lambda_init, nchunk=nchunk),
        grid=(HEADS, s // ATT_TQ),
        in_specs=[pl.BlockSpec(memory_space=pltpu.SMEM),
                  pl.BlockSpec((1, 2, LANES, ATT_TQ), lambda h, i: (h, 0, 0, i)),
                  pl.BlockSpec((1, 2, nchunk, ATT_TK, LANES), lambda h, i: (h, 0, 0, 0, 0)),
                  pl.BlockSpec((1, nchunk, V_ROWS, ATT_TK), lambda h, i: (h, 0, 0, 0)),
                  vec, vec, vec, vec,
                  pl.BlockSpec((HEAD_V, 1), lambda h, i: (0, 0))],
        out_specs=pl.BlockSpec((ATT_TQ, HEAD_V), lambda h, i: (i, h)),
        out_shape=jax.ShapeDtypeStruct((s, DIFF_WIDTH), F32),
        scratch_shapes=[pltpu.VMEM((2, V_ROWS, ATT_TQ), F32), pltpu.VMEM((2, 1, ATT_TQ), F32)],
        compiler_params=_params(2),
        name="diff_attn",
    )(slopes, qt, ka, vt, lq1, lk1, lq2, lk2, subw.reshape(HEAD_V, 1))


def _pool_kernel(prev_ref, cur_ref, next_ref, w_ref, scale_ref, o_ref, ext_ref, *, tm, seq):
    i = pl.program_id(0)
    last = pl.num_programs(0) - 1
    zero_halo = jnp.zeros((POOL_HALO, POOL_WIDTH), F32)
    ext_ref[0:POOL_HALO] = jnp.where(i > 0, prev_ref[...], zero_halo)
    ext_ref[POOL_HALO:POOL_HALO + tm] = cur_ref[...]
    ext_ref[POOL_HALO + tm:POOL_HALO + tm + POOL_HALO] = jnp.where(i < last, next_ref[...], zero_halo)
    t = lax.broadcasted_iota(jnp.int32, (tm, 1), 0) + i * tm
    for g, w in enumerate(POOL_WINDOWS):
        cols = pl.ds(g * POOL_GROUP, POOL_GROUP)
        win = ext_ref[pl.ds(POOL_HALO - w // 2, tm), cols]
        for d in range(-(w // 2) + 1, w - w // 2):
            win = win + ext_ref[pl.ds(POOL_HALO + d, tm), cols]
        lo = jnp.maximum(t - w // 2, 0)
        hi = jnp.minimum(t + (w - w // 2) - 1, seq - 1)
        cnt = (hi - lo + 1).astype(F32)
        pooled = win / cnt - cur_ref[:, cols]
        y = jnp.dot(pooled.astype(BF16), w_ref[g], preferred_element_type=F32)
        o_ref[:, cols] = y * scale_ref[:, cols]


def _pool_mix(u, w_pool, pool_scale):
    s, width = u.shape
    tm = 512
    hb = tm // POOL_HALO
    nhalo = s // POOL_HALO
    return pl.pallas_call(
        functools.partial(_pool_kernel, tm=tm, seq=s),
        grid=(s // tm,),
        in_specs=[pl.BlockSpec((POOL_HALO, width), lambda i: (jnp.maximum(i * hb - 1, 0), 0)),
                  pl.BlockSpec((tm, width), lambda i: (i, 0)),
                  pl.BlockSpec((POOL_HALO, width), lambda i: (jnp.minimum((i + 1) * hb, nhalo - 1), 0)),
                  pl.BlockSpec(w_pool.shape, lambda i: (0, 0, 0)),
                  pl.BlockSpec((1, width), lambda i: (0, 0))],
        out_specs=pl.BlockSpec((tm, width), lambda i: (i, 0)),
        out_shape=jax.ShapeDtypeStruct((s, width), F32),
        scratch_shapes=[pltpu.VMEM((tm + 2 * POOL_HALO, width), F32)],
        compiler_params=_params(1),
        name="pool_mix",
    )(u, u, u, w_pool, pool_scale)


def _out_kernel(*refs, widths):
    n = len(widths)
    y_refs = refs[:n]
    g_ref, w_ref, x_ref, gate_ref, o_ref, yb_ref = refs[n:]

    @pl.when(pl.program_id(1) == 0)
    def _():
        off = 0
        for y_ref, wd in zip(y_refs, widths):
            yb_ref[:, off:off + wd] = (y_ref[...] * jax.nn.silu(g_ref[:, off:off + wd])).astype(BF16)
            off += wd

    acc = jnp.dot(yb_ref[...], w_ref[...], preferred_element_type=F32)
    o_ref[...] = x_ref[...] + gate_ref[...] * acc


def _out_proj(ys, g, w, x, gate):
    s, d = x.shape
    k = w.shape[0]
    tm, tn = 512, 512
    widths = tuple(y.shape[1] for y in ys)
    return pl.pallas_call(
        functools.partial(_out_kernel, widths=widths),
        grid=(s // tm, d // tn),
        in_specs=[pl.BlockSpec((tm, wd), lambda i, j: (i, 0)) for wd in widths]
        + [pl.BlockSpec((tm, k), lambda i, j: (i, 0)),
           pl.BlockSpec((k, tn), lambda i, j: (0, j)),
           pl.BlockSpec((tm, tn), lambda i, j: (i, j)),
           pl.BlockSpec((1, tn), lambda i, j: (0, j))],
        out_specs=pl.BlockSpec((tm, tn), lambda i, j: (i, j)),
        out_shape=jax.ShapeDtypeStruct((s, d), F32),
        scratch_shapes=[pltpu.VMEM((tm, k), BF16)],
        compiler_params=_params(2),
        name="out_proj",
    )(*ys, g, w, x, gate)


def _dft_split(s):
    n1 = 1 << (int(math.log2(s)) // 2)
    n2 = s // n1
    assert n1 * n2 == s and n1 % SUBLANES == 0 and n2 % SUBLANES == 0
    return n1, n2


def _angles(num, den):
    ang = (2.0 * math.pi / den) * (num % den).astype(F32)
    return jnp.cos(ang), jnp.sin(ang)


def _fold_cs_kernel(cs_ref, wf_ref, o_ref):
    wf = wf_ref[...]
    o_ref[:, 0:FOURIER_GROUP] = jnp.dot(cs_ref[0], wf, preferred_element_type=F32,
                                        precision=lax.Precision.HIGHEST)
    o_ref[:, FOURIER_GROUP:] = jnp.dot(cs_ref[1], wf, preferred_element_type=F32,
                                       precision=lax.Precision.HIGHEST)


def _fold_channel_dft(w_fourier_j, seq):
    fg = FOURIER_GROUP
    idx = jnp.arange(fg, dtype=jnp.int32)
    cc, sc = _angles(idx[:, None] * idx[None, :], fg)
    cs = jnp.stack([cc, sc]) * (1.0 / math.sqrt(seq * fg))
    return pl.pallas_call(
        _fold_cs_kernel,
        grid=(FOURIER_GROUPS,),
        in_specs=[pl.BlockSpec((2, fg, fg), lambda g: (0, 0, 0)),
                  pl.BlockSpec((None, fg, fg), lambda g: (g, 0, 0))],
        out_specs=pl.BlockSpec((None, fg, 2 * fg), lambda g: (g, 0, 0)),
        out_shape=jax.ShapeDtypeStruct((FOURIER_GROUPS, fg, 2 * fg), F32),
        compiler_params=_params(1),
        name="fold_channel_dft",
    )(cs, w_fourier_j)


def _fold_win_kernel(w_ref, ab_ref, p_ref, q_ref):
    r = jnp.dot(w_ref[...], ab_ref[...].astype(BF16), preferred_element_type=F32)
    p_ref[...] = r[:, 0:FOURIER_GROUP].astype(BF16)
    q_ref[...] = r[:, FOURIER_GROUP:].astype(BF16)


def _fold_in_proj(w_in_u, ab):
    d = w_in_u.shape[0]
    fg = FOURIER_GROUP
    ospec = pl.BlockSpec((d, fg), lambda g: (0, g))
    return pl.pallas_call(
        _fold_win_kernel,
        grid=(FOURIER_GROUPS,),
        in_specs=[pl.BlockSpec((d, fg), lambda g: (0, g)),
                  pl.BlockSpec((None, fg, 2 * fg), lambda g: (g, 0, 0))],
        out_specs=[ospec, ospec],
        out_shape=[jax.ShapeDtypeStruct((d, D_MODEL), BF16)] * 2,
        compiler_params=_params(1),
        name="fold_in_proj",
    )(w_in_u, ab)


def _dft1_kernel(f_ref, p_ref, q_ref, t_ref, *, nb, n1):
    f = f_ref[...]
    for jj in range(nb):
        rows = slice(jj * n1, (jj + 1) * n1)
        rhs = jnp.concatenate([p_ref[rows, :], q_ref[rows, :]], axis=0)
        t_ref[jj] = jnp.dot(f, rhs, preferred_element_type=F32).astype(BF16)


def _dft_stage1(pq_t, f1, n1, n2):
    s, c2 = pq_t.shape
    c = c2 // 2
    nb = SUBLANES
    cb = min(1024, c)
    ncb = c // cb
    return pl.pallas_call(
        functools.partial(_dft1_kernel, nb=nb, n1=n1),
        grid=(n2 // nb, ncb),
        in_specs=[pl.BlockSpec((2 * n1, 2 * n1), lambda b, j: (0, 0)),
                  pl.BlockSpec((nb * n1, cb), lambda b, j: (b, j)),
                  pl.BlockSpec((nb * n1, cb), lambda b, j: (b, j + ncb))],
        out_specs=pl.BlockSpec((nb, 2 * n1, cb), lambda b, j: (b, 0, j)),
        out_shape=jax.ShapeDtypeStruct((n2, 2 * n1, c), BF16),
        compiler_params=_params(2),
        name="dft_stage1",
    )(f1, pq_t, pq_t)


def _dft2_kernel(l_ref, t_ref, o_ref, *, n2, cb):
    rhs = t_ref[...].reshape(SUBLANES * 2 * n2, cb)
    y = jnp.dot(l_ref[0], rhs, preferred_element_type=F32)
    o_ref[...] = y.reshape(n2, SUBLANES, cb)


def _dft_stage2(t_kn, l2, n1, n2):
    c = t_kn.shape[-1]
    cb = min(1024, c)
    out = pl.pallas_call(
        functools.partial(_dft2_kernel, n2=n2, cb=cb),
        grid=(n1 // SUBLANES, c // cb),
        in_specs=[pl.BlockSpec((1, SUBLANES * n2, SUBLANES * 2 * n2), lambda g, j: (g, 0, 0)),
                  pl.BlockSpec((SUBLANES, 2, n2, cb), lambda g, j: (g, 0, 0, j))],
        out_specs=pl.BlockSpec((n2, SUBLANES, cb), lambda g, j: (0, g, j)),
        out_shape=jax.ShapeDtypeStruct((n2, n1, c), F32),
        compiler_params=_params(2),
        name="dft_stage2",
    )(l2, t_kn)
    return out.reshape(n1 * n2, c)


def _dft_matrices(s):
    n1, n2 = _dft_split(s)
    a = jnp.arange(n1, dtype=jnp.int32)
    c1, s1 = _angles(a[:, None] * a[None, :], n1)
    f1 = jnp.concatenate([jnp.concatenate([c1, -s1], axis=1),
                          jnp.concatenate([-s1, -c1], axis=1)], axis=0).astype(BF16)
    groups = n1 // SUBLANES
    g = jnp.arange(groups, dtype=jnp.int32)[:, None, None, None]
    k2 = jnp.arange(n2, dtype=jnp.int32)[None, :, None, None]
    j = jnp.arange(SUBLANES, dtype=jnp.int32)[None, None, :, None]
    nn = jnp.arange(n2, dtype=jnp.int32)[None, None, None, :]
    c2, s2 = _angles((SUBLANES * g + j + n1 * k2) * nn, s)
    cs = jnp.stack([c2, s2], axis=3).astype(BF16)
    eye = jnp.eye(SUBLANES, dtype=BF16)
    l2 = cs[:, :, :, None] * eye[None, None, :, :, None, None]
    return n1, n2, f1, l2.reshape(groups, n2 * SUBLANES, SUBLANES * 2 * n2)


def kernel(x, c, norm_w, ada_w, ada_b, w_in_ab, w_pool, pool_scale, q_norm_w, k_norm_w, lambda_q1, lambda_k1,
           lambda_q2, lambda_k2, subln_w, w_out_ab, w_in_c, w_fourier, w_out_c):
    batch, s, d = x.shape
    assert batch == 1 and d == D_MODEL and s % ATT_TK == 0
    xs = x.reshape(s, d)
    mod = _ada_mod(c, ada_w, ada_b)
    n1, n2, f1, l2 = _dft_matrices(s)
    row = lambda v: v.reshape(1, -1)

    for i in range(DEPTH):
        shift, scale, gate = mod[i, :, 0:d], mod[i, :, d:2 * d], mod[i, :, 2 * d:3 * d]
        h = _modulate(xs, row(norm_w[i]), shift, scale)
        j = i // 2
        if i % 2 == 0:
            w_in = w_in_ab[j].astype(BF16)
            o1, o2, o3, o4 = POOL_WIDTH, POOL_WIDTH + DIFF_WIDTH, POOL_WIDTH + 2 * DIFF_WIDTH, POOL_WIDTH + 3 * DIFF_WIDTH
            lambda_init = 0.8 - 0.6 * math.exp(-0.3 * i)
            u_pool = _matmul(h, w_in[:, :o1], F32, 1024, 512, name="pool_in_proj")
            gte = _matmul(h, w_in[:, o4:], F32, 1024, 512, name="gate_in_proj")
            qt, ka, vt = _qkv_proj(h, w_in[:, o1:o2], w_in[:, o2:o3], w_in[:, o3:o4],
                                   row(q_norm_w[j]), row(k_norm_w[j]))
            y_a = _pool_mix(u_pool, w_pool[j].astype(BF16), row(pool_scale[j]))
            y_b = _diff_attn(qt, ka, vt, row(lambda_q1[j]), row(lambda_k1[j]), row(lambda_q2[j]),
                             row(lambda_k2[j]), subln_w[j], lambda_init)
            xs = _out_proj([y_a, y_b], gte, w_out_ab[j].astype(BF16), xs, gate)
        else:
            w_in = w_in_c[j].astype(BF16)
            ab = _fold_channel_dft(w_fourier[j], s)
            w_p, w_q = _fold_in_proj(w_in[:, :d], ab)
            gte = _matmul(h, w_in[:, d:], F32, 1024, 512, name="gate_in_proj")
            h_t = h.reshape(n1, n2, d).transpose(1, 0, 2).reshape(s, d)
            pq_t = _matmul(h_t, jnp.concatenate([w_p, w_q], axis=1), BF16, 1024, 512, name="fourier_in_proj")
            t_nk = _dft_stage1(pq_t, f1, n1, n2)
            t_kn = t_nk.reshape(n2, 2, n1, d).transpose(2, 1, 0, 3)
            f = _dft_stage2(t_kn, l2, n1, n2)
            xs = _out_proj([f], gte, w_out_c[j].astype(BF16), xs, gate)
    return xs.reshape(batch, s, d)
```

```python
import functools
import math

import jax
import jax.numpy as jnp
from jax import lax
from jax.experimental import pallas as pl
from jax.experimental.pallas import tpu as pltpu

F32 = jnp.float32
BF16 = jnp.bfloat16

D_MODEL = 2048
DEPTH = 4
NORM_EPS = 1e-6

POOL_WINDOWS = (2, 4, 8, 16)
POOL_WIDTH = D_MODEL // 2
POOL_GROUP = POOL_WIDTH // len(POOL_WINDOWS)
POOL_HALO = 8

HEADS = 8
HEAD_DIM = 64
HEAD_V = 2 * HEAD_DIM
DIFF_WIDTH = HEADS * HEAD_V
AB_WIDTH = POOL_WIDTH + DIFF_WIDTH

FOURIER_GROUPS = 4
FOURIER_GROUP = D_MODEL // FOURIER_GROUPS

LANES = 128
SUBLANES = 8
BF16_ROWS = 16
POS_RADIX = 256

ATT_TQ = 512
ATT_TK = 512
V_ROWS = HEAD_V + BF16_ROWS
NEG_BIG = -1e30

VMEM_LIMIT = 56 * 1024 * 1024


def _params(n_axes):
    return pltpu.CompilerParams(dimension_semantics=("arbitrary",) * n_axes,
                                vmem_limit_bytes=VMEM_LIMIT)


def _mod_kernel(c_ref, w_ref, b_ref, o_ref):
    c = c_ref[...]
    o_ref[...] = jnp.sum(jax.nn.silu(c) * w_ref[...], axis=0, keepdims=True) + b_ref[...]


def _ada_mod(c, ada_w, ada_b):
    depth, d, n = ada_w.shape
    tn = 512
    return pl.pallas_call(
        _mod_kernel,
        grid=(depth, n // tn),
        in_specs=[pl.BlockSpec((d, 1), lambda i, j: (0, 0)),
                  pl.BlockSpec((None, d, tn), lambda i, j: (i, 0, j)),
                  pl.BlockSpec((None, 1, tn), lambda i, j: (i, 0, j))],
        out_specs=pl.BlockSpec((None, 1, tn), lambda i, j: (i, 0, j)),
        out_shape=jax.ShapeDtypeStruct((depth, 1, n), F32),
        compiler_params=_params(2),
        name="ada_mod",
    )(c.reshape(d, 1), ada_w, ada_b.reshape(depth, 1, n))


def _modulate_kernel(x_ref, nw_ref, shift_ref, scale_ref, o_ref):
    xf = x_ref[...]
    r = lax.rsqrt(jnp.mean(xf * xf, axis=-1, keepdims=True) + NORM_EPS)
    y = xf * r * nw_ref[...]
    y = y * (1.0 + scale_ref[...]) + shift_ref[...]
    o_ref[...] = y.astype(o_ref.dtype)


def _modulate(x, nw, shift, scale):
    s, d = x.shape
    tm = 512
    vec = pl.BlockSpec((1, d), lambda i: (0, 0))
    return pl.pallas_call(
        _modulate_kernel,
        grid=(s // tm,),
        in_specs=[pl.BlockSpec((tm, d), lambda i: (i, 0)), vec, vec, vec],
        out_specs=pl.BlockSpec((tm, d), lambda i: (i, 0)),
        out_shape=jax.ShapeDtypeStruct((s, d), BF16),
        compiler_params=_params(1),
        name="modulate",
    )(x, nw, shift, scale)


def _mm_kernel(a_ref, w_ref, o_ref, *, precision):
    o_ref[...] = jnp.dot(a_ref[...], w_ref[...], preferred_element_type=F32,
                         precision=precision).astype(o_ref.dtype)


def _matmul(a, w, out_dtype, tm, tn, precision=None, name="matmul"):
    m, k = a.shape
    n = w.shape[1]
    tm, tn = min(tm, m), min(tn, n)
    return pl.pallas_call(
        functools.partial(_mm_kernel, precision=precision),
        grid=(m // tm, n // tn),
        in_specs=[pl.BlockSpec((tm, k), lambda i, j: (i, 0)),
                  pl.BlockSpec((k, tn), lambda i, j: (0, j))],
        out_specs=pl.BlockSpec((tm, tn), lambda i, j: (i, j)),
        out_shape=jax.ShapeDtypeStruct((m, n), out_dtype),
        compiler_params=_params(2),
        name=name,
    )(a, w)


def _qkv_kernel(h_ref, wq_ref, wk_ref, wv_ref, qnw_ref, knw_ref, qt_ref, ka_ref, vt_ref, *, tm):
    i = pl.program_id(0)
    h = h_ref[...]
    lane = lax.broadcasted_iota(jnp.int32, (tm, LANES), 1)
    first = lane < HEAD_DIM
    pos = lax.broadcasted_iota(jnp.int32, (tm, LANES), 0) + i * tm
    pos_lo = (pos % POS_RADIX).astype(F32)
    pos_hi = ((pos % ATT_TK) // POS_RADIX).astype(F32)
    pos_cols = jnp.where(lane == HEAD_DIM, pos_lo, jnp.where(lane == HEAD_DIM + 1, pos_hi, 0.0))
    ones_rows = (lax.broadcasted_iota(jnp.int32, (BF16_ROWS, ATT_TK), 0) == 0).astype(BF16)

    def halves_rms(z, w):
        sq = z * z
        ss1 = jnp.sum(jnp.where(first, sq, 0.0), axis=-1, keepdims=True)
        ss2 = jnp.sum(jnp.where(first, 0.0, sq), axis=-1, keepdims=True)
        r = jnp.where(first, lax.rsqrt(ss1 / HEAD_DIM + NORM_EPS), lax.rsqrt(ss2 / HEAD_DIM + NORM_EPS))
        return z * r * w

    zq = jnp.dot(h, wq_ref[...], preferred_element_type=F32)
    zk = jnp.dot(h, wk_ref[...], preferred_element_type=F32)
    zv = jnp.dot(h, wv_ref[...], preferred_element_type=F32)
    for hh in range(2):
        cols = slice(hh * LANES, (hh + 1) * LANES)
        qn = halves_rms(zq[:, cols], qnw_ref[...]) * (HEAD_DIM ** -0.5)
        kn = halves_rms(zk[:, cols], knw_ref[...])
        for m in range(2):
            qm = qn if m == 0 else pltpu.roll(qn, HEAD_DIM, axis=1)
            km = kn if m == 0 else pltpu.roll(kn, HEAD_DIM, axis=1)
            qt_ref[hh, m] = jnp.where(first, qm, 0.0).T.astype(BF16)
            ka = jnp.where(first, km, pos_cols).astype(BF16)
            for cc in range(tm // ATT_TK):
                ka_ref[hh, m, cc] = ka[cc * ATT_TK:(cc + 1) * ATT_TK]
        v = zv[:, cols]
        for cc in range(tm // ATT_TK):
            vt_ref[hh, cc, 0:HEAD_V, :] = v[cc * ATT_TK:(cc + 1) * ATT_TK].T.astype(BF16)
            vt_ref[hh, cc, HEAD_V:V_ROWS, :] = ones_rows


def _qkv_proj(h, wq, wk, wv, qnw, knw):
    s, d = h.shape
    tm = min(1024, s)
    nchunk = s // ATT_TK
    cpt = tm // ATT_TK
    tn = 2 * LANES
    wspec = pl.BlockSpec((d, tn), lambda i, j: (0, j))
    vec = pl.BlockSpec((1, LANES), lambda i, j: (0, 0))
    return pl.pallas_call(
        functools.partial(_qkv_kernel, tm=tm),
        grid=(s // tm, HEADS // 2),
        in_specs=[pl.BlockSpec((tm, d), lambda i, j: (i, 0)), wspec, wspec, wspec, vec, vec],
        out_specs=[pl.BlockSpec((2, 2, LANES, tm), lambda i, j: (j, 0, 0, i)),
                   pl.BlockSpec((2, 2, cpt, ATT_TK, LANES), lambda i, j: (j, 0, i, 0, 0)),
                   pl.BlockSpec((2, cpt, V_ROWS, ATT_TK), lambda i, j: (j, i, 0, 0))],
        out_shape=[jax.ShapeDtypeStruct((HEADS, 2, LANES, s), BF16),
                   jax.ShapeDtypeStruct((HEADS, 2, nchunk, ATT_TK, LANES), BF16),
                   jax.ShapeDtypeStruct((HEADS, nchunk, V_ROWS, ATT_TK), BF16)],
        compiler_params=_params(2),
        name="qkv_proj",
    )(h, wq, wk, wv, jnp.tile(qnw, (1, 2)), jnp.tile(knw, (1, 2)))


def _attn_kernel(slopes_ref, qt_ref, ka_ref, vt_ref, lq1_ref, lk1_ref, lq2_ref, lk2_ref, subw_ref,
                 o_ref, acc_ref, m_ref, qv_ref, st_a, st_b, tm_a, tm_b, *, lambda_init, nchunk):
    hd = pl.program_id(0)
    i = pl.program_id(1)
    slope = slopes_ref[hd]
    rowq = lax.broadcasted_iota(jnp.int32, (LANES, ATT_TQ), 0)
    q_off = lax.broadcasted_iota(jnp.int32, (1, ATT_TQ), 1).astype(F32)

    for m in range(2):
        qf = qt_ref[0, m].astype(F32)
        for kind, coef in ((0, -slope), (1, 0.0), (2, slope)):
            qv_ref[kind, m] = jnp.where(rowq == HEAD_DIM, coef,
                                        jnp.where(rowq == HEAD_DIM + 1, coef * POS_RADIX, qf)).astype(BF16)
    acc_ref[...] = jnp.zeros(acc_ref.shape, F32)
    m_ref[...] = jnp.full(m_ref.shape, NEG_BIG, F32)

    def qk(c, st_ref, tm_ref):
        kind = 1 + (i > c).astype(jnp.int32) - (i < c).astype(jnp.int32)
        for m in range(2):
            st = jnp.dot(ka_ref[0, m, c], qv_ref[kind, m], preferred_element_type=F32)
            st_ref[m] = st
            tm_ref[m] = jnp.max(st, axis=0, keepdims=True)

    def fix_diag(c, st_ref, tm_ref):
        @pl.when(c == i)
        def _():
            key_off = lax.broadcasted_iota(jnp.int32, (ATT_TK, ATT_TQ), 0)
            qry_off = lax.broadcasted_iota(jnp.int32, (ATT_TK, ATT_TQ), 1)
            bias = slope * jnp.abs(key_off - qry_off).astype(F32)
            for m in range(2):
                st = st_ref[m] - bias
                st_ref[m] = st
                tm_ref[m] = jnp.max(st, axis=0, keepdims=True)

    def softmax_pv(c, st_ref, tm_ref):
        gap = i - c
        sgn = (gap > 0).astype(F32) - (gap < 0).astype(F32)
        cvec = -slope * ((jnp.abs(gap) * ATT_TK).astype(F32) + sgn * q_off)
        for m in range(2):
            m_old = m_ref[m]
            m_new = jnp.maximum(m_old, tm_ref[m] + cvec)
            p = jnp.exp(st_ref[m] - (m_new - cvec))
            pv = jnp.dot(vt_ref[0, c], p.astype(BF16), preferred_element_type=F32)
            acc_ref[m] = acc_ref[m] * jnp.exp(m_old - m_new) + pv
            m_ref[m] = m_new

    def pair(c0, last):
        fix_diag(c0, st_a, tm_a)
        qk(c0 + 1, st_b, tm_b)
        softmax_pv(c0, st_a, tm_a)
        fix_diag(c0 + 1, st_b, tm_b)
        if not last:
            qk(c0 + 2, st_a, tm_a)
        softmax_pv(c0 + 1, st_b, tm_b)

    def pair_body(cc, carry):
        pair(2 * cc, False)
        return carry

    qk(0, st_a, tm_a)
    lax.fori_loop(0, nchunk // 2 - 1, pair_body, 0)
    pair(nchunk - 2, True)

    lam = (jnp.exp(jnp.sum(lq1_ref[...] * lk1_ref[...], axis=-1, keepdims=True))
           - jnp.exp(jnp.sum(lq2_ref[...] * lk2_ref[...], axis=-1, keepdims=True)) + lambda_init)
    a1 = acc_ref[0]
    a2 = acc_ref[1]
    o = a1[0:HEAD_V] / a1[HEAD_V:HEAD_V + 1] - lam * (a2[0:HEAD_V] / a2[HEAD_V:HEAD_V + 1])
    r = lax.rsqrt(jnp.mean(o * o, axis=0, keepdims=True) + NORM_EPS)
    y = (o * r * subw_ref[...]) * (1.0 - lambda_init)
    o_ref[...] = y.T


def _diff_attn(qt, ka, vt, lq1, lk1, lq2, lk2, subw, lambda_init):
    s = qt.shape[-1]
    nchunk = s // ATT_TK
    assert nchunk % 2 == 0
    slopes = jnp.asarray([2.0 ** (-8.0 * (h + 1) / HEADS) for h in range(HEADS)], F32)
    vec = pl.BlockSpec((1, HEAD_DIM), lambda h, i: (0, 0))
    scores = pltpu.VMEM((2, ATT_TK, ATT_TQ), F32)
    colmax = pltpu.VMEM((2, 1, ATT_TQ), F32)
    return pl.pallas_call(
        functools.partial(_attn_kernel, lambda_init=lambda_init, nchunk=nchunk),
        grid=(HEADS, s // ATT_TQ),
        in_specs=[pl.BlockSpec(memory_space=pltpu.SMEM),
                  pl.BlockSpec((1, 2, LANES, ATT_TQ), lambda h, i: (h, 0, 0, i)),
                  pl.BlockSpec((1, 2, nchunk, ATT_TK, LANES), lambda h, i: (h, 0, 0, 0, 0)),
                  pl.BlockSpec((1, nchunk, V_ROWS, ATT_TK), lambda h, i: (h, 0, 0, 0)),
                  vec, vec, vec, vec,
                  pl.BlockSpec((HEAD_V, 1), lambda h, i: (0, 0))],
        out_specs=pl.BlockSpec((ATT_TQ, HEAD_V), lambda h, i: (i, h)),
        out_shape=jax.ShapeDtypeStruct((s, DIFF_WIDTH), F32),
        scratch_shapes=[pltpu.VMEM((2, V_ROWS, ATT_TQ), F32), pltpu.VMEM((2, 1, ATT_TQ), F32),
                        pltpu.VMEM((3, 2, LANES, ATT_TQ), BF16), scores, scores, colmax, colmax],
        compiler_params=_params(2),
        name="diff_attn",
    )(slopes, qt, ka, vt, lq1, lk1, lq2, lk2, subw.reshape(HEAD_V, 1))


def _pool_kernel(prev_ref, cur_ref, next_ref, w_ref, scale_ref, o_ref, ext_ref, *, tm, seq):
    i = pl.program_id(0)
    last = pl.num_programs(0) - 1
    zero_halo = jnp.zeros((POOL_HALO, POOL_WIDTH), F32)
    ext_ref[0:POOL_HALO] = jnp.where(i > 0, prev_ref[...], zero_halo)
    ext_ref[POOL_HALO:POOL_HALO + tm] = cur_ref[...]
    ext_ref[POOL_HALO + tm:POOL_HALO + tm + POOL_HALO] = jnp.where(i < last, next_ref[...], zero_halo)
    t = lax.broadcasted_iota(jnp.int32, (tm, 1), 0) + i * tm
    for g, w in enumerate(POOL_WINDOWS):
        cols = pl.ds(g * POOL_GROUP, POOL_GROUP)
        win = ext_ref[pl.ds(POOL_HALO - w // 2, tm), cols]
        for d in range(-(w // 2) + 1, w - w // 2):
            win = win + ext_ref[pl.ds(POOL_HALO + d, tm), cols]
        lo = jnp.maximum(t - w // 2, 0)
        hi = jnp.minimum(t + (w - w // 2) - 1, seq - 1)
        cnt = (hi - lo + 1).astype(F32)
        pooled = win / cnt - cur_ref[:, cols]
        y = jnp.dot(pooled.astype(BF16), w_ref[g], preferred_element_type=F32)
        o_ref[:, cols] = y * scale_ref[:, cols]


def _pool_mix(u, w_pool, pool_scale):
    s, width = u.shape
    tm = 512
    hb = tm // POOL_HALO
    nhalo = s // POOL_HALO
    return pl.pallas_call(
        functools.partial(_pool_kernel, tm=tm, seq=s),
        grid=(s // tm,),
        in_specs=[pl.BlockSpec((POOL_HALO, width), lambda i: (jnp.maximum(i * hb - 1, 0), 0)),
                  pl.BlockSpec((tm, width), lambda i: (i, 0)),
                  pl.BlockSpec((POOL_HALO, width), lambda i: (jnp.minimum((i + 1) * hb, nhalo - 1), 0)),
                  pl.BlockSpec(w_pool.shape, lambda i: (0, 0, 0)),
                  pl.BlockSpec((1, width), lambda i: (0, 0))],
        out_specs=pl.BlockSpec((tm, width), lambda i: (i, 0)),
        out_shape=jax.ShapeDtypeStruct((s, width), F32),
        scratch_shapes=[pltpu.VMEM((tm + 2 * POOL_HALO, width), F32)],
        compiler_params=_params(1),
        name="pool_mix",
    )(u, u, u, w_pool, pool_scale)


def _out_kernel(*refs, widths):
    n = len(widths)
    y_refs = refs[:n]
    g_ref, w_ref, x_ref, gate_ref, o_ref, yb_ref = refs[n:]

    @pl.when(pl.program_id(1) == 0)
    def _():
        off = 0
        for y_ref, wd in zip(y_refs, widths):
            yb_ref[:, off:off + wd] = (y_ref[...] * jax.nn.silu(g_ref[:, off:off + wd])).astype(BF16)
            off += wd

    acc = jnp.dot(yb_ref[...], w_ref[...], preferred_element_type=F32)
    o_ref[...] = x_ref[...] + gate_ref[...] * acc


def _out_proj(ys, g, w, x, gate):
    s, d = x.shape
    k = w.shape[0]
    tm, tn = 512, 512
    widths = tuple(y.shape[1] for y in ys)
    return pl.pallas_call(
        functools.partial(_out_kernel, widths=widths),
        grid=(s // tm, d // tn),
        in_specs=[pl.BlockSpec((tm, wd), lambda i, j: (i, 0)) for wd in widths]
        + [pl.BlockSpec((tm, k), lambda i, j: (i, 0)),
           pl.BlockSpec((k, tn), lambda i, j: (0, j)),
           pl.BlockSpec((tm, tn), lambda i, j: (i, j)),
           pl.BlockSpec((1, tn), lambda i, j: (0, j))],
        out_specs=pl.BlockSpec((tm, tn), lambda i, j: (i, j)),
        out_shape=jax.ShapeDtypeStruct((s, d), F32),
        scratch_shapes=[pltpu.VMEM((tm, k), BF16)],
        compiler_params=_params(2),
        name="out_proj",
    )(*ys, g, w, x, gate)


def _dft_split(s):
    n1 = 1 << (int(math.log2(s)) // 2)
    n2 = s // n1
    assert n1 * n2 == s and n1 % SUBLANES == 0 and n2 % SUBLANES == 0
    return n1, n2


def _angles(num, den):
    ang = (2.0 * math.pi / den) * (num % den).astype(F32)
    return jnp.cos(ang), jnp.sin(ang)


def _fold_cs_kernel(cs_ref, wf_ref, o_ref):
    wf = wf_ref[...]
    o_ref[:, 0:FOURIER_GROUP] = jnp.dot(cs_ref[0], wf, preferred_element_type=F32,
                                        precision=lax.Precision.HIGHEST)
    o_ref[:, FOURIER_GROUP:] = jnp.dot(cs_ref[1], wf, preferred_element_type=F32,
                                       precision=lax.Precision.HIGHEST)


def _fold_channel_dft(w_fourier_j, seq):
    fg = FOURIER_GROUP
    idx = jnp.arange(fg, dtype=jnp.int32)
    cc, sc = _angles(idx[:, None] * idx[None, :], fg)
    cs = jnp.stack([cc, sc]) * (1.0 / math.sqrt(seq * fg))
    return pl.pallas_call(
        _fold_cs_kernel,
        grid=(FOURIER_GROUPS,),
        in_specs=[pl.BlockSpec((2, fg, fg), lambda g: (0, 0, 0)),
                  pl.BlockSpec((None, fg, fg), lambda g: (g, 0, 0))],
        out_specs=pl.BlockSpec((None, fg, 2 * fg), lambda g: (g, 0, 0)),
        out_shape=jax.ShapeDtypeStruct((FOURIER_GROUPS, fg, 2 * fg), F32),
        compiler_params=_params(1),
        name="fold_channel_dft",
    )(cs, w_fourier_j)


def _fold_win_kernel(w_ref, ab_ref, p_ref, q_ref):
    r = jnp.dot(w_ref[...], ab_ref[...].astype(BF16), preferred_element_type=F32)
    p_ref[...] = r[:, 0:FOURIER_GROUP].astype(BF16)
    q_ref[...] = r[:, FOURIER_GROUP:].astype(BF16)


def _fold_in_proj(w_in_u, ab):
    d = w_in_u.shape[0]
    fg = FOURIER_GROUP
    ospec = pl.BlockSpec((d, fg), lambda g: (0, g))
    return pl.pallas_call(
        _fold_win_kernel,
        grid=(FOURIER_GROUPS,),
        in_specs=[pl.BlockSpec((d, fg), lambda g: (0, g)),
                  pl.BlockSpec((None, fg, 2 * fg), lambda g: (g, 0, 0))],
        out_specs=[ospec, ospec],
        out_shape=[jax.ShapeDtypeStruct((d, D_MODEL), BF16)] * 2,
        compiler_params=_params(1),
        name="fold_in_proj",
    )(w_in_u, ab)


def _dft1_kernel(f_ref, p_ref, q_ref, t_ref, *, nb, n1):
    f = f_ref[...]
    for jj in range(nb):
        rows = slice(jj * n1, (jj + 1) * n1)
        rhs = jnp.concatenate([p_ref[rows, :], q_ref[rows, :]], axis=0)
        t_ref[jj] = jnp.dot(f, rhs, preferred_element_type=F32).astype(BF16)


def _dft_stage1(pq_t, f1, n1, n2):
    s, c2 = pq_t.shape
    c = c2 // 2
    nb = SUBLANES
    cb = min(1024, c)
    ncb = c // cb
    return pl.pallas_call(
        functools.partial(_dft1_kernel, nb=nb, n1=n1),
        grid=(n2 // nb, ncb),
        in_specs=[pl.BlockSpec((2 * n1, 2 * n1), lambda b, j: (0, 0)),
                  pl.BlockSpec((nb * n1, cb), lambda b, j: (b, j)),
                  pl.BlockSpec((nb * n1, cb), lambda b, j: (b, j + ncb))],
        out_specs=pl.BlockSpec((nb, 2 * n1, cb), lambda b, j: (b, 0, j)),
        out_shape=jax.ShapeDtypeStruct((n2, 2 * n1, c), BF16),
        compiler_params=_params(2),
        name="dft_stage1",
    )(f1, pq_t, pq_t)


def _dft2_kernel(l_ref, t_ref, o_ref, *, n2, cb):
    rhs = t_ref[...].reshape(SUBLANES * 2 * n2, cb)
    y = jnp.dot(l_ref[0], rhs, preferred_element_type=F32)
    o_ref[...] = y.reshape(n2, SUBLANES, cb)


def _dft_stage2(t_kn, l2, n1, n2):
    c = t_kn.shape[-1]
    cb = min(1024, c)
    out = pl.pallas_call(
        functools.partial(_dft2_kernel, n2=n2, cb=cb),
        grid=(n1 // SUBLANES, c // cb),
        in_specs=[pl.BlockSpec((1, SUBLANES * n2, SUBLANES * 2 * n2), lambda g, j: (g, 0, 0)),
                  pl.BlockSpec((SUBLANES, 2, n2, cb), lambda g, j: (g, 0, 0, j))],
        out_specs=pl.BlockSpec((n2, SUBLANES, cb), lambda g, j: (0, g, j)),
        out_shape=jax.ShapeDtypeStruct((n2, n1, c), F32),
        compiler_params=_params(2),
        name="dft_stage2",
    )(l2, t_kn)
    return out.reshape(n1 * n2, c)


def _dft_matrices(s):
    n1, n2 = _dft_split(s)
    a = jnp.arange(n1, dtype=jnp.int32)
    c1, s1 = _angles(a[:, None] * a[None, :], n1)
    f1 = jnp.concatenate([jnp.concatenate([c1, -s1], axis=1),
                          jnp.concatenate([-s1, -c1], axis=1)], axis=0).astype(BF16)
    groups = n1 // SUBLANES
    g = jnp.arange(groups, dtype=jnp.int32)[:, None, None, None]
    k2 = jnp.arange(n2, dtype=jnp.int32)[None, :, None, None]
    j = jnp.arange(SUBLANES, dtype=jnp.int32)[None, None, :, None]
    nn = jnp.arange(n2, dtype=jnp.int32)[None, None, None, :]
    c2, s2 = _angles((SUBLANES * g + j + n1 * k2) * nn, s)
    cs = jnp.stack([c2, s2], axis=3).astype(BF16)
    eye = jnp.eye(SUBLANES, dtype=BF16)
    l2 = cs[:, :, :, None] * eye[None, None, :, :, None, None]
    return n1, n2, f1, l2.reshape(groups, n2 * SUBLANES, SUBLANES * 2 * n2)


def kernel(x, c, norm_w, ada_w, ada_b, w_in_ab, w_pool, pool_scale, q_norm_w, k_norm_w, lambda_q1, lambda_k1,
           lambda_q2, lambda_k2, subln_w, w_out_ab, w_in_c, w_fourier, w_out_c):
    batch, s, d = x.shape
    assert batch == 1 and d == D_MODEL and s % ATT_TK == 0
    xs = x.reshape(s, d)
    mod = _ada_mod(c, ada_w, ada_b)
    n1, n2, f1, l2 = _dft_matrices(s)
    row = lambda v: v.reshape(1, -1)

    for i in range(DEPTH):
        shift, scale, gate = mod[i, :, 0:d], mod[i, :, d:2 * d], mod[i, :, 2 * d:3 * d]
        h = _modulate(xs, row(norm_w[i]), shift, scale)
        j = i // 2
        if i % 2 == 0:
            w_in = w_in_ab[j].astype(BF16)
            o1, o2, o3, o4 = POOL_WIDTH, POOL_WIDTH + DIFF_WIDTH, POOL_WIDTH + 2 * DIFF_WIDTH, POOL_WIDTH + 3 * DIFF_WIDTH
            lambda_init = 0.8 - 0.6 * math.exp(-0.3 * i)
            u_pool = _matmul(h, w_in[:, :o1], F32, 1024, 512, name="pool_in_proj")
            gte = _matmul(h, w_in[:, o4:], F32, 1024, 512, name="gate_in_proj")
            qt, ka, vt = _qkv_proj(h, w_in[:, o1:o2], w_in[:, o2:o3], w_in[:, o3:o4],
                                   row(q_norm_w[j]), row(k_norm_w[j]))
            y_a = _pool_mix(u_pool, w_pool[j].astype(BF16), row(pool_scale[j]))
            y_b = _diff_attn(qt, ka, vt, row(lambda_q1[j]), row(lambda_k1[j]), row(lambda_q2[j]),
                             row(lambda_k2[j]), subln_w[j], lambda_init)
            xs = _out_proj([y_a, y_b], gte, w_out_ab[j].astype(BF16), xs, gate)
        else:
            w_in = w_in_c[j].astype(BF16)
            ab = _fold_channel_dft(w_fourier[j], s)
            w_p, w_q = _fold_in_proj(w_in[:, :d], ab)
            gte = _matmul(h, w_in[:, d:], F32, 1024, 512, name="gate_in_proj")
            h_t = h.reshape(n1, n2, d).transpose(1, 0, 2).reshape(s, d)
            pq_t = _matmul(h_t, jnp.concatenate([w_p, w_q], axis=1), BF16, 1024, 512, name="fourier_in_proj")
            t_nk = _dft_stage1(pq_t, f1, n1, n2)
            t_kn = t_nk.reshape(n2, 2, n1, d).transpose(2, 1, 0, 3)
            f = _dft_stage2(t_kn, l2, n1, n2)
            xs = _out_proj([f], gte, w_out_c[j].astype(BF16), xs, gate)
    return xs.reshape(batch, s, d)
```

```python
import functools
import math

import jax
import jax.numpy as jnp
from jax import lax
from jax.experimental import pallas as pl
from jax.experimental.pallas import tpu as pltpu

F32 = jnp.float32
BF16 = jnp.bfloat16

D_MODEL = 2048
DEPTH = 4
NORM_EPS = 1e-6

POOL_WINDOWS = (2, 4, 8, 16)
POOL_WIDTH = D_MODEL // 2
POOL_GROUP = POOL_WIDTH // len(POOL_WINDOWS)
POOL_HALO = 8

HEADS = 8
HEAD_DIM = 64
HEAD_V = 2 * HEAD_DIM
DIFF_WIDTH = HEADS * HEAD_V
AB_WIDTH = POOL_WIDTH + DIFF_WIDTH

FOURIER_GROUPS = 4
FOURIER_GROUP = D_MODEL // FOURIER_GROUPS

LANES = 128
SUBLANES = 8
BF16_ROWS = 16
POS_RADIX = 256

ATT_TQ = 512
ATT_TK = 512
ATT_UNROLL = 4
V_ROWS = HEAD_V + BF16_ROWS
NEG_BIG = -1e30

VMEM_LIMIT = 56 * 1024 * 1024


def _params(n_axes):
    return pltpu.CompilerParams(dimension_semantics=("arbitrary",) * n_axes,
                                vmem_limit_bytes=VMEM_LIMIT)


def _mod_kernel(c_ref, w_ref, b_ref, o_ref):
    c = c_ref[...]
    o_ref[...] = jnp.sum(jax.nn.silu(c) * w_ref[...], axis=0, keepdims=True) + b_ref[...]


def _ada_mod(c, ada_w, ada_b):
    depth, d, n = ada_w.shape
    tn = 512
    return pl.pallas_call(
        _mod_kernel,
        grid=(depth, n // tn),
        in_specs=[pl.BlockSpec((d, 1), lambda i, j: (0, 0)),
                  pl.BlockSpec((None, d, tn), lambda i, j: (i, 0, j)),
                  pl.BlockSpec((None, 1, tn), lambda i, j: (i, 0, j))],
        out_specs=pl.BlockSpec((None, 1, tn), lambda i, j: (i, 0, j)),
        out_shape=jax.ShapeDtypeStruct((depth, 1, n), F32),
        compiler_params=_params(2),
        name="ada_mod",
    )(c.reshape(d, 1), ada_w, ada_b.reshape(depth, 1, n))


def _modulate_kernel(x_ref, nw_ref, shift_ref, scale_ref, o_ref):
    xf = x_ref[...]
    r = lax.rsqrt(jnp.mean(xf * xf, axis=-1, keepdims=True) + NORM_EPS)
    y = xf * r * nw_ref[...]
    y = y * (1.0 + scale_ref[...]) + shift_ref[...]
    o_ref[...] = y.astype(o_ref.dtype)


def _modulate(x, nw, shift, scale):
    s, d = x.shape
    tm = 512
    vec = pl.BlockSpec((1, d), lambda i: (0, 0))
    return pl.pallas_call(
        _modulate_kernel,
        grid=(s // tm,),
        in_specs=[pl.BlockSpec((tm, d), lambda i: (i, 0)), vec, vec, vec],
        out_specs=pl.BlockSpec((tm, d), lambda i: (i, 0)),
        out_shape=jax.ShapeDtypeStruct((s, d), BF16),
        compiler_params=_params(1),
        name="modulate",
    )(x, nw, shift, scale)


def _mm_kernel(a_ref, w_ref, o_ref, *, precision):
    o_ref[...] = jnp.dot(a_ref[...], w_ref[...], preferred_element_type=F32,
                         precision=precision).astype(o_ref.dtype)


def _matmul(a, w, out_dtype, tm, tn, precision=None, name="matmul"):
    m, k = a.shape
    n = w.shape[1]
    tm, tn = min(tm, m), min(tn, n)
    return pl.pallas_call(
        functools.partial(_mm_kernel, precision=precision),
        grid=(m // tm, n // tn),
        in_specs=[pl.BlockSpec((tm, k), lambda i, j: (i, 0)),
                  pl.BlockSpec((k, tn), lambda i, j: (0, j))],
        out_specs=pl.BlockSpec((tm, tn), lambda i, j: (i, j)),
        out_shape=jax.ShapeDtypeStruct((m, n), out_dtype),
        compiler_params=_params(2),
        name=name,
    )(a, w)


def _qkv_kernel(h_ref, wq_ref, wk_ref, wv_ref, qnw_ref, knw_ref, qt_ref, ka_ref, vt_ref, *, tm):
    i = pl.program_id(0)
    h = h_ref[...]
    lane = lax.broadcasted_iota(jnp.int32, (tm, LANES), 1)
    first = lane < HEAD_DIM
    pos = lax.broadcasted_iota(jnp.int32, (tm, LANES), 0) + i * tm
    pos_lo = (pos % POS_RADIX).astype(F32)
    pos_hi = ((pos % ATT_TK) // POS_RADIX).astype(F32)
    pos_cols = jnp.where(lane == HEAD_DIM, pos_lo, jnp.where(lane == HEAD_DIM + 1, pos_hi, 0.0))
    ones_rows = (lax.broadcasted_iota(jnp.int32, (BF16_ROWS, ATT_TK), 0) == 0).astype(BF16)

    def halves_rms(z, w):
        sq = z * z
        ss1 = jnp.sum(jnp.where(first, sq, 0.0), axis=-1, keepdims=True)
        ss2 = jnp.sum(jnp.where(first, 0.0, sq), axis=-1, keepdims=True)
        r = jnp.where(first, lax.rsqrt(ss1 / HEAD_DIM + NORM_EPS), lax.rsqrt(ss2 / HEAD_DIM + NORM_EPS))
        return z * r * w

    zq = jnp.dot(h, wq_ref[...], preferred_element_type=F32)
    zk = jnp.dot(h, wk_ref[...], preferred_element_type=F32)
    zv = jnp.dot(h, wv_ref[...], preferred_element_type=F32)
    for hh in range(2):
        cols = slice(hh * LANES, (hh + 1) * LANES)
        qn = halves_rms(zq[:, cols], qnw_ref[...]) * (HEAD_DIM ** -0.5)
        kn = halves_rms(zk[:, cols], knw_ref[...])
        for m in range(2):
            qm = qn if m == 0 else pltpu.roll(qn, HEAD_DIM, axis=1)
            km = kn if m == 0 else pltpu.roll(kn, HEAD_DIM, axis=1)
            qt_ref[hh, m] = jnp.where(first, qm, 0.0).T.astype(BF16)
            ka = jnp.where(first, km, pos_cols).astype(BF16)
            for cc in range(tm // ATT_TK):
                ka_ref[hh, m, cc] = ka[cc * ATT_TK:(cc + 1) * ATT_TK]
        v = zv[:, cols]
        for cc in range(tm // ATT_TK):
            vt_ref[hh, cc, 0:HEAD_V, :] = v[cc * ATT_TK:(cc + 1) * ATT_TK].T.astype(BF16)
            vt_ref[hh, cc, HEAD_V:V_ROWS, :] = ones_rows


def _qkv_proj(h, wq, wk, wv, qnw, knw):
    s, d = h.shape
    tm = min(1024, s)
    nchunk = s // ATT_TK
    cpt = tm // ATT_TK
    tn = 2 * LANES
    wspec = pl.BlockSpec((d, tn), lambda i, j: (0, j))
    vec = pl.BlockSpec((1, LANES), lambda i, j: (0, 0))
    return pl.pallas_call(
        functools.partial(_qkv_kernel, tm=tm),
        grid=(s // tm, HEADS // 2),
        in_specs=[pl.BlockSpec((tm, d), lambda i, j: (i, 0)), wspec, wspec, wspec, vec, vec],
        out_specs=[pl.BlockSpec((2, 2, LANES, tm), lambda i, j: (j, 0, 0, i)),
                   pl.BlockSpec((2, 2, cpt, ATT_TK, LANES), lambda i, j: (j, 0, i, 0, 0)),
                   pl.BlockSpec((2, cpt, V_ROWS, ATT_TK), lambda i, j: (j, i, 0, 0))],
        out_shape=[jax.ShapeDtypeStruct((HEADS, 2, LANES, s), BF16),
                   jax.ShapeDtypeStruct((HEADS, 2, nchunk, ATT_TK, LANES), BF16),
                   jax.ShapeDtypeStruct((HEADS, nchunk, V_ROWS, ATT_TK), BF16)],
        compiler_params=_params(2),
        name="qkv_proj",
    )(h, wq, wk, wv, jnp.tile(qnw, (1, 2)), jnp.tile(knw, (1, 2)))


def _attn_kernel(slopes_ref, qt_ref, ka_ref, vt_ref, lq1_ref, lk1_ref, lq2_ref, lk2_ref, subw_ref,
                 o_ref, acc_ref, m_ref, qv_ref, st_a, st_b, tm_a, tm_b, *, lambda_init, nchunk, unroll):
    hd = pl.program_id(0)
    i = pl.program_id(1)
    slope = slopes_ref[hd]
    rowq = lax.broadcasted_iota(jnp.int32, (LANES, ATT_TQ), 0)
    q_off = lax.broadcasted_iota(jnp.int32, (1, ATT_TQ), 1).astype(F32)

    for m in range(2):
        qf = qt_ref[0, m].astype(F32)
        for kind, coef in ((0, -slope), (1, 0.0), (2, slope)):
            qv_ref[kind, m] = jnp.where(rowq == HEAD_DIM, coef,
                                        jnp.where(rowq == HEAD_DIM + 1, coef * POS_RADIX, qf)).astype(BF16)
    acc_ref[...] = jnp.zeros(acc_ref.shape, F32)
    m_ref[...] = jnp.full(m_ref.shape, NEG_BIG, F32)

    bufs = ((st_a, tm_a), (st_b, tm_b))

    def chunk_of(e):
        return jnp.where(e == 0, i, e - 1 + (e - 1 >= i).astype(jnp.int32))

    def qk(e, st_ref, tm_ref):
        c = chunk_of(e)
        kind = 1 + (i > c).astype(jnp.int32) - (i < c).astype(jnp.int32)
        for m in range(2):
            st = jnp.dot(ka_ref[0, m, c], qv_ref[kind, m], preferred_element_type=F32)
            st_ref[m] = st
            tm_ref[m] = jnp.max(st, axis=0, keepdims=True)

    def add_diag_bias(st_ref, tm_ref):
        key_off = lax.broadcasted_iota(jnp.int32, (ATT_TK, ATT_TQ), 0)
        qry_off = lax.broadcasted_iota(jnp.int32, (ATT_TK, ATT_TQ), 1)
        bias = slope * jnp.abs(key_off - qry_off).astype(F32)
        for m in range(2):
            st = st_ref[m] - bias
            st_ref[m] = st
            tm_ref[m] = jnp.max(st, axis=0, keepdims=True)

    def softmax_pv(e, st_ref, tm_ref):
        c = chunk_of(e)
        gap = i - c
        sgn = (gap > 0).astype(F32) - (gap < 0).astype(F32)
        cvec = -slope * ((jnp.abs(gap) * ATT_TK).astype(F32) + sgn * q_off)
        for m in range(2):
            m_old = m_ref[m]
            m_new = jnp.maximum(m_old, tm_ref[m] + cvec)
            p = jnp.exp(st_ref[m] - (m_new - cvec))
            pv = jnp.dot(vt_ref[0, c], p.astype(BF16), preferred_element_type=F32)
            acc_ref[m] = acc_ref[m] * jnp.exp(m_old - m_new) + pv
            m_ref[m] = m_new

    def run(e0, count, qk_last):
        for u in range(count):
            if u < count - 1 or qk_last:
                qk(e0 + u + 1, *bufs[(u + 1) % 2])
            softmax_pv(e0 + u, *bufs[u % 2])

    def loop_body(it, carry):
        run(it * unroll, unroll, True)
        return carry

    qk(0, st_a, tm_a)
    add_diag_bias(st_a, tm_a)
    n_loop = nchunk // unroll - 1
    lax.fori_loop(0, n_loop, loop_body, 0)
    run(n_loop * unroll, unroll, False)

    lam = (jnp.exp(jnp.sum(lq1_ref[...] * lk1_ref[...], axis=-1, keepdims=True))
           - jnp.exp(jnp.sum(lq2_ref[...] * lk2_ref[...], axis=-1, keepdims=True)) + lambda_init)
    a1 = acc_ref[0]
    a2 = acc_ref[1]
    o = a1[0:HEAD_V] / a1[HEAD_V:HEAD_V + 1] - lam * (a2[0:HEAD_V] / a2[HEAD_V:HEAD_V + 1])
    r = lax.rsqrt(jnp.mean(o * o, axis=0, keepdims=True) + NORM_EPS)
    y = (o * r * subw_ref[...]) * (1.0 - lambda_init)
    o_ref[...] = y.T


def _diff_attn(qt, ka, vt, lq1, lk1, lq2, lk2, subw, lambda_init):
    s = qt.shape[-1]
    nchunk = s // ATT_TK
    unroll = min(ATT_UNROLL, nchunk)
    assert unroll % 2 == 0 and nchunk % unroll == 0
    slopes = jnp.asarray([2.0 ** (-8.0 * (h + 1) / HEADS) for h in range(HEADS)], F32)
    vec = pl.BlockSpec((1, HEAD_DIM), lambda h, i: (0, 0))
    scores = pltpu.VMEM((2, ATT_TK, ATT_TQ), F32)
    colmax = pltpu.VMEM((2, 1, ATT_TQ), F32)
    return pl.pallas_call(
        functools.partial(_attn_kernel, lambda_init=lambda_init, nchunk=nchunk, unroll=unroll),
        grid=(HEADS, s // ATT_TQ),
        in_specs=[pl.BlockSpec(memory_space=pltpu.SMEM),
                  pl.BlockSpec((1, 2, LANES, ATT_TQ), lambda h, i: (h, 0, 0, i)),
                  pl.BlockSpec((1, 2, nchunk, ATT_TK, LANES), lambda h, i: (h, 0, 0, 0, 0)),
                  pl.BlockSpec((1, nchunk, V_ROWS, ATT_TK), lambda h, i: (h, 0, 0, 0)),
                  vec, vec, vec, vec,
                  pl.BlockSpec((HEAD_V, 1), lambda h, i: (0, 0))],
        out_specs=pl.BlockSpec((ATT_TQ, HEAD_V), lambda h, i: (i, h)),
        out_shape=jax.ShapeDtypeStruct((s, DIFF_WIDTH), F32),
        scratch_shapes=[pltpu.VMEM((2, V_ROWS, ATT_TQ), F32), pltpu.VMEM((2, 1, ATT_TQ), F32),
                        pltpu.VMEM((3, 2, LANES, ATT_TQ), BF16), scores, scores, colmax, colmax],
        compiler_params=_params(2),
        name="diff_attn",
    )(slopes, qt, ka, vt, lq1, lk1, lq2, lk2, subw.reshape(HEAD_V, 1))


def _pool_kernel(prev_ref, cur_ref, next_ref, w_ref, scale_ref, o_ref, ext_ref, *, tm, seq):
    i = pl.program_id(0)
    last = pl.num_programs(0) - 1
    zero_halo = jnp.zeros((POOL_HALO, POOL_WIDTH), F32)
    ext_ref[0:POOL_HALO] = jnp.where(i > 0, prev_ref[...], zero_halo)
    ext_ref[POOL_HALO:POOL_HALO + tm] = cur_ref[...]
    ext_ref[POOL_HALO + tm:POOL_HALO + tm + POOL_HALO] = jnp.where(i < last, next_ref[...], zero_halo)
    t = lax.broadcasted_iota(jnp.int32, (tm, 1), 0) + i * tm
    for g, w in enumerate(POOL_WINDOWS):
        cols = pl.ds(g * POOL_GROUP, POOL_GROUP)
        win = ext_ref[pl.ds(POOL_HALO - w // 2, tm), cols]
        for d in range(-(w // 2) + 1, w - w // 2):
            win = win + ext_ref[pl.ds(POOL_HALO + d, tm), cols]
        lo = jnp.maximum(t - w // 2, 0)
        hi = jnp.minimum(t + (w - w // 2) - 1, seq - 1)
        cnt = (hi - lo + 1).astype(F32)
        pooled = win / cnt - cur_ref[:, cols]
        y = jnp.dot(pooled.astype(BF16), w_ref[g], preferred_element_type=F32)
        o_ref[:, cols] = y * scale_ref[:, cols]


def _pool_mix(u, w_pool, pool_scale):
    s, width = u.shape
    tm = 512
    hb = tm // POOL_HALO
    nhalo = s // POOL_HALO
    return pl.pallas_call(
        functools.partial(_pool_kernel, tm=tm, seq=s),
        grid=(s // tm,),
        in_specs=[pl.BlockSpec((POOL_HALO, width), lambda i: (jnp.maximum(i * hb - 1, 0), 0)),
                  pl.BlockSpec((tm, width), lambda i: (i, 0)),
                  pl.BlockSpec((POOL_HALO, width), lambda i: (jnp.minimum((i + 1) * hb, nhalo - 1), 0)),
                  pl.BlockSpec(w_pool.shape, lambda i: (0, 0, 0)),
                  pl.BlockSpec((1, width), lambda i: (0, 0))],
        out_specs=pl.BlockSpec((tm, width), lambda i: (i, 0)),
        out_shape=jax.ShapeDtypeStruct((s, width), F32),
        scratch_shapes=[pltpu.VMEM((tm + 2 * POOL_HALO, width), F32)],
        compiler_params=_params(1),
        name="pool_mix",
    )(u, u, u, w_pool, pool_scale)


def _out_kernel(*refs, widths):
    n = len(widths)
    y_refs = refs[:n]
    g_ref, w_ref, x_ref, gate_ref, o_ref, yb_ref = refs[n:]

    @pl.when(pl.program_id(1) == 0)
    def _():
        off = 0
        for y_ref, wd in zip(y_refs, widths):
            yb_ref[:, off:off + wd] = (y_ref[...] * jax.nn.silu(g_ref[:, off:off + wd])).astype(BF16)
            off += wd

    acc = jnp.dot(yb_ref[...], w_ref[...], preferred_element_type=F32)
    o_ref[...] = x_ref[...] + gate_ref[...] * acc


def _out_proj(ys, g, w, x, gate):
    s, d = x.shape
    k = w.shape[0]
    tm, tn = 512, 512
    widths = tuple(y.shape[1] for y in ys)
    return pl.pallas_call(
        functools.partial(_out_kernel, widths=widths),
        grid=(s // tm, d // tn),
        in_specs=[pl.BlockSpec((tm, wd), lambda i, j: (i, 0)) for wd in widths]
        + [pl.BlockSpec((tm, k), lambda i, j: (i, 0)),
           pl.BlockSpec((k, tn), lambda i, j: (0, j)),
           pl.BlockSpec((tm, tn), lambda i, j: (i, j)),
           pl.BlockSpec((1, tn), lambda i, j: (0, j))],
        out_specs=pl.BlockSpec((tm, tn), lambda i, j: (i, j)),
        out_shape=jax.ShapeDtypeStruct((s, d), F32),
        scratch_shapes=[pltpu.VMEM((tm, k), BF16)],
        compiler_params=_params(2),
        name="out_proj",
    )(*ys, g, w, x, gate)


def _dft_split(s):
    n1 = 1 << (int(math.log2(s)) // 2)
    n2 = s // n1
    assert n1 * n2 == s and n1 % SUBLANES == 0 and n2 % SUBLANES == 0
    return n1, n2


def _angles(num, den):
    ang = (2.0 * math.pi / den) * (num % den).astype(F32)
    return jnp.cos(ang), jnp.sin(ang)


def _fold_cs_kernel(cs_ref, wf_ref, o_ref):
    wf = wf_ref[...]
    o_ref[:, 0:FOURIER_GROUP] = jnp.dot(cs_ref[0], wf, preferred_element_type=F32,
                                        precision=lax.Precision.HIGHEST)
    o_ref[:, FOURIER_GROUP:] = jnp.dot(cs_ref[1], wf, preferred_element_type=F32,
                                       precision=lax.Precision.HIGHEST)


def _fold_channel_dft(w_fourier_j, seq):
    fg = FOURIER_GROUP
    idx = jnp.arange(fg, dtype=jnp.int32)
    cc, sc = _angles(idx[:, None] * idx[None, :], fg)
    cs = jnp.stack([cc, sc]) * (1.0 / math.sqrt(seq * fg))
    return pl.pallas_call(
        _fold_cs_kernel,
        grid=(FOURIER_GROUPS,),
        in_specs=[pl.BlockSpec((2, fg, fg), lambda g: (0, 0, 0)),
                  pl.BlockSpec((None, fg, fg), lambda g: (g, 0, 0))],
        out_specs=pl.BlockSpec((None, fg, 2 * fg), lambda g: (g, 0, 0)),
        out_shape=jax.ShapeDtypeStruct((FOURIER_GROUPS, fg, 2 * fg), F32),
        compiler_params=_params(1),
        name="fold_channel_dft",
    )(cs, w_fourier_j)


def _fold_win_kernel(w_ref, ab_ref, p_ref, q_ref):
    r = jnp.dot(w_ref[...], ab_ref[...].astype(BF16), preferred_element_type=F32)
    p_ref[...] = r[:, 0:FOURIER_GROUP].astype(BF16)
    q_ref[...] = r[:, FOURIER_GROUP:].astype(BF16)


def _fold_in_proj(w_in_u, ab):
    d = w_in_u.shape[0]
    fg = FOURIER_GROUP
    ospec = pl.BlockSpec((d, fg), lambda g: (0, g))
    return pl.pallas_call(
        _fold_win_kernel,
        grid=(FOURIER_GROUPS,),
        in_specs=[pl.BlockSpec((d, fg), lambda g: (0, g)),
                  pl.BlockSpec((None, fg, 2 * fg), lambda g: (g, 0, 0))],
        out_specs=[ospec, ospec],
        out_shape=[jax.ShapeDtypeStruct((d, D_MODEL), BF16)] * 2,
        compiler_params=_params(1),
        name="fold_in_proj",
    )(w_in_u, ab)


def _dft1_kernel(f_ref, p_ref, q_ref, t_ref, *, nb, n1):
    f = f_ref[...]
    for jj in range(nb):
        rows = slice(jj * n1, (jj + 1) * n1)
        rhs = jnp.concatenate([p_ref[rows, :], q_ref[rows, :]], axis=0)
        t_ref[jj] = jnp.dot(f, rhs, preferred_element_type=F32).astype(BF16)


def _dft_stage1(pq_t, f1, n1, n2):
    s, c2 = pq_t.shape
    c = c2 // 2
    nb = SUBLANES
    cb = min(1024, c)
    ncb = c // cb
    return pl.pallas_call(
        functools.partial(_dft1_kernel, nb=nb, n1=n1),
        grid=(n2 // nb, ncb),
        in_specs=[pl.BlockSpec((2 * n1, 2 * n1), lambda b, j: (0, 0)),
                  pl.BlockSpec((nb * n1, cb), lambda b, j: (b, j)),
                  pl.BlockSpec((nb * n1, cb), lambda b, j: (b, j + ncb))],
        out_specs=pl.BlockSpec((nb, 2 * n1, cb), lambda b, j: (b, 0, j)),
        out_shape=jax.ShapeDtypeStruct((n2, 2 * n1, c), BF16),
        compiler_params=_params(2),
        name="dft_stage1",
    )(f1, pq_t, pq_t)


def _dft2_kernel(l_ref, t_ref, o_ref, *, n2, cb):
    rhs = t_ref[...].reshape(SUBLANES * 2 * n2, cb)
    y = jnp.dot(l_ref[0], rhs, preferred_element_type=F32)
    o_ref[...] = y.reshape(n2, SUBLANES, cb)


def _dft_stage2(t_kn, l2, n1, n2):
    c = t_kn.shape[-1]
    cb = min(1024, c)
    out = pl.pallas_call(
        functools.partial(_dft2_kernel, n2=n2, cb=cb),
        grid=(n1 // SUBLANES, c // cb),
        in_specs=[pl.BlockSpec((1, SUBLANES * n2, SUBLANES * 2 * n2), lambda g, j: (g, 0, 0)),
                  pl.BlockSpec((SUBLANES, 2, n2, cb), lambda g, j: (g, 0, 0, j))],
        out_specs=pl.BlockSpec((n2, SUBLANES, cb), lambda g, j: (0, g, j)),
        out_shape=jax.ShapeDtypeStruct((n2, n1, c), F32),
        compiler_params=_params(2),
        name="dft_stage2",
    )(l2, t_kn)
    return out.reshape(n1 * n2, c)


def _dft_matrices(s):
    n1, n2 = _dft_split(s)
    a = jnp.arange(n1, dtype=jnp.int32)
    c1, s1 = _angles(a[:, None] * a[None, :], n1)
    f1 = jnp.concatenate([jnp.concatenate([c1, -s1], axis=1),
                          jnp.concatenate([-s1, -c1], axis=1)], axis=0).astype(BF16)
    groups = n1 // SUBLANES
    g = jnp.arange(groups, dtype=jnp.int32)[:, None, None, None]
    k2 = jnp.arange(n2, dtype=jnp.int32)[None, :, None, None]
    j = jnp.arange(SUBLANES, dtype=jnp.int32)[None, None, :, None]
    nn = jnp.arange(n2, dtype=jnp.int32)[None, None, None, :]
    c2, s2 = _angles((SUBLANES * g + j + n1 * k2) * nn, s)
    cs = jnp.stack([c2, s2], axis=3).astype(BF16)
    eye = jnp.eye(SUBLANES, dtype=BF16)
    l2 = cs[:, :, :, None] * eye[None, None, :, :, None, None]
    return n1, n2, f1, l2.reshape(groups, n2 * SUBLANES, SUBLANES * 2 * n2)


def kernel(x, c, norm_w, ada_w, ada_b, w_in_ab, w_pool, pool_scale, q_norm_w, k_norm_w, lambda_q1, lambda_k1,
           lambda_q2, lambda_k2, subln_w, w_out_ab, w_in_c, w_fourier, w_out_c):
    batch, s, d = x.shape
    assert batch == 1 and d == D_MODEL and s % ATT_TK == 0
    xs = x.reshape(s, d)
    mod = _ada_mod(c, ada_w, ada_b)
    n1, n2, f1, l2 = _dft_matrices(s)
    row = lambda v: v.reshape(1, -1)

    for i in range(DEPTH):
        shift, scale, gate = mod[i, :, 0:d], mod[i, :, d:2 * d], mod[i, :, 2 * d:3 * d]
        h = _modulate(xs, row(norm_w[i]), shift, scale)
        j = i // 2
        if i % 2 == 0:
            w_in = w_in_ab[j].astype(BF16)
            o1, o2, o3, o4 = POOL_WIDTH, POOL_WIDTH + DIFF_WIDTH, POOL_WIDTH + 2 * DIFF_WIDTH, POOL_WIDTH + 3 * DIFF_WIDTH
            lambda_init = 0.8 - 0.6 * math.exp(-0.3 * i)
            u_pool = _matmul(h, w_in[:, :o1], F32, 1024, 512, name="pool_in_proj")
            gte = _matmul(h, w_in[:, o4:], F32, 1024, 512, name="gate_in_proj")
            qt, ka, vt = _qkv_proj(h, w_in[:, o1:o2], w_in[:, o2:o3], w_in[:, o3:o4],
                                   row(q_norm_w[j]), row(k_norm_w[j]))
            y_a = _pool_mix(u_pool, w_pool[j].astype(BF16), row(pool_scale[j]))
            y_b = _diff_attn(qt, ka, vt, row(lambda_q1[j]), row(lambda_k1[j]), row(lambda_q2[j]),
                             row(lambda_k2[j]), subln_w[j], lambda_init)
            xs = _out_proj([y_a, y_b], gte, w_out_ab[j].astype(BF16), xs, gate)
        else:
            w_in = w_in_c[j].astype(BF16)
            ab = _fold_channel_dft(w_fourier[j], s)
            w_p, w_q = _fold_in_proj(w_in[:, :d], ab)
            gte = _matmul(h, w_in[:, d:], F32, 1024, 512, name="gate_in_proj")
            h_t = h.reshape(n1, n2, d).transpose(1, 0, 2).reshape(s, d)
            pq_t = _matmul(h_t, jnp.concatenate([w_p, w_q], axis=1), BF16, 1024, 512, name="fourier_in_proj")
            t_nk = _dft_stage1(pq_t, f1, n1, n2)
            t_kn = t_nk.reshape(n2, 2, n1, d).transpose(2, 1, 0, 3)
            f = _dft_stage2(t_kn, l2, n1, n2)
            xs = _out_proj([f], gte, w_out_c[j].astype(BF16), xs, gate)
    return xs.reshape(batch, s, d)
```

```python
import functools
import math

import jax
import jax.numpy as jnp
from jax import lax
from jax.experimental import pallas as pl
from jax.experimental.pallas import tpu as pltpu

F32 = jnp.float32
BF16 = jnp.bfloat16

D_MODEL = 2048
DEPTH = 4
NORM_EPS = 1e-6

POOL_WINDOWS = (2, 4, 8, 16)
POOL_WIDTH = D_MODEL // 2
POOL_GROUP = POOL_WIDTH // len(POOL_WINDOWS)
POOL_HALO = 8

HEADS = 8
HEAD_DIM = 64
HEAD_V = 2 * HEAD_DIM
DIFF_WIDTH = HEADS * HEAD_V
AB_WIDTH = POOL_WIDTH + DIFF_WIDTH

FOURIER_GROUPS = 4
FOURIER_GROUP = D_MODEL // FOURIER_GROUPS

LANES = 128
SUBLANES = 8
BF16_ROWS = 16
POS_RADIX = 256

ATT_TQ = 512
ATT_TK = 512
ATT_UNROLL = 4
V_ROWS = HEAD_V + BF16_ROWS
NEG_BIG = -1e30

VMEM_LIMIT = 56 * 1024 * 1024


def _params(n_axes):
    return pltpu.CompilerParams(dimension_semantics=("arbitrary",) * n_axes,
                                vmem_limit_bytes=VMEM_LIMIT)


def _mod_kernel(c_ref, w_ref, b_ref, o_ref):
    c = c_ref[...]
    o_ref[...] = jnp.sum(jax.nn.silu(c) * w_ref[...], axis=0, keepdims=True) + b_ref[...]


def _ada_mod(c, ada_w, ada_b):
    depth, d, n = ada_w.shape
    tn = 512
    return pl.pallas_call(
        _mod_kernel,
        grid=(depth, n // tn),
        in_specs=[pl.BlockSpec((d, 1), lambda i, j: (0, 0)),
                  pl.BlockSpec((None, d, tn), lambda i, j: (i, 0, j)),
                  pl.BlockSpec((None, 1, tn), lambda i, j: (i, 0, j))],
        out_specs=pl.BlockSpec((None, 1, tn), lambda i, j: (i, 0, j)),
        out_shape=jax.ShapeDtypeStruct((depth, 1, n), F32),
        compiler_params=_params(2),
        name="ada_mod",
    )(c.reshape(d, 1), ada_w, ada_b.reshape(depth, 1, n))


def _modulate_kernel(x_ref, nw_ref, shift_ref, scale_ref, o_ref):
    xf = x_ref[...]
    r = lax.rsqrt(jnp.mean(xf * xf, axis=-1, keepdims=True) + NORM_EPS)
    y = xf * r * nw_ref[...]
    y = y * (1.0 + scale_ref[...]) + shift_ref[...]
    o_ref[...] = y.astype(o_ref.dtype)


def _modulate(x, nw, shift, scale):
    s, d = x.shape
    tm = 512
    vec = pl.BlockSpec((1, d), lambda i: (0, 0))
    return pl.pallas_call(
        _modulate_kernel,
        grid=(s // tm,),
        in_specs=[pl.BlockSpec((tm, d), lambda i: (i, 0)), vec, vec, vec],
        out_specs=pl.BlockSpec((tm, d), lambda i: (i, 0)),
        out_shape=jax.ShapeDtypeStruct((s, d), BF16),
        compiler_params=_params(1),
        name="modulate",
    )(x, nw, shift, scale)


def _mm_kernel(a_ref, w_ref, o_ref, *, precision):
    o_ref[...] = jnp.dot(a_ref[...], w_ref[...], preferred_element_type=F32,
                         precision=precision).astype(o_ref.dtype)


def _matmul(a, w, out_dtype, tm, tn, precision=None, name="matmul"):
    m, k = a.shape
    n = w.shape[1]
    tm, tn = min(tm, m), min(tn, n)
    return pl.pallas_call(
        functools.partial(_mm_kernel, precision=precision),
        grid=(m // tm, n // tn),
        in_specs=[pl.BlockSpec((tm, k), lambda i, j: (i, 0)),
                  pl.BlockSpec((k, tn), lambda i, j: (0, j))],
        out_specs=pl.BlockSpec((tm, tn), lambda i, j: (i, j)),
        out_shape=jax.ShapeDtypeStruct((m, n), out_dtype),
        compiler_params=_params(2),
        name=name,
    )(a, w)


def _qkv_kernel(h_ref, wq_ref, wk_ref, wv_ref, qnw_ref, knw_ref, qt_ref, ka_ref, vt_ref, *, tm):
    i = pl.program_id(0)
    h = h_ref[...]
    lane = lax.broadcasted_iota(jnp.int32, (tm, LANES), 1)
    first = lane < HEAD_DIM
    pos = lax.broadcasted_iota(jnp.int32, (tm, LANES), 0) + i * tm
    pos_lo = (pos % POS_RADIX).astype(F32)
    pos_hi = ((pos % ATT_TK) // POS_RADIX).astype(F32)
    pos_cols = jnp.where(lane == HEAD_DIM, pos_lo, jnp.where(lane == HEAD_DIM + 1, pos_hi, 0.0))
    ones_rows = (lax.broadcasted_iota(jnp.int32, (BF16_ROWS, ATT_TK), 0) == 0).astype(BF16)

    def halves_rms(z, w):
        sq = z * z
        ss1 = jnp.sum(jnp.where(first, sq, 0.0), axis=-1, keepdims=True)
        ss2 = jnp.sum(jnp.where(first, 0.0, sq), axis=-1, keepdims=True)
        r = jnp.where(first, lax.rsqrt(ss1 / HEAD_DIM + NORM_EPS), lax.rsqrt(ss2 / HEAD_DIM + NORM_EPS))
        return z * r * w

    zq = jnp.dot(h, wq_ref[...], preferred_element_type=F32)
    zk = jnp.dot(h, wk_ref[...], preferred_element_type=F32)
    zv = jnp.dot(h, wv_ref[...], preferred_element_type=F32)
    for hh in range(2):
        cols = slice(hh * LANES, (hh + 1) * LANES)
        qn = halves_rms(zq[:, cols], qnw_ref[...]) * (HEAD_DIM ** -0.5)
        kn = halves_rms(zk[:, cols], knw_ref[...])
        for m in range(2):
            qm = qn if m == 0 else pltpu.roll(qn, HEAD_DIM, axis=1)
            km = kn if m == 0 else pltpu.roll(kn, HEAD_DIM, axis=1)
            qt_ref[hh, m] = jnp.where(first, qm, 0.0).T.astype(BF16)
            ka = jnp.where(first, km, pos_cols).astype(BF16)
            for cc in range(tm // ATT_TK):
                ka_ref[hh, m, cc] = ka[cc * ATT_TK:(cc + 1) * ATT_TK]
        v = zv[:, cols]
        for cc in range(tm // ATT_TK):
            vt_ref[hh, cc, 0:HEAD_V, :] = v[cc * ATT_TK:(cc + 1) * ATT_TK].T.astype(BF16)
            vt_ref[hh, cc, HEAD_V:V_ROWS, :] = ones_rows


def _qkv_proj(h, wq, wk, wv, qnw, knw):
    s, d = h.shape
    tm = min(1024, s)
    nchunk = s // ATT_TK
    cpt = tm // ATT_TK
    tn = 2 * LANES
    wspec = pl.BlockSpec((d, tn), lambda i, j: (0, j))
    vec = pl.BlockSpec((1, LANES), lambda i, j: (0, 0))
    return pl.pallas_call(
        functools.partial(_qkv_kernel, tm=tm),
        grid=(s // tm, HEADS // 2),
        in_specs=[pl.BlockSpec((tm, d), lambda i, j: (i, 0)), wspec, wspec, wspec, vec, vec],
        out_specs=[pl.BlockSpec((2, 2, LANES, tm), lambda i, j: (j, 0, 0, i)),
                   pl.BlockSpec((2, 2, cpt, ATT_TK, LANES), lambda i, j: (j, 0, i, 0, 0)),
                   pl.BlockSpec((2, cpt, V_ROWS, ATT_TK), lambda i, j: (j, i, 0, 0))],
        out_shape=[jax.ShapeDtypeStruct((HEADS, 2, LANES, s), BF16),
                   jax.ShapeDtypeStruct((HEADS, 2, nchunk, ATT_TK, LANES), BF16),
                   jax.ShapeDtypeStruct((HEADS, nchunk, V_ROWS, ATT_TK), BF16)],
        compiler_params=_params(2),
        name="qkv_proj",
    )(h, wq, wk, wv, jnp.tile(qnw, (1, 2)), jnp.tile(knw, (1, 2)))


def _attn_kernel(slopes_ref, qt_ref, ka_ref, vt_ref, lq1_ref, lk1_ref, lq2_ref, lk2_ref, subw_ref, g_ref,
                 o_ref, acc_ref, m_ref, qv_ref, st_a, st_b, tm_a, tm_b, *, lambda_init, nchunk, unroll):
    hd = pl.program_id(0)
    i = pl.program_id(1)
    slope = slopes_ref[hd]
    rowq = lax.broadcasted_iota(jnp.int32, (LANES, ATT_TQ), 0)
    q_off = lax.broadcasted_iota(jnp.int32, (1, ATT_TQ), 1).astype(F32)

    for m in range(2):
        qf = qt_ref[0, m].astype(F32)
        for kind, coef in ((0, -slope), (1, 0.0), (2, slope)):
            qv_ref[kind, m] = jnp.where(rowq == HEAD_DIM, coef,
                                        jnp.where(rowq == HEAD_DIM + 1, coef * POS_RADIX, qf)).astype(BF16)
    acc_ref[...] = jnp.zeros(acc_ref.shape, F32)
    m_ref[...] = jnp.full(m_ref.shape, NEG_BIG, F32)

    bufs = ((st_a, tm_a), (st_b, tm_b))

    def chunk_of(e):
        return jnp.where(e == 0, i, e - 1 + (e - 1 >= i).astype(jnp.int32))

    def qk(e, st_ref, tm_ref):
        c = chunk_of(e)
        kind = 1 + (i > c).astype(jnp.int32) - (i < c).astype(jnp.int32)
        for m in range(2):
            st = jnp.dot(ka_ref[0, m, c], qv_ref[kind, m], preferred_element_type=F32)
            st_ref[m] = st
            tm_ref[m] = jnp.max(st, axis=0, keepdims=True)

    def add_diag_bias(st_ref, tm_ref):
        key_off = lax.broadcasted_iota(jnp.int32, (ATT_TK, ATT_TQ), 0)
        qry_off = lax.broadcasted_iota(jnp.int32, (ATT_TK, ATT_TQ), 1)
        bias = slope * jnp.abs(key_off - qry_off).astype(F32)
        for m in range(2):
            st = st_ref[m] - bias
            st_ref[m] = st
            tm_ref[m] = jnp.max(st, axis=0, keepdims=True)

    def softmax_pv(e, st_ref, tm_ref):
        c = chunk_of(e)
        gap = i - c
        sgn = (gap > 0).astype(F32) - (gap < 0).astype(F32)
        cvec = -slope * ((jnp.abs(gap) * ATT_TK).astype(F32) + sgn * q_off)
        for m in range(2):
            m_old = m_ref[m]
            m_new = jnp.maximum(m_old, tm_ref[m] + cvec)
            p = jnp.exp(st_ref[m] - (m_new - cvec))
            pv = jnp.dot(vt_ref[0, c], p.astype(BF16), preferred_element_type=F32)
            acc_ref[m] = acc_ref[m] * jnp.exp(m_old - m_new) + pv
            m_ref[m] = m_new

    def run(e0, count, qk_last):
        for u in range(count):
            if u < count - 1 or qk_last:
                qk(e0 + u + 1, *bufs[(u + 1) % 2])
            softmax_pv(e0 + u, *bufs[u % 2])

    def loop_body(it, carry):
        run(it * unroll, unroll, True)
        return carry

    qk(0, st_a, tm_a)
    add_diag_bias(st_a, tm_a)
    n_loop = nchunk // unroll - 1
    lax.fori_loop(0, n_loop, loop_body, 0)
    run(n_loop * unroll, unroll, False)

    lam = (jnp.exp(jnp.sum(lq1_ref[...] * lk1_ref[...], axis=-1, keepdims=True))
           - jnp.exp(jnp.sum(lq2_ref[...] * lk2_ref[...], axis=-1, keepdims=True)) + lambda_init)
    a1 = acc_ref[0]
    a2 = acc_ref[1]
    o = a1[0:HEAD_V] / a1[HEAD_V:HEAD_V + 1] - lam * (a2[0:HEAD_V] / a2[HEAD_V:HEAD_V + 1])
    r = lax.rsqrt(jnp.mean(o * o, axis=0, keepdims=True) + NORM_EPS)
    y = (o * r * subw_ref[...]) * (1.0 - lambda_init)
    o_ref[...] = (y.T * jax.nn.silu(g_ref[...])).astype(o_ref.dtype)


def _diff_attn(qt, ka, vt, lq1, lk1, lq2, lk2, subw, g, lambda_init):
    s = qt.shape[-1]
    g_col0 = (g.shape[1] - DIFF_WIDTH) // HEAD_V
    nchunk = s // ATT_TK
    unroll = min(ATT_UNROLL, nchunk)
    assert unroll % 2 == 0 and nchunk % unroll == 0
    slopes = jnp.asarray([2.0 ** (-8.0 * (h + 1) / HEADS) for h in range(HEADS)], F32)
    vec = pl.BlockSpec((1, HEAD_DIM), lambda h, i: (0, 0))
    scores = pltpu.VMEM((2, ATT_TK, ATT_TQ), F32)
    colmax = pltpu.VMEM((2, 1, ATT_TQ), F32)
    return pl.pallas_call(
        functools.partial(_attn_kernel, lambda_init=lambda_init, nchunk=nchunk, unroll=unroll),
        grid=(HEADS, s // ATT_TQ),
        in_specs=[pl.BlockSpec(memory_space=pltpu.SMEM),
                  pl.BlockSpec((1, 2, LANES, ATT_TQ), lambda h, i: (h, 0, 0, i)),
                  pl.BlockSpec((1, 2, nchunk, ATT_TK, LANES), lambda h, i: (h, 0, 0, 0, 0)),
                  pl.BlockSpec((1, nchunk, V_ROWS, ATT_TK), lambda h, i: (h, 0, 0, 0)),
                  vec, vec, vec, vec,
                  pl.BlockSpec((HEAD_V, 1), lambda h, i: (0, 0)),
                  pl.BlockSpec((ATT_TQ, HEAD_V), lambda h, i: (i, g_col0 + h))],
        out_specs=pl.BlockSpec((ATT_TQ, HEAD_V), lambda h, i: (i, h)),
        out_shape=jax.ShapeDtypeStruct((s, DIFF_WIDTH), BF16),
        scratch_shapes=[pltpu.VMEM((2, V_ROWS, ATT_TQ), F32), pltpu.VMEM((2, 1, ATT_TQ), F32),
                        pltpu.VMEM((3, 2, LANES, ATT_TQ), BF16), scores, scores, colmax, colmax],
        compiler_params=_params(2),
        name="diff_attn",
    )(slopes, qt, ka, vt, lq1, lk1, lq2, lk2, subw.reshape(HEAD_V, 1), g)


def _pool_kernel(prev_ref, cur_ref, next_ref, w_ref, scale_ref, g_ref, o_ref, ext_ref, *, tm, seq):
    i = pl.program_id(0)
    last = pl.num_programs(0) - 1
    zero_halo = jnp.zeros((POOL_HALO, POOL_WIDTH), F32)
    ext_ref[0:POOL_HALO] = jnp.where(i > 0, prev_ref[...], zero_halo)
    ext_ref[POOL_HALO:POOL_HALO + tm] = cur_ref[...]
    ext_ref[POOL_HALO + tm:POOL_HALO + tm + POOL_HALO] = jnp.where(i < last, next_ref[...], zero_halo)
    t = lax.broadcasted_iota(jnp.int32, (tm, 1), 0) + i * tm
    for g, w in enumerate(POOL_WINDOWS):
        cols = pl.ds(g * POOL_GROUP, POOL_GROUP)
        win = ext_ref[pl.ds(POOL_HALO - w // 2, tm), cols]
        for d in range(-(w // 2) + 1, w - w // 2):
            win = win + ext_ref[pl.ds(POOL_HALO + d, tm), cols]
        lo = jnp.maximum(t - w // 2, 0)
        hi = jnp.minimum(t + (w - w // 2) - 1, seq - 1)
        cnt = (hi - lo + 1).astype(F32)
        pooled = win / cnt - cur_ref[:, cols]
        y = jnp.dot(pooled.astype(BF16), w_ref[g], preferred_element_type=F32)
        o_ref[:, cols] = (y * scale_ref[:, cols] * jax.nn.silu(g_ref[:, cols])).astype(o_ref.dtype)


def _pool_mix(u, w_pool, pool_scale, g):
    s, width = u.shape
    tm = 512
    hb = tm // POOL_HALO
    nhalo = s // POOL_HALO
    return pl.pallas_call(
        functools.partial(_pool_kernel, tm=tm, seq=s),
        grid=(s // tm,),
        in_specs=[pl.BlockSpec((POOL_HALO, width), lambda i: (jnp.maximum(i * hb - 1, 0), 0)),
                  pl.BlockSpec((tm, width), lambda i: (i, 0)),
                  pl.BlockSpec((POOL_HALO, width), lambda i: (jnp.minimum((i + 1) * hb, nhalo - 1), 0)),
                  pl.BlockSpec(w_pool.shape, lambda i: (0, 0, 0)),
                  pl.BlockSpec((1, width), lambda i: (0, 0)),
                  pl.BlockSpec((tm, width), lambda i: (i, 0))],
        out_specs=pl.BlockSpec((tm, width), lambda i: (i, 0)),
        out_shape=jax.ShapeDtypeStruct((s, width), BF16),
        scratch_shapes=[pltpu.VMEM((tm + 2 * POOL_HALO, width), F32)],
        compiler_params=_params(1),
        name="pool_mix",
    )(u, u, u, w_pool, pool_scale, g)


def _out_kernel(*refs, widths):
    n = len(widths)
    y_refs = refs[:n]
    w_ref, x_ref, gate_ref, o_ref = refs[n:n + 4]
    if len(refs) > n + 4:
        yb_ref = refs[n + 4]

        @pl.when(pl.program_id(1) == 0)
        def _():
            yb_ref[...] = y_refs[0][...].astype(BF16)

        y_refs = [yb_ref]
    acc = None
    off = 0
    for y_ref, wd in zip(y_refs, widths):
        part = jnp.dot(y_ref[...], w_ref[off:off + wd, :], preferred_element_type=F32)
        acc = part if acc is None else acc + part
        off += wd
    o_ref[...] = x_ref[...] + gate_ref[...] * acc


def _out_proj(ys, w, x, gate):
    s, d = x.shape
    k = w.shape[0]
    tm, tn = min(1024, s), 512
    widths = tuple(y.shape[1] for y in ys)
    needs_cast = ys[0].dtype != BF16
    assert not needs_cast or len(ys) == 1
    return pl.pallas_call(
        functools.partial(_out_kernel, widths=widths),
        grid=(s // tm, d // tn),
        in_specs=[pl.BlockSpec((tm, wd), lambda i, j: (i, 0)) for wd in widths]
        + [pl.BlockSpec((k, tn), lambda i, j: (0, j)),
           pl.BlockSpec((tm, tn), lambda i, j: (i, j)),
           pl.BlockSpec((1, tn), lambda i, j: (0, j))],
        out_specs=pl.BlockSpec((tm, tn), lambda i, j: (i, j)),
        out_shape=jax.ShapeDtypeStruct((s, d), F32),
        scratch_shapes=[pltpu.VMEM((tm, k), BF16)] if needs_cast else [],
        compiler_params=_params(2),
        name="out_proj",
    )(*ys, w, x, gate)


def _dft_split(s):
    n1 = 1 << (int(math.log2(s)) // 2)
    n2 = s // n1
    assert n1 * n2 == s and n1 % SUBLANES == 0 and n2 % SUBLANES == 0
    return n1, n2


def _angles(num, den):
    ang = (2.0 * math.pi / den) * (num % den).astype(F32)
    return jnp.cos(ang), jnp.sin(ang)


def _fold_cs_kernel(cs_ref, wf_ref, o_ref):
    wf = wf_ref[...]
    o_ref[:, 0:FOURIER_GROUP] = jnp.dot(cs_ref[0], wf, preferred_element_type=F32,
                                        precision=lax.Precision.HIGHEST)
    o_ref[:, FOURIER_GROUP:] = jnp.dot(cs_ref[1], wf, preferred_element_type=F32,
                                       precision=lax.Precision.HIGHEST)


def _fold_channel_dft(w_fourier_j, seq):
    fg = FOURIER_GROUP
    idx = jnp.arange(fg, dtype=jnp.int32)
    cc, sc = _angles(idx[:, None] * idx[None, :], fg)
    cs = jnp.stack([cc, sc]) * (1.0 / math.sqrt(seq * fg))
    return pl.pallas_call(
        _fold_cs_kernel,
        grid=(FOURIER_GROUPS,),
        in_specs=[pl.BlockSpec((2, fg, fg), lambda g: (0, 0, 0)),
                  pl.BlockSpec((None, fg, fg), lambda g: (g, 0, 0))],
        out_specs=pl.BlockSpec((None, fg, 2 * fg), lambda g: (g, 0, 0)),
        out_shape=jax.ShapeDtypeStruct((FOURIER_GROUPS, fg, 2 * fg), F32),
        compiler_params=_params(1),
        name="fold_channel_dft",
    )(cs, w_fourier_j)


def _fold_win_kernel(w_ref, ab_ref, p_ref, q_ref):
    r = jnp.dot(w_ref[...], ab_ref[...].astype(BF16), preferred_element_type=F32)
    p_ref[...] = r[:, 0:FOURIER_GROUP].astype(BF16)
    q_ref[...] = r[:, FOURIER_GROUP:].astype(BF16)


def _fold_in_proj(w_in_u, ab):
    d = w_in_u.shape[0]
    fg = FOURIER_GROUP
    ospec = pl.BlockSpec((d, fg), lambda g: (0, g))
    return pl.pallas_call(
        _fold_win_kernel,
        grid=(FOURIER_GROUPS,),
        in_specs=[pl.BlockSpec((d, fg), lambda g: (0, g)),
                  pl.BlockSpec((None, fg, 2 * fg), lambda g: (g, 0, 0))],
        out_specs=[ospec, ospec],
        out_shape=[jax.ShapeDtypeStruct((d, D_MODEL), BF16)] * 2,
        compiler_params=_params(1),
        name="fold_in_proj",
    )(w_in_u, ab)


def _dft1_kernel(f_ref, p_ref, q_ref, t_ref, *, nb, n1):
    f = f_ref[...]
    for jj in range(nb):
        rows = slice(jj * n1, (jj + 1) * n1)
        rhs = jnp.concatenate([p_ref[rows, :], q_ref[rows, :]], axis=0)
        t_ref[jj] = jnp.dot(f, rhs, preferred_element_type=F32).astype(BF16)


def _dft_stage1(pq_t, f1, n1, n2):
    s, c2 = pq_t.shape
    c = c2 // 2
    nb = SUBLANES
    cb = min(1024, c)
    ncb = c // cb
    return pl.pallas_call(
        functools.partial(_dft1_kernel, nb=nb, n1=n1),
        grid=(n2 // nb, ncb),
        in_specs=[pl.BlockSpec((2 * n1, 2 * n1), lambda b, j: (0, 0)),
                  pl.BlockSpec((nb * n1, cb), lambda b, j: (b, j)),
                  pl.BlockSpec((nb * n1, cb), lambda b, j: (b, j + ncb))],
        out_specs=pl.BlockSpec((nb, 2 * n1, cb), lambda b, j: (b, 0, j)),
        out_shape=jax.ShapeDtypeStruct((n2, 2 * n1, c), BF16),
        compiler_params=_params(2),
        name="dft_stage1",
    )(f1, pq_t, pq_t)


def _dft2_kernel(l_ref, t_ref, g_ref, o_ref, *, n2, cb):
    rhs = t_ref[...].reshape(SUBLANES * 2 * n2, cb)
    y = jnp.dot(l_ref[0], rhs, preferred_element_type=F32)
    o_ref[...] = y.reshape(n2, SUBLANES, cb) * jax.nn.silu(g_ref[...])


def _dft_stage2(t_kn, l2, g, n1, n2):
    c = t_kn.shape[-1]
    cb = min(1024, c)
    row_blocks = pl.BlockSpec((n2, SUBLANES, cb), lambda g, j: (0, g, j))
    out = pl.pallas_call(
        functools.partial(_dft2_kernel, n2=n2, cb=cb),
        grid=(n1 // SUBLANES, c // cb),
        in_specs=[pl.BlockSpec((1, SUBLANES * n2, SUBLANES * 2 * n2), lambda g, j: (g, 0, 0)),
                  pl.BlockSpec((SUBLANES, 2, n2, cb), lambda g, j: (g, 0, 0, j)),
                  row_blocks],
        out_specs=row_blocks,
        out_shape=jax.ShapeDtypeStruct((n2, n1, c), F32),
        compiler_params=_params(2),
        name="dft_stage2",
    )(l2, t_kn, g.reshape(n2, n1, c))
    return out.reshape(n1 * n2, c)


def _dft_matrices(s):
    n1, n2 = _dft_split(s)
    a = jnp.arange(n1, dtype=jnp.int32)
    c1, s1 = _angles(a[:, None] * a[None, :], n1)
    f1 = jnp.concatenate([jnp.concatenate([c1, -s1], axis=1),
                          jnp.concatenate([-s1, -c1], axis=1)], axis=0).astype(BF16)
    groups = n1 // SUBLANES
    g = jnp.arange(groups, dtype=jnp.int32)[:, None, None, None]
    k2 = jnp.arange(n2, dtype=jnp.int32)[None, :, None, None]
    j = jnp.arange(SUBLANES, dtype=jnp.int32)[None, None, :, None]
    nn = jnp.arange(n2, dtype=jnp.int32)[None, None, None, :]
    c2, s2 = _angles((SUBLANES * g + j + n1 * k2) * nn, s)
    cs = jnp.stack([c2, s2], axis=3).astype(BF16)
    eye = jnp.eye(SUBLANES, dtype=BF16)
    l2 = cs[:, :, :, None] * eye[None, None, :, :, None, None]
    return n1, n2, f1, l2.reshape(groups, n2 * SUBLANES, SUBLANES * 2 * n2)


def kernel(x, c, norm_w, ada_w, ada_b, w_in_ab, w_pool, pool_scale, q_norm_w, k_norm_w, lambda_q1, lambda_k1,
           lambda_q2, lambda_k2, subln_w, w_out_ab, w_in_c, w_fourier, w_out_c):
    batch, s, d = x.shape
    assert batch == 1 and d == D_MODEL and s % ATT_TK == 0
    xs = x.reshape(s, d)
    mod = _ada_mod(c, ada_w, ada_b)
    n1, n2, f1, l2 = _dft_matrices(s)
    row = lambda v: v.reshape(1, -1)

    for i in range(DEPTH):
        shift, scale, gate = mod[i, :, 0:d], mod[i, :, d:2 * d], mod[i, :, 2 * d:3 * d]
        h = _modulate(xs, row(norm_w[i]), shift, scale)
        j = i // 2
        if i % 2 == 0:
            w_in = w_in_ab[j].astype(BF16)
            o1, o2, o3, o4 = POOL_WIDTH, POOL_WIDTH + DIFF_WIDTH, POOL_WIDTH + 2 * DIFF_WIDTH, POOL_WIDTH + 3 * DIFF_WIDTH
            lambda_init = 0.8 - 0.6 * math.exp(-0.3 * i)
            u_pool = _matmul(h, w_in[:, :o1], F32, 1024, 512, name="pool_in_proj")
            gte = _matmul(h, w_in[:, o4:], F32, 1024, 512, name="gate_in_proj")
            qt, ka, vt = _qkv_proj(h, w_in[:, o1:o2], w_in[:, o2:o3], w_in[:, o3:o4],
                                   row(q_norm_w[j]), row(k_norm_w[j]))
            y_a = _pool_mix(u_pool, w_pool[j].astype(BF16), row(pool_scale[j]), gte)
            y_b = _diff_attn(qt, ka, vt, row(lambda_q1[j]), row(lambda_k1[j]), row(lambda_q2[j]),
                             row(lambda_k2[j]), subln_w[j], gte, lambda_init)
            xs = _out_proj([y_a, y_b], w_out_ab[j].astype(BF16), xs, gate)
        else:
            w_in = w_in_c[j].astype(BF16)
            ab = _fold_channel_dft(w_fourier[j], s)
            w_p, w_q = _fold_in_proj(w_in[:, :d], ab)
            gte = _matmul(h, w_in[:, d:], F32, 1024, 512, name="gate_in_proj")
            h_t = h.reshape(n1, n2, d).transpose(1, 0, 2).reshape(s, d)
            pq_t = _matmul(h_t, jnp.concatenate([w_p, w_q], axis=1), BF16, 1024, 512, name="fourier_in_proj")
            t_nk = _dft_stage1(pq_t, f1, n1, n2)
            t_kn = t_nk.reshape(n2, 2, n1, d).transpose(2, 1, 0, 3)
            y = _dft_stage2(t_kn, l2, gte, n1, n2)
            xs = _out_proj([y], w_out_c[j].astype(BF16), xs, gate)
    return xs.reshape(batch, s, d)
```

```python
import functools
import math

import jax
import jax.numpy as jnp
from jax import lax
from jax.experimental import pallas as pl
from jax.experimental.pallas import tpu as pltpu

F32 = jnp.float32
BF16 = jnp.bfloat16

D_MODEL = 2048
DEPTH = 4
NORM_EPS = 1e-6

POOL_WINDOWS = (2, 4, 8, 16)
POOL_WIDTH = D_MODEL // 2
POOL_GROUP = POOL_WIDTH // len(POOL_WINDOWS)
POOL_HALO = 8

HEADS = 8
HEAD_DIM = 64
HEAD_V = 2 * HEAD_DIM
DIFF_WIDTH = HEADS * HEAD_V
AB_WIDTH = POOL_WIDTH + DIFF_WIDTH

FOURIER_GROUPS = 4
FOURIER_GROUP = D_MODEL // FOURIER_GROUPS

LANES = 128
SUBLANES = 8
BF16_ROWS = 16
POS_RADIX = 256
COEF_PARTS = 3
LOG2E = math.log2(math.e)

ATT_TQ = 512
ATT_TK = 512
ATT_UNROLL = 4
V_ROWS = HEAD_V + BF16_ROWS
NEG_BIG = -1e30

VMEM_LIMIT = 56 * 1024 * 1024


def _params(n_axes):
    return pltpu.CompilerParams(dimension_semantics=("arbitrary",) * n_axes,
                                vmem_limit_bytes=VMEM_LIMIT)


def _mod_kernel(c_ref, w_ref, b_ref, o_ref):
    c = c_ref[...]
    o_ref[...] = jnp.sum(jax.nn.silu(c) * w_ref[...], axis=0, keepdims=True) + b_ref[...]


def _ada_mod(c, ada_w, ada_b):
    depth, d, n = ada_w.shape
    tn = 512
    return pl.pallas_call(
        _mod_kernel,
        grid=(depth, n // tn),
        in_specs=[pl.BlockSpec((d, 1), lambda i, j: (0, 0)),
                  pl.BlockSpec((None, d, tn), lambda i, j: (i, 0, j)),
                  pl.BlockSpec((None, 1, tn), lambda i, j: (i, 0, j))],
        out_specs=pl.BlockSpec((None, 1, tn), lambda i, j: (i, 0, j)),
        out_shape=jax.ShapeDtypeStruct((depth, 1, n), F32),
        compiler_params=_params(2),
        name="ada_mod",
    )(c.reshape(d, 1), ada_w, ada_b.reshape(depth, 1, n))


def _modulate_kernel(x_ref, nw_ref, shift_ref, scale_ref, o_ref):
    xf = x_ref[...]
    r = lax.rsqrt(jnp.mean(xf * xf, axis=-1, keepdims=True) + NORM_EPS)
    y = xf * r * nw_ref[...]
    y = y * (1.0 + scale_ref[...]) + shift_ref[...]
    o_ref[...] = y.astype(o_ref.dtype)


def _modulate(x, nw, shift, scale):
    s, d = x.shape
    tm = 512
    vec = pl.BlockSpec((1, d), lambda i: (0, 0))
    return pl.pallas_call(
        _modulate_kernel,
        grid=(s // tm,),
        in_specs=[pl.BlockSpec((tm, d), lambda i: (i, 0)), vec, vec, vec],
        out_specs=pl.BlockSpec((tm, d), lambda i: (i, 0)),
        out_shape=jax.ShapeDtypeStruct((s, d), BF16),
        compiler_params=_params(1),
        name="modulate",
    )(x, nw, shift, scale)


def _mm_kernel(a_ref, w_ref, o_ref, *, precision):
    o_ref[...] = jnp.dot(a_ref[...], w_ref[...], preferred_element_type=F32,
                         precision=precision).astype(o_ref.dtype)


def _matmul(a, w, out_dtype, tm, tn, precision=None, name="matmul"):
    m, k = a.shape
    n = w.shape[1]
    tm, tn = min(tm, m), min(tn, n)
    return pl.pallas_call(
        functools.partial(_mm_kernel, precision=precision),
        grid=(m // tm, n // tn),
        in_specs=[pl.BlockSpec((tm, k), lambda i, j: (i, 0)),
                  pl.BlockSpec((k, tn), lambda i, j: (0, j))],
        out_specs=pl.BlockSpec((tm, tn), lambda i, j: (i, j)),
        out_shape=jax.ShapeDtypeStruct((m, n), out_dtype),
        compiler_params=_params(2),
        name=name,
    )(a, w)


def _qkv_kernel(h_ref, wq_ref, wk_ref, wv_ref, qnw_ref, knw_ref, qt_ref, ka_ref, vt_ref, *, tm):
    i = pl.program_id(0)
    h = h_ref[...]
    lane = lax.broadcasted_iota(jnp.int32, (tm, LANES), 1)
    first = lane < HEAD_DIM
    pos = lax.broadcasted_iota(jnp.int32, (tm, LANES), 0) + i * tm
    pos_lo = (pos % POS_RADIX).astype(F32)
    pos_hi = ((pos % ATT_TK) // POS_RADIX).astype(F32)
    pos_cols = jnp.where(lane < HEAD_DIM + COEF_PARTS, pos_lo, jnp.where(lane < HEAD_DIM + 2 * COEF_PARTS, pos_hi, 0.0))
    ones_rows = (lax.broadcasted_iota(jnp.int32, (BF16_ROWS, ATT_TK), 0) == 0).astype(BF16)

    def halves_rms(z, w):
        sq = z * z
        ss1 = jnp.sum(jnp.where(first, sq, 0.0), axis=-1, keepdims=True)
        ss2 = jnp.sum(jnp.where(first, 0.0, sq), axis=-1, keepdims=True)
        r = jnp.where(first, lax.rsqrt(ss1 / HEAD_DIM + NORM_EPS), lax.rsqrt(ss2 / HEAD_DIM + NORM_EPS))
        return z * r * w

    zq = jnp.dot(h, wq_ref[...], preferred_element_type=F32)
    zk = jnp.dot(h, wk_ref[...], preferred_element_type=F32)
    zv = jnp.dot(h, wv_ref[...], preferred_element_type=F32)
    for hh in range(2):
        cols = slice(hh * LANES, (hh + 1) * LANES)
        qn = halves_rms(zq[:, cols], qnw_ref[...]) * (HEAD_DIM ** -0.5) * LOG2E
        kn = halves_rms(zk[:, cols], knw_ref[...])
        for m in range(2):
            qm = qn if m == 0 else pltpu.roll(qn, HEAD_DIM, axis=1)
            km = kn if m == 0 else pltpu.roll(kn, HEAD_DIM, axis=1)
            qt_ref[hh, m] = jnp.where(first, qm, 0.0).T.astype(BF16)
            ka = jnp.where(first, km, pos_cols).astype(BF16)
            for cc in range(tm // ATT_TK):
                ka_ref[hh, m, cc] = ka[cc * ATT_TK:(cc + 1) * ATT_TK]
        v = zv[:, cols]
        for cc in range(tm // ATT_TK):
            vt_ref[hh, cc, 0:HEAD_V, :] = v[cc * ATT_TK:(cc + 1) * ATT_TK].T.astype(BF16)
            vt_ref[hh, cc, HEAD_V:V_ROWS, :] = ones_rows


def _qkv_proj(h, wq, wk, wv, qnw, knw):
    s, d = h.shape
    tm = min(1024, s)
    nchunk = s // ATT_TK
    cpt = tm // ATT_TK
    tn = 2 * LANES
    wspec = pl.BlockSpec((d, tn), lambda i, j: (0, j))
    vec = pl.BlockSpec((1, LANES), lambda i, j: (0, 0))
    return pl.pallas_call(
        functools.partial(_qkv_kernel, tm=tm),
        grid=(s // tm, HEADS // 2),
        in_specs=[pl.BlockSpec((tm, d), lambda i, j: (i, 0)), wspec, wspec, wspec, vec, vec],
        out_specs=[pl.BlockSpec((2, 2, LANES, tm), lambda i, j: (j, 0, 0, i)),
                   pl.BlockSpec((2, 2, cpt, ATT_TK, LANES), lambda i, j: (j, 0, i, 0, 0)),
                   pl.BlockSpec((2, cpt, V_ROWS, ATT_TK), lambda i, j: (j, i, 0, 0))],
        out_shape=[jax.ShapeDtypeStruct((HEADS, 2, LANES, s), BF16),
                   jax.ShapeDtypeStruct((HEADS, 2, nchunk, ATT_TK, LANES), BF16),
                   jax.ShapeDtypeStruct((HEADS, nchunk, V_ROWS, ATT_TK), BF16)],
        compiler_params=_params(2),
        name="qkv_proj",
    )(h, wq, wk, wv, jnp.tile(qnw, (1, 2)), jnp.tile(knw, (1, 2)))


def _attn_kernel(slopes_ref, qt_ref, ka_ref, vt_ref, lq1_ref, lk1_ref, lq2_ref, lk2_ref, subw_ref, g_ref,
                 o_ref, acc_ref, m_ref, qv_ref, st_a, st_b, tm_a, tm_b, *, lambda_init, nchunk, unroll):
    hd = pl.program_id(0)
    i = pl.program_id(1)
    base = hd * (1 + COEF_PARTS)
    slope = slopes_ref[base]
    rowq = lax.broadcasted_iota(jnp.int32, (LANES, ATT_TQ), 0)
    q_off = lax.broadcasted_iota(jnp.int32, (1, ATT_TQ), 1).astype(F32)

    row1 = lax.broadcasted_iota(jnp.int32, (LANES, 1), 0)
    coef_col = jnp.zeros((LANES, 1), F32)
    for t in range(COEF_PARTS):
        piece = slopes_ref[base + 1 + t]
        coef_col = jnp.where(row1 == HEAD_DIM + t, piece, coef_col)
        coef_col = jnp.where(row1 == HEAD_DIM + COEF_PARTS + t, piece * POS_RADIX, coef_col)
    for m in range(2):
        qf = qt_ref[0, m].astype(F32)
        for kind, sign in ((0, -1.0), (1, 0.0), (2, 1.0)):
            qv_ref[kind, m] = jnp.where(rowq >= HEAD_DIM, sign * coef_col, qf).astype(BF16)
    acc_ref[...] = jnp.zeros(acc_ref.shape, F32)
    m_ref[...] = jnp.full(m_ref.shape, NEG_BIG, F32)

    bufs = ((st_a, tm_a), (st_b, tm_b))

    def chunk_of(e):
        return jnp.where(e == 0, i, e - 1 + (e - 1 >= i).astype(jnp.int32))

    def qk(e, st_ref, tm_ref):
        c = chunk_of(e)
        kind = 1 + (i > c).astype(jnp.int32) - (i < c).astype(jnp.int32)
        for m in range(2):
            st = jnp.dot(ka_ref[0, m, c], qv_ref[kind, m], preferred_element_type=F32)
            st_ref[m] = st
            tm_ref[m] = jnp.max(st, axis=0, keepdims=True)

    def add_diag_bias(st_ref, tm_ref):
        key_off = lax.broadcasted_iota(jnp.int32, (ATT_TK, ATT_TQ), 0)
        qry_off = lax.broadcasted_iota(jnp.int32, (ATT_TK, ATT_TQ), 1)
        bias = slope * jnp.abs(key_off - qry_off).astype(F32)
        for m in range(2):
            st = st_ref[m] - bias
            st_ref[m] = st
            tm_ref[m] = jnp.max(st, axis=0, keepdims=True)

    def softmax_pv(e, st_ref, tm_ref):
        c = chunk_of(e)
        gap = i - c
        sgn = (gap > 0).astype(F32) - (gap < 0).astype(F32)
        cvec = -slope * ((jnp.abs(gap) * ATT_TK).astype(F32) + sgn * q_off)
        for m in range(2):
            m_old = m_ref[m]
            m_new = jnp.maximum(m_old, tm_ref[m] + cvec)
            p = jnp.exp2(st_ref[m] - (m_new - cvec))
            pv = jnp.dot(vt_ref[0, c], p.astype(BF16), preferred_element_type=F32)
            acc_ref[m] = acc_ref[m] * jnp.exp2(m_old - m_new) + pv
            m_ref[m] = m_new

    def run(e0, count, qk_last):
        for u in range(count):
            if u < count - 1 or qk_last:
                qk(e0 + u + 1, *bufs[(u + 1) % 2])
            softmax_pv(e0 + u, *bufs[u % 2])

    def loop_body(it, carry):
        run(it * unroll, unroll, True)
        return carry

    qk(0, st_a, tm_a)
    add_diag_bias(st_a, tm_a)
    n_loop = nchunk // unroll - 1
    lax.fori_loop(0, n_loop, loop_body, 0)
    run(n_loop * unroll, unroll, False)

    lam = (jnp.exp(jnp.sum(lq1_ref[...] * lk1_ref[...], axis=-1, keepdims=True))
           - jnp.exp(jnp.sum(lq2_ref[...] * lk2_ref[...], axis=-1, keepdims=True)) + lambda_init)
    a1 = acc_ref[0]
    a2 = acc_ref[1]
    o = a1[0:HEAD_V] / a1[HEAD_V:HEAD_V + 1] - lam * (a2[0:HEAD_V] / a2[HEAD_V:HEAD_V + 1])
    r = lax.rsqrt(jnp.mean(o * o, axis=0, keepdims=True) + NORM_EPS)
    y = (o * r * subw_ref[...]) * (1.0 - lambda_init)
    o_ref[...] = (y.T * jax.nn.silu(g_ref[...])).astype(o_ref.dtype)


def _diff_attn(qt, ka, vt, lq1, lk1, lq2, lk2, subw, g, lambda_init):
    s = qt.shape[-1]
    g_col0 = (g.shape[1] - DIFF_WIDTH) // HEAD_V
    nchunk = s // ATT_TK
    unroll = min(ATT_UNROLL, nchunk)
    assert unroll % 2 == 0 and nchunk % unroll == 0
    whole = jnp.asarray([LOG2E * 2.0 ** (-8.0 * (h + 1) / HEADS) for h in range(HEADS)], F32)
    pieces, rest = [], whole
    for _ in range(COEF_PARTS):
        pieces.append(rest.astype(BF16).astype(F32))
        rest = rest - pieces[-1]
    slopes = jnp.stack([whole] + pieces, axis=1).reshape(-1)
    vec = pl.BlockSpec((1, HEAD_DIM), lambda h, i: (0, 0))
    scores = pltpu.VMEM((2, ATT_TK, ATT_TQ), F32)
    colmax = pltpu.VMEM((2, 1, ATT_TQ), F32)
    return pl.pallas_call(
        functools.partial(_attn_kernel, lambda_init=lambda_init, nchunk=nchunk, unroll=unroll),
        grid=(HEADS, s // ATT_TQ),
        in_specs=[pl.BlockSpec(memory_space=pltpu.SMEM),
                  pl.BlockSpec((1, 2, LANES, ATT_TQ), lambda h, i: (h, 0, 0, i)),
                  pl.BlockSpec((1, 2, nchunk, ATT_TK, LANES), lambda h, i: (h, 0, 0, 0, 0)),
                  pl.BlockSpec((1, nchunk, V_ROWS, ATT_TK), lambda h, i: (h, 0, 0, 0)),
                  vec, vec, vec, vec,
                  pl.BlockSpec((HEAD_V, 1), lambda h, i: (0, 0)),
                  pl.BlockSpec((ATT_TQ, HEAD_V), lambda h, i: (i, g_col0 + h))],
        out_specs=pl.BlockSpec((ATT_TQ, HEAD_V), lambda h, i: (i, h)),
        out_shape=jax.ShapeDtypeStruct((s, DIFF_WIDTH), BF16),
        scratch_shapes=[pltpu.VMEM((2, V_ROWS, ATT_TQ), F32), pltpu.VMEM((2, 1, ATT_TQ), F32),
                        pltpu.VMEM((3, 2, LANES, ATT_TQ), BF16), scores, scores, colmax, colmax],
        compiler_params=_params(2),
        name="diff_attn",
    )(slopes, qt, ka, vt, lq1, lk1, lq2, lk2, subw.reshape(HEAD_V, 1), g)


def _pool_kernel(prev_ref, cur_ref, next_ref, w_ref, scale_ref, g_ref, o_ref, ext_ref, *, tm, seq):
    i = pl.program_id(0)
    last = pl.num_programs(0) - 1
    zero_halo = jnp.zeros((POOL_HALO, POOL_WIDTH), F32)
    ext_ref[0:POOL_HALO] = jnp.where(i > 0, prev_ref[...], zero_halo)
    ext_ref[POOL_HALO:POOL_HALO + tm] = cur_ref[...]
    ext_ref[POOL_HALO + tm:POOL_HALO + tm + POOL_HALO] = jnp.where(i < last, next_ref[...], zero_halo)
    t = lax.broadcasted_iota(jnp.int32, (tm, 1), 0) + i * tm
    for g, w in enumerate(POOL_WINDOWS):
        cols = pl.ds(g * POOL_GROUP, POOL_GROUP)
        win = ext_ref[pl.ds(POOL_HALO - w // 2, tm), cols]
        for d in range(-(w // 2) + 1, w - w // 2):
            win = win + ext_ref[pl.ds(POOL_HALO + d, tm), cols]
        lo = jnp.maximum(t - w // 2, 0)
        hi = jnp.minimum(t + (w - w // 2) - 1, seq - 1)
        cnt = (hi - lo + 1).astype(F32)
        pooled = win / cnt - cur_ref[:, cols]
        y = jnp.dot(pooled.astype(BF16), w_ref[g], preferred_element_type=F32)
        o_ref[:, cols] = (y * scale_ref[:, cols] * jax.nn.silu(g_ref[:, cols])).astype(o_ref.dtype)


def _pool_mix(u, w_pool, pool_scale, g):
    s, width = u.shape
    tm = 512
    hb = tm // POOL_HALO
    nhalo = s // POOL_HALO
    return pl.pallas_call(
        functools.partial(_pool_kernel, tm=tm, seq=s),
        grid=(s // tm,),
        in_specs=[pl.BlockSpec((POOL_HALO, width), lambda i: (jnp.maximum(i * hb - 1, 0), 0)),
                  pl.BlockSpec((tm, width), lambda i: (i, 0)),
                  pl.BlockSpec((POOL_HALO, width), lambda i: (jnp.minimum((i + 1) * hb, nhalo - 1), 0)),
                  pl.BlockSpec(w_pool.shape, lambda i: (0, 0, 0)),
                  pl.BlockSpec((1, width), lambda i: (0, 0)),
                  pl.BlockSpec((tm, width), lambda i: (i, 0))],
        out_specs=pl.BlockSpec((tm, width), lambda i: (i, 0)),
        out_shape=jax.ShapeDtypeStruct((s, width), BF16),
        scratch_shapes=[pltpu.VMEM((tm + 2 * POOL_HALO, width), F32)],
        compiler_params=_params(1),
        name="pool_mix",
    )(u, u, u, w_pool, pool_scale, g)


def _out_kernel(*refs, widths):
    n = len(widths)
    y_refs = refs[:n]
    w_ref, x_ref, gate_ref, o_ref = refs[n:n + 4]
    if len(refs) > n + 4:
        yb_ref = refs[n + 4]

        @pl.when(pl.program_id(1) == 0)
        def _():
            yb_ref[...] = y_refs[0][...].astype(BF16)

        y_refs = [yb_ref]
    acc = None
    off = 0
    for y_ref, wd in zip(y_refs, widths):
        part = jnp.dot(y_ref[...], w_ref[off:off + wd, :], preferred_element_type=F32)
        acc = part if acc is None else acc + part
        off += wd
    o_ref[...] = x_ref[...] + gate_ref[...] * acc


def _out_proj(ys, w, x, gate):
    s, d = x.shape
    k = w.shape[0]
    tm, tn = min(1024, s), 512
    widths = tuple(y.shape[1] for y in ys)
    needs_cast = ys[0].dtype != BF16
    assert not needs_cast or len(ys) == 1
    return pl.pallas_call(
        functools.partial(_out_kernel, widths=widths),
        grid=(s // tm, d // tn),
        in_specs=[pl.BlockSpec((tm, wd), lambda i, j: (i, 0)) for wd in widths]
        + [pl.BlockSpec((k, tn), lambda i, j: (0, j)),
           pl.BlockSpec((tm, tn), lambda i, j: (i, j)),
           pl.BlockSpec((1, tn), lambda i, j: (0, j))],
        out_specs=pl.BlockSpec((tm, tn), lambda i, j: (i, j)),
        out_shape=jax.ShapeDtypeStruct((s, d), F32),
        scratch_shapes=[pltpu.VMEM((tm, k), BF16)] if needs_cast else [],
        compiler_params=_params(2),
        name="out_proj",
    )(*ys, w, x, gate)


def _dft_split(s):
    n1 = 1 << (int(math.log2(s)) // 2)
    n2 = s // n1
    assert n1 * n2 == s and n1 % SUBLANES == 0 and n2 % SUBLANES == 0
    return n1, n2


def _angles(num, den):
    ang = (2.0 * math.pi / den) * (num % den).astype(F32)
    return jnp.cos(ang), jnp.sin(ang)


def _fold_cs_kernel(cs_ref, wf_ref, o_ref):
    wf = wf_ref[...]
    o_ref[:, 0:FOURIER_GROUP] = jnp.dot(cs_ref[0], wf, preferred_element_type=F32,
                                        precision=lax.Precision.HIGHEST)
    o_ref[:, FOURIER_GROUP:] = jnp.dot(cs_ref[1], wf, preferred_element_type=F32,
                                       precision=lax.Precision.HIGHEST)


def _fold_channel_dft(w_fourier_j, seq):
    fg = FOURIER_GROUP
    idx = jnp.arange(fg, dtype=jnp.int32)
    cc, sc = _angles(idx[:, None] * idx[None, :], fg)
    cs = jnp.stack([cc, sc]) * (1.0 / math.sqrt(seq * fg))
    return pl.pallas_call(
        _fold_cs_kernel,
        grid=(FOURIER_GROUPS,),
        in_specs=[pl.BlockSpec((2, fg, fg), lambda g: (0, 0, 0)),
                  pl.BlockSpec((None, fg, fg), lambda g: (g, 0, 0))],
        out_specs=pl.BlockSpec((None, fg, 2 * fg), lambda g: (g, 0, 0)),
        out_shape=jax.ShapeDtypeStruct((FOURIER_GROUPS, fg, 2 * fg), F32),
        compiler_params=_params(1),
        name="fold_channel_dft",
    )(cs, w_fourier_j)


def _fold_win_kernel(w_ref, ab_ref, p_ref, q_ref):
    r = jnp.dot(w_ref[...], ab_ref[...].astype(BF16), preferred_element_type=F32)
    p_ref[...] = r[:, 0:FOURIER_GROUP].astype(BF16)
    q_ref[...] = r[:, FOURIER_GROUP:].astype(BF16)


def _fold_in_proj(w_in_u, ab):
    d = w_in_u.shape[0]
    fg = FOURIER_GROUP
    ospec = pl.BlockSpec((d, fg), lambda g: (0, g))
    return pl.pallas_call(
        _fold_win_kernel,
        grid=(FOURIER_GROUPS,),
        in_specs=[pl.BlockSpec((d, fg), lambda g: (0, g)),
                  pl.BlockSpec((None, fg, 2 * fg), lambda g: (g, 0, 0))],
        out_specs=[ospec, ospec],
        out_shape=[jax.ShapeDtypeStruct((d, D_MODEL), BF16)] * 2,
        compiler_params=_params(1),
        name="fold_in_proj",
    )(w_in_u, ab)


def _dft1_kernel(f_ref, p_ref, q_ref, t_ref, *, nb, n1):
    f = f_ref[...]
    for jj in range(nb):
        rows = slice(jj * n1, (jj + 1) * n1)
        rhs = jnp.concatenate([p_ref[rows, :], q_ref[rows, :]], axis=0)
        t_ref[jj] = jnp.dot(f, rhs, preferred_element_type=F32).astype(BF16)


def _dft_stage1(pq_t, f1, n1, n2):
    s, c2 = pq_t.shape
    c = c2 // 2
    nb = SUBLANES
    cb = min(1024, c)
    ncb = c // cb
    return pl.pallas_call(
        functools.partial(_dft1_kernel, nb=nb, n1=n1),
        grid=(n2 // nb, ncb),
        in_specs=[pl.BlockSpec((2 * n1, 2 * n1), lambda b, j: (0, 0)),
                  pl.BlockSpec((nb * n1, cb), lambda b, j: (b, j)),
                  pl.BlockSpec((nb * n1, cb), lambda b, j: (b, j + ncb))],
        out_specs=pl.BlockSpec((nb, 2 * n1, cb), lambda b, j: (b, 0, j)),
        out_shape=jax.ShapeDtypeStruct((n2, 2 * n1, c), BF16),
        compiler_params=_params(2),
        name="dft_stage1",
    )(f1, pq_t, pq_t)


def _dft2_kernel(l_ref, t_ref, g_ref, o_ref, *, n2, cb):
    rhs = t_ref[...].reshape(SUBLANES * 2 * n2, cb)
    y = jnp.dot(l_ref[0], rhs, preferred_element_type=F32)
    o_ref[...] = y.reshape(n2, SUBLANES, cb) * jax.nn.silu(g_ref[...])


def _dft_stage2(t_kn, l2, g, n1, n2):
    c = t_kn.shape[-1]
    cb = min(1024, c)
    row_blocks = pl.BlockSpec((n2, SUBLANES, cb), lambda g, j: (0, g, j))
    out = pl.pallas_call(
        functools.partial(_dft2_kernel, n2=n2, cb=cb),
        grid=(n1 // SUBLANES, c // cb),
        in_specs=[pl.BlockSpec((1, SUBLANES * n2, SUBLANES * 2 * n2), lambda g, j: (g, 0, 0)),
                  pl.BlockSpec((SUBLANES, 2, n2, cb), lambda g, j: (g, 0, 0, j)),
                  row_blocks],
        out_specs=row_blocks,
        out_shape=jax.ShapeDtypeStruct((n2, n1, c), F32),
        compiler_params=_params(2),
        name="dft_stage2",
    )(l2, t_kn, g.reshape(n2, n1, c))
    return out.reshape(n1 * n2, c)


def _dft_matrices(s):
    n1, n2 = _dft_split(s)
    a = jnp.arange(n1, dtype=jnp.int32)
    c1, s1 = _angles(a[:, None] * a[None, :], n1)
    f1 = jnp.concatenate([jnp.concatenate([c1, -s1], axis=1),
                          jnp.concatenate([-s1, -c1], axis=1)], axis=0).astype(BF16)
    groups = n1 // SUBLANES
    g = jnp.arange(groups, dtype=jnp.int32)[:, None, None, None]
    k2 = jnp.arange(n2, dtype=jnp.int32)[None, :, None, None]
    j = jnp.arange(SUBLANES, dtype=jnp.int32)[None, None, :, None]
    nn = jnp.arange(n2, dtype=jnp.int32)[None, None, None, :]
    c2, s2 = _angles((SUBLANES * g + j + n1 * k2) * nn, s)
    cs = jnp.stack([c2, s2], axis=3).astype(BF16)
    eye = jnp.eye(SUBLANES, dtype=BF16)
    l2 = cs[:, :, :, None] * eye[None, None, :, :, None, None]
    return n1, n2, f1, l2.reshape(groups, n2 * SUBLANES, SUBLANES * 2 * n2)


def kernel(x, c, norm_w, ada_w, ada_b, w_in_ab, w_pool, pool_scale, q_norm_w, k_norm_w, lambda_q1, lambda_k1,
           lambda_q2, lambda_k2, subln_w, w_out_ab, w_in_c, w_fourier, w_out_c):
    batch, s, d = x.shape
    assert batch == 1 and d == D_MODEL and s % ATT_TK == 0
    xs = x.reshape(s, d)
    mod = _ada_mod(c, ada_w, ada_b)
    n1, n2, f1, l2 = _dft_matrices(s)
    row = lambda v: v.reshape(1, -1)

    for i in range(DEPTH):
        shift, scale, gate = mod[i, :, 0:d], mod[i, :, d:2 * d], mod[i, :, 2 * d:3 * d]
        h = _modulate(xs, row(norm_w[i]), shift, scale)
        j = i // 2
        if i % 2 == 0:
            w_in = w_in_ab[j].astype(BF16)
            o1, o2, o3, o4 = POOL_WIDTH, POOL_WIDTH + DIFF_WIDTH, POOL_WIDTH + 2 * DIFF_WIDTH, POOL_WIDTH + 3 * DIFF_WIDTH
            lambda_init = 0.8 - 0.6 * math.exp(-0.3 * i)
            u_pool = _matmul(h, w_in[:, :o1], F32, 1024, 512, name="pool_in_proj")
            gte = _matmul(h, w_in[:, o4:], F32, 1024, 512, name="gate_in_proj")
            qt, ka, vt = _qkv_proj(h, w_in[:, o1:o2], w_in[:, o2:o3], w_in[:, o3:o4],
                                   row(q_norm_w[j]), row(k_norm_w[j]))
            y_a = _pool_mix(u_pool, w_pool[j].astype(BF16), row(pool_scale[j]), gte)
            y_b = _diff_attn(qt, ka, vt, row(lambda_q1[j]), row(lambda_k1[j]), row(lambda_q2[j]),
                             row(lambda_k2[j]), subln_w[j], gte, lambda_init)
            xs = _out_proj([y_a, y_b], w_out_ab[j].astype(BF16), xs, gate)
        else:
            w_in = w_in_c[j].astype(BF16)
            ab = _fold_channel_dft(w_fourier[j], s)
            w_p, w_q = _fold_in_proj(w_in[:, :d], ab)
            gte = _matmul(h, w_in[:, d:], F32, 1024, 512, name="gate_in_proj")
            h_t = h.reshape(n1, n2, d).transpose(1, 0, 2).reshape(s, d)
            pq_t = _matmul(h_t, jnp.concatenate([w_p, w_q], axis=1), BF16, 1024, 512, name="fourier_in_proj")
            t_nk = _dft_stage1(pq_t, f1, n1, n2)
            t_kn = t_nk.reshape(n2, 2, n1, d).transpose(2, 1, 0, 3)
            y = _dft_stage2(t_kn, l2, gte, n1, n2)
            xs = _out_proj([y], w_out_c[j].astype(BF16), xs, gate)
    return xs.reshape(batch, s, d)
```

```python
import functools
import math

import jax
import jax.numpy as jnp
from jax import lax
from jax.experimental import pallas as pl
from jax.experimental.pallas import tpu as pltpu

F32 = jnp.float32
BF16 = jnp.bfloat16

D_MODEL = 2048
DEPTH = 4
NORM_EPS = 1e-6

POOL_WINDOWS = (2, 4, 8, 16)
POOL_WIDTH = D_MODEL // 2
POOL_GROUP = POOL_WIDTH // len(POOL_WINDOWS)
POOL_HALO = 8

HEADS = 8
HEAD_DIM = 64
HEAD_V = 2 * HEAD_DIM
DIFF_WIDTH = HEADS * HEAD_V
AB_WIDTH = POOL_WIDTH + DIFF_WIDTH

FOURIER_GROUPS = 4
FOURIER_GROUP = D_MODEL // FOURIER_GROUPS

LANES = 128
SUBLANES = 8
BF16_ROWS = 16
POS_RADIX = 256
COEF_PARTS = 3
LOG2E = math.log2(math.e)

ATT_TQ = 512
ATT_TK = 512
ATT_UNROLL = 8
V_ROWS = HEAD_V + BF16_ROWS
NEG_BIG = -1e30

VMEM_LIMIT = 56 * 1024 * 1024


def _params(n_axes):
    return pltpu.CompilerParams(dimension_semantics=("arbitrary",) * n_axes,
                                vmem_limit_bytes=VMEM_LIMIT)


def _mod_kernel(c_ref, w_ref, b_ref, o_ref):
    c = c_ref[...]
    o_ref[...] = jnp.sum(jax.nn.silu(c) * w_ref[...], axis=0, keepdims=True) + b_ref[...]


def _ada_mod(c, ada_w, ada_b):
    depth, d, n = ada_w.shape
    tn = 512
    return pl.pallas_call(
        _mod_kernel,
        grid=(depth, n // tn),
        in_specs=[pl.BlockSpec((d, 1), lambda i, j: (0, 0)),
                  pl.BlockSpec((None, d, tn), lambda i, j: (i, 0, j)),
                  pl.BlockSpec((None, 1, tn), lambda i, j: (i, 0, j))],
        out_specs=pl.BlockSpec((None, 1, tn), lambda i, j: (i, 0, j)),
        out_shape=jax.ShapeDtypeStruct((depth, 1, n), F32),
        compiler_params=_params(2),
        name="ada_mod",
    )(c.reshape(d, 1), ada_w, ada_b.reshape(depth, 1, n))


def _modulate_kernel(x_ref, nw_ref, shift_ref, scale_ref, o_ref):
    xf = x_ref[...]
    r = lax.rsqrt(jnp.mean(xf * xf, axis=-1, keepdims=True) + NORM_EPS)
    y = xf * r * nw_ref[...]
    y = y * (1.0 + scale_ref[...]) + shift_ref[...]
    o_ref[...] = y.astype(o_ref.dtype)


def _modulate(x, nw, shift, scale):
    s, d = x.shape
    tm = 512
    vec = pl.BlockSpec((1, d), lambda i: (0, 0))
    return pl.pallas_call(
        _modulate_kernel,
        grid=(s // tm,),
        in_specs=[pl.BlockSpec((tm, d), lambda i: (i, 0)), vec, vec, vec],
        out_specs=pl.BlockSpec((tm, d), lambda i: (i, 0)),
        out_shape=jax.ShapeDtypeStruct((s, d), BF16),
        compiler_params=_params(1),
        name="modulate",
    )(x, nw, shift, scale)


def _mm_kernel(a_ref, w_ref, o_ref, *, precision):
    o_ref[...] = jnp.dot(a_ref[...], w_ref[...], preferred_element_type=F32,
                         precision=precision).astype(o_ref.dtype)


def _matmul(a, w, out_dtype, tm, tn, precision=None, name="matmul"):
    m, k = a.shape
    n = w.shape[1]
    tm, tn = min(tm, m), min(tn, n)
    return pl.pallas_call(
        functools.partial(_mm_kernel, precision=precision),
        grid=(m // tm, n // tn),
        in_specs=[pl.BlockSpec((tm, k), lambda i, j: (i, 0)),
                  pl.BlockSpec((k, tn), lambda i, j: (0, j))],
        out_specs=pl.BlockSpec((tm, tn), lambda i, j: (i, j)),
        out_shape=jax.ShapeDtypeStruct((m, n), out_dtype),
        compiler_params=_params(2),
        name=name,
    )(a, w)


def _qkv_kernel(h_ref, wq_ref, wk_ref, wv_ref, qnw_ref, knw_ref, qt_ref, ka_ref, vt_ref, *, tm):
    i = pl.program_id(0)
    h = h_ref[...]
    lane = lax.broadcasted_iota(jnp.int32, (tm, LANES), 1)
    first = lane < HEAD_DIM
    pos = lax.broadcasted_iota(jnp.int32, (tm, LANES), 0) + i * tm
    pos_lo = (pos % POS_RADIX).astype(F32)
    pos_hi = ((pos % ATT_TK) // POS_RADIX).astype(F32)
    pos_cols = jnp.where(lane < HEAD_DIM + COEF_PARTS, pos_lo, jnp.where(lane < HEAD_DIM + 2 * COEF_PARTS, pos_hi, 0.0))
    ones_rows = (lax.broadcasted_iota(jnp.int32, (BF16_ROWS, ATT_TK), 0) == 0).astype(BF16)

    def halves_rms(z, w):
        sq = z * z
        ss1 = jnp.sum(jnp.where(first, sq, 0.0), axis=-1, keepdims=True)
        ss2 = jnp.sum(jnp.where(first, 0.0, sq), axis=-1, keepdims=True)
        r = jnp.where(first, lax.rsqrt(ss1 / HEAD_DIM + NORM_EPS), lax.rsqrt(ss2 / HEAD_DIM + NORM_EPS))
        return z * r * w

    zq = jnp.dot(h, wq_ref[...], preferred_element_type=F32)
    zk = jnp.dot(h, wk_ref[...], preferred_element_type=F32)
    zv = jnp.dot(h, wv_ref[...], preferred_element_type=F32)
    for hh in range(2):
        cols = slice(hh * LANES, (hh + 1) * LANES)
        qn = halves_rms(zq[:, cols], qnw_ref[...]) * (HEAD_DIM ** -0.5) * LOG2E
        kn = halves_rms(zk[:, cols], knw_ref[...])
        for m in range(2):
            qm = qn if m == 0 else pltpu.roll(qn, HEAD_DIM, axis=1)
            km = kn if m == 0 else pltpu.roll(kn, HEAD_DIM, axis=1)
            qt_ref[hh, m] = jnp.where(first, qm, 0.0).T.astype(BF16)
            ka = jnp.where(first, km, pos_cols).astype(BF16)
            for cc in range(tm // ATT_TK):
                ka_ref[hh, m, cc] = ka[cc * ATT_TK:(cc + 1) * ATT_TK]
        v = zv[:, cols]
        for cc in range(tm // ATT_TK):
            vt_ref[hh, cc, 0:HEAD_V, :] = v[cc * ATT_TK:(cc + 1) * ATT_TK].T.astype(BF16)
            vt_ref[hh, cc, HEAD_V:V_ROWS, :] = ones_rows


def _qkv_proj(h, wq, wk, wv, qnw, knw):
    s, d = h.shape
    tm = min(1024, s)
    nchunk = s // ATT_TK
    cpt = tm // ATT_TK
    tn = 2 * LANES
    wspec = pl.BlockSpec((d, tn), lambda i, j: (0, j))
    vec = pl.BlockSpec((1, LANES), lambda i, j: (0, 0))
    return pl.pallas_call(
        functools.partial(_qkv_kernel, tm=tm),
        grid=(s // tm, HEADS // 2),
        in_specs=[pl.BlockSpec((tm, d), lambda i, j: (i, 0)), wspec, wspec, wspec, vec, vec],
        out_specs=[pl.BlockSpec((2, 2, LANES, tm), lambda i, j: (j, 0, 0, i)),
                   pl.BlockSpec((2, 2, cpt, ATT_TK, LANES), lambda i, j: (j, 0, i, 0, 0)),
                   pl.BlockSpec((2, cpt, V_ROWS, ATT_TK), lambda i, j: (j, i, 0, 0))],
        out_shape=[jax.ShapeDtypeStruct((HEADS, 2, LANES, s), BF16),
                   jax.ShapeDtypeStruct((HEADS, 2, nchunk, ATT_TK, LANES), BF16),
                   jax.ShapeDtypeStruct((HEADS, nchunk, V_ROWS, ATT_TK), BF16)],
        compiler_params=_params(2),
        name="qkv_proj",
    )(h, wq, wk, wv, jnp.tile(qnw, (1, 2)), jnp.tile(knw, (1, 2)))


def _attn_kernel(slopes_ref, qt_ref, ka_ref, vt_ref, lq1_ref, lk1_ref, lq2_ref, lk2_ref, subw_ref, g_ref,
                 o_ref, acc_ref, m_ref, qv_ref, st_a, st_b, tm_a, tm_b, *, lambda_init, nchunk, unroll):
    hd = pl.program_id(0)
    i = pl.program_id(1)
    base = hd * (1 + COEF_PARTS)
    slope = slopes_ref[base]
    rowq = lax.broadcasted_iota(jnp.int32, (LANES, ATT_TQ), 0)
    q_off = lax.broadcasted_iota(jnp.int32, (1, ATT_TQ), 1).astype(F32)

    row1 = lax.broadcasted_iota(jnp.int32, (LANES, 1), 0)
    coef_col = jnp.zeros((LANES, 1), F32)
    for t in range(COEF_PARTS):
        piece = slopes_ref[base + 1 + t]
        coef_col = jnp.where(row1 == HEAD_DIM + t, piece, coef_col)
        coef_col = jnp.where(row1 == HEAD_DIM + COEF_PARTS + t, piece * POS_RADIX, coef_col)
    for m in range(2):
        qf = qt_ref[0, m].astype(F32)
        for kind, sign in ((0, -1.0), (1, 0.0), (2, 1.0)):
            qv_ref[kind, m] = jnp.where(rowq >= HEAD_DIM, sign * coef_col, qf).astype(BF16)
    acc_ref[...] = jnp.zeros(acc_ref.shape, F32)
    m_ref[...] = jnp.full(m_ref.shape, NEG_BIG, F32)

    bufs = ((st_a, tm_a), (st_b, tm_b))

    def chunk_of(e):
        return jnp.where(e == 0, i, e - 1 + (e - 1 >= i).astype(jnp.int32))

    def qk(e, st_ref, tm_ref):
        c = chunk_of(e)
        kind = 1 + (i > c).astype(jnp.int32) - (i < c).astype(jnp.int32)
        for m in range(2):
            st = jnp.dot(ka_ref[0, m, c], qv_ref[kind, m], preferred_element_type=F32)
            st_ref[m] = st
            tm_ref[m] = jnp.max(st, axis=0, keepdims=True)

    def add_diag_bias(st_ref, tm_ref):
        key_off = lax.broadcasted_iota(jnp.int32, (ATT_TK, ATT_TQ), 0)
        qry_off = lax.broadcasted_iota(jnp.int32, (ATT_TK, ATT_TQ), 1)
        bias = slope * jnp.abs(key_off - qry_off).astype(F32)
        for m in range(2):
            st = st_ref[m] - bias
            st_ref[m] = st
            tm_ref[m] = jnp.max(st, axis=0, keepdims=True)

    def softmax_pv(e, st_ref, tm_ref):
        c = chunk_of(e)
        gap = i - c
        sgn = (gap > 0).astype(F32) - (gap < 0).astype(F32)
        cvec = -slope * ((jnp.abs(gap) * ATT_TK).astype(F32) + sgn * q_off)
        for m in range(2):
            m_old = m_ref[m]
            m_new = jnp.maximum(m_old, tm_ref[m] + cvec)
            p = jnp.exp2(st_ref[m] - (m_new - cvec))
            pv = jnp.dot(vt_ref[0, c], p.astype(BF16), preferred_element_type=F32)
            acc_ref[m] = acc_ref[m] * jnp.exp2(m_old - m_new) + pv
            m_ref[m] = m_new

    def run(e0, count, qk_last):
        for u in range(count):
            if u < count - 1 or qk_last:
                qk(e0 + u + 1, *bufs[(u + 1) % 2])
            softmax_pv(e0 + u, *bufs[u % 2])

    def loop_body(it, carry):
        run(it * unroll, unroll, True)
        return carry

    qk(0, st_a, tm_a)
    add_diag_bias(st_a, tm_a)
    n_loop = nchunk // unroll - 1
    lax.fori_loop(0, n_loop, loop_body, 0)
    run(n_loop * unroll, unroll, False)

    lam = (jnp.exp(jnp.sum(lq1_ref[...] * lk1_ref[...], axis=-1, keepdims=True))
           - jnp.exp(jnp.sum(lq2_ref[...] * lk2_ref[...], axis=-1, keepdims=True)) + lambda_init)
    a1 = acc_ref[0]
    a2 = acc_ref[1]
    o = a1[0:HEAD_V] / a1[HEAD_V:HEAD_V + 1] - lam * (a2[0:HEAD_V] / a2[HEAD_V:HEAD_V + 1])
    r = lax.rsqrt(jnp.mean(o * o, axis=0, keepdims=True) + NORM_EPS)
    y = (o * r * subw_ref[...]) * (1.0 - lambda_init)
    o_ref[...] = (y.T * jax.nn.silu(g_ref[...])).astype(o_ref.dtype)


def _diff_attn(qt, ka, vt, lq1, lk1, lq2, lk2, subw, g, lambda_init):
    s = qt.shape[-1]
    g_col0 = (g.shape[1] - DIFF_WIDTH) // HEAD_V
    nchunk = s // ATT_TK
    unroll = min(ATT_UNROLL, nchunk)
    assert unroll % 2 == 0 and nchunk % unroll == 0
    whole = jnp.asarray([LOG2E * 2.0 ** (-8.0 * (h + 1) / HEADS) for h in range(HEADS)], F32)
    pieces, rest = [], whole
    for _ in range(COEF_PARTS):
        pieces.append(rest.astype(BF16).astype(F32))
        rest = rest - pieces[-1]
    slopes = jnp.stack([whole] + pieces, axis=1).reshape(-1)
    vec = pl.BlockSpec((1, HEAD_DIM), lambda h, i: (0, 0))
    scores = pltpu.VMEM((2, ATT_TK, ATT_TQ), F32)
    colmax = pltpu.VMEM((2, 1, ATT_TQ), F32)
    return pl.pallas_call(
        functools.partial(_attn_kernel, lambda_init=lambda_init, nchunk=nchunk, unroll=unroll),
        grid=(HEADS, s // ATT_TQ),
        in_specs=[pl.BlockSpec(memory_space=pltpu.SMEM),
                  pl.BlockSpec((1, 2, LANES, ATT_TQ), lambda h, i: (h, 0, 0, i)),
                  pl.BlockSpec((1, 2, nchunk, ATT_TK, LANES), lambda h, i: (h, 0, 0, 0, 0)),
                  pl.BlockSpec((1, nchunk, V_ROWS, ATT_TK), lambda h, i: (h, 0, 0, 0)),
                  vec, vec, vec, vec,
                  pl.BlockSpec((HEAD_V, 1), lambda h, i: (0, 0)),
                  pl.BlockSpec((ATT_TQ, HEAD_V), lambda h, i: (i, g_col0 + h))],
        out_specs=pl.BlockSpec((ATT_TQ, HEAD_V), lambda h, i: (i, h)),
        out_shape=jax.ShapeDtypeStruct((s, DIFF_WIDTH), BF16),
        scratch_shapes=[pltpu.VMEM((2, V_ROWS, ATT_TQ), F32), pltpu.VMEM((2, 1, ATT_TQ), F32),
                        pltpu.VMEM((3, 2, LANES, ATT_TQ), BF16), scores, scores, colmax, colmax],
        compiler_params=_params(2),
        name="diff_attn",
    )(slopes, qt, ka, vt, lq1, lk1, lq2, lk2, subw.reshape(HEAD_V, 1), g)


def _pool_kernel(prev_ref, cur_ref, next_ref, w_ref, scale_ref, g_ref, o_ref, ext_ref, *, tm, seq):
    i = pl.program_id(0)
    last = pl.num_programs(0) - 1
    zero_halo = jnp.zeros((POOL_HALO, POOL_WIDTH), F32)
    ext_ref[0:POOL_HALO] = jnp.where(i > 0, prev_ref[...], zero_halo)
    ext_ref[POOL_HALO:POOL_HALO + tm] = cur_ref[...]
    ext_ref[POOL_HALO + tm:POOL_HALO + tm + POOL_HALO] = jnp.where(i < last, next_ref[...], zero_halo)
    t = lax.broadcasted_iota(jnp.int32, (tm, 1), 0) + i * tm
    for g, w in enumerate(POOL_WINDOWS):
        cols = pl.ds(g * POOL_GROUP, POOL_GROUP)
        win = ext_ref[pl.ds(POOL_HALO - w // 2, tm), cols]
        for d in range(-(w // 2) + 1, w - w // 2):
            win = win + ext_ref[pl.ds(POOL_HALO + d, tm), cols]
        lo = jnp.maximum(t - w // 2, 0)
        hi = jnp.minimum(t + (w - w // 2) - 1, seq - 1)
        cnt = (hi - lo + 1).astype(F32)
        pooled = win / cnt - cur_ref[:, cols]
        y = jnp.dot(pooled.astype(BF16), w_ref[g], preferred_element_type=F32)
        o_ref[:, cols] = (y * scale_ref[:, cols] * jax.nn.silu(g_ref[:, cols])).astype(o_ref.dtype)


def _pool_mix(u, w_pool, pool_scale, g):
    s, width = u.shape
    tm = 512
    hb = tm // POOL_HALO
    nhalo = s // POOL_HALO
    return pl.pallas_call(
        functools.partial(_pool_kernel, tm=tm, seq=s),
        grid=(s // tm,),
        in_specs=[pl.BlockSpec((POOL_HALO, width), lambda i: (jnp.maximum(i * hb - 1, 0), 0)),
                  pl.BlockSpec((tm, width), lambda i: (i, 0)),
                  pl.BlockSpec((POOL_HALO, width), lambda i: (jnp.minimum((i + 1) * hb, nhalo - 1), 0)),
                  pl.BlockSpec(w_pool.shape, lambda i: (0, 0, 0)),
                  pl.BlockSpec((1, width), lambda i: (0, 0)),
                  pl.BlockSpec((tm, width), lambda i: (i, 0))],
        out_specs=pl.BlockSpec((tm, width), lambda i: (i, 0)),
        out_shape=jax.ShapeDtypeStruct((s, width), BF16),
        scratch_shapes=[pltpu.VMEM((tm + 2 * POOL_HALO, width), F32)],
        compiler_params=_params(1),
        name="pool_mix",
    )(u, u, u, w_pool, pool_scale, g)


def _out_kernel(*refs, widths):
    n = len(widths)
    y_refs = refs[:n]
    w_ref, x_ref, gate_ref, o_ref = refs[n:n + 4]
    if len(refs) > n + 4:
        yb_ref = refs[n + 4]

        @pl.when(pl.program_id(1) == 0)
        def _():
            yb_ref[...] = y_refs[0][...].astype(BF16)

        y_refs = [yb_ref]
    acc = None
    off = 0
    for y_ref, wd in zip(y_refs, widths):
        part = jnp.dot(y_ref[...], w_ref[off:off + wd, :], preferred_element_type=F32)
        acc = part if acc is None else acc + part
        off += wd
    o_ref[...] = x_ref[...] + gate_ref[...] * acc


def _out_proj(ys, w, x, gate):
    s, d = x.shape
    k = w.shape[0]
    tm, tn = min(1024, s), 512
    widths = tuple(y.shape[1] for y in ys)
    needs_cast = ys[0].dtype != BF16
    assert not needs_cast or len(ys) == 1
    return pl.pallas_call(
        functools.partial(_out_kernel, widths=widths),
        grid=(s // tm, d // tn),
        in_specs=[pl.BlockSpec((tm, wd), lambda i, j: (i, 0)) for wd in widths]
        + [pl.BlockSpec((k, tn), lambda i, j: (0, j)),
           pl.BlockSpec((tm, tn), lambda i, j: (i, j)),
           pl.BlockSpec((1, tn), lambda i, j: (0, j))],
        out_specs=pl.BlockSpec((tm, tn), lambda i, j: (i, j)),
        out_shape=jax.ShapeDtypeStruct((s, d), F32),
        scratch_shapes=[pltpu.VMEM((tm, k), BF16)] if needs_cast else [],
        compiler_params=_params(2),
        name="out_proj",
    )(*ys, w, x, gate)


def _dft_split(s):
    n1 = 1 << (int(math.log2(s)) // 2)
    n2 = s // n1
    assert n1 * n2 == s and n1 % SUBLANES == 0 and n2 % SUBLANES == 0
    return n1, n2


def _angles(num, den):
    ang = (2.0 * math.pi / den) * (num % den).astype(F32)
    return jnp.cos(ang), jnp.sin(ang)


def _fold_cs_kernel(cs_ref, wf_ref, o_ref):
    wf = wf_ref[...]
    o_ref[:, 0:FOURIER_GROUP] = jnp.dot(cs_ref[0], wf, preferred_element_type=F32,
                                        precision=lax.Precision.HIGHEST)
    o_ref[:, FOURIER_GROUP:] = jnp.dot(cs_ref[1], wf, preferred_element_type=F32,
                                       precision=lax.Precision.HIGHEST)


def _fold_channel_dft(w_fourier_j, seq):
    fg = FOURIER_GROUP
    idx = jnp.arange(fg, dtype=jnp.int32)
    cc, sc = _angles(idx[:, None] * idx[None, :], fg)
    cs = jnp.stack([cc, sc]) * (1.0 / math.sqrt(seq * fg))
    return pl.pallas_call(
        _fold_cs_kernel,
        grid=(FOURIER_GROUPS,),
        in_specs=[pl.BlockSpec((2, fg, fg), lambda g: (0, 0, 0)),
                  pl.BlockSpec((None, fg, fg), lambda g: (g, 0, 0))],
        out_specs=pl.BlockSpec((None, fg, 2 * fg), lambda g: (g, 0, 0)),
        out_shape=jax.ShapeDtypeStruct((FOURIER_GROUPS, fg, 2 * fg), F32),
        compiler_params=_params(1),
        name="fold_channel_dft",
    )(cs, w_fourier_j)


def _fold_win_kernel(w_ref, ab_ref, p_ref, q_ref):
    r = jnp.dot(w_ref[...], ab_ref[...].astype(BF16), preferred_element_type=F32)
    p_ref[...] = r[:, 0:FOURIER_GROUP].astype(BF16)
    q_ref[...] = r[:, FOURIER_GROUP:].astype(BF16)


def _fold_in_proj(w_in_u, ab):
    d = w_in_u.shape[0]
    fg = FOURIER_GROUP
    ospec = pl.BlockSpec((d, fg), lambda g: (0, g))
    return pl.pallas_call(
        _fold_win_kernel,
        grid=(FOURIER_GROUPS,),
        in_specs=[pl.BlockSpec((d, fg), lambda g: (0, g)),
                  pl.BlockSpec((None, fg, 2 * fg), lambda g: (g, 0, 0))],
        out_specs=[ospec, ospec],
        out_shape=[jax.ShapeDtypeStruct((d, D_MODEL), BF16)] * 2,
        compiler_params=_params(1),
        name="fold_in_proj",
    )(w_in_u, ab)


def _dft1_kernel(f_ref, p_ref, q_ref, t_ref, *, nb, n1):
    f = f_ref[...]
    for jj in range(nb):
        rows = slice(jj * n1, (jj + 1) * n1)
        rhs = jnp.concatenate([p_ref[rows, :], q_ref[rows, :]], axis=0)
        t_ref[jj] = jnp.dot(f, rhs, preferred_element_type=F32).astype(BF16)


def _dft_stage1(pq_t, f1, n1, n2):
    s, c2 = pq_t.shape
    c = c2 // 2
    nb = SUBLANES
    cb = min(1024, c)
    ncb = c // cb
    return pl.pallas_call(
        functools.partial(_dft1_kernel, nb=nb, n1=n1),
        grid=(n2 // nb, ncb),
        in_specs=[pl.BlockSpec((2 * n1, 2 * n1), lambda b, j: (0, 0)),
                  pl.BlockSpec((nb * n1, cb), lambda b, j: (b, j)),
                  pl.BlockSpec((nb * n1, cb), lambda b, j: (b, j + ncb))],
        out_specs=pl.BlockSpec((nb, 2 * n1, cb), lambda b, j: (b, 0, j)),
        out_shape=jax.ShapeDtypeStruct((n2, 2 * n1, c), BF16),
        compiler_params=_params(2),
        name="dft_stage1",
    )(f1, pq_t, pq_t)


def _dft2_kernel(l_ref, t_ref, g_ref, o_ref, *, n2, cb):
    rhs = t_ref[...].reshape(SUBLANES * 2 * n2, cb)
    y = jnp.dot(l_ref[0], rhs, preferred_element_type=F32)
    o_ref[...] = y.reshape(n2, SUBLANES, cb) * jax.nn.silu(g_ref[...])


def _dft_stage2(t_kn, l2, g, n1, n2):
    c = t_kn.shape[-1]
    cb = min(1024, c)
    row_blocks = pl.BlockSpec((n2, SUBLANES, cb), lambda g, j: (0, g, j))
    out = pl.pallas_call(
        functools.partial(_dft2_kernel, n2=n2, cb=cb),
        grid=(n1 // SUBLANES, c // cb),
        in_specs=[pl.BlockSpec((1, SUBLANES * n2, SUBLANES * 2 * n2), lambda g, j: (g, 0, 0)),
                  pl.BlockSpec((SUBLANES, 2, n2, cb), lambda g, j: (g, 0, 0, j)),
                  row_blocks],
        out_specs=row_blocks,
        out_shape=jax.ShapeDtypeStruct((n2, n1, c), F32),
        compiler_params=_params(2),
        name="dft_stage2",
    )(l2, t_kn, g.reshape(n2, n1, c))
    return out.reshape(n1 * n2, c)


def _dft_matrices(s):
    n1, n2 = _dft_split(s)
    a = jnp.arange(n1, dtype=jnp.int32)
    c1, s1 = _angles(a[:, None] * a[None, :], n1)
    f1 = jnp.concatenate([jnp.concatenate([c1, -s1], axis=1),
                          jnp.concatenate([-s1, -c1], axis=1)], axis=0).astype(BF16)
    groups = n1 // SUBLANES
    g = jnp.arange(groups, dtype=jnp.int32)[:, None, None, None]
    k2 = jnp.arange(n2, dtype=jnp.int32)[None, :, None, None]
    j = jnp.arange(SUBLANES, dtype=jnp.int32)[None, None, :, None]
    nn = jnp.arange(n2, dtype=jnp.int32)[None, None, None, :]
    c2, s2 = _angles((SUBLANES * g + j + n1 * k2) * nn, s)
    cs = jnp.stack([c2, s2], axis=3).astype(BF16)
    eye = jnp.eye(SUBLANES, dtype=BF16)
    l2 = cs[:, :, :, None] * eye[None, None, :, :, None, None]
    return n1, n2, f1, l2.reshape(groups, n2 * SUBLANES, SUBLANES * 2 * n2)


def kernel(x, c, norm_w, ada_w, ada_b, w_in_ab, w_pool, pool_scale, q_norm_w, k_norm_w, lambda_q1, lambda_k1,
           lambda_q2, lambda_k2, subln_w, w_out_ab, w_in_c, w_fourier, w_out_c):
    batch, s, d = x.shape
    assert batch == 1 and d == D_MODEL and s % ATT_TK == 0
    xs = x.reshape(s, d)
    mod = _ada_mod(c, ada_w, ada_b)
    n1, n2, f1, l2 = _dft_matrices(s)
    row = lambda v: v.reshape(1, -1)

    for i in range(DEPTH):
        shift, scale, gate = mod[i, :, 0:d], mod[i, :, d:2 * d], mod[i, :, 2 * d:3 * d]
        h = _modulate(xs, row(norm_w[i]), shift, scale)
        j = i // 2
        if i % 2 == 0:
            w_in = w_in_ab[j].astype(BF16)
            o1, o2, o3, o4 = POOL_WIDTH, POOL_WIDTH + DIFF_WIDTH, POOL_WIDTH + 2 * DIFF_WIDTH, POOL_WIDTH + 3 * DIFF_WIDTH
            lambda_init = 0.8 - 0.6 * math.exp(-0.3 * i)
            u_pool = _matmul(h, w_in[:, :o1], F32, 1024, 512, name="pool_in_proj")
            gte = _matmul(h, w_in[:, o4:], F32, 1024, 512, name="gate_in_proj")
            qt, ka, vt = _qkv_proj(h, w_in[:, o1:o2], w_in[:, o2:o3], w_in[:, o3:o4],
                                   row(q_norm_w[j]), row(k_norm_w[j]))
            y_a = _pool_mix(u_pool, w_pool[j].astype(BF16), row(pool_scale[j]), gte)
            y_b = _diff_attn(qt, ka, vt, row(lambda_q1[j]), row(lambda_k1[j]), row(lambda_q2[j]),
                             row(lambda_k2[j]), subln_w[j], gte, lambda_init)
            xs = _out_proj([y_a, y_b], w_out_ab[j].astype(BF16), xs, gate)
        else:
            w_in = w_in_c[j].astype(BF16)
            ab = _fold_channel_dft(w_fourier[j], s)
            w_p, w_q = _fold_in_proj(w_in[:, :d], ab)
            gte = _matmul(h, w_in[:, d:], F32, 1024, 512, name="gate_in_proj")
            h_t = h.reshape(n1, n2, d).transpose(1, 0, 2).reshape(s, d)
            pq_t = _matmul(h_t, jnp.concatenate([w_p, w_q], axis=1), BF16, 1024, 512, name="fourier_in_proj")
            t_nk = _dft_stage1(pq_t, f1, n1, n2)
            t_kn = t_nk.reshape(n2, 2, n1, d).transpose(2, 1, 0, 3)
            y = _dft_stage2(t_kn, l2, gte, n1, n2)
            xs = _out_proj([y], w_out_c[j].astype(BF16), xs, gate)
    return xs.reshape(batch, s, d)
```

```python
import functools
import math

import jax
import jax.numpy as jnp
from jax import lax
from jax.experimental import pallas as pl
from jax.experimental.pallas import tpu as pltpu

F32 = jnp.float32
BF16 = jnp.bfloat16

D_MODEL = 2048
DEPTH = 4
NORM_EPS = 1e-6

POOL_WINDOWS = (2, 4, 8, 16)
POOL_WIDTH = D_MODEL // 2
POOL_GROUP = POOL_WIDTH // len(POOL_WINDOWS)
POOL_HALO = 8

HEADS = 8
HEAD_DIM = 64
HEAD_V = 2 * HEAD_DIM
DIFF_WIDTH = HEADS * HEAD_V
AB_WIDTH = POOL_WIDTH + DIFF_WIDTH

FOURIER_GROUPS = 4
FOURIER_GROUP = D_MODEL // FOURIER_GROUPS

LANES = 128
SUBLANES = 8
BF16_ROWS = 16
POS_RADIX = 256
COEF_PARTS = 3
LOG2E = math.log2(math.e)

ATT_TQ = 512
ATT_TK = 512
ATT_DIAG = ATT_TQ // ATT_TK
ATT_UNROLL = 8
V_ROWS = HEAD_V + BF16_ROWS
REF_MARGIN = 60.0
REF_FLOOR = 2.0 ** -60
BOUND_SLACK = 1.001

VMEM_LIMIT = 56 * 1024 * 1024


def _params(n_axes):
    return pltpu.CompilerParams(dimension_semantics=("arbitrary",) * n_axes,
                                vmem_limit_bytes=VMEM_LIMIT)


def _mod_kernel(c_ref, w_ref, b_ref, o_ref):
    c = c_ref[...]
    o_ref[...] = jnp.sum(jax.nn.silu(c) * w_ref[...], axis=0, keepdims=True) + b_ref[...]


def _ada_mod(c, ada_w, ada_b):
    depth, d, n = ada_w.shape
    tn = 512
    return pl.pallas_call(
        _mod_kernel,
        grid=(depth, n // tn),
        in_specs=[pl.BlockSpec((d, 1), lambda i, j: (0, 0)),
                  pl.BlockSpec((None, d, tn), lambda i, j: (i, 0, j)),
                  pl.BlockSpec((None, 1, tn), lambda i, j: (i, 0, j))],
        out_specs=pl.BlockSpec((None, 1, tn), lambda i, j: (i, 0, j)),
        out_shape=jax.ShapeDtypeStruct((depth, 1, n), F32),
        compiler_params=_params(2),
        name="ada_mod",
    )(c.reshape(d, 1), ada_w, ada_b.reshape(depth, 1, n))


def _modulate_kernel(x_ref, nw_ref, shift_ref, scale_ref, o_ref):
    xf = x_ref[...]
    r = lax.rsqrt(jnp.mean(xf * xf, axis=-1, keepdims=True) + NORM_EPS)
    y = xf * r * nw_ref[...]
    y = y * (1.0 + scale_ref[...]) + shift_ref[...]
    o_ref[...] = y.astype(o_ref.dtype)


def _modulate(x, nw, shift, scale):
    s, d = x.shape
    tm = 512
    vec = pl.BlockSpec((1, d), lambda i: (0, 0))
    return pl.pallas_call(
        _modulate_kernel,
        grid=(s // tm,),
        in_specs=[pl.BlockSpec((tm, d), lambda i: (i, 0)), vec, vec, vec],
        out_specs=pl.BlockSpec((tm, d), lambda i: (i, 0)),
        out_shape=jax.ShapeDtypeStruct((s, d), BF16),
        compiler_params=_params(1),
        name="modulate",
    )(x, nw, shift, scale)


def _mm_kernel(a_ref, w_ref, o_ref, *, precision):
    o_ref[...] = jnp.dot(a_ref[...], w_ref[...], preferred_element_type=F32,
                         precision=precision).astype(o_ref.dtype)


def _matmul(a, w, out_dtype, tm, tn, precision=None, name="matmul"):
    m, k = a.shape
    n = w.shape[1]
    tm, tn = min(tm, m), min(tn, n)
    return pl.pallas_call(
        functools.partial(_mm_kernel, precision=precision),
        grid=(m // tm, n // tn),
        in_specs=[pl.BlockSpec((tm, k), lambda i, j: (i, 0)),
                  pl.BlockSpec((k, tn), lambda i, j: (0, j))],
        out_specs=pl.BlockSpec((tm, tn), lambda i, j: (i, j)),
        out_shape=jax.ShapeDtypeStruct((m, n), out_dtype),
        compiler_params=_params(2),
        name=name,
    )(a, w)


def _qkv_kernel(h_ref, wq_ref, wk_ref, wv_ref, qnw_ref, knw_ref, qt_ref, ka_ref, vt_ref, *, tm):
    i = pl.program_id(0)
    h = h_ref[...]
    lane = lax.broadcasted_iota(jnp.int32, (tm, LANES), 1)
    first = lane < HEAD_DIM
    pos = lax.broadcasted_iota(jnp.int32, (tm, LANES), 0) + i * tm
    pos_lo = (pos % POS_RADIX).astype(F32)
    pos_hi = ((pos % ATT_TK) // POS_RADIX).astype(F32)
    pos_cols = jnp.where(lane < HEAD_DIM + COEF_PARTS, pos_lo, jnp.where(lane < HEAD_DIM + 2 * COEF_PARTS, pos_hi, 0.0))
    ones_rows = (lax.broadcasted_iota(jnp.int32, (BF16_ROWS, ATT_TK), 0) == 0).astype(BF16)

    def halves_rms(z, w):
        sq = z * z
        ss1 = jnp.sum(jnp.where(first, sq, 0.0), axis=-1, keepdims=True)
        ss2 = jnp.sum(jnp.where(first, 0.0, sq), axis=-1, keepdims=True)
        r = jnp.where(first, lax.rsqrt(ss1 / HEAD_DIM + NORM_EPS), lax.rsqrt(ss2 / HEAD_DIM + NORM_EPS))
        return z * r * w

    zq = jnp.dot(h, wq_ref[...], preferred_element_type=F32)
    zk = jnp.dot(h, wk_ref[...], preferred_element_type=F32)
    zv = jnp.dot(h, wv_ref[...], preferred_element_type=F32)
    for hh in range(2):
        cols = slice(hh * LANES, (hh + 1) * LANES)
        qn = halves_rms(zq[:, cols], qnw_ref[...]) * (HEAD_DIM ** -0.5) * LOG2E
        kn = halves_rms(zk[:, cols], knw_ref[...])
        for m in range(2):
            qm = qn if m == 0 else pltpu.roll(qn, HEAD_DIM, axis=1)
            km = kn if m == 0 else pltpu.roll(kn, HEAD_DIM, axis=1)
            qt_ref[hh, m] = jnp.where(first, qm, 0.0).T.astype(BF16)
            ka = jnp.where(first, km, pos_cols).astype(BF16)
            for cc in range(tm // ATT_TK):
                ka_ref[hh, m, cc] = ka[cc * ATT_TK:(cc + 1) * ATT_TK]
        v = zv[:, cols]
        for cc in range(tm // ATT_TK):
            vt_ref[hh, cc, 0:HEAD_V, :] = v[cc * ATT_TK:(cc + 1) * ATT_TK].T.astype(BF16)
            vt_ref[hh, cc, HEAD_V:V_ROWS, :] = ones_rows


def _qkv_proj(h, wq, wk, wv, qnw, knw):
    s, d = h.shape
    tm = min(1024, s)
    nchunk = s // ATT_TK
    cpt = tm // ATT_TK
    tn = 2 * LANES
    wspec = pl.BlockSpec((d, tn), lambda i, j: (0, j))
    vec = pl.BlockSpec((1, LANES), lambda i, j: (0, 0))
    return pl.pallas_call(
        functools.partial(_qkv_kernel, tm=tm),
        grid=(s // tm, HEADS // 2),
        in_specs=[pl.BlockSpec((tm, d), lambda i, j: (i, 0)), wspec, wspec, wspec, vec, vec],
        out_specs=[pl.BlockSpec((2, 2, LANES, tm), lambda i, j: (j, 0, 0, i)),
                   pl.BlockSpec((2, 2, cpt, ATT_TK, LANES), lambda i, j: (j, 0, i, 0, 0)),
                   pl.BlockSpec((2, cpt, V_ROWS, ATT_TK), lambda i, j: (j, i, 0, 0))],
        out_shape=[jax.ShapeDtypeStruct((HEADS, 2, LANES, s), BF16),
                   jax.ShapeDtypeStruct((HEADS, 2, nchunk, ATT_TK, LANES), BF16),
                   jax.ShapeDtypeStruct((HEADS, nchunk, V_ROWS, ATT_TK), BF16)],
        compiler_params=_params(2),
        name="qkv_proj",
    )(h, wq, wk, wv, jnp.tile(qnw, (1, 2)), jnp.tile(knw, (1, 2)))


def _attn_kernel(slopes_ref, qt_ref, ka_ref, vt_ref, lq1_ref, lk1_ref, lq2_ref, lk2_ref, subw_ref, g_ref,
                 o_ref, acc_ref, mu_ref, qv_ref, kmax_ref, p_a, p_b, *, lambda_init, nchunk, unroll):
    hd = pl.program_id(0)
    i = pl.program_id(1)
    base = hd * (1 + COEF_PARTS)
    slope = slopes_ref[base]
    rowq = lax.broadcasted_iota(jnp.int32, (LANES, ATT_TQ), 0)
    q_off = lax.broadcasted_iota(jnp.int32, (1, ATT_TQ), 1).astype(F32)

    row1 = lax.broadcasted_iota(jnp.int32, (LANES, 1), 0)
    coef_col = jnp.zeros((LANES, 1), F32)
    for t in range(COEF_PARTS):
        piece = slopes_ref[base + 1 + t]
        coef_col = jnp.where(row1 == HEAD_DIM + t, piece, coef_col)
        coef_col = jnp.where(row1 == HEAD_DIM + COEF_PARTS + t, piece * POS_RADIX, coef_col)
    for m in range(2):
        qf = qt_ref[0, m].astype(F32)
        for kind, sign in ((0, -1.0), (1, 0.0), (2, 1.0)):
            qv_ref[kind, m] = jnp.where(rowq >= HEAD_DIM, sign * coef_col, qf).astype(BF16)

    @pl.when(i == 0)
    def _():
        lane = lax.broadcasted_iota(jnp.int32, (ATT_TK, LANES), 1)
        for m in range(2):
            def widest(c, best):
                kc = ka_ref[0, m, c].astype(F32)
                return jnp.maximum(best, jnp.sum(jnp.where(lane < HEAD_DIM, kc * kc, 0.0), axis=1, keepdims=True))
            kmax_ref[m] = jnp.max(lax.fori_loop(0, nchunk, widest, jnp.zeros((ATT_TK, 1), F32)))

    for m in range(2):
        qf = qt_ref[0, m].astype(F32)
        bound = jnp.sqrt(jnp.sum(qf * qf, axis=0, keepdims=True) * kmax_ref[m]) * BOUND_SLACK
        mu_ref[m] = bound - REF_MARGIN

    c_diag = i * ATT_DIAG

    def chunk_of(e):
        if isinstance(e, int) and e < ATT_DIAG:
            return c_diag + e
        rest = e - ATT_DIAG
        return jnp.where(e < ATT_DIAG, c_diag + e, rest + ATT_DIAG * (rest >= c_diag).astype(jnp.int32))

    def side_of(c):
        return (c < c_diag).astype(jnp.int32) - (c >= c_diag + ATT_DIAG).astype(jnp.int32)

    p_bufs = (p_a, p_b)

    def probs(e, p_ref):
        c = chunk_of(e)
        side = side_of(c)
        sidef = side.astype(F32)
        gap = jnp.abs(c * ATT_TK - i * ATT_TQ).astype(F32)
        cvec = -slope * (sidef * sidef * gap + sidef * q_off)
        for m in range(2):
            st = jnp.dot(ka_ref[0, m, c], qv_ref[1 + side, m], preferred_element_type=F32)
            if isinstance(e, int) and e < ATT_DIAG:
                key_off = lax.broadcasted_iota(jnp.int32, (ATT_TK, ATT_TQ), 0) + e * ATT_TK
                qry_off = lax.broadcasted_iota(jnp.int32, (ATT_TK, ATT_TQ), 1)
                st = st - slope * jnp.abs(key_off - qry_off).astype(F32)
            p_ref[m] = jnp.exp2(st - (mu_ref[m] - cvec)).astype(BF16)

    def group(e0, count, probs_last):
        sums = [None, None]
        for u in range(count):
            if u < count - 1 or probs_last:
                probs(e0 + u + 1, p_bufs[(u + 1) % 2])
            c = chunk_of(e0 + u)
            for m in range(2):
                pv = jnp.dot(vt_ref[0, c], p_bufs[u % 2][m], preferred_element_type=F32)
                sums[m] = pv if sums[m] is None else sums[m] + pv
        for m in range(2):
            acc_ref[m] += sums[m]

    def loop_body(it, carry):
        group(it * unroll, unroll, True)
        return carry

    def one_pass(_):
        acc_ref[...] = jnp.zeros(acc_ref.shape, F32)
        n_groups = nchunk // unroll
        probs(0, p_a)
        group(0, unroll, True)
        lax.fori_loop(1, n_groups - 1, loop_body, 0)
        group((n_groups - 1) * unroll, unroll, False)
        lowest = None
        for m in range(2):
            den = acc_ref[m, HEAD_V:HEAD_V + 1, :]
            mu_ref[m] = jnp.where(den < REF_FLOOR, mu_ref[m] - 2.0 * REF_MARGIN, mu_ref[m])
            lowest = jnp.min(den) if lowest is None else jnp.minimum(lowest, jnp.min(den))
        return (lowest < REF_FLOOR).astype(jnp.int32)

    lax.while_loop(lambda retry: retry > 0, one_pass, jnp.int32(1))

    lam = (jnp.exp(jnp.sum(lq1_ref[...] * lk1_ref[...], axis=-1, keepdims=True))
           - jnp.exp(jnp.sum(lq2_ref[...] * lk2_ref[...], axis=-1, keepdims=True)) + lambda_init)
    a1 = acc_ref[0]
    a2 = acc_ref[1]
    o = a1[0:HEAD_V] / a1[HEAD_V:HEAD_V + 1] - lam * (a2[0:HEAD_V] / a2[HEAD_V:HEAD_V + 1])
    r = lax.rsqrt(jnp.mean(o * o, axis=0, keepdims=True) + NORM_EPS)
    y = (o * r * subw_ref[...]) * (1.0 - lambda_init)
    o_ref[...] = (y.T * jax.nn.silu(g_ref[...])).astype(o_ref.dtype)


def _diff_attn(qt, ka, vt, lq1, lk1, lq2, lk2, subw, g, lambda_init):
    s = qt.shape[-1]
    g_col0 = (g.shape[1] - DIFF_WIDTH) // HEAD_V
    nchunk = s // ATT_TK
    unroll = min(ATT_UNROLL, nchunk // 2)
    assert unroll % 2 == 0 and nchunk % unroll == 0 and unroll >= ATT_DIAG and s % ATT_TQ == 0
    whole = jnp.asarray([LOG2E * 2.0 ** (-8.0 * (h + 1) / HEADS) for h in range(HEADS)], F32)
    pieces, rest = [], whole
    for _ in range(COEF_PARTS):
        pieces.append(rest.astype(BF16).astype(F32))
        rest = rest - pieces[-1]
    slopes = jnp.stack([whole] + pieces, axis=1).reshape(-1)
    vec = pl.BlockSpec((1, HEAD_DIM), lambda h, i: (0, 0))
    return pl.pallas_call(
        functools.partial(_attn_kernel, lambda_init=lambda_init, nchunk=nchunk, unroll=unroll),
        grid=(HEADS, s // ATT_TQ),
        in_specs=[pl.BlockSpec(memory_space=pltpu.SMEM),
                  pl.BlockSpec((1, 2, LANES, ATT_TQ), lambda h, i: (h, 0, 0, i)),
                  pl.BlockSpec((1, 2, nchunk, ATT_TK, LANES), lambda h, i: (h, 0, 0, 0, 0)),
                  pl.BlockSpec((1, nchunk, V_ROWS, ATT_TK), lambda h, i: (h, 0, 0, 0)),
                  vec, vec, vec, vec,
                  pl.BlockSpec((HEAD_V, 1), lambda h, i: (0, 0)),
                  pl.BlockSpec((ATT_TQ, HEAD_V), lambda h, i: (i, g_col0 + h))],
        out_specs=pl.BlockSpec((ATT_TQ, HEAD_V), lambda h, i: (i, h)),
        out_shape=jax.ShapeDtypeStruct((s, DIFF_WIDTH), BF16),
        scratch_shapes=[pltpu.VMEM((2, V_ROWS, ATT_TQ), F32), pltpu.VMEM((2, 1, ATT_TQ), F32),
                        pltpu.VMEM((3, 2, LANES, ATT_TQ), BF16), pltpu.SMEM((2,), F32),
                        pltpu.VMEM((2, ATT_TK, ATT_TQ), BF16), pltpu.VMEM((2, ATT_TK, ATT_TQ), BF16)],
        compiler_params=_params(2),
        name="diff_attn",
    )(slopes, qt, ka, vt, lq1, lk1, lq2, lk2, subw.reshape(HEAD_V, 1), g)


def _pool_kernel(prev_ref, cur_ref, next_ref, w_ref, scale_ref, g_ref, o_ref, ext_ref, *, tm, seq):
    i = pl.program_id(0)
    last = pl.num_programs(0) - 1
    zero_halo = jnp.zeros((POOL_HALO, POOL_WIDTH), F32)
    ext_ref[0:POOL_HALO] = jnp.where(i > 0, prev_ref[...], zero_halo)
    ext_ref[POOL_HALO:POOL_HALO + tm] = cur_ref[...]
    ext_ref[POOL_HALO + tm:POOL_HALO + tm + POOL_HALO] = jnp.where(i < last, next_ref[...], zero_halo)
    t = lax.broadcasted_iota(jnp.int32, (tm, 1), 0) + i * tm
    for g, w in enumerate(POOL_WINDOWS):
        cols = pl.ds(g * POOL_GROUP, POOL_GROUP)
        win = ext_ref[pl.ds(POOL_HALO - w // 2, tm), cols]
        for d in range(-(w // 2) + 1, w - w // 2):
            win = win + ext_ref[pl.ds(POOL_HALO + d, tm), cols]
        lo = jnp.maximum(t - w // 2, 0)
        hi = jnp.minimum(t + (w - w // 2) - 1, seq - 1)
        cnt = (hi - lo + 1).astype(F32)
        pooled = win / cnt - cur_ref[:, cols]
        y = jnp.dot(pooled.astype(BF16), w_ref[g], preferred_element_type=F32)
        o_ref[:, cols] = (y * scale_ref[:, cols] * jax.nn.silu(g_ref[:, cols])).astype(o_ref.dtype)


def _pool_mix(u, w_pool, pool_scale, g):
    s, width = u.shape
    tm = 512
    hb = tm // POOL_HALO
    nhalo = s // POOL_HALO
    return pl.pallas_call(
        functools.partial(_pool_kernel, tm=tm, seq=s),
        grid=(s // tm,),
        in_specs=[pl.BlockSpec((POOL_HALO, width), lambda i: (jnp.maximum(i * hb - 1, 0), 0)),
                  pl.BlockSpec((tm, width), lambda i: (i, 0)),
                  pl.BlockSpec((POOL_HALO, width), lambda i: (jnp.minimum((i + 1) * hb, nhalo - 1), 0)),
                  pl.BlockSpec(w_pool.shape, lambda i: (0, 0, 0)),
                  pl.BlockSpec((1, width), lambda i: (0, 0)),
                  pl.BlockSpec((tm, width), lambda i: (i, 0))],
        out_specs=pl.BlockSpec((tm, width), lambda i: (i, 0)),
        out_shape=jax.ShapeDtypeStruct((s, width), BF16),
        scratch_shapes=[pltpu.VMEM((tm + 2 * POOL_HALO, width), F32)],
        compiler_params=_params(1),
        name="pool_mix",
    )(u, u, u, w_pool, pool_scale, g)


def _out_kernel(*refs, widths):
    n = len(widths)
    y_refs = refs[:n]
    w_ref, x_ref, gate_ref, o_ref = refs[n:n + 4]
    if len(refs) > n + 4:
        yb_ref = refs[n + 4]

        @pl.when(pl.program_id(1) == 0)
        def _():
            yb_ref[...] = y_refs[0][...].astype(BF16)

        y_refs = [yb_ref]
    acc = None
    off = 0
    for y_ref, wd in zip(y_refs, widths):
        part = jnp.dot(y_ref[...], w_ref[off:off + wd, :], preferred_element_type=F32)
        acc = part if acc is None else acc + part
        off += wd
    o_ref[...] = x_ref[...] + gate_ref[...] * acc


def _out_proj(ys, w, x, gate):
    s, d = x.shape
    k = w.shape[0]
    tm, tn = min(1024, s), 512
    widths = tuple(y.shape[1] for y in ys)
    needs_cast = ys[0].dtype != BF16
    assert not needs_cast or len(ys) == 1
    return pl.pallas_call(
        functools.partial(_out_kernel, widths=widths),
        grid=(s // tm, d // tn),
        in_specs=[pl.BlockSpec((tm, wd), lambda i, j: (i, 0)) for wd in widths]
        + [pl.BlockSpec((k, tn), lambda i, j: (0, j)),
           pl.BlockSpec((tm, tn), lambda i, j: (i, j)),
           pl.BlockSpec((1, tn), lambda i, j: (0, j))],
        out_specs=pl.BlockSpec((tm, tn), lambda i, j: (i, j)),
        out_shape=jax.ShapeDtypeStruct((s, d), F32),
        scratch_shapes=[pltpu.VMEM((tm, k), BF16)] if needs_cast else [],
        compiler_params=_params(2),
        name="out_proj",
    )(*ys, w, x, gate)


def _dft_split(s):
    n1 = 1 << (int(math.log2(s)) // 2)
    n2 = s // n1
    assert n1 * n2 == s and n1 % SUBLANES == 0 and n2 % SUBLANES == 0
    return n1, n2


def _angles(num, den):
    ang = (2.0 * math.pi / den) * (num % den).astype(F32)
    return jnp.cos(ang), jnp.sin(ang)


def _fold_cs_kernel(cs_ref, wf_ref, o_ref):
    wf = wf_ref[...]
    o_ref[:, 0:FOURIER_GROUP] = jnp.dot(cs_ref[0], wf, preferred_element_type=F32,
                                        precision=lax.Precision.HIGHEST)
    o_ref[:, FOURIER_GROUP:] = jnp.dot(cs_ref[1], wf, preferred_element_type=F32,
                                       precision=lax.Precision.HIGHEST)


def _fold_channel_dft(w_fourier_j, seq):
    fg = FOURIER_GROUP
    idx = jnp.arange(fg, dtype=jnp.int32)
    cc, sc = _angles(idx[:, None] * idx[None, :], fg)
    cs = jnp.stack([cc, sc]) * (1.0 / math.sqrt(seq * fg))
    return pl.pallas_call(
        _fold_cs_kernel,
        grid=(FOURIER_GROUPS,),
        in_specs=[pl.BlockSpec((2, fg, fg), lambda g: (0, 0, 0)),
                  pl.BlockSpec((None, fg, fg), lambda g: (g, 0, 0))],
        out_specs=pl.BlockSpec((None, fg, 2 * fg), lambda g: (g, 0, 0)),
        out_shape=jax.ShapeDtypeStruct((FOURIER_GROUPS, fg, 2 * fg), F32),
        compiler_params=_params(1),
        name="fold_channel_dft",
    )(cs, w_fourier_j)


def _fold_win_kernel(w_ref, ab_ref, p_ref, q_ref):
    r = jnp.dot(w_ref[...], ab_ref[...].astype(BF16), preferred_element_type=F32)
    p_ref[...] = r[:, 0:FOURIER_GROUP].astype(BF16)
    q_ref[...] = r[:, FOURIER_GROUP:].astype(BF16)


def _fold_in_proj(w_in_u, ab):
    d = w_in_u.shape[0]
    fg = FOURIER_GROUP
    ospec = pl.BlockSpec((d, fg), lambda g: (0, g))
    return pl.pallas_call(
        _fold_win_kernel,
        grid=(FOURIER_GROUPS,),
        in_specs=[pl.BlockSpec((d, fg), lambda g: (0, g)),
                  pl.BlockSpec((None, fg, 2 * fg), lambda g: (g, 0, 0))],
        out_specs=[ospec, ospec],
        out_shape=[jax.ShapeDtypeStruct((d, D_MODEL), BF16)] * 2,
        compiler_params=_params(1),
        name="fold_in_proj",
    )(w_in_u, ab)


def _dft1_kernel(f_ref, p_ref, q_ref, t_ref, *, nb, n1):
    f = f_ref[...]
    for jj in range(nb):
        rows = slice(jj * n1, (jj + 1) * n1)
        rhs = jnp.concatenate([p_ref[rows, :], q_ref[rows, :]], axis=0)
        t_ref[jj] = jnp.dot(f, rhs, preferred_element_type=F32).astype(BF16)


def _dft_stage1(pq_t, f1, n1, n2):
    s, c2 = pq_t.shape
    c = c2 // 2
    nb = SUBLANES
    cb = min(1024, c)
    ncb = c // cb
    return pl.pallas_call(
        functools.partial(_dft1_kernel, nb=nb, n1=n1),
        grid=(n2 // nb, ncb),
        in_specs=[pl.BlockSpec((2 * n1, 2 * n1), lambda b, j: (0, 0)),
                  pl.BlockSpec((nb * n1, cb), lambda b, j: (b, j)),
                  pl.BlockSpec((nb * n1, cb), lambda b, j: (b, j + ncb))],
        out_specs=pl.BlockSpec((nb, 2 * n1, cb), lambda b, j: (b, 0, j)),
        out_shape=jax.ShapeDtypeStruct((n2, 2 * n1, c), BF16),
        compiler_params=_params(2),
        name="dft_stage1",
    )(f1, pq_t, pq_t)


def _dft2_kernel(l_ref, t_ref, g_ref, o_ref, *, n2, cb):
    rhs = t_ref[...].reshape(SUBLANES * 2 * n2, cb)
    y = jnp.dot(l_ref[0], rhs, preferred_element_type=F32)
    o_ref[...] = y.reshape(n2, SUBLANES, cb) * jax.nn.silu(g_ref[...])


def _dft_stage2(t_kn, l2, g, n1, n2):
    c = t_kn.shape[-1]
    cb = min(1024, c)
    row_blocks = pl.BlockSpec((n2, SUBLANES, cb), lambda g, j: (0, g, j))
    out = pl.pallas_call(
        functools.partial(_dft2_kernel, n2=n2, cb=cb),
        grid=(n1 // SUBLANES, c // cb),
        in_specs=[pl.BlockSpec((1, SUBLANES * n2, SUBLANES * 2 * n2), lambda g, j: (g, 0, 0)),
                  pl.BlockSpec((SUBLANES, 2, n2, cb), lambda g, j: (g, 0, 0, j)),
                  row_blocks],
        out_specs=row_blocks,
        out_shape=jax.ShapeDtypeStruct((n2, n1, c), F32),
        compiler_params=_params(2),
        name="dft_stage2",
    )(l2, t_kn, g.reshape(n2, n1, c))
    return out.reshape(n1 * n2, c)


def _dft_matrices(s):
    n1, n2 = _dft_split(s)
    a = jnp.arange(n1, dtype=jnp.int32)
    c1, s1 = _angles(a[:, None] * a[None, :], n1)
    f1 = jnp.concatenate([jnp.concatenate([c1, -s1], axis=1),
                          jnp.concatenate([-s1, -c1], axis=1)], axis=0).astype(BF16)
    groups = n1 // SUBLANES
    g = jnp.arange(groups, dtype=jnp.int32)[:, None, None, None]
    k2 = jnp.arange(n2, dtype=jnp.int32)[None, :, None, None]
    j = jnp.arange(SUBLANES, dtype=jnp.int32)[None, None, :, None]
    nn = jnp.arange(n2, dtype=jnp.int32)[None, None, None, :]
    c2, s2 = _angles((SUBLANES * g + j + n1 * k2) * nn, s)
    cs = jnp.stack([c2, s2], axis=3).astype(BF16)
    eye = jnp.eye(SUBLANES, dtype=BF16)
    l2 = cs[:, :, :, None] * eye[None, None, :, :, None, None]
    return n1, n2, f1, l2.reshape(groups, n2 * SUBLANES, SUBLANES * 2 * n2)


def kernel(x, c, norm_w, ada_w, ada_b, w_in_ab, w_pool, pool_scale, q_norm_w, k_norm_w, lambda_q1, lambda_k1,
           lambda_q2, lambda_k2, subln_w, w_out_ab, w_in_c, w_fourier, w_out_c):
    batch, s, d = x.shape
    assert batch == 1 and d == D_MODEL and s % ATT_TK == 0
    xs = x.reshape(s, d)
    mod = _ada_mod(c, ada_w, ada_b)
    n1, n2, f1, l2 = _dft_matrices(s)
    row = lambda v: v.reshape(1, -1)

    for i in range(DEPTH):
        shift, scale, gate = mod[i, :, 0:d], mod[i, :, d:2 * d], mod[i, :, 2 * d:3 * d]
        h = _modulate(xs, row(norm_w[i]), shift, scale)
        j = i // 2
        if i % 2 == 0:
            w_in = w_in_ab[j].astype(BF16)
            o1, o2, o3, o4 = POOL_WIDTH, POOL_WIDTH + DIFF_WIDTH, POOL_WIDTH + 2 * DIFF_WIDTH, POOL_WIDTH + 3 * DIFF_WIDTH
            lambda_init = 0.8 - 0.6 * math.exp(-0.3 * i)
            u_pool = _matmul(h, w_in[:, :o1], F32, 1024, 512, name="pool_in_proj")
            gte = _matmul(h, w_in[:, o4:], F32, 1024, 512, name="gate_in_proj")
            qt, ka, vt = _qkv_proj(h, w_in[:, o1:o2], w_in[:, o2:o3], w_in[:, o3:o4],
                                   row(q_norm_w[j]), row(k_norm_w[j]))
            y_a = _pool_mix(u_pool, w_pool[j].astype(BF16), row(pool_scale[j]), gte)
            y_b = _diff_attn(qt, ka, vt, row(lambda_q1[j]), row(lambda_k1[j]), row(lambda_q2[j]),
                             row(lambda_k2[j]), subln_w[j], gte, lambda_init)
            xs = _out_proj([y_a, y_b], w_out_ab[j].astype(BF16), xs, gate)
        else:
            w_in = w_in_c[j].astype(BF16)
            ab = _fold_channel_dft(w_fourier[j], s)
            w_p, w_q = _fold_in_proj(w_in[:, :d], ab)
            gte = _matmul(h, w_in[:, d:], F32, 1024, 512, name="gate_in_proj")
            h_t = h.reshape(n1, n2, d).transpose(1, 0, 2).reshape(s, d)
            pq_t = _matmul(h_t, jnp.concatenate([w_p, w_q], axis=1), BF16, 1024, 512, name="fourier_in_proj")
            t_nk = _dft_stage1(pq_t, f1, n1, n2)
            t_kn = t_nk.reshape(n2, 2, n1, d).transpose(2, 1, 0, 3)
            y = _dft_stage2(t_kn, l2, gte, n1, n2)
            xs = _out_proj([y], w_out_c[j].astype(BF16), xs, gate)
    return xs.reshape(batch, s, d)
```

```python
import functools
import math

import jax
import jax.numpy as jnp
from jax import lax
from jax.experimental import pallas as pl
from jax.experimental.pallas import tpu as pltpu

F32 = jnp.float32
BF16 = jnp.bfloat16

D_MODEL = 2048
DEPTH = 4
NORM_EPS = 1e-6

POOL_WINDOWS = (2, 4, 8, 16)
POOL_WIDTH = D_MODEL // 2
POOL_GROUP = POOL_WIDTH // len(POOL_WINDOWS)
POOL_HALO = 8

HEADS = 8
HEAD_DIM = 64
HEAD_V = 2 * HEAD_DIM
DIFF_WIDTH = HEADS * HEAD_V
AB_WIDTH = POOL_WIDTH + DIFF_WIDTH

FOURIER_GROUPS = 4
FOURIER_GROUP = D_MODEL // FOURIER_GROUPS

LANES = 128
SUBLANES = 8
BF16_ROWS = 16
POS_RADIX = 256
COEF_PARTS = 3
LOG2E = math.log2(math.e)

ATT_TQ = 512
ATT_TK = 512
ATT_DIAG = ATT_TQ // ATT_TK
ATT_UNROLL = 8
V_ROWS = HEAD_V + BF16_ROWS
REF_MARGIN = 60.0
REF_FLOOR = 2.0 ** -60
BOUND_SLACK = 1.001

PROJ_TM, PROJ_TN = 2048, 512

VMEM_LIMIT = 56 * 1024 * 1024


def _params(n_axes):
    return pltpu.CompilerParams(dimension_semantics=("arbitrary",) * n_axes,
                                vmem_limit_bytes=VMEM_LIMIT)


def _mod_kernel(c_ref, w_ref, b_ref, o_ref):
    c = c_ref[...]
    o_ref[...] = jnp.sum(jax.nn.silu(c) * w_ref[...], axis=0, keepdims=True) + b_ref[...]


def _ada_mod(c, ada_w, ada_b):
    depth, d, n = ada_w.shape
    tn = 512
    return pl.pallas_call(
        _mod_kernel,
        grid=(depth, n // tn),
        in_specs=[pl.BlockSpec((d, 1), lambda i, j: (0, 0)),
                  pl.BlockSpec((None, d, tn), lambda i, j: (i, 0, j)),
                  pl.BlockSpec((None, 1, tn), lambda i, j: (i, 0, j))],
        out_specs=pl.BlockSpec((None, 1, tn), lambda i, j: (i, 0, j)),
        out_shape=jax.ShapeDtypeStruct((depth, 1, n), F32),
        compiler_params=_params(2),
        name="ada_mod",
    )(c.reshape(d, 1), ada_w, ada_b.reshape(depth, 1, n))


def _modulate_kernel(x_ref, nw_ref, shift_ref, scale_ref, o_ref):
    xf = x_ref[...]
    r = lax.rsqrt(jnp.mean(xf * xf, axis=-1, keepdims=True) + NORM_EPS)
    y = xf * r * nw_ref[...]
    y = y * (1.0 + scale_ref[...]) + shift_ref[...]
    o_ref[...] = y.astype(o_ref.dtype)


def _modulate(x, nw, shift, scale):
    s, d = x.shape
    tm = 512
    vec = pl.BlockSpec((1, d), lambda i: (0, 0))
    return pl.pallas_call(
        _modulate_kernel,
        grid=(s // tm,),
        in_specs=[pl.BlockSpec((tm, d), lambda i: (i, 0)), vec, vec, vec],
        out_specs=pl.BlockSpec((tm, d), lambda i: (i, 0)),
        out_shape=jax.ShapeDtypeStruct((s, d), BF16),
        compiler_params=_params(1),
        name="modulate",
    )(x, nw, shift, scale)


def _mm_kernel(a_ref, w_ref, o_ref, *, precision):
    o_ref[...] = jnp.dot(a_ref[...], w_ref[...], preferred_element_type=F32,
                         precision=precision).astype(o_ref.dtype)


def _matmul(a, w, out_dtype, tm, tn, precision=None, name="matmul"):
    m, k = a.shape
    n = w.shape[1]
    tm, tn = min(tm, m), min(tn, n)
    return pl.pallas_call(
        functools.partial(_mm_kernel, precision=precision),
        grid=(m // tm, n // tn),
        in_specs=[pl.BlockSpec((tm, k), lambda i, j: (i, 0)),
                  pl.BlockSpec((k, tn), lambda i, j: (0, j))],
        out_specs=pl.BlockSpec((tm, tn), lambda i, j: (i, j)),
        out_shape=jax.ShapeDtypeStruct((m, n), out_dtype),
        compiler_params=_params(2),
        name=name,
    )(a, w)


def _qkv_kernel(h_ref, wq_ref, wk_ref, wv_ref, qnw_ref, knw_ref, qt_ref, ka_ref, vt_ref, *, tm):
    i = pl.program_id(0)
    h = h_ref[...]
    lane = lax.broadcasted_iota(jnp.int32, (tm, LANES), 1)
    first = lane < HEAD_DIM
    pos = lax.broadcasted_iota(jnp.int32, (tm, LANES), 0) + i * tm
    pos_lo = (pos % POS_RADIX).astype(F32)
    pos_hi = ((pos % ATT_TK) // POS_RADIX).astype(F32)
    pos_cols = jnp.where(lane < HEAD_DIM + COEF_PARTS, pos_lo, jnp.where(lane < HEAD_DIM + 2 * COEF_PARTS, pos_hi, 0.0))
    ones_rows = (lax.broadcasted_iota(jnp.int32, (BF16_ROWS, ATT_TK), 0) == 0).astype(BF16)

    def halves_rms(z, w):
        sq = z * z
        ss1 = jnp.sum(jnp.where(first, sq, 0.0), axis=-1, keepdims=True)
        ss2 = jnp.sum(jnp.where(first, 0.0, sq), axis=-1, keepdims=True)
        r = jnp.where(first, lax.rsqrt(ss1 / HEAD_DIM + NORM_EPS), lax.rsqrt(ss2 / HEAD_DIM + NORM_EPS))
        return z * r * w

    zq = jnp.dot(h, wq_ref[...], preferred_element_type=F32)
    zk = jnp.dot(h, wk_ref[...], preferred_element_type=F32)
    zv = jnp.dot(h, wv_ref[...], preferred_element_type=F32)
    for hh in range(2):
        cols = slice(hh * LANES, (hh + 1) * LANES)
        qn = halves_rms(zq[:, cols], qnw_ref[...]) * (HEAD_DIM ** -0.5) * LOG2E
        kn = halves_rms(zk[:, cols], knw_ref[...])
        for m in range(2):
            qm = qn if m == 0 else pltpu.roll(qn, HEAD_DIM, axis=1)
            km = kn if m == 0 else pltpu.roll(kn, HEAD_DIM, axis=1)
            qt_ref[hh, m] = jnp.where(first, qm, 0.0).T.astype(BF16)
            ka = jnp.where(first, km, pos_cols).astype(BF16)
            for cc in range(tm // ATT_TK):
                ka_ref[hh, m, cc] = ka[cc * ATT_TK:(cc + 1) * ATT_TK]
        v = zv[:, cols]
        for cc in range(tm // ATT_TK):
            vt_ref[hh, cc, 0:HEAD_V, :] = v[cc * ATT_TK:(cc + 1) * ATT_TK].T.astype(BF16)
            vt_ref[hh, cc, HEAD_V:V_ROWS, :] = ones_rows


def _qkv_proj(h, wq, wk, wv, qnw, knw):
    s, d = h.shape
    tm = min(1024, s)
    nchunk = s // ATT_TK
    cpt = tm // ATT_TK
    tn = 2 * LANES
    wspec = pl.BlockSpec((d, tn), lambda i, j: (0, j))
    vec = pl.BlockSpec((1, LANES), lambda i, j: (0, 0))
    return pl.pallas_call(
        functools.partial(_qkv_kernel, tm=tm),
        grid=(s // tm, HEADS // 2),
        in_specs=[pl.BlockSpec((tm, d), lambda i, j: (i, 0)), wspec, wspec, wspec, vec, vec],
        out_specs=[pl.BlockSpec((2, 2, LANES, tm), lambda i, j: (j, 0, 0, i)),
                   pl.BlockSpec((2, 2, cpt, ATT_TK, LANES), lambda i, j: (j, 0, i, 0, 0)),
                   pl.BlockSpec((2, cpt, V_ROWS, ATT_TK), lambda i, j: (j, i, 0, 0))],
        out_shape=[jax.ShapeDtypeStruct((HEADS, 2, LANES, s), BF16),
                   jax.ShapeDtypeStruct((HEADS, 2, nchunk, ATT_TK, LANES), BF16),
                   jax.ShapeDtypeStruct((HEADS, nchunk, V_ROWS, ATT_TK), BF16)],
        compiler_params=_params(2),
        name="qkv_proj",
    )(h, wq, wk, wv, jnp.tile(qnw, (1, 2)), jnp.tile(knw, (1, 2)))


def _attn_kernel(slopes_ref, qt_ref, ka_ref, vt_ref, lq1_ref, lk1_ref, lq2_ref, lk2_ref, subw_ref, g_ref,
                 o_ref, acc_ref, mu_ref, qv_ref, kmax_ref, p_a, p_b, *, lambda_init, nchunk, unroll):
    hd = pl.program_id(0)
    i = pl.program_id(1)
    base = hd * (1 + COEF_PARTS)
    slope = slopes_ref[base]
    rowq = lax.broadcasted_iota(jnp.int32, (LANES, ATT_TQ), 0)
    q_off = lax.broadcasted_iota(jnp.int32, (1, ATT_TQ), 1).astype(F32)

    row1 = lax.broadcasted_iota(jnp.int32, (LANES, 1), 0)
    coef_col = jnp.zeros((LANES, 1), F32)
    for t in range(COEF_PARTS):
        piece = slopes_ref[base + 1 + t]
        coef_col = jnp.where(row1 == HEAD_DIM + t, piece, coef_col)
        coef_col = jnp.where(row1 == HEAD_DIM + COEF_PARTS + t, piece * POS_RADIX, coef_col)
    for m in range(2):
        qf = qt_ref[0, m].astype(F32)
        for kind, sign in ((0, -1.0), (1, 0.0), (2, 1.0)):
            qv_ref[kind, m] = jnp.where(rowq >= HEAD_DIM, sign * coef_col, qf).astype(BF16)

    @pl.when(i == 0)
    def _():
        lane = lax.broadcasted_iota(jnp.int32, (ATT_TK, LANES), 1)
        for m in range(2):
            def widest(c, best):
                kc = ka_ref[0, m, c].astype(F32)
                return jnp.maximum(best, jnp.sum(jnp.where(lane < HEAD_DIM, kc * kc, 0.0), axis=1, keepdims=True))
            kmax_ref[m] = jnp.max(lax.fori_loop(0, nchunk, widest, jnp.zeros((ATT_TK, 1), F32)))

    for m in range(2):
        qf = qt_ref[0, m].astype(F32)
        bound = jnp.sqrt(jnp.sum(qf * qf, axis=0, keepdims=True) * kmax_ref[m]) * BOUND_SLACK
        mu_ref[m] = bound - REF_MARGIN

    c_diag = i * ATT_DIAG

    def chunk_of(e):
        if isinstance(e, int) and e < ATT_DIAG:
            return c_diag + e
        rest = e - ATT_DIAG
        return jnp.where(e < ATT_DIAG, c_diag + e, rest + ATT_DIAG * (rest >= c_diag).astype(jnp.int32))

    def side_of(c):
        return (c < c_diag).astype(jnp.int32) - (c >= c_diag + ATT_DIAG).astype(jnp.int32)

    p_bufs = (p_a, p_b)

    def probs(e, p_ref):
        c = chunk_of(e)
        side = side_of(c)
        sidef = side.astype(F32)
        gap = jnp.abs(c * ATT_TK - i * ATT_TQ).astype(F32)
        cvec = -slope * (sidef * sidef * gap + sidef * q_off)
        for m in range(2):
            st = jnp.dot(ka_ref[0, m, c], qv_ref[1 + side, m], preferred_element_type=F32)
            if isinstance(e, int) and e < ATT_DIAG:
                key_off = lax.broadcasted_iota(jnp.int32, (ATT_TK, ATT_TQ), 0) + e * ATT_TK
                qry_off = lax.broadcasted_iota(jnp.int32, (ATT_TK, ATT_TQ), 1)
                st = st - slope * jnp.abs(key_off - qry_off).astype(F32)
            p_ref[m] = jnp.exp2(st - (mu_ref[m] - cvec)).astype(BF16)

    def group(e0, count, probs_last):
        sums = [None, None]
        for u in range(count):
            if u < count - 1 or probs_last:
                probs(e0 + u + 1, p_bufs[(u + 1) % 2])
            c = chunk_of(e0 + u)
            for m in range(2):
                pv = jnp.dot(vt_ref[0, c], p_bufs[u % 2][m], preferred_element_type=F32)
                sums[m] = pv if sums[m] is None else sums[m] + pv
        for m in range(2):
            acc_ref[m] += sums[m]

    def loop_body(it, carry):
        group(it * unroll, unroll, True)
        return carry

    def one_pass(_):
        acc_ref[...] = jnp.zeros(acc_ref.shape, F32)
        n_groups = nchunk // unroll
        probs(0, p_a)
        group(0, unroll, True)
        lax.fori_loop(1, n_groups - 1, loop_body, 0)
        group((n_groups - 1) * unroll, unroll, False)
        lowest = None
        for m in range(2):
            den = acc_ref[m, HEAD_V:HEAD_V + 1, :]
            mu_ref[m] = jnp.where(den < REF_FLOOR, mu_ref[m] - 2.0 * REF_MARGIN, mu_ref[m])
            lowest = jnp.min(den) if lowest is None else jnp.minimum(lowest, jnp.min(den))
        return (lowest < REF_FLOOR).astype(jnp.int32)

    lax.while_loop(lambda retry: retry > 0, one_pass, jnp.int32(1))

    lam = (jnp.exp(jnp.sum(lq1_ref[...] * lk1_ref[...], axis=-1, keepdims=True))
           - jnp.exp(jnp.sum(lq2_ref[...] * lk2_ref[...], axis=-1, keepdims=True)) + lambda_init)
    a1 = acc_ref[0]
    a2 = acc_ref[1]
    o = a1[0:HEAD_V] / a1[HEAD_V:HEAD_V + 1] - lam * (a2[0:HEAD_V] / a2[HEAD_V:HEAD_V + 1])
    r = lax.rsqrt(jnp.mean(o * o, axis=0, keepdims=True) + NORM_EPS)
    y = (o * r * subw_ref[...]) * (1.0 - lambda_init)
    o_ref[...] = (y.T * jax.nn.silu(g_ref[...])).astype(o_ref.dtype)


def _diff_attn(qt, ka, vt, lq1, lk1, lq2, lk2, subw, g, lambda_init):
    s = qt.shape[-1]
    g_col0 = (g.shape[1] - DIFF_WIDTH) // HEAD_V
    nchunk = s // ATT_TK
    unroll = min(ATT_UNROLL, nchunk // 2)
    assert unroll % 2 == 0 and nchunk % unroll == 0 and unroll >= ATT_DIAG and s % ATT_TQ == 0
    whole = jnp.asarray([LOG2E * 2.0 ** (-8.0 * (h + 1) / HEADS) for h in range(HEADS)], F32)
    pieces, rest = [], whole
    for _ in range(COEF_PARTS):
        pieces.append(rest.astype(BF16).astype(F32))
        rest = rest - pieces[-1]
    slopes = jnp.stack([whole] + pieces, axis=1).reshape(-1)
    vec = pl.BlockSpec((1, HEAD_DIM), lambda h, i: (0, 0))
    return pl.pallas_call(
        functools.partial(_attn_kernel, lambda_init=lambda_init, nchunk=nchunk, unroll=unroll),
        grid=(HEADS, s // ATT_TQ),
        in_specs=[pl.BlockSpec(memory_space=pltpu.SMEM),
                  pl.BlockSpec((1, 2, LANES, ATT_TQ), lambda h, i: (h, 0, 0, i)),
                  pl.BlockSpec((1, 2, nchunk, ATT_TK, LANES), lambda h, i: (h, 0, 0, 0, 0)),
                  pl.BlockSpec((1, nchunk, V_ROWS, ATT_TK), lambda h, i: (h, 0, 0, 0)),
                  vec, vec, vec, vec,
                  pl.BlockSpec((HEAD_V, 1), lambda h, i: (0, 0)),
                  pl.BlockSpec((ATT_TQ, HEAD_V), lambda h, i: (i, g_col0 + h))],
        out_specs=pl.BlockSpec((ATT_TQ, HEAD_V), lambda h, i: (i, h)),
        out_shape=jax.ShapeDtypeStruct((s, DIFF_WIDTH), BF16),
        scratch_shapes=[pltpu.VMEM((2, V_ROWS, ATT_TQ), F32), pltpu.VMEM((2, 1, ATT_TQ), F32),
                        pltpu.VMEM((3, 2, LANES, ATT_TQ), BF16), pltpu.SMEM((2,), F32),
                        pltpu.VMEM((2, ATT_TK, ATT_TQ), BF16), pltpu.VMEM((2, ATT_TK, ATT_TQ), BF16)],
        compiler_params=_params(2),
        name="diff_attn",
    )(slopes, qt, ka, vt, lq1, lk1, lq2, lk2, subw.reshape(HEAD_V, 1), g)


def _pool_kernel(prev_ref, cur_ref, next_ref, w_ref, scale_ref, g_ref, o_ref, ext_ref, *, tm, seq):
    i = pl.program_id(0)
    last = pl.num_programs(0) - 1
    zero_halo = jnp.zeros((POOL_HALO, POOL_WIDTH), F32)
    ext_ref[0:POOL_HALO] = jnp.where(i > 0, prev_ref[...], zero_halo)
    ext_ref[POOL_HALO:POOL_HALO + tm] = cur_ref[...]
    ext_ref[POOL_HALO + tm:POOL_HALO + tm + POOL_HALO] = jnp.where(i < last, next_ref[...], zero_halo)
    t = lax.broadcasted_iota(jnp.int32, (tm, 1), 0) + i * tm
    for g, w in enumerate(POOL_WINDOWS):
        cols = pl.ds(g * POOL_GROUP, POOL_GROUP)
        win = ext_ref[pl.ds(POOL_HALO - w // 2, tm), cols]
        for d in range(-(w // 2) + 1, w - w // 2):
            win = win + ext_ref[pl.ds(POOL_HALO + d, tm), cols]
        lo = jnp.maximum(t - w // 2, 0)
        hi = jnp.minimum(t + (w - w // 2) - 1, seq - 1)
        cnt = (hi - lo + 1).astype(F32)
        pooled = win / cnt - cur_ref[:, cols]
        y = jnp.dot(pooled.astype(BF16), w_ref[g], preferred_element_type=F32)
        o_ref[:, cols] = (y * scale_ref[:, cols] * jax.nn.silu(g_ref[:, cols])).astype(o_ref.dtype)


def _pool_mix(u, w_pool, pool_scale, g):
    s, width = u.shape
    tm = 512
    hb = tm // POOL_HALO
    nhalo = s // POOL_HALO
    return pl.pallas_call(
        functools.partial(_pool_kernel, tm=tm, seq=s),
        grid=(s // tm,),
        in_specs=[pl.BlockSpec((POOL_HALO, width), lambda i: (jnp.maximum(i * hb - 1, 0), 0)),
                  pl.BlockSpec((tm, width), lambda i: (i, 0)),
                  pl.BlockSpec((POOL_HALO, width), lambda i: (jnp.minimum((i + 1) * hb, nhalo - 1), 0)),
                  pl.BlockSpec(w_pool.shape, lambda i: (0, 0, 0)),
                  pl.BlockSpec((1, width), lambda i: (0, 0)),
                  pl.BlockSpec((tm, width), lambda i: (i, 0))],
        out_specs=pl.BlockSpec((tm, width), lambda i: (i, 0)),
        out_shape=jax.ShapeDtypeStruct((s, width), BF16),
        scratch_shapes=[pltpu.VMEM((tm + 2 * POOL_HALO, width), F32)],
        compiler_params=_params(1),
        name="pool_mix",
    )(u, u, u, w_pool, pool_scale, g)


def _out_kernel(*refs, widths):
    n = len(widths)
    y_refs = refs[:n]
    w_ref, x_ref, gate_ref, o_ref = refs[n:n + 4]
    if len(refs) > n + 4:
        yb_ref = refs[n + 4]

        @pl.when(pl.program_id(1) == 0)
        def _():
            yb_ref[...] = y_refs[0][...].astype(BF16)

        y_refs = [yb_ref]
    acc = None
    off = 0
    for y_ref, wd in zip(y_refs, widths):
        part = jnp.dot(y_ref[...], w_ref[off:off + wd, :], preferred_element_type=F32)
        acc = part if acc is None else acc + part
        off += wd
    o_ref[...] = x_ref[...] + gate_ref[...] * acc


def _out_proj(ys, w, x, gate):
    s, d = x.shape
    k = w.shape[0]
    tm, tn = min(1024, s), 512
    widths = tuple(y.shape[1] for y in ys)
    needs_cast = ys[0].dtype != BF16
    assert not needs_cast or len(ys) == 1
    return pl.pallas_call(
        functools.partial(_out_kernel, widths=widths),
        grid=(s // tm, d // tn),
        in_specs=[pl.BlockSpec((tm, wd), lambda i, j: (i, 0)) for wd in widths]
        + [pl.BlockSpec((k, tn), lambda i, j: (0, j)),
           pl.BlockSpec((tm, tn), lambda i, j: (i, j)),
           pl.BlockSpec((1, tn), lambda i, j: (0, j))],
        out_specs=pl.BlockSpec((tm, tn), lambda i, j: (i, j)),
        out_shape=jax.ShapeDtypeStruct((s, d), F32),
        scratch_shapes=[pltpu.VMEM((tm, k), BF16)] if needs_cast else [],
        compiler_params=_params(2),
        name="out_proj",
    )(*ys, w, x, gate)


def _dft_split(s):
    n1 = 1 << (int(math.log2(s)) // 2)
    n2 = s // n1
    assert n1 * n2 == s and n1 % SUBLANES == 0 and n2 % SUBLANES == 0
    return n1, n2


def _angles(num, den):
    ang = (2.0 * math.pi / den) * (num % den).astype(F32)
    return jnp.cos(ang), jnp.sin(ang)


def _fold_cs_kernel(cs_ref, wf_ref, o_ref):
    wf = wf_ref[...]
    o_ref[:, 0:FOURIER_GROUP] = jnp.dot(cs_ref[0], wf, preferred_element_type=F32,
                                        precision=lax.Precision.HIGHEST)
    o_ref[:, FOURIER_GROUP:] = jnp.dot(cs_ref[1], wf, preferred_element_type=F32,
                                       precision=lax.Precision.HIGHEST)


def _fold_channel_dft(w_fourier_j, seq):
    fg = FOURIER_GROUP
    idx = jnp.arange(fg, dtype=jnp.int32)
    cc, sc = _angles(idx[:, None] * idx[None, :], fg)
    cs = jnp.stack([cc, sc]) * (1.0 / math.sqrt(seq * fg))
    return pl.pallas_call(
        _fold_cs_kernel,
        grid=(FOURIER_GROUPS,),
        in_specs=[pl.BlockSpec((2, fg, fg), lambda g: (0, 0, 0)),
                  pl.BlockSpec((None, fg, fg), lambda g: (g, 0, 0))],
        out_specs=pl.BlockSpec((None, fg, 2 * fg), lambda g: (g, 0, 0)),
        out_shape=jax.ShapeDtypeStruct((FOURIER_GROUPS, fg, 2 * fg), F32),
        compiler_params=_params(1),
        name="fold_channel_dft",
    )(cs, w_fourier_j)


def _fold_win_kernel(w_ref, ab_ref, p_ref, q_ref):
    r = jnp.dot(w_ref[...], ab_ref[...].astype(BF16), preferred_element_type=F32)
    p_ref[...] = r[:, 0:FOURIER_GROUP].astype(BF16)
    q_ref[...] = r[:, FOURIER_GROUP:].astype(BF16)


def _fold_in_proj(w_in_u, ab):
    d = w_in_u.shape[0]
    fg = FOURIER_GROUP
    ospec = pl.BlockSpec((d, fg), lambda g: (0, g))
    return pl.pallas_call(
        _fold_win_kernel,
        grid=(FOURIER_GROUPS,),
        in_specs=[pl.BlockSpec((d, fg), lambda g: (0, g)),
                  pl.BlockSpec((None, fg, 2 * fg), lambda g: (g, 0, 0))],
        out_specs=[ospec, ospec],
        out_shape=[jax.ShapeDtypeStruct((d, D_MODEL), BF16)] * 2,
        compiler_params=_params(1),
        name="fold_in_proj",
    )(w_in_u, ab)


def _dft1_kernel(f_ref, p_ref, q_ref, t_ref, *, nb, n1):
    f = f_ref[...]
    for jj in range(nb):
        rows = slice(jj * n1, (jj + 1) * n1)
        rhs = jnp.concatenate([p_ref[rows, :], q_ref[rows, :]], axis=0)
        t_ref[jj] = jnp.dot(f, rhs, preferred_element_type=F32)


def _dft_stage1(pq_t, f1, n1, n2):
    s, c2 = pq_t.shape
    c = c2 // 2
    nb = SUBLANES
    cb = min(1024, c)
    ncb = c // cb
    return pl.pallas_call(
        functools.partial(_dft1_kernel, nb=nb, n1=n1),
        grid=(n2 // nb, ncb),
        in_specs=[pl.BlockSpec((2 * n1, 2 * n1), lambda b, j: (0, 0)),
                  pl.BlockSpec((nb * n1, cb), lambda b, j: (b, j)),
                  pl.BlockSpec((nb * n1, cb), lambda b, j: (b, j + ncb))],
        out_specs=pl.BlockSpec((nb, 2 * n1, cb), lambda b, j: (b, 0, j)),
        out_shape=jax.ShapeDtypeStruct((n2, 2 * n1, c), F32),
        compiler_params=_params(2),
        name="dft_stage1",
    )(f1, pq_t, pq_t)


def _dft2_kernel(l_ref, t_ref, g_ref, o_ref, *, n2, cb):
    rhs = t_ref[...].reshape(n2 * 2 * SUBLANES, cb).astype(BF16)
    y = jnp.dot(l_ref[0], rhs, preferred_element_type=F32)
    o_ref[...] = y.reshape(n2, SUBLANES, cb) * jax.nn.silu(g_ref[...])


def _dft_stage2(t_nk, l2, g, n1, n2):
    c = t_nk.shape[-1]
    cb = min(1024, c)
    row_blocks = pl.BlockSpec((n2, SUBLANES, cb), lambda g, j: (0, g, j))
    out = pl.pallas_call(
        functools.partial(_dft2_kernel, n2=n2, cb=cb),
        grid=(n1 // SUBLANES, c // cb),
        in_specs=[pl.BlockSpec((1, SUBLANES * n2, SUBLANES * 2 * n2), lambda g, j: (g, 0, 0)),
                  pl.BlockSpec((n2, 2, SUBLANES, cb), lambda g, j: (0, 0, g, j)),
                  row_blocks],
        out_specs=row_blocks,
        out_shape=jax.ShapeDtypeStruct((n2, n1, c), F32),
        compiler_params=_params(2),
        name="dft_stage2",
    )(l2, t_nk, g.reshape(n2, n1, c))
    return out.reshape(n1 * n2, c)


def _dft_matrices(s):
    n1, n2 = _dft_split(s)
    a = jnp.arange(n1, dtype=jnp.int32)
    c1, s1 = _angles(a[:, None] * a[None, :], n1)
    f1 = jnp.concatenate([jnp.concatenate([c1, -s1], axis=1),
                          jnp.concatenate([-s1, -c1], axis=1)], axis=0).astype(BF16)
    groups = n1 // SUBLANES
    g = jnp.arange(groups, dtype=jnp.int32)[:, None, None, None]
    k2 = jnp.arange(n2, dtype=jnp.int32)[None, :, None, None]
    j = jnp.arange(SUBLANES, dtype=jnp.int32)[None, None, :, None]
    nn = jnp.arange(n2, dtype=jnp.int32)[None, None, None, :]
    c2, s2 = _angles((SUBLANES * g + j + n1 * k2) * nn, s)
    cs = jnp.stack([c2, s2], axis=4).astype(BF16)
    eye = jnp.eye(SUBLANES, dtype=BF16)
    l2 = cs[..., None] * eye[None, None, :, None, None, :]
    return n1, n2, f1, l2.reshape(groups, n2 * SUBLANES, n2 * 2 * SUBLANES)


def kernel(x, c, norm_w, ada_w, ada_b, w_in_ab, w_pool, pool_scale, q_norm_w, k_norm_w, lambda_q1, lambda_k1,
           lambda_q2, lambda_k2, subln_w, w_out_ab, w_in_c, w_fourier, w_out_c):
    batch, s, d = x.shape
    assert batch == 1 and d == D_MODEL and s % ATT_TK == 0
    xs = x.reshape(s, d)
    mod = _ada_mod(c, ada_w, ada_b)
    n1, n2, f1, l2 = _dft_matrices(s)
    row = lambda v: v.reshape(1, -1)

    for i in range(DEPTH):
        shift, scale, gate = mod[i, :, 0:d], mod[i, :, d:2 * d], mod[i, :, 2 * d:3 * d]
        h = _modulate(xs, row(norm_w[i]), shift, scale)
        j = i // 2
        if i % 2 == 0:
            w_in = w_in_ab[j].astype(BF16)
            o1, o2, o3, o4 = POOL_WIDTH, POOL_WIDTH + DIFF_WIDTH, POOL_WIDTH + 2 * DIFF_WIDTH, POOL_WIDTH + 3 * DIFF_WIDTH
            lambda_init = 0.8 - 0.6 * math.exp(-0.3 * i)
            u_pool = _matmul(h, w_in[:, :o1], F32, PROJ_TM, PROJ_TN, name="pool_in_proj")
            gte = _matmul(h, w_in[:, o4:], F32, PROJ_TM, PROJ_TN, name="gate_in_proj")
            qt, ka, vt = _qkv_proj(h, w_in[:, o1:o2], w_in[:, o2:o3], w_in[:, o3:o4],
                                   row(q_norm_w[j]), row(k_norm_w[j]))
            y_a = _pool_mix(u_pool, w_pool[j].astype(BF16), row(pool_scale[j]), gte)
            y_b = _diff_attn(qt, ka, vt, row(lambda_q1[j]), row(lambda_k1[j]), row(lambda_q2[j]),
                             row(lambda_k2[j]), subln_w[j], gte, lambda_init)
            xs = _out_proj([y_a, y_b], w_out_ab[j].astype(BF16), xs, gate)
        else:
            w_in = w_in_c[j].astype(BF16)
            ab = _fold_channel_dft(w_fourier[j], s)
            w_p, w_q = _fold_in_proj(w_in[:, :d], ab)
            gte = _matmul(h, w_in[:, d:], F32, PROJ_TM, PROJ_TN, name="gate_in_proj")
            h_t = h.reshape(n1, n2, d).transpose(1, 0, 2).reshape(s, d)
            pq_t = _matmul(h_t, jnp.concatenate([w_p, w_q], axis=1), BF16, PROJ_TM, PROJ_TN, name="fourier_in_proj")
            t_nk = _dft_stage1(pq_t, f1, n1, n2)
            y = _dft_stage2(t_nk.reshape(n2, 2, n1, d), l2, gte, n1, n2)
            xs = _out_proj([y], w_out_c[j].astype(BF16), xs, gate)
    return xs.reshape(batch, s, d)
```

```python
import functools
import math

import jax
import jax.numpy as jnp
from jax import lax
from jax.experimental import pallas as pl
from jax.experimental.pallas import tpu as pltpu

F32 = jnp.float32
BF16 = jnp.bfloat16

D_MODEL = 2048
DEPTH = 4
NORM_EPS = 1e-6

POOL_WINDOWS = (2, 4, 8, 16)
POOL_WIDTH = D_MODEL // 2
POOL_GROUP = POOL_WIDTH // len(POOL_WINDOWS)
POOL_HALO = 8

HEADS = 8
HEAD_DIM = 64
HEAD_V = 2 * HEAD_DIM
DIFF_WIDTH = HEADS * HEAD_V
AB_WIDTH = POOL_WIDTH + DIFF_WIDTH

FOURIER_GROUPS = 4
FOURIER_GROUP = D_MODEL // FOURIER_GROUPS

LANES = 128
SUBLANES = 8
BF16_ROWS = 16
POS_RADIX = 256
COEF_PARTS = 3
LOG2E = math.log2(math.e)

ATT_TQ = 512
ATT_TK = 512
ATT_DIAG = ATT_TQ // ATT_TK
ATT_UNROLL = 8
V_ROWS = HEAD_V + BF16_ROWS
REF_MARGIN = 60.0
REF_FLOOR = 2.0 ** -60
BOUND_SLACK = 1.001

PROJ_TM, PROJ_TN = 2048, 512

VMEM_LIMIT = 56 * 1024 * 1024


def _params(n_axes):
    return pltpu.CompilerParams(dimension_semantics=("arbitrary",) * n_axes,
                                vmem_limit_bytes=VMEM_LIMIT)


def _mod_kernel(c_ref, w_ref, b_ref, o_ref):
    c = c_ref[...]
    o_ref[...] = jnp.sum(jax.nn.silu(c) * w_ref[...], axis=0, keepdims=True) + b_ref[...]


def _ada_mod(c, ada_w, ada_b):
    depth, d, n = ada_w.shape
    tn = 512
    return pl.pallas_call(
        _mod_kernel,
        grid=(depth, n // tn),
        in_specs=[pl.BlockSpec((d, 1), lambda i, j: (0, 0)),
                  pl.BlockSpec((None, d, tn), lambda i, j: (i, 0, j)),
                  pl.BlockSpec((None, 1, tn), lambda i, j: (i, 0, j))],
        out_specs=pl.BlockSpec((None, 1, tn), lambda i, j: (i, 0, j)),
        out_shape=jax.ShapeDtypeStruct((depth, 1, n), F32),
        compiler_params=_params(2),
        name="ada_mod",
    )(c.reshape(d, 1), ada_w, ada_b.reshape(depth, 1, n))


def _modulate_kernel(x_ref, nw_ref, shift_ref, scale_ref, o_ref):
    xf = x_ref[...]
    r = lax.rsqrt(jnp.mean(xf * xf, axis=-1, keepdims=True) + NORM_EPS)
    y = xf * r * nw_ref[...]
    y = y * (1.0 + scale_ref[...]) + shift_ref[...]
    o_ref[...] = y.astype(o_ref.dtype)


def _modulate(x, nw, shift, scale):
    s, d = x.shape
    tm = 512
    vec = pl.BlockSpec((1, d), lambda i: (0, 0))
    return pl.pallas_call(
        _modulate_kernel,
        grid=(s // tm,),
        in_specs=[pl.BlockSpec((tm, d), lambda i: (i, 0)), vec, vec, vec],
        out_specs=pl.BlockSpec((tm, d), lambda i: (i, 0)),
        out_shape=jax.ShapeDtypeStruct((s, d), BF16),
        compiler_params=_params(1),
        name="modulate",
    )(x, nw, shift, scale)


def _mm_kernel(a_ref, w_ref, o_ref, *, precision):
    o_ref[...] = jnp.dot(a_ref[...], w_ref[...], preferred_element_type=F32,
                         precision=precision).astype(o_ref.dtype)


def _matmul(a, w, out_dtype, tm, tn, precision=None, name="matmul"):
    m, k = a.shape
    n = w.shape[1]
    tm, tn = min(tm, m), min(tn, n)
    return pl.pallas_call(
        functools.partial(_mm_kernel, precision=precision),
        grid=(m // tm, n // tn),
        in_specs=[pl.BlockSpec((tm, k), lambda i, j: (i, 0)),
                  pl.BlockSpec((k, tn), lambda i, j: (0, j))],
        out_specs=pl.BlockSpec((tm, tn), lambda i, j: (i, j)),
        out_shape=jax.ShapeDtypeStruct((m, n), out_dtype),
        compiler_params=_params(2),
        name=name,
    )(a, w)


def _qkv_kernel(h_ref, wq_ref, wk_ref, wv_ref, qnw_ref, knw_ref, qt_ref, ka_ref, vt_ref, *, tm):
    i = pl.program_id(0)
    h = h_ref[...]
    lane = lax.broadcasted_iota(jnp.int32, (tm, LANES), 1)
    first = lane < HEAD_DIM
    pos = lax.broadcasted_iota(jnp.int32, (tm, LANES), 0) + i * tm
    pos_lo = (pos % POS_RADIX).astype(F32)
    pos_hi = ((pos % ATT_TK) // POS_RADIX).astype(F32)
    pos_cols = jnp.where(lane < HEAD_DIM + COEF_PARTS, pos_lo, jnp.where(lane < HEAD_DIM + 2 * COEF_PARTS, pos_hi, 0.0))
    ones_rows = (lax.broadcasted_iota(jnp.int32, (BF16_ROWS, ATT_TK), 0) == 0).astype(BF16)

    def halves_rms(z, w):
        sq = z * z
        ss1 = jnp.sum(jnp.where(first, sq, 0.0), axis=-1, keepdims=True)
        ss2 = jnp.sum(jnp.where(first, 0.0, sq), axis=-1, keepdims=True)
        r = jnp.where(first, lax.rsqrt(ss1 / HEAD_DIM + NORM_EPS), lax.rsqrt(ss2 / HEAD_DIM + NORM_EPS))
        return z * r * w

    zq = jnp.dot(h, wq_ref[...], preferred_element_type=F32)
    zk = jnp.dot(h, wk_ref[...], preferred_element_type=F32)
    zv = jnp.dot(h, wv_ref[...], preferred_element_type=F32)
    for hh in range(2):
        cols = slice(hh * LANES, (hh + 1) * LANES)
        qn = halves_rms(zq[:, cols], qnw_ref[...]) * (HEAD_DIM ** -0.5) * LOG2E
        kn = halves_rms(zk[:, cols], knw_ref[...])
        for m in range(2):
            qm = qn if m == 0 else pltpu.roll(qn, HEAD_DIM, axis=1)
            km = kn if m == 0 else pltpu.roll(kn, HEAD_DIM, axis=1)
            qt_ref[hh, m] = jnp.where(first, qm, 0.0).T.astype(BF16)
            ka = jnp.where(first, km, pos_cols).astype(BF16)
            for cc in range(tm // ATT_TK):
                ka_ref[hh, m, cc] = ka[cc * ATT_TK:(cc + 1) * ATT_TK]
        v = zv[:, cols]
        for cc in range(tm // ATT_TK):
            vt_ref[hh, cc, 0:HEAD_V, :] = v[cc * ATT_TK:(cc + 1) * ATT_TK].T.astype(BF16)
            vt_ref[hh, cc, HEAD_V:V_ROWS, :] = ones_rows


def _qkv_proj(h, wq, wk, wv, qnw, knw):
    s, d = h.shape
    tm = min(1024, s)
    nchunk = s // ATT_TK
    cpt = tm // ATT_TK
    tn = 2 * LANES
    wspec = pl.BlockSpec((d, tn), lambda i, j: (0, j))
    vec = pl.BlockSpec((1, LANES), lambda i, j: (0, 0))
    return pl.pallas_call(
        functools.partial(_qkv_kernel, tm=tm),
        grid=(s // tm, HEADS // 2),
        in_specs=[pl.BlockSpec((tm, d), lambda i, j: (i, 0)), wspec, wspec, wspec, vec, vec],
        out_specs=[pl.BlockSpec((2, 2, LANES, tm), lambda i, j: (j, 0, 0, i)),
                   pl.BlockSpec((2, 2, cpt, ATT_TK, LANES), lambda i, j: (j, 0, i, 0, 0)),
                   pl.BlockSpec((2, cpt, V_ROWS, ATT_TK), lambda i, j: (j, i, 0, 0))],
        out_shape=[jax.ShapeDtypeStruct((HEADS, 2, LANES, s), BF16),
                   jax.ShapeDtypeStruct((HEADS, 2, nchunk, ATT_TK, LANES), BF16),
                   jax.ShapeDtypeStruct((HEADS, nchunk, V_ROWS, ATT_TK), BF16)],
        compiler_params=_params(2),
        name="qkv_proj",
    )(h, wq, wk, wv, jnp.tile(qnw, (1, 2)), jnp.tile(knw, (1, 2)))


def _attn_kernel(slopes_ref, qt_ref, ka_ref, vt_ref, lq1_ref, lk1_ref, lq2_ref, lk2_ref, subw_ref, g_ref,
                 o_ref, acc_ref, mu_ref, qv_ref, kmax_ref, p_a, p_b, *, lambda_init, nchunk, unroll):
    hd = pl.program_id(0)
    i = pl.program_id(1)
    base = hd * (1 + COEF_PARTS)
    slope = slopes_ref[base]
    rowq = lax.broadcasted_iota(jnp.int32, (LANES, ATT_TQ), 0)
    q_off = lax.broadcasted_iota(jnp.int32, (1, ATT_TQ), 1).astype(F32)

    row1 = lax.broadcasted_iota(jnp.int32, (LANES, 1), 0)
    coef_col = jnp.zeros((LANES, 1), F32)
    for t in range(COEF_PARTS):
        piece = slopes_ref[base + 1 + t]
        coef_col = jnp.where(row1 == HEAD_DIM + t, piece, coef_col)
        coef_col = jnp.where(row1 == HEAD_DIM + COEF_PARTS + t, piece * POS_RADIX, coef_col)
    for m in range(2):
        qf = qt_ref[0, m].astype(F32)
        for kind, sign in ((0, -1.0), (1, 0.0), (2, 1.0)):
            qv_ref[kind, m] = jnp.where(rowq >= HEAD_DIM, sign * coef_col, qf).astype(BF16)

    @pl.when(i == 0)
    def _():
        lane = lax.broadcasted_iota(jnp.int32, (ATT_TK, LANES), 1)
        for m in range(2):
            def widest(c, best):
                kc = ka_ref[0, m, c].astype(F32)
                return jnp.maximum(best, jnp.sum(jnp.where(lane < HEAD_DIM, kc * kc, 0.0), axis=1, keepdims=True))
            kmax_ref[m] = jnp.max(lax.fori_loop(0, nchunk, widest, jnp.zeros((ATT_TK, 1), F32)))

    for m in range(2):
        qf = qt_ref[0, m].astype(F32)
        bound = jnp.sqrt(jnp.sum(qf * qf, axis=0, keepdims=True) * kmax_ref[m]) * BOUND_SLACK
        mu_ref[m] = bound - REF_MARGIN

    c_diag = i * ATT_DIAG

    def chunk_of(e):
        if isinstance(e, int) and e < ATT_DIAG:
            return c_diag + e
        rest = e - ATT_DIAG
        return jnp.where(e < ATT_DIAG, c_diag + e, rest + ATT_DIAG * (rest >= c_diag).astype(jnp.int32))

    def side_of(c):
        return (c < c_diag).astype(jnp.int32) - (c >= c_diag + ATT_DIAG).astype(jnp.int32)

    p_bufs = (p_a, p_b)

    def probs(e, p_ref):
        c = chunk_of(e)
        side = side_of(c)
        sidef = side.astype(F32)
        gap = jnp.abs(c * ATT_TK - i * ATT_TQ).astype(F32)
        cvec = -slope * (sidef * sidef * gap + sidef * q_off)
        for m in range(2):
            st = jnp.dot(ka_ref[0, m, c], qv_ref[1 + side, m], preferred_element_type=F32)
            if isinstance(e, int) and e < ATT_DIAG:
                key_off = lax.broadcasted_iota(jnp.int32, (ATT_TK, ATT_TQ), 0) + e * ATT_TK
                qry_off = lax.broadcasted_iota(jnp.int32, (ATT_TK, ATT_TQ), 1)
                st = st - slope * jnp.abs(key_off - qry_off).astype(F32)
            p_ref[m] = jnp.exp2(st - (mu_ref[m] - cvec)).astype(BF16)

    def group(e0, count, probs_last):
        sums = [None, None]
        for u in range(count):
            if u < count - 1 or probs_last:
                probs(e0 + u + 1, p_bufs[(u + 1) % 2])
            c = chunk_of(e0 + u)
            for m in range(2):
                pv = jnp.dot(vt_ref[0, c], p_bufs[u % 2][m], preferred_element_type=F32)
                sums[m] = pv if sums[m] is None else sums[m] + pv
        for m in range(2):
            acc_ref[m] += sums[m]

    def loop_body(it, carry):
        group(it * unroll, unroll, True)
        return carry

    def one_pass(_):
        acc_ref[...] = jnp.zeros(acc_ref.shape, F32)
        n_groups = nchunk // unroll
        probs(0, p_a)
        group(0, unroll, True)
        lax.fori_loop(1, n_groups - 1, loop_body, 0)
        group((n_groups - 1) * unroll, unroll, False)
        lowest = None
        for m in range(2):
            den = acc_ref[m, HEAD_V:HEAD_V + 1, :]
            mu_ref[m] = jnp.where(den < REF_FLOOR, mu_ref[m] - 2.0 * REF_MARGIN, mu_ref[m])
            lowest = jnp.min(den) if lowest is None else jnp.minimum(lowest, jnp.min(den))
        return (lowest < REF_FLOOR).astype(jnp.int32)

    lax.while_loop(lambda retry: retry > 0, one_pass, jnp.int32(1))

    lam = (jnp.exp(jnp.sum(lq1_ref[...] * lk1_ref[...], axis=-1, keepdims=True))
           - jnp.exp(jnp.sum(lq2_ref[...] * lk2_ref[...], axis=-1, keepdims=True)) + lambda_init)
    a1 = acc_ref[0]
    a2 = acc_ref[1]
    o = a1[0:HEAD_V] / a1[HEAD_V:HEAD_V + 1] - lam * (a2[0:HEAD_V] / a2[HEAD_V:HEAD_V + 1])
    r = lax.rsqrt(jnp.mean(o * o, axis=0, keepdims=True) + NORM_EPS)
    y = (o * r * subw_ref[...]) * (1.0 - lambda_init)
    o_ref[...] = (y.T * jax.nn.silu(g_ref[...])).astype(o_ref.dtype)


def _diff_attn(qt, ka, vt, lq1, lk1, lq2, lk2, subw, g, lambda_init):
    s = qt.shape[-1]
    g_col0 = (g.shape[1] - DIFF_WIDTH) // HEAD_V
    nchunk = s // ATT_TK
    unroll = min(ATT_UNROLL, nchunk // 2)
    assert unroll % 2 == 0 and nchunk % unroll == 0 and unroll >= ATT_DIAG and s % ATT_TQ == 0
    whole = jnp.asarray([LOG2E * 2.0 ** (-8.0 * (h + 1) / HEADS) for h in range(HEADS)], F32)
    pieces, rest = [], whole
    for _ in range(COEF_PARTS):
        pieces.append(rest.astype(BF16).astype(F32))
        rest = rest - pieces[-1]
    slopes = jnp.stack([whole] + pieces, axis=1).reshape(-1)
    vec = pl.BlockSpec((1, HEAD_DIM), lambda h, i: (0, 0))
    return pl.pallas_call(
        functools.partial(_attn_kernel, lambda_init=lambda_init, nchunk=nchunk, unroll=unroll),
        grid=(HEADS, s // ATT_TQ),
        in_specs=[pl.BlockSpec(memory_space=pltpu.SMEM),
                  pl.BlockSpec((1, 2, LANES, ATT_TQ), lambda h, i: (h, 0, 0, i)),
                  pl.BlockSpec((1, 2, nchunk, ATT_TK, LANES), lambda h, i: (h, 0, 0, 0, 0)),
                  pl.BlockSpec((1, nchunk, V_ROWS, ATT_TK), lambda h, i: (h, 0, 0, 0)),
                  vec, vec, vec, vec,
                  pl.BlockSpec((HEAD_V, 1), lambda h, i: (0, 0)),
                  pl.BlockSpec((ATT_TQ, HEAD_V), lambda h, i: (i, g_col0 + h))],
        out_specs=pl.BlockSpec((ATT_TQ, HEAD_V), lambda h, i: (i, h)),
        out_shape=jax.ShapeDtypeStruct((s, DIFF_WIDTH), BF16),
        scratch_shapes=[pltpu.VMEM((2, V_ROWS, ATT_TQ), F32), pltpu.VMEM((2, 1, ATT_TQ), F32),
                        pltpu.VMEM((3, 2, LANES, ATT_TQ), BF16), pltpu.SMEM((2,), F32),
                        pltpu.VMEM((2, ATT_TK, ATT_TQ), BF16), pltpu.VMEM((2, ATT_TK, ATT_TQ), BF16)],
        compiler_params=_params(2),
        name="diff_attn",
    )(slopes, qt, ka, vt, lq1, lk1, lq2, lk2, subw.reshape(HEAD_V, 1), g)


def _pool_kernel(prev_ref, cur_ref, next_ref, w_ref, scale_ref, g_ref, o_ref, ext_ref, *, tm, seq):
    i = pl.program_id(0)
    last = pl.num_programs(0) - 1
    zero_halo = jnp.zeros((POOL_HALO, POOL_WIDTH), F32)
    ext_ref[0:POOL_HALO] = jnp.where(i > 0, prev_ref[...], zero_halo)
    ext_ref[POOL_HALO:POOL_HALO + tm] = cur_ref[...]
    ext_ref[POOL_HALO + tm:POOL_HALO + tm + POOL_HALO] = jnp.where(i < last, next_ref[...], zero_halo)
    t = lax.broadcasted_iota(jnp.int32, (tm, 1), 0) + i * tm
    for g, w in enumerate(POOL_WINDOWS):
        cols = pl.ds(g * POOL_GROUP, POOL_GROUP)
        win = ext_ref[pl.ds(POOL_HALO - w // 2, tm), cols]
        for d in range(-(w // 2) + 1, w - w // 2):
            win = win + ext_ref[pl.ds(POOL_HALO + d, tm), cols]
        lo = jnp.maximum(t - w // 2, 0)
        hi = jnp.minimum(t + (w - w // 2) - 1, seq - 1)
        cnt = (hi - lo + 1).astype(F32)
        pooled = win / cnt - cur_ref[:, cols]
        y = jnp.dot(pooled.astype(BF16), w_ref[g], preferred_element_type=F32)
        o_ref[:, cols] = (y * scale_ref[:, cols] * jax.nn.silu(g_ref[:, cols])).astype(o_ref.dtype)


def _pool_mix(u, w_pool, pool_scale, g):
    s, width = u.shape
    tm = 512
    hb = tm // POOL_HALO
    nhalo = s // POOL_HALO
    return pl.pallas_call(
        functools.partial(_pool_kernel, tm=tm, seq=s),
        grid=(s // tm,),
        in_specs=[pl.BlockSpec((POOL_HALO, width), lambda i: (jnp.maximum(i * hb - 1, 0), 0)),
                  pl.BlockSpec((tm, width), lambda i: (i, 0)),
                  pl.BlockSpec((POOL_HALO, width), lambda i: (jnp.minimum((i + 1) * hb, nhalo - 1), 0)),
                  pl.BlockSpec(w_pool.shape, lambda i: (0, 0, 0)),
                  pl.BlockSpec((1, width), lambda i: (0, 0)),
                  pl.BlockSpec((tm, width), lambda i: (i, 0))],
        out_specs=pl.BlockSpec((tm, width), lambda i: (i, 0)),
        out_shape=jax.ShapeDtypeStruct((s, width), BF16),
        scratch_shapes=[pltpu.VMEM((tm + 2 * POOL_HALO, width), F32)],
        compiler_params=_params(1),
        name="pool_mix",
    )(u, u, u, w_pool, pool_scale, g)


def _out_kernel(*refs, widths):
    n = len(widths)
    y_refs = refs[:n]
    w_ref, x_ref, gate_ref, o_ref = refs[n:n + 4]
    if len(refs) > n + 4:
        yb_ref = refs[n + 4]

        @pl.when(pl.program_id(1) == 0)
        def _():
            yb_ref[...] = y_refs[0][...].astype(BF16)

        y_refs = [yb_ref]
    acc = None
    off = 0
    for y_ref, wd in zip(y_refs, widths):
        part = jnp.dot(y_ref[...], w_ref[off:off + wd, :], preferred_element_type=F32)
        acc = part if acc is None else acc + part
        off += wd
    o_ref[...] = x_ref[...] + gate_ref[...] * acc


def _out_proj(ys, w, x, gate):
    s, d = x.shape
    k = w.shape[0]
    tm, tn = min(1024, s), 512
    widths = tuple(y.shape[1] for y in ys)
    needs_cast = ys[0].dtype != BF16
    assert not needs_cast or len(ys) == 1
    return pl.pallas_call(
        functools.partial(_out_kernel, widths=widths),
        grid=(s // tm, d // tn),
        in_specs=[pl.BlockSpec((tm, wd), lambda i, j: (i, 0)) for wd in widths]
        + [pl.BlockSpec((k, tn), lambda i, j: (0, j)),
           pl.BlockSpec((tm, tn), lambda i, j: (i, j)),
           pl.BlockSpec((1, tn), lambda i, j: (0, j))],
        out_specs=pl.BlockSpec((tm, tn), lambda i, j: (i, j)),
        out_shape=jax.ShapeDtypeStruct((s, d), F32),
        scratch_shapes=[pltpu.VMEM((tm, k), BF16)] if needs_cast else [],
        compiler_params=_params(2),
        name="out_proj",
    )(*ys, w, x, gate)


def _dft_split(s):
    n1 = 1 << (int(math.log2(s)) // 2)
    n2 = s // n1
    assert n1 * n2 == s and n1 % SUBLANES == 0 and n2 % SUBLANES == 0
    return n1, n2


def _angles(num, den):
    ang = (2.0 * math.pi / den) * (num % den).astype(F32)
    return jnp.cos(ang), jnp.sin(ang)


def _fold_cs_kernel(cs_ref, wf_ref, o_ref):
    wf = wf_ref[...]
    o_ref[:, 0:FOURIER_GROUP] = jnp.dot(cs_ref[0], wf, preferred_element_type=F32,
                                        precision=lax.Precision.HIGHEST)
    o_ref[:, FOURIER_GROUP:] = jnp.dot(cs_ref[1], wf, preferred_element_type=F32,
                                       precision=lax.Precision.HIGHEST)


def _fold_channel_dft(w_fourier_j, seq):
    fg = FOURIER_GROUP
    idx = jnp.arange(fg, dtype=jnp.int32)
    cc, sc = _angles(idx[:, None] * idx[None, :], fg)
    cs = jnp.stack([cc, sc]) * (1.0 / math.sqrt(seq * fg))
    return pl.pallas_call(
        _fold_cs_kernel,
        grid=(FOURIER_GROUPS,),
        in_specs=[pl.BlockSpec((2, fg, fg), lambda g: (0, 0, 0)),
                  pl.BlockSpec((None, fg, fg), lambda g: (g, 0, 0))],
        out_specs=pl.BlockSpec((None, fg, 2 * fg), lambda g: (g, 0, 0)),
        out_shape=jax.ShapeDtypeStruct((FOURIER_GROUPS, fg, 2 * fg), F32),
        compiler_params=_params(1),
        name="fold_channel_dft",
    )(cs, w_fourier_j)


def _fold_win_kernel(w_ref, ab_ref, p_ref, q_ref):
    r = jnp.dot(w_ref[...], ab_ref[...].astype(BF16), preferred_element_type=F32)
    p_ref[...] = r[:, 0:FOURIER_GROUP].astype(BF16)
    q_ref[...] = r[:, FOURIER_GROUP:].astype(BF16)


def _fold_in_proj(w_in_u, ab):
    d = w_in_u.shape[0]
    fg = FOURIER_GROUP
    ospec = pl.BlockSpec((d, fg), lambda g: (0, g))
    return pl.pallas_call(
        _fold_win_kernel,
        grid=(FOURIER_GROUPS,),
        in_specs=[pl.BlockSpec((d, fg), lambda g: (0, g)),
                  pl.BlockSpec((None, fg, 2 * fg), lambda g: (g, 0, 0))],
        out_specs=[ospec, ospec],
        out_shape=[jax.ShapeDtypeStruct((d, D_MODEL), BF16)] * 2,
        compiler_params=_params(1),
        name="fold_in_proj",
    )(w_in_u, ab)


def _dft1_kernel(f_ref, p_ref, q_ref, t_ref, *, nb, n1):
    f = f_ref[...]
    for jj in range(nb):
        rows = slice(jj * n1, (jj + 1) * n1)
        rhs = jnp.concatenate([p_ref[rows, :], q_ref[rows, :]], axis=0)
        t_ref[jj] = jnp.dot(f, rhs, preferred_element_type=F32)


def _dft_stage1(pq_t, f1, n1, n2):
    s, c2 = pq_t.shape
    c = c2 // 2
    nb = SUBLANES
    cb = min(1024, c)
    ncb = c // cb
    return pl.pallas_call(
        functools.partial(_dft1_kernel, nb=nb, n1=n1),
        grid=(n2 // nb, ncb),
        in_specs=[pl.BlockSpec((2 * n1, 2 * n1), lambda b, j: (0, 0)),
                  pl.BlockSpec((nb * n1, cb), lambda b, j: (b, j)),
                  pl.BlockSpec((nb * n1, cb), lambda b, j: (b, j + ncb))],
        out_specs=pl.BlockSpec((nb, 2 * n1, cb), lambda b, j: (b, 0, j)),
        out_shape=jax.ShapeDtypeStruct((n2, 2 * n1, c), F32),
        compiler_params=_params(2),
        name="dft_stage1",
    )(f1, pq_t, pq_t)


def _dft2_kernel(l_ref, t_ref, g_ref, o_ref, *, n2, cb):
    rhs = t_ref[...].reshape(n2 * 2 * SUBLANES, cb).astype(BF16)
    y = jnp.dot(l_ref[0], rhs, preferred_element_type=F32)
    o_ref[...] = y.reshape(n2, SUBLANES, cb) * jax.nn.silu(g_ref[...])


def _dft_stage2(t_nk, l2, g, n1, n2):
    c = t_nk.shape[-1]
    cb = min(1024, c)
    row_blocks = pl.BlockSpec((n2, SUBLANES, cb), lambda g, j: (0, g, j))
    out = pl.pallas_call(
        functools.partial(_dft2_kernel, n2=n2, cb=cb),
        grid=(n1 // SUBLANES, c // cb),
        in_specs=[pl.BlockSpec((1, SUBLANES * n2, SUBLANES * 2 * n2), lambda g, j: (g, 0, 0)),
                  pl.BlockSpec((n2, 2, SUBLANES, cb), lambda g, j: (0, 0, g, j)),
                  row_blocks],
        out_specs=row_blocks,
        out_shape=jax.ShapeDtypeStruct((n2, n1, c), F32),
        compiler_params=_params(2),
        name="dft_stage2",
    )(l2, t_nk, g.reshape(n2, n1, c))
    return out.reshape(n1 * n2, c)


def _dft_matrices(s):
    n1, n2 = _dft_split(s)
    a = jnp.arange(n1, dtype=jnp.int32)
    c1, s1 = _angles(a[:, None] * a[None, :], n1)
    f1 = jnp.concatenate([jnp.concatenate([c1, -s1], axis=1),
                          jnp.concatenate([-s1, -c1], axis=1)], axis=0).astype(BF16)
    groups = n1 // SUBLANES
    g = jnp.arange(groups, dtype=jnp.int32)[:, None, None, None]
    k2 = jnp.arange(n2, dtype=jnp.int32)[None, :, None, None]
    j = jnp.arange(SUBLANES, dtype=jnp.int32)[None, None, :, None]
    nn = jnp.arange(n2, dtype=jnp.int32)[None, None, None, :]
    c2, s2 = _angles((SUBLANES * g + j + n1 * k2) * nn, s)
    cs = jnp.stack([c2, s2], axis=4).astype(BF16)
    return n1, n2, f1, _expand_twiddles(cs.reshape(groups, n2 * SUBLANES, n2 * 2))


def _expand_kernel(cs_ref, e_ref, o_ref):
    spread = jnp.dot(cs_ref[0], e_ref[...], preferred_element_type=F32)
    row = lax.broadcasted_iota(jnp.int32, spread.shape, 0)
    col = lax.broadcasted_iota(jnp.int32, spread.shape, 1)
    o_ref[0] = jnp.where(row % SUBLANES == col % SUBLANES, spread, 0.0).astype(BF16)


def _expand_twiddles(cs):
    groups, rows, cols = cs.shape
    wide = cols * SUBLANES
    e = (jnp.arange(wide, dtype=jnp.int32)[None, :] // SUBLANES == jnp.arange(cols, dtype=jnp.int32)[:, None])
    return pl.pallas_call(
        _expand_kernel,
        grid=(groups,),
        in_specs=[pl.BlockSpec((1, rows, cols), lambda g: (g, 0, 0)),
                  pl.BlockSpec((cols, wide), lambda g: (0, 0))],
        out_specs=pl.BlockSpec((1, rows, wide), lambda g: (g, 0, 0)),
        out_shape=jax.ShapeDtypeStruct((groups, rows, wide), BF16),
        compiler_params=_params(1),
        name="expand_twiddles",
    )(cs, e.astype(BF16))


def kernel(x, c, norm_w, ada_w, ada_b, w_in_ab, w_pool, pool_scale, q_norm_w, k_norm_w, lambda_q1, lambda_k1,
           lambda_q2, lambda_k2, subln_w, w_out_ab, w_in_c, w_fourier, w_out_c):
    batch, s, d = x.shape
    assert batch == 1 and d == D_MODEL and s % ATT_TK == 0
    xs = x.reshape(s, d)
    mod = _ada_mod(c, ada_w, ada_b)
    n1, n2, f1, l2 = _dft_matrices(s)
    row = lambda v: v.reshape(1, -1)

    for i in range(DEPTH):
        shift, scale, gate = mod[i, :, 0:d], mod[i, :, d:2 * d], mod[i, :, 2 * d:3 * d]
        h = _modulate(xs, row(norm_w[i]), shift, scale)
        j = i // 2
        if i % 2 == 0:
            w_in = w_in_ab[j].astype(BF16)
            o1, o2, o3, o4 = POOL_WIDTH, POOL_WIDTH + DIFF_WIDTH, POOL_WIDTH + 2 * DIFF_WIDTH, POOL_WIDTH + 3 * DIFF_WIDTH
            lambda_init = 0.8 - 0.6 * math.exp(-0.3 * i)
            u_pool = _matmul(h, w_in[:, :o1], F32, PROJ_TM, PROJ_TN, name="pool_in_proj")
            gte = _matmul(h, w_in[:, o4:], F32, PROJ_TM, PROJ_TN, name="gate_in_proj")
            qt, ka, vt = _qkv_proj(h, w_in[:, o1:o2], w_in[:, o2:o3], w_in[:, o3:o4],
                                   row(q_norm_w[j]), row(k_norm_w[j]))
            y_a = _pool_mix(u_pool, w_pool[j].astype(BF16), row(pool_scale[j]), gte)
            y_b = _diff_attn(qt, ka, vt, row(lambda_q1[j]), row(lambda_k1[j]), row(lambda_q2[j]),
                             row(lambda_k2[j]), subln_w[j], gte, lambda_init)
            xs = _out_proj([y_a, y_b], w_out_ab[j].astype(BF16), xs, gate)
        else:
            w_in = w_in_c[j].astype(BF16)
            ab = _fold_channel_dft(w_fourier[j], s)
            w_p, w_q = _fold_in_proj(w_in[:, :d], ab)
            gte = _matmul(h, w_in[:, d:], F32, PROJ_TM, PROJ_TN, name="gate_in_proj")
            h_t = h.reshape(n1, n2, d).transpose(1, 0, 2).reshape(s, d)
            pq_t = _matmul(h_t, jnp.concatenate([w_p, w_q], axis=1), BF16, PROJ_TM, PROJ_TN, name="fourier_in_proj")
            t_nk = _dft_stage1(pq_t, f1, n1, n2)
            y = _dft_stage2(t_nk.reshape(n2, 2, n1, d), l2, gte, n1, n2)
            xs = _out_proj([y], w_out_c[j].astype(BF16), xs, gate)
    return xs.reshape(batch, s, d)
```

```python
import functools
import math

import jax
import jax.numpy as jnp
from jax import lax
from jax.experimental import pallas as pl
from jax.experimental.pallas import tpu as pltpu

F32 = jnp.float32
BF16 = jnp.bfloat16

D_MODEL = 2048
DEPTH = 4
NORM_EPS = 1e-6

POOL_WINDOWS = (2, 4, 8, 16)
POOL_WIDTH = D_MODEL // 2
POOL_GROUP = POOL_WIDTH // len(POOL_WINDOWS)
POOL_HALO = 8

HEADS = 8
HEAD_DIM = 64
HEAD_V = 2 * HEAD_DIM
DIFF_WIDTH = HEADS * HEAD_V
AB_WIDTH = POOL_WIDTH + DIFF_WIDTH

FOURIER_GROUPS = 4
FOURIER_GROUP = D_MODEL // FOURIER_GROUPS

LANES = 128
SUBLANES = 8
BF16_ROWS = 16
POS_RADIX = 256
COEF_PARTS = 3
LOG2E = math.log2(math.e)

ATT_TQ = 512
ATT_TK = 512
ATT_DIAG = ATT_TQ // ATT_TK
ATT_UNROLL = 8
V_ROWS = HEAD_V + BF16_ROWS
REF_MARGIN = 60.0
REF_FLOOR = 2.0 ** -60
BOUND_SLACK = 1.001

PROJ_TM, PROJ_TN = 2048, 512
OUT_SUB = 128

VMEM_LIMIT = 56 * 1024 * 1024


def _params(n_axes):
    return pltpu.CompilerParams(dimension_semantics=("arbitrary",) * n_axes,
                                vmem_limit_bytes=VMEM_LIMIT)


def _mod_kernel(c_ref, w_ref, b_ref, o_ref):
    c = c_ref[...]
    o_ref[...] = jnp.sum(jax.nn.silu(c) * w_ref[...], axis=0, keepdims=True) + b_ref[...]


def _ada_mod(c, ada_w, ada_b):
    depth, d, n = ada_w.shape
    tn = 512
    return pl.pallas_call(
        _mod_kernel,
        grid=(depth, n // tn),
        in_specs=[pl.BlockSpec((d, 1), lambda i, j: (0, 0)),
                  pl.BlockSpec((None, d, tn), lambda i, j: (i, 0, j)),
                  pl.BlockSpec((None, 1, tn), lambda i, j: (i, 0, j))],
        out_specs=pl.BlockSpec((None, 1, tn), lambda i, j: (i, 0, j)),
        out_shape=jax.ShapeDtypeStruct((depth, 1, n), F32),
        compiler_params=_params(2),
        name="ada_mod",
    )(c.reshape(d, 1), ada_w, ada_b.reshape(depth, 1, n))


def _modulate_kernel(x_ref, nw_ref, shift_ref, scale_ref, o_ref):
    xf = x_ref[...]
    r = lax.rsqrt(jnp.mean(xf * xf, axis=-1, keepdims=True) + NORM_EPS)
    y = xf * r * nw_ref[...]
    y = y * (1.0 + scale_ref[...]) + shift_ref[...]
    o_ref[...] = y.astype(o_ref.dtype)


def _modulate(x, nw, shift, scale):
    s, d = x.shape
    tm = 512
    vec = pl.BlockSpec((1, d), lambda i: (0, 0))
    return pl.pallas_call(
        _modulate_kernel,
        grid=(s // tm,),
        in_specs=[pl.BlockSpec((tm, d), lambda i: (i, 0)), vec, vec, vec],
        out_specs=pl.BlockSpec((tm, d), lambda i: (i, 0)),
        out_shape=jax.ShapeDtypeStruct((s, d), BF16),
        compiler_params=_params(1),
        name="modulate",
    )(x, nw, shift, scale)


def _mm_kernel(a_ref, w_ref, o_ref, *, precision):
    o_ref[...] = jnp.dot(a_ref[...], w_ref[...], preferred_element_type=F32,
                         precision=precision).astype(o_ref.dtype)


def _matmul(a, w, out_dtype, tm, tn, precision=None, name="matmul"):
    m, k = a.shape
    n = w.shape[1]
    tm, tn = min(tm, m), min(tn, n)
    return pl.pallas_call(
        functools.partial(_mm_kernel, precision=precision),
        grid=(m // tm, n // tn),
        in_specs=[pl.BlockSpec((tm, k), lambda i, j: (i, 0)),
                  pl.BlockSpec((k, tn), lambda i, j: (0, j))],
        out_specs=pl.BlockSpec((tm, tn), lambda i, j: (i, j)),
        out_shape=jax.ShapeDtypeStruct((m, n), out_dtype),
        compiler_params=_params(2),
        name=name,
    )(a, w)


def _qkv_kernel(h_ref, wq_ref, wk_ref, wv_ref, qnw_ref, knw_ref, qt_ref, ka_ref, vt_ref, *, tm):
    i = pl.program_id(0)
    h = h_ref[...]
    lane = lax.broadcasted_iota(jnp.int32, (tm, LANES), 1)
    first = lane < HEAD_DIM
    pos = lax.broadcasted_iota(jnp.int32, (tm, LANES), 0) + i * tm
    pos_lo = (pos % POS_RADIX).astype(F32)
    pos_hi = ((pos % ATT_TK) // POS_RADIX).astype(F32)
    pos_cols = jnp.where(lane < HEAD_DIM + COEF_PARTS, pos_lo, jnp.where(lane < HEAD_DIM + 2 * COEF_PARTS, pos_hi, 0.0))
    ones_rows = (lax.broadcasted_iota(jnp.int32, (BF16_ROWS, ATT_TK), 0) == 0).astype(BF16)

    def halves_rms(z, w):
        sq = z * z
        ss1 = jnp.sum(jnp.where(first, sq, 0.0), axis=-1, keepdims=True)
        ss2 = jnp.sum(jnp.where(first, 0.0, sq), axis=-1, keepdims=True)
        r = jnp.where(first, lax.rsqrt(ss1 / HEAD_DIM + NORM_EPS), lax.rsqrt(ss2 / HEAD_DIM + NORM_EPS))
        return z * r * w

    zq = jnp.dot(h, wq_ref[...], preferred_element_type=F32)
    zk = jnp.dot(h, wk_ref[...], preferred_element_type=F32)
    zv = jnp.dot(h, wv_ref[...], preferred_element_type=F32)
    for hh in range(2):
        cols = slice(hh * LANES, (hh + 1) * LANES)
        qn = halves_rms(zq[:, cols], qnw_ref[...]) * (HEAD_DIM ** -0.5) * LOG2E
        kn = halves_rms(zk[:, cols], knw_ref[...])
        for m in range(2):
            qm = qn if m == 0 else pltpu.roll(qn, HEAD_DIM, axis=1)
            km = kn if m == 0 else pltpu.roll(kn, HEAD_DIM, axis=1)
            qt_ref[hh, m] = jnp.where(first, qm, 0.0).T.astype(BF16)
            ka = jnp.where(first, km, pos_cols).astype(BF16)
            for cc in range(tm // ATT_TK):
                ka_ref[hh, m, cc] = ka[cc * ATT_TK:(cc + 1) * ATT_TK]
        v = zv[:, cols]
        for cc in range(tm // ATT_TK):
            vt_ref[hh, cc, 0:HEAD_V, :] = v[cc * ATT_TK:(cc + 1) * ATT_TK].T.astype(BF16)
            vt_ref[hh, cc, HEAD_V:V_ROWS, :] = ones_rows


def _qkv_proj(h, wq, wk, wv, qnw, knw):
    s, d = h.shape
    tm = min(1024, s)
    nchunk = s // ATT_TK
    cpt = tm // ATT_TK
    tn = 2 * LANES
    wspec = pl.BlockSpec((d, tn), lambda i, j: (0, j))
    vec = pl.BlockSpec((1, LANES), lambda i, j: (0, 0))
    return pl.pallas_call(
        functools.partial(_qkv_kernel, tm=tm),
        grid=(s // tm, HEADS // 2),
        in_specs=[pl.BlockSpec((tm, d), lambda i, j: (i, 0)), wspec, wspec, wspec, vec, vec],
        out_specs=[pl.BlockSpec((2, 2, LANES, tm), lambda i, j: (j, 0, 0, i)),
                   pl.BlockSpec((2, 2, cpt, ATT_TK, LANES), lambda i, j: (j, 0, i, 0, 0)),
                   pl.BlockSpec((2, cpt, V_ROWS, ATT_TK), lambda i, j: (j, i, 0, 0))],
        out_shape=[jax.ShapeDtypeStruct((HEADS, 2, LANES, s), BF16),
                   jax.ShapeDtypeStruct((HEADS, 2, nchunk, ATT_TK, LANES), BF16),
                   jax.ShapeDtypeStruct((HEADS, nchunk, V_ROWS, ATT_TK), BF16)],
        compiler_params=_params(2),
        name="qkv_proj",
    )(h, wq, wk, wv, jnp.tile(qnw, (1, 2)), jnp.tile(knw, (1, 2)))


def _attn_kernel(slopes_ref, qt_ref, ka_ref, vt_ref, lq1_ref, lk1_ref, lq2_ref, lk2_ref, subw_ref, g_ref,
                 o_ref, acc_ref, mu_ref, qv_ref, kmax_ref, p_a, p_b, *, lambda_init, nchunk, unroll):
    hd = pl.program_id(0)
    i = pl.program_id(1)
    base = hd * (1 + COEF_PARTS)
    slope = slopes_ref[base]
    rowq = lax.broadcasted_iota(jnp.int32, (LANES, ATT_TQ), 0)
    q_off = lax.broadcasted_iota(jnp.int32, (1, ATT_TQ), 1).astype(F32)

    row1 = lax.broadcasted_iota(jnp.int32, (LANES, 1), 0)
    coef_col = jnp.zeros((LANES, 1), F32)
    for t in range(COEF_PARTS):
        piece = slopes_ref[base + 1 + t]
        coef_col = jnp.where(row1 == HEAD_DIM + t, piece, coef_col)
        coef_col = jnp.where(row1 == HEAD_DIM + COEF_PARTS + t, piece * POS_RADIX, coef_col)
    for m in range(2):
        qf = qt_ref[0, m].astype(F32)
        for kind, sign in ((0, -1.0), (1, 0.0), (2, 1.0)):
            qv_ref[kind, m] = jnp.where(rowq >= HEAD_DIM, sign * coef_col, qf).astype(BF16)

    @pl.when(i == 0)
    def _():
        lane = lax.broadcasted_iota(jnp.int32, (ATT_TK, LANES), 1)
        for m in range(2):
            def widest(c, best):
                kc = ka_ref[0, m, c].astype(F32)
                return jnp.maximum(best, jnp.sum(jnp.where(lane < HEAD_DIM, kc * kc, 0.0), axis=1, keepdims=True))
            kmax_ref[m] = jnp.max(lax.fori_loop(0, nchunk, widest, jnp.zeros((ATT_TK, 1), F32)))

    for m in range(2):
        qf = qt_ref[0, m].astype(F32)
        bound = jnp.sqrt(jnp.sum(qf * qf, axis=0, keepdims=True) * kmax_ref[m]) * BOUND_SLACK
        mu_ref[m] = bound - REF_MARGIN

    c_diag = i * ATT_DIAG

    def chunk_of(e):
        if isinstance(e, int) and e < ATT_DIAG:
            return c_diag + e
        rest = e - ATT_DIAG
        return jnp.where(e < ATT_DIAG, c_diag + e, rest + ATT_DIAG * (rest >= c_diag).astype(jnp.int32))

    def side_of(c):
        return (c < c_diag).astype(jnp.int32) - (c >= c_diag + ATT_DIAG).astype(jnp.int32)

    p_bufs = (p_a, p_b)

    def probs(e, p_ref):
        c = chunk_of(e)
        side = side_of(c)
        sidef = side.astype(F32)
        gap = jnp.abs(c * ATT_TK - i * ATT_TQ).astype(F32)
        cvec = -slope * (sidef * sidef * gap + sidef * q_off)
        for m in range(2):
            st = jnp.dot(ka_ref[0, m, c], qv_ref[1 + side, m], preferred_element_type=F32)
            if isinstance(e, int) and e < ATT_DIAG:
                key_off = lax.broadcasted_iota(jnp.int32, (ATT_TK, ATT_TQ), 0) + e * ATT_TK
                qry_off = lax.broadcasted_iota(jnp.int32, (ATT_TK, ATT_TQ), 1)
                st = st - slope * jnp.abs(key_off - qry_off).astype(F32)
            p_ref[m] = jnp.exp2(st - (mu_ref[m] - cvec)).astype(BF16)

    def group(e0, count, probs_last):
        sums = [None, None]
        for u in range(count):
            if u < count - 1 or probs_last:
                probs(e0 + u + 1, p_bufs[(u + 1) % 2])
            c = chunk_of(e0 + u)
            for m in range(2):
                pv = jnp.dot(vt_ref[0, c], p_bufs[u % 2][m], preferred_element_type=F32)
                sums[m] = pv if sums[m] is None else sums[m] + pv
        for m in range(2):
            acc_ref[m] += sums[m]

    def loop_body(it, carry):
        group(it * unroll, unroll, True)
        return carry

    def one_pass(_):
        acc_ref[...] = jnp.zeros(acc_ref.shape, F32)
        n_groups = nchunk // unroll
        probs(0, p_a)
        group(0, unroll, True)
        lax.fori_loop(1, n_groups - 1, loop_body, 0)
        group((n_groups - 1) * unroll, unroll, False)
        lowest = None
        for m in range(2):
            den = acc_ref[m, HEAD_V:HEAD_V + 1, :]
            mu_ref[m] = jnp.where(den < REF_FLOOR, mu_ref[m] - 2.0 * REF_MARGIN, mu_ref[m])
            lowest = jnp.min(den) if lowest is None else jnp.minimum(lowest, jnp.min(den))
        return (lowest < REF_FLOOR).astype(jnp.int32)

    lax.while_loop(lambda retry: retry > 0, one_pass, jnp.int32(1))

    lam = (jnp.exp(jnp.sum(lq1_ref[...] * lk1_ref[...], axis=-1, keepdims=True))
           - jnp.exp(jnp.sum(lq2_ref[...] * lk2_ref[...], axis=-1, keepdims=True)) + lambda_init)
    a1 = acc_ref[0]
    a2 = acc_ref[1]
    o = a1[0:HEAD_V] / a1[HEAD_V:HEAD_V + 1] - lam * (a2[0:HEAD_V] / a2[HEAD_V:HEAD_V + 1])
    r = lax.rsqrt(jnp.mean(o * o, axis=0, keepdims=True) + NORM_EPS)
    y = (o * r * subw_ref[...]) * (1.0 - lambda_init)
    o_ref[...] = (y.T * jax.nn.silu(g_ref[...])).astype(o_ref.dtype)


def _diff_attn(qt, ka, vt, lq1, lk1, lq2, lk2, subw, g, lambda_init):
    s = qt.shape[-1]
    g_col0 = (g.shape[1] - DIFF_WIDTH) // HEAD_V
    nchunk = s // ATT_TK
    unroll = min(ATT_UNROLL, nchunk // 2)
    assert unroll % 2 == 0 and nchunk % unroll == 0 and unroll >= ATT_DIAG and s % ATT_TQ == 0
    whole = jnp.asarray([LOG2E * 2.0 ** (-8.0 * (h + 1) / HEADS) for h in range(HEADS)], F32)
    pieces, rest = [], whole
    for _ in range(COEF_PARTS):
        pieces.append(rest.astype(BF16).astype(F32))
        rest = rest - pieces[-1]
    slopes = jnp.stack([whole] + pieces, axis=1).reshape(-1)
    vec = pl.BlockSpec((1, HEAD_DIM), lambda h, i: (0, 0))
    return pl.pallas_call(
        functools.partial(_attn_kernel, lambda_init=lambda_init, nchunk=nchunk, unroll=unroll),
        grid=(HEADS, s // ATT_TQ),
        in_specs=[pl.BlockSpec(memory_space=pltpu.SMEM),
                  pl.BlockSpec((1, 2, LANES, ATT_TQ), lambda h, i: (h, 0, 0, i)),
                  pl.BlockSpec((1, 2, nchunk, ATT_TK, LANES), lambda h, i: (h, 0, 0, 0, 0)),
                  pl.BlockSpec((1, nchunk, V_ROWS, ATT_TK), lambda h, i: (h, 0, 0, 0)),
                  vec, vec, vec, vec,
                  pl.BlockSpec((HEAD_V, 1), lambda h, i: (0, 0)),
                  pl.BlockSpec((ATT_TQ, HEAD_V), lambda h, i: (i, g_col0 + h))],
        out_specs=pl.BlockSpec((ATT_TQ, HEAD_V), lambda h, i: (i, h)),
        out_shape=jax.ShapeDtypeStruct((s, DIFF_WIDTH), BF16),
        scratch_shapes=[pltpu.VMEM((2, V_ROWS, ATT_TQ), F32), pltpu.VMEM((2, 1, ATT_TQ), F32),
                        pltpu.VMEM((3, 2, LANES, ATT_TQ), BF16), pltpu.SMEM((2,), F32),
                        pltpu.VMEM((2, ATT_TK, ATT_TQ), BF16), pltpu.VMEM((2, ATT_TK, ATT_TQ), BF16)],
        compiler_params=_params(2),
        name="diff_attn",
    )(slopes, qt, ka, vt, lq1, lk1, lq2, lk2, subw.reshape(HEAD_V, 1), g)


def _pool_kernel(prev_ref, cur_ref, next_ref, w_ref, scale_ref, g_ref, o_ref, ext_ref, *, tm, seq):
    i = pl.program_id(0)
    last = pl.num_programs(0) - 1
    zero_halo = jnp.zeros((POOL_HALO, POOL_WIDTH), F32)
    ext_ref[0:POOL_HALO] = jnp.where(i > 0, prev_ref[...], zero_halo)
    ext_ref[POOL_HALO:POOL_HALO + tm] = cur_ref[...]
    ext_ref[POOL_HALO + tm:POOL_HALO + tm + POOL_HALO] = jnp.where(i < last, next_ref[...], zero_halo)
    t = lax.broadcasted_iota(jnp.int32, (tm, 1), 0) + i * tm
    for g, w in enumerate(POOL_WINDOWS):
        cols = pl.ds(g * POOL_GROUP, POOL_GROUP)
        win = ext_ref[pl.ds(POOL_HALO - w // 2, tm), cols]
        for d in range(-(w // 2) + 1, w - w // 2):
            win = win + ext_ref[pl.ds(POOL_HALO + d, tm), cols]
        lo = jnp.maximum(t - w // 2, 0)
        hi = jnp.minimum(t + (w - w // 2) - 1, seq - 1)
        cnt = (hi - lo + 1).astype(F32)
        pooled = win / cnt - cur_ref[:, cols]
        y = jnp.dot(pooled.astype(BF16), w_ref[g], preferred_element_type=F32)
        o_ref[:, cols] = (y * scale_ref[:, cols] * jax.nn.silu(g_ref[:, cols])).astype(o_ref.dtype)


def _pool_mix(u, w_pool, pool_scale, g):
    s, width = u.shape
    tm = 512
    hb = tm // POOL_HALO
    nhalo = s // POOL_HALO
    return pl.pallas_call(
        functools.partial(_pool_kernel, tm=tm, seq=s),
        grid=(s // tm,),
        in_specs=[pl.BlockSpec((POOL_HALO, width), lambda i: (jnp.maximum(i * hb - 1, 0), 0)),
                  pl.BlockSpec((tm, width), lambda i: (i, 0)),
                  pl.BlockSpec((POOL_HALO, width), lambda i: (jnp.minimum((i + 1) * hb, nhalo - 1), 0)),
                  pl.BlockSpec(w_pool.shape, lambda i: (0, 0, 0)),
                  pl.BlockSpec((1, width), lambda i: (0, 0)),
                  pl.BlockSpec((tm, width), lambda i: (i, 0))],
        out_specs=pl.BlockSpec((tm, width), lambda i: (i, 0)),
        out_shape=jax.ShapeDtypeStruct((s, width), BF16),
        scratch_shapes=[pltpu.VMEM((tm + 2 * POOL_HALO, width), F32)],
        compiler_params=_params(1),
        name="pool_mix",
    )(u, u, u, w_pool, pool_scale, g)


def _out_kernel(*refs, widths, tm, modulate_next):
    n = len(widths)
    y_refs = refs[:n]
    w_ref, x_ref, gate_ref = refs[n:n + 3]
    if modulate_next:
        nw_ref, shift_ref, scale_ref, o_ref, h_ref = refs[n + 3:]
        mul = nw_ref[...]
        one_plus_scale = 1.0 + scale_ref[...]
    else:
        (o_ref,) = refs[n + 3:]
    for sb in range(tm // OUT_SUB):
        rows = slice(sb * OUT_SUB, (sb + 1) * OUT_SUB)
        acc = None
        off = 0
        for y_ref, wd in zip(y_refs, widths):
            part = jnp.dot(y_ref[rows, :].astype(BF16), w_ref[off:off + wd, :], preferred_element_type=F32)
            acc = part if acc is None else acc + part
            off += wd
        xn = x_ref[rows, :] + gate_ref[...] * acc
        o_ref[rows, :] = xn
        if modulate_next:
            r = lax.rsqrt(jnp.mean(xn * xn, axis=-1, keepdims=True) + NORM_EPS)
            h_ref[rows, :] = ((xn * r * mul) * one_plus_scale + shift_ref[...]).astype(BF16)


def _out_proj(ys, w, x, gate, next_mod=None):
    s, d = x.shape
    k = w.shape[0]
    tm = min(512, s)
    widths = tuple(y.shape[1] for y in ys)
    vec = pl.BlockSpec((1, d), lambda i: (0, 0))
    rows = pl.BlockSpec((tm, d), lambda i: (i, 0))
    modulate_next = next_mod is not None
    return pl.pallas_call(
        functools.partial(_out_kernel, widths=widths, tm=tm, modulate_next=modulate_next),
        grid=(s // tm,),
        in_specs=[pl.BlockSpec((tm, wd), lambda i: (i, 0)) for wd in widths]
        + [pl.BlockSpec((k, d), lambda i: (0, 0)), rows, vec] + ([vec, vec, vec] if modulate_next else []),
        out_specs=[rows, rows] if modulate_next else rows,
        out_shape=([jax.ShapeDtypeStruct((s, d), F32), jax.ShapeDtypeStruct((s, d), BF16)] if modulate_next
                   else jax.ShapeDtypeStruct((s, d), F32)),
        compiler_params=_params(1),
        name="out_proj",
    )(*ys, w, x, gate, *(next_mod if modulate_next else ()))


def _dft_split(s):
    n1 = 1 << (int(math.log2(s)) // 2)
    n2 = s // n1
    assert n1 * n2 == s and n1 % SUBLANES == 0 and n2 % SUBLANES == 0
    return n1, n2


def _angles(num, den):
    ang = (2.0 * math.pi / den) * (num % den).astype(F32)
    return jnp.cos(ang), jnp.sin(ang)


def _fold_cs_kernel(cs_ref, wf_ref, o_ref):
    wf = wf_ref[...]
    o_ref[:, 0:FOURIER_GROUP] = jnp.dot(cs_ref[0], wf, preferred_element_type=F32,
                                        precision=lax.Precision.HIGHEST)
    o_ref[:, FOURIER_GROUP:] = jnp.dot(cs_ref[1], wf, preferred_element_type=F32,
                                       precision=lax.Precision.HIGHEST)


def _fold_channel_dft(w_fourier_j, seq):
    fg = FOURIER_GROUP
    idx = jnp.arange(fg, dtype=jnp.int32)
    cc, sc = _angles(idx[:, None] * idx[None, :], fg)
    cs = jnp.stack([cc, sc]) * (1.0 / math.sqrt(seq * fg))
    return pl.pallas_call(
        _fold_cs_kernel,
        grid=(FOURIER_GROUPS,),
        in_specs=[pl.BlockSpec((2, fg, fg), lambda g: (0, 0, 0)),
                  pl.BlockSpec((None, fg, fg), lambda g: (g, 0, 0))],
        out_specs=pl.BlockSpec((None, fg, 2 * fg), lambda g: (g, 0, 0)),
        out_shape=jax.ShapeDtypeStruct((FOURIER_GROUPS, fg, 2 * fg), F32),
        compiler_params=_params(1),
        name="fold_channel_dft",
    )(cs, w_fourier_j)


def _fold_win_kernel(w_ref, ab_ref, p_ref, q_ref):
    r = jnp.dot(w_ref[...], ab_ref[...].astype(BF16), preferred_element_type=F32)
    p_ref[...] = r[:, 0:FOURIER_GROUP].astype(BF16)
    q_ref[...] = r[:, FOURIER_GROUP:].astype(BF16)


def _fold_in_proj(w_in_u, ab):
    d = w_in_u.shape[0]
    fg = FOURIER_GROUP
    ospec = pl.BlockSpec((d, fg), lambda g: (0, g))
    return pl.pallas_call(
        _fold_win_kernel,
        grid=(FOURIER_GROUPS,),
        in_specs=[pl.BlockSpec((d, fg), lambda g: (0, g)),
                  pl.BlockSpec((None, fg, 2 * fg), lambda g: (g, 0, 0))],
        out_specs=[ospec, ospec],
        out_shape=[jax.ShapeDtypeStruct((d, D_MODEL), BF16)] * 2,
        compiler_params=_params(1),
        name="fold_in_proj",
    )(w_in_u, ab)


def _dft1_kernel(f_ref, p_ref, q_ref, t_ref, *, nb, n1):
    f = f_ref[...]
    for jj in range(nb):
        rows = slice(jj * n1, (jj + 1) * n1)
        rhs = jnp.concatenate([p_ref[rows, :], q_ref[rows, :]], axis=0)
        t_ref[jj] = jnp.dot(f, rhs, preferred_element_type=F32)


def _dft_stage1(pq_t, f1, n1, n2):
    s, c2 = pq_t.shape
    c = c2 // 2
    nb = SUBLANES
    cb = min(1024, c)
    ncb = c // cb
    return pl.pallas_call(
        functools.partial(_dft1_kernel, nb=nb, n1=n1),
        grid=(n2 // nb, ncb),
        in_specs=[pl.BlockSpec((2 * n1, 2 * n1), lambda b, j: (0, 0)),
                  pl.BlockSpec((nb * n1, cb), lambda b, j: (b, j)),
                  pl.BlockSpec((nb * n1, cb), lambda b, j: (b, j + ncb))],
        out_specs=pl.BlockSpec((nb, 2 * n1, cb), lambda b, j: (b, 0, j)),
        out_shape=jax.ShapeDtypeStruct((n2, 2 * n1, c), F32),
        compiler_params=_params(2),
        name="dft_stage1",
    )(f1, pq_t, pq_t)


def _dft2_kernel(l_ref, t_ref, g_ref, o_ref, *, n2, cb):
    rhs = t_ref[...].reshape(n2 * 2 * SUBLANES, cb).astype(BF16)
    y = jnp.dot(l_ref[0], rhs, preferred_element_type=F32)
    o_ref[...] = y.reshape(n2, SUBLANES, cb) * jax.nn.silu(g_ref[...])


def _dft_stage2(t_nk, l2, g, n1, n2):
    c = t_nk.shape[-1]
    cb = min(1024, c)
    row_blocks = pl.BlockSpec((n2, SUBLANES, cb), lambda g, j: (0, g, j))
    out = pl.pallas_call(
        functools.partial(_dft2_kernel, n2=n2, cb=cb),
        grid=(n1 // SUBLANES, c // cb),
        in_specs=[pl.BlockSpec((1, SUBLANES * n2, SUBLANES * 2 * n2), lambda g, j: (g, 0, 0)),
                  pl.BlockSpec((n2, 2, SUBLANES, cb), lambda g, j: (0, 0, g, j)),
                  row_blocks],
        out_specs=row_blocks,
        out_shape=jax.ShapeDtypeStruct((n2, n1, c), F32),
        compiler_params=_params(2),
        name="dft_stage2",
    )(l2, t_nk, g.reshape(n2, n1, c))
    return out.reshape(n1 * n2, c)


def _dft_matrices(s):
    n1, n2 = _dft_split(s)
    a = jnp.arange(n1, dtype=jnp.int32)
    c1, s1 = _angles(a[:, None] * a[None, :], n1)
    f1 = jnp.concatenate([jnp.concatenate([c1, -s1], axis=1),
                          jnp.concatenate([-s1, -c1], axis=1)], axis=0).astype(BF16)
    groups = n1 // SUBLANES
    g = jnp.arange(groups, dtype=jnp.int32)[:, None, None, None]
    k2 = jnp.arange(n2, dtype=jnp.int32)[None, :, None, None]
    j = jnp.arange(SUBLANES, dtype=jnp.int32)[None, None, :, None]
    nn = jnp.arange(n2, dtype=jnp.int32)[None, None, None, :]
    c2, s2 = _angles((SUBLANES * g + j + n1 * k2) * nn, s)
    cs = jnp.stack([c2, s2], axis=4).astype(BF16)
    return n1, n2, f1, _expand_twiddles(cs.reshape(groups, n2 * SUBLANES, n2 * 2))


def _expand_kernel(cs_ref, e_ref, o_ref):
    spread = jnp.dot(cs_ref[0], e_ref[...], preferred_element_type=F32)
    row = lax.broadcasted_iota(jnp.int32, spread.shape, 0)
    col = lax.broadcasted_iota(jnp.int32, spread.shape, 1)
    o_ref[0] = jnp.where(row % SUBLANES == col % SUBLANES, spread, 0.0).astype(BF16)


def _expand_twiddles(cs):
    groups, rows, cols = cs.shape
    wide = cols * SUBLANES
    e = (jnp.arange(wide, dtype=jnp.int32)[None, :] // SUBLANES == jnp.arange(cols, dtype=jnp.int32)[:, None])
    return pl.pallas_call(
        _expand_kernel,
        grid=(groups,),
        in_specs=[pl.BlockSpec((1, rows, cols), lambda g: (g, 0, 0)),
                  pl.BlockSpec((cols, wide), lambda g: (0, 0))],
        out_specs=pl.BlockSpec((1, rows, wide), lambda g: (g, 0, 0)),
        out_shape=jax.ShapeDtypeStruct((groups, rows, wide), BF16),
        compiler_params=_params(1),
        name="expand_twiddles",
    )(cs, e.astype(BF16))


def kernel(x, c, norm_w, ada_w, ada_b, w_in_ab, w_pool, pool_scale, q_norm_w, k_norm_w, lambda_q1, lambda_k1,
           lambda_q2, lambda_k2, subln_w, w_out_ab, w_in_c, w_fourier, w_out_c):
    batch, s, d = x.shape
    assert batch == 1 and d == D_MODEL and s % ATT_TK == 0
    xs = x.reshape(s, d)
    mod = _ada_mod(c, ada_w, ada_b)
    n1, n2, f1, l2 = _dft_matrices(s)
    row = lambda v: v.reshape(1, -1)

    def mod_of(i):
        return mod[i, :, 0:d], mod[i, :, d:2 * d], mod[i, :, 2 * d:3 * d]

    def finish(ys, w_out, xs, i):
        gate = mod_of(i)[2]
        if i + 1 == DEPTH:
            return _out_proj(ys, w_out, xs, gate), None
        shift, scale, _ = mod_of(i + 1)
        return _out_proj(ys, w_out, xs, gate, (row(norm_w[i + 1]), shift, scale))

    h = _modulate(xs, row(norm_w[0]), *mod_of(0)[:2])
    for i in range(DEPTH):
        j = i // 2
        if i % 2 == 0:
            w_in = w_in_ab[j].astype(BF16)
            o1, o2, o3, o4 = POOL_WIDTH, POOL_WIDTH + DIFF_WIDTH, POOL_WIDTH + 2 * DIFF_WIDTH, POOL_WIDTH + 3 * DIFF_WIDTH
            lambda_init = 0.8 - 0.6 * math.exp(-0.3 * i)
            u_pool = _matmul(h, w_in[:, :o1], F32, PROJ_TM, PROJ_TN, name="pool_in_proj")
            gte = _matmul(h, w_in[:, o4:], F32, PROJ_TM, PROJ_TN, name="gate_in_proj")
            qt, ka, vt = _qkv_proj(h, w_in[:, o1:o2], w_in[:, o2:o3], w_in[:, o3:o4],
                                   row(q_norm_w[j]), row(k_norm_w[j]))
            y_a = _pool_mix(u_pool, w_pool[j].astype(BF16), row(pool_scale[j]), gte)
            y_b = _diff_attn(qt, ka, vt, row(lambda_q1[j]), row(lambda_k1[j]), row(lambda_q2[j]),
                             row(lambda_k2[j]), subln_w[j], gte, lambda_init)
            xs, h = finish([y_a, y_b], w_out_ab[j].astype(BF16), xs, i)
        else:
            w_in = w_in_c[j].astype(BF16)
            ab = _fold_channel_dft(w_fourier[j], s)
            w_p, w_q = _fold_in_proj(w_in[:, :d], ab)
            gte = _matmul(h, w_in[:, d:], F32, PROJ_TM, PROJ_TN, name="gate_in_proj")
            h_t = h.reshape(n1, n2, d).transpose(1, 0, 2).reshape(s, d)
            pq_t = _matmul(h_t, jnp.concatenate([w_p, w_q], axis=1), BF16, PROJ_TM, PROJ_TN, name="fourier_in_proj")
            t_nk = _dft_stage1(pq_t, f1, n1, n2)
            y = _dft_stage2(t_nk.reshape(n2, 2, n1, d), l2, gte, n1, n2)
            xs, h = finish([y], w_out_c[j].astype(BF16), xs, i)
    return xs.reshape(batch, s, d)
```

```python
import functools
import math

import jax
import jax.numpy as jnp
from jax import lax
from jax.experimental import pallas as pl
from jax.experimental.pallas import tpu as pltpu

F32 = jnp.float32
BF16 = jnp.bfloat16

D_MODEL = 2048
DEPTH = 4
NORM_EPS = 1e-6

POOL_WINDOWS = (2, 4, 8, 16)
POOL_WIDTH = D_MODEL // 2
POOL_GROUP = POOL_WIDTH // len(POOL_WINDOWS)
POOL_HALO = 8

HEADS = 8
HEAD_DIM = 64
HEAD_V = 2 * HEAD_DIM
DIFF_WIDTH = HEADS * HEAD_V
AB_WIDTH = POOL_WIDTH + DIFF_WIDTH

FOURIER_GROUPS = 4
FOURIER_GROUP = D_MODEL // FOURIER_GROUPS

LANES = 128
SUBLANES = 8
BF16_ROWS = 16
POS_RADIX = 256
COEF_PARTS = 3
LOG2E = math.log2(math.e)

ATT_TQ = 512
ATT_TK = 512
ATT_DIAG = ATT_TQ // ATT_TK
ATT_UNROLL = 8
V_ROWS = HEAD_V + BF16_ROWS
REF_MARGIN = 60.0
REF_FLOOR = 2.0 ** -60
BOUND_SLACK = 1.001

PROJ_TM, PROJ_TN = 2048, 512
OUT_SUB = 128

VMEM_LIMIT = 56 * 1024 * 1024


def _params(n_axes):
    return pltpu.CompilerParams(dimension_semantics=("arbitrary",) * n_axes,
                                vmem_limit_bytes=VMEM_LIMIT)


def _mod_kernel(c_ref, w_ref, b_ref, o_ref):
    c = c_ref[...]
    o_ref[...] = jnp.sum(jax.nn.silu(c) * w_ref[...], axis=0, keepdims=True) + b_ref[...]


def _ada_mod(c, ada_w, ada_b):
    depth, d, n = ada_w.shape
    tn = 512
    return pl.pallas_call(
        _mod_kernel,
        grid=(depth, n // tn),
        in_specs=[pl.BlockSpec((d, 1), lambda i, j: (0, 0)),
                  pl.BlockSpec((None, d, tn), lambda i, j: (i, 0, j)),
                  pl.BlockSpec((None, 1, tn), lambda i, j: (i, 0, j))],
        out_specs=pl.BlockSpec((None, 1, tn), lambda i, j: (i, 0, j)),
        out_shape=jax.ShapeDtypeStruct((depth, 1, n), F32),
        compiler_params=_params(2),
        name="ada_mod",
    )(c.reshape(d, 1), ada_w, ada_b.reshape(depth, 1, n))


def _modulate_kernel(x_ref, nw_ref, shift_ref, scale_ref, o_ref):
    xf = x_ref[...]
    r = lax.rsqrt(jnp.mean(xf * xf, axis=-1, keepdims=True) + NORM_EPS)
    y = xf * r * nw_ref[...]
    y = y * (1.0 + scale_ref[...]) + shift_ref[...]
    o_ref[...] = y.astype(o_ref.dtype)


def _modulate(x, nw, shift, scale):
    s, d = x.shape
    tm = 512
    vec = pl.BlockSpec((1, d), lambda i: (0, 0))
    return pl.pallas_call(
        _modulate_kernel,
        grid=(s // tm,),
        in_specs=[pl.BlockSpec((tm, d), lambda i: (i, 0)), vec, vec, vec],
        out_specs=pl.BlockSpec((tm, d), lambda i: (i, 0)),
        out_shape=jax.ShapeDtypeStruct((s, d), BF16),
        compiler_params=_params(1),
        name="modulate",
    )(x, nw, shift, scale)


def _mm_kernel(a_ref, w_ref, o_ref, *, precision):
    o_ref[...] = jnp.dot(a_ref[...], w_ref[...], preferred_element_type=F32,
                         precision=precision).astype(o_ref.dtype)


def _matmul(a, w, out_dtype, tm, tn, precision=None, name="matmul", cols=None):
    m, k = a.shape
    col0, col1 = cols if cols is not None else (0, w.shape[1])
    n = col1 - col0
    tm, tn = min(tm, m), min(tn, n)
    assert col0 % tn == 0 and n % tn == 0
    jb = col0 // tn
    return pl.pallas_call(
        functools.partial(_mm_kernel, precision=precision),
        grid=(m // tm, n // tn),
        in_specs=[pl.BlockSpec((tm, k), lambda i, j: (i, 0)),
                  pl.BlockSpec((k, tn), lambda i, j: (0, j + jb))],
        out_specs=pl.BlockSpec((tm, tn), lambda i, j: (i, j)),
        out_shape=jax.ShapeDtypeStruct((m, n), out_dtype),
        compiler_params=_params(2),
        name=name,
    )(a, w)


def _qkv_kernel(h_ref, wq_ref, wk_ref, wv_ref, qnw_ref, knw_ref, qt_ref, ka_ref, vt_ref, *, tm):
    lane = lax.broadcasted_iota(jnp.int32, (ATT_TK, LANES), 1)
    first = lane < HEAD_DIM
    pos = lax.broadcasted_iota(jnp.int32, (ATT_TK, LANES), 0)
    pos_lo = (pos % POS_RADIX).astype(F32)
    pos_hi = (pos // POS_RADIX).astype(F32)
    pos_cols = jnp.where(lane < HEAD_DIM + COEF_PARTS, pos_lo, jnp.where(lane < HEAD_DIM + 2 * COEF_PARTS, pos_hi, 0.0))
    ones_rows = (lax.broadcasted_iota(jnp.int32, (BF16_ROWS, ATT_TK), 0) == 0).astype(BF16)

    def halves_rms(z, w):
        sq = z * z
        ss1 = jnp.sum(jnp.where(first, sq, 0.0), axis=-1, keepdims=True)
        ss2 = jnp.sum(jnp.where(first, 0.0, sq), axis=-1, keepdims=True)
        r = jnp.where(first, lax.rsqrt(ss1 / HEAD_DIM + NORM_EPS), lax.rsqrt(ss2 / HEAD_DIM + NORM_EPS))
        return z * r * w

    for cc in range(tm // ATT_TK):
        rows = slice(cc * ATT_TK, (cc + 1) * ATT_TK)
        h = h_ref[rows, :]
        zq = jnp.dot(h, wq_ref[...], preferred_element_type=F32)
        zk = jnp.dot(h, wk_ref[...], preferred_element_type=F32)
        zv = jnp.dot(h, wv_ref[...], preferred_element_type=F32)
        for hh in range(2):
            cols = slice(hh * LANES, (hh + 1) * LANES)
            qn = halves_rms(zq[:, cols], qnw_ref[...]) * (HEAD_DIM ** -0.5) * LOG2E
            kn = halves_rms(zk[:, cols], knw_ref[...])
            for m in range(2):
                qm = qn if m == 0 else pltpu.roll(qn, HEAD_DIM, axis=1)
                km = kn if m == 0 else pltpu.roll(kn, HEAD_DIM, axis=1)
                qt_ref[hh, m, :, rows] = jnp.where(first, qm, 0.0).T.astype(BF16)
                ka_ref[hh, m, cc] = jnp.where(first, km, pos_cols).astype(BF16)
            vt_ref[hh, cc, 0:HEAD_V, :] = zv[:, cols].T.astype(BF16)
            vt_ref[hh, cc, HEAD_V:V_ROWS, :] = ones_rows


def _qkv_proj(h, w, col_q, col_k, col_v, qnw, knw):
    s, d = h.shape
    tm = min(2048, s)
    nchunk = s // ATT_TK
    cpt = tm // ATT_TK
    tn = 2 * LANES
    assert col_q % tn == 0 and col_k % tn == 0 and col_v % tn == 0
    wspecs = [pl.BlockSpec((d, tn), functools.partial(lambda i, j, jb: (0, j + jb), jb=c0 // tn))
              for c0 in (col_q, col_k, col_v)]
    vec = pl.BlockSpec((1, LANES), lambda i, j: (0, 0))
    return pl.pallas_call(
        functools.partial(_qkv_kernel, tm=tm),
        grid=(s // tm, HEADS // 2),
        in_specs=[pl.BlockSpec((tm, d), lambda i, j: (i, 0)), *wspecs, vec, vec],
        out_specs=[pl.BlockSpec((2, 2, LANES, tm), lambda i, j: (j, 0, 0, i)),
                   pl.BlockSpec((2, 2, cpt, ATT_TK, LANES), lambda i, j: (j, 0, i, 0, 0)),
                   pl.BlockSpec((2, cpt, V_ROWS, ATT_TK), lambda i, j: (j, i, 0, 0))],
        out_shape=[jax.ShapeDtypeStruct((HEADS, 2, LANES, s), BF16),
                   jax.ShapeDtypeStruct((HEADS, 2, nchunk, ATT_TK, LANES), BF16),
                   jax.ShapeDtypeStruct((HEADS, nchunk, V_ROWS, ATT_TK), BF16)],
        compiler_params=_params(2),
        name="qkv_proj",
    )(h, w, w, w, jnp.tile(qnw, (1, 2)), jnp.tile(knw, (1, 2)))


def _attn_kernel(slopes_ref, qt_ref, ka_ref, vt_ref, lq1_ref, lk1_ref, lq2_ref, lk2_ref, subw_ref, g_ref,
                 o_ref, acc_ref, mu_ref, qv_ref, kmax_ref, p_a, p_b, *, lambda_init, nchunk, unroll):
    hd = pl.program_id(0)
    i = pl.program_id(1)
    base = hd * (1 + COEF_PARTS)
    slope = slopes_ref[base]
    rowq = lax.broadcasted_iota(jnp.int32, (LANES, ATT_TQ), 0)
    q_off = lax.broadcasted_iota(jnp.int32, (1, ATT_TQ), 1).astype(F32)

    row1 = lax.broadcasted_iota(jnp.int32, (LANES, 1), 0)
    coef_col = jnp.zeros((LANES, 1), F32)
    for t in range(COEF_PARTS):
        piece = slopes_ref[base + 1 + t]
        coef_col = jnp.where(row1 == HEAD_DIM + t, piece, coef_col)
        coef_col = jnp.where(row1 == HEAD_DIM + COEF_PARTS + t, piece * POS_RADIX, coef_col)
    for m in range(2):
        qf = qt_ref[0, m].astype(F32)
        for kind, sign in ((0, -1.0), (1, 0.0), (2, 1.0)):
            qv_ref[kind, m] = jnp.where(rowq >= HEAD_DIM, sign * coef_col, qf).astype(BF16)

    @pl.when(i == 0)
    def _():
        lane = lax.broadcasted_iota(jnp.int32, (ATT_TK, LANES), 1)
        for m in range(2):
            def widest(c, best):
                kc = ka_ref[0, m, c].astype(F32)
                return jnp.maximum(best, jnp.sum(jnp.where(lane < HEAD_DIM, kc * kc, 0.0), axis=1, keepdims=True))
            kmax_ref[m] = jnp.max(lax.fori_loop(0, nchunk, widest, jnp.zeros((ATT_TK, 1), F32)))

    for m in range(2):
        qf = qt_ref[0, m].astype(F32)
        bound = jnp.sqrt(jnp.sum(qf * qf, axis=0, keepdims=True) * kmax_ref[m]) * BOUND_SLACK
        mu_ref[m] = bound - REF_MARGIN

    c_diag = i * ATT_DIAG

    def chunk_of(e):
        if isinstance(e, int) and e < ATT_DIAG:
            return c_diag + e
        rest = e - ATT_DIAG
        return jnp.where(e < ATT_DIAG, c_diag + e, rest + ATT_DIAG * (rest >= c_diag).astype(jnp.int32))

    def side_of(c):
        return (c < c_diag).astype(jnp.int32) - (c >= c_diag + ATT_DIAG).astype(jnp.int32)

    p_bufs = (p_a, p_b)

    def probs(e, p_ref):
        c = chunk_of(e)
        side = side_of(c)
        sidef = side.astype(F32)
        gap = jnp.abs(c * ATT_TK - i * ATT_TQ).astype(F32)
        cvec = -slope * (sidef * sidef * gap + sidef * q_off)
        for m in range(2):
            st = jnp.dot(ka_ref[0, m, c], qv_ref[1 + side, m], preferred_element_type=F32)
            if isinstance(e, int) and e < ATT_DIAG:
                key_off = lax.broadcasted_iota(jnp.int32, (ATT_TK, ATT_TQ), 0) + e * ATT_TK
                qry_off = lax.broadcasted_iota(jnp.int32, (ATT_TK, ATT_TQ), 1)
                st = st - slope * jnp.abs(key_off - qry_off).astype(F32)
            p_ref[m] = jnp.exp2(st - (mu_ref[m] - cvec)).astype(BF16)

    def group(e0, count, probs_last):
        sums = [None, None]
        for u in range(count):
            if u < count - 1 or probs_last:
                probs(e0 + u + 1, p_bufs[(u + 1) % 2])
            c = chunk_of(e0 + u)
            for m in range(2):
                pv = jnp.dot(vt_ref[0, c], p_bufs[u % 2][m], preferred_element_type=F32)
                sums[m] = pv if sums[m] is None else sums[m] + pv
        for m in range(2):
            acc_ref[m] += sums[m]

    def loop_body(it, carry):
        group(it * unroll, unroll, True)
        return carry

    def one_pass(_):
        acc_ref[...] = jnp.zeros(acc_ref.shape, F32)
        n_groups = nchunk // unroll
        probs(0, p_a)
        group(0, unroll, True)
        lax.fori_loop(1, n_groups - 1, loop_body, 0)
        group((n_groups - 1) * unroll, unroll, False)
        lowest = None
        for m in range(2):
            den = acc_ref[m, HEAD_V:HEAD_V + 1, :]
            mu_ref[m] = jnp.where(den < REF_FLOOR, mu_ref[m] - 2.0 * REF_MARGIN, mu_ref[m])
            lowest = jnp.min(den) if lowest is None else jnp.minimum(lowest, jnp.min(den))
        return (lowest < REF_FLOOR).astype(jnp.int32)

    lax.while_loop(lambda retry: retry > 0, one_pass, jnp.int32(1))

    lam = (jnp.exp(jnp.sum(lq1_ref[...] * lk1_ref[...], axis=-1, keepdims=True))
           - jnp.exp(jnp.sum(lq2_ref[...] * lk2_ref[...], axis=-1, keepdims=True)) + lambda_init)
    a1 = acc_ref[0]
    a2 = acc_ref[1]
    o = a1[0:HEAD_V] / a1[HEAD_V:HEAD_V + 1] - lam * (a2[0:HEAD_V] / a2[HEAD_V:HEAD_V + 1])
    r = lax.rsqrt(jnp.mean(o * o, axis=0, keepdims=True) + NORM_EPS)
    y = (o * r * subw_ref[...]) * (1.0 - lambda_init)
    o_ref[...] = (y.T * jax.nn.silu(g_ref[...])).astype(o_ref.dtype)


def _diff_attn(qt, ka, vt, lq1, lk1, lq2, lk2, subw, g, lambda_init):
    s = qt.shape[-1]
    g_col0 = (g.shape[1] - DIFF_WIDTH) // HEAD_V
    nchunk = s // ATT_TK
    unroll = min(ATT_UNROLL, nchunk // 2)
    assert unroll % 2 == 0 and nchunk % unroll == 0 and unroll >= ATT_DIAG and s % ATT_TQ == 0
    whole = jnp.asarray([LOG2E * 2.0 ** (-8.0 * (h + 1) / HEADS) for h in range(HEADS)], F32)
    pieces, rest = [], whole
    for _ in range(COEF_PARTS):
        pieces.append(rest.astype(BF16).astype(F32))
        rest = rest - pieces[-1]
    slopes = jnp.stack([whole] + pieces, axis=1).reshape(-1)
    vec = pl.BlockSpec((1, HEAD_DIM), lambda h, i: (0, 0))
    return pl.pallas_call(
        functools.partial(_attn_kernel, lambda_init=lambda_init, nchunk=nchunk, unroll=unroll),
        grid=(HEADS, s // ATT_TQ),
        in_specs=[pl.BlockSpec(memory_space=pltpu.SMEM),
                  pl.BlockSpec((1, 2, LANES, ATT_TQ), lambda h, i: (h, 0, 0, i)),
                  pl.BlockSpec((1, 2, nchunk, ATT_TK, LANES), lambda h, i: (h, 0, 0, 0, 0)),
                  pl.BlockSpec((1, nchunk, V_ROWS, ATT_TK), lambda h, i: (h, 0, 0, 0)),
                  vec, vec, vec, vec,
                  pl.BlockSpec((HEAD_V, 1), lambda h, i: (0, 0)),
                  pl.BlockSpec((ATT_TQ, HEAD_V), lambda h, i: (i, g_col0 + h))],
        out_specs=pl.BlockSpec((ATT_TQ, HEAD_V), lambda h, i: (i, h)),
        out_shape=jax.ShapeDtypeStruct((s, DIFF_WIDTH), BF16),
        scratch_shapes=[pltpu.VMEM((2, V_ROWS, ATT_TQ), F32), pltpu.VMEM((2, 1, ATT_TQ), F32),
                        pltpu.VMEM((3, 2, LANES, ATT_TQ), BF16), pltpu.SMEM((2,), F32),
                        pltpu.VMEM((2, ATT_TK, ATT_TQ), BF16), pltpu.VMEM((2, ATT_TK, ATT_TQ), BF16)],
        compiler_params=_params(2),
        name="diff_attn",
    )(slopes, qt, ka, vt, lq1, lk1, lq2, lk2, subw.reshape(HEAD_V, 1), g)


def _pool_kernel(prev_ref, cur_ref, next_ref, w_ref, scale_ref, g_ref, o_ref, ext_ref, *, tm, seq):
    i = pl.program_id(0)
    last = pl.num_programs(0) - 1
    zero_halo = jnp.zeros((POOL_HALO, POOL_WIDTH), F32)
    ext_ref[0:POOL_HALO] = jnp.where(i > 0, prev_ref[...], zero_halo)
    ext_ref[POOL_HALO:POOL_HALO + tm] = cur_ref[...]
    ext_ref[POOL_HALO + tm:POOL_HALO + tm + POOL_HALO] = jnp.where(i < last, next_ref[...], zero_halo)
    t = lax.broadcasted_iota(jnp.int32, (tm, 1), 0) + i * tm
    for g, w in enumerate(POOL_WINDOWS):
        cols = pl.ds(g * POOL_GROUP, POOL_GROUP)
        win = ext_ref[pl.ds(POOL_HALO - w // 2, tm), cols]
        for d in range(-(w // 2) + 1, w - w // 2):
            win = win + ext_ref[pl.ds(POOL_HALO + d, tm), cols]
        lo = jnp.maximum(t - w // 2, 0)
        hi = jnp.minimum(t + (w - w // 2) - 1, seq - 1)
        cnt = (hi - lo + 1).astype(F32)
        pooled = win / cnt - cur_ref[:, cols]
        y = jnp.dot(pooled.astype(BF16), w_ref[g], preferred_element_type=F32)
        o_ref[:, cols] = (y * scale_ref[:, cols] * jax.nn.silu(g_ref[:, cols])).astype(o_ref.dtype)


def _pool_mix(u, w_pool, pool_scale, g):
    s, width = u.shape
    tm = 512
    hb = tm // POOL_HALO
    nhalo = s // POOL_HALO
    return pl.pallas_call(
        functools.partial(_pool_kernel, tm=tm, seq=s),
        grid=(s // tm,),
        in_specs=[pl.BlockSpec((POOL_HALO, width), lambda i: (jnp.maximum(i * hb - 1, 0), 0)),
                  pl.BlockSpec((tm, width), lambda i: (i, 0)),
                  pl.BlockSpec((POOL_HALO, width), lambda i: (jnp.minimum((i + 1) * hb, nhalo - 1), 0)),
                  pl.BlockSpec(w_pool.shape, lambda i: (0, 0, 0)),
                  pl.BlockSpec((1, width), lambda i: (0, 0)),
                  pl.BlockSpec((tm, width), lambda i: (i, 0))],
        out_specs=pl.BlockSpec((tm, width), lambda i: (i, 0)),
        out_shape=jax.ShapeDtypeStruct((s, width), BF16),
        scratch_shapes=[pltpu.VMEM((tm + 2 * POOL_HALO, width), F32)],
        compiler_params=_params(1),
        name="pool_mix",
    )(u, u, u, w_pool, pool_scale, g)


def _out_kernel(*refs, widths, tm, modulate_next):
    n = len(widths)
    y_refs = refs[:n]
    w_ref, x_ref, gate_ref = refs[n:n + 3]
    if modulate_next:
        nw_ref, shift_ref, scale_ref, o_ref, h_ref = refs[n + 3:]
        mul = nw_ref[...]
        one_plus_scale = 1.0 + scale_ref[...]
    else:
        (o_ref,) = refs[n + 3:]
    for sb in range(tm // OUT_SUB):
        rows = slice(sb * OUT_SUB, (sb + 1) * OUT_SUB)
        acc = None
        off = 0
        for y_ref, wd in zip(y_refs, widths):
            part = jnp.dot(y_ref[rows, :].astype(BF16), w_ref[off:off + wd, :], preferred_element_type=F32)
            acc = part if acc is None else acc + part
            off += wd
        xn = x_ref[rows, :] + gate_ref[...] * acc
        o_ref[rows, :] = xn
        if modulate_next:
            r = lax.rsqrt(jnp.mean(xn * xn, axis=-1, keepdims=True) + NORM_EPS)
            h_ref[rows, :] = ((xn * r * mul) * one_plus_scale + shift_ref[...]).astype(BF16)


def _out_proj(ys, w, x, gate, next_mod=None):
    s, d = x.shape
    k = w.shape[0]
    tm = min(512, s)
    widths = tuple(y.shape[1] for y in ys)
    vec = pl.BlockSpec((1, d), lambda i: (0, 0))
    rows = pl.BlockSpec((tm, d), lambda i: (i, 0))
    modulate_next = next_mod is not None
    return pl.pallas_call(
        functools.partial(_out_kernel, widths=widths, tm=tm, modulate_next=modulate_next),
        grid=(s // tm,),
        in_specs=[pl.BlockSpec((tm, wd), lambda i: (i, 0)) for wd in widths]
        + [pl.BlockSpec((k, d), lambda i: (0, 0)), rows, vec] + ([vec, vec, vec] if modulate_next else []),
        out_specs=[rows, rows] if modulate_next else rows,
        out_shape=([jax.ShapeDtypeStruct((s, d), F32), jax.ShapeDtypeStruct((s, d), BF16)] if modulate_next
                   else jax.ShapeDtypeStruct((s, d), F32)),
        compiler_params=_params(1),
        name="out_proj",
    )(*ys, w, x, gate, *(next_mod if modulate_next else ()))


def _dft_split(s):
    n1 = 1 << (int(math.log2(s)) // 2)
    n2 = s // n1
    assert n1 * n2 == s and n1 % SUBLANES == 0 and n2 % SUBLANES == 0
    return n1, n2


def _angles(num, den):
    ang = (2.0 * math.pi / den) * (num % den).astype(F32)
    return jnp.cos(ang), jnp.sin(ang)


def _fold_cs_kernel(cs_ref, wf_ref, o_ref):
    wf = wf_ref[...]
    o_ref[:, 0:FOURIER_GROUP] = jnp.dot(cs_ref[0], wf, preferred_element_type=F32,
                                        precision=lax.Precision.HIGHEST)
    o_ref[:, FOURIER_GROUP:] = jnp.dot(cs_ref[1], wf, preferred_element_type=F32,
                                       precision=lax.Precision.HIGHEST)


def _fold_channel_dft(w_fourier_j, seq):
    fg = FOURIER_GROUP
    idx = jnp.arange(fg, dtype=jnp.int32)
    cc, sc = _angles(idx[:, None] * idx[None, :], fg)
    cs = jnp.stack([cc, sc]) * (1.0 / math.sqrt(seq * fg))
    return pl.pallas_call(
        _fold_cs_kernel,
        grid=(FOURIER_GROUPS,),
        in_specs=[pl.BlockSpec((2, fg, fg), lambda g: (0, 0, 0)),
                  pl.BlockSpec((None, fg, fg), lambda g: (g, 0, 0))],
        out_specs=pl.BlockSpec((None, fg, 2 * fg), lambda g: (g, 0, 0)),
        out_shape=jax.ShapeDtypeStruct((FOURIER_GROUPS, fg, 2 * fg), F32),
        compiler_params=_params(1),
        name="fold_channel_dft",
    )(cs, w_fourier_j)


def _fold_win_kernel(w_ref, ab_ref, p_ref, q_ref):
    r = jnp.dot(w_ref[...], ab_ref[...].astype(BF16), preferred_element_type=F32)
    p_ref[...] = r[:, 0:FOURIER_GROUP].astype(BF16)
    q_ref[...] = r[:, FOURIER_GROUP:].astype(BF16)


def _fold_in_proj(w_in_u, ab):
    d = w_in_u.shape[0]
    fg = FOURIER_GROUP
    ospec = pl.BlockSpec((d, fg), lambda g: (0, g))
    return pl.pallas_call(
        _fold_win_kernel,
        grid=(FOURIER_GROUPS,),
        in_specs=[pl.BlockSpec((d, fg), lambda g: (0, g)),
                  pl.BlockSpec((None, fg, 2 * fg), lambda g: (g, 0, 0))],
        out_specs=[ospec, ospec],
        out_shape=[jax.ShapeDtypeStruct((d, D_MODEL), BF16)] * 2,
        compiler_params=_params(1),
        name="fold_in_proj",
    )(w_in_u, ab)


def _dft1_kernel(f_ref, p_ref, q_ref, t_ref, *, nb, n1):
    f = f_ref[...]
    for jj in range(nb):
        rows = slice(jj * n1, (jj + 1) * n1)
        rhs = jnp.concatenate([p_ref[rows, :], q_ref[rows, :]], axis=0)
        t_ref[jj] = jnp.dot(f, rhs, preferred_element_type=F32)


def _dft_stage1(pq_t, f1, n1, n2):
    s, c2 = pq_t.shape
    c = c2 // 2
    nb = SUBLANES
    cb = min(1024, c)
    ncb = c // cb
    return pl.pallas_call(
        functools.partial(_dft1_kernel, nb=nb, n1=n1),
        grid=(n2 // nb, ncb),
        in_specs=[pl.BlockSpec((2 * n1, 2 * n1), lambda b, j: (0, 0)),
                  pl.BlockSpec((nb * n1, cb), lambda b, j: (b, j)),
                  pl.BlockSpec((nb * n1, cb), lambda b, j: (b, j + ncb))],
        out_specs=pl.BlockSpec((nb, 2 * n1, cb), lambda b, j: (b, 0, j)),
        out_shape=jax.ShapeDtypeStruct((n2, 2 * n1, c), F32),
        compiler_params=_params(2),
        name="dft_stage1",
    )(f1, pq_t, pq_t)


def _dft2_kernel(l_ref, t_ref, g_ref, o_ref, *, n2, cb):
    rhs = t_ref[...].reshape(n2 * 2 * SUBLANES, cb).astype(BF16)
    y = jnp.dot(l_ref[0], rhs, preferred_element_type=F32)
    o_ref[...] = y.reshape(n2, SUBLANES, cb) * jax.nn.silu(g_ref[...])


def _dft_stage2(t_nk, l2, g, n1, n2):
    c = t_nk.shape[-1]
    cb = min(1024, c)
    row_blocks = pl.BlockSpec((n2, SUBLANES, cb), lambda g, j: (0, g, j))
    out = pl.pallas_call(
        functools.partial(_dft2_kernel, n2=n2, cb=cb),
        grid=(n1 // SUBLANES, c // cb),
        in_specs=[pl.BlockSpec((1, SUBLANES * n2, SUBLANES * 2 * n2), lambda g, j: (g, 0, 0)),
                  pl.BlockSpec((n2, 2, SUBLANES, cb), lambda g, j: (0, 0, g, j)),
                  row_blocks],
        out_specs=row_blocks,
        out_shape=jax.ShapeDtypeStruct((n2, n1, c), F32),
        compiler_params=_params(2),
        name="dft_stage2",
    )(l2, t_nk, g.reshape(n2, n1, c))
    return out.reshape(n1 * n2, c)


def _dft_matrices(s):
    n1, n2 = _dft_split(s)
    a = jnp.arange(n1, dtype=jnp.int32)
    c1, s1 = _angles(a[:, None] * a[None, :], n1)
    f1 = jnp.concatenate([jnp.concatenate([c1, -s1], axis=1),
                          jnp.concatenate([-s1, -c1], axis=1)], axis=0).astype(BF16)
    groups = n1 // SUBLANES
    g = jnp.arange(groups, dtype=jnp.int32)[:, None, None, None]
    k2 = jnp.arange(n2, dtype=jnp.int32)[None, :, None, None]
    j = jnp.arange(SUBLANES, dtype=jnp.int32)[None, None, :, None]
    nn = jnp.arange(n2, dtype=jnp.int32)[None, None, None, :]
    c2, s2 = _angles((SUBLANES * g + j + n1 * k2) * nn, s)
    cs = jnp.stack([c2, s2], axis=4).astype(BF16)
    return n1, n2, f1, _expand_twiddles(cs.reshape(groups, n2 * SUBLANES, n2 * 2))


def _expand_kernel(cs_ref, e_ref, o_ref):
    spread = jnp.dot(cs_ref[0], e_ref[...], preferred_element_type=F32)
    row = lax.broadcasted_iota(jnp.int32, spread.shape, 0)
    col = lax.broadcasted_iota(jnp.int32, spread.shape, 1)
    o_ref[0] = jnp.where(row % SUBLANES == col % SUBLANES, spread, 0.0).astype(BF16)


def _expand_twiddles(cs):
    groups, rows, cols = cs.shape
    wide = cols * SUBLANES
    e = (jnp.arange(wide, dtype=jnp.int32)[None, :] // SUBLANES == jnp.arange(cols, dtype=jnp.int32)[:, None])
    return pl.pallas_call(
        _expand_kernel,
        grid=(groups,),
        in_specs=[pl.BlockSpec((1, rows, cols), lambda g: (g, 0, 0)),
                  pl.BlockSpec((cols, wide), lambda g: (0, 0))],
        out_specs=pl.BlockSpec((1, rows, wide), lambda g: (g, 0, 0)),
        out_shape=jax.ShapeDtypeStruct((groups, rows, wide), BF16),
        compiler_params=_params(1),
        name="expand_twiddles",
    )(cs, e.astype(BF16))


def kernel(x, c, norm_w, ada_w, ada_b, w_in_ab, w_pool, pool_scale, q_norm_w, k_norm_w, lambda_q1, lambda_k1,
           lambda_q2, lambda_k2, subln_w, w_out_ab, w_in_c, w_fourier, w_out_c):
    batch, s, d = x.shape
    assert batch == 1 and d == D_MODEL and s % ATT_TK == 0
    xs = x.reshape(s, d)
    mod = _ada_mod(c, ada_w, ada_b)
    n1, n2, f1, l2 = _dft_matrices(s)
    row = lambda v: v.reshape(1, -1)

    def mod_of(i):
        return mod[i, :, 0:d], mod[i, :, d:2 * d], mod[i, :, 2 * d:3 * d]

    def finish(ys, w_out, xs, i):
        gate = mod_of(i)[2]
        if i + 1 == DEPTH:
            return _out_proj(ys, w_out, xs, gate), None
        shift, scale, _ = mod_of(i + 1)
        return _out_proj(ys, w_out, xs, gate, (row(norm_w[i + 1]), shift, scale))

    h = _modulate(xs, row(norm_w[0]), *mod_of(0)[:2])
    for i in range(DEPTH):
        j = i // 2
        if i % 2 == 0:
            w_in = w_in_ab[j].astype(BF16)
            o1, o2, o3, o4 = POOL_WIDTH, POOL_WIDTH + DIFF_WIDTH, POOL_WIDTH + 2 * DIFF_WIDTH, POOL_WIDTH + 3 * DIFF_WIDTH
            lambda_init = 0.8 - 0.6 * math.exp(-0.3 * i)
            u_pool = _matmul(h, w_in, F32, PROJ_TM, PROJ_TN, name="pool_in_proj", cols=(0, o1))
            gte = _matmul(h, w_in, F32, PROJ_TM, PROJ_TN, name="gate_in_proj", cols=(o4, o4 + AB_WIDTH))
            qt, ka, vt = _qkv_proj(h, w_in, o1, o2, o3, row(q_norm_w[j]), row(k_norm_w[j]))
            y_a = _pool_mix(u_pool, w_pool[j].astype(BF16), row(pool_scale[j]), gte)
            y_b = _diff_attn(qt, ka, vt, row(lambda_q1[j]), row(lambda_k1[j]), row(lambda_q2[j]),
                             row(lambda_k2[j]), subln_w[j], gte, lambda_init)
            xs, h = finish([y_a, y_b], w_out_ab[j].astype(BF16), xs, i)
        else:
            w_in = w_in_c[j].astype(BF16)
            ab = _fold_channel_dft(w_fourier[j], s)
            w_p, w_q = _fold_in_proj(w_in, ab)
            gte = _matmul(h, w_in, F32, PROJ_TM, PROJ_TN, name="gate_in_proj", cols=(d, 2 * d))
            h_t = h.reshape(n1, n2, d).transpose(1, 0, 2).reshape(s, d)
            pq_t = _matmul(h_t, jnp.concatenate([w_p, w_q], axis=1), BF16, PROJ_TM, PROJ_TN, name="fourier_in_proj")
            t_nk = _dft_stage1(pq_t, f1, n1, n2)
            y = _dft_stage2(t_nk.reshape(n2, 2, n1, d), l2, gte, n1, n2)
            xs, h = finish([y], w_out_c[j].astype(BF16), xs, i)
    return xs.reshape(batch, s, d)
```

```python
import functools
import math

import jax
import jax.numpy as jnp
from jax import lax
from jax.experimental import pallas as pl
from jax.experimental.pallas import tpu as pltpu

F32 = jnp.float32
BF16 = jnp.bfloat16

D_MODEL = 2048
DEPTH = 4
NORM_EPS = 1e-6

POOL_WINDOWS = (2, 4, 8, 16)
POOL_WIDTH = D_MODEL // 2
POOL_GROUP = POOL_WIDTH // len(POOL_WINDOWS)
POOL_HALO = 8

HEADS = 8
HEAD_DIM = 64
HEAD_V = 2 * HEAD_DIM
DIFF_WIDTH = HEADS * HEAD_V
AB_WIDTH = POOL_WIDTH + DIFF_WIDTH

FOURIER_GROUPS = 4
FOURIER_GROUP = D_MODEL // FOURIER_GROUPS

LANES = 128
SUBLANES = 8
POS_RADIX = 256
COEF_PARTS = 3
LOG2E = math.log2(math.e)

ATT_TQ = 512
ATT_TK = 512
ATT_DIAG = ATT_TQ // ATT_TK
ATT_UNROLL = 8
REF_MARGIN = 60.0
REF_FLOOR = 2.0 ** -60
BOUND_SLACK = 1.001

PROJ_TM, PROJ_TN = 2048, 512
OUT_SUB = 128

VMEM_LIMIT = 56 * 1024 * 1024


def _params(n_axes):
    return pltpu.CompilerParams(dimension_semantics=("arbitrary",) * n_axes,
                                vmem_limit_bytes=VMEM_LIMIT)


def _mod_kernel(c_ref, w_ref, b_ref, o_ref):
    c = c_ref[...]
    o_ref[...] = jnp.sum(jax.nn.silu(c) * w_ref[...], axis=0, keepdims=True) + b_ref[...]


def _ada_mod(c, ada_w, ada_b):
    depth, d, n = ada_w.shape
    tn = 512
    return pl.pallas_call(
        _mod_kernel,
        grid=(depth, n // tn),
        in_specs=[pl.BlockSpec((d, 1), lambda i, j: (0, 0)),
                  pl.BlockSpec((None, d, tn), lambda i, j: (i, 0, j)),
                  pl.BlockSpec((None, 1, tn), lambda i, j: (i, 0, j))],
        out_specs=pl.BlockSpec((None, 1, tn), lambda i, j: (i, 0, j)),
        out_shape=jax.ShapeDtypeStruct((depth, 1, n), F32),
        compiler_params=_params(2),
        name="ada_mod",
    )(c.reshape(d, 1), ada_w, ada_b.reshape(depth, 1, n))


def _modulate_kernel(x_ref, nw_ref, shift_ref, scale_ref, o_ref):
    xf = x_ref[...]
    r = lax.rsqrt(jnp.mean(xf * xf, axis=-1, keepdims=True) + NORM_EPS)
    y = xf * r * nw_ref[...]
    y = y * (1.0 + scale_ref[...]) + shift_ref[...]
    o_ref[...] = y.astype(o_ref.dtype)


def _modulate(x, nw, shift, scale):
    s, d = x.shape
    tm = 512
    vec = pl.BlockSpec((1, d), lambda i: (0, 0))
    return pl.pallas_call(
        _modulate_kernel,
        grid=(s // tm,),
        in_specs=[pl.BlockSpec((tm, d), lambda i: (i, 0)), vec, vec, vec],
        out_specs=pl.BlockSpec((tm, d), lambda i: (i, 0)),
        out_shape=jax.ShapeDtypeStruct((s, d), BF16),
        compiler_params=_params(1),
        name="modulate",
    )(x, nw, shift, scale)


def _mm_kernel(a_ref, w_ref, o_ref, *, precision):
    o_ref[...] = jnp.dot(a_ref[...], w_ref[...], preferred_element_type=F32,
                         precision=precision).astype(o_ref.dtype)


def _matmul(a, w, out_dtype, tm, tn, precision=None, name="matmul", cols=None):
    m, k = a.shape
    col0, col1 = cols if cols is not None else (0, w.shape[1])
    n = col1 - col0
    tm, tn = min(tm, m), min(tn, n)
    assert col0 % tn == 0 and n % tn == 0
    jb = col0 // tn
    return pl.pallas_call(
        functools.partial(_mm_kernel, precision=precision),
        grid=(m // tm, n // tn),
        in_specs=[pl.BlockSpec((tm, k), lambda i, j: (i, 0)),
                  pl.BlockSpec((k, tn), lambda i, j: (0, j + jb))],
        out_specs=pl.BlockSpec((tm, tn), lambda i, j: (i, j)),
        out_shape=jax.ShapeDtypeStruct((m, n), out_dtype),
        compiler_params=_params(2),
        name=name,
    )(a, w)


def _qkv_kernel(h_ref, wq_ref, wk_ref, wv_ref, qnw_ref, knw_ref, qt_ref, ka_ref, vt_ref, *, tm):
    lane = lax.broadcasted_iota(jnp.int32, (ATT_TK, LANES), 1)
    first = lane < HEAD_DIM
    pos = lax.broadcasted_iota(jnp.int32, (ATT_TK, LANES), 0)
    pos_lo = (pos % POS_RADIX).astype(F32)
    pos_hi = (pos // POS_RADIX).astype(F32)
    pos_cols = jnp.where(lane < HEAD_DIM + COEF_PARTS, pos_lo, jnp.where(lane < HEAD_DIM + 2 * COEF_PARTS, pos_hi, 0.0))

    def halves_rms(z, w):
        sq = z * z
        ss1 = jnp.sum(jnp.where(first, sq, 0.0), axis=-1, keepdims=True)
        ss2 = jnp.sum(jnp.where(first, 0.0, sq), axis=-1, keepdims=True)
        r = jnp.where(first, lax.rsqrt(ss1 / HEAD_DIM + NORM_EPS), lax.rsqrt(ss2 / HEAD_DIM + NORM_EPS))
        return z * r * w

    for cc in range(tm // ATT_TK):
        rows = slice(cc * ATT_TK, (cc + 1) * ATT_TK)
        h = h_ref[rows, :]
        zq = jnp.dot(h, wq_ref[...], preferred_element_type=F32)
        zk = jnp.dot(h, wk_ref[...], preferred_element_type=F32)
        zv = jnp.dot(h, wv_ref[...], preferred_element_type=F32)
        for hh in range(2):
            cols = slice(hh * LANES, (hh + 1) * LANES)
            qn = halves_rms(zq[:, cols], qnw_ref[...]) * (HEAD_DIM ** -0.5) * LOG2E
            kn = halves_rms(zk[:, cols], knw_ref[...])
            for m in range(2):
                qm = qn if m == 0 else pltpu.roll(qn, HEAD_DIM, axis=1)
                km = kn if m == 0 else pltpu.roll(kn, HEAD_DIM, axis=1)
                qt_ref[hh, m, :, rows] = jnp.where(first, qm, 0.0).T.astype(BF16)
                ka_ref[hh, m, cc] = jnp.where(first, km, pos_cols).astype(BF16)
            vt_ref[hh, cc] = zv[:, cols].T.astype(BF16)


def _qkv_proj(h, w, col_q, col_k, col_v, qnw, knw):
    s, d = h.shape
    tm = min(2048, s)
    nchunk = s // ATT_TK
    cpt = tm // ATT_TK
    tn = 2 * LANES
    assert col_q % tn == 0 and col_k % tn == 0 and col_v % tn == 0
    wspecs = [pl.BlockSpec((d, tn), functools.partial(lambda i, j, jb: (0, j + jb), jb=c0 // tn))
              for c0 in (col_q, col_k, col_v)]
    vec = pl.BlockSpec((1, LANES), lambda i, j: (0, 0))
    return pl.pallas_call(
        functools.partial(_qkv_kernel, tm=tm),
        grid=(s // tm, HEADS // 2),
        in_specs=[pl.BlockSpec((tm, d), lambda i, j: (i, 0)), *wspecs, vec, vec],
        out_specs=[pl.BlockSpec((2, 2, LANES, tm), lambda i, j: (j, 0, 0, i)),
                   pl.BlockSpec((2, 2, cpt, ATT_TK, LANES), lambda i, j: (j, 0, i, 0, 0)),
                   pl.BlockSpec((2, cpt, HEAD_V, ATT_TK), lambda i, j: (j, i, 0, 0))],
        out_shape=[jax.ShapeDtypeStruct((HEADS, 2, LANES, s), BF16),
                   jax.ShapeDtypeStruct((HEADS, 2, nchunk, ATT_TK, LANES), BF16),
                   jax.ShapeDtypeStruct((HEADS, nchunk, HEAD_V, ATT_TK), BF16)],
        compiler_params=_params(2),
        name="qkv_proj",
    )(h, w, w, w, jnp.tile(qnw, (1, 2)), jnp.tile(knw, (1, 2)))


def _attn_kernel(slopes_ref, qt_ref, ka_ref, vt_ref, lq1_ref, lk1_ref, lq2_ref, lk2_ref, subw_ref, g_ref,
                 o_ref, acc_ref, mu_ref, qv_ref, kmax_ref, p_a, p_b, den_ref, *, lambda_init, nchunk, unroll):
    hd = pl.program_id(0)
    i = pl.program_id(1)
    base = hd * (1 + COEF_PARTS)
    slope = slopes_ref[base]
    rowq = lax.broadcasted_iota(jnp.int32, (LANES, ATT_TQ), 0)
    q_off = lax.broadcasted_iota(jnp.int32, (1, ATT_TQ), 1).astype(F32)

    row1 = lax.broadcasted_iota(jnp.int32, (LANES, 1), 0)
    coef_col = jnp.zeros((LANES, 1), F32)
    for t in range(COEF_PARTS):
        piece = slopes_ref[base + 1 + t]
        coef_col = jnp.where(row1 == HEAD_DIM + t, piece, coef_col)
        coef_col = jnp.where(row1 == HEAD_DIM + COEF_PARTS + t, piece * POS_RADIX, coef_col)
    for m in range(2):
        qf = qt_ref[0, m].astype(F32)
        for kind, sign in ((0, -1.0), (1, 0.0), (2, 1.0)):
            qv_ref[kind, m] = jnp.where(rowq >= HEAD_DIM, sign * coef_col, qf).astype(BF16)

    @pl.when(i == 0)
    def _():
        lane = lax.broadcasted_iota(jnp.int32, (ATT_TK, LANES), 1)
        for m in range(2):
            def widest(c, best):
                kc = ka_ref[0, m, c].astype(F32)
                return jnp.maximum(best, jnp.sum(jnp.where(lane < HEAD_DIM, kc * kc, 0.0), axis=1, keepdims=True))
            kmax_ref[m] = jnp.max(lax.fori_loop(0, nchunk, widest, jnp.zeros((ATT_TK, 1), F32)))

    for m in range(2):
        qf = qt_ref[0, m].astype(F32)
        bound = jnp.sqrt(jnp.sum(qf * qf, axis=0, keepdims=True) * kmax_ref[m]) * BOUND_SLACK
        mu_ref[m] = bound - REF_MARGIN

    c_diag = i * ATT_DIAG

    def chunk_of(e):
        if isinstance(e, int) and e < ATT_DIAG:
            return c_diag + e
        rest = e - ATT_DIAG
        return jnp.where(e < ATT_DIAG, c_diag + e, rest + ATT_DIAG * (rest >= c_diag).astype(jnp.int32))

    def side_of(c):
        return (c < c_diag).astype(jnp.int32) - (c >= c_diag + ATT_DIAG).astype(jnp.int32)

    p_bufs = (p_a, p_b)

    def probs(e, p_ref):
        c = chunk_of(e)
        side = side_of(c)
        sidef = side.astype(F32)
        gap = jnp.abs(c * ATT_TK - i * ATT_TQ).astype(F32)
        cvec = -slope * (sidef * sidef * gap + sidef * q_off)
        for m in range(2):
            st = jnp.dot(ka_ref[0, m, c], qv_ref[1 + side, m], preferred_element_type=F32)
            if isinstance(e, int) and e < ATT_DIAG:
                key_off = lax.broadcasted_iota(jnp.int32, (ATT_TK, ATT_TQ), 0) + e * ATT_TK
                qry_off = lax.broadcasted_iota(jnp.int32, (ATT_TK, ATT_TQ), 1)
                st = st - slope * jnp.abs(key_off - qry_off).astype(F32)
            p = jnp.exp2(st - (mu_ref[m] - cvec))
            p_ref[m] = p.astype(BF16)
            den_ref[m] += jnp.sum(p.reshape(ATT_TK // SUBLANES, SUBLANES, ATT_TQ), axis=0)

    def group(e0, count, probs_last):
        sums = [None, None]
        for u in range(count):
            if u < count - 1 or probs_last:
                probs(e0 + u + 1, p_bufs[(u + 1) % 2])
            c = chunk_of(e0 + u)
            for m in range(2):
                pv = jnp.dot(vt_ref[0, c], p_bufs[u % 2][m], preferred_element_type=F32)
                sums[m] = pv if sums[m] is None else sums[m] + pv
        for m in range(2):
            acc_ref[m] += sums[m]

    def loop_body(it, carry):
        group(it * unroll, unroll, True)
        return carry

    def one_pass(_):
        acc_ref[...] = jnp.zeros(acc_ref.shape, F32)
        den_ref[...] = jnp.zeros(den_ref.shape, F32)
        n_groups = nchunk // unroll
        probs(0, p_a)
        group(0, unroll, True)
        lax.fori_loop(1, n_groups - 1, loop_body, 0)
        group((n_groups - 1) * unroll, unroll, False)
        lowest = None
        for m in range(2):
            den = jnp.sum(den_ref[m], axis=0, keepdims=True)
            mu_ref[m] = jnp.where(den < REF_FLOOR, mu_ref[m] - 2.0 * REF_MARGIN, mu_ref[m])
            lowest = jnp.min(den) if lowest is None else jnp.minimum(lowest, jnp.min(den))
        return (lowest < REF_FLOOR).astype(jnp.int32)

    lax.while_loop(lambda retry: retry > 0, one_pass, jnp.int32(1))

    lam = (jnp.exp(jnp.sum(lq1_ref[...] * lk1_ref[...], axis=-1, keepdims=True))
           - jnp.exp(jnp.sum(lq2_ref[...] * lk2_ref[...], axis=-1, keepdims=True)) + lambda_init)
    den1 = jnp.sum(den_ref[0], axis=0, keepdims=True)
    den2 = jnp.sum(den_ref[1], axis=0, keepdims=True)
    o = acc_ref[0] / den1 - lam * (acc_ref[1] / den2)
    r = lax.rsqrt(jnp.mean(o * o, axis=0, keepdims=True) + NORM_EPS)
    y = (o * r * subw_ref[...]) * (1.0 - lambda_init)
    o_ref[...] = (y.T * jax.nn.silu(g_ref[...])).astype(o_ref.dtype)


def _diff_attn(qt, ka, vt, lq1, lk1, lq2, lk2, subw, g, lambda_init):
    s = qt.shape[-1]
    g_col0 = (g.shape[1] - DIFF_WIDTH) // HEAD_V
    nchunk = s // ATT_TK
    unroll = min(ATT_UNROLL, nchunk // 2)
    assert unroll % 2 == 0 and nchunk % unroll == 0 and unroll >= ATT_DIAG and s % ATT_TQ == 0
    whole = jnp.asarray([LOG2E * 2.0 ** (-8.0 * (h + 1) / HEADS) for h in range(HEADS)], F32)
    pieces, rest = [], whole
    for _ in range(COEF_PARTS):
        pieces.append(rest.astype(BF16).astype(F32))
        rest = rest - pieces[-1]
    slopes = jnp.stack([whole] + pieces, axis=1).reshape(-1)
    vec = pl.BlockSpec((1, HEAD_DIM), lambda h, i: (0, 0))
    return pl.pallas_call(
        functools.partial(_attn_kernel, lambda_init=lambda_init, nchunk=nchunk, unroll=unroll),
        grid=(HEADS, s // ATT_TQ),
        in_specs=[pl.BlockSpec(memory_space=pltpu.SMEM),
                  pl.BlockSpec((1, 2, LANES, ATT_TQ), lambda h, i: (h, 0, 0, i)),
                  pl.BlockSpec((1, 2, nchunk, ATT_TK, LANES), lambda h, i: (h, 0, 0, 0, 0)),
                  pl.BlockSpec((1, nchunk, HEAD_V, ATT_TK), lambda h, i: (h, 0, 0, 0)),
                  vec, vec, vec, vec,
                  pl.BlockSpec((HEAD_V, 1), lambda h, i: (0, 0)),
                  pl.BlockSpec((ATT_TQ, HEAD_V), lambda h, i: (i, g_col0 + h))],
        out_specs=pl.BlockSpec((ATT_TQ, HEAD_V), lambda h, i: (i, h)),
        out_shape=jax.ShapeDtypeStruct((s, DIFF_WIDTH), BF16),
        scratch_shapes=[pltpu.VMEM((2, HEAD_V, ATT_TQ), F32), pltpu.VMEM((2, 1, ATT_TQ), F32),
                        pltpu.VMEM((3, 2, LANES, ATT_TQ), BF16), pltpu.SMEM((2,), F32),
                        pltpu.VMEM((2, ATT_TK, ATT_TQ), BF16), pltpu.VMEM((2, ATT_TK, ATT_TQ), BF16),
                        pltpu.VMEM((2, SUBLANES, ATT_TQ), F32)],
        compiler_params=_params(2),
        name="diff_attn",
    )(slopes, qt, ka, vt, lq1, lk1, lq2, lk2, subw.reshape(HEAD_V, 1), g)


def _pool_kernel(prev_ref, cur_ref, next_ref, w_ref, scale_ref, g_ref, o_ref, ext_ref, *, tm, seq):
    i = pl.program_id(0)
    last = pl.num_programs(0) - 1
    zero_halo = jnp.zeros((POOL_HALO, POOL_WIDTH), F32)
    ext_ref[0:POOL_HALO] = jnp.where(i > 0, prev_ref[...], zero_halo)
    ext_ref[POOL_HALO:POOL_HALO + tm] = cur_ref[...]
    ext_ref[POOL_HALO + tm:POOL_HALO + tm + POOL_HALO] = jnp.where(i < last, next_ref[...], zero_halo)
    t = lax.broadcasted_iota(jnp.int32, (tm, 1), 0) + i * tm
    for g, w in enumerate(POOL_WINDOWS):
        cols = pl.ds(g * POOL_GROUP, POOL_GROUP)
        win = ext_ref[pl.ds(POOL_HALO - w // 2, tm), cols]
        for d in range(-(w // 2) + 1, w - w // 2):
            win = win + ext_ref[pl.ds(POOL_HALO + d, tm), cols]
        lo = jnp.maximum(t - w // 2, 0)
        hi = jnp.minimum(t + (w - w // 2) - 1, seq - 1)
        cnt = (hi - lo + 1).astype(F32)
        pooled = win / cnt - cur_ref[:, cols]
        y = jnp.dot(pooled.astype(BF16), w_ref[g], preferred_element_type=F32)
        o_ref[:, cols] = (y * scale_ref[:, cols] * jax.nn.silu(g_ref[:, cols])).astype(o_ref.dtype)


def _pool_mix(u, w_pool, pool_scale, g):
    s, width = u.shape
    tm = 512
    hb = tm // POOL_HALO
    nhalo = s // POOL_HALO
    return pl.pallas_call(
        functools.partial(_pool_kernel, tm=tm, seq=s),
        grid=(s // tm,),
        in_specs=[pl.BlockSpec((POOL_HALO, width), lambda i: (jnp.maximum(i * hb - 1, 0), 0)),
                  pl.BlockSpec((tm, width), lambda i: (i, 0)),
                  pl.BlockSpec((POOL_HALO, width), lambda i: (jnp.minimum((i + 1) * hb, nhalo - 1), 0)),
                  pl.BlockSpec(w_pool.shape, lambda i: (0, 0, 0)),
                  pl.BlockSpec((1, width), lambda i: (0, 0)),
                  pl.BlockSpec((tm, width), lambda i: (i, 0))],
        out_specs=pl.BlockSpec((tm, width), lambda i: (i, 0)),
        out_shape=jax.ShapeDtypeStruct((s, width), BF16),
        scratch_shapes=[pltpu.VMEM((tm + 2 * POOL_HALO, width), F32)],
        compiler_params=_params(1),
        name="pool_mix",
    )(u, u, u, w_pool, pool_scale, g)


def _out_kernel(*refs, widths, tm, modulate_next):
    n = len(widths)
    y_refs = refs[:n]
    w_ref, x_ref, gate_ref = refs[n:n + 3]
    if modulate_next:
        nw_ref, shift_ref, scale_ref, o_ref, h_ref = refs[n + 3:]
        mul = nw_ref[...]
        one_plus_scale = 1.0 + scale_ref[...]
    else:
        (o_ref,) = refs[n + 3:]
    for sb in range(tm // OUT_SUB):
        rows = slice(sb * OUT_SUB, (sb + 1) * OUT_SUB)
        acc = None
        off = 0
        for y_ref, wd in zip(y_refs, widths):
            part = jnp.dot(y_ref[rows, :].astype(BF16), w_ref[off:off + wd, :], preferred_element_type=F32)
            acc = part if acc is None else acc + part
            off += wd
        xn = x_ref[rows, :] + gate_ref[...] * acc
        o_ref[rows, :] = xn
        if modulate_next:
            r = lax.rsqrt(jnp.mean(xn * xn, axis=-1, keepdims=True) + NORM_EPS)
            h_ref[rows, :] = ((xn * r * mul) * one_plus_scale + shift_ref[...]).astype(BF16)


def _out_proj(ys, w, x, gate, next_mod=None):
    s, d = x.shape
    k = w.shape[0]
    tm = min(512, s)
    widths = tuple(y.shape[1] for y in ys)
    vec = pl.BlockSpec((1, d), lambda i: (0, 0))
    rows = pl.BlockSpec((tm, d), lambda i: (i, 0))
    modulate_next = next_mod is not None
    return pl.pallas_call(
        functools.partial(_out_kernel, widths=widths, tm=tm, modulate_next=modulate_next),
        grid=(s // tm,),
        in_specs=[pl.BlockSpec((tm, wd), lambda i: (i, 0)) for wd in widths]
        + [pl.BlockSpec((k, d), lambda i: (0, 0)), rows, vec] + ([vec, vec, vec] if modulate_next else []),
        out_specs=[rows, rows] if modulate_next else rows,
        out_shape=([jax.ShapeDtypeStruct((s, d), F32), jax.ShapeDtypeStruct((s, d), BF16)] if modulate_next
                   else jax.ShapeDtypeStruct((s, d), F32)),
        compiler_params=_params(1),
        name="out_proj",
    )(*ys, w, x, gate, *(next_mod if modulate_next else ()))


def _dft_split(s):
    n1 = 1 << (int(math.log2(s)) // 2)
    n2 = s // n1
    assert n1 * n2 == s and n1 % SUBLANES == 0 and n2 % SUBLANES == 0
    return n1, n2


def _angles(num, den):
    ang = (2.0 * math.pi / den) * (num % den).astype(F32)
    return jnp.cos(ang), jnp.sin(ang)


def _fold_cs_kernel(cs_ref, wf_ref, o_ref):
    wf = wf_ref[...]
    o_ref[:, 0:FOURIER_GROUP] = jnp.dot(cs_ref[0], wf, preferred_element_type=F32,
                                        precision=lax.Precision.HIGHEST)
    o_ref[:, FOURIER_GROUP:] = jnp.dot(cs_ref[1], wf, preferred_element_type=F32,
                                       precision=lax.Precision.HIGHEST)


def _fold_channel_dft(w_fourier_j, seq):
    fg = FOURIER_GROUP
    idx = jnp.arange(fg, dtype=jnp.int32)
    cc, sc = _angles(idx[:, None] * idx[None, :], fg)
    cs = jnp.stack([cc, sc]) * (1.0 / math.sqrt(seq * fg))
    return pl.pallas_call(
        _fold_cs_kernel,
        grid=(FOURIER_GROUPS,),
        in_specs=[pl.BlockSpec((2, fg, fg), lambda g: (0, 0, 0)),
                  pl.BlockSpec((None, fg, fg), lambda g: (g, 0, 0))],
        out_specs=pl.BlockSpec((None, fg, 2 * fg), lambda g: (g, 0, 0)),
        out_shape=jax.ShapeDtypeStruct((FOURIER_GROUPS, fg, 2 * fg), F32),
        compiler_params=_params(1),
        name="fold_channel_dft",
    )(cs, w_fourier_j)


def _fold_win_kernel(w_ref, ab_ref, p_ref, q_ref):
    r = jnp.dot(w_ref[...], ab_ref[...].astype(BF16), preferred_element_type=F32)
    p_ref[...] = r[:, 0:FOURIER_GROUP].astype(BF16)
    q_ref[...] = r[:, FOURIER_GROUP:].astype(BF16)


def _fold_in_proj(w_in_u, ab):
    d = w_in_u.shape[0]
    fg = FOURIER_GROUP
    ospec = pl.BlockSpec((d, fg), lambda g: (0, g))
    return pl.pallas_call(
        _fold_win_kernel,
        grid=(FOURIER_GROUPS,),
        in_specs=[pl.BlockSpec((d, fg), lambda g: (0, g)),
                  pl.BlockSpec((None, fg, 2 * fg), lambda g: (g, 0, 0))],
        out_specs=[ospec, ospec],
        out_shape=[jax.ShapeDtypeStruct((d, D_MODEL), BF16)] * 2,
        compiler_params=_params(1),
        name="fold_in_proj",
    )(w_in_u, ab)


def _dft1_kernel(f_ref, p_ref, q_ref, t_ref, *, nb, n1):
    f = f_ref[...]
    for jj in range(nb):
        rows = slice(jj * n1, (jj + 1) * n1)
        rhs = jnp.concatenate([p_ref[rows, :], q_ref[rows, :]], axis=0)
        t_ref[jj] = jnp.dot(f, rhs, preferred_element_type=F32)


def _dft_stage1(pq_t, f1, n1, n2):
    s, c2 = pq_t.shape
    c = c2 // 2
    nb = SUBLANES
    cb = min(1024, c)
    ncb = c // cb
    return pl.pallas_call(
        functools.partial(_dft1_kernel, nb=nb, n1=n1),
        grid=(n2 // nb, ncb),
        in_specs=[pl.BlockSpec((2 * n1, 2 * n1), lambda b, j: (0, 0)),
                  pl.BlockSpec((nb * n1, cb), lambda b, j: (b, j)),
                  pl.BlockSpec((nb * n1, cb), lambda b, j: (b, j + ncb))],
        out_specs=pl.BlockSpec((nb, 2 * n1, cb), lambda b, j: (b, 0, j)),
        out_shape=jax.ShapeDtypeStruct((n2, 2 * n1, c), F32),
        compiler_params=_params(2),
        name="dft_stage1",
    )(f1, pq_t, pq_t)


def _dft2_kernel(l_ref, t_ref, g_ref, o_ref, *, n2, cb):
    rhs = t_ref[...].reshape(n2 * 2 * SUBLANES, cb).astype(BF16)
    y = jnp.dot(l_ref[0], rhs, preferred_element_type=F32)
    o_ref[...] = y.reshape(n2, SUBLANES, cb) * jax.nn.silu(g_ref[...])


def _dft_stage2(t_nk, l2, g, n1, n2):
    c = t_nk.shape[-1]
    cb = min(1024, c)
    row_blocks = pl.BlockSpec((n2, SUBLANES, cb), lambda g, j: (0, g, j))
    out = pl.pallas_call(
        functools.partial(_dft2_kernel, n2=n2, cb=cb),
        grid=(n1 // SUBLANES, c // cb),
        in_specs=[pl.BlockSpec((1, SUBLANES * n2, SUBLANES * 2 * n2), lambda g, j: (g, 0, 0)),
                  pl.BlockSpec((n2, 2, SUBLANES, cb), lambda g, j: (0, 0, g, j)),
                  row_blocks],
        out_specs=row_blocks,
        out_shape=jax.ShapeDtypeStruct((n2, n1, c), F32),
        compiler_params=_params(2),
        name="dft_stage2",
    )(l2, t_nk, g.reshape(n2, n1, c))
    return out.reshape(n1 * n2, c)


def _dft_matrices(s):
    n1, n2 = _dft_split(s)
    a = jnp.arange(n1, dtype=jnp.int32)
    c1, s1 = _angles(a[:, None] * a[None, :], n1)
    f1 = jnp.concatenate([jnp.concatenate([c1, -s1], axis=1),
                          jnp.concatenate([-s1, -c1], axis=1)], axis=0).astype(BF16)
    groups = n1 // SUBLANES
    g = jnp.arange(groups, dtype=jnp.int32)[:, None, None, None]
    k2 = jnp.arange(n2, dtype=jnp.int32)[None, :, None, None]
    j = jnp.arange(SUBLANES, dtype=jnp.int32)[None, None, :, None]
    nn = jnp.arange(n2, dtype=jnp.int32)[None, None, None, :]
    c2, s2 = _angles((SUBLANES * g + j + n1 * k2) * nn, s)
    cs = jnp.stack([c2, s2], axis=4).astype(BF16)
    return n1, n2, f1, _expand_twiddles(cs.reshape(groups, n2 * SUBLANES, n2 * 2))


def _expand_kernel(cs_ref, e_ref, o_ref):
    spread = jnp.dot(cs_ref[0], e_ref[...], preferred_element_type=F32)
    row = lax.broadcasted_iota(jnp.int32, spread.shape, 0)
    col = lax.broadcasted_iota(jnp.int32, spread.shape, 1)
    o_ref[0] = jnp.where(row % SUBLANES == col % SUBLANES, spread, 0.0).astype(BF16)


def _expand_twiddles(cs):
    groups, rows, cols = cs.shape
    wide = cols * SUBLANES
    e = (jnp.arange(wide, dtype=jnp.int32)[None, :] // SUBLANES == jnp.arange(cols, dtype=jnp.int32)[:, None])
    return pl.pallas_call(
        _expand_kernel,
        grid=(groups,),
        in_specs=[pl.BlockSpec((1, rows, cols), lambda g: (g, 0, 0)),
                  pl.BlockSpec((cols, wide), lambda g: (0, 0))],
        out_specs=pl.BlockSpec((1, rows, wide), lambda g: (g, 0, 0)),
        out_shape=jax.ShapeDtypeStruct((groups, rows, wide), BF16),
        compiler_params=_params(1),
        name="expand_twiddles",
    )(cs, e.astype(BF16))


def kernel(x, c, norm_w, ada_w, ada_b, w_in_ab, w_pool, pool_scale, q_norm_w, k_norm_w, lambda_q1, lambda_k1,
           lambda_q2, lambda_k2, subln_w, w_out_ab, w_in_c, w_fourier, w_out_c):
    batch, s, d = x.shape
    assert batch == 1 and d == D_MODEL and s % ATT_TK == 0
    xs = x.reshape(s, d)
    mod = _ada_mod(c, ada_w, ada_b)
    n1, n2, f1, l2 = _dft_matrices(s)
    row = lambda v: v.reshape(1, -1)

    def mod_of(i):
        return mod[i, :, 0:d], mod[i, :, d:2 * d], mod[i, :, 2 * d:3 * d]

    def finish(ys, w_out, xs, i):
        gate = mod_of(i)[2]
        if i + 1 == DEPTH:
            return _out_proj(ys, w_out, xs, gate), None
        shift, scale, _ = mod_of(i + 1)
        return _out_proj(ys, w_out, xs, gate, (row(norm_w[i + 1]), shift, scale))

    h = _modulate(xs, row(norm_w[0]), *mod_of(0)[:2])
    for i in range(DEPTH):
        j = i // 2
        if i % 2 == 0:
            w_in = w_in_ab[j].astype(BF16)
            o1, o2, o3, o4 = POOL_WIDTH, POOL_WIDTH + DIFF_WIDTH, POOL_WIDTH + 2 * DIFF_WIDTH, POOL_WIDTH + 3 * DIFF_WIDTH
            lambda_init = 0.8 - 0.6 * math.exp(-0.3 * i)
            u_pool = _matmul(h, w_in, F32, PROJ_TM, PROJ_TN, name="pool_in_proj", cols=(0, o1))
            gte = _matmul(h, w_in, F32, PROJ_TM, PROJ_TN, name="gate_in_proj", cols=(o4, o4 + AB_WIDTH))
            qt, ka, vt = _qkv_proj(h, w_in, o1, o2, o3, row(q_norm_w[j]), row(k_norm_w[j]))
            y_a = _pool_mix(u_pool, w_pool[j].astype(BF16), row(pool_scale[j]), gte)
            y_b = _diff_attn(qt, ka, vt, row(lambda_q1[j]), row(lambda_k1[j]), row(lambda_q2[j]),
                             row(lambda_k2[j]), subln_w[j], gte, lambda_init)
            xs, h = finish([y_a, y_b], w_out_ab[j].astype(BF16), xs, i)
        else:
            w_in = w_in_c[j].astype(BF16)
            ab = _fold_channel_dft(w_fourier[j], s)
            w_p, w_q = _fold_in_proj(w_in, ab)
            gte = _matmul(h, w_in, F32, PROJ_TM, PROJ_TN, name="gate_in_proj", cols=(d, 2 * d))
            h_t = h.reshape(n1, n2, d).transpose(1, 0, 2).reshape(s, d)
            pq_t = _matmul(h_t, jnp.concatenate([w_p, w_q], axis=1), BF16, PROJ_TM, PROJ_TN, name="fourier_in_proj")
            t_nk = _dft_stage1(pq_t, f1, n1, n2)
            y = _dft_stage2(t_nk.reshape(n2, 2, n1, d), l2, gte, n1, n2)
            xs, h = finish([y], w_out_c[j].astype(BF16), xs, i)
    return xs.reshape(batch, s, d)
```

```python
import functools
import math

import jax
import jax.numpy as jnp
from jax import lax
from jax.experimental import pallas as pl
from jax.experimental.pallas import tpu as pltpu

F32 = jnp.float32
BF16 = jnp.bfloat16

D_MODEL = 2048
DEPTH = 4
NORM_EPS = 1e-6

POOL_WINDOWS = (2, 4, 8, 16)
POOL_WIDTH = D_MODEL // 2
POOL_GROUP = POOL_WIDTH // len(POOL_WINDOWS)
POOL_HALO = 8

HEADS = 8
HEAD_DIM = 64
HEAD_V = 2 * HEAD_DIM
DIFF_WIDTH = HEADS * HEAD_V
AB_WIDTH = POOL_WIDTH + DIFF_WIDTH

FOURIER_GROUPS = 4
FOURIER_GROUP = D_MODEL // FOURIER_GROUPS

LANES = 128
SUBLANES = 8
POS_RADIX = 256
COEF_PARTS = 3
LOG2E = math.log2(math.e)

ATT_TQ = 512
ATT_TK = 512
ATT_DIAG = ATT_TQ // ATT_TK
ATT_UNROLL = 8
REF_MARGIN = 60.0
REF_FLOOR = 2.0 ** -60
BOUND_SLACK = 1.001

PROJ_TM, PROJ_TN = 2048, 512
OUT_SUB = 128

VMEM_LIMIT = 56 * 1024 * 1024


def _params(n_axes):
    return pltpu.CompilerParams(dimension_semantics=("arbitrary",) * n_axes,
                                vmem_limit_bytes=VMEM_LIMIT)


def _mod_kernel(c_ref, w_ref, b_ref, o_ref):
    c = c_ref[...]
    o_ref[...] = jnp.sum(jax.nn.silu(c) * w_ref[...], axis=0, keepdims=True) + b_ref[...]


def _ada_mod(c, ada_w, ada_b):
    depth, d, n = ada_w.shape
    tn = 512
    return pl.pallas_call(
        _mod_kernel,
        grid=(depth, n // tn),
        in_specs=[pl.BlockSpec((d, 1), lambda i, j: (0, 0)),
                  pl.BlockSpec((None, d, tn), lambda i, j: (i, 0, j)),
                  pl.BlockSpec((None, 1, tn), lambda i, j: (i, 0, j))],
        out_specs=pl.BlockSpec((None, 1, tn), lambda i, j: (i, 0, j)),
        out_shape=jax.ShapeDtypeStruct((depth, 1, n), F32),
        compiler_params=_params(2),
        name="ada_mod",
    )(c.reshape(d, 1), ada_w, ada_b.reshape(depth, 1, n))


def _modulate_kernel(x_ref, nw_ref, shift_ref, scale_ref, o_ref):
    xf = x_ref[...]
    r = lax.rsqrt(jnp.mean(xf * xf, axis=-1, keepdims=True) + NORM_EPS)
    y = xf * r * nw_ref[...]
    y = y * (1.0 + scale_ref[...]) + shift_ref[...]
    o_ref[...] = y.astype(o_ref.dtype)


def _modulate(x, nw, shift, scale):
    s, d = x.shape
    tm = 512
    vec = pl.BlockSpec((1, d), lambda i: (0, 0))
    return pl.pallas_call(
        _modulate_kernel,
        grid=(s // tm,),
        in_specs=[pl.BlockSpec((tm, d), lambda i: (i, 0)), vec, vec, vec],
        out_specs=pl.BlockSpec((tm, d), lambda i: (i, 0)),
        out_shape=jax.ShapeDtypeStruct((s, d), BF16),
        compiler_params=_params(1),
        name="modulate",
    )(x, nw, shift, scale)


def _mm_kernel(a_ref, w_ref, o_ref, *, precision):
    o_ref[...] = jnp.dot(a_ref[...], w_ref[...], preferred_element_type=F32,
                         precision=precision).astype(o_ref.dtype)


def _matmul(a, w, out_dtype, tm, tn, precision=None, name="matmul", cols=None, layer=None):
    m, k = a.shape
    col0, col1 = cols if cols is not None else (0, w.shape[-1])
    n = col1 - col0
    tm, tn = min(tm, m), min(tn, n)
    assert col0 % tn == 0 and n % tn == 0
    jb = col0 // tn
    if layer is None:
        wspec = pl.BlockSpec((k, tn), lambda i, j: (0, j + jb))
    else:
        wspec = pl.BlockSpec((None, k, tn), lambda i, j: (layer, 0, j + jb))
    return pl.pallas_call(
        functools.partial(_mm_kernel, precision=precision),
        grid=(m // tm, n // tn),
        in_specs=[pl.BlockSpec((tm, k), lambda i, j: (i, 0)), wspec],
        out_specs=pl.BlockSpec((tm, tn), lambda i, j: (i, j)),
        out_shape=jax.ShapeDtypeStruct((m, n), out_dtype),
        compiler_params=_params(2),
        name=name,
    )(a, w)


def _qkv_kernel(h_ref, wq_ref, wk_ref, wv_ref, qnw_ref, knw_ref, qt_ref, ka_ref, vt_ref, *, tm):
    lane = lax.broadcasted_iota(jnp.int32, (ATT_TK, LANES), 1)
    first = lane < HEAD_DIM
    pos = lax.broadcasted_iota(jnp.int32, (ATT_TK, LANES), 0)
    pos_lo = (pos % POS_RADIX).astype(F32)
    pos_hi = (pos // POS_RADIX).astype(F32)
    pos_cols = jnp.where(lane < HEAD_DIM + COEF_PARTS, pos_lo, jnp.where(lane < HEAD_DIM + 2 * COEF_PARTS, pos_hi, 0.0))

    def halves_rms(z, w):
        sq = z * z
        ss1 = jnp.sum(jnp.where(first, sq, 0.0), axis=-1, keepdims=True)
        ss2 = jnp.sum(jnp.where(first, 0.0, sq), axis=-1, keepdims=True)
        r = jnp.where(first, lax.rsqrt(ss1 / HEAD_DIM + NORM_EPS), lax.rsqrt(ss2 / HEAD_DIM + NORM_EPS))
        return z * r * w

    for cc in range(tm // ATT_TK):
        rows = slice(cc * ATT_TK, (cc + 1) * ATT_TK)
        h = h_ref[rows, :]
        zq = jnp.dot(h, wq_ref[...], preferred_element_type=F32)
        zk = jnp.dot(h, wk_ref[...], preferred_element_type=F32)
        zv = jnp.dot(h, wv_ref[...], preferred_element_type=F32)
        for hh in range(2):
            cols = slice(hh * LANES, (hh + 1) * LANES)
            qn = halves_rms(zq[:, cols], qnw_ref[...]) * (HEAD_DIM ** -0.5) * LOG2E
            kn = halves_rms(zk[:, cols], knw_ref[...])
            for m in range(2):
                qm = qn if m == 0 else pltpu.roll(qn, HEAD_DIM, axis=1)
                km = kn if m == 0 else pltpu.roll(kn, HEAD_DIM, axis=1)
                qt_ref[hh, m, :, rows] = jnp.where(first, qm, 0.0).T.astype(BF16)
                ka_ref[hh, m, cc] = jnp.where(first, km, pos_cols).astype(BF16)
            vt_ref[hh, cc] = zv[:, cols].T.astype(BF16)


def _qkv_proj(h, w, layer, col_q, col_k, col_v, qnw, knw):
    s, d = h.shape
    tm = min(2048, s)
    nchunk = s // ATT_TK
    cpt = tm // ATT_TK
    tn = 2 * LANES
    assert col_q % tn == 0 and col_k % tn == 0 and col_v % tn == 0
    wspecs = [pl.BlockSpec((None, d, tn), functools.partial(lambda i, j, jb: (layer, 0, j + jb), jb=c0 // tn))
              for c0 in (col_q, col_k, col_v)]
    vec = pl.BlockSpec((1, LANES), lambda i, j: (0, 0))
    return pl.pallas_call(
        functools.partial(_qkv_kernel, tm=tm),
        grid=(s // tm, HEADS // 2),
        in_specs=[pl.BlockSpec((tm, d), lambda i, j: (i, 0)), *wspecs, vec, vec],
        out_specs=[pl.BlockSpec((2, 2, LANES, tm), lambda i, j: (j, 0, 0, i)),
                   pl.BlockSpec((2, 2, cpt, ATT_TK, LANES), lambda i, j: (j, 0, i, 0, 0)),
                   pl.BlockSpec((2, cpt, HEAD_V, ATT_TK), lambda i, j: (j, i, 0, 0))],
        out_shape=[jax.ShapeDtypeStruct((HEADS, 2, LANES, s), BF16),
                   jax.ShapeDtypeStruct((HEADS, 2, nchunk, ATT_TK, LANES), BF16),
                   jax.ShapeDtypeStruct((HEADS, nchunk, HEAD_V, ATT_TK), BF16)],
        compiler_params=_params(2),
        name="qkv_proj",
    )(h, w, w, w, jnp.tile(qnw, (1, 2)), jnp.tile(knw, (1, 2)))


def _attn_kernel(slopes_ref, qt_ref, ka_ref, vt_ref, lq1_ref, lk1_ref, lq2_ref, lk2_ref, subw_ref, g_ref,
                 o_ref, acc_ref, mu_ref, qv_ref, kmax_ref, p_a, p_b, den_ref, *, lambda_init, nchunk, unroll):
    hd = pl.program_id(0)
    i = pl.program_id(1)
    base = hd * (1 + COEF_PARTS)
    slope = slopes_ref[base]
    rowq = lax.broadcasted_iota(jnp.int32, (LANES, ATT_TQ), 0)
    q_off = lax.broadcasted_iota(jnp.int32, (1, ATT_TQ), 1).astype(F32)

    row1 = lax.broadcasted_iota(jnp.int32, (LANES, 1), 0)
    coef_col = jnp.zeros((LANES, 1), F32)
    for t in range(COEF_PARTS):
        piece = slopes_ref[base + 1 + t]
        coef_col = jnp.where(row1 == HEAD_DIM + t, piece, coef_col)
        coef_col = jnp.where(row1 == HEAD_DIM + COEF_PARTS + t, piece * POS_RADIX, coef_col)
    for m in range(2):
        qf = qt_ref[0, m].astype(F32)
        for kind, sign in ((0, -1.0), (1, 0.0), (2, 1.0)):
            qv_ref[kind, m] = jnp.where(rowq >= HEAD_DIM, sign * coef_col, qf).astype(BF16)

    @pl.when(i == 0)
    def _():
        lane = lax.broadcasted_iota(jnp.int32, (ATT_TK, LANES), 1)
        for m in range(2):
            def widest(c, best):
                kc = ka_ref[0, m, c].astype(F32)
                return jnp.maximum(best, jnp.sum(jnp.where(lane < HEAD_DIM, kc * kc, 0.0), axis=1, keepdims=True))
            kmax_ref[m] = jnp.max(lax.fori_loop(0, nchunk, widest, jnp.zeros((ATT_TK, 1), F32)))

    for m in range(2):
        qf = qt_ref[0, m].astype(F32)
        bound = jnp.sqrt(jnp.sum(qf * qf, axis=0, keepdims=True) * kmax_ref[m]) * BOUND_SLACK
        mu_ref[m] = bound - REF_MARGIN

    c_diag = i * ATT_DIAG

    def chunk_of(e):
        if isinstance(e, int) and e < ATT_DIAG:
            return c_diag + e
        rest = e - ATT_DIAG
        return jnp.where(e < ATT_DIAG, c_diag + e, rest + ATT_DIAG * (rest >= c_diag).astype(jnp.int32))

    def side_of(c):
        return (c < c_diag).astype(jnp.int32) - (c >= c_diag + ATT_DIAG).astype(jnp.int32)

    p_bufs = (p_a, p_b)

    def probs(e, p_ref):
        c = chunk_of(e)
        side = side_of(c)
        sidef = side.astype(F32)
        gap = jnp.abs(c * ATT_TK - i * ATT_TQ).astype(F32)
        cvec = -slope * (sidef * sidef * gap + sidef * q_off)
        for m in range(2):
            st = jnp.dot(ka_ref[0, m, c], qv_ref[1 + side, m], preferred_element_type=F32)
            if isinstance(e, int) and e < ATT_DIAG:
                key_off = lax.broadcasted_iota(jnp.int32, (ATT_TK, ATT_TQ), 0) + e * ATT_TK
                qry_off = lax.broadcasted_iota(jnp.int32, (ATT_TK, ATT_TQ), 1)
                st = st - slope * jnp.abs(key_off - qry_off).astype(F32)
            p = jnp.exp2(st - (mu_ref[m] - cvec))
            p_ref[m] = p.astype(BF16)
            den_ref[m] += jnp.sum(p.reshape(ATT_TK // SUBLANES, SUBLANES, ATT_TQ), axis=0)

    def group(e0, count, probs_last):
        sums = [None, None]
        for u in range(count):
            if u < count - 1 or probs_last:
                probs(e0 + u + 1, p_bufs[(u + 1) % 2])
            c = chunk_of(e0 + u)
            for m in range(2):
                pv = jnp.dot(vt_ref[0, c], p_bufs[u % 2][m], preferred_element_type=F32)
                sums[m] = pv if sums[m] is None else sums[m] + pv
        for m in range(2):
            acc_ref[m] += sums[m]

    def loop_body(it, carry):
        group(it * unroll, unroll, True)
        return carry

    def one_pass(_):
        acc_ref[...] = jnp.zeros(acc_ref.shape, F32)
        den_ref[...] = jnp.zeros(den_ref.shape, F32)
        n_groups = nchunk // unroll
        probs(0, p_a)
        group(0, unroll, True)
        lax.fori_loop(1, n_groups - 1, loop_body, 0)
        group((n_groups - 1) * unroll, unroll, False)
        lowest = None
        for m in range(2):
            den = jnp.sum(den_ref[m], axis=0, keepdims=True)
            mu_ref[m] = jnp.where(den < REF_FLOOR, mu_ref[m] - 2.0 * REF_MARGIN, mu_ref[m])
            lowest = jnp.min(den) if lowest is None else jnp.minimum(lowest, jnp.min(den))
        return (lowest < REF_FLOOR).astype(jnp.int32)

    lax.while_loop(lambda retry: retry > 0, one_pass, jnp.int32(1))

    lam = (jnp.exp(jnp.sum(lq1_ref[...] * lk1_ref[...], axis=-1, keepdims=True))
           - jnp.exp(jnp.sum(lq2_ref[...] * lk2_ref[...], axis=-1, keepdims=True)) + lambda_init)
    den1 = jnp.sum(den_ref[0], axis=0, keepdims=True)
    den2 = jnp.sum(den_ref[1], axis=0, keepdims=True)
    o = acc_ref[0] / den1 - lam * (acc_ref[1] / den2)
    r = lax.rsqrt(jnp.mean(o * o, axis=0, keepdims=True) + NORM_EPS)
    y = (o * r * subw_ref[...]) * (1.0 - lambda_init)
    o_ref[...] = (y.T * jax.nn.silu(g_ref[...])).astype(o_ref.dtype)


def _diff_attn(qt, ka, vt, lq1, lk1, lq2, lk2, subw, g, lambda_init):
    s = qt.shape[-1]
    g_col0 = (g.shape[1] - DIFF_WIDTH) // HEAD_V
    nchunk = s // ATT_TK
    unroll = min(ATT_UNROLL, nchunk // 2)
    assert unroll % 2 == 0 and nchunk % unroll == 0 and unroll >= ATT_DIAG and s % ATT_TQ == 0
    whole = jnp.asarray([LOG2E * 2.0 ** (-8.0 * (h + 1) / HEADS) for h in range(HEADS)], F32)
    pieces, rest = [], whole
    for _ in range(COEF_PARTS):
        pieces.append(rest.astype(BF16).astype(F32))
        rest = rest - pieces[-1]
    slopes = jnp.stack([whole] + pieces, axis=1).reshape(-1)
    vec = pl.BlockSpec((1, HEAD_DIM), lambda h, i: (0, 0))
    return pl.pallas_call(
        functools.partial(_attn_kernel, lambda_init=lambda_init, nchunk=nchunk, unroll=unroll),
        grid=(HEADS, s // ATT_TQ),
        in_specs=[pl.BlockSpec(memory_space=pltpu.SMEM),
                  pl.BlockSpec((1, 2, LANES, ATT_TQ), lambda h, i: (h, 0, 0, i)),
                  pl.BlockSpec((1, 2, nchunk, ATT_TK, LANES), lambda h, i: (h, 0, 0, 0, 0)),
                  pl.BlockSpec((1, nchunk, HEAD_V, ATT_TK), lambda h, i: (h, 0, 0, 0)),
                  vec, vec, vec, vec,
                  pl.BlockSpec((HEAD_V, 1), lambda h, i: (0, 0)),
                  pl.BlockSpec((ATT_TQ, HEAD_V), lambda h, i: (i, g_col0 + h))],
        out_specs=pl.BlockSpec((ATT_TQ, HEAD_V), lambda h, i: (i, h)),
        out_shape=jax.ShapeDtypeStruct((s, DIFF_WIDTH), BF16),
        scratch_shapes=[pltpu.VMEM((2, HEAD_V, ATT_TQ), F32), pltpu.VMEM((2, 1, ATT_TQ), F32),
                        pltpu.VMEM((3, 2, LANES, ATT_TQ), BF16), pltpu.SMEM((2,), F32),
                        pltpu.VMEM((2, ATT_TK, ATT_TQ), BF16), pltpu.VMEM((2, ATT_TK, ATT_TQ), BF16),
                        pltpu.VMEM((2, SUBLANES, ATT_TQ), F32)],
        compiler_params=_params(2),
        name="diff_attn",
    )(slopes, qt, ka, vt, lq1, lk1, lq2, lk2, subw.reshape(HEAD_V, 1), g)


def _pool_kernel(prev_ref, cur_ref, next_ref, w_ref, scale_ref, g_ref, o_ref, ext_ref, *, tm, seq):
    i = pl.program_id(0)
    last = pl.num_programs(0) - 1
    zero_halo = jnp.zeros((POOL_HALO, POOL_WIDTH), F32)
    ext_ref[0:POOL_HALO] = jnp.where(i > 0, prev_ref[...], zero_halo)
    ext_ref[POOL_HALO:POOL_HALO + tm] = cur_ref[...]
    ext_ref[POOL_HALO + tm:POOL_HALO + tm + POOL_HALO] = jnp.where(i < last, next_ref[...], zero_halo)
    t = lax.broadcasted_iota(jnp.int32, (tm, 1), 0) + i * tm
    for g, w in enumerate(POOL_WINDOWS):
        cols = pl.ds(g * POOL_GROUP, POOL_GROUP)
        win = ext_ref[pl.ds(POOL_HALO - w // 2, tm), cols]
        for d in range(-(w // 2) + 1, w - w // 2):
            win = win + ext_ref[pl.ds(POOL_HALO + d, tm), cols]
        lo = jnp.maximum(t - w // 2, 0)
        hi = jnp.minimum(t + (w - w // 2) - 1, seq - 1)
        cnt = (hi - lo + 1).astype(F32)
        pooled = win / cnt - cur_ref[:, cols]
        y = jnp.dot(pooled.astype(BF16), w_ref[g], preferred_element_type=F32)
        o_ref[:, cols] = (y * scale_ref[:, cols] * jax.nn.silu(g_ref[:, cols])).astype(o_ref.dtype)


def _pool_mix(u, w_pool, pool_scale, g):
    s, width = u.shape
    tm = 512
    hb = tm // POOL_HALO
    nhalo = s // POOL_HALO
    return pl.pallas_call(
        functools.partial(_pool_kernel, tm=tm, seq=s),
        grid=(s // tm,),
        in_specs=[pl.BlockSpec((POOL_HALO, width), lambda i: (jnp.maximum(i * hb - 1, 0), 0)),
                  pl.BlockSpec((tm, width), lambda i: (i, 0)),
                  pl.BlockSpec((POOL_HALO, width), lambda i: (jnp.minimum((i + 1) * hb, nhalo - 1), 0)),
                  pl.BlockSpec(w_pool.shape, lambda i: (0, 0, 0)),
                  pl.BlockSpec((1, width), lambda i: (0, 0)),
                  pl.BlockSpec((tm, width), lambda i: (i, 0))],
        out_specs=pl.BlockSpec((tm, width), lambda i: (i, 0)),
        out_shape=jax.ShapeDtypeStruct((s, width), BF16),
        scratch_shapes=[pltpu.VMEM((tm + 2 * POOL_HALO, width), F32)],
        compiler_params=_params(1),
        name="pool_mix",
    )(u, u, u, w_pool, pool_scale, g)


def _out_kernel(*refs, widths, tm, modulate_next):
    n = len(widths)
    y_refs = refs[:n]
    w_ref, x_ref, gate_ref = refs[n:n + 3]
    if modulate_next:
        nw_ref, shift_ref, scale_ref, o_ref, h_ref = refs[n + 3:]
        mul = nw_ref[...]
        one_plus_scale = 1.0 + scale_ref[...]
    else:
        (o_ref,) = refs[n + 3:]
    for sb in range(tm // OUT_SUB):
        rows = slice(sb * OUT_SUB, (sb + 1) * OUT_SUB)
        acc = None
        off = 0
        for y_ref, wd in zip(y_refs, widths):
            part = jnp.dot(y_ref[rows, :].astype(BF16), w_ref[off:off + wd, :], preferred_element_type=F32)
            acc = part if acc is None else acc + part
            off += wd
        xn = x_ref[rows, :] + gate_ref[...] * acc
        o_ref[rows, :] = xn
        if modulate_next:
            r = lax.rsqrt(jnp.mean(xn * xn, axis=-1, keepdims=True) + NORM_EPS)
            h_ref[rows, :] = ((xn * r * mul) * one_plus_scale + shift_ref[...]).astype(BF16)


def _out_proj(ys, w, layer, x, gate, next_mod=None):
    s, d = x.shape
    k = w.shape[1]
    tm = min(512, s)
    widths = tuple(y.shape[1] for y in ys)
    vec = pl.BlockSpec((1, d), lambda i: (0, 0))
    rows = pl.BlockSpec((tm, d), lambda i: (i, 0))
    modulate_next = next_mod is not None
    return pl.pallas_call(
        functools.partial(_out_kernel, widths=widths, tm=tm, modulate_next=modulate_next),
        grid=(s // tm,),
        in_specs=[pl.BlockSpec((tm, wd), lambda i: (i, 0)) for wd in widths]
        + [pl.BlockSpec((None, k, d), lambda i: (layer, 0, 0)), rows, vec] + ([vec, vec, vec] if modulate_next else []),
        out_specs=[rows, rows] if modulate_next else rows,
        out_shape=([jax.ShapeDtypeStruct((s, d), F32), jax.ShapeDtypeStruct((s, d), BF16)] if modulate_next
                   else jax.ShapeDtypeStruct((s, d), F32)),
        compiler_params=_params(1),
        name="out_proj",
    )(*ys, w, x, gate, *(next_mod if modulate_next else ()))


def _dft_split(s):
    n1 = 1 << (int(math.log2(s)) // 2)
    n2 = s // n1
    assert n1 * n2 == s and n1 % SUBLANES == 0 and n2 % SUBLANES == 0
    return n1, n2


def _angles(num, den):
    ang = (2.0 * math.pi / den) * (num % den).astype(F32)
    return jnp.cos(ang), jnp.sin(ang)


def _fold_cs_kernel(cs_ref, wf_ref, o_ref):
    wf = wf_ref[...]
    o_ref[:, 0:FOURIER_GROUP] = jnp.dot(cs_ref[0], wf, preferred_element_type=F32,
                                        precision=lax.Precision.HIGHEST)
    o_ref[:, FOURIER_GROUP:] = jnp.dot(cs_ref[1], wf, preferred_element_type=F32,
                                       precision=lax.Precision.HIGHEST)


def _fold_channel_dft(w_fourier_j, seq):
    fg = FOURIER_GROUP
    idx = jnp.arange(fg, dtype=jnp.int32)
    cc, sc = _angles(idx[:, None] * idx[None, :], fg)
    cs = jnp.stack([cc, sc]) * (1.0 / math.sqrt(seq * fg))
    return pl.pallas_call(
        _fold_cs_kernel,
        grid=(FOURIER_GROUPS,),
        in_specs=[pl.BlockSpec((2, fg, fg), lambda g: (0, 0, 0)),
                  pl.BlockSpec((None, fg, fg), lambda g: (g, 0, 0))],
        out_specs=pl.BlockSpec((None, fg, 2 * fg), lambda g: (g, 0, 0)),
        out_shape=jax.ShapeDtypeStruct((FOURIER_GROUPS, fg, 2 * fg), F32),
        compiler_params=_params(1),
        name="fold_channel_dft",
    )(cs, w_fourier_j)


def _fold_win_kernel(w_ref, ab_ref, o_ref):
    o_ref[...] = jnp.dot(w_ref[...], ab_ref[...].astype(BF16), preferred_element_type=F32).astype(BF16)


def _fold_in_proj(w_in, layer, ab):
    d = w_in.shape[1]
    fg = FOURIER_GROUP
    return pl.pallas_call(
        _fold_win_kernel,
        grid=(FOURIER_GROUPS, 2),
        in_specs=[pl.BlockSpec((None, d, fg), lambda g, t: (layer, 0, g)),
                  pl.BlockSpec((None, fg, fg), lambda g, t: (g, 0, t))],
        out_specs=pl.BlockSpec((d, fg), lambda g, t: (0, g + t * FOURIER_GROUPS)),
        out_shape=jax.ShapeDtypeStruct((d, 2 * D_MODEL), BF16),
        compiler_params=_params(2),
        name="fold_in_proj",
    )(w_in, ab)


def _dft1_kernel(f_ref, p_ref, q_ref, t_ref, *, nb, n1):
    f = f_ref[...]
    for jj in range(nb):
        rows = slice(jj * n1, (jj + 1) * n1)
        rhs = jnp.concatenate([p_ref[rows, :], q_ref[rows, :]], axis=0)
        t_ref[jj] = jnp.dot(f, rhs, preferred_element_type=F32)


def _dft_stage1(pq_t, f1, n1, n2):
    s, c2 = pq_t.shape
    c = c2 // 2
    nb = SUBLANES
    cb = min(1024, c)
    ncb = c // cb
    return pl.pallas_call(
        functools.partial(_dft1_kernel, nb=nb, n1=n1),
        grid=(n2 // nb, ncb),
        in_specs=[pl.BlockSpec((2 * n1, 2 * n1), lambda b, j: (0, 0)),
                  pl.BlockSpec((nb * n1, cb), lambda b, j: (b, j)),
                  pl.BlockSpec((nb * n1, cb), lambda b, j: (b, j + ncb))],
        out_specs=pl.BlockSpec((nb, 2 * n1, cb), lambda b, j: (b, 0, j)),
        out_shape=jax.ShapeDtypeStruct((n2, 2 * n1, c), F32),
        compiler_params=_params(2),
        name="dft_stage1",
    )(f1, pq_t, pq_t)


def _dft2_kernel(l_ref, t_ref, g_ref, o_ref, *, n2, cb):
    rhs = t_ref[...].reshape(n2 * 2 * SUBLANES, cb).astype(BF16)
    y = jnp.dot(l_ref[0], rhs, preferred_element_type=F32)
    o_ref[...] = y.reshape(n2, SUBLANES, cb) * jax.nn.silu(g_ref[...])


def _dft_stage2(t_nk, l2, g, n1, n2):
    c = t_nk.shape[-1]
    cb = min(1024, c)
    row_blocks = pl.BlockSpec((n2, SUBLANES, cb), lambda g, j: (0, g, j))
    out = pl.pallas_call(
        functools.partial(_dft2_kernel, n2=n2, cb=cb),
        grid=(n1 // SUBLANES, c // cb),
        in_specs=[pl.BlockSpec((1, SUBLANES * n2, SUBLANES * 2 * n2), lambda g, j: (g, 0, 0)),
                  pl.BlockSpec((n2, 2, SUBLANES, cb), lambda g, j: (0, 0, g, j)),
                  row_blocks],
        out_specs=row_blocks,
        out_shape=jax.ShapeDtypeStruct((n2, n1, c), F32),
        compiler_params=_params(2),
        name="dft_stage2",
    )(l2, t_nk, g.reshape(n2, n1, c))
    return out.reshape(n1 * n2, c)


def _dft_matrices(s):
    n1, n2 = _dft_split(s)
    a = jnp.arange(n1, dtype=jnp.int32)
    c1, s1 = _angles(a[:, None] * a[None, :], n1)
    f1 = jnp.concatenate([jnp.concatenate([c1, -s1], axis=1),
                          jnp.concatenate([-s1, -c1], axis=1)], axis=0).astype(BF16)
    groups = n1 // SUBLANES
    g = jnp.arange(groups, dtype=jnp.int32)[:, None, None, None]
    k2 = jnp.arange(n2, dtype=jnp.int32)[None, :, None, None]
    j = jnp.arange(SUBLANES, dtype=jnp.int32)[None, None, :, None]
    nn = jnp.arange(n2, dtype=jnp.int32)[None, None, None, :]
    c2, s2 = _angles((SUBLANES * g + j + n1 * k2) * nn, s)
    cs = jnp.stack([c2, s2], axis=4).astype(BF16)
    return n1, n2, f1, _expand_twiddles(cs.reshape(groups, n2 * SUBLANES, n2 * 2))


def _expand_kernel(cs_ref, e_ref, o_ref):
    spread = jnp.dot(cs_ref[0], e_ref[...], preferred_element_type=F32)
    row = lax.broadcasted_iota(jnp.int32, spread.shape, 0)
    col = lax.broadcasted_iota(jnp.int32, spread.shape, 1)
    o_ref[0] = jnp.where(row % SUBLANES == col % SUBLANES, spread, 0.0).astype(BF16)


def _expand_twiddles(cs):
    groups, rows, cols = cs.shape
    wide = cols * SUBLANES
    e = (jnp.arange(wide, dtype=jnp.int32)[None, :] // SUBLANES == jnp.arange(cols, dtype=jnp.int32)[:, None])
    return pl.pallas_call(
        _expand_kernel,
        grid=(groups,),
        in_specs=[pl.BlockSpec((1, rows, cols), lambda g: (g, 0, 0)),
                  pl.BlockSpec((cols, wide), lambda g: (0, 0))],
        out_specs=pl.BlockSpec((1, rows, wide), lambda g: (g, 0, 0)),
        out_shape=jax.ShapeDtypeStruct((groups, rows, wide), BF16),
        compiler_params=_params(1),
        name="expand_twiddles",
    )(cs, e.astype(BF16))


def kernel(x, c, norm_w, ada_w, ada_b, w_in_ab, w_pool, pool_scale, q_norm_w, k_norm_w, lambda_q1, lambda_k1,
           lambda_q2, lambda_k2, subln_w, w_out_ab, w_in_c, w_fourier, w_out_c):
    batch, s, d = x.shape
    assert batch == 1 and d == D_MODEL and s % ATT_TK == 0
    xs = x.reshape(s, d)
    mod = _ada_mod(c, ada_w, ada_b)
    n1, n2, f1, l2 = _dft_matrices(s)
    row = lambda v: v.reshape(1, -1)

    def mod_of(i):
        return mod[i, :, 0:d], mod[i, :, d:2 * d], mod[i, :, 2 * d:3 * d]

    def finish(ys, w_out, j, xs, i):
        gate = mod_of(i)[2]
        if i + 1 == DEPTH:
            return _out_proj(ys, w_out, j, xs, gate), None
        shift, scale, _ = mod_of(i + 1)
        return _out_proj(ys, w_out, j, xs, gate, (row(norm_w[i + 1]), shift, scale))

    w_in_ab, w_out_ab, w_in_c, w_out_c = (w.astype(BF16) for w in (w_in_ab, w_out_ab, w_in_c, w_out_c))
    h = _modulate(xs, row(norm_w[0]), *mod_of(0)[:2])
    for i in range(DEPTH):
        j = i // 2
        if i % 2 == 0:
            o1, o2, o3, o4 = POOL_WIDTH, POOL_WIDTH + DIFF_WIDTH, POOL_WIDTH + 2 * DIFF_WIDTH, POOL_WIDTH + 3 * DIFF_WIDTH
            lambda_init = 0.8 - 0.6 * math.exp(-0.3 * i)
            u_pool = _matmul(h, w_in_ab, F32, PROJ_TM, PROJ_TN, name="pool_in_proj", cols=(0, o1), layer=j)
            gte = _matmul(h, w_in_ab, F32, PROJ_TM, PROJ_TN, name="gate_in_proj", cols=(o4, o4 + AB_WIDTH), layer=j)
            qt, ka, vt = _qkv_proj(h, w_in_ab, j, o1, o2, o3, row(q_norm_w[j]), row(k_norm_w[j]))
            y_a = _pool_mix(u_pool, w_pool[j].astype(BF16), row(pool_scale[j]), gte)
            y_b = _diff_attn(qt, ka, vt, row(lambda_q1[j]), row(lambda_k1[j]), row(lambda_q2[j]),
                             row(lambda_k2[j]), subln_w[j], gte, lambda_init)
            xs, h = finish([y_a, y_b], w_out_ab, j, xs, i)
        else:
            ab = _fold_channel_dft(w_fourier[j], s)
            w_pq = _fold_in_proj(w_in_c, j, ab)
            gte = _matmul(h, w_in_c, F32, PROJ_TM, PROJ_TN, name="gate_in_proj", cols=(d, 2 * d), layer=j)
            h_t = h.reshape(n1, n2, d).transpose(1, 0, 2).reshape(s, d)
            pq_t = _matmul(h_t, w_pq, BF16, PROJ_TM, PROJ_TN, name="fourier_in_proj")
            t_nk = _dft_stage1(pq_t, f1, n1, n2)
            y = _dft_stage2(t_nk.reshape(n2, 2, n1, d), l2, gte, n1, n2)
            xs, h = finish([y], w_out_c, j, xs, i)
    return xs.reshape(batch, s, d)
```

```python
import functools
import math

import jax
import jax.numpy as jnp
from jax import lax
from jax.experimental import pallas as pl
from jax.experimental.pallas import tpu as pltpu

F32 = jnp.float32
BF16 = jnp.bfloat16

D_MODEL = 2048
DEPTH = 4
NORM_EPS = 1e-6

POOL_WINDOWS = (2, 4, 8, 16)
POOL_WIDTH = D_MODEL // 2
POOL_GROUP = POOL_WIDTH // len(POOL_WINDOWS)
POOL_HALO = 8

HEADS = 8
HEAD_DIM = 64
HEAD_V = 2 * HEAD_DIM
DIFF_WIDTH = HEADS * HEAD_V
AB_WIDTH = POOL_WIDTH + DIFF_WIDTH

FOURIER_GROUPS = 4
FOURIER_GROUP = D_MODEL // FOURIER_GROUPS

LANES = 128
SUBLANES = 8
POS_RADIX = 256
COEF_PARTS = 3
LOG2E = math.log2(math.e)

ATT_TQ = 512
ATT_TK = 512
ATT_DIAG = ATT_TQ // ATT_TK
ATT_UNROLL = 8
REF_MARGIN = 60.0
REF_FLOOR = 2.0 ** -60
BOUND_SLACK = 1.001

PROJ_TM, PROJ_TN = 2048, 512
OUT_SUB = 128

VMEM_LIMIT = 56 * 1024 * 1024


def _params(n_axes):
    return pltpu.CompilerParams(dimension_semantics=("arbitrary",) * n_axes,
                                vmem_limit_bytes=VMEM_LIMIT)


def _mod_kernel(c_ref, w_ref, b_ref, o_ref):
    c = c_ref[...]
    o_ref[...] = jnp.sum(jax.nn.silu(c) * w_ref[...], axis=0, keepdims=True) + b_ref[...]


def _ada_mod(c, ada_w, ada_b):
    depth, d, n = ada_w.shape
    tn = 512
    return pl.pallas_call(
        _mod_kernel,
        grid=(depth, n // tn),
        in_specs=[pl.BlockSpec((d, 1), lambda i, j: (0, 0)),
                  pl.BlockSpec((None, d, tn), lambda i, j: (i, 0, j)),
                  pl.BlockSpec((None, 1, tn), lambda i, j: (i, 0, j))],
        out_specs=pl.BlockSpec((None, 1, tn), lambda i, j: (i, 0, j)),
        out_shape=jax.ShapeDtypeStruct((depth, 1, n), F32),
        compiler_params=_params(2),
        name="ada_mod",
    )(c.reshape(d, 1), ada_w, ada_b.reshape(depth, 1, n))


def _modulate_kernel(x_ref, nw_ref, shift_ref, scale_ref, o_ref):
    xf = x_ref[...]
    r = lax.rsqrt(jnp.mean(xf * xf, axis=-1, keepdims=True) + NORM_EPS)
    y = xf * r * nw_ref[...]
    y = y * (1.0 + scale_ref[...]) + shift_ref[...]
    o_ref[...] = y.astype(o_ref.dtype)


def _modulate(x, nw, shift, scale):
    s, d = x.shape
    tm = 512
    vec = pl.BlockSpec((1, d), lambda i: (0, 0))
    return pl.pallas_call(
        _modulate_kernel,
        grid=(s // tm,),
        in_specs=[pl.BlockSpec((tm, d), lambda i: (i, 0)), vec, vec, vec],
        out_specs=pl.BlockSpec((tm, d), lambda i: (i, 0)),
        out_shape=jax.ShapeDtypeStruct((s, d), BF16),
        compiler_params=_params(1),
        name="modulate",
    )(x, nw, shift, scale)


def _mm_kernel(a_ref, w_ref, o_ref, *, precision):
    o_ref[...] = jnp.dot(a_ref[...], w_ref[...], preferred_element_type=F32,
                         precision=precision).astype(o_ref.dtype)


def _matmul(a, w, out_dtype, tm, tn, precision=None, name="matmul", cols=None, layer=None):
    m, k = a.shape
    col0, col1 = cols if cols is not None else (0, w.shape[-1])
    n = col1 - col0
    tm, tn = min(tm, m), min(tn, n)
    assert col0 % tn == 0 and n % tn == 0
    jb = col0 // tn
    if layer is None:
        wspec = pl.BlockSpec((k, tn), lambda i, j: (0, j + jb))
    else:
        wspec = pl.BlockSpec((None, k, tn), lambda i, j: (layer, 0, j + jb))
    return pl.pallas_call(
        functools.partial(_mm_kernel, precision=precision),
        grid=(m // tm, n // tn),
        in_specs=[pl.BlockSpec((tm, k), lambda i, j: (i, 0)), wspec],
        out_specs=pl.BlockSpec((tm, tn), lambda i, j: (i, j)),
        out_shape=jax.ShapeDtypeStruct((m, n), out_dtype),
        compiler_params=_params(2),
        name=name,
    )(a, w)


def _qkv_kernel(h_ref, wq_ref, wk_ref, wv_ref, qnw_ref, knw_ref, qt_ref, ka_ref, vt_ref, *, tm):
    lane = lax.broadcasted_iota(jnp.int32, (ATT_TK, LANES), 1)
    first = lane < HEAD_DIM
    pos = lax.broadcasted_iota(jnp.int32, (ATT_TK, LANES), 0)
    pos_lo = (pos % POS_RADIX).astype(F32)
    pos_hi = (pos // POS_RADIX).astype(F32)
    pos_cols = jnp.where(lane < HEAD_DIM + COEF_PARTS, pos_lo, jnp.where(lane < HEAD_DIM + 2 * COEF_PARTS, pos_hi, 0.0))

    def halves_rms(z, w):
        sq = z * z
        ss1 = jnp.sum(jnp.where(first, sq, 0.0), axis=-1, keepdims=True)
        ss2 = jnp.sum(jnp.where(first, 0.0, sq), axis=-1, keepdims=True)
        r = jnp.where(first, lax.rsqrt(ss1 / HEAD_DIM + NORM_EPS), lax.rsqrt(ss2 / HEAD_DIM + NORM_EPS))
        return z * r * w

    for cc in range(tm // ATT_TK):
        rows = slice(cc * ATT_TK, (cc + 1) * ATT_TK)
        h = h_ref[rows, :]
        zq = jnp.dot(h, wq_ref[...], preferred_element_type=F32)
        zk = jnp.dot(h, wk_ref[...], preferred_element_type=F32)
        zv = jnp.dot(h, wv_ref[...], preferred_element_type=F32)
        for hh in range(2):
            cols = slice(hh * LANES, (hh + 1) * LANES)
            qn = halves_rms(zq[:, cols], qnw_ref[...]) * (HEAD_DIM ** -0.5) * LOG2E
            kn = halves_rms(zk[:, cols], knw_ref[...])
            for m in range(2):
                qm = qn if m == 0 else pltpu.roll(qn, HEAD_DIM, axis=1)
                km = kn if m == 0 else pltpu.roll(kn, HEAD_DIM, axis=1)
                qt_ref[hh, m, :, rows] = jnp.where(first, qm, 0.0).T.astype(BF16)
                ka_ref[hh, m, cc] = jnp.where(first, km, pos_cols).astype(BF16)
            vt_ref[hh, cc] = zv[:, cols].T.astype(BF16)


def _qkv_proj(h, w, layer, col_q, col_k, col_v, qnw, knw):
    s, d = h.shape
    tm = min(2048, s)
    nchunk = s // ATT_TK
    cpt = tm // ATT_TK
    tn = 2 * LANES
    assert col_q % tn == 0 and col_k % tn == 0 and col_v % tn == 0
    wspecs = [pl.BlockSpec((None, d, tn), functools.partial(lambda i, j, jb: (layer, 0, j + jb), jb=c0 // tn))
              for c0 in (col_q, col_k, col_v)]
    vec = pl.BlockSpec((1, LANES), lambda i, j: (0, 0))
    return pl.pallas_call(
        functools.partial(_qkv_kernel, tm=tm),
        grid=(s // tm, HEADS // 2),
        in_specs=[pl.BlockSpec((tm, d), lambda i, j: (i, 0)), *wspecs, vec, vec],
        out_specs=[pl.BlockSpec((2, 2, LANES, tm), lambda i, j: (j, 0, 0, i)),
                   pl.BlockSpec((2, 2, cpt, ATT_TK, LANES), lambda i, j: (j, 0, i, 0, 0)),
                   pl.BlockSpec((2, cpt, HEAD_V, ATT_TK), lambda i, j: (j, i, 0, 0))],
        out_shape=[jax.ShapeDtypeStruct((HEADS, 2, LANES, s), BF16),
                   jax.ShapeDtypeStruct((HEADS, 2, nchunk, ATT_TK, LANES), BF16),
                   jax.ShapeDtypeStruct((HEADS, nchunk, HEAD_V, ATT_TK), BF16)],
        compiler_params=_params(2),
        name="qkv_proj",
    )(h, w, w, w, jnp.tile(qnw, (1, 2)), jnp.tile(knw, (1, 2)))


def _attn_kernel(slopes_ref, qt_ref, ka_ref, vt_ref, lq1_ref, lk1_ref, lq2_ref, lk2_ref, subw_ref, g_ref,
                 o_ref, acc_ref, mu_ref, qv_ref, kmax_ref, p_a, p_b, den_ref, *, lambda_init, nchunk, unroll):
    hd = pl.program_id(0)
    i = pl.program_id(1)
    base = hd * (1 + COEF_PARTS)
    slope = slopes_ref[base]
    rowq = lax.broadcasted_iota(jnp.int32, (LANES, ATT_TQ), 0)
    q_off = lax.broadcasted_iota(jnp.int32, (1, ATT_TQ), 1).astype(F32)

    row1 = lax.broadcasted_iota(jnp.int32, (LANES, 1), 0)
    coef_col = jnp.zeros((LANES, 1), F32)
    for t in range(COEF_PARTS):
        piece = slopes_ref[base + 1 + t]
        coef_col = jnp.where(row1 == HEAD_DIM + t, piece, coef_col)
        coef_col = jnp.where(row1 == HEAD_DIM + COEF_PARTS + t, piece * POS_RADIX, coef_col)
    for m in range(2):
        qf = qt_ref[0, m].astype(F32)
        for kind, sign in ((0, -1.0), (1, 0.0), (2, 1.0)):
            qv_ref[kind, m] = jnp.where(rowq >= HEAD_DIM, sign * coef_col, qf).astype(BF16)

    @pl.when(i == 0)
    def _():
        lane = lax.broadcasted_iota(jnp.int32, (ATT_TK, LANES), 1)
        for m in range(2):
            def widest(c, best):
                kc = ka_ref[0, m, c].astype(F32)
                return jnp.maximum(best, jnp.sum(jnp.where(lane < HEAD_DIM, kc * kc, 0.0), axis=1, keepdims=True))
            kmax_ref[m] = jnp.max(lax.fori_loop(0, nchunk, widest, jnp.zeros((ATT_TK, 1), F32)))

    for m in range(2):
        qf = qt_ref[0, m].astype(F32)
        bound = jnp.sqrt(jnp.sum(qf * qf, axis=0, keepdims=True) * kmax_ref[m]) * BOUND_SLACK
        mu_ref[m] = bound - REF_MARGIN

    c_diag = i * ATT_DIAG

    def chunk_of(e):
        if isinstance(e, int) and e < ATT_DIAG:
            return c_diag + e
        rest = e - ATT_DIAG
        return jnp.where(e < ATT_DIAG, c_diag + e, rest + ATT_DIAG * (rest >= c_diag).astype(jnp.int32))

    def side_of(c):
        return (c < c_diag).astype(jnp.int32) - (c >= c_diag + ATT_DIAG).astype(jnp.int32)

    p_bufs = (p_a, p_b)

    def probs(e, p_ref):
        c = chunk_of(e)
        side = side_of(c)
        sidef = side.astype(F32)
        gap = jnp.abs(c * ATT_TK - i * ATT_TQ).astype(F32)
        cvec = -slope * (sidef * sidef * gap + sidef * q_off)
        for m in range(2):
            st = jnp.dot(ka_ref[0, m, c], qv_ref[1 + side, m], preferred_element_type=F32)
            if isinstance(e, int) and e < ATT_DIAG:
                key_off = lax.broadcasted_iota(jnp.int32, (ATT_TK, ATT_TQ), 0) + e * ATT_TK
                qry_off = lax.broadcasted_iota(jnp.int32, (ATT_TK, ATT_TQ), 1)
                st = st - slope * jnp.abs(key_off - qry_off).astype(F32)
            p = jnp.exp2(st - (mu_ref[m] - cvec))
            p_ref[m] = p.astype(BF16)
            den_ref[m] += jnp.sum(p.reshape(ATT_TK // SUBLANES, SUBLANES, ATT_TQ), axis=0)

    def group(e0, count, probs_last):
        sums = [None, None]
        for u in range(count):
            if u < count - 1 or probs_last:
                probs(e0 + u + 1, p_bufs[(u + 1) % 2])
            c = chunk_of(e0 + u)
            for m in range(2):
                pv = jnp.dot(vt_ref[0, c], p_bufs[u % 2][m], preferred_element_type=F32)
                sums[m] = pv if sums[m] is None else sums[m] + pv
        for m in range(2):
            acc_ref[m] += sums[m]

    def loop_body(it, carry):
        group(it * unroll, unroll, True)
        return carry

    def one_pass(_):
        acc_ref[...] = jnp.zeros(acc_ref.shape, F32)
        den_ref[...] = jnp.zeros(den_ref.shape, F32)
        n_groups = nchunk // unroll
        probs(0, p_a)
        group(0, unroll, True)
        lax.fori_loop(1, n_groups - 1, loop_body, 0)
        group((n_groups - 1) * unroll, unroll, False)
        lowest = None
        for m in range(2):
            den = jnp.sum(den_ref[m], axis=0, keepdims=True)
            mu_ref[m] = jnp.where(den < REF_FLOOR, mu_ref[m] - 2.0 * REF_MARGIN, mu_ref[m])
            lowest = jnp.min(den) if lowest is None else jnp.minimum(lowest, jnp.min(den))
        return (lowest < REF_FLOOR).astype(jnp.int32)

    lax.while_loop(lambda retry: retry > 0, one_pass, jnp.int32(1))

    lam = (jnp.exp(jnp.sum(lq1_ref[...] * lk1_ref[...], axis=-1, keepdims=True))
           - jnp.exp(jnp.sum(lq2_ref[...] * lk2_ref[...], axis=-1, keepdims=True)) + lambda_init)
    den1 = jnp.sum(den_ref[0], axis=0, keepdims=True)
    den2 = jnp.sum(den_ref[1], axis=0, keepdims=True)
    o = acc_ref[0] / den1 - lam * (acc_ref[1] / den2)
    r = lax.rsqrt(jnp.mean(o * o, axis=0, keepdims=True) + NORM_EPS)
    y = (o * r * subw_ref[...]) * (1.0 - lambda_init)
    o_ref[...] = (y.T * jax.nn.silu(g_ref[...])).astype(o_ref.dtype)


def _diff_attn(qt, ka, vt, lq1, lk1, lq2, lk2, subw, g, lambda_init):
    s = qt.shape[-1]
    g_col0 = (g.shape[1] - DIFF_WIDTH) // HEAD_V
    nchunk = s // ATT_TK
    unroll = min(ATT_UNROLL, nchunk // 2)
    assert unroll % 2 == 0 and nchunk % unroll == 0 and unroll >= ATT_DIAG and s % ATT_TQ == 0
    whole = jnp.asarray([LOG2E * 2.0 ** (-8.0 * (h + 1) / HEADS) for h in range(HEADS)], F32)
    pieces, rest = [], whole
    for _ in range(COEF_PARTS):
        pieces.append(rest.astype(BF16).astype(F32))
        rest = rest - pieces[-1]
    slopes = jnp.stack([whole] + pieces, axis=1).reshape(-1)
    vec = pl.BlockSpec((1, HEAD_DIM), lambda h, i: (0, 0))
    return pl.pallas_call(
        functools.partial(_attn_kernel, lambda_init=lambda_init, nchunk=nchunk, unroll=unroll),
        grid=(HEADS, s // ATT_TQ),
        in_specs=[pl.BlockSpec(memory_space=pltpu.SMEM),
                  pl.BlockSpec((1, 2, LANES, ATT_TQ), lambda h, i: (h, 0, 0, i)),
                  pl.BlockSpec((1, 2, nchunk, ATT_TK, LANES), lambda h, i: (h, 0, 0, 0, 0)),
                  pl.BlockSpec((1, nchunk, HEAD_V, ATT_TK), lambda h, i: (h, 0, 0, 0)),
                  vec, vec, vec, vec,
                  pl.BlockSpec((HEAD_V, 1), lambda h, i: (0, 0)),
                  pl.BlockSpec((ATT_TQ, HEAD_V), lambda h, i: (i, g_col0 + h))],
        out_specs=pl.BlockSpec((ATT_TQ, HEAD_V), lambda h, i: (i, h)),
        out_shape=jax.ShapeDtypeStruct((s, DIFF_WIDTH), BF16),
        scratch_shapes=[pltpu.VMEM((2, HEAD_V, ATT_TQ), F32), pltpu.VMEM((2, 1, ATT_TQ), F32),
                        pltpu.VMEM((3, 2, LANES, ATT_TQ), BF16), pltpu.SMEM((2,), F32),
                        pltpu.VMEM((2, ATT_TK, ATT_TQ), BF16), pltpu.VMEM((2, ATT_TK, ATT_TQ), BF16),
                        pltpu.VMEM((2, SUBLANES, ATT_TQ), F32)],
        compiler_params=_params(2),
        name="diff_attn",
    )(slopes, qt, ka, vt, lq1, lk1, lq2, lk2, subw.reshape(HEAD_V, 1), g)


def _pool_kernel(prev_ref, cur_ref, next_ref, w_ref, scale_ref, g_ref, o_ref, ext_ref, *, tm, seq):
    i = pl.program_id(0)
    last = pl.num_programs(0) - 1
    zero_halo = jnp.zeros((POOL_HALO, POOL_WIDTH), F32)
    ext_ref[0:POOL_HALO] = jnp.where(i > 0, prev_ref[...], zero_halo)
    ext_ref[POOL_HALO:POOL_HALO + tm] = cur_ref[...]
    ext_ref[POOL_HALO + tm:POOL_HALO + tm + POOL_HALO] = jnp.where(i < last, next_ref[...], zero_halo)
    t = lax.broadcasted_iota(jnp.int32, (tm, 1), 0) + i * tm
    for g, w in enumerate(POOL_WINDOWS):
        cols = pl.ds(g * POOL_GROUP, POOL_GROUP)
        win = ext_ref[pl.ds(POOL_HALO - w // 2, tm), cols]
        for d in range(-(w // 2) + 1, w - w // 2):
            win = win + ext_ref[pl.ds(POOL_HALO + d, tm), cols]
        lo = jnp.maximum(t - w // 2, 0)
        hi = jnp.minimum(t + (w - w // 2) - 1, seq - 1)
        cnt = (hi - lo + 1).astype(F32)
        pooled = win / cnt - cur_ref[:, cols]
        y = jnp.dot(pooled.astype(BF16), w_ref[g], preferred_element_type=F32)
        o_ref[:, cols] = (y * scale_ref[:, cols] * jax.nn.silu(g_ref[:, cols])).astype(o_ref.dtype)


def _pool_mix(u, w_pool, pool_scale, g):
    s, width = u.shape
    tm = 512
    hb = tm // POOL_HALO
    nhalo = s // POOL_HALO
    return pl.pallas_call(
        functools.partial(_pool_kernel, tm=tm, seq=s),
        grid=(s // tm,),
        in_specs=[pl.BlockSpec((POOL_HALO, width), lambda i: (jnp.maximum(i * hb - 1, 0), 0)),
                  pl.BlockSpec((tm, width), lambda i: (i, 0)),
                  pl.BlockSpec((POOL_HALO, width), lambda i: (jnp.minimum((i + 1) * hb, nhalo - 1), 0)),
                  pl.BlockSpec(w_pool.shape, lambda i: (0, 0, 0)),
                  pl.BlockSpec((1, width), lambda i: (0, 0)),
                  pl.BlockSpec((tm, width), lambda i: (i, 0))],
        out_specs=pl.BlockSpec((tm, width), lambda i: (i, 0)),
        out_shape=jax.ShapeDtypeStruct((s, width), BF16),
        scratch_shapes=[pltpu.VMEM((tm + 2 * POOL_HALO, width), F32)],
        compiler_params=_params(1),
        name="pool_mix",
    )(u, u, u, w_pool, pool_scale, g)


def _out_kernel(*refs, widths, tm, modulate_next):
    n = len(widths)
    y_refs = refs[:n]
    w_ref, x_ref, gate_ref = refs[n:n + 3]
    if modulate_next:
        nw_ref, shift_ref, scale_ref, o_ref, h_ref = refs[n + 3:]
        mul = nw_ref[...]
        one_plus_scale = 1.0 + scale_ref[...]
    else:
        (o_ref,) = refs[n + 3:]
    for sb in range(tm // OUT_SUB):
        rows = slice(sb * OUT_SUB, (sb + 1) * OUT_SUB)
        acc = None
        off = 0
        for y_ref, wd in zip(y_refs, widths):
            part = jnp.dot(y_ref[rows, :].astype(BF16), w_ref[off:off + wd, :], preferred_element_type=F32)
            acc = part if acc is None else acc + part
            off += wd
        xn = x_ref[rows, :] + gate_ref[...] * acc
        o_ref[rows, :] = xn
        if modulate_next:
            r = lax.rsqrt(jnp.mean(xn * xn, axis=-1, keepdims=True) + NORM_EPS)
            h_ref[rows, :] = ((xn * r * mul) * one_plus_scale + shift_ref[...]).astype(BF16)


def _out_proj(ys, w, layer, x, gate, next_mod=None):
    s, d = x.shape
    k = w.shape[1]
    tm = min(512, s)
    widths = tuple(y.shape[1] for y in ys)
    vec = pl.BlockSpec((1, d), lambda i: (0, 0))
    rows = pl.BlockSpec((tm, d), lambda i: (i, 0))
    modulate_next = next_mod is not None
    return pl.pallas_call(
        functools.partial(_out_kernel, widths=widths, tm=tm, modulate_next=modulate_next),
        grid=(s // tm,),
        in_specs=[pl.BlockSpec((tm, wd), lambda i: (i, 0)) for wd in widths]
        + [pl.BlockSpec((None, k, d), lambda i: (layer, 0, 0)), rows, vec] + ([vec, vec, vec] if modulate_next else []),
        out_specs=[rows, rows] if modulate_next else rows,
        out_shape=([jax.ShapeDtypeStruct((s, d), F32), jax.ShapeDtypeStruct((s, d), BF16)] if modulate_next
                   else jax.ShapeDtypeStruct((s, d), F32)),
        compiler_params=_params(1),
        name="out_proj",
    )(*ys, w, x, gate, *(next_mod if modulate_next else ()))


def _dft_split(s):
    n1 = 1 << (int(math.log2(s)) // 2)
    n2 = s // n1
    assert n1 * n2 == s and n1 % SUBLANES == 0 and n2 % SUBLANES == 0
    return n1, n2


def _angles(num, den):
    ang = (2.0 * math.pi / den) * (num % den).astype(F32)
    return jnp.cos(ang), jnp.sin(ang)


def _fold_cs_kernel(cs_ref, wf_ref, o_ref):
    wf = wf_ref[...]
    o_ref[:, 0:FOURIER_GROUP] = jnp.dot(cs_ref[0], wf, preferred_element_type=F32,
                                        precision=lax.Precision.HIGHEST)
    o_ref[:, FOURIER_GROUP:] = jnp.dot(cs_ref[1], wf, preferred_element_type=F32,
                                       precision=lax.Precision.HIGHEST)


def _fold_channel_dft(w_fourier_j, seq):
    fg = FOURIER_GROUP
    idx = jnp.arange(fg, dtype=jnp.int32)
    cc, sc = _angles(idx[:, None] * idx[None, :], fg)
    cs = jnp.stack([cc, sc]) * (1.0 / math.sqrt(seq * fg))
    return pl.pallas_call(
        _fold_cs_kernel,
        grid=(FOURIER_GROUPS,),
        in_specs=[pl.BlockSpec((2, fg, fg), lambda g: (0, 0, 0)),
                  pl.BlockSpec((None, fg, fg), lambda g: (g, 0, 0))],
        out_specs=pl.BlockSpec((None, fg, 2 * fg), lambda g: (g, 0, 0)),
        out_shape=jax.ShapeDtypeStruct((FOURIER_GROUPS, fg, 2 * fg), F32),
        compiler_params=_params(1),
        name="fold_channel_dft",
    )(cs, w_fourier_j)


def _fold_win_kernel(w_ref, ab_ref, o_ref):
    o_ref[...] = jnp.dot(w_ref[...], ab_ref[...].astype(BF16), preferred_element_type=F32).astype(BF16)


def _fold_in_proj(w_in, layer, ab):
    d = w_in.shape[1]
    fg = FOURIER_GROUP
    return pl.pallas_call(
        _fold_win_kernel,
        grid=(FOURIER_GROUPS, 2),
        in_specs=[pl.BlockSpec((None, d, fg), lambda g, t: (layer, 0, g)),
                  pl.BlockSpec((None, fg, fg), lambda g, t: (g, 0, t))],
        out_specs=pl.BlockSpec((d, fg), lambda g, t: (0, g + t * FOURIER_GROUPS)),
        out_shape=jax.ShapeDtypeStruct((d, 2 * D_MODEL), BF16),
        compiler_params=_params(2),
        name="fold_in_proj",
    )(w_in, ab)


HIGH_HALF = -65536


def _bf16_bits(v):
    b = lax.bitcast_convert_type(v, jnp.int32)
    return b + 0x7FFF + (lax.shift_right_logical(b, 16) & 1)


def _dft1_kernel(f_ref, p_ref, q_ref, t_ref, *, nb, n1):
    f = f_ref[...]
    for jj in range(nb):
        rows = slice(jj * n1, (jj + 1) * n1)
        rhs = jnp.concatenate([p_ref[rows, :], q_ref[rows, :]], axis=0)
        t = jnp.dot(f, rhs, preferred_element_type=F32)
        t_ref[jj] = (_bf16_bits(t[0:n1]) & HIGH_HALF) | lax.shift_right_logical(_bf16_bits(t[n1:2 * n1]), 16)


def _dft_stage1(pq_t, f1, n1, n2):
    s, c2 = pq_t.shape
    c = c2 // 2
    nb = SUBLANES
    cb = min(1024, c)
    ncb = c // cb
    return pl.pallas_call(
        functools.partial(_dft1_kernel, nb=nb, n1=n1),
        grid=(n2 // nb, ncb),
        in_specs=[pl.BlockSpec((2 * n1, 2 * n1), lambda b, j: (0, 0)),
                  pl.BlockSpec((nb * n1, cb), lambda b, j: (b, j)),
                  pl.BlockSpec((nb * n1, cb), lambda b, j: (b, j + ncb))],
        out_specs=pl.BlockSpec((nb, n1, cb), lambda b, j: (b, 0, j)),
        out_shape=jax.ShapeDtypeStruct((n2, n1, c), jnp.int32),
        compiler_params=_params(2),
        name="dft_stage1",
    )(f1, pq_t, pq_t)


def _dft2_kernel(l_ref, t_ref, g_ref, o_ref, *, n2, cb):
    packed = t_ref[...]
    re = lax.bitcast_convert_type(packed & HIGH_HALF, F32)
    im = lax.bitcast_convert_type(lax.shift_left(packed, 16), F32)
    rhs = jnp.stack([re, im], axis=1).reshape(n2 * 2 * SUBLANES, cb).astype(BF16)
    y = jnp.dot(l_ref[0], rhs, preferred_element_type=F32)
    o_ref[...] = y.reshape(n2, SUBLANES, cb) * jax.nn.silu(g_ref[...])


def _dft_stage2(t_nk, l2, g, n1, n2):
    c = t_nk.shape[-1]
    cb = min(1024, c)
    row_blocks = pl.BlockSpec((n2, SUBLANES, cb), lambda g, j: (0, g, j))
    out = pl.pallas_call(
        functools.partial(_dft2_kernel, n2=n2, cb=cb),
        grid=(n1 // SUBLANES, c // cb),
        in_specs=[pl.BlockSpec((1, SUBLANES * n2, SUBLANES * 2 * n2), lambda g, j: (g, 0, 0)),
                  row_blocks, row_blocks],
        out_specs=row_blocks,
        out_shape=jax.ShapeDtypeStruct((n2, n1, c), F32),
        compiler_params=_params(2),
        name="dft_stage2",
    )(l2, t_nk, g.reshape(n2, n1, c))
    return out.reshape(n1 * n2, c)


def _dft_matrices(s):
    n1, n2 = _dft_split(s)
    a = jnp.arange(n1, dtype=jnp.int32)
    c1, s1 = _angles(a[:, None] * a[None, :], n1)
    f1 = jnp.concatenate([jnp.concatenate([c1, -s1], axis=1),
                          jnp.concatenate([-s1, -c1], axis=1)], axis=0).astype(BF16)
    groups = n1 // SUBLANES
    g = jnp.arange(groups, dtype=jnp.int32)[:, None, None, None]
    k2 = jnp.arange(n2, dtype=jnp.int32)[None, :, None, None]
    j = jnp.arange(SUBLANES, dtype=jnp.int32)[None, None, :, None]
    nn = jnp.arange(n2, dtype=jnp.int32)[None, None, None, :]
    c2, s2 = _angles((SUBLANES * g + j + n1 * k2) * nn, s)
    cs = jnp.concatenate([c2, s2], axis=3).astype(BF16)
    return n1, n2, f1, _expand_twiddles(cs.reshape(groups, n2 * SUBLANES, 2 * n2))


def _expand_kernel(cs_ref, e_ref, o_ref):
    spread = jnp.dot(cs_ref[0], e_ref[...], preferred_element_type=F32)
    row = lax.broadcasted_iota(jnp.int32, spread.shape, 0)
    col = lax.broadcasted_iota(jnp.int32, spread.shape, 1)
    o_ref[0] = jnp.where(row % SUBLANES == col % SUBLANES, spread, 0.0).astype(BF16)


def _expand_twiddles(cs):
    groups, rows, cols = cs.shape
    wide = cols * SUBLANES
    src = jnp.arange(cols, dtype=jnp.int32)[:, None]
    slot = (src % (cols // 2)) * 2 + src // (cols // 2)
    e = jnp.arange(wide, dtype=jnp.int32)[None, :] // SUBLANES == slot
    return pl.pallas_call(
        _expand_kernel,
        grid=(groups,),
        in_specs=[pl.BlockSpec((1, rows, cols), lambda g: (g, 0, 0)),
                  pl.BlockSpec((cols, wide), lambda g: (0, 0))],
        out_specs=pl.BlockSpec((1, rows, wide), lambda g: (g, 0, 0)),
        out_shape=jax.ShapeDtypeStruct((groups, rows, wide), BF16),
        compiler_params=_params(1),
        name="expand_twiddles",
    )(cs, e.astype(BF16))


def kernel(x, c, norm_w, ada_w, ada_b, w_in_ab, w_pool, pool_scale, q_norm_w, k_norm_w, lambda_q1, lambda_k1,
           lambda_q2, lambda_k2, subln_w, w_out_ab, w_in_c, w_fourier, w_out_c):
    batch, s, d = x.shape
    assert batch == 1 and d == D_MODEL and s % ATT_TK == 0
    xs = x.reshape(s, d)
    mod = _ada_mod(c, ada_w, ada_b)
    n1, n2, f1, l2 = _dft_matrices(s)
    row = lambda v: v.reshape(1, -1)

    def mod_of(i):
        return mod[i, :, 0:d], mod[i, :, d:2 * d], mod[i, :, 2 * d:3 * d]

    def finish(ys, w_out, j, xs, i):
        gate = mod_of(i)[2]
        if i + 1 == DEPTH:
            return _out_proj(ys, w_out, j, xs, gate), None
        shift, scale, _ = mod_of(i + 1)
        return _out_proj(ys, w_out, j, xs, gate, (row(norm_w[i + 1]), shift, scale))

    w_in_ab, w_out_ab, w_in_c, w_out_c = (w.astype(BF16) for w in (w_in_ab, w_out_ab, w_in_c, w_out_c))
    h = _modulate(xs, row(norm_w[0]), *mod_of(0)[:2])
    for i in range(DEPTH):
        j = i // 2
        if i % 2 == 0:
            o1, o2, o3, o4 = POOL_WIDTH, POOL_WIDTH + DIFF_WIDTH, POOL_WIDTH + 2 * DIFF_WIDTH, POOL_WIDTH + 3 * DIFF_WIDTH
            lambda_init = 0.8 - 0.6 * math.exp(-0.3 * i)
            u_pool = _matmul(h, w_in_ab, F32, PROJ_TM, PROJ_TN, name="pool_in_proj", cols=(0, o1), layer=j)
            gte = _matmul(h, w_in_ab, F32, PROJ_TM, PROJ_TN, name="gate_in_proj", cols=(o4, o4 + AB_WIDTH), layer=j)
            qt, ka, vt = _qkv_proj(h, w_in_ab, j, o1, o2, o3, row(q_norm_w[j]), row(k_norm_w[j]))
            y_a = _pool_mix(u_pool, w_pool[j].astype(BF16), row(pool_scale[j]), gte)
            y_b = _diff_attn(qt, ka, vt, row(lambda_q1[j]), row(lambda_k1[j]), row(lambda_q2[j]),
                             row(lambda_k2[j]), subln_w[j], gte, lambda_init)
            xs, h = finish([y_a, y_b], w_out_ab, j, xs, i)
        else:
            ab = _fold_channel_dft(w_fourier[j], s)
            w_pq = _fold_in_proj(w_in_c, j, ab)
            gte = _matmul(h, w_in_c, F32, PROJ_TM, PROJ_TN, name="gate_in_proj", cols=(d, 2 * d), layer=j)
            h_t = h.reshape(n1, n2, d).transpose(1, 0, 2).reshape(s, d)
            pq_t = _matmul(h_t, w_pq, BF16, PROJ_TM, PROJ_TN, name="fourier_in_proj")
            t_nk = _dft_stage1(pq_t, f1, n1, n2)
            y = _dft_stage2(t_nk, l2, gte, n1, n2)
            xs, h = finish([y], w_out_c, j, xs, i)
    return xs.reshape(batch, s, d)
```

```python
import functools
import math

import jax
import jax.numpy as jnp
from jax import lax
from jax.experimental import pallas as pl
from jax.experimental.pallas import tpu as pltpu

F32 = jnp.float32
BF16 = jnp.bfloat16

D_MODEL = 2048
DEPTH = 4
NORM_EPS = 1e-6

POOL_WINDOWS = (2, 4, 8, 16)
POOL_WIDTH = D_MODEL // 2
POOL_GROUP = POOL_WIDTH // len(POOL_WINDOWS)
POOL_HALO = 8

HEADS = 8
HEAD_DIM = 64
HEAD_V = 2 * HEAD_DIM
DIFF_WIDTH = HEADS * HEAD_V
AB_WIDTH = POOL_WIDTH + DIFF_WIDTH

FOURIER_GROUPS = 4
FOURIER_GROUP = D_MODEL // FOURIER_GROUPS

LANES = 128
SUBLANES = 8
POS_RADIX = 256
COEF_PARTS = 3
LOG2E = math.log2(math.e)

ATT_TQ = 512
ATT_TK = 512
ATT_DIAG = ATT_TQ // ATT_TK
ATT_UNROLL = 8
REF_MARGIN = 60.0
REF_FLOOR = 2.0 ** -60
BOUND_SLACK = 1.001

PROJ_TM, PROJ_TN = 2048, 512
OUT_SUB = 128

VMEM_LIMIT = 56 * 1024 * 1024


def _params(n_axes):
    return pltpu.CompilerParams(dimension_semantics=("arbitrary",) * n_axes,
                                vmem_limit_bytes=VMEM_LIMIT)


def _mod_kernel(c_ref, w_ref, b_ref, o_ref):
    c = c_ref[...]
    o_ref[...] = jnp.sum(jax.nn.silu(c) * w_ref[...], axis=0, keepdims=True) + b_ref[...]


def _ada_mod(c, ada_w, ada_b):
    depth, d, n = ada_w.shape
    tn = 512
    return pl.pallas_call(
        _mod_kernel,
        grid=(depth, n // tn),
        in_specs=[pl.BlockSpec((d, 1), lambda i, j: (0, 0)),
                  pl.BlockSpec((None, d, tn), lambda i, j: (i, 0, j)),
                  pl.BlockSpec((None, 1, tn), lambda i, j: (i, 0, j))],
        out_specs=pl.BlockSpec((None, 1, tn), lambda i, j: (i, 0, j)),
        out_shape=jax.ShapeDtypeStruct((depth, 1, n), F32),
        compiler_params=_params(2),
        name="ada_mod",
    )(c.reshape(d, 1), ada_w, ada_b.reshape(depth, 1, n))


def _modulate_kernel(x_ref, nw_ref, shift_ref, scale_ref, o_ref):
    xf = x_ref[...]
    r = lax.rsqrt(jnp.mean(xf * xf, axis=-1, keepdims=True) + NORM_EPS)
    y = xf * r * nw_ref[...]
    y = y * (1.0 + scale_ref[...]) + shift_ref[...]
    o_ref[...] = y.astype(o_ref.dtype)


def _modulate(x, nw, shift, scale):
    s, d = x.shape
    tm = 512
    vec = pl.BlockSpec((1, d), lambda i: (0, 0))
    return pl.pallas_call(
        _modulate_kernel,
        grid=(s // tm,),
        in_specs=[pl.BlockSpec((tm, d), lambda i: (i, 0)), vec, vec, vec],
        out_specs=pl.BlockSpec((tm, d), lambda i: (i, 0)),
        out_shape=jax.ShapeDtypeStruct((s, d), BF16),
        compiler_params=_params(1),
        name="modulate",
    )(x, nw, shift, scale)


def _mm_kernel(a_ref, w_ref, o_ref, *, precision):
    o_ref[...] = jnp.dot(a_ref[...], w_ref[...], preferred_element_type=F32,
                         precision=precision).astype(o_ref.dtype)


def _matmul(a, w, out_dtype, tm, tn, precision=None, name="matmul", cols=None, layer=None):
    m, k = a.shape
    col0, col1 = cols if cols is not None else (0, w.shape[-1])
    n = col1 - col0
    tm, tn = min(tm, m), min(tn, n)
    assert col0 % tn == 0 and n % tn == 0
    jb = col0 // tn
    if layer is None:
        wspec = pl.BlockSpec((k, tn), lambda i, j: (0, j + jb))
    else:
        wspec = pl.BlockSpec((None, k, tn), lambda i, j: (layer, 0, j + jb))
    return pl.pallas_call(
        functools.partial(_mm_kernel, precision=precision),
        grid=(m // tm, n // tn),
        in_specs=[pl.BlockSpec((tm, k), lambda i, j: (i, 0)), wspec],
        out_specs=pl.BlockSpec((tm, tn), lambda i, j: (i, j)),
        out_shape=jax.ShapeDtypeStruct((m, n), out_dtype),
        compiler_params=_params(2),
        name=name,
    )(a, w)


def _qkv_kernel(h_ref, wq_ref, wk_ref, wv_ref, qnw_ref, knw_ref, qt_ref, ka_ref, vt_ref, *, tm):
    lane = lax.broadcasted_iota(jnp.int32, (ATT_TK, LANES), 1)
    first = lane < HEAD_DIM
    pos = lax.broadcasted_iota(jnp.int32, (ATT_TK, LANES), 0)
    pos_lo = (pos % POS_RADIX).astype(F32)
    pos_hi = (pos // POS_RADIX).astype(F32)
    pos_cols = jnp.where(lane < HEAD_DIM + COEF_PARTS, pos_lo, jnp.where(lane < HEAD_DIM + 2 * COEF_PARTS, pos_hi, 0.0))

    def halves_rms(z, w):
        sq = z * z
        ss1 = jnp.sum(jnp.where(first, sq, 0.0), axis=-1, keepdims=True)
        ss2 = jnp.sum(jnp.where(first, 0.0, sq), axis=-1, keepdims=True)
        r = jnp.where(first, lax.rsqrt(ss1 / HEAD_DIM + NORM_EPS), lax.rsqrt(ss2 / HEAD_DIM + NORM_EPS))
        return z * r * w

    for cc in range(tm // ATT_TK):
        rows = slice(cc * ATT_TK, (cc + 1) * ATT_TK)
        h = h_ref[rows, :]
        zq = jnp.dot(h, wq_ref[...], preferred_element_type=F32)
        zk = jnp.dot(h, wk_ref[...], preferred_element_type=F32)
        zv = jnp.dot(h, wv_ref[...], preferred_element_type=F32)
        for hh in range(2):
            cols = slice(hh * LANES, (hh + 1) * LANES)
            qn = halves_rms(zq[:, cols], qnw_ref[...]) * (HEAD_DIM ** -0.5) * LOG2E
            kn = halves_rms(zk[:, cols], knw_ref[...])
            for m in range(2):
                qm = qn if m == 0 else pltpu.roll(qn, HEAD_DIM, axis=1)
                km = kn if m == 0 else pltpu.roll(kn, HEAD_DIM, axis=1)
                qt_ref[hh, m, :, rows] = jnp.where(first, qm, 0.0).T.astype(BF16)
                ka_ref[hh, m, cc] = jnp.where(first, km, pos_cols).astype(BF16)
            vt_ref[hh, cc] = zv[:, cols].T.astype(BF16)


def _qkv_proj(h, w, layer, col_q, col_k, col_v, qnw, knw):
    s, d = h.shape
    tm = min(2048, s)
    nchunk = s // ATT_TK
    cpt = tm // ATT_TK
    tn = 2 * LANES
    assert col_q % tn == 0 and col_k % tn == 0 and col_v % tn == 0
    wspecs = [pl.BlockSpec((None, d, tn), functools.partial(lambda i, j, jb: (layer, 0, j + jb), jb=c0 // tn))
              for c0 in (col_q, col_k, col_v)]
    vec = pl.BlockSpec((1, LANES), lambda i, j: (0, 0))
    return pl.pallas_call(
        functools.partial(_qkv_kernel, tm=tm),
        grid=(s // tm, HEADS // 2),
        in_specs=[pl.BlockSpec((tm, d), lambda i, j: (i, 0)), *wspecs, vec, vec],
        out_specs=[pl.BlockSpec((2, 2, LANES, tm), lambda i, j: (j, 0, 0, i)),
                   pl.BlockSpec((2, 2, cpt, ATT_TK, LANES), lambda i, j: (j, 0, i, 0, 0)),
                   pl.BlockSpec((2, cpt, HEAD_V, ATT_TK), lambda i, j: (j, i, 0, 0))],
        out_shape=[jax.ShapeDtypeStruct((HEADS, 2, LANES, s), BF16),
                   jax.ShapeDtypeStruct((HEADS, 2, nchunk, ATT_TK, LANES), BF16),
                   jax.ShapeDtypeStruct((HEADS, nchunk, HEAD_V, ATT_TK), BF16)],
        compiler_params=_params(2),
        name="qkv_proj",
    )(h, w, w, w, jnp.tile(qnw, (1, 2)), jnp.tile(knw, (1, 2)))


def _attn_kernel(slopes_ref, qt_ref, ka_ref, vt_ref, lq1_ref, lk1_ref, lq2_ref, lk2_ref, subw_ref, g_ref,
                 o_ref, acc_ref, mu_ref, qv_ref, kmax_ref, p_a, p_b, den_ref, *, lambda_init, nchunk, unroll):
    hd = pl.program_id(0)
    i = pl.program_id(1)
    base = hd * (1 + COEF_PARTS)
    slope = slopes_ref[base]
    rowq = lax.broadcasted_iota(jnp.int32, (LANES, ATT_TQ), 0)
    q_off = lax.broadcasted_iota(jnp.int32, (1, ATT_TQ), 1).astype(F32)

    row1 = lax.broadcasted_iota(jnp.int32, (LANES, 1), 0)
    coef_col = jnp.zeros((LANES, 1), F32)
    for t in range(COEF_PARTS):
        piece = slopes_ref[base + 1 + t]
        coef_col = jnp.where(row1 == HEAD_DIM + t, piece, coef_col)
        coef_col = jnp.where(row1 == HEAD_DIM + COEF_PARTS + t, piece * POS_RADIX, coef_col)
    for m in range(2):
        qf = qt_ref[0, m].astype(F32)
        for kind, sign in ((0, -1.0), (1, 0.0), (2, 1.0)):
            qv_ref[kind, m] = jnp.where(rowq >= HEAD_DIM, sign * coef_col, qf).astype(BF16)

    @pl.when(i == 0)
    def _():
        lane = lax.broadcasted_iota(jnp.int32, (ATT_TK, LANES), 1)
        for m in range(2):
            def widest(c, best):
                kc = ka_ref[0, m, c].astype(F32)
                return jnp.maximum(best, jnp.sum(jnp.where(lane < HEAD_DIM, kc * kc, 0.0), axis=1, keepdims=True))
            kmax_ref[m] = jnp.max(lax.fori_loop(0, nchunk, widest, jnp.zeros((ATT_TK, 1), F32)))

    for m in range(2):
        qf = qt_ref[0, m].astype(F32)
        bound = jnp.sqrt(jnp.sum(qf * qf, axis=0, keepdims=True) * kmax_ref[m]) * BOUND_SLACK
        mu_ref[m] = bound - REF_MARGIN

    c_diag = i * ATT_DIAG

    def chunk_of(e):
        if isinstance(e, int) and e < ATT_DIAG:
            return c_diag + e
        rest = e - ATT_DIAG
        return jnp.where(e < ATT_DIAG, c_diag + e, rest + ATT_DIAG * (rest >= c_diag).astype(jnp.int32))

    def side_of(c):
        return (c < c_diag).astype(jnp.int32) - (c >= c_diag + ATT_DIAG).astype(jnp.int32)

    p_bufs = (p_a, p_b)

    def probs(e, p_ref):
        c = chunk_of(e)
        side = side_of(c)
        sidef = side.astype(F32)
        gap = jnp.abs(c * ATT_TK - i * ATT_TQ).astype(F32)
        cvec = -slope * (sidef * sidef * gap + sidef * q_off)
        for m in range(2):
            st = jnp.dot(ka_ref[0, m, c], qv_ref[1 + side, m], preferred_element_type=F32)
            if isinstance(e, int) and e < ATT_DIAG:
                key_off = lax.broadcasted_iota(jnp.int32, (ATT_TK, ATT_TQ), 0) + e * ATT_TK
                qry_off = lax.broadcasted_iota(jnp.int32, (ATT_TK, ATT_TQ), 1)
                st = st - slope * jnp.abs(key_off - qry_off).astype(F32)
            p = jnp.exp2(st - (mu_ref[m] - cvec))
            p_ref[m] = p.astype(BF16)
            den_ref[m] += jnp.sum(p.reshape(ATT_TK // SUBLANES, SUBLANES, ATT_TQ), axis=0)

    def group(e0, count, probs_last):
        sums = [None, None]
        for u in range(count):
            if u < count - 1 or probs_last:
                probs(e0 + u + 1, p_bufs[(u + 1) % 2])
            c = chunk_of(e0 + u)
            for m in range(2):
                pv = jnp.dot(vt_ref[0, c], p_bufs[u % 2][m], preferred_element_type=F32)
                sums[m] = pv if sums[m] is None else sums[m] + pv
        for m in range(2):
            acc_ref[m] += sums[m]

    def loop_body(it, carry):
        group(it * unroll, unroll, True)
        return carry

    def one_pass(_):
        acc_ref[...] = jnp.zeros(acc_ref.shape, F32)
        den_ref[...] = jnp.zeros(den_ref.shape, F32)
        n_groups = nchunk // unroll
        probs(0, p_a)
        group(0, unroll, True)
        lax.fori_loop(1, n_groups - 1, loop_body, 0)
        group((n_groups - 1) * unroll, unroll, False)
        lowest = None
        for m in range(2):
            den = jnp.sum(den_ref[m], axis=0, keepdims=True)
            mu_ref[m] = jnp.where(den < REF_FLOOR, mu_ref[m] - 2.0 * REF_MARGIN, mu_ref[m])
            lowest = jnp.min(den) if lowest is None else jnp.minimum(lowest, jnp.min(den))
        return (lowest < REF_FLOOR).astype(jnp.int32)

    lax.while_loop(lambda retry: retry > 0, one_pass, jnp.int32(1))

    lam = (jnp.exp(jnp.sum(lq1_ref[...] * lk1_ref[...], axis=-1, keepdims=True))
           - jnp.exp(jnp.sum(lq2_ref[...] * lk2_ref[...], axis=-1, keepdims=True)) + lambda_init)
    den1 = jnp.sum(den_ref[0], axis=0, keepdims=True)
    den2 = jnp.sum(den_ref[1], axis=0, keepdims=True)
    o = acc_ref[0] / den1 - lam * (acc_ref[1] / den2)
    r = lax.rsqrt(jnp.mean(o * o, axis=0, keepdims=True) + NORM_EPS)
    y = (o * r * subw_ref[...]) * (1.0 - lambda_init)
    o_ref[...] = (y.T * jax.nn.silu(g_ref[...])).astype(o_ref.dtype)


def _diff_attn(qt, ka, vt, lq1, lk1, lq2, lk2, subw, g, lambda_init):
    s = qt.shape[-1]
    g_col0 = (g.shape[1] - DIFF_WIDTH) // HEAD_V
    nchunk = s // ATT_TK
    unroll = min(ATT_UNROLL, nchunk // 2)
    assert unroll % 2 == 0 and nchunk % unroll == 0 and unroll >= ATT_DIAG and s % ATT_TQ == 0
    whole = jnp.asarray([LOG2E * 2.0 ** (-8.0 * (h + 1) / HEADS) for h in range(HEADS)], F32)
    pieces, rest = [], whole
    for _ in range(COEF_PARTS):
        pieces.append(rest.astype(BF16).astype(F32))
        rest = rest - pieces[-1]
    slopes = jnp.stack([whole] + pieces, axis=1).reshape(-1)
    vec = pl.BlockSpec((1, HEAD_DIM), lambda h, i: (0, 0))
    return pl.pallas_call(
        functools.partial(_attn_kernel, lambda_init=lambda_init, nchunk=nchunk, unroll=unroll),
        grid=(HEADS, s // ATT_TQ),
        in_specs=[pl.BlockSpec(memory_space=pltpu.SMEM),
                  pl.BlockSpec((1, 2, LANES, ATT_TQ), lambda h, i: (h, 0, 0, i)),
                  pl.BlockSpec((1, 2, nchunk, ATT_TK, LANES), lambda h, i: (h, 0, 0, 0, 0)),
                  pl.BlockSpec((1, nchunk, HEAD_V, ATT_TK), lambda h, i: (h, 0, 0, 0)),
                  vec, vec, vec, vec,
                  pl.BlockSpec((HEAD_V, 1), lambda h, i: (0, 0)),
                  pl.BlockSpec((ATT_TQ, HEAD_V), lambda h, i: (i, g_col0 + h))],
        out_specs=pl.BlockSpec((ATT_TQ, HEAD_V), lambda h, i: (i, h)),
        out_shape=jax.ShapeDtypeStruct((s, DIFF_WIDTH), BF16),
        scratch_shapes=[pltpu.VMEM((2, HEAD_V, ATT_TQ), F32), pltpu.VMEM((2, 1, ATT_TQ), F32),
                        pltpu.VMEM((3, 2, LANES, ATT_TQ), BF16), pltpu.SMEM((2,), F32),
                        pltpu.VMEM((2, ATT_TK, ATT_TQ), BF16), pltpu.VMEM((2, ATT_TK, ATT_TQ), BF16),
                        pltpu.VMEM((2, SUBLANES, ATT_TQ), F32)],
        compiler_params=_params(2),
        name="diff_attn",
    )(slopes, qt, ka, vt, lq1, lk1, lq2, lk2, subw.reshape(HEAD_V, 1), g)


def _pool_kernel(prev_ref, cur_ref, next_ref, w_ref, scale_ref, g_ref, o_ref, ext_ref, *, tm, seq):
    i = pl.program_id(0)
    last = pl.num_programs(0) - 1
    zero_halo = jnp.zeros((POOL_HALO, POOL_WIDTH), F32)
    ext_ref[0:POOL_HALO] = jnp.where(i > 0, prev_ref[...], zero_halo)
    ext_ref[POOL_HALO:POOL_HALO + tm] = cur_ref[...]
    ext_ref[POOL_HALO + tm:POOL_HALO + tm + POOL_HALO] = jnp.where(i < last, next_ref[...], zero_halo)
    t = lax.broadcasted_iota(jnp.int32, (tm, 1), 0) + i * tm
    for g, w in enumerate(POOL_WINDOWS):
        cols = pl.ds(g * POOL_GROUP, POOL_GROUP)
        win = ext_ref[pl.ds(POOL_HALO - w // 2, tm), cols]
        for d in range(-(w // 2) + 1, w - w // 2):
            win = win + ext_ref[pl.ds(POOL_HALO + d, tm), cols]
        lo = jnp.maximum(t - w // 2, 0)
        hi = jnp.minimum(t + (w - w // 2) - 1, seq - 1)
        cnt = (hi - lo + 1).astype(F32)
        pooled = win / cnt - cur_ref[:, cols]
        y = jnp.dot(pooled.astype(BF16), w_ref[g], preferred_element_type=F32)
        o_ref[:, cols] = (y * scale_ref[:, cols] * jax.nn.silu(g_ref[:, cols])).astype(o_ref.dtype)


def _pool_mix(u, w_pool, pool_scale, g):
    s, width = u.shape
    tm = 512
    hb = tm // POOL_HALO
    nhalo = s // POOL_HALO
    return pl.pallas_call(
        functools.partial(_pool_kernel, tm=tm, seq=s),
        grid=(s // tm,),
        in_specs=[pl.BlockSpec((POOL_HALO, width), lambda i: (jnp.maximum(i * hb - 1, 0), 0)),
                  pl.BlockSpec((tm, width), lambda i: (i, 0)),
                  pl.BlockSpec((POOL_HALO, width), lambda i: (jnp.minimum((i + 1) * hb, nhalo - 1), 0)),
                  pl.BlockSpec(w_pool.shape, lambda i: (0, 0, 0)),
                  pl.BlockSpec((1, width), lambda i: (0, 0)),
                  pl.BlockSpec((tm, width), lambda i: (i, 0))],
        out_specs=pl.BlockSpec((tm, width), lambda i: (i, 0)),
        out_shape=jax.ShapeDtypeStruct((s, width), BF16),
        scratch_shapes=[pltpu.VMEM((tm + 2 * POOL_HALO, width), F32)],
        compiler_params=_params(1),
        name="pool_mix",
    )(u, u, u, w_pool, pool_scale, g)


def _out_kernel(*refs, widths, tm, modulate_next, gate_proj):
    n = len(widths)
    y_refs = refs[:n]
    refs = refs[n:]
    if gate_proj:
        hin_ref, wg_ref = refs[:2]
        refs = refs[2:]
    w_ref, x_ref, gate_ref = refs[:3]
    if modulate_next:
        nw_ref, shift_ref, scale_ref, o_ref, h_ref = refs[3:]
        mul = nw_ref[...]
        one_plus_scale = 1.0 + scale_ref[...]
    else:
        (o_ref,) = refs[3:]
    for sb in range(tm // OUT_SUB):
        rows = slice(sb * OUT_SUB, (sb + 1) * OUT_SUB)
        acc = None
        off = 0
        for y_ref, wd in zip(y_refs, widths):
            y = y_ref[rows, :]
            if gate_proj:
                y = y * jax.nn.silu(jnp.dot(hin_ref[rows, :], wg_ref[:, off:off + wd], preferred_element_type=F32))
            part = jnp.dot(y.astype(BF16), w_ref[off:off + wd, :], preferred_element_type=F32)
            acc = part if acc is None else acc + part
            off += wd
        xn = x_ref[rows, :] + gate_ref[...] * acc
        o_ref[rows, :] = xn
        if modulate_next:
            r = lax.rsqrt(jnp.mean(xn * xn, axis=-1, keepdims=True) + NORM_EPS)
            h_ref[rows, :] = ((xn * r * mul) * one_plus_scale + shift_ref[...]).astype(BF16)


def _out_proj(ys, w, layer, x, gate, next_mod=None, gate_from=None):
    s, d = x.shape
    k = w.shape[1]
    gate_proj = gate_from is not None
    tm = min(256 if gate_proj else 512, s)
    widths = tuple(y.shape[1] for y in ys)
    vec = pl.BlockSpec((1, d), lambda i: (0, 0))
    rows = pl.BlockSpec((tm, d), lambda i: (i, 0))
    modulate_next = next_mod is not None
    gate_specs, gate_args = [], ()
    if gate_proj:
        hin, wg, col0 = gate_from
        assert col0 % k == 0 and sum(widths) == k
        gate_specs = [pl.BlockSpec((tm, hin.shape[1]), lambda i: (i, 0)),
                      pl.BlockSpec((None, wg.shape[1], k), lambda i: (layer, 0, col0 // k))]
        gate_args = (hin, wg)
    return pl.pallas_call(
        functools.partial(_out_kernel, widths=widths, tm=tm, modulate_next=modulate_next, gate_proj=gate_proj),
        grid=(s // tm,),
        in_specs=[pl.BlockSpec((tm, wd), lambda i: (i, 0)) for wd in widths] + gate_specs
        + [pl.BlockSpec((None, k, d), lambda i: (layer, 0, 0)), rows, vec] + ([vec, vec, vec] if modulate_next else []),
        out_specs=[rows, rows] if modulate_next else rows,
        out_shape=([jax.ShapeDtypeStruct((s, d), F32), jax.ShapeDtypeStruct((s, d), BF16)] if modulate_next
                   else jax.ShapeDtypeStruct((s, d), F32)),
        compiler_params=_params(1),
        name="out_proj",
    )(*ys, *gate_args, w, x, gate, *(next_mod if modulate_next else ()))


def _dft_split(s):
    n1 = 1 << (int(math.log2(s)) // 2)
    n2 = s // n1
    assert n1 * n2 == s and n1 % SUBLANES == 0 and n2 % SUBLANES == 0
    return n1, n2


def _angles(num, den):
    ang = (2.0 * math.pi / den) * (num % den).astype(F32)
    return jnp.cos(ang), jnp.sin(ang)


def _fold_cs_kernel(cs_ref, wf_ref, o_ref):
    wf = wf_ref[...]
    o_ref[:, 0:FOURIER_GROUP] = jnp.dot(cs_ref[0], wf, preferred_element_type=F32,
                                        precision=lax.Precision.HIGHEST)
    o_ref[:, FOURIER_GROUP:] = jnp.dot(cs_ref[1], wf, preferred_element_type=F32,
                                       precision=lax.Precision.HIGHEST)


def _fold_channel_dft(w_fourier_j, seq):
    fg = FOURIER_GROUP
    idx = jnp.arange(fg, dtype=jnp.int32)
    cc, sc = _angles(idx[:, None] * idx[None, :], fg)
    cs = jnp.stack([cc, sc]) * (1.0 / math.sqrt(seq * fg))
    return pl.pallas_call(
        _fold_cs_kernel,
        grid=(FOURIER_GROUPS,),
        in_specs=[pl.BlockSpec((2, fg, fg), lambda g: (0, 0, 0)),
                  pl.BlockSpec((None, fg, fg), lambda g: (g, 0, 0))],
        out_specs=pl.BlockSpec((None, fg, 2 * fg), lambda g: (g, 0, 0)),
        out_shape=jax.ShapeDtypeStruct((FOURIER_GROUPS, fg, 2 * fg), F32),
        compiler_params=_params(1),
        name="fold_channel_dft",
    )(cs, w_fourier_j)


def _fold_win_kernel(w_ref, ab_ref, o_ref):
    o_ref[...] = jnp.dot(w_ref[...], ab_ref[...].astype(BF16), preferred_element_type=F32).astype(BF16)


def _fold_in_proj(w_in, layer, ab):
    d = w_in.shape[1]
    fg = FOURIER_GROUP
    return pl.pallas_call(
        _fold_win_kernel,
        grid=(FOURIER_GROUPS, 2),
        in_specs=[pl.BlockSpec((None, d, fg), lambda g, t: (layer, 0, g)),
                  pl.BlockSpec((None, fg, fg), lambda g, t: (g, 0, t))],
        out_specs=pl.BlockSpec((d, fg), lambda g, t: (0, g + t * FOURIER_GROUPS)),
        out_shape=jax.ShapeDtypeStruct((d, 2 * D_MODEL), BF16),
        compiler_params=_params(2),
        name="fold_in_proj",
    )(w_in, ab)


HIGH_HALF = -65536


def _bf16_bits(v):
    b = lax.bitcast_convert_type(v, jnp.int32)
    return b + 0x7FFF + (lax.shift_right_logical(b, 16) & 1)


def _dft1_kernel(f_ref, p_ref, q_ref, t_ref, *, nb, n1):
    f = f_ref[...]
    for jj in range(nb):
        rows = slice(jj * n1, (jj + 1) * n1)
        rhs = jnp.concatenate([p_ref[rows, :], q_ref[rows, :]], axis=0)
        t = jnp.dot(f, rhs, preferred_element_type=F32)
        t_ref[jj] = (_bf16_bits(t[0:n1]) & HIGH_HALF) | lax.shift_right_logical(_bf16_bits(t[n1:2 * n1]), 16)


def _dft_stage1(pq_t, f1, n1, n2):
    s, c2 = pq_t.shape
    c = c2 // 2
    nb = SUBLANES
    cb = min(1024, c)
    ncb = c // cb
    return pl.pallas_call(
        functools.partial(_dft1_kernel, nb=nb, n1=n1),
        grid=(n2 // nb, ncb),
        in_specs=[pl.BlockSpec((2 * n1, 2 * n1), lambda b, j: (0, 0)),
                  pl.BlockSpec((nb * n1, cb), lambda b, j: (b, j)),
                  pl.BlockSpec((nb * n1, cb), lambda b, j: (b, j + ncb))],
        out_specs=pl.BlockSpec((nb, n1, cb), lambda b, j: (b, 0, j)),
        out_shape=jax.ShapeDtypeStruct((n2, n1, c), jnp.int32),
        compiler_params=_params(2),
        name="dft_stage1",
    )(f1, pq_t, pq_t)


def _dft2_kernel(l_ref, t_ref, o_ref, *, n2, cb):
    packed = t_ref[...]
    re = lax.bitcast_convert_type(packed & HIGH_HALF, F32)
    im = lax.bitcast_convert_type(lax.shift_left(packed, 16), F32)
    rhs = jnp.stack([re, im], axis=1).reshape(n2 * 2 * SUBLANES, cb).astype(BF16)
    o_ref[...] = jnp.dot(l_ref[0], rhs, preferred_element_type=F32).reshape(n2, SUBLANES, cb)


def _dft_stage2(t_nk, l2, n1, n2):
    c = t_nk.shape[-1]
    cb = min(1024, c)
    row_blocks = pl.BlockSpec((n2, SUBLANES, cb), lambda g, j: (0, g, j))
    out = pl.pallas_call(
        functools.partial(_dft2_kernel, n2=n2, cb=cb),
        grid=(n1 // SUBLANES, c // cb),
        in_specs=[pl.BlockSpec((1, SUBLANES * n2, SUBLANES * 2 * n2), lambda g, j: (g, 0, 0)),
                  row_blocks],
        out_specs=row_blocks,
        out_shape=jax.ShapeDtypeStruct((n2, n1, c), F32),
        compiler_params=_params(2),
        name="dft_stage2",
    )(l2, t_nk)
    return out.reshape(n1 * n2, c)


def _dft_matrices(s):
    n1, n2 = _dft_split(s)
    a = jnp.arange(n1, dtype=jnp.int32)
    c1, s1 = _angles(a[:, None] * a[None, :], n1)
    f1 = jnp.concatenate([jnp.concatenate([c1, -s1], axis=1),
                          jnp.concatenate([-s1, -c1], axis=1)], axis=0).astype(BF16)
    groups = n1 // SUBLANES
    g = jnp.arange(groups, dtype=jnp.int32)[:, None, None, None]
    k2 = jnp.arange(n2, dtype=jnp.int32)[None, :, None, None]
    j = jnp.arange(SUBLANES, dtype=jnp.int32)[None, None, :, None]
    nn = jnp.arange(n2, dtype=jnp.int32)[None, None, None, :]
    c2, s2 = _angles((SUBLANES * g + j + n1 * k2) * nn, s)
    cs = jnp.concatenate([c2, s2], axis=3).astype(BF16)
    return n1, n2, f1, _expand_twiddles(cs.reshape(groups, n2 * SUBLANES, 2 * n2))


def _expand_kernel(cs_ref, e_ref, o_ref):
    spread = jnp.dot(cs_ref[0], e_ref[...], preferred_element_type=F32)
    row = lax.broadcasted_iota(jnp.int32, spread.shape, 0)
    col = lax.broadcasted_iota(jnp.int32, spread.shape, 1)
    o_ref[0] = jnp.where(row % SUBLANES == col % SUBLANES, spread, 0.0).astype(BF16)


def _expand_twiddles(cs):
    groups, rows, cols = cs.shape
    wide = cols * SUBLANES
    src = jnp.arange(cols, dtype=jnp.int32)[:, None]
    slot = (src % (cols // 2)) * 2 + src // (cols // 2)
    e = jnp.arange(wide, dtype=jnp.int32)[None, :] // SUBLANES == slot
    return pl.pallas_call(
        _expand_kernel,
        grid=(groups,),
        in_specs=[pl.BlockSpec((1, rows, cols), lambda g: (g, 0, 0)),
                  pl.BlockSpec((cols, wide), lambda g: (0, 0))],
        out_specs=pl.BlockSpec((1, rows, wide), lambda g: (g, 0, 0)),
        out_shape=jax.ShapeDtypeStruct((groups, rows, wide), BF16),
        compiler_params=_params(1),
        name="expand_twiddles",
    )(cs, e.astype(BF16))


def kernel(x, c, norm_w, ada_w, ada_b, w_in_ab, w_pool, pool_scale, q_norm_w, k_norm_w, lambda_q1, lambda_k1,
           lambda_q2, lambda_k2, subln_w, w_out_ab, w_in_c, w_fourier, w_out_c):
    batch, s, d = x.shape
    assert batch == 1 and d == D_MODEL and s % ATT_TK == 0
    xs = x.reshape(s, d)
    mod = _ada_mod(c, ada_w, ada_b)
    n1, n2, f1, l2 = _dft_matrices(s)
    row = lambda v: v.reshape(1, -1)

    def mod_of(i):
        return mod[i, :, 0:d], mod[i, :, d:2 * d], mod[i, :, 2 * d:3 * d]

    def finish(ys, w_out, j, xs, i, gate_from=None):
        gate = mod_of(i)[2]
        if i + 1 == DEPTH:
            return _out_proj(ys, w_out, j, xs, gate, gate_from=gate_from), None
        shift, scale, _ = mod_of(i + 1)
        return _out_proj(ys, w_out, j, xs, gate, (row(norm_w[i + 1]), shift, scale), gate_from=gate_from)

    w_in_ab, w_out_ab, w_in_c, w_out_c = (w.astype(BF16) for w in (w_in_ab, w_out_ab, w_in_c, w_out_c))
    h = _modulate(xs, row(norm_w[0]), *mod_of(0)[:2])
    for i in range(DEPTH):
        j = i // 2
        if i % 2 == 0:
            o1, o2, o3, o4 = POOL_WIDTH, POOL_WIDTH + DIFF_WIDTH, POOL_WIDTH + 2 * DIFF_WIDTH, POOL_WIDTH + 3 * DIFF_WIDTH
            lambda_init = 0.8 - 0.6 * math.exp(-0.3 * i)
            u_pool = _matmul(h, w_in_ab, F32, PROJ_TM, PROJ_TN, name="pool_in_proj", cols=(0, o1), layer=j)
            gte = _matmul(h, w_in_ab, F32, PROJ_TM, PROJ_TN, name="gate_in_proj", cols=(o4, o4 + AB_WIDTH), layer=j)
            qt, ka, vt = _qkv_proj(h, w_in_ab, j, o1, o2, o3, row(q_norm_w[j]), row(k_norm_w[j]))
            y_a = _pool_mix(u_pool, w_pool[j].astype(BF16), row(pool_scale[j]), gte)
            y_b = _diff_attn(qt, ka, vt, row(lambda_q1[j]), row(lambda_k1[j]), row(lambda_q2[j]),
                             row(lambda_k2[j]), subln_w[j], gte, lambda_init)
            xs, h = finish([y_a, y_b], w_out_ab, j, xs, i)
        else:
            ab = _fold_channel_dft(w_fourier[j], s)
            w_pq = _fold_in_proj(w_in_c, j, ab)
            h_t = h.reshape(n1, n2, d).transpose(1, 0, 2).reshape(s, d)
            pq_t = _matmul(h_t, w_pq, BF16, PROJ_TM, PROJ_TN, name="fourier_in_proj")
            t_nk = _dft_stage1(pq_t, f1, n1, n2)
            f = _dft_stage2(t_nk, l2, n1, n2)
            xs, h = finish([f], w_out_c, j, xs, i, gate_from=(h, w_in_c, d))
    return xs.reshape(batch, s, d)
```

```python
import functools
import math

import jax
import jax.numpy as jnp
from jax import lax
from jax.experimental import pallas as pl
from jax.experimental.pallas import tpu as pltpu

F32 = jnp.float32
BF16 = jnp.bfloat16

D_MODEL = 2048
DEPTH = 4
NORM_EPS = 1e-6

POOL_WINDOWS = (2, 4, 8, 16)
POOL_WIDTH = D_MODEL // 2
POOL_GROUP = POOL_WIDTH // len(POOL_WINDOWS)
POOL_HALO = 8

HEADS = 8
HEAD_DIM = 64
HEAD_V = 2 * HEAD_DIM
DIFF_WIDTH = HEADS * HEAD_V
AB_WIDTH = POOL_WIDTH + DIFF_WIDTH

FOURIER_GROUPS = 4
FOURIER_GROUP = D_MODEL // FOURIER_GROUPS

LANES = 128
SUBLANES = 8
POS_RADIX = 256
COEF_PARTS = 3
LOG2E = math.log2(math.e)

ATT_TQ = 512
ATT_TK = 512
ATT_DIAG = ATT_TQ // ATT_TK
ATT_UNROLL = 8
REF_MARGIN = 60.0
REF_FLOOR = 2.0 ** -60
BOUND_SLACK = 1.001

PROJ_TM, PROJ_TN = 2048, 512
OUT_SUB = 128

VMEM_LIMIT = 56 * 1024 * 1024


def _params(n_axes):
    return pltpu.CompilerParams(dimension_semantics=("arbitrary",) * n_axes,
                                vmem_limit_bytes=VMEM_LIMIT)


def _mod_kernel(c_ref, w_ref, b_ref, o_ref):
    c = c_ref[...]
    o_ref[...] = jnp.sum(jax.nn.silu(c) * w_ref[...], axis=0, keepdims=True) + b_ref[...]


def _ada_mod(c, ada_w, ada_b):
    depth, d, n = ada_w.shape
    tn = 512
    return pl.pallas_call(
        _mod_kernel,
        grid=(depth, n // tn),
        in_specs=[pl.BlockSpec((d, 1), lambda i, j: (0, 0)),
                  pl.BlockSpec((None, d, tn), lambda i, j: (i, 0, j)),
                  pl.BlockSpec((None, 1, tn), lambda i, j: (i, 0, j))],
        out_specs=pl.BlockSpec((None, 1, tn), lambda i, j: (i, 0, j)),
        out_shape=jax.ShapeDtypeStruct((depth, 1, n), F32),
        compiler_params=_params(2),
        name="ada_mod",
    )(c.reshape(d, 1), ada_w, ada_b.reshape(depth, 1, n))


def _modulate_kernel(x_ref, nw_ref, shift_ref, scale_ref, o_ref):
    xf = x_ref[...]
    r = lax.rsqrt(jnp.mean(xf * xf, axis=-1, keepdims=True) + NORM_EPS)
    y = xf * r * nw_ref[...]
    y = y * (1.0 + scale_ref[...]) + shift_ref[...]
    o_ref[...] = y.astype(o_ref.dtype)


def _modulate(x, nw, shift, scale):
    s, d = x.shape
    tm = 512
    vec = pl.BlockSpec((1, d), lambda i: (0, 0))
    return pl.pallas_call(
        _modulate_kernel,
        grid=(s // tm,),
        in_specs=[pl.BlockSpec((tm, d), lambda i: (i, 0)), vec, vec, vec],
        out_specs=pl.BlockSpec((tm, d), lambda i: (i, 0)),
        out_shape=jax.ShapeDtypeStruct((s, d), BF16),
        compiler_params=_params(1),
        name="modulate",
    )(x, nw, shift, scale)


def _mm_kernel(a_ref, w_ref, o_ref, *, precision):
    o_ref[...] = jnp.dot(a_ref[...], w_ref[...], preferred_element_type=F32,
                         precision=precision).astype(o_ref.dtype)


def _matmul(a, w, out_dtype, tm, tn, precision=None, name="matmul", cols=None, layer=None):
    m, k = a.shape
    cols = cols if cols is not None else ((0, w.shape[-1]),)
    n = sum(stop - start for start, stop in cols)
    tm, tn = min(tm, m), min(tn, n)
    assert all(start % tn == 0 and stop % tn == 0 for start, stop in cols)

    def wblock(j):
        jb, first = j, 0
        for start, stop in cols:
            jb = jnp.where(j >= first, j - first + start // tn, jb)
            first += (stop - start) // tn
        return jb

    if layer is None:
        wspec = pl.BlockSpec((k, tn), lambda i, j: (0, wblock(j)))
    else:
        wspec = pl.BlockSpec((None, k, tn), lambda i, j: (layer, 0, wblock(j)))
    return pl.pallas_call(
        functools.partial(_mm_kernel, precision=precision),
        grid=(m // tm, n // tn),
        in_specs=[pl.BlockSpec((tm, k), lambda i, j: (i, 0)), wspec],
        out_specs=pl.BlockSpec((tm, tn), lambda i, j: (i, j)),
        out_shape=jax.ShapeDtypeStruct((m, n), out_dtype),
        compiler_params=_params(2),
        name=name,
    )(a, w)


def _qkv_kernel(h_ref, wq_ref, wk_ref, wv_ref, qnw_ref, knw_ref, qt_ref, ka_ref, vt_ref, *, tm):
    lane = lax.broadcasted_iota(jnp.int32, (ATT_TK, LANES), 1)
    first = lane < HEAD_DIM
    pos = lax.broadcasted_iota(jnp.int32, (ATT_TK, LANES), 0)
    pos_lo = (pos % POS_RADIX).astype(F32)
    pos_hi = (pos // POS_RADIX).astype(F32)
    pos_cols = jnp.where(lane < HEAD_DIM + COEF_PARTS, pos_lo, jnp.where(lane < HEAD_DIM + 2 * COEF_PARTS, pos_hi, 0.0))

    def halves_rms(z, w):
        sq = z * z
        ss1 = jnp.sum(jnp.where(first, sq, 0.0), axis=-1, keepdims=True)
        ss2 = jnp.sum(jnp.where(first, 0.0, sq), axis=-1, keepdims=True)
        r = jnp.where(first, lax.rsqrt(ss1 / HEAD_DIM + NORM_EPS), lax.rsqrt(ss2 / HEAD_DIM + NORM_EPS))
        return z * r * w

    for cc in range(tm // ATT_TK):
        rows = slice(cc * ATT_TK, (cc + 1) * ATT_TK)
        h = h_ref[rows, :]
        zq = jnp.dot(h, wq_ref[...], preferred_element_type=F32)
        zk = jnp.dot(h, wk_ref[...], preferred_element_type=F32)
        zv = jnp.dot(h, wv_ref[...], preferred_element_type=F32)
        for hh in range(2):
            cols = slice(hh * LANES, (hh + 1) * LANES)
            qn = halves_rms(zq[:, cols], qnw_ref[...]) * (HEAD_DIM ** -0.5) * LOG2E
            kn = halves_rms(zk[:, cols], knw_ref[...])
            for m in range(2):
                qm = qn if m == 0 else pltpu.roll(qn, HEAD_DIM, axis=1)
                km = kn if m == 0 else pltpu.roll(kn, HEAD_DIM, axis=1)
                qt_ref[hh, m, :, rows] = jnp.where(first, qm, 0.0).T.astype(BF16)
                ka_ref[hh, m, cc] = jnp.where(first, km, pos_cols).astype(BF16)
            vt_ref[hh, cc] = zv[:, cols].T.astype(BF16)


def _qkv_proj(h, w, layer, col_q, col_k, col_v, qnw, knw):
    s, d = h.shape
    tm = min(2048, s)
    nchunk = s // ATT_TK
    cpt = tm // ATT_TK
    tn = 2 * LANES
    assert col_q % tn == 0 and col_k % tn == 0 and col_v % tn == 0
    wspecs = [pl.BlockSpec((None, d, tn), functools.partial(lambda i, j, jb: (layer, 0, j + jb), jb=c0 // tn))
              for c0 in (col_q, col_k, col_v)]
    vec = pl.BlockSpec((1, LANES), lambda i, j: (0, 0))
    return pl.pallas_call(
        functools.partial(_qkv_kernel, tm=tm),
        grid=(s // tm, HEADS // 2),
        in_specs=[pl.BlockSpec((tm, d), lambda i, j: (i, 0)), *wspecs, vec, vec],
        out_specs=[pl.BlockSpec((2, 2, LANES, tm), lambda i, j: (j, 0, 0, i)),
                   pl.BlockSpec((2, 2, cpt, ATT_TK, LANES), lambda i, j: (j, 0, i, 0, 0)),
                   pl.BlockSpec((2, cpt, HEAD_V, ATT_TK), lambda i, j: (j, i, 0, 0))],
        out_shape=[jax.ShapeDtypeStruct((HEADS, 2, LANES, s), BF16),
                   jax.ShapeDtypeStruct((HEADS, 2, nchunk, ATT_TK, LANES), BF16),
                   jax.ShapeDtypeStruct((HEADS, nchunk, HEAD_V, ATT_TK), BF16)],
        compiler_params=_params(2),
        name="qkv_proj",
    )(h, w, w, w, jnp.tile(qnw, (1, 2)), jnp.tile(knw, (1, 2)))


def _attn_kernel(slopes_ref, qt_ref, ka_ref, vt_ref, lq1_ref, lk1_ref, lq2_ref, lk2_ref, subw_ref, g_ref,
                 o_ref, acc_ref, mu_ref, qv_ref, kmax_ref, p_a, p_b, den_ref, *, lambda_init, nchunk, unroll):
    hd = pl.program_id(0)
    i = pl.program_id(1)
    base = hd * (1 + COEF_PARTS)
    slope = slopes_ref[base]
    q_off = lax.broadcasted_iota(jnp.int32, (1, ATT_TQ), 1).astype(F32)

    row1 = lax.broadcasted_iota(jnp.int32, (LANES - HEAD_DIM, 1), 0)
    coef_col = jnp.zeros((LANES - HEAD_DIM, 1), F32)
    for t in range(COEF_PARTS):
        piece = slopes_ref[base + 1 + t]
        coef_col = jnp.where(row1 == t, piece, coef_col)
        coef_col = jnp.where(row1 == COEF_PARTS + t, piece * POS_RADIX, coef_col)
    coef_rows = jnp.broadcast_to(coef_col, (LANES - HEAD_DIM, ATT_TQ))
    for m in range(2):
        for kind, sign in ((0, -1.0), (1, 0.0), (2, 1.0)):
            qv_ref[kind, m, 0:HEAD_DIM, :] = qt_ref[0, m, 0:HEAD_DIM, :]
            qv_ref[kind, m, HEAD_DIM:LANES, :] = (sign * coef_rows).astype(BF16)

    @pl.when(i == 0)
    def _():
        lane = lax.broadcasted_iota(jnp.int32, (ATT_TK, LANES), 1)
        for m in range(2):
            def widest(c, best):
                kc = ka_ref[0, m, c].astype(F32)
                return jnp.maximum(best, jnp.sum(jnp.where(lane < HEAD_DIM, kc * kc, 0.0), axis=1, keepdims=True))
            kmax_ref[m] = jnp.max(lax.fori_loop(0, nchunk, widest, jnp.zeros((ATT_TK, 1), F32)))

    for m in range(2):
        qf = qt_ref[0, m, 0:HEAD_DIM, :].astype(F32)
        bound = jnp.sqrt(jnp.sum(qf * qf, axis=0, keepdims=True) * kmax_ref[m]) * BOUND_SLACK
        mu_ref[m] = bound - REF_MARGIN

    c_diag = i * ATT_DIAG

    def chunk_of(e):
        if isinstance(e, int) and e < ATT_DIAG:
            return c_diag + e
        rest = e - ATT_DIAG
        return jnp.where(e < ATT_DIAG, c_diag + e, rest + ATT_DIAG * (rest >= c_diag).astype(jnp.int32))

    def side_of(c):
        return (c < c_diag).astype(jnp.int32) - (c >= c_diag + ATT_DIAG).astype(jnp.int32)

    p_bufs = (p_a, p_b)

    def probs(e, p_ref):
        c = chunk_of(e)
        side = side_of(c)
        sidef = side.astype(F32)
        gap = jnp.abs(c * ATT_TK - i * ATT_TQ).astype(F32)
        cvec = -slope * (sidef * sidef * gap + sidef * q_off)
        for m in range(2):
            st = jnp.dot(ka_ref[0, m, c], qv_ref[1 + side, m], preferred_element_type=F32)
            if isinstance(e, int) and e < ATT_DIAG:
                key_off = lax.broadcasted_iota(jnp.int32, (ATT_TK, ATT_TQ), 0) + e * ATT_TK
                qry_off = lax.broadcasted_iota(jnp.int32, (ATT_TK, ATT_TQ), 1)
                st = st - slope * jnp.abs(key_off - qry_off).astype(F32)
            p = jnp.exp2(st - (mu_ref[m] - cvec))
            p_ref[m] = p.astype(BF16)
            den_ref[m] += jnp.sum(p.reshape(ATT_TK // SUBLANES, SUBLANES, ATT_TQ), axis=0)

    def group(e0, count, probs_last):
        sums = [None, None]
        for u in range(count):
            if u < count - 1 or probs_last:
                probs(e0 + u + 1, p_bufs[(u + 1) % 2])
            c = chunk_of(e0 + u)
            for m in range(2):
                pv = jnp.dot(vt_ref[0, c], p_bufs[u % 2][m], preferred_element_type=F32)
                sums[m] = pv if sums[m] is None else sums[m] + pv
        for m in range(2):
            acc_ref[m] += sums[m]

    def loop_body(it, carry):
        group(it * unroll, unroll, True)
        return carry

    def one_pass(_):
        acc_ref[...] = jnp.zeros(acc_ref.shape, F32)
        den_ref[...] = jnp.zeros(den_ref.shape, F32)
        n_groups = nchunk // unroll
        probs(0, p_a)
        group(0, unroll, True)
        lax.fori_loop(1, n_groups - 1, loop_body, 0)
        group((n_groups - 1) * unroll, unroll, False)
        lowest = None
        for m in range(2):
            den = jnp.sum(den_ref[m], axis=0, keepdims=True)
            mu_ref[m] = jnp.where(den < REF_FLOOR, mu_ref[m] - 2.0 * REF_MARGIN, mu_ref[m])
            lowest = jnp.min(den) if lowest is None else jnp.minimum(lowest, jnp.min(den))
        return (lowest < REF_FLOOR).astype(jnp.int32)

    lax.while_loop(lambda retry: retry > 0, one_pass, jnp.int32(1))

    lam = (jnp.exp(jnp.sum(lq1_ref[...] * lk1_ref[...], axis=-1, keepdims=True))
           - jnp.exp(jnp.sum(lq2_ref[...] * lk2_ref[...], axis=-1, keepdims=True)) + lambda_init)
    den1 = jnp.sum(den_ref[0], axis=0, keepdims=True)
    den2 = jnp.sum(den_ref[1], axis=0, keepdims=True)
    o = acc_ref[0] / den1 - lam * (acc_ref[1] / den2)
    r = lax.rsqrt(jnp.mean(o * o, axis=0, keepdims=True) + NORM_EPS)
    y = (o * r * subw_ref[...]) * (1.0 - lambda_init)
    o_ref[...] = (y.T * jax.nn.silu(g_ref[...])).astype(o_ref.dtype)


def _diff_attn(qt, ka, vt, lq1, lk1, lq2, lk2, subw, g, lambda_init):
    s = qt.shape[-1]
    g_col0 = (g.shape[1] - DIFF_WIDTH) // HEAD_V
    nchunk = s // ATT_TK
    unroll = min(ATT_UNROLL, nchunk // 2)
    assert unroll % 2 == 0 and nchunk % unroll == 0 and unroll >= ATT_DIAG and s % ATT_TQ == 0
    whole = jnp.asarray([LOG2E * 2.0 ** (-8.0 * (h + 1) / HEADS) for h in range(HEADS)], F32)
    pieces, rest = [], whole
    for _ in range(COEF_PARTS):
        pieces.append(rest.astype(BF16).astype(F32))
        rest = rest - pieces[-1]
    slopes = jnp.stack([whole] + pieces, axis=1).reshape(-1)
    vec = pl.BlockSpec((1, HEAD_DIM), lambda h, i: (0, 0))
    return pl.pallas_call(
        functools.partial(_attn_kernel, lambda_init=lambda_init, nchunk=nchunk, unroll=unroll),
        grid=(HEADS, s // ATT_TQ),
        in_specs=[pl.BlockSpec(memory_space=pltpu.SMEM),
                  pl.BlockSpec((1, 2, LANES, ATT_TQ), lambda h, i: (h, 0, 0, i)),
                  pl.BlockSpec((1, 2, nchunk, ATT_TK, LANES), lambda h, i: (h, 0, 0, 0, 0)),
                  pl.BlockSpec((1, nchunk, HEAD_V, ATT_TK), lambda h, i: (h, 0, 0, 0)),
                  vec, vec, vec, vec,
                  pl.BlockSpec((HEAD_V, 1), lambda h, i: (0, 0)),
                  pl.BlockSpec((ATT_TQ, HEAD_V), lambda h, i: (i, g_col0 + h))],
        out_specs=pl.BlockSpec((ATT_TQ, HEAD_V), lambda h, i: (i, h)),
        out_shape=jax.ShapeDtypeStruct((s, DIFF_WIDTH), BF16),
        scratch_shapes=[pltpu.VMEM((2, HEAD_V, ATT_TQ), F32), pltpu.VMEM((2, 1, ATT_TQ), F32),
                        pltpu.VMEM((3, 2, LANES, ATT_TQ), BF16), pltpu.SMEM((2,), F32),
                        pltpu.VMEM((2, ATT_TK, ATT_TQ), BF16), pltpu.VMEM((2, ATT_TK, ATT_TQ), BF16),
                        pltpu.VMEM((2, SUBLANES, ATT_TQ), F32)],
        compiler_params=_params(2),
        name="diff_attn",
    )(slopes, qt, ka, vt, lq1, lk1, lq2, lk2, subw.reshape(HEAD_V, 1), g)


def _pool_kernel(prev_ref, cur_ref, next_ref, w_ref, scale_ref, g_ref, o_ref, ext_ref, *, tm, seq):
    i = pl.program_id(0)
    last = pl.num_programs(0) - 1
    zero_halo = jnp.zeros((POOL_HALO, POOL_WIDTH), F32)
    ext_ref[0:POOL_HALO] = jnp.where(i > 0, prev_ref[...], zero_halo)
    ext_ref[POOL_HALO:POOL_HALO + tm] = cur_ref[...]
    ext_ref[POOL_HALO + tm:POOL_HALO + tm + POOL_HALO] = jnp.where(i < last, next_ref[...], zero_halo)
    t = lax.broadcasted_iota(jnp.int32, (tm, 1), 0) + i * tm
    for g, w in enumerate(POOL_WINDOWS):
        cols = pl.ds(g * POOL_GROUP, POOL_GROUP)
        span = tm + 2 * POOL_HALO
        width = 1
        while width < w:
            span -= width
            ext_ref[pl.ds(0, span), cols] = ext_ref[pl.ds(0, span), cols] + ext_ref[pl.ds(width, span), cols]
            width *= 2
        assert width == w
        win = ext_ref[pl.ds(POOL_HALO - w // 2, tm), cols]
        lo = jnp.maximum(t - w // 2, 0)
        hi = jnp.minimum(t + (w - w // 2) - 1, seq - 1)
        cnt = (hi - lo + 1).astype(F32)
        pooled = win * (1.0 / cnt) - cur_ref[:, cols]
        y = jnp.dot(pooled.astype(BF16), w_ref[g], preferred_element_type=F32)
        o_ref[:, cols] = (y * scale_ref[:, cols] * jax.nn.silu(g_ref[:, cols])).astype(o_ref.dtype)


def _pool_mix(ug, w_pool, pool_scale):
    s = ug.shape[0]
    width = POOL_WIDTH
    tm = 512
    hb = tm // POOL_HALO
    nhalo = s // POOL_HALO
    return pl.pallas_call(
        functools.partial(_pool_kernel, tm=tm, seq=s),
        grid=(s // tm,),
        in_specs=[pl.BlockSpec((POOL_HALO, width), lambda i: (jnp.maximum(i * hb - 1, 0), 0)),
                  pl.BlockSpec((tm, width), lambda i: (i, 0)),
                  pl.BlockSpec((POOL_HALO, width), lambda i: (jnp.minimum((i + 1) * hb, nhalo - 1), 0)),
                  pl.BlockSpec(w_pool.shape, lambda i: (0, 0, 0)),
                  pl.BlockSpec((1, width), lambda i: (0, 0)),
                  pl.BlockSpec((tm, width), lambda i: (i, 1))],
        out_specs=pl.BlockSpec((tm, width), lambda i: (i, 0)),
        out_shape=jax.ShapeDtypeStruct((s, width), BF16),
        scratch_shapes=[pltpu.VMEM((tm + 2 * POOL_HALO, width), F32)],
        compiler_params=_params(1),
        name="pool_mix",
    )(ug, ug, ug, w_pool, pool_scale, ug)


def _out_kernel(*refs, widths, tm, modulate_next, gate_proj):
    n = len(widths)
    y_refs = refs[:n]
    refs = refs[n:]
    if gate_proj:
        hin_ref, wg_ref = refs[:2]
        refs = refs[2:]
    w_ref, x_ref, gate_ref = refs[:3]
    if modulate_next:
        nw_ref, shift_ref, scale_ref, o_ref, h_ref = refs[3:]
        mul = nw_ref[...]
        one_plus_scale = 1.0 + scale_ref[...]
    else:
        (o_ref,) = refs[3:]
    for sb in range(tm // OUT_SUB):
        rows = slice(sb * OUT_SUB, (sb + 1) * OUT_SUB)
        acc = None
        off = 0
        for y_ref, wd in zip(y_refs, widths):
            y = y_ref[rows, :]
            if gate_proj:
                y = y * jax.nn.silu(jnp.dot(hin_ref[rows, :], wg_ref[:, off:off + wd], preferred_element_type=F32))
            part = jnp.dot(y.astype(BF16), w_ref[off:off + wd, :], preferred_element_type=F32)
            acc = part if acc is None else acc + part
            off += wd
        xn = x_ref[rows, :] + gate_ref[...] * acc
        o_ref[rows, :] = xn
        if modulate_next:
            r = lax.rsqrt(jnp.mean(xn * xn, axis=-1, keepdims=True) + NORM_EPS)
            h_ref[rows, :] = ((xn * r * mul) * one_plus_scale + shift_ref[...]).astype(BF16)


def _out_proj(ys, w, layer, x, gate, next_mod=None, gate_from=None):
    s, d = x.shape
    k = w.shape[1]
    gate_proj = gate_from is not None
    tm = min(256 if gate_proj else 512, s)
    widths = tuple(y.shape[1] for y in ys)
    vec = pl.BlockSpec((1, d), lambda i: (0, 0))
    rows = pl.BlockSpec((tm, d), lambda i: (i, 0))
    modulate_next = next_mod is not None
    gate_specs, gate_args = [], ()
    if gate_proj:
        hin, wg, col0 = gate_from
        assert col0 % k == 0 and sum(widths) == k
        gate_specs = [pl.BlockSpec((tm, hin.shape[1]), lambda i: (i, 0)),
                      pl.BlockSpec((None, wg.shape[1], k), lambda i: (layer, 0, col0 // k))]
        gate_args = (hin, wg)
    return pl.pallas_call(
        functools.partial(_out_kernel, widths=widths, tm=tm, modulate_next=modulate_next, gate_proj=gate_proj),
        grid=(s // tm,),
        in_specs=[pl.BlockSpec((tm, wd), lambda i: (i, 0)) for wd in widths] + gate_specs
        + [pl.BlockSpec((None, k, d), lambda i: (layer, 0, 0)), rows, vec] + ([vec, vec, vec] if modulate_next else []),
        out_specs=[rows, rows] if modulate_next else rows,
        out_shape=([jax.ShapeDtypeStruct((s, d), F32), jax.ShapeDtypeStruct((s, d), BF16)] if modulate_next
                   else jax.ShapeDtypeStruct((s, d), F32)),
        compiler_params=_params(1),
        name="out_proj",
    )(*ys, *gate_args, w, x, gate, *(next_mod if modulate_next else ()))


def _dft_split(s):
    n1 = 1 << (int(math.log2(s)) // 2)
    n2 = s // n1
    assert n1 * n2 == s and n1 % SUBLANES == 0 and n2 % SUBLANES == 0
    return n1, n2


def _angles(num, den):
    ang = (2.0 * math.pi / den) * (num % den).astype(F32)
    return jnp.cos(ang), jnp.sin(ang)


def _fold_cs_kernel(cs_ref, wf_ref, o_ref):
    wf = wf_ref[...]
    o_ref[:, 0:FOURIER_GROUP] = jnp.dot(cs_ref[0], wf, preferred_element_type=F32,
                                        precision=lax.Precision.HIGHEST)
    o_ref[:, FOURIER_GROUP:] = jnp.dot(cs_ref[1], wf, preferred_element_type=F32,
                                       precision=lax.Precision.HIGHEST)


def _fold_channel_dft(w_fourier_j, seq):
    fg = FOURIER_GROUP
    idx = jnp.arange(fg, dtype=jnp.int32)
    cc, sc = _angles(idx[:, None] * idx[None, :], fg)
    cs = jnp.stack([cc, sc]) * (1.0 / math.sqrt(seq * fg))
    return pl.pallas_call(
        _fold_cs_kernel,
        grid=(FOURIER_GROUPS,),
        in_specs=[pl.BlockSpec((2, fg, fg), lambda g: (0, 0, 0)),
                  pl.BlockSpec((None, fg, fg), lambda g: (g, 0, 0))],
        out_specs=pl.BlockSpec((None, fg, 2 * fg), lambda g: (g, 0, 0)),
        out_shape=jax.ShapeDtypeStruct((FOURIER_GROUPS, fg, 2 * fg), F32),
        compiler_params=_params(1),
        name="fold_channel_dft",
    )(cs, w_fourier_j)


def _fold_win_kernel(w_ref, ab_ref, o_ref):
    o_ref[...] = jnp.dot(w_ref[...], ab_ref[...].astype(BF16), preferred_element_type=F32).astype(BF16)


def _fold_in_proj(w_in, layer, ab):
    d = w_in.shape[1]
    fg = FOURIER_GROUP
    return pl.pallas_call(
        _fold_win_kernel,
        grid=(FOURIER_GROUPS, 2),
        in_specs=[pl.BlockSpec((None, d, fg), lambda g, t: (layer, 0, g)),
                  pl.BlockSpec((None, fg, fg), lambda g, t: (g, 0, t))],
        out_specs=pl.BlockSpec((d, fg), lambda g, t: (0, g + t * FOURIER_GROUPS)),
        out_shape=jax.ShapeDtypeStruct((d, 2 * D_MODEL), BF16),
        compiler_params=_params(2),
        name="fold_in_proj",
    )(w_in, ab)


HIGH_HALF = -65536


def _bf16_bits(v):
    b = lax.bitcast_convert_type(v, jnp.int32)
    return b + 0x7FFF + (lax.shift_right_logical(b, 16) & 1)


def _dft1_kernel(f_ref, p_ref, q_ref, t_ref, *, nb, n1):
    f = f_ref[...]
    for jj in range(nb):
        rows = slice(jj * n1, (jj + 1) * n1)
        rhs = jnp.concatenate([p_ref[rows, :], q_ref[rows, :]], axis=0)
        t = jnp.dot(f, rhs, preferred_element_type=F32)
        t_ref[jj] = (_bf16_bits(t[0:n1]) & HIGH_HALF) | lax.shift_right_logical(_bf16_bits(t[n1:2 * n1]), 16)


def _dft_stage1(pq_t, f1, n1, n2):
    s, c2 = pq_t.shape
    c = c2 // 2
    nb = SUBLANES
    cb = min(1024, c)
    ncb = c // cb
    return pl.pallas_call(
        functools.partial(_dft1_kernel, nb=nb, n1=n1),
        grid=(n2 // nb, ncb),
        in_specs=[pl.BlockSpec((2 * n1, 2 * n1), lambda b, j: (0, 0)),
                  pl.BlockSpec((nb * n1, cb), lambda b, j: (b, j)),
                  pl.BlockSpec((nb * n1, cb), lambda b, j: (b, j + ncb))],
        out_specs=pl.BlockSpec((nb, n1, cb), lambda b, j: (b, 0, j)),
        out_shape=jax.ShapeDtypeStruct((n2, n1, c), jnp.int32),
        compiler_params=_params(2),
        name="dft_stage1",
    )(f1, pq_t, pq_t)


def _dft2_kernel(l_ref, t_ref, o_ref, *, n2, cb):
    packed = t_ref[...]
    re = lax.bitcast_convert_type(packed & HIGH_HALF, F32)
    im = lax.bitcast_convert_type(lax.shift_left(packed, 16), F32)
    rhs = jnp.stack([re, im], axis=1).reshape(n2 * 2 * SUBLANES, cb).astype(BF16)
    o_ref[...] = jnp.dot(l_ref[0], rhs, preferred_element_type=F32).reshape(n2, SUBLANES, cb)


def _dft_stage2(t_nk, l2, n1, n2):
    c = t_nk.shape[-1]
    cb = min(1024, c)
    row_blocks = pl.BlockSpec((n2, SUBLANES, cb), lambda g, j: (0, g, j))
    out = pl.pallas_call(
        functools.partial(_dft2_kernel, n2=n2, cb=cb),
        grid=(n1 // SUBLANES, c // cb),
        in_specs=[pl.BlockSpec((1, SUBLANES * n2, SUBLANES * 2 * n2), lambda g, j: (g, 0, 0)),
                  row_blocks],
        out_specs=row_blocks,
        out_shape=jax.ShapeDtypeStruct((n2, n1, c), F32),
        compiler_params=_params(2),
        name="dft_stage2",
    )(l2, t_nk)
    return out.reshape(n1 * n2, c)


def _dft_matrices(s):
    n1, n2 = _dft_split(s)
    a = jnp.arange(n1, dtype=jnp.int32)
    c1, s1 = _angles(a[:, None] * a[None, :], n1)
    f1 = jnp.concatenate([jnp.concatenate([c1, -s1], axis=1),
                          jnp.concatenate([-s1, -c1], axis=1)], axis=0).astype(BF16)
    groups = n1 // SUBLANES
    g = jnp.arange(groups, dtype=jnp.int32)[:, None, None, None]
    k2 = jnp.arange(n2, dtype=jnp.int32)[None, :, None, None]
    j = jnp.arange(SUBLANES, dtype=jnp.int32)[None, None, :, None]
    nn = jnp.arange(n2, dtype=jnp.int32)[None, None, None, :]
    c2, s2 = _angles((SUBLANES * g + j + n1 * k2) * nn, s)
    cs = jnp.concatenate([c2, s2], axis=3).astype(BF16)
    return n1, n2, f1, _expand_twiddles(cs.reshape(groups, n2 * SUBLANES, 2 * n2))


def _expand_kernel(cs_ref, e_ref, o_ref):
    spread = jnp.dot(cs_ref[0], e_ref[...], preferred_element_type=F32)
    row = lax.broadcasted_iota(jnp.int32, spread.shape, 0)
    col = lax.broadcasted_iota(jnp.int32, spread.shape, 1)
    o_ref[0] = jnp.where(row % SUBLANES == col % SUBLANES, spread, 0.0).astype(BF16)


def _expand_twiddles(cs):
    groups, rows, cols = cs.shape
    wide = cols * SUBLANES
    src = jnp.arange(cols, dtype=jnp.int32)[:, None]
    slot = (src % (cols // 2)) * 2 + src // (cols // 2)
    e = jnp.arange(wide, dtype=jnp.int32)[None, :] // SUBLANES == slot
    return pl.pallas_call(
        _expand_kernel,
        grid=(groups,),
        in_specs=[pl.BlockSpec((1, rows, cols), lambda g: (g, 0, 0)),
                  pl.BlockSpec((cols, wide), lambda g: (0, 0))],
        out_specs=pl.BlockSpec((1, rows, wide), lambda g: (g, 0, 0)),
        out_shape=jax.ShapeDtypeStruct((groups, rows, wide), BF16),
        compiler_params=_params(1),
        name="expand_twiddles",
    )(cs, e.astype(BF16))


def kernel(x, c, norm_w, ada_w, ada_b, w_in_ab, w_pool, pool_scale, q_norm_w, k_norm_w, lambda_q1, lambda_k1,
           lambda_q2, lambda_k2, subln_w, w_out_ab, w_in_c, w_fourier, w_out_c):
    batch, s, d = x.shape
    assert batch == 1 and d == D_MODEL and s % ATT_TK == 0
    xs = x.reshape(s, d)
    mod = _ada_mod(c, ada_w, ada_b)
    n1, n2, f1, l2 = _dft_matrices(s)
    row = lambda v: v.reshape(1, -1)

    def mod_of(i):
        return mod[i, :, 0:d], mod[i, :, d:2 * d], mod[i, :, 2 * d:3 * d]

    def finish(ys, w_out, j, xs, i, gate_from=None):
        gate = mod_of(i)[2]
        if i + 1 == DEPTH:
            return _out_proj(ys, w_out, j, xs, gate, gate_from=gate_from), None
        shift, scale, _ = mod_of(i + 1)
        return _out_proj(ys, w_out, j, xs, gate, (row(norm_w[i + 1]), shift, scale), gate_from=gate_from)

    w_in_ab, w_out_ab, w_in_c, w_out_c = (w.astype(BF16) for w in (w_in_ab, w_out_ab, w_in_c, w_out_c))
    h = _modulate(xs, row(norm_w[0]), *mod_of(0)[:2])
    for i in range(DEPTH):
        j = i // 2
        if i % 2 == 0:
            o1, o2, o3, o4 = POOL_WIDTH, POOL_WIDTH + DIFF_WIDTH, POOL_WIDTH + 2 * DIFF_WIDTH, POOL_WIDTH + 3 * DIFF_WIDTH
            lambda_init = 0.8 - 0.6 * math.exp(-0.3 * i)
            ug = _matmul(h, w_in_ab, F32, PROJ_TM, PROJ_TN, name="pool_gate_in_proj",
                         cols=((0, o1), (o4, o4 + AB_WIDTH)), layer=j)
            qt, ka, vt = _qkv_proj(h, w_in_ab, j, o1, o2, o3, row(q_norm_w[j]), row(k_norm_w[j]))
            y_a = _pool_mix(ug, w_pool[j].astype(BF16), row(pool_scale[j]))
            y_b = _diff_attn(qt, ka, vt, row(lambda_q1[j]), row(lambda_k1[j]), row(lambda_q2[j]),
                             row(lambda_k2[j]), subln_w[j], ug, lambda_init)
            xs, h = finish([y_a, y_b], w_out_ab, j, xs, i)
        else:
            ab = _fold_channel_dft(w_fourier[j], s)
            w_pq = _fold_in_proj(w_in_c, j, ab)
            h_t = h.reshape(n1, n2, d).transpose(1, 0, 2).reshape(s, d)
            pq_t = _matmul(h_t, w_pq, BF16, PROJ_TM, PROJ_TN, name="fourier_in_proj")
            t_nk = _dft_stage1(pq_t, f1, n1, n2)
            f = _dft_stage2(t_nk, l2, n1, n2)
            xs, h = finish([f], w_out_c, j, xs, i, gate_from=(h, w_in_c, d))
    return xs.reshape(batch, s, d)
```

```python
import functools
import math

import jax
import jax.numpy as jnp
import numpy as np
from jax import lax
from jax.experimental import pallas as pl
from jax.experimental.pallas import tpu as pltpu

F32 = jnp.float32
BF16 = jnp.bfloat16

D_MODEL = 2048
DEPTH = 4
NORM_EPS = 1e-6

POOL_WINDOWS = (2, 4, 8, 16)
POOL_WIDTH = D_MODEL // 2
POOL_GROUP = POOL_WIDTH // len(POOL_WINDOWS)
POOL_HALO = 8

HEADS = 8
HEAD_DIM = 64
HEAD_V = 2 * HEAD_DIM
DIFF_WIDTH = HEADS * HEAD_V
AB_WIDTH = POOL_WIDTH + DIFF_WIDTH

FOURIER_GROUPS = 4
FOURIER_GROUP = D_MODEL // FOURIER_GROUPS

LANES = 128
SUBLANES = 8
POS_RADIX = 256
COEF_PARTS = 3
LOG2E = math.log2(math.e)

ATT_TQ = 512
ATT_TK = 512
ATT_DIAG = ATT_TQ // ATT_TK
ATT_UNROLL = 8
REF_MARGIN = 60.0
REF_FLOOR = 2.0 ** -60
BOUND_SLACK = 1.001

PROJ_TM, PROJ_TN = 2048, 512
OUT_SUB = 128

VMEM_LIMIT = 56 * 1024 * 1024


def _params(n_axes):
    return pltpu.CompilerParams(dimension_semantics=("arbitrary",) * n_axes,
                                vmem_limit_bytes=VMEM_LIMIT)


def _mod_kernel(c_ref, w_ref, b_ref, o_ref):
    c = c_ref[...]
    o_ref[...] = jnp.sum(jax.nn.silu(c) * w_ref[...], axis=0, keepdims=True) + b_ref[...]


def _ada_mod(c, ada_w, ada_b):
    depth, d, n = ada_w.shape
    tn = 1024
    return pl.pallas_call(
        _mod_kernel,
        grid=(depth, n // tn),
        in_specs=[pl.BlockSpec((d, 1), lambda i, j: (0, 0)),
                  pl.BlockSpec((None, d, tn), lambda i, j: (i, 0, j)),
                  pl.BlockSpec((None, 1, tn), lambda i, j: (i, 0, j))],
        out_specs=pl.BlockSpec((None, 1, tn), lambda i, j: (i, 0, j)),
        out_shape=jax.ShapeDtypeStruct((depth, 1, n), F32),
        compiler_params=_params(2),
        name="ada_mod",
    )(c.reshape(d, 1), ada_w, ada_b.reshape(depth, 1, n))


def _modulate_kernel(x_ref, nw_ref, shift_ref, scale_ref, o_ref):
    xf = x_ref[...]
    r = lax.rsqrt(jnp.mean(xf * xf, axis=-1, keepdims=True) + NORM_EPS)
    y = xf * r * nw_ref[...]
    y = y * (1.0 + scale_ref[...]) + shift_ref[...]
    o_ref[...] = y.astype(o_ref.dtype)


def _modulate(x, nw, shift, scale):
    s, d = x.shape
    tm = min(1024, s)
    vec = pl.BlockSpec((1, d), lambda i: (0, 0))
    return pl.pallas_call(
        _modulate_kernel,
        grid=(s // tm,),
        in_specs=[pl.BlockSpec((tm, d), lambda i: (i, 0)), vec, vec, vec],
        out_specs=pl.BlockSpec((tm, d), lambda i: (i, 0)),
        out_shape=jax.ShapeDtypeStruct((s, d), BF16),
        compiler_params=_params(1),
        name="modulate",
    )(x, nw, shift, scale)


def _mm_kernel(a_ref, w_ref, o_ref, *, precision):
    o_ref[...] = jnp.dot(a_ref[...], w_ref[...], preferred_element_type=F32,
                         precision=precision).astype(o_ref.dtype)


def _matmul(a, w, out_dtype, tm, tn, precision=None, name="matmul", cols=None, layer=None):
    m, k = a.shape
    cols = cols if cols is not None else ((0, w.shape[-1]),)
    n = sum(stop - start for start, stop in cols)
    tm, tn = min(tm, m), min(tn, n)
    assert all(start % tn == 0 and stop % tn == 0 for start, stop in cols)

    def wblock(j):
        jb, first = j, 0
        for start, stop in cols:
            jb = jnp.where(j >= first, j - first + start // tn, jb)
            first += (stop - start) // tn
        return jb

    if layer is None:
        wspec = pl.BlockSpec((k, tn), lambda i, j: (0, wblock(j)))
    else:
        wspec = pl.BlockSpec((None, k, tn), lambda i, j: (layer, 0, wblock(j)))
    return pl.pallas_call(
        functools.partial(_mm_kernel, precision=precision),
        grid=(m // tm, n // tn),
        in_specs=[pl.BlockSpec((tm, k), lambda i, j: (i, 0)), wspec],
        out_specs=pl.BlockSpec((tm, tn), lambda i, j: (i, j)),
        out_shape=jax.ShapeDtypeStruct((m, n), out_dtype),
        compiler_params=_params(2),
        name=name,
    )(a, w)


def _qkv_kernel(h_ref, wq_ref, wk_ref, wv_ref, qnw_ref, knw_ref, qt_ref, ka_ref, vt_ref, *, tm):
    lane = lax.broadcasted_iota(jnp.int32, (ATT_TK, LANES), 1)
    first = lane < HEAD_DIM
    pos = lax.broadcasted_iota(jnp.int32, (ATT_TK, LANES), 0)
    pos_lo = (pos % POS_RADIX).astype(F32)
    pos_hi = (pos // POS_RADIX).astype(F32)
    pos_cols = jnp.where(lane < HEAD_DIM + COEF_PARTS, pos_lo, jnp.where(lane < HEAD_DIM + 2 * COEF_PARTS, pos_hi, 0.0))

    def halves_rms(z, w):
        sq = z * z
        ss1 = jnp.sum(jnp.where(first, sq, 0.0), axis=-1, keepdims=True)
        ss2 = jnp.sum(jnp.where(first, 0.0, sq), axis=-1, keepdims=True)
        r = jnp.where(first, lax.rsqrt(ss1 / HEAD_DIM + NORM_EPS), lax.rsqrt(ss2 / HEAD_DIM + NORM_EPS))
        return z * r * w

    for cc in range(tm // ATT_TK):
        rows = slice(cc * ATT_TK, (cc + 1) * ATT_TK)
        h = h_ref[rows, :]
        zq = jnp.dot(h, wq_ref[...], preferred_element_type=F32)
        zk = jnp.dot(h, wk_ref[...], preferred_element_type=F32)
        zv = jnp.dot(h, wv_ref[...], preferred_element_type=F32)
        for hh in range(2):
            cols = slice(hh * LANES, (hh + 1) * LANES)
            qn = halves_rms(zq[:, cols], qnw_ref[...]) * (HEAD_DIM ** -0.5) * LOG2E
            kn = halves_rms(zk[:, cols], knw_ref[...])
            for m in range(2):
                qm = qn if m == 0 else pltpu.roll(qn, HEAD_DIM, axis=1)
                km = kn if m == 0 else pltpu.roll(kn, HEAD_DIM, axis=1)
                qt_ref[hh, m, :, rows] = jnp.where(first, qm, 0.0).T.astype(BF16)
                ka_ref[hh, m, cc] = jnp.where(first, km, pos_cols).astype(BF16)
            vt_ref[hh, cc] = zv[:, cols].T.astype(BF16)


def _qkv_proj(h, w, layer, col_q, col_k, col_v, qnw, knw):
    s, d = h.shape
    tm = min(2048, s)
    nchunk = s // ATT_TK
    cpt = tm // ATT_TK
    tn = 2 * LANES
    assert col_q % tn == 0 and col_k % tn == 0 and col_v % tn == 0
    wspecs = [pl.BlockSpec((None, d, tn), functools.partial(lambda i, j, jb: (layer, 0, j + jb), jb=c0 // tn))
              for c0 in (col_q, col_k, col_v)]
    vec = pl.BlockSpec((1, LANES), lambda i, j: (0, 0))
    return pl.pallas_call(
        functools.partial(_qkv_kernel, tm=tm),
        grid=(s // tm, HEADS // 2),
        in_specs=[pl.BlockSpec((tm, d), lambda i, j: (i, 0)), *wspecs, vec, vec],
        out_specs=[pl.BlockSpec((2, 2, LANES, tm), lambda i, j: (j, 0, 0, i)),
                   pl.BlockSpec((2, 2, cpt, ATT_TK, LANES), lambda i, j: (j, 0, i, 0, 0)),
                   pl.BlockSpec((2, cpt, HEAD_V, ATT_TK), lambda i, j: (j, i, 0, 0))],
        out_shape=[jax.ShapeDtypeStruct((HEADS, 2, LANES, s), BF16),
                   jax.ShapeDtypeStruct((HEADS, 2, nchunk, ATT_TK, LANES), BF16),
                   jax.ShapeDtypeStruct((HEADS, nchunk, HEAD_V, ATT_TK), BF16)],
        compiler_params=_params(2),
        name="qkv_proj",
    )(h, w, w, w, jnp.tile(qnw, (1, 2)), jnp.tile(knw, (1, 2)))


def _attn_kernel(slopes_ref, qt_ref, ka_ref, vt_ref, lq1_ref, lk1_ref, lq2_ref, lk2_ref, subw_ref, g_ref,
                 o_ref, acc_ref, mu_ref, qv_ref, kmax_ref, p_a, p_b, den_ref, *, lambda_init, nchunk, unroll):
    hd = pl.program_id(0)
    i = pl.program_id(1)
    base = hd * (1 + COEF_PARTS)
    slope = slopes_ref[base]
    q_off = lax.broadcasted_iota(jnp.int32, (1, ATT_TQ), 1).astype(F32)

    row1 = lax.broadcasted_iota(jnp.int32, (LANES - HEAD_DIM, 1), 0)
    coef_col = jnp.zeros((LANES - HEAD_DIM, 1), F32)
    for t in range(COEF_PARTS):
        piece = slopes_ref[base + 1 + t]
        coef_col = jnp.where(row1 == t, piece, coef_col)
        coef_col = jnp.where(row1 == COEF_PARTS + t, piece * POS_RADIX, coef_col)
    coef_rows = jnp.broadcast_to(coef_col, (LANES - HEAD_DIM, ATT_TQ))
    for m in range(2):
        for kind, sign in ((0, -1.0), (1, 0.0), (2, 1.0)):
            qv_ref[kind, m, 0:HEAD_DIM, :] = qt_ref[0, m, 0:HEAD_DIM, :]
            qv_ref[kind, m, HEAD_DIM:LANES, :] = (sign * coef_rows).astype(BF16)

    @pl.when(i == 0)
    def _():
        lane = lax.broadcasted_iota(jnp.int32, (ATT_TK, LANES), 1)
        for m in range(2):
            def widest(c, best):
                kc = ka_ref[0, m, c].astype(F32)
                return jnp.maximum(best, jnp.sum(jnp.where(lane < HEAD_DIM, kc * kc, 0.0), axis=1, keepdims=True))
            kmax_ref[m] = jnp.max(lax.fori_loop(0, nchunk, widest, jnp.zeros((ATT_TK, 1), F32)))

    for m in range(2):
        qf = qt_ref[0, m, 0:HEAD_DIM, :].astype(F32)
        bound = jnp.sqrt(jnp.sum(qf * qf, axis=0, keepdims=True) * kmax_ref[m]) * BOUND_SLACK
        mu_ref[m] = bound - REF_MARGIN

    c_diag = i * ATT_DIAG

    def chunk_of(e):
        if isinstance(e, int) and e < ATT_DIAG:
            return c_diag + e
        rest = e - ATT_DIAG
        return jnp.where(e < ATT_DIAG, c_diag + e, rest + ATT_DIAG * (rest >= c_diag).astype(jnp.int32))

    def side_of(c):
        return (c < c_diag).astype(jnp.int32) - (c >= c_diag + ATT_DIAG).astype(jnp.int32)

    p_bufs = (p_a, p_b)

    def probs(e, p_ref):
        c = chunk_of(e)
        side = side_of(c)
        sidef = side.astype(F32)
        gap = jnp.abs(c * ATT_TK - i * ATT_TQ).astype(F32)
        cvec = -slope * (sidef * sidef * gap + sidef * q_off)
        for m in range(2):
            st = jnp.dot(ka_ref[0, m, c], qv_ref[1 + side, m], preferred_element_type=F32)
            if isinstance(e, int) and e < ATT_DIAG:
                key_off = lax.broadcasted_iota(jnp.int32, (ATT_TK, ATT_TQ), 0) + e * ATT_TK
                qry_off = lax.broadcasted_iota(jnp.int32, (ATT_TK, ATT_TQ), 1)
                st = st - slope * jnp.abs(key_off - qry_off).astype(F32)
            p = jnp.exp2(st - (mu_ref[m] - cvec))
            p_ref[m] = p.astype(BF16)
            den_ref[m] += jnp.sum(p.reshape(ATT_TK // SUBLANES, SUBLANES, ATT_TQ), axis=0)

    def group(e0, count, probs_last):
        sums = [None, None]
        for u in range(count):
            if u < count - 1 or probs_last:
                probs(e0 + u + 1, p_bufs[(u + 1) % 2])
            c = chunk_of(e0 + u)
            for m in range(2):
                pv = jnp.dot(vt_ref[0, c], p_bufs[u % 2][m], preferred_element_type=F32)
                sums[m] = pv if sums[m] is None else sums[m] + pv
        for m in range(2):
            acc_ref[m] += sums[m]

    def loop_body(it, carry):
        group(it * unroll, unroll, True)
        return carry

    def one_pass(_):
        acc_ref[...] = jnp.zeros(acc_ref.shape, F32)
        den_ref[...] = jnp.zeros(den_ref.shape, F32)
        n_groups = nchunk // unroll
        probs(0, p_a)
        group(0, unroll, True)
        lax.fori_loop(1, n_groups - 1, loop_body, 0)
        group((n_groups - 1) * unroll, unroll, False)
        lowest = None
        for m in range(2):
            den = jnp.sum(den_ref[m], axis=0, keepdims=True)
            mu_ref[m] = jnp.where(den < REF_FLOOR, mu_ref[m] - 2.0 * REF_MARGIN, mu_ref[m])
            lowest = jnp.min(den) if lowest is None else jnp.minimum(lowest, jnp.min(den))
        return (lowest < REF_FLOOR).astype(jnp.int32)

    lax.while_loop(lambda retry: retry > 0, one_pass, jnp.int32(1))

    lam = (jnp.exp(jnp.sum(lq1_ref[...] * lk1_ref[...], axis=-1, keepdims=True))
           - jnp.exp(jnp.sum(lq2_ref[...] * lk2_ref[...], axis=-1, keepdims=True)) + lambda_init)
    den1 = jnp.sum(den_ref[0], axis=0, keepdims=True)
    den2 = jnp.sum(den_ref[1], axis=0, keepdims=True)
    o = acc_ref[0] / den1 - lam * (acc_ref[1] / den2)
    r = lax.rsqrt(jnp.mean(o * o, axis=0, keepdims=True) + NORM_EPS)
    y = (o * r * subw_ref[...]) * (1.0 - lambda_init)
    o_ref[...] = (y.T * jax.nn.silu(g_ref[...])).astype(o_ref.dtype)


def _diff_attn(qt, ka, vt, lq1, lk1, lq2, lk2, subw, g, lambda_init):
    s = qt.shape[-1]
    g_col0 = (g.shape[1] - DIFF_WIDTH) // HEAD_V
    nchunk = s // ATT_TK
    unroll = min(ATT_UNROLL, nchunk // 2)
    assert unroll % 2 == 0 and nchunk % unroll == 0 and unroll >= ATT_DIAG and s % ATT_TQ == 0
    whole = jnp.asarray([LOG2E * 2.0 ** (-8.0 * (h + 1) / HEADS) for h in range(HEADS)], F32)
    pieces, rest = [], whole
    for _ in range(COEF_PARTS):
        pieces.append(rest.astype(BF16).astype(F32))
        rest = rest - pieces[-1]
    slopes = jnp.stack([whole] + pieces, axis=1).reshape(-1)
    vec = pl.BlockSpec((1, HEAD_DIM), lambda h, i: (0, 0))
    return pl.pallas_call(
        functools.partial(_attn_kernel, lambda_init=lambda_init, nchunk=nchunk, unroll=unroll),
        grid=(HEADS, s // ATT_TQ),
        in_specs=[pl.BlockSpec(memory_space=pltpu.SMEM),
                  pl.BlockSpec((1, 2, LANES, ATT_TQ), lambda h, i: (h, 0, 0, i)),
                  pl.BlockSpec((1, 2, nchunk, ATT_TK, LANES), lambda h, i: (h, 0, 0, 0, 0)),
                  pl.BlockSpec((1, nchunk, HEAD_V, ATT_TK), lambda h, i: (h, 0, 0, 0)),
                  vec, vec, vec, vec,
                  pl.BlockSpec((HEAD_V, 1), lambda h, i: (0, 0)),
                  pl.BlockSpec((ATT_TQ, HEAD_V), lambda h, i: (i, g_col0 + h))],
        out_specs=pl.BlockSpec((ATT_TQ, HEAD_V), lambda h, i: (i, h)),
        out_shape=jax.ShapeDtypeStruct((s, DIFF_WIDTH), BF16),
        scratch_shapes=[pltpu.VMEM((2, HEAD_V, ATT_TQ), F32), pltpu.VMEM((2, 1, ATT_TQ), F32),
                        pltpu.VMEM((3, 2, LANES, ATT_TQ), BF16), pltpu.SMEM((2,), F32),
                        pltpu.VMEM((2, ATT_TK, ATT_TQ), BF16), pltpu.VMEM((2, ATT_TK, ATT_TQ), BF16),
                        pltpu.VMEM((2, SUBLANES, ATT_TQ), F32)],
        compiler_params=_params(2),
        name="diff_attn",
    )(slopes, qt, ka, vt, lq1, lk1, lq2, lk2, subw.reshape(HEAD_V, 1), g)


def _pool_kernel(prev_ref, cur_ref, next_ref, w_ref, scale_ref, g_ref, o_ref, ext_ref, *, tm, seq):
    i = pl.program_id(0)
    last = pl.num_programs(0) - 1
    zero_halo = jnp.zeros((POOL_HALO, POOL_WIDTH), F32)
    ext_ref[0:POOL_HALO] = jnp.where(i > 0, prev_ref[...], zero_halo)
    ext_ref[POOL_HALO:POOL_HALO + tm] = cur_ref[...]
    ext_ref[POOL_HALO + tm:POOL_HALO + tm + POOL_HALO] = jnp.where(i < last, next_ref[...], zero_halo)
    t = lax.broadcasted_iota(jnp.int32, (tm, 1), 0) + i * tm
    for g, w in enumerate(POOL_WINDOWS):
        cols = pl.ds(g * POOL_GROUP, POOL_GROUP)
        span = tm + 2 * POOL_HALO
        width = 1
        while width < w:
            span -= width
            ext_ref[pl.ds(0, span), cols] = ext_ref[pl.ds(0, span), cols] + ext_ref[pl.ds(width, span), cols]
            width *= 2
        assert width == w
        win = ext_ref[pl.ds(POOL_HALO - w // 2, tm), cols]
        lo = jnp.maximum(t - w // 2, 0)
        hi = jnp.minimum(t + (w - w // 2) - 1, seq - 1)
        cnt = (hi - lo + 1).astype(F32)
        pooled = win * (1.0 / cnt) - cur_ref[:, cols]
        y = jnp.dot(pooled.astype(BF16), w_ref[g], preferred_element_type=F32)
        o_ref[:, cols] = (y * scale_ref[:, cols] * jax.nn.silu(g_ref[:, cols])).astype(o_ref.dtype)


def _pool_mix(ug, w_pool, pool_scale):
    s = ug.shape[0]
    width = POOL_WIDTH
    tm = 512
    hb = tm // POOL_HALO
    nhalo = s // POOL_HALO
    return pl.pallas_call(
        functools.partial(_pool_kernel, tm=tm, seq=s),
        grid=(s // tm,),
        in_specs=[pl.BlockSpec((POOL_HALO, width), lambda i: (jnp.maximum(i * hb - 1, 0), 0)),
                  pl.BlockSpec((tm, width), lambda i: (i, 0)),
                  pl.BlockSpec((POOL_HALO, width), lambda i: (jnp.minimum((i + 1) * hb, nhalo - 1), 0)),
                  pl.BlockSpec(w_pool.shape, lambda i: (0, 0, 0)),
                  pl.BlockSpec((1, width), lambda i: (0, 0)),
                  pl.BlockSpec((tm, width), lambda i: (i, 1))],
        out_specs=pl.BlockSpec((tm, width), lambda i: (i, 0)),
        out_shape=jax.ShapeDtypeStruct((s, width), BF16),
        scratch_shapes=[pltpu.VMEM((tm + 2 * POOL_HALO, width), F32)],
        compiler_params=_params(1),
        name="pool_mix",
    )(ug, ug, ug, w_pool, pool_scale, ug)


def _out_kernel(*refs, widths, tm, modulate_next, gate_proj):
    n = len(widths)
    y_refs = refs[:n]
    refs = refs[n:]
    if gate_proj:
        hin_ref, wg_ref = refs[:2]
        refs = refs[2:]
    w_ref, x_ref, gate_ref = refs[:3]
    if modulate_next:
        nw_ref, shift_ref, scale_ref, o_ref, h_ref = refs[3:]
        mul = nw_ref[...]
        one_plus_scale = 1.0 + scale_ref[...]
    else:
        (o_ref,) = refs[3:]
    for sb in range(tm // OUT_SUB):
        rows = slice(sb * OUT_SUB, (sb + 1) * OUT_SUB)
        acc = None
        off = 0
        for y_ref, wd in zip(y_refs, widths):
            y = y_ref[rows, :]
            if gate_proj:
                y = y * jax.nn.silu(jnp.dot(hin_ref[rows, :], wg_ref[:, off:off + wd], preferred_element_type=F32))
            part = jnp.dot(y.astype(BF16), w_ref[off:off + wd, :], preferred_element_type=F32)
            acc = part if acc is None else acc + part
            off += wd
        xn = x_ref[rows, :] + gate_ref[...] * acc
        o_ref[rows, :] = xn
        if modulate_next:
            r = lax.rsqrt(jnp.mean(xn * xn, axis=-1, keepdims=True) + NORM_EPS)
            h_ref[rows, :] = ((xn * r * mul) * one_plus_scale + shift_ref[...]).astype(BF16)


def _out_proj(ys, w, layer, x, gate, next_mod=None, gate_from=None):
    s, d = x.shape
    k = w.shape[1]
    gate_proj = gate_from is not None
    tm = min(256 if gate_proj else 512, s)
    widths = tuple(y.shape[1] for y in ys)
    vec = pl.BlockSpec((1, d), lambda i: (0, 0))
    rows = pl.BlockSpec((tm, d), lambda i: (i, 0))
    modulate_next = next_mod is not None
    gate_specs, gate_args = [], ()
    if gate_proj:
        hin, wg, col0 = gate_from
        assert col0 % k == 0 and sum(widths) == k
        gate_specs = [pl.BlockSpec((tm, hin.shape[1]), lambda i: (i, 0)),
                      pl.BlockSpec((None, wg.shape[1], k), lambda i: (layer, 0, col0 // k))]
        gate_args = (hin, wg)
    return pl.pallas_call(
        functools.partial(_out_kernel, widths=widths, tm=tm, modulate_next=modulate_next, gate_proj=gate_proj),
        grid=(s // tm,),
        in_specs=[pl.BlockSpec((tm, wd), lambda i: (i, 0)) for wd in widths] + gate_specs
        + [pl.BlockSpec((None, k, d), lambda i: (layer, 0, 0)), rows, vec] + ([vec, vec, vec] if modulate_next else []),
        out_specs=[rows, rows] if modulate_next else rows,
        out_shape=([jax.ShapeDtypeStruct((s, d), F32), jax.ShapeDtypeStruct((s, d), BF16)] if modulate_next
                   else jax.ShapeDtypeStruct((s, d), F32)),
        compiler_params=_params(1),
        name="out_proj",
    )(*ys, *gate_args, w, x, gate, *(next_mod if modulate_next else ()))


def _dft_split(s):
    n1 = 1 << (int(math.log2(s)) // 2)
    n2 = s // n1
    assert n1 * n2 == s and n1 % SUBLANES == 0 and n2 % SUBLANES == 0
    return n1, n2


def _angles(num, den):
    ang = (2.0 * math.pi / den) * (num % den)
    return np.cos(ang), np.sin(ang)


def _fold_cs_kernel(cs_ref, wf_ref, o_ref):
    wf = wf_ref[...]
    o_ref[:, 0:FOURIER_GROUP] = jnp.dot(cs_ref[0], wf, preferred_element_type=F32,
                                        precision=lax.Precision.HIGHEST)
    o_ref[:, FOURIER_GROUP:] = jnp.dot(cs_ref[1], wf, preferred_element_type=F32,
                                       precision=lax.Precision.HIGHEST)


def _fold_channel_dft(w_fourier_j, seq):
    fg = FOURIER_GROUP
    idx = np.arange(fg)
    cc, sc = _angles(idx[:, None] * idx[None, :], fg)
    cs = jnp.asarray(np.stack([cc, sc]) / math.sqrt(seq * fg), F32)
    return pl.pallas_call(
        _fold_cs_kernel,
        grid=(FOURIER_GROUPS,),
        in_specs=[pl.BlockSpec((2, fg, fg), lambda g: (0, 0, 0)),
                  pl.BlockSpec((None, fg, fg), lambda g: (g, 0, 0))],
        out_specs=pl.BlockSpec((None, fg, 2 * fg), lambda g: (g, 0, 0)),
        out_shape=jax.ShapeDtypeStruct((FOURIER_GROUPS, fg, 2 * fg), F32),
        compiler_params=_params(1),
        name="fold_channel_dft",
    )(cs, w_fourier_j)


def _fold_win_kernel(w_ref, ab_ref, o_ref):
    o_ref[...] = jnp.dot(w_ref[...], ab_ref[...].astype(BF16), preferred_element_type=F32).astype(BF16)


def _fold_in_proj(w_in, layer, ab):
    d = w_in.shape[1]
    fg = FOURIER_GROUP
    return pl.pallas_call(
        _fold_win_kernel,
        grid=(FOURIER_GROUPS, 2),
        in_specs=[pl.BlockSpec((None, d, fg), lambda g, t: (layer, 0, g)),
                  pl.BlockSpec((None, fg, fg), lambda g, t: (g, 0, t))],
        out_specs=pl.BlockSpec((d, fg), lambda g, t: (0, g + t * FOURIER_GROUPS)),
        out_shape=jax.ShapeDtypeStruct((d, 2 * D_MODEL), BF16),
        compiler_params=_params(2),
        name="fold_in_proj",
    )(w_in, ab)


HIGH_HALF = -65536


def _bf16_bits(v):
    b = lax.bitcast_convert_type(v, jnp.int32)
    return b + 0x7FFF + (lax.shift_right_logical(b, 16) & 1)


def _dft1_kernel(f_ref, p_ref, q_ref, t_ref, *, nb, n1):
    f = f_ref[...]
    for jj in range(nb):
        rows = slice(jj * n1, (jj + 1) * n1)
        rhs = jnp.concatenate([p_ref[rows, :], q_ref[rows, :]], axis=0)
        t = jnp.dot(f, rhs, preferred_element_type=F32)
        t_ref[jj] = (_bf16_bits(t[0:n1]) & HIGH_HALF) | lax.shift_right_logical(_bf16_bits(t[n1:2 * n1]), 16)


def _dft_stage1(pq_t, f1, n1, n2):
    s, c2 = pq_t.shape
    c = c2 // 2
    nb = SUBLANES
    cb = min(1024, c)
    ncb = c // cb
    return pl.pallas_call(
        functools.partial(_dft1_kernel, nb=nb, n1=n1),
        grid=(n2 // nb, ncb),
        in_specs=[pl.BlockSpec((2 * n1, 2 * n1), lambda b, j: (0, 0)),
                  pl.BlockSpec((nb * n1, cb), lambda b, j: (b, j)),
                  pl.BlockSpec((nb * n1, cb), lambda b, j: (b, j + ncb))],
        out_specs=pl.BlockSpec((nb, n1, cb), lambda b, j: (b, 0, j)),
        out_shape=jax.ShapeDtypeStruct((n2, n1, c), jnp.int32),
        compiler_params=_params(2),
        name="dft_stage1",
    )(f1, pq_t, pq_t)


def _dft2_kernel(l_ref, t_ref, o_ref, *, n2, cb):
    packed = t_ref[...]
    re = lax.bitcast_convert_type(packed & HIGH_HALF, F32)
    im = lax.bitcast_convert_type(lax.shift_left(packed, 16), F32)
    rhs = jnp.stack([re, im], axis=1).reshape(n2 * 2 * SUBLANES, cb).astype(BF16)
    o_ref[...] = jnp.dot(l_ref[0], rhs, preferred_element_type=F32).reshape(n2, SUBLANES, cb)


def _dft_stage2(t_nk, l2, n1, n2):
    c = t_nk.shape[-1]
    cb = min(1024, c)
    row_blocks = pl.BlockSpec((n2, SUBLANES, cb), lambda g, j: (0, g, j))
    out = pl.pallas_call(
        functools.partial(_dft2_kernel, n2=n2, cb=cb),
        grid=(n1 // SUBLANES, c // cb),
        in_specs=[pl.BlockSpec((1, SUBLANES * n2, SUBLANES * 2 * n2), lambda g, j: (g, 0, 0)),
                  row_blocks],
        out_specs=row_blocks,
        out_shape=jax.ShapeDtypeStruct((n2, n1, c), F32),
        compiler_params=_params(2),
        name="dft_stage2",
    )(l2, t_nk)
    return out.reshape(n1 * n2, c)


def _dft_matrices(s):
    n1, n2 = _dft_split(s)
    a = np.arange(n1)
    c1, s1 = _angles(a[:, None] * a[None, :], n1)
    f1 = jnp.asarray(np.block([[c1, -s1], [-s1, -c1]]).astype(BF16))
    groups = n1 // SUBLANES
    g = np.arange(groups)[:, None, None, None]
    k2 = np.arange(n2)[None, :, None, None]
    j = np.arange(SUBLANES)[None, None, :, None]
    nn = np.arange(n2)[None, None, None, :]
    c2, s2 = _angles((SUBLANES * g + j + n1 * k2) * nn, s)
    cs = np.concatenate([c2, s2], axis=3).astype(BF16)
    return n1, n2, f1, _expand_twiddles(jnp.asarray(cs.reshape(groups, n2 * SUBLANES, 2 * n2)))


def _expand_kernel(cs_ref, e_ref, o_ref):
    spread = jnp.dot(cs_ref[0], e_ref[...], preferred_element_type=F32)
    row = lax.broadcasted_iota(jnp.int32, spread.shape, 0)
    col = lax.broadcasted_iota(jnp.int32, spread.shape, 1)
    o_ref[0] = jnp.where(row % SUBLANES == col % SUBLANES, spread, 0.0).astype(BF16)


def _expand_twiddles(cs):
    groups, rows, cols = cs.shape
    wide = cols * SUBLANES
    src = jnp.arange(cols, dtype=jnp.int32)[:, None]
    slot = (src % (cols // 2)) * 2 + src // (cols // 2)
    e = jnp.arange(wide, dtype=jnp.int32)[None, :] // SUBLANES == slot
    return pl.pallas_call(
        _expand_kernel,
        grid=(groups,),
        in_specs=[pl.BlockSpec((1, rows, cols), lambda g: (g, 0, 0)),
                  pl.BlockSpec((cols, wide), lambda g: (0, 0))],
        out_specs=pl.BlockSpec((1, rows, wide), lambda g: (g, 0, 0)),
        out_shape=jax.ShapeDtypeStruct((groups, rows, wide), BF16),
        compiler_params=_params(1),
        name="expand_twiddles",
    )(cs, e.astype(BF16))


def kernel(x, c, norm_w, ada_w, ada_b, w_in_ab, w_pool, pool_scale, q_norm_w, k_norm_w, lambda_q1, lambda_k1,
           lambda_q2, lambda_k2, subln_w, w_out_ab, w_in_c, w_fourier, w_out_c):
    batch, s, d = x.shape
    assert batch == 1 and d == D_MODEL and s % ATT_TK == 0
    xs = x.reshape(s, d)
    mod = _ada_mod(c, ada_w, ada_b)
    n1, n2, f1, l2 = _dft_matrices(s)
    row = lambda v: v.reshape(1, -1)

    def mod_of(i):
        return mod[i, :, 0:d], mod[i, :, d:2 * d], mod[i, :, 2 * d:3 * d]

    def finish(ys, w_out, j, xs, i, gate_from=None):
        gate = mod_of(i)[2]
        if i + 1 == DEPTH:
            return _out_proj(ys, w_out, j, xs, gate, gate_from=gate_from), None
        shift, scale, _ = mod_of(i + 1)
        return _out_proj(ys, w_out, j, xs, gate, (row(norm_w[i + 1]), shift, scale), gate_from=gate_from)

    w_in_ab, w_out_ab, w_in_c, w_out_c = (w.astype(BF16) for w in (w_in_ab, w_out_ab, w_in_c, w_out_c))
    h = _modulate(xs, row(norm_w[0]), *mod_of(0)[:2])
    for i in range(DEPTH):
        j = i // 2
        if i % 2 == 0:
            o1, o2, o3, o4 = POOL_WIDTH, POOL_WIDTH + DIFF_WIDTH, POOL_WIDTH + 2 * DIFF_WIDTH, POOL_WIDTH + 3 * DIFF_WIDTH
            lambda_init = 0.8 - 0.6 * math.exp(-0.3 * i)
            ug = _matmul(h, w_in_ab, F32, PROJ_TM, PROJ_TN, name="pool_gate_in_proj",
                         cols=((0, o1), (o4, o4 + AB_WIDTH)), layer=j)
            qt, ka, vt = _qkv_proj(h, w_in_ab, j, o1, o2, o3, row(q_norm_w[j]), row(k_norm_w[j]))
            y_a = _pool_mix(ug, w_pool[j].astype(BF16), row(pool_scale[j]))
            y_b = _diff_attn(qt, ka, vt, row(lambda_q1[j]), row(lambda_k1[j]), row(lambda_q2[j]),
                             row(lambda_k2[j]), subln_w[j], ug, lambda_init)
            xs, h = finish([y_a, y_b], w_out_ab, j, xs, i)
        else:
            ab = _fold_channel_dft(w_fourier[j], s)
            w_pq = _fold_in_proj(w_in_c, j, ab)
            h_t = h.reshape(n1, n2, d).transpose(1, 0, 2).reshape(s, d)
            pq_t = _matmul(h_t, w_pq, BF16, PROJ_TM, PROJ_TN, name="fourier_in_proj")
            t_nk = _dft_stage1(pq_t, f1, n1, n2)
            f = _dft_stage2(t_nk, l2, n1, n2)
            xs, h = finish([f], w_out_c, j, xs, i, gate_from=(h, w_in_c, d))
    return xs.reshape(batch, s, d)
```

```python
import functools
import math

import jax
import jax.numpy as jnp
import numpy as np
from jax import lax
from jax.experimental import pallas as pl
from jax.experimental.pallas import tpu as pltpu

F32 = jnp.float32
BF16 = jnp.bfloat16

D_MODEL = 2048
DEPTH = 4
NORM_EPS = 1e-6

POOL_WINDOWS = (2, 4, 8, 16)
POOL_WIDTH = D_MODEL // 2
POOL_GROUP = POOL_WIDTH // len(POOL_WINDOWS)
POOL_HALO = 8

HEADS = 8
HEAD_DIM = 64
HEAD_V = 2 * HEAD_DIM
DIFF_WIDTH = HEADS * HEAD_V
AB_WIDTH = POOL_WIDTH + DIFF_WIDTH

FOURIER_GROUPS = 4
FOURIER_GROUP = D_MODEL // FOURIER_GROUPS

LANES = 128
SUBLANES = 8
POS_RADIX = 256
COEF_PARTS = 3
LOG2E = math.log2(math.e)

ATT_TQ = 512
ATT_TK = 512
ATT_DIAG = ATT_TQ // ATT_TK
ATT_UNROLL = 8
REF_MARGIN = 60.0
REF_FLOOR = 2.0 ** -60
BOUND_SLACK = 1.001

PROJ_TM, PROJ_TN = 2048, 512
OUT_SUB = 128

VMEM_LIMIT = 56 * 1024 * 1024


def _params(n_axes):
    return pltpu.CompilerParams(dimension_semantics=("arbitrary",) * n_axes,
                                vmem_limit_bytes=VMEM_LIMIT)


def _mod_kernel(c_ref, w_ref, b_ref, o_ref):
    c = c_ref[...]
    o_ref[...] = jnp.sum(jax.nn.silu(c) * w_ref[...], axis=0, keepdims=True) + b_ref[...]


def _ada_mod(c, ada_w, ada_b):
    depth, d, n = ada_w.shape
    tn = 1024
    return pl.pallas_call(
        _mod_kernel,
        grid=(depth, n // tn),
        in_specs=[pl.BlockSpec((d, 1), lambda i, j: (0, 0)),
                  pl.BlockSpec((None, d, tn), lambda i, j: (i, 0, j)),
                  pl.BlockSpec((None, 1, tn), lambda i, j: (i, 0, j))],
        out_specs=pl.BlockSpec((None, 1, tn), lambda i, j: (i, 0, j)),
        out_shape=jax.ShapeDtypeStruct((depth, 1, n), F32),
        compiler_params=_params(2),
        name="ada_mod",
    )(c.reshape(d, 1), ada_w, ada_b.reshape(depth, 1, n))


def _modulate_kernel(x_ref, nw_ref, shift_ref, scale_ref, o_ref):
    xf = x_ref[...]
    r = lax.rsqrt(jnp.mean(xf * xf, axis=-1, keepdims=True) + NORM_EPS)
    y = xf * r * nw_ref[...]
    y = y * (1.0 + scale_ref[...]) + shift_ref[...]
    o_ref[...] = y.astype(o_ref.dtype)


def _modulate(x, nw, shift, scale):
    s, d = x.shape
    tm = min(1024, s)
    vec = pl.BlockSpec((1, d), lambda i: (0, 0))
    return pl.pallas_call(
        _modulate_kernel,
        grid=(s // tm,),
        in_specs=[pl.BlockSpec((tm, d), lambda i: (i, 0)), vec, vec, vec],
        out_specs=pl.BlockSpec((tm, d), lambda i: (i, 0)),
        out_shape=jax.ShapeDtypeStruct((s, d), BF16),
        compiler_params=_params(1),
        name="modulate",
    )(x, nw, shift, scale)


def _cast_kernel(w_ref, o_ref):
    o_ref[...] = w_ref[...].astype(o_ref.dtype)


def _to_bf16(w):
    layers, k, n = w.shape
    tk = 512
    spec = pl.BlockSpec((None, tk, n), lambda l, i: (l, i, 0))
    return pl.pallas_call(
        _cast_kernel,
        grid=(layers, k // tk),
        in_specs=[spec],
        out_specs=spec,
        out_shape=jax.ShapeDtypeStruct(w.shape, BF16),
        compiler_params=_params(2),
        name="to_bf16",
    )(w)


def _mm_kernel(a_ref, w_ref, o_ref, *, precision):
    o_ref[...] = jnp.dot(a_ref[...], w_ref[...], preferred_element_type=F32,
                         precision=precision).astype(o_ref.dtype)


def _matmul(a, w, out_dtype, tm, tn, precision=None, name="matmul", cols=None, layer=None):
    m, k = a.shape
    cols = cols if cols is not None else ((0, w.shape[-1]),)
    n = sum(stop - start for start, stop in cols)
    tm, tn = min(tm, m), min(tn, n)
    assert all(start % tn == 0 and stop % tn == 0 for start, stop in cols)

    def wblock(j):
        jb, first = j, 0
        for start, stop in cols:
            jb = jnp.where(j >= first, j - first + start // tn, jb)
            first += (stop - start) // tn
        return jb

    if layer is None:
        wspec = pl.BlockSpec((k, tn), lambda i, j: (0, wblock(j)))
    else:
        wspec = pl.BlockSpec((None, k, tn), lambda i, j: (layer, 0, wblock(j)))
    return pl.pallas_call(
        functools.partial(_mm_kernel, precision=precision),
        grid=(m // tm, n // tn),
        in_specs=[pl.BlockSpec((tm, k), lambda i, j: (i, 0)), wspec],
        out_specs=pl.BlockSpec((tm, tn), lambda i, j: (i, j)),
        out_shape=jax.ShapeDtypeStruct((m, n), out_dtype),
        compiler_params=_params(2),
        name=name,
    )(a, w)


def _qkv_kernel(h_ref, wq_ref, wk_ref, wv_ref, qnw_ref, knw_ref, qt_ref, ka_ref, vt_ref, *, tm):
    lane = lax.broadcasted_iota(jnp.int32, (ATT_TK, LANES), 1)
    first = lane < HEAD_DIM
    pos = lax.broadcasted_iota(jnp.int32, (ATT_TK, LANES), 0)
    pos_lo = (pos % POS_RADIX).astype(F32)
    pos_hi = (pos // POS_RADIX).astype(F32)
    pos_cols = jnp.where(lane < HEAD_DIM + COEF_PARTS, pos_lo, jnp.where(lane < HEAD_DIM + 2 * COEF_PARTS, pos_hi, 0.0))

    def halves_rms(z, w):
        sq = z * z
        ss1 = jnp.sum(jnp.where(first, sq, 0.0), axis=-1, keepdims=True)
        ss2 = jnp.sum(jnp.where(first, 0.0, sq), axis=-1, keepdims=True)
        r = jnp.where(first, lax.rsqrt(ss1 / HEAD_DIM + NORM_EPS), lax.rsqrt(ss2 / HEAD_DIM + NORM_EPS))
        return z * r * w

    for cc in range(tm // ATT_TK):
        rows = slice(cc * ATT_TK, (cc + 1) * ATT_TK)
        h = h_ref[rows, :]
        zq = jnp.dot(h, wq_ref[...], preferred_element_type=F32)
        zk = jnp.dot(h, wk_ref[...], preferred_element_type=F32)
        zv = jnp.dot(h, wv_ref[...], preferred_element_type=F32)
        for hh in range(2):
            cols = slice(hh * LANES, (hh + 1) * LANES)
            qn = halves_rms(zq[:, cols], qnw_ref[...]) * (HEAD_DIM ** -0.5) * LOG2E
            kn = halves_rms(zk[:, cols], knw_ref[...])
            for m in range(2):
                qm = qn if m == 0 else pltpu.roll(qn, HEAD_DIM, axis=1)
                km = kn if m == 0 else pltpu.roll(kn, HEAD_DIM, axis=1)
                qt_ref[hh, m, :, rows] = jnp.where(first, qm, 0.0).T.astype(BF16)
                ka_ref[hh, m, cc] = jnp.where(first, km, pos_cols).astype(BF16)
            vt_ref[hh, cc] = zv[:, cols].T.astype(BF16)


def _qkv_proj(h, w, layer, col_q, col_k, col_v, qnw, knw):
    s, d = h.shape
    tm = min(2048, s)
    nchunk = s // ATT_TK
    cpt = tm // ATT_TK
    tn = 2 * LANES
    assert col_q % tn == 0 and col_k % tn == 0 and col_v % tn == 0
    wspecs = [pl.BlockSpec((None, d, tn), functools.partial(lambda i, j, jb: (layer, 0, j + jb), jb=c0 // tn))
              for c0 in (col_q, col_k, col_v)]
    vec = pl.BlockSpec((1, LANES), lambda i, j: (0, 0))
    return pl.pallas_call(
        functools.partial(_qkv_kernel, tm=tm),
        grid=(s // tm, HEADS // 2),
        in_specs=[pl.BlockSpec((tm, d), lambda i, j: (i, 0)), *wspecs, vec, vec],
        out_specs=[pl.BlockSpec((2, 2, LANES, tm), lambda i, j: (j, 0, 0, i)),
                   pl.BlockSpec((2, 2, cpt, ATT_TK, LANES), lambda i, j: (j, 0, i, 0, 0)),
                   pl.BlockSpec((2, cpt, HEAD_V, ATT_TK), lambda i, j: (j, i, 0, 0))],
        out_shape=[jax.ShapeDtypeStruct((HEADS, 2, LANES, s), BF16),
                   jax.ShapeDtypeStruct((HEADS, 2, nchunk, ATT_TK, LANES), BF16),
                   jax.ShapeDtypeStruct((HEADS, nchunk, HEAD_V, ATT_TK), BF16)],
        compiler_params=_params(2),
        name="qkv_proj",
    )(h, w, w, w, jnp.tile(qnw, (1, 2)), jnp.tile(knw, (1, 2)))


def _attn_kernel(slopes_ref, qt_ref, ka_ref, vt_ref, lq1_ref, lk1_ref, lq2_ref, lk2_ref, subw_ref, g_ref,
                 o_ref, acc_ref, mu_ref, qv_ref, kmax_ref, p_a, p_b, den_ref, *, lambda_init, nchunk, unroll):
    hd = pl.program_id(0)
    i = pl.program_id(1)
    base = hd * (1 + COEF_PARTS)
    slope = slopes_ref[base]
    q_off = lax.broadcasted_iota(jnp.int32, (1, ATT_TQ), 1).astype(F32)

    row1 = lax.broadcasted_iota(jnp.int32, (LANES - HEAD_DIM, 1), 0)
    coef_col = jnp.zeros((LANES - HEAD_DIM, 1), F32)
    for t in range(COEF_PARTS):
        piece = slopes_ref[base + 1 + t]
        coef_col = jnp.where(row1 == t, piece, coef_col)
        coef_col = jnp.where(row1 == COEF_PARTS + t, piece * POS_RADIX, coef_col)
    coef_rows = jnp.broadcast_to(coef_col, (LANES - HEAD_DIM, ATT_TQ))
    for m in range(2):
        for kind, sign in ((0, -1.0), (1, 0.0), (2, 1.0)):
            qv_ref[kind, m, 0:HEAD_DIM, :] = qt_ref[0, m, 0:HEAD_DIM, :]
            qv_ref[kind, m, HEAD_DIM:LANES, :] = (sign * coef_rows).astype(BF16)

    @pl.when(i == 0)
    def _():
        lane = lax.broadcasted_iota(jnp.int32, (ATT_TK, LANES), 1)
        for m in range(2):
            def widest(c, best):
                kc = ka_ref[0, m, c].astype(F32)
                return jnp.maximum(best, jnp.sum(jnp.where(lane < HEAD_DIM, kc * kc, 0.0), axis=1, keepdims=True))
            kmax_ref[m] = jnp.max(lax.fori_loop(0, nchunk, widest, jnp.zeros((ATT_TK, 1), F32)))

    for m in range(2):
        qf = qt_ref[0, m, 0:HEAD_DIM, :].astype(F32)
        bound = jnp.sqrt(jnp.sum(qf * qf, axis=0, keepdims=True) * kmax_ref[m]) * BOUND_SLACK
        mu_ref[m] = bound - REF_MARGIN

    c_diag = i * ATT_DIAG

    def chunk_of(e):
        if isinstance(e, int) and e < ATT_DIAG:
            return c_diag + e
        rest = e - ATT_DIAG
        return jnp.where(e < ATT_DIAG, c_diag + e, rest + ATT_DIAG * (rest >= c_diag).astype(jnp.int32))

    def side_of(c):
        return (c < c_diag).astype(jnp.int32) - (c >= c_diag + ATT_DIAG).astype(jnp.int32)

    p_bufs = (p_a, p_b)

    def probs(e, p_ref):
        c = chunk_of(e)
        side = side_of(c)
        sidef = side.astype(F32)
        gap = jnp.abs(c * ATT_TK - i * ATT_TQ).astype(F32)
        cvec = -slope * (sidef * sidef * gap + sidef * q_off)
        for m in range(2):
            st = jnp.dot(ka_ref[0, m, c], qv_ref[1 + side, m], preferred_element_type=F32)
            if isinstance(e, int) and e < ATT_DIAG:
                key_off = lax.broadcasted_iota(jnp.int32, (ATT_TK, ATT_TQ), 0) + e * ATT_TK
                qry_off = lax.broadcasted_iota(jnp.int32, (ATT_TK, ATT_TQ), 1)
                st = st - slope * jnp.abs(key_off - qry_off).astype(F32)
            p = jnp.exp2(st - (mu_ref[m] - cvec))
            p_ref[m] = p.astype(BF16)
            den_ref[m] += jnp.sum(p.reshape(ATT_TK // SUBLANES, SUBLANES, ATT_TQ), axis=0)

    def group(e0, count, probs_last):
        sums = [None, None]
        for u in range(count):
            if u < count - 1 or probs_last:
                probs(e0 + u + 1, p_bufs[(u + 1) % 2])
            c = chunk_of(e0 + u)
            for m in range(2):
                pv = jnp.dot(vt_ref[0, c], p_bufs[u % 2][m], preferred_element_type=F32)
                sums[m] = pv if sums[m] is None else sums[m] + pv
        for m in range(2):
            acc_ref[m] += sums[m]

    def loop_body(it, carry):
        group(it * unroll, unroll, True)
        return carry

    def one_pass(_):
        acc_ref[...] = jnp.zeros(acc_ref.shape, F32)
        den_ref[...] = jnp.zeros(den_ref.shape, F32)
        n_groups = nchunk // unroll
        probs(0, p_a)
        group(0, unroll, True)
        lax.fori_loop(1, n_groups - 1, loop_body, 0)
        group((n_groups - 1) * unroll, unroll, False)
        lowest = None
        for m in range(2):
            den = jnp.sum(den_ref[m], axis=0, keepdims=True)
            mu_ref[m] = jnp.where(den < REF_FLOOR, mu_ref[m] - 2.0 * REF_MARGIN, mu_ref[m])
            lowest = jnp.min(den) if lowest is None else jnp.minimum(lowest, jnp.min(den))
        return (lowest < REF_FLOOR).astype(jnp.int32)

    lax.while_loop(lambda retry: retry > 0, one_pass, jnp.int32(1))

    lam = (jnp.exp(jnp.sum(lq1_ref[...] * lk1_ref[...], axis=-1, keepdims=True))
           - jnp.exp(jnp.sum(lq2_ref[...] * lk2_ref[...], axis=-1, keepdims=True)) + lambda_init)
    den1 = jnp.sum(den_ref[0], axis=0, keepdims=True)
    den2 = jnp.sum(den_ref[1], axis=0, keepdims=True)
    o = acc_ref[0] / den1 - lam * (acc_ref[1] / den2)
    r = lax.rsqrt(jnp.mean(o * o, axis=0, keepdims=True) + NORM_EPS)
    y = (o * r * subw_ref[...]) * (1.0 - lambda_init)
    o_ref[...] = (y.T * jax.nn.silu(g_ref[...])).astype(o_ref.dtype)


def _diff_attn(qt, ka, vt, lq1, lk1, lq2, lk2, subw, g, lambda_init):
    s = qt.shape[-1]
    g_col0 = (g.shape[1] - DIFF_WIDTH) // HEAD_V
    nchunk = s // ATT_TK
    unroll = min(ATT_UNROLL, nchunk // 2)
    assert unroll % 2 == 0 and nchunk % unroll == 0 and unroll >= ATT_DIAG and s % ATT_TQ == 0
    whole = jnp.asarray([LOG2E * 2.0 ** (-8.0 * (h + 1) / HEADS) for h in range(HEADS)], F32)
    pieces, rest = [], whole
    for _ in range(COEF_PARTS):
        pieces.append(rest.astype(BF16).astype(F32))
        rest = rest - pieces[-1]
    slopes = jnp.stack([whole] + pieces, axis=1).reshape(-1)
    vec = pl.BlockSpec((1, HEAD_DIM), lambda h, i: (0, 0))
    return pl.pallas_call(
        functools.partial(_attn_kernel, lambda_init=lambda_init, nchunk=nchunk, unroll=unroll),
        grid=(HEADS, s // ATT_TQ),
        in_specs=[pl.BlockSpec(memory_space=pltpu.SMEM),
                  pl.BlockSpec((1, 2, LANES, ATT_TQ), lambda h, i: (h, 0, 0, i)),
                  pl.BlockSpec((1, 2, nchunk, ATT_TK, LANES), lambda h, i: (h, 0, 0, 0, 0)),
                  pl.BlockSpec((1, nchunk, HEAD_V, ATT_TK), lambda h, i: (h, 0, 0, 0)),
                  vec, vec, vec, vec,
                  pl.BlockSpec((HEAD_V, 1), lambda h, i: (0, 0)),
                  pl.BlockSpec((ATT_TQ, HEAD_V), lambda h, i: (i, g_col0 + h))],
        out_specs=pl.BlockSpec((ATT_TQ, HEAD_V), lambda h, i: (i, h)),
        out_shape=jax.ShapeDtypeStruct((s, DIFF_WIDTH), BF16),
        scratch_shapes=[pltpu.VMEM((2, HEAD_V, ATT_TQ), F32), pltpu.VMEM((2, 1, ATT_TQ), F32),
                        pltpu.VMEM((3, 2, LANES, ATT_TQ), BF16), pltpu.SMEM((2,), F32),
                        pltpu.VMEM((2, ATT_TK, ATT_TQ), BF16), pltpu.VMEM((2, ATT_TK, ATT_TQ), BF16),
                        pltpu.VMEM((2, SUBLANES, ATT_TQ), F32)],
        compiler_params=_params(2),
        name="diff_attn",
    )(slopes, qt, ka, vt, lq1, lk1, lq2, lk2, subw.reshape(HEAD_V, 1), g)


def _pool_kernel(prev_ref, cur_ref, next_ref, w_ref, scale_ref, g_ref, o_ref, ext_ref, *, tm, seq):
    i = pl.program_id(0)
    last = pl.num_programs(0) - 1
    zero_halo = jnp.zeros((POOL_HALO, POOL_WIDTH), F32)
    ext_ref[0:POOL_HALO] = jnp.where(i > 0, prev_ref[...], zero_halo)
    ext_ref[POOL_HALO:POOL_HALO + tm] = cur_ref[...]
    ext_ref[POOL_HALO + tm:POOL_HALO + tm + POOL_HALO] = jnp.where(i < last, next_ref[...], zero_halo)
    t = lax.broadcasted_iota(jnp.int32, (tm, 1), 0) + i * tm
    for g, w in enumerate(POOL_WINDOWS):
        cols = pl.ds(g * POOL_GROUP, POOL_GROUP)
        span = tm + 2 * POOL_HALO
        width = 1
        while width < w:
            span -= width
            ext_ref[pl.ds(0, span), cols] = ext_ref[pl.ds(0, span), cols] + ext_ref[pl.ds(width, span), cols]
            width *= 2
        assert width == w
        win = ext_ref[pl.ds(POOL_HALO - w // 2, tm), cols]
        lo = jnp.maximum(t - w // 2, 0)
        hi = jnp.minimum(t + (w - w // 2) - 1, seq - 1)
        cnt = (hi - lo + 1).astype(F32)
        pooled = win * (1.0 / cnt) - cur_ref[:, cols]
        y = jnp.dot(pooled.astype(BF16), w_ref[g], preferred_element_type=F32)
        o_ref[:, cols] = (y * scale_ref[:, cols] * jax.nn.silu(g_ref[:, cols])).astype(o_ref.dtype)


def _pool_mix(ug, w_pool, pool_scale):
    s = ug.shape[0]
    width = POOL_WIDTH
    tm = 512
    hb = tm // POOL_HALO
    nhalo = s // POOL_HALO
    return pl.pallas_call(
        functools.partial(_pool_kernel, tm=tm, seq=s),
        grid=(s // tm,),
        in_specs=[pl.BlockSpec((POOL_HALO, width), lambda i: (jnp.maximum(i * hb - 1, 0), 0)),
                  pl.BlockSpec((tm, width), lambda i: (i, 0)),
                  pl.BlockSpec((POOL_HALO, width), lambda i: (jnp.minimum((i + 1) * hb, nhalo - 1), 0)),
                  pl.BlockSpec(w_pool.shape, lambda i: (0, 0, 0)),
                  pl.BlockSpec((1, width), lambda i: (0, 0)),
                  pl.BlockSpec((tm, width), lambda i: (i, 1))],
        out_specs=pl.BlockSpec((tm, width), lambda i: (i, 0)),
        out_shape=jax.ShapeDtypeStruct((s, width), BF16),
        scratch_shapes=[pltpu.VMEM((tm + 2 * POOL_HALO, width), F32)],
        compiler_params=_params(1),
        name="pool_mix",
    )(ug, ug, ug, w_pool, pool_scale, ug)


def _out_kernel(*refs, widths, tm, modulate_next, gate_proj):
    n = len(widths)
    y_refs = refs[:n]
    refs = refs[n:]
    if gate_proj:
        hin_ref, wg_ref = refs[:2]
        refs = refs[2:]
    w_ref, x_ref, gate_ref = refs[:3]
    if modulate_next:
        nw_ref, shift_ref, scale_ref, o_ref, h_ref = refs[3:]
        mul = nw_ref[...]
        one_plus_scale = 1.0 + scale_ref[...]
    else:
        (o_ref,) = refs[3:]
    for sb in range(tm // OUT_SUB):
        rows = slice(sb * OUT_SUB, (sb + 1) * OUT_SUB)
        acc = None
        off = 0
        for y_ref, wd in zip(y_refs, widths):
            y = y_ref[rows, :]
            if gate_proj:
                y = y * jax.nn.silu(jnp.dot(hin_ref[rows, :], wg_ref[:, off:off + wd], preferred_element_type=F32))
            part = jnp.dot(y.astype(BF16), w_ref[off:off + wd, :], preferred_element_type=F32)
            acc = part if acc is None else acc + part
            off += wd
        xn = x_ref[rows, :] + gate_ref[...] * acc
        o_ref[rows, :] = xn
        if modulate_next:
            r = lax.rsqrt(jnp.mean(xn * xn, axis=-1, keepdims=True) + NORM_EPS)
            h_ref[rows, :] = ((xn * r * mul) * one_plus_scale + shift_ref[...]).astype(BF16)


def _out_proj(ys, w, layer, x, gate, next_mod=None, gate_from=None):
    s, d = x.shape
    k = w.shape[1]
    gate_proj = gate_from is not None
    tm = min(256 if gate_proj else 512, s)
    widths = tuple(y.shape[1] for y in ys)
    vec = pl.BlockSpec((1, d), lambda i: (0, 0))
    rows = pl.BlockSpec((tm, d), lambda i: (i, 0))
    modulate_next = next_mod is not None
    gate_specs, gate_args = [], ()
    if gate_proj:
        hin, wg, col0 = gate_from
        assert col0 % k == 0 and sum(widths) == k
        gate_specs = [pl.BlockSpec((tm, hin.shape[1]), lambda i: (i, 0)),
                      pl.BlockSpec((None, wg.shape[1], k), lambda i: (layer, 0, col0 // k))]
        gate_args = (hin, wg)
    return pl.pallas_call(
        functools.partial(_out_kernel, widths=widths, tm=tm, modulate_next=modulate_next, gate_proj=gate_proj),
        grid=(s // tm,),
        in_specs=[pl.BlockSpec((tm, wd), lambda i: (i, 0)) for wd in widths] + gate_specs
        + [pl.BlockSpec((None, k, d), lambda i: (layer, 0, 0)), rows, vec] + ([vec, vec, vec] if modulate_next else []),
        out_specs=[rows, rows] if modulate_next else rows,
        out_shape=([jax.ShapeDtypeStruct((s, d), F32), jax.ShapeDtypeStruct((s, d), BF16)] if modulate_next
                   else jax.ShapeDtypeStruct((s, d), F32)),
        compiler_params=_params(1),
        name="out_proj",
    )(*ys, *gate_args, w, x, gate, *(next_mod if modulate_next else ()))


def _dft_split(s):
    n1 = 1 << (int(math.log2(s)) // 2)
    n2 = s // n1
    assert n1 * n2 == s and n1 % SUBLANES == 0 and n2 % SUBLANES == 0
    return n1, n2


def _angles(num, den):
    ang = (2.0 * math.pi / den) * (num % den)
    return np.cos(ang), np.sin(ang)


def _fold_cs_kernel(cs_ref, wf_ref, o_ref):
    wf = wf_ref[...]
    o_ref[:, 0:FOURIER_GROUP] = jnp.dot(cs_ref[0], wf, preferred_element_type=F32,
                                        precision=lax.Precision.HIGHEST)
    o_ref[:, FOURIER_GROUP:] = jnp.dot(cs_ref[1], wf, preferred_element_type=F32,
                                       precision=lax.Precision.HIGHEST)


def _fold_channel_dft(w_fourier_j, seq):
    fg = FOURIER_GROUP
    idx = np.arange(fg)
    cc, sc = _angles(idx[:, None] * idx[None, :], fg)
    cs = jnp.asarray(np.stack([cc, sc]) / math.sqrt(seq * fg), F32)
    return pl.pallas_call(
        _fold_cs_kernel,
        grid=(FOURIER_GROUPS,),
        in_specs=[pl.BlockSpec((2, fg, fg), lambda g: (0, 0, 0)),
                  pl.BlockSpec((None, fg, fg), lambda g: (g, 0, 0))],
        out_specs=pl.BlockSpec((None, fg, 2 * fg), lambda g: (g, 0, 0)),
        out_shape=jax.ShapeDtypeStruct((FOURIER_GROUPS, fg, 2 * fg), F32),
        compiler_params=_params(1),
        name="fold_channel_dft",
    )(cs, w_fourier_j)


def _fold_win_kernel(w_ref, ab_ref, o_ref):
    o_ref[...] = jnp.dot(w_ref[...], ab_ref[...].astype(BF16), preferred_element_type=F32).astype(BF16)


def _fold_in_proj(w_in, layer, ab):
    d = w_in.shape[1]
    fg = FOURIER_GROUP
    return pl.pallas_call(
        _fold_win_kernel,
        grid=(FOURIER_GROUPS, 2),
        in_specs=[pl.BlockSpec((None, d, fg), lambda g, t: (layer, 0, g)),
                  pl.BlockSpec((None, fg, fg), lambda g, t: (g, 0, t))],
        out_specs=pl.BlockSpec((d, fg), lambda g, t: (0, g + t * FOURIER_GROUPS)),
        out_shape=jax.ShapeDtypeStruct((d, 2 * D_MODEL), BF16),
        compiler_params=_params(2),
        name="fold_in_proj",
    )(w_in, ab)


HIGH_HALF = -65536


def _bf16_bits(v):
    b = lax.bitcast_convert_type(v, jnp.int32)
    return b + 0x7FFF + (lax.shift_right_logical(b, 16) & 1)


def _dft1_kernel(f_ref, p_ref, q_ref, t_ref, *, nb, n1):
    f = f_ref[...]
    for jj in range(nb):
        rows = slice(jj * n1, (jj + 1) * n1)
        rhs = jnp.concatenate([p_ref[rows, :], q_ref[rows, :]], axis=0)
        t = jnp.dot(f, rhs, preferred_element_type=F32)
        t_ref[jj] = (_bf16_bits(t[0:n1]) & HIGH_HALF) | lax.shift_right_logical(_bf16_bits(t[n1:2 * n1]), 16)


def _dft_stage1(pq_t, f1, n1, n2):
    s, c2 = pq_t.shape
    c = c2 // 2
    nb = SUBLANES
    cb = min(1024, c)
    ncb = c // cb
    return pl.pallas_call(
        functools.partial(_dft1_kernel, nb=nb, n1=n1),
        grid=(n2 // nb, ncb),
        in_specs=[pl.BlockSpec((2 * n1, 2 * n1), lambda b, j: (0, 0)),
                  pl.BlockSpec((nb * n1, cb), lambda b, j: (b, j)),
                  pl.BlockSpec((nb * n1, cb), lambda b, j: (b, j + ncb))],
        out_specs=pl.BlockSpec((nb, n1, cb), lambda b, j: (b, 0, j)),
        out_shape=jax.ShapeDtypeStruct((n2, n1, c), jnp.int32),
        compiler_params=_params(2),
        name="dft_stage1",
    )(f1, pq_t, pq_t)


def _dft2_kernel(l_ref, t_ref, o_ref, *, n2, cb):
    packed = t_ref[...]
    re = lax.bitcast_convert_type(packed & HIGH_HALF, F32)
    im = lax.bitcast_convert_type(lax.shift_left(packed, 16), F32)
    rhs = jnp.stack([re, im], axis=1).reshape(n2 * 2 * SUBLANES, cb).astype(BF16)
    o_ref[...] = jnp.dot(l_ref[0], rhs, preferred_element_type=F32).reshape(n2, SUBLANES, cb)


def _dft_stage2(t_nk, l2, n1, n2):
    c = t_nk.shape[-1]
    cb = min(1024, c)
    row_blocks = pl.BlockSpec((n2, SUBLANES, cb), lambda g, j: (0, g, j))
    out = pl.pallas_call(
        functools.partial(_dft2_kernel, n2=n2, cb=cb),
        grid=(n1 // SUBLANES, c // cb),
        in_specs=[pl.BlockSpec((1, SUBLANES * n2, SUBLANES * 2 * n2), lambda g, j: (g, 0, 0)),
                  row_blocks],
        out_specs=row_blocks,
        out_shape=jax.ShapeDtypeStruct((n2, n1, c), F32),
        compiler_params=_params(2),
        name="dft_stage2",
    )(l2, t_nk)
    return out.reshape(n1 * n2, c)


def _dft_matrices(s):
    n1, n2 = _dft_split(s)
    a = np.arange(n1)
    c1, s1 = _angles(a[:, None] * a[None, :], n1)
    f1 = jnp.asarray(np.block([[c1, -s1], [-s1, -c1]]).astype(BF16))
    groups = n1 // SUBLANES
    g = np.arange(groups)[:, None, None, None]
    k2 = np.arange(n2)[None, :, None, None]
    j = np.arange(SUBLANES)[None, None, :, None]
    nn = np.arange(n2)[None, None, None, :]
    c2, s2 = _angles((SUBLANES * g + j + n1 * k2) * nn, s)
    cs = np.concatenate([c2, s2], axis=3).astype(BF16)
    return n1, n2, f1, _expand_twiddles(jnp.asarray(cs.reshape(groups, n2 * SUBLANES, 2 * n2)))


def _expand_kernel(cs_ref, e_ref, o_ref):
    spread = jnp.dot(cs_ref[0], e_ref[...], preferred_element_type=F32)
    row = lax.broadcasted_iota(jnp.int32, spread.shape, 0)
    col = lax.broadcasted_iota(jnp.int32, spread.shape, 1)
    o_ref[0] = jnp.where(row % SUBLANES == col % SUBLANES, spread, 0.0).astype(BF16)


def _expand_twiddles(cs):
    groups, rows, cols = cs.shape
    wide = cols * SUBLANES
    src = jnp.arange(cols, dtype=jnp.int32)[:, None]
    slot = (src % (cols // 2)) * 2 + src // (cols // 2)
    e = jnp.arange(wide, dtype=jnp.int32)[None, :] // SUBLANES == slot
    return pl.pallas_call(
        _expand_kernel,
        grid=(groups,),
        in_specs=[pl.BlockSpec((1, rows, cols), lambda g: (g, 0, 0)),
                  pl.BlockSpec((cols, wide), lambda g: (0, 0))],
        out_specs=pl.BlockSpec((1, rows, wide), lambda g: (g, 0, 0)),
        out_shape=jax.ShapeDtypeStruct((groups, rows, wide), BF16),
        compiler_params=_params(1),
        name="expand_twiddles",
    )(cs, e.astype(BF16))


def kernel(x, c, norm_w, ada_w, ada_b, w_in_ab, w_pool, pool_scale, q_norm_w, k_norm_w, lambda_q1, lambda_k1,
           lambda_q2, lambda_k2, subln_w, w_out_ab, w_in_c, w_fourier, w_out_c):
    batch, s, d = x.shape
    assert batch == 1 and d == D_MODEL and s % ATT_TK == 0
    xs = x.reshape(s, d)
    mod = _ada_mod(c, ada_w, ada_b)
    n1, n2, f1, l2 = _dft_matrices(s)
    row = lambda v: v.reshape(1, -1)

    def mod_of(i):
        return mod[i, :, 0:d], mod[i, :, d:2 * d], mod[i, :, 2 * d:3 * d]

    def finish(ys, w_out, j, xs, i, gate_from=None):
        gate = mod_of(i)[2]
        if i + 1 == DEPTH:
            return _out_proj(ys, w_out, j, xs, gate, gate_from=gate_from), None
        shift, scale, _ = mod_of(i + 1)
        return _out_proj(ys, w_out, j, xs, gate, (row(norm_w[i + 1]), shift, scale), gate_from=gate_from)

    w_in_ab, w_out_ab, w_in_c, w_out_c = (_to_bf16(w) for w in (w_in_ab, w_out_ab, w_in_c, w_out_c))
    h = _modulate(xs, row(norm_w[0]), *mod_of(0)[:2])
    for i in range(DEPTH):
        j = i // 2
        if i % 2 == 0:
            o1, o2, o3, o4 = POOL_WIDTH, POOL_WIDTH + DIFF_WIDTH, POOL_WIDTH + 2 * DIFF_WIDTH, POOL_WIDTH + 3 * DIFF_WIDTH
            lambda_init = 0.8 - 0.6 * math.exp(-0.3 * i)
            ug = _matmul(h, w_in_ab, F32, PROJ_TM, PROJ_TN, name="pool_gate_in_proj",
                         cols=((0, o1), (o4, o4 + AB_WIDTH)), layer=j)
            qt, ka, vt = _qkv_proj(h, w_in_ab, j, o1, o2, o3, row(q_norm_w[j]), row(k_norm_w[j]))
            y_a = _pool_mix(ug, w_pool[j].astype(BF16), row(pool_scale[j]))
            y_b = _diff_attn(qt, ka, vt, row(lambda_q1[j]), row(lambda_k1[j]), row(lambda_q2[j]),
                             row(lambda_k2[j]), subln_w[j], ug, lambda_init)
            xs, h = finish([y_a, y_b], w_out_ab, j, xs, i)
        else:
            ab = _fold_channel_dft(w_fourier[j], s)
            w_pq = _fold_in_proj(w_in_c, j, ab)
            h_t = h.reshape(n1, n2, d).transpose(1, 0, 2).reshape(s, d)
            pq_t = _matmul(h_t, w_pq, BF16, PROJ_TM, PROJ_TN, name="fourier_in_proj")
            t_nk = _dft_stage1(pq_t, f1, n1, n2)
            f = _dft_stage2(t_nk, l2, n1, n2)
            xs, h = finish([f], w_out_c, j, xs, i, gate_from=(h, w_in_c, d))
    return xs.reshape(batch, s, d)
```

```python
import functools
import math

import jax
import jax.numpy as jnp
import numpy as np
from jax import lax
from jax.experimental import pallas as pl
from jax.experimental.pallas import tpu as pltpu

F32 = jnp.float32
BF16 = jnp.bfloat16

D_MODEL = 2048
DEPTH = 4
NORM_EPS = 1e-6

POOL_WINDOWS = (2, 4, 8, 16)
POOL_WIDTH = D_MODEL // 2
POOL_GROUP = POOL_WIDTH // len(POOL_WINDOWS)
POOL_HALO = 8

HEADS = 8
HEAD_DIM = 64
HEAD_V = 2 * HEAD_DIM
DIFF_WIDTH = HEADS * HEAD_V
AB_WIDTH = POOL_WIDTH + DIFF_WIDTH

FOURIER_GROUPS = 4
FOURIER_GROUP = D_MODEL // FOURIER_GROUPS

LANES = 128
SUBLANES = 8
POS_RADIX = 256
COEF_PARTS = 3
LOG2E = math.log2(math.e)

ATT_TQ = 512
ATT_TK = 512
ATT_DIAG = ATT_TQ // ATT_TK
ATT_UNROLL = 8
REF_MARGIN = 60.0
REF_FLOOR = 2.0 ** -60
BOUND_SLACK = 1.001

PROJ_TM, PROJ_TN = 2048, 1024
OUT_SUB = 128

VMEM_LIMIT = 56 * 1024 * 1024


def _params(n_axes):
    return pltpu.CompilerParams(dimension_semantics=("arbitrary",) * n_axes,
                                vmem_limit_bytes=VMEM_LIMIT)


def _mod_kernel(c_ref, w_ref, b_ref, o_ref):
    c = c_ref[...]
    o_ref[...] = jnp.sum(jax.nn.silu(c) * w_ref[...], axis=0, keepdims=True) + b_ref[...]


def _ada_mod(c, ada_w, ada_b):
    depth, d, n = ada_w.shape
    tn = 1024
    return pl.pallas_call(
        _mod_kernel,
        grid=(depth, n // tn),
        in_specs=[pl.BlockSpec((d, 1), lambda i, j: (0, 0)),
                  pl.BlockSpec((None, d, tn), lambda i, j: (i, 0, j)),
                  pl.BlockSpec((None, 1, tn), lambda i, j: (i, 0, j))],
        out_specs=pl.BlockSpec((None, 1, tn), lambda i, j: (i, 0, j)),
        out_shape=jax.ShapeDtypeStruct((depth, 1, n), F32),
        compiler_params=_params(2),
        name="ada_mod",
    )(c.reshape(d, 1), ada_w, ada_b.reshape(depth, 1, n))


def _modulate_kernel(x_ref, nw_ref, shift_ref, scale_ref, o_ref):
    xf = x_ref[...]
    r = lax.rsqrt(jnp.mean(xf * xf, axis=-1, keepdims=True) + NORM_EPS)
    y = xf * r * nw_ref[...]
    y = y * (1.0 + scale_ref[...]) + shift_ref[...]
    o_ref[...] = y.astype(o_ref.dtype)


def _modulate(x, nw, shift, scale):
    s, d = x.shape
    tm = min(1024, s)
    vec = pl.BlockSpec((1, d), lambda i: (0, 0))
    return pl.pallas_call(
        _modulate_kernel,
        grid=(s // tm,),
        in_specs=[pl.BlockSpec((tm, d), lambda i: (i, 0)), vec, vec, vec],
        out_specs=pl.BlockSpec((tm, d), lambda i: (i, 0)),
        out_shape=jax.ShapeDtypeStruct((s, d), BF16),
        compiler_params=_params(1),
        name="modulate",
    )(x, nw, shift, scale)


def _mm_kernel(a_ref, w_ref, o_ref, *, precision):
    o_ref[...] = jnp.dot(a_ref[...], w_ref[...], preferred_element_type=F32,
                         precision=precision).astype(o_ref.dtype)


def _matmul(a, w, out_dtype, tm, tn, precision=None, name="matmul", cols=None, layer=None):
    m, k = a.shape
    cols = cols if cols is not None else ((0, w.shape[-1]),)
    n = sum(stop - start for start, stop in cols)
    tm, tn = min(tm, m), min(tn, n)
    assert all(start % tn == 0 and stop % tn == 0 for start, stop in cols)

    def wblock(j):
        jb, first = j, 0
        for start, stop in cols:
            jb = jnp.where(j >= first, j - first + start // tn, jb)
            first += (stop - start) // tn
        return jb

    if layer is None:
        wspec = pl.BlockSpec((k, tn), lambda i, j: (0, wblock(j)))
    else:
        wspec = pl.BlockSpec((None, k, tn), lambda i, j: (layer, 0, wblock(j)))
    return pl.pallas_call(
        functools.partial(_mm_kernel, precision=precision),
        grid=(m // tm, n // tn),
        in_specs=[pl.BlockSpec((tm, k), lambda i, j: (i, 0)), wspec],
        out_specs=pl.BlockSpec((tm, tn), lambda i, j: (i, j)),
        out_shape=jax.ShapeDtypeStruct((m, n), out_dtype),
        compiler_params=_params(2),
        name=name,
    )(a, w)


def _qkv_kernel(h_ref, wq_ref, wk_ref, wv_ref, qnw_ref, knw_ref, qt_ref, ka_ref, vt_ref, *, tm):
    lane = lax.broadcasted_iota(jnp.int32, (ATT_TK, LANES), 1)
    first = lane < HEAD_DIM
    pos = lax.broadcasted_iota(jnp.int32, (ATT_TK, LANES), 0)
    pos_lo = (pos % POS_RADIX).astype(F32)
    pos_hi = (pos // POS_RADIX).astype(F32)
    pos_cols = jnp.where(lane < HEAD_DIM + COEF_PARTS, pos_lo, jnp.where(lane < HEAD_DIM + 2 * COEF_PARTS, pos_hi, 0.0))

    def halves_rms(z, w):
        sq = z * z
        ss1 = jnp.sum(jnp.where(first, sq, 0.0), axis=-1, keepdims=True)
        ss2 = jnp.sum(jnp.where(first, 0.0, sq), axis=-1, keepdims=True)
        r = jnp.where(first, lax.rsqrt(ss1 / HEAD_DIM + NORM_EPS), lax.rsqrt(ss2 / HEAD_DIM + NORM_EPS))
        return z * r * w

    for cc in range(tm // ATT_TK):
        rows = slice(cc * ATT_TK, (cc + 1) * ATT_TK)
        h = h_ref[rows, :]
        zq = jnp.dot(h, wq_ref[...], preferred_element_type=F32)
        zk = jnp.dot(h, wk_ref[...], preferred_element_type=F32)
        zv = jnp.dot(h, wv_ref[...], preferred_element_type=F32)
        for hh in range(2):
            cols = slice(hh * LANES, (hh + 1) * LANES)
            qn = halves_rms(zq[:, cols], qnw_ref[...]) * (HEAD_DIM ** -0.5) * LOG2E
            kn = halves_rms(zk[:, cols], knw_ref[...])
            for m in range(2):
                qm = qn if m == 0 else pltpu.roll(qn, HEAD_DIM, axis=1)
                km = kn if m == 0 else pltpu.roll(kn, HEAD_DIM, axis=1)
                qt_ref[hh, m, :, rows] = jnp.where(first, qm, 0.0).T.astype(BF16)
                ka_ref[hh, m, cc] = jnp.where(first, km, pos_cols).astype(BF16)
            vt_ref[hh, cc] = zv[:, cols].T.astype(BF16)


def _qkv_proj(h, w, layer, col_q, col_k, col_v, qnw, knw):
    s, d = h.shape
    tm = min(2048, s)
    nchunk = s // ATT_TK
    cpt = tm // ATT_TK
    tn = 2 * LANES
    assert col_q % tn == 0 and col_k % tn == 0 and col_v % tn == 0
    wspecs = [pl.BlockSpec((None, d, tn), functools.partial(lambda i, j, jb: (layer, 0, j + jb), jb=c0 // tn))
              for c0 in (col_q, col_k, col_v)]
    vec = pl.BlockSpec((1, LANES), lambda i, j: (0, 0))
    return pl.pallas_call(
        functools.partial(_qkv_kernel, tm=tm),
        grid=(s // tm, HEADS // 2),
        in_specs=[pl.BlockSpec((tm, d), lambda i, j: (i, 0)), *wspecs, vec, vec],
        out_specs=[pl.BlockSpec((2, 2, LANES, tm), lambda i, j: (j, 0, 0, i)),
                   pl.BlockSpec((2, 2, cpt, ATT_TK, LANES), lambda i, j: (j, 0, i, 0, 0)),
                   pl.BlockSpec((2, cpt, HEAD_V, ATT_TK), lambda i, j: (j, i, 0, 0))],
        out_shape=[jax.ShapeDtypeStruct((HEADS, 2, LANES, s), BF16),
                   jax.ShapeDtypeStruct((HEADS, 2, nchunk, ATT_TK, LANES), BF16),
                   jax.ShapeDtypeStruct((HEADS, nchunk, HEAD_V, ATT_TK), BF16)],
        compiler_params=_params(2),
        name="qkv_proj",
    )(h, w, w, w, jnp.tile(qnw, (1, 2)), jnp.tile(knw, (1, 2)))


def _attn_kernel(slopes_ref, qt_ref, ka_ref, vt_ref, lq1_ref, lk1_ref, lq2_ref, lk2_ref, subw_ref, g_ref,
                 o_ref, acc_ref, mu_ref, qv_ref, kmax_ref, p_a, p_b, den_ref, *, lambda_init, nchunk, unroll):
    hd = pl.program_id(0)
    i = pl.program_id(1)
    base = hd * (1 + COEF_PARTS)
    slope = slopes_ref[base]
    q_off = lax.broadcasted_iota(jnp.int32, (1, ATT_TQ), 1).astype(F32)

    row1 = lax.broadcasted_iota(jnp.int32, (LANES - HEAD_DIM, 1), 0)
    coef_col = jnp.zeros((LANES - HEAD_DIM, 1), F32)
    for t in range(COEF_PARTS):
        piece = slopes_ref[base + 1 + t]
        coef_col = jnp.where(row1 == t, piece, coef_col)
        coef_col = jnp.where(row1 == COEF_PARTS + t, piece * POS_RADIX, coef_col)
    coef_rows = jnp.broadcast_to(coef_col, (LANES - HEAD_DIM, ATT_TQ))
    for m in range(2):
        for kind, sign in ((0, -1.0), (1, 0.0), (2, 1.0)):
            qv_ref[kind, m, 0:HEAD_DIM, :] = qt_ref[0, m, 0:HEAD_DIM, :]
            qv_ref[kind, m, HEAD_DIM:LANES, :] = (sign * coef_rows).astype(BF16)

    @pl.when(i == 0)
    def _():
        lane = lax.broadcasted_iota(jnp.int32, (ATT_TK, LANES), 1)
        for m in range(2):
            def widest(c, best):
                kc = ka_ref[0, m, c].astype(F32)
                return jnp.maximum(best, jnp.sum(jnp.where(lane < HEAD_DIM, kc * kc, 0.0), axis=1, keepdims=True))
            kmax_ref[m] = jnp.max(lax.fori_loop(0, nchunk, widest, jnp.zeros((ATT_TK, 1), F32)))

    for m in range(2):
        qf = qt_ref[0, m, 0:HEAD_DIM, :].astype(F32)
        bound = jnp.sqrt(jnp.sum(qf * qf, axis=0, keepdims=True) * kmax_ref[m]) * BOUND_SLACK
        mu_ref[m] = bound - REF_MARGIN

    c_diag = i * ATT_DIAG

    def chunk_of(e):
        if isinstance(e, int) and e < ATT_DIAG:
            return c_diag + e
        rest = e - ATT_DIAG
        return jnp.where(e < ATT_DIAG, c_diag + e, rest + ATT_DIAG * (rest >= c_diag).astype(jnp.int32))

    def side_of(c):
        return (c < c_diag).astype(jnp.int32) - (c >= c_diag + ATT_DIAG).astype(jnp.int32)

    p_bufs = (p_a, p_b)

    def probs(e, p_ref):
        c = chunk_of(e)
        side = side_of(c)
        sidef = side.astype(F32)
        gap = jnp.abs(c * ATT_TK - i * ATT_TQ).astype(F32)
        cvec = -slope * (sidef * sidef * gap + sidef * q_off)
        for m in range(2):
            st = jnp.dot(ka_ref[0, m, c], qv_ref[1 + side, m], preferred_element_type=F32)
            if isinstance(e, int) and e < ATT_DIAG:
                key_off = lax.broadcasted_iota(jnp.int32, (ATT_TK, ATT_TQ), 0) + e * ATT_TK
                qry_off = lax.broadcasted_iota(jnp.int32, (ATT_TK, ATT_TQ), 1)
                st = st - slope * jnp.abs(key_off - qry_off).astype(F32)
            p = jnp.exp2(st - (mu_ref[m] - cvec))
            p_ref[m] = p.astype(BF16)
            den_ref[m] += jnp.sum(p.reshape(ATT_TK // SUBLANES, SUBLANES, ATT_TQ), axis=0)

    def group(e0, count, probs_last):
        sums = [None, None]
        for u in range(count):
            if u < count - 1 or probs_last:
                probs(e0 + u + 1, p_bufs[(u + 1) % 2])
            c = chunk_of(e0 + u)
            for m in range(2):
                pv = jnp.dot(vt_ref[0, c], p_bufs[u % 2][m], preferred_element_type=F32)
                sums[m] = pv if sums[m] is None else sums[m] + pv
        for m in range(2):
            acc_ref[m] += sums[m]

    def loop_body(it, carry):
        group(it * unroll, unroll, True)
        return carry

    def one_pass(_):
        acc_ref[...] = jnp.zeros(acc_ref.shape, F32)
        den_ref[...] = jnp.zeros(den_ref.shape, F32)
        n_groups = nchunk // unroll
        probs(0, p_a)
        group(0, unroll, True)
        lax.fori_loop(1, n_groups - 1, loop_body, 0)
        group((n_groups - 1) * unroll, unroll, False)
        lowest = None
        for m in range(2):
            den = jnp.sum(den_ref[m], axis=0, keepdims=True)
            mu_ref[m] = jnp.where(den < REF_FLOOR, mu_ref[m] - 2.0 * REF_MARGIN, mu_ref[m])
            lowest = jnp.min(den) if lowest is None else jnp.minimum(lowest, jnp.min(den))
        return (lowest < REF_FLOOR).astype(jnp.int32)

    lax.while_loop(lambda retry: retry > 0, one_pass, jnp.int32(1))

    lam = (jnp.exp(jnp.sum(lq1_ref[...] * lk1_ref[...], axis=-1, keepdims=True))
           - jnp.exp(jnp.sum(lq2_ref[...] * lk2_ref[...], axis=-1, keepdims=True)) + lambda_init)
    den1 = jnp.sum(den_ref[0], axis=0, keepdims=True)
    den2 = jnp.sum(den_ref[1], axis=0, keepdims=True)
    o = acc_ref[0] / den1 - lam * (acc_ref[1] / den2)
    r = lax.rsqrt(jnp.mean(o * o, axis=0, keepdims=True) + NORM_EPS)
    y = (o * r * subw_ref[...]) * (1.0 - lambda_init)
    o_ref[...] = (y.T * jax.nn.silu(g_ref[...])).astype(o_ref.dtype)


def _diff_attn(qt, ka, vt, lq1, lk1, lq2, lk2, subw, g, lambda_init):
    s = qt.shape[-1]
    g_col0 = (g.shape[1] - DIFF_WIDTH) // HEAD_V
    nchunk = s // ATT_TK
    unroll = min(ATT_UNROLL, nchunk // 2)
    assert unroll % 2 == 0 and nchunk % unroll == 0 and unroll >= ATT_DIAG and s % ATT_TQ == 0
    whole = jnp.asarray([LOG2E * 2.0 ** (-8.0 * (h + 1) / HEADS) for h in range(HEADS)], F32)
    pieces, rest = [], whole
    for _ in range(COEF_PARTS):
        pieces.append(rest.astype(BF16).astype(F32))
        rest = rest - pieces[-1]
    slopes = jnp.stack([whole] + pieces, axis=1).reshape(-1)
    vec = pl.BlockSpec((1, HEAD_DIM), lambda h, i: (0, 0))
    return pl.pallas_call(
        functools.partial(_attn_kernel, lambda_init=lambda_init, nchunk=nchunk, unroll=unroll),
        grid=(HEADS, s // ATT_TQ),
        in_specs=[pl.BlockSpec(memory_space=pltpu.SMEM),
                  pl.BlockSpec((1, 2, LANES, ATT_TQ), lambda h, i: (h, 0, 0, i)),
                  pl.BlockSpec((1, 2, nchunk, ATT_TK, LANES), lambda h, i: (h, 0, 0, 0, 0)),
                  pl.BlockSpec((1, nchunk, HEAD_V, ATT_TK), lambda h, i: (h, 0, 0, 0)),
                  vec, vec, vec, vec,
                  pl.BlockSpec((HEAD_V, 1), lambda h, i: (0, 0)),
                  pl.BlockSpec((ATT_TQ, HEAD_V), lambda h, i: (i, g_col0 + h))],
        out_specs=pl.BlockSpec((ATT_TQ, HEAD_V), lambda h, i: (i, h)),
        out_shape=jax.ShapeDtypeStruct((s, DIFF_WIDTH), BF16),
        scratch_shapes=[pltpu.VMEM((2, HEAD_V, ATT_TQ), F32), pltpu.VMEM((2, 1, ATT_TQ), F32),
                        pltpu.VMEM((3, 2, LANES, ATT_TQ), BF16), pltpu.SMEM((2,), F32),
                        pltpu.VMEM((2, ATT_TK, ATT_TQ), BF16), pltpu.VMEM((2, ATT_TK, ATT_TQ), BF16),
                        pltpu.VMEM((2, SUBLANES, ATT_TQ), F32)],
        compiler_params=_params(2),
        name="diff_attn",
    )(slopes, qt, ka, vt, lq1, lk1, lq2, lk2, subw.reshape(HEAD_V, 1), g)


def _pool_kernel(prev_ref, cur_ref, next_ref, w_ref, scale_ref, g_ref, o_ref, ext_ref, *, tm, seq):
    i = pl.program_id(0)
    last = pl.num_programs(0) - 1
    zero_halo = jnp.zeros((POOL_HALO, POOL_WIDTH), F32)
    ext_ref[0:POOL_HALO] = jnp.where(i > 0, prev_ref[...], zero_halo)
    ext_ref[POOL_HALO:POOL_HALO + tm] = cur_ref[...]
    ext_ref[POOL_HALO + tm:POOL_HALO + tm + POOL_HALO] = jnp.where(i < last, next_ref[...], zero_halo)
    t = lax.broadcasted_iota(jnp.int32, (tm, 1), 0) + i * tm
    for g, w in enumerate(POOL_WINDOWS):
        cols = pl.ds(g * POOL_GROUP, POOL_GROUP)
        span = tm + 2 * POOL_HALO
        width = 1
        while width < w:
            span -= width
            ext_ref[pl.ds(0, span), cols] = ext_ref[pl.ds(0, span), cols] + ext_ref[pl.ds(width, span), cols]
            width *= 2
        assert width == w
        win = ext_ref[pl.ds(POOL_HALO - w // 2, tm), cols]
        lo = jnp.maximum(t - w // 2, 0)
        hi = jnp.minimum(t + (w - w // 2) - 1, seq - 1)
        cnt = (hi - lo + 1).astype(F32)
        pooled = win * (1.0 / cnt) - cur_ref[:, cols]
        y = jnp.dot(pooled.astype(BF16), w_ref[g], preferred_element_type=F32)
        o_ref[:, cols] = (y * scale_ref[:, cols] * jax.nn.silu(g_ref[:, cols])).astype(o_ref.dtype)


def _pool_mix(ug, w_pool, pool_scale):
    s = ug.shape[0]
    width = POOL_WIDTH
    tm = 512
    hb = tm // POOL_HALO
    nhalo = s // POOL_HALO
    return pl.pallas_call(
        functools.partial(_pool_kernel, tm=tm, seq=s),
        grid=(s // tm,),
        in_specs=[pl.BlockSpec((POOL_HALO, width), lambda i: (jnp.maximum(i * hb - 1, 0), 0)),
                  pl.BlockSpec((tm, width), lambda i: (i, 0)),
                  pl.BlockSpec((POOL_HALO, width), lambda i: (jnp.minimum((i + 1) * hb, nhalo - 1), 0)),
                  pl.BlockSpec(w_pool.shape, lambda i: (0, 0, 0)),
                  pl.BlockSpec((1, width), lambda i: (0, 0)),
                  pl.BlockSpec((tm, width), lambda i: (i, 1))],
        out_specs=pl.BlockSpec((tm, width), lambda i: (i, 0)),
        out_shape=jax.ShapeDtypeStruct((s, width), BF16),
        scratch_shapes=[pltpu.VMEM((tm + 2 * POOL_HALO, width), F32)],
        compiler_params=_params(1),
        name="pool_mix",
    )(ug, ug, ug, w_pool, pool_scale, ug)


def _out_kernel(*refs, widths, tm, modulate_next, gate_proj):
    n = len(widths)
    y_refs = refs[:n]
    refs = refs[n:]
    if gate_proj:
        hin_ref, wg_ref = refs[:2]
        refs = refs[2:]
    w_ref, x_ref, gate_ref = refs[:3]
    if modulate_next:
        nw_ref, shift_ref, scale_ref, o_ref, h_ref = refs[3:]
        mul = nw_ref[...]
        one_plus_scale = 1.0 + scale_ref[...]
    else:
        (o_ref,) = refs[3:]
    for sb in range(tm // OUT_SUB):
        rows = slice(sb * OUT_SUB, (sb + 1) * OUT_SUB)
        acc = None
        off = 0
        for y_ref, wd in zip(y_refs, widths):
            y = y_ref[rows, :]
            if gate_proj:
                y = y * jax.nn.silu(jnp.dot(hin_ref[rows, :], wg_ref[:, off:off + wd], preferred_element_type=F32))
            part = jnp.dot(y.astype(BF16), w_ref[off:off + wd, :], preferred_element_type=F32)
            acc = part if acc is None else acc + part
            off += wd
        xn = x_ref[rows, :] + gate_ref[...] * acc
        o_ref[rows, :] = xn
        if modulate_next:
            r = lax.rsqrt(jnp.mean(xn * xn, axis=-1, keepdims=True) + NORM_EPS)
            h_ref[rows, :] = ((xn * r * mul) * one_plus_scale + shift_ref[...]).astype(BF16)


def _out_proj(ys, w, layer, x, gate, next_mod=None, gate_from=None):
    s, d = x.shape
    k = w.shape[1]
    gate_proj = gate_from is not None
    tm = min(256 if gate_proj else 512, s)
    widths = tuple(y.shape[1] for y in ys)
    vec = pl.BlockSpec((1, d), lambda i: (0, 0))
    rows = pl.BlockSpec((tm, d), lambda i: (i, 0))
    modulate_next = next_mod is not None
    gate_specs, gate_args = [], ()
    if gate_proj:
        hin, wg, col0 = gate_from
        assert col0 % k == 0 and sum(widths) == k
        gate_specs = [pl.BlockSpec((tm, hin.shape[1]), lambda i: (i, 0)),
                      pl.BlockSpec((None, wg.shape[1], k), lambda i: (layer, 0, col0 // k))]
        gate_args = (hin, wg)
    return pl.pallas_call(
        functools.partial(_out_kernel, widths=widths, tm=tm, modulate_next=modulate_next, gate_proj=gate_proj),
        grid=(s // tm,),
        in_specs=[pl.BlockSpec((tm, wd), lambda i: (i, 0)) for wd in widths] + gate_specs
        + [pl.BlockSpec((None, k, d), lambda i: (layer, 0, 0)), rows, vec] + ([vec, vec, vec] if modulate_next else []),
        out_specs=[rows, rows] if modulate_next else rows,
        out_shape=([jax.ShapeDtypeStruct((s, d), F32), jax.ShapeDtypeStruct((s, d), BF16)] if modulate_next
                   else jax.ShapeDtypeStruct((s, d), F32)),
        compiler_params=_params(1),
        name="out_proj",
    )(*ys, *gate_args, w, x, gate, *(next_mod if modulate_next else ()))


def _dft_split(s):
    n1 = 1 << (int(math.log2(s)) // 2)
    n2 = s // n1
    assert n1 * n2 == s and n1 % SUBLANES == 0 and n2 % SUBLANES == 0
    return n1, n2


def _angles(num, den):
    ang = (2.0 * math.pi / den) * (num % den)
    return np.cos(ang), np.sin(ang)


def _fold_cs_kernel(cs_ref, wf_ref, o_ref):
    wf = wf_ref[...]
    o_ref[:, 0:FOURIER_GROUP] = jnp.dot(cs_ref[0], wf, preferred_element_type=F32,
                                        precision=lax.Precision.HIGHEST)
    o_ref[:, FOURIER_GROUP:] = jnp.dot(cs_ref[1], wf, preferred_element_type=F32,
                                       precision=lax.Precision.HIGHEST)


def _fold_channel_dft(w_fourier_j, seq):
    fg = FOURIER_GROUP
    idx = np.arange(fg)
    cc, sc = _angles(idx[:, None] * idx[None, :], fg)
    cs = jnp.asarray(np.stack([cc, sc]) / math.sqrt(seq * fg), F32)
    return pl.pallas_call(
        _fold_cs_kernel,
        grid=(FOURIER_GROUPS,),
        in_specs=[pl.BlockSpec((2, fg, fg), lambda g: (0, 0, 0)),
                  pl.BlockSpec((None, fg, fg), lambda g: (g, 0, 0))],
        out_specs=pl.BlockSpec((None, fg, 2 * fg), lambda g: (g, 0, 0)),
        out_shape=jax.ShapeDtypeStruct((FOURIER_GROUPS, fg, 2 * fg), F32),
        compiler_params=_params(1),
        name="fold_channel_dft",
    )(cs, w_fourier_j)


def _fold_win_kernel(w_ref, ab_ref, o_ref):
    o_ref[...] = jnp.dot(w_ref[...], ab_ref[...].astype(BF16), preferred_element_type=F32).astype(BF16)


def _fold_in_proj(w_in, layer, ab):
    d = w_in.shape[1]
    fg = FOURIER_GROUP
    return pl.pallas_call(
        _fold_win_kernel,
        grid=(FOURIER_GROUPS, 2),
        in_specs=[pl.BlockSpec((None, d, fg), lambda g, t: (layer, 0, g)),
                  pl.BlockSpec((None, fg, fg), lambda g, t: (g, 0, t))],
        out_specs=pl.BlockSpec((d, fg), lambda g, t: (0, g + t * FOURIER_GROUPS)),
        out_shape=jax.ShapeDtypeStruct((d, 2 * D_MODEL), BF16),
        compiler_params=_params(2),
        name="fold_in_proj",
    )(w_in, ab)


HIGH_HALF = -65536


def _bf16_bits(v):
    b = lax.bitcast_convert_type(v, jnp.int32)
    return b + 0x7FFF + (lax.shift_right_logical(b, 16) & 1)


def _dft1_kernel(f_ref, p_ref, q_ref, t_ref, *, nb, n1):
    f = f_ref[...]
    for jj in range(nb):
        rows = slice(jj * n1, (jj + 1) * n1)
        rhs = jnp.concatenate([p_ref[rows, :], q_ref[rows, :]], axis=0)
        t = jnp.dot(f, rhs, preferred_element_type=F32)
        t_ref[jj] = (_bf16_bits(t[0:n1]) & HIGH_HALF) | lax.shift_right_logical(_bf16_bits(t[n1:2 * n1]), 16)


def _dft_stage1(pq_t, f1, n1, n2):
    s, c2 = pq_t.shape
    c = c2 // 2
    nb = SUBLANES
    cb = min(1024, c)
    ncb = c // cb
    return pl.pallas_call(
        functools.partial(_dft1_kernel, nb=nb, n1=n1),
        grid=(n2 // nb, ncb),
        in_specs=[pl.BlockSpec((2 * n1, 2 * n1), lambda b, j: (0, 0)),
                  pl.BlockSpec((nb * n1, cb), lambda b, j: (b, j)),
                  pl.BlockSpec((nb * n1, cb), lambda b, j: (b, j + ncb))],
        out_specs=pl.BlockSpec((nb, n1, cb), lambda b, j: (b, 0, j)),
        out_shape=jax.ShapeDtypeStruct((n2, n1, c), jnp.int32),
        compiler_params=_params(2),
        name="dft_stage1",
    )(f1, pq_t, pq_t)


def _dft2_kernel(l_ref, t_ref, o_ref, *, n2, cb):
    packed = t_ref[...]
    re = lax.bitcast_convert_type(packed & HIGH_HALF, F32)
    im = lax.bitcast_convert_type(lax.shift_left(packed, 16), F32)
    rhs = jnp.stack([re, im], axis=1).reshape(n2 * 2 * SUBLANES, cb).astype(BF16)
    o_ref[...] = jnp.dot(l_ref[0], rhs, preferred_element_type=F32).reshape(n2, SUBLANES, cb)


def _dft_stage2(t_nk, l2, n1, n2):
    c = t_nk.shape[-1]
    cb = min(1024, c)
    row_blocks = pl.BlockSpec((n2, SUBLANES, cb), lambda g, j: (0, g, j))
    out = pl.pallas_call(
        functools.partial(_dft2_kernel, n2=n2, cb=cb),
        grid=(n1 // SUBLANES, c // cb),
        in_specs=[pl.BlockSpec((1, SUBLANES * n2, SUBLANES * 2 * n2), lambda g, j: (g, 0, 0)),
                  row_blocks],
        out_specs=row_blocks,
        out_shape=jax.ShapeDtypeStruct((n2, n1, c), F32),
        compiler_params=_params(2),
        name="dft_stage2",
    )(l2, t_nk)
    return out.reshape(n1 * n2, c)


def _dft_matrices(s):
    n1, n2 = _dft_split(s)
    a = np.arange(n1)
    c1, s1 = _angles(a[:, None] * a[None, :], n1)
    f1 = jnp.asarray(np.block([[c1, -s1], [-s1, -c1]]).astype(BF16))
    groups = n1 // SUBLANES
    g = np.arange(groups)[:, None, None, None]
    k2 = np.arange(n2)[None, :, None, None]
    j = np.arange(SUBLANES)[None, None, :, None]
    nn = np.arange(n2)[None, None, None, :]
    c2, s2 = _angles((SUBLANES * g + j + n1 * k2) * nn, s)
    cs = np.concatenate([c2, s2], axis=3).astype(BF16)
    return n1, n2, f1, _expand_twiddles(jnp.asarray(cs.reshape(groups, n2 * SUBLANES, 2 * n2)))


def _expand_kernel(cs_ref, e_ref, o_ref):
    spread = jnp.dot(cs_ref[0], e_ref[...], preferred_element_type=F32)
    row = lax.broadcasted_iota(jnp.int32, spread.shape, 0)
    col = lax.broadcasted_iota(jnp.int32, spread.shape, 1)
    o_ref[0] = jnp.where(row % SUBLANES == col % SUBLANES, spread, 0.0).astype(BF16)


def _expand_twiddles(cs):
    groups, rows, cols = cs.shape
    wide = cols * SUBLANES
    src = jnp.arange(cols, dtype=jnp.int32)[:, None]
    slot = (src % (cols // 2)) * 2 + src // (cols // 2)
    e = jnp.arange(wide, dtype=jnp.int32)[None, :] // SUBLANES == slot
    return pl.pallas_call(
        _expand_kernel,
        grid=(groups,),
        in_specs=[pl.BlockSpec((1, rows, cols), lambda g: (g, 0, 0)),
                  pl.BlockSpec((cols, wide), lambda g: (0, 0))],
        out_specs=pl.BlockSpec((1, rows, wide), lambda g: (g, 0, 0)),
        out_shape=jax.ShapeDtypeStruct((groups, rows, wide), BF16),
        compiler_params=_params(1),
        name="expand_twiddles",
    )(cs, e.astype(BF16))


def kernel(x, c, norm_w, ada_w, ada_b, w_in_ab, w_pool, pool_scale, q_norm_w, k_norm_w, lambda_q1, lambda_k1,
           lambda_q2, lambda_k2, subln_w, w_out_ab, w_in_c, w_fourier, w_out_c):
    batch, s, d = x.shape
    assert batch == 1 and d == D_MODEL and s % ATT_TK == 0
    xs = x.reshape(s, d)
    mod = _ada_mod(c, ada_w, ada_b)
    n1, n2, f1, l2 = _dft_matrices(s)
    row = lambda v: v.reshape(1, -1)

    def mod_of(i):
        return mod[i, :, 0:d], mod[i, :, d:2 * d], mod[i, :, 2 * d:3 * d]

    def finish(ys, w_out, j, xs, i, gate_from=None):
        gate = mod_of(i)[2]
        if i + 1 == DEPTH:
            return _out_proj(ys, w_out, j, xs, gate, gate_from=gate_from), None
        shift, scale, _ = mod_of(i + 1)
        return _out_proj(ys, w_out, j, xs, gate, (row(norm_w[i + 1]), shift, scale), gate_from=gate_from)

    w_in_ab, w_out_ab, w_in_c, w_out_c = (w.astype(BF16) for w in (w_in_ab, w_out_ab, w_in_c, w_out_c))
    h = _modulate(xs, row(norm_w[0]), *mod_of(0)[:2])
    for i in range(DEPTH):
        j = i // 2
        if i % 2 == 0:
            o1, o2, o3, o4 = POOL_WIDTH, POOL_WIDTH + DIFF_WIDTH, POOL_WIDTH + 2 * DIFF_WIDTH, POOL_WIDTH + 3 * DIFF_WIDTH
            lambda_init = 0.8 - 0.6 * math.exp(-0.3 * i)
            ug = _matmul(h, w_in_ab, F32, PROJ_TM, PROJ_TN, name="pool_gate_in_proj",
                         cols=((0, o1), (o4, o4 + AB_WIDTH)), layer=j)
            qt, ka, vt = _qkv_proj(h, w_in_ab, j, o1, o2, o3, row(q_norm_w[j]), row(k_norm_w[j]))
            y_a = _pool_mix(ug, w_pool[j].astype(BF16), row(pool_scale[j]))
            y_b = _diff_attn(qt, ka, vt, row(lambda_q1[j]), row(lambda_k1[j]), row(lambda_q2[j]),
                             row(lambda_k2[j]), subln_w[j], ug, lambda_init)
            xs, h = finish([y_a, y_b], w_out_ab, j, xs, i)
        else:
            ab = _fold_channel_dft(w_fourier[j], s)
            w_pq = _fold_in_proj(w_in_c, j, ab)
            h_t = h.reshape(n1, n2, d).transpose(1, 0, 2).reshape(s, d)
            pq_t = _matmul(h_t, w_pq, BF16, PROJ_TM, PROJ_TN, name="fourier_in_proj")
            t_nk = _dft_stage1(pq_t, f1, n1, n2)
            f = _dft_stage2(t_nk, l2, n1, n2)
            xs, h = finish([f], w_out_c, j, xs, i, gate_from=(h, w_in_c, d))
    return xs.reshape(batch, s, d)
```

```python
import functools
import math

import jax
import jax.numpy as jnp
import numpy as np
from jax import lax
from jax.experimental import pallas as pl
from jax.experimental.pallas import tpu as pltpu

F32 = jnp.float32
BF16 = jnp.bfloat16

D_MODEL = 2048
DEPTH = 4
NORM_EPS = 1e-6

POOL_WINDOWS = (2, 4, 8, 16)
POOL_WIDTH = D_MODEL // 2
POOL_GROUP = POOL_WIDTH // len(POOL_WINDOWS)
POOL_HALO = 8

HEADS = 8
HEAD_DIM = 64
HEAD_V = 2 * HEAD_DIM
DIFF_WIDTH = HEADS * HEAD_V
AB_WIDTH = POOL_WIDTH + DIFF_WIDTH

FOURIER_GROUPS = 4
FOURIER_GROUP = D_MODEL // FOURIER_GROUPS

LANES = 128
SUBLANES = 8
POS_RADIX = 256
COEF_PARTS = 3
LOG2E = math.log2(math.e)

ATT_TQ = 512
ATT_TK = 512
ATT_DIAG = ATT_TQ // ATT_TK
ATT_UNROLL = 8
REF_MARGIN = 60.0
REF_FLOOR = 2.0 ** -60
BOUND_SLACK = 1.001

PROJ_TM, PROJ_TN = 2048, 1024
OUT_SUB = 256
DFT_CB = 2048

VMEM_LIMIT = 56 * 1024 * 1024


def _params(n_axes):
    return pltpu.CompilerParams(dimension_semantics=("arbitrary",) * n_axes,
                                vmem_limit_bytes=VMEM_LIMIT)


def _mod_kernel(c_ref, w_ref, b_ref, o_ref):
    c = c_ref[...]
    o_ref[...] = jnp.sum(jax.nn.silu(c) * w_ref[...], axis=0, keepdims=True) + b_ref[...]


def _ada_mod(c, ada_w, ada_b):
    depth, d, n = ada_w.shape
    tn = 1024
    return pl.pallas_call(
        _mod_kernel,
        grid=(depth, n // tn),
        in_specs=[pl.BlockSpec((d, 1), lambda i, j: (0, 0)),
                  pl.BlockSpec((None, d, tn), lambda i, j: (i, 0, j)),
                  pl.BlockSpec((None, 1, tn), lambda i, j: (i, 0, j))],
        out_specs=pl.BlockSpec((None, 1, tn), lambda i, j: (i, 0, j)),
        out_shape=jax.ShapeDtypeStruct((depth, 1, n), F32),
        compiler_params=_params(2),
        name="ada_mod",
    )(c.reshape(d, 1), ada_w, ada_b.reshape(depth, 1, n))


def _modulate_kernel(x_ref, nw_ref, shift_ref, scale_ref, o_ref):
    xf = x_ref[...]
    r = lax.rsqrt(jnp.mean(xf * xf, axis=-1, keepdims=True) + NORM_EPS)
    y = xf * r * nw_ref[...]
    y = y * (1.0 + scale_ref[...]) + shift_ref[...]
    o_ref[...] = y.astype(o_ref.dtype)


def _modulate(x, nw, shift, scale):
    s, d = x.shape
    tm = min(1024, s)
    vec = pl.BlockSpec((1, d), lambda i: (0, 0))
    return pl.pallas_call(
        _modulate_kernel,
        grid=(s // tm,),
        in_specs=[pl.BlockSpec((tm, d), lambda i: (i, 0)), vec, vec, vec],
        out_specs=pl.BlockSpec((tm, d), lambda i: (i, 0)),
        out_shape=jax.ShapeDtypeStruct((s, d), BF16),
        compiler_params=_params(1),
        name="modulate",
    )(x, nw, shift, scale)


def _mm_kernel(a_ref, w_ref, o_ref, *, precision):
    o_ref[...] = jnp.dot(a_ref[...], w_ref[...], preferred_element_type=F32,
                         precision=precision).astype(o_ref.dtype)


def _matmul(a, w, out_dtype, tm, tn, precision=None, name="matmul", cols=None, layer=None):
    m, k = a.shape
    cols = cols if cols is not None else ((0, w.shape[-1]),)
    n = sum(stop - start for start, stop in cols)
    tm, tn = min(tm, m), min(tn, n)
    assert all(start % tn == 0 and stop % tn == 0 for start, stop in cols)

    def wblock(j):
        jb, first = j, 0
        for start, stop in cols:
            jb = jnp.where(j >= first, j - first + start // tn, jb)
            first += (stop - start) // tn
        return jb

    if layer is None:
        wspec = pl.BlockSpec((k, tn), lambda i, j: (0, wblock(j)))
    else:
        wspec = pl.BlockSpec((None, k, tn), lambda i, j: (layer, 0, wblock(j)))
    return pl.pallas_call(
        functools.partial(_mm_kernel, precision=precision),
        grid=(m // tm, n // tn),
        in_specs=[pl.BlockSpec((tm, k), lambda i, j: (i, 0)), wspec],
        out_specs=pl.BlockSpec((tm, tn), lambda i, j: (i, j)),
        out_shape=jax.ShapeDtypeStruct((m, n), out_dtype),
        compiler_params=_params(2),
        name=name,
    )(a, w)


def _qkv_kernel(h_ref, wq_ref, wk_ref, wv_ref, qnw_ref, knw_ref, qt_ref, ka_ref, vt_ref, *, tm):
    lane = lax.broadcasted_iota(jnp.int32, (ATT_TK, LANES), 1)
    first = lane < HEAD_DIM
    pos = lax.broadcasted_iota(jnp.int32, (ATT_TK, LANES), 0)
    pos_lo = (pos % POS_RADIX).astype(F32)
    pos_hi = (pos // POS_RADIX).astype(F32)
    pos_cols = jnp.where(lane < HEAD_DIM + COEF_PARTS, pos_lo, jnp.where(lane < HEAD_DIM + 2 * COEF_PARTS, pos_hi, 0.0))

    def halves_rms(z, w):
        sq = z * z
        ss1 = jnp.sum(jnp.where(first, sq, 0.0), axis=-1, keepdims=True)
        ss2 = jnp.sum(jnp.where(first, 0.0, sq), axis=-1, keepdims=True)
        r = jnp.where(first, lax.rsqrt(ss1 / HEAD_DIM + NORM_EPS), lax.rsqrt(ss2 / HEAD_DIM + NORM_EPS))
        return z * r * w

    for cc in range(tm // ATT_TK):
        rows = slice(cc * ATT_TK, (cc + 1) * ATT_TK)
        h = h_ref[rows, :]
        zq = jnp.dot(h, wq_ref[...], preferred_element_type=F32)
        zk = jnp.dot(h, wk_ref[...], preferred_element_type=F32)
        zv = jnp.dot(h, wv_ref[...], preferred_element_type=F32)
        for hh in range(2):
            cols = slice(hh * LANES, (hh + 1) * LANES)
            qn = halves_rms(zq[:, cols], qnw_ref[...]) * (HEAD_DIM ** -0.5) * LOG2E
            kn = halves_rms(zk[:, cols], knw_ref[...])
            for m in range(2):
                qm = qn if m == 0 else pltpu.roll(qn, HEAD_DIM, axis=1)
                km = kn if m == 0 else pltpu.roll(kn, HEAD_DIM, axis=1)
                qt_ref[hh, m, :, rows] = jnp.where(first, qm, 0.0).T.astype(BF16)
                ka_ref[hh, m, cc] = jnp.where(first, km, pos_cols).astype(BF16)
            vt_ref[hh, cc] = zv[:, cols].T.astype(BF16)


def _qkv_proj(h, w, layer, col_q, col_k, col_v, qnw, knw):
    s, d = h.shape
    tm = min(2048, s)
    nchunk = s // ATT_TK
    cpt = tm // ATT_TK
    tn = 2 * LANES
    assert col_q % tn == 0 and col_k % tn == 0 and col_v % tn == 0
    wspecs = [pl.BlockSpec((None, d, tn), functools.partial(lambda i, j, jb: (layer, 0, j + jb), jb=c0 // tn))
              for c0 in (col_q, col_k, col_v)]
    vec = pl.BlockSpec((1, LANES), lambda i, j: (0, 0))
    return pl.pallas_call(
        functools.partial(_qkv_kernel, tm=tm),
        grid=(s // tm, HEADS // 2),
        in_specs=[pl.BlockSpec((tm, d), lambda i, j: (i, 0)), *wspecs, vec, vec],
        out_specs=[pl.BlockSpec((2, 2, LANES, tm), lambda i, j: (j, 0, 0, i)),
                   pl.BlockSpec((2, 2, cpt, ATT_TK, LANES), lambda i, j: (j, 0, i, 0, 0)),
                   pl.BlockSpec((2, cpt, HEAD_V, ATT_TK), lambda i, j: (j, i, 0, 0))],
        out_shape=[jax.ShapeDtypeStruct((HEADS, 2, LANES, s), BF16),
                   jax.ShapeDtypeStruct((HEADS, 2, nchunk, ATT_TK, LANES), BF16),
                   jax.ShapeDtypeStruct((HEADS, nchunk, HEAD_V, ATT_TK), BF16)],
        compiler_params=_params(2),
        name="qkv_proj",
    )(h, w, w, w, jnp.tile(qnw, (1, 2)), jnp.tile(knw, (1, 2)))


def _attn_kernel(slopes_ref, qt_ref, ka_ref, vt_ref, lq1_ref, lk1_ref, lq2_ref, lk2_ref, subw_ref, g_ref,
                 o_ref, acc_ref, mu_ref, qv_ref, kmax_ref, p_a, p_b, den_ref, *, lambda_init, nchunk, unroll):
    hd = pl.program_id(0)
    i = pl.program_id(1)
    base = hd * (1 + COEF_PARTS)
    slope = slopes_ref[base]
    q_off = lax.broadcasted_iota(jnp.int32, (1, ATT_TQ), 1).astype(F32)

    row1 = lax.broadcasted_iota(jnp.int32, (LANES - HEAD_DIM, 1), 0)
    coef_col = jnp.zeros((LANES - HEAD_DIM, 1), F32)
    for t in range(COEF_PARTS):
        piece = slopes_ref[base + 1 + t]
        coef_col = jnp.where(row1 == t, piece, coef_col)
        coef_col = jnp.where(row1 == COEF_PARTS + t, piece * POS_RADIX, coef_col)
    coef_rows = jnp.broadcast_to(coef_col, (LANES - HEAD_DIM, ATT_TQ))
    for m in range(2):
        for kind, sign in ((0, -1.0), (1, 0.0), (2, 1.0)):
            qv_ref[kind, m, 0:HEAD_DIM, :] = qt_ref[0, m, 0:HEAD_DIM, :]
            qv_ref[kind, m, HEAD_DIM:LANES, :] = (sign * coef_rows).astype(BF16)

    @pl.when(i == 0)
    def _():
        lane = lax.broadcasted_iota(jnp.int32, (ATT_TK, LANES), 1)
        for m in range(2):
            def widest(c, best):
                kc = ka_ref[0, m, c].astype(F32)
                return jnp.maximum(best, jnp.sum(jnp.where(lane < HEAD_DIM, kc * kc, 0.0), axis=1, keepdims=True))
            kmax_ref[m] = jnp.max(lax.fori_loop(0, nchunk, widest, jnp.zeros((ATT_TK, 1), F32)))

    for m in range(2):
        qf = qt_ref[0, m, 0:HEAD_DIM, :].astype(F32)
        bound = jnp.sqrt(jnp.sum(qf * qf, axis=0, keepdims=True) * kmax_ref[m]) * BOUND_SLACK
        mu_ref[m] = bound - REF_MARGIN

    c_diag = i * ATT_DIAG

    def chunk_of(e):
        if isinstance(e, int) and e < ATT_DIAG:
            return c_diag + e
        rest = e - ATT_DIAG
        return jnp.where(e < ATT_DIAG, c_diag + e, rest + ATT_DIAG * (rest >= c_diag).astype(jnp.int32))

    def side_of(c):
        return (c < c_diag).astype(jnp.int32) - (c >= c_diag + ATT_DIAG).astype(jnp.int32)

    p_bufs = (p_a, p_b)

    def probs(e, p_ref):
        c = chunk_of(e)
        side = side_of(c)
        sidef = side.astype(F32)
        gap = jnp.abs(c * ATT_TK - i * ATT_TQ).astype(F32)
        cvec = -slope * (sidef * sidef * gap + sidef * q_off)
        for m in range(2):
            st = jnp.dot(ka_ref[0, m, c], qv_ref[1 + side, m], preferred_element_type=F32)
            if isinstance(e, int) and e < ATT_DIAG:
                key_off = lax.broadcasted_iota(jnp.int32, (ATT_TK, ATT_TQ), 0) + e * ATT_TK
                qry_off = lax.broadcasted_iota(jnp.int32, (ATT_TK, ATT_TQ), 1)
                st = st - slope * jnp.abs(key_off - qry_off).astype(F32)
            p = jnp.exp2(st - (mu_ref[m] - cvec))
            p_ref[m] = p.astype(BF16)
            den_ref[m] += jnp.sum(p.reshape(ATT_TK // SUBLANES, SUBLANES, ATT_TQ), axis=0)

    def group(e0, count, probs_last):
        sums = [None, None]
        for u in range(count):
            if u < count - 1 or probs_last:
                probs(e0 + u + 1, p_bufs[(u + 1) % 2])
            c = chunk_of(e0 + u)
            for m in range(2):
                pv = jnp.dot(vt_ref[0, c], p_bufs[u % 2][m], preferred_element_type=F32)
                sums[m] = pv if sums[m] is None else sums[m] + pv
        for m in range(2):
            acc_ref[m] += sums[m]

    def loop_body(it, carry):
        group(it * unroll, unroll, True)
        return carry

    def one_pass(_):
        acc_ref[...] = jnp.zeros(acc_ref.shape, F32)
        den_ref[...] = jnp.zeros(den_ref.shape, F32)
        n_groups = nchunk // unroll
        probs(0, p_a)
        group(0, unroll, True)
        lax.fori_loop(1, n_groups - 1, loop_body, 0)
        group((n_groups - 1) * unroll, unroll, False)
        lowest = None
        for m in range(2):
            den = jnp.sum(den_ref[m], axis=0, keepdims=True)
            mu_ref[m] = jnp.where(den < REF_FLOOR, mu_ref[m] - 2.0 * REF_MARGIN, mu_ref[m])
            lowest = jnp.min(den) if lowest is None else jnp.minimum(lowest, jnp.min(den))
        return (lowest < REF_FLOOR).astype(jnp.int32)

    lax.while_loop(lambda retry: retry > 0, one_pass, jnp.int32(1))

    lam = (jnp.exp(jnp.sum(lq1_ref[...] * lk1_ref[...], axis=-1, keepdims=True))
           - jnp.exp(jnp.sum(lq2_ref[...] * lk2_ref[...], axis=-1, keepdims=True)) + lambda_init)
    den1 = jnp.sum(den_ref[0], axis=0, keepdims=True)
    den2 = jnp.sum(den_ref[1], axis=0, keepdims=True)
    o = acc_ref[0] / den1 - lam * (acc_ref[1] / den2)
    r = lax.rsqrt(jnp.mean(o * o, axis=0, keepdims=True) + NORM_EPS)
    y = (o * r * subw_ref[...]) * (1.0 - lambda_init)
    o_ref[...] = (y.T * jax.nn.silu(g_ref[...])).astype(o_ref.dtype)


def _diff_attn(qt, ka, vt, lq1, lk1, lq2, lk2, subw, g, lambda_init):
    s = qt.shape[-1]
    g_col0 = (g.shape[1] - DIFF_WIDTH) // HEAD_V
    nchunk = s // ATT_TK
    unroll = min(ATT_UNROLL, nchunk // 2)
    assert unroll % 2 == 0 and nchunk % unroll == 0 and unroll >= ATT_DIAG and s % ATT_TQ == 0
    whole = jnp.asarray([LOG2E * 2.0 ** (-8.0 * (h + 1) / HEADS) for h in range(HEADS)], F32)
    pieces, rest = [], whole
    for _ in range(COEF_PARTS):
        pieces.append(rest.astype(BF16).astype(F32))
        rest = rest - pieces[-1]
    slopes = jnp.stack([whole] + pieces, axis=1).reshape(-1)
    vec = pl.BlockSpec((1, HEAD_DIM), lambda h, i: (0, 0))
    return pl.pallas_call(
        functools.partial(_attn_kernel, lambda_init=lambda_init, nchunk=nchunk, unroll=unroll),
        grid=(HEADS, s // ATT_TQ),
        in_specs=[pl.BlockSpec(memory_space=pltpu.SMEM),
                  pl.BlockSpec((1, 2, LANES, ATT_TQ), lambda h, i: (h, 0, 0, i)),
                  pl.BlockSpec((1, 2, nchunk, ATT_TK, LANES), lambda h, i: (h, 0, 0, 0, 0)),
                  pl.BlockSpec((1, nchunk, HEAD_V, ATT_TK), lambda h, i: (h, 0, 0, 0)),
                  vec, vec, vec, vec,
                  pl.BlockSpec((HEAD_V, 1), lambda h, i: (0, 0)),
                  pl.BlockSpec((ATT_TQ, HEAD_V), lambda h, i: (i, g_col0 + h))],
        out_specs=pl.BlockSpec((ATT_TQ, HEAD_V), lambda h, i: (i, h)),
        out_shape=jax.ShapeDtypeStruct((s, DIFF_WIDTH), BF16),
        scratch_shapes=[pltpu.VMEM((2, HEAD_V, ATT_TQ), F32), pltpu.VMEM((2, 1, ATT_TQ), F32),
                        pltpu.VMEM((3, 2, LANES, ATT_TQ), BF16), pltpu.SMEM((2,), F32),
                        pltpu.VMEM((2, ATT_TK, ATT_TQ), BF16), pltpu.VMEM((2, ATT_TK, ATT_TQ), BF16),
                        pltpu.VMEM((2, SUBLANES, ATT_TQ), F32)],
        compiler_params=_params(2),
        name="diff_attn",
    )(slopes, qt, ka, vt, lq1, lk1, lq2, lk2, subw.reshape(HEAD_V, 1), g)


def _pool_kernel(prev_ref, cur_ref, next_ref, w_ref, scale_ref, g_ref, o_ref, ext_ref, *, tm, seq):
    i = pl.program_id(0)
    last = pl.num_programs(0) - 1
    zero_halo = jnp.zeros((POOL_HALO, POOL_WIDTH), F32)
    ext_ref[0:POOL_HALO] = jnp.where(i > 0, prev_ref[...], zero_halo)
    ext_ref[POOL_HALO:POOL_HALO + tm] = cur_ref[...]
    ext_ref[POOL_HALO + tm:POOL_HALO + tm + POOL_HALO] = jnp.where(i < last, next_ref[...], zero_halo)
    t = lax.broadcasted_iota(jnp.int32, (tm, 1), 0) + i * tm
    for g, w in enumerate(POOL_WINDOWS):
        cols = pl.ds(g * POOL_GROUP, POOL_GROUP)
        span = tm + 2 * POOL_HALO
        width = 1
        while width < w:
            span -= width
            ext_ref[pl.ds(0, span), cols] = ext_ref[pl.ds(0, span), cols] + ext_ref[pl.ds(width, span), cols]
            width *= 2
        assert width == w
        win = ext_ref[pl.ds(POOL_HALO - w // 2, tm), cols]
        lo = jnp.maximum(t - w // 2, 0)
        hi = jnp.minimum(t + (w - w // 2) - 1, seq - 1)
        cnt = (hi - lo + 1).astype(F32)
        pooled = win * (1.0 / cnt) - cur_ref[:, cols]
        y = jnp.dot(pooled.astype(BF16), w_ref[g], preferred_element_type=F32)
        o_ref[:, cols] = (y * scale_ref[:, cols] * jax.nn.silu(g_ref[:, cols])).astype(o_ref.dtype)


def _pool_mix(ug, w_pool, pool_scale):
    s = ug.shape[0]
    width = POOL_WIDTH
    tm = 512
    hb = tm // POOL_HALO
    nhalo = s // POOL_HALO
    return pl.pallas_call(
        functools.partial(_pool_kernel, tm=tm, seq=s),
        grid=(s // tm,),
        in_specs=[pl.BlockSpec((POOL_HALO, width), lambda i: (jnp.maximum(i * hb - 1, 0), 0)),
                  pl.BlockSpec((tm, width), lambda i: (i, 0)),
                  pl.BlockSpec((POOL_HALO, width), lambda i: (jnp.minimum((i + 1) * hb, nhalo - 1), 0)),
                  pl.BlockSpec(w_pool.shape, lambda i: (0, 0, 0)),
                  pl.BlockSpec((1, width), lambda i: (0, 0)),
                  pl.BlockSpec((tm, width), lambda i: (i, 1))],
        out_specs=pl.BlockSpec((tm, width), lambda i: (i, 0)),
        out_shape=jax.ShapeDtypeStruct((s, width), BF16),
        scratch_shapes=[pltpu.VMEM((tm + 2 * POOL_HALO, width), F32)],
        compiler_params=_params(1),
        name="pool_mix",
    )(ug, ug, ug, w_pool, pool_scale, ug)


def _out_kernel(*refs, widths, tm, modulate_next, gate_proj):
    n = len(widths)
    y_refs = refs[:n]
    refs = refs[n:]
    if gate_proj:
        hin_ref, wg_ref = refs[:2]
        refs = refs[2:]
    w_ref, x_ref, gate_ref = refs[:3]
    if modulate_next:
        nw_ref, shift_ref, scale_ref, o_ref, h_ref = refs[3:]
        mul = nw_ref[...]
        one_plus_scale = 1.0 + scale_ref[...]
    else:
        (o_ref,) = refs[3:]
    sub = min(OUT_SUB, tm // 2)
    for sb in range(tm // sub):
        rows = slice(sb * sub, (sb + 1) * sub)
        acc = None
        off = 0
        for y_ref, wd in zip(y_refs, widths):
            y = y_ref[rows, :]
            if gate_proj:
                y = y * jax.nn.silu(jnp.dot(hin_ref[rows, :], wg_ref[:, off:off + wd], preferred_element_type=F32))
            part = jnp.dot(y.astype(BF16), w_ref[off:off + wd, :], preferred_element_type=F32)
            acc = part if acc is None else acc + part
            off += wd
        xn = x_ref[rows, :] + gate_ref[...] * acc
        o_ref[rows, :] = xn
        if modulate_next:
            r = lax.rsqrt(jnp.mean(xn * xn, axis=-1, keepdims=True) + NORM_EPS)
            h_ref[rows, :] = ((xn * r * mul) * one_plus_scale + shift_ref[...]).astype(BF16)


def _out_proj(ys, w, layer, x, gate, next_mod=None, gate_from=None):
    s, d = x.shape
    k = w.shape[1]
    gate_proj = gate_from is not None
    tm = min(256 if gate_proj else 512, s)
    widths = tuple(y.shape[1] for y in ys)
    vec = pl.BlockSpec((1, d), lambda i: (0, 0))
    rows = pl.BlockSpec((tm, d), lambda i: (i, 0))
    modulate_next = next_mod is not None
    gate_specs, gate_args = [], ()
    if gate_proj:
        hin, wg, col0 = gate_from
        assert col0 % k == 0 and sum(widths) == k
        gate_specs = [pl.BlockSpec((tm, hin.shape[1]), lambda i: (i, 0)),
                      pl.BlockSpec((None, wg.shape[1], k), lambda i: (layer, 0, col0 // k))]
        gate_args = (hin, wg)
    return pl.pallas_call(
        functools.partial(_out_kernel, widths=widths, tm=tm, modulate_next=modulate_next, gate_proj=gate_proj),
        grid=(s // tm,),
        in_specs=[pl.BlockSpec((tm, wd), lambda i: (i, 0)) for wd in widths] + gate_specs
        + [pl.BlockSpec((None, k, d), lambda i: (layer, 0, 0)), rows, vec] + ([vec, vec, vec] if modulate_next else []),
        out_specs=[rows, rows] if modulate_next else rows,
        out_shape=([jax.ShapeDtypeStruct((s, d), F32), jax.ShapeDtypeStruct((s, d), BF16)] if modulate_next
                   else jax.ShapeDtypeStruct((s, d), F32)),
        compiler_params=_params(1),
        name="out_proj",
    )(*ys, *gate_args, w, x, gate, *(next_mod if modulate_next else ()))


def _dft_split(s):
    n1 = 1 << (int(math.log2(s)) // 2)
    n2 = s // n1
    assert n1 * n2 == s and n1 % SUBLANES == 0 and n2 % SUBLANES == 0
    return n1, n2


def _angles(num, den):
    ang = (2.0 * math.pi / den) * (num % den)
    return np.cos(ang), np.sin(ang)


def _fold_cs_kernel(cs_ref, wf_ref, o_ref):
    wf = wf_ref[...]
    o_ref[:, 0:FOURIER_GROUP] = jnp.dot(cs_ref[0], wf, preferred_element_type=F32,
                                        precision=lax.Precision.HIGHEST)
    o_ref[:, FOURIER_GROUP:] = jnp.dot(cs_ref[1], wf, preferred_element_type=F32,
                                       precision=lax.Precision.HIGHEST)


def _fold_channel_dft(w_fourier_j, seq):
    fg = FOURIER_GROUP
    idx = np.arange(fg)
    cc, sc = _angles(idx[:, None] * idx[None, :], fg)
    cs = jnp.asarray(np.stack([cc, sc]) / math.sqrt(seq * fg), F32)
    return pl.pallas_call(
        _fold_cs_kernel,
        grid=(FOURIER_GROUPS,),
        in_specs=[pl.BlockSpec((2, fg, fg), lambda g: (0, 0, 0)),
                  pl.BlockSpec((None, fg, fg), lambda g: (g, 0, 0))],
        out_specs=pl.BlockSpec((None, fg, 2 * fg), lambda g: (g, 0, 0)),
        out_shape=jax.ShapeDtypeStruct((FOURIER_GROUPS, fg, 2 * fg), F32),
        compiler_params=_params(1),
        name="fold_channel_dft",
    )(cs, w_fourier_j)


def _fold_win_kernel(w_ref, ab_ref, o_ref):
    o_ref[...] = jnp.dot(w_ref[...], ab_ref[...].astype(BF16), preferred_element_type=F32).astype(BF16)


def _fold_in_proj(w_in, layer, ab):
    d = w_in.shape[1]
    fg = FOURIER_GROUP
    return pl.pallas_call(
        _fold_win_kernel,
        grid=(FOURIER_GROUPS, 2),
        in_specs=[pl.BlockSpec((None, d, fg), lambda g, t: (layer, 0, g)),
                  pl.BlockSpec((None, fg, fg), lambda g, t: (g, 0, t))],
        out_specs=pl.BlockSpec((d, fg), lambda g, t: (0, g + t * FOURIER_GROUPS)),
        out_shape=jax.ShapeDtypeStruct((d, 2 * D_MODEL), BF16),
        compiler_params=_params(2),
        name="fold_in_proj",
    )(w_in, ab)


HIGH_HALF = -65536


def _bf16_bits(v):
    b = lax.bitcast_convert_type(v, jnp.int32)
    return b + 0x7FFF + (lax.shift_right_logical(b, 16) & 1)


def _dft1_kernel(f_ref, p_ref, q_ref, t_ref, *, nb, n1):
    f = f_ref[...]
    for jj in range(nb):
        rows = slice(jj * n1, (jj + 1) * n1)
        rhs = jnp.concatenate([p_ref[rows, :], q_ref[rows, :]], axis=0)
        t = jnp.dot(f, rhs, preferred_element_type=F32)
        t_ref[jj] = (_bf16_bits(t[0:n1]) & HIGH_HALF) | lax.shift_right_logical(_bf16_bits(t[n1:2 * n1]), 16)


def _dft_stage1(pq_t, f1, n1, n2):
    s, c2 = pq_t.shape
    c = c2 // 2
    nb = SUBLANES
    cb = min(DFT_CB, c)
    ncb = c // cb
    return pl.pallas_call(
        functools.partial(_dft1_kernel, nb=nb, n1=n1),
        grid=(n2 // nb, ncb),
        in_specs=[pl.BlockSpec((2 * n1, 2 * n1), lambda b, j: (0, 0)),
                  pl.BlockSpec((nb * n1, cb), lambda b, j: (b, j)),
                  pl.BlockSpec((nb * n1, cb), lambda b, j: (b, j + ncb))],
        out_specs=pl.BlockSpec((nb, n1, cb), lambda b, j: (b, 0, j)),
        out_shape=jax.ShapeDtypeStruct((n2, n1, c), jnp.int32),
        compiler_params=_params(2),
        name="dft_stage1",
    )(f1, pq_t, pq_t)


def _dft2_kernel(l_ref, t_ref, o_ref, *, n2, cb):
    packed = t_ref[...]
    re = lax.bitcast_convert_type(packed & HIGH_HALF, F32)
    im = lax.bitcast_convert_type(lax.shift_left(packed, 16), F32)
    rhs = jnp.stack([re, im], axis=1).reshape(n2 * 2 * SUBLANES, cb).astype(BF16)
    o_ref[...] = jnp.dot(l_ref[0], rhs, preferred_element_type=F32).reshape(n2, SUBLANES, cb)


def _dft_stage2(t_nk, l2, n1, n2):
    c = t_nk.shape[-1]
    cb = min(DFT_CB, c)
    row_blocks = pl.BlockSpec((n2, SUBLANES, cb), lambda g, j: (0, g, j))
    out = pl.pallas_call(
        functools.partial(_dft2_kernel, n2=n2, cb=cb),
        grid=(n1 // SUBLANES, c // cb),
        in_specs=[pl.BlockSpec((1, SUBLANES * n2, SUBLANES * 2 * n2), lambda g, j: (g, 0, 0)),
                  row_blocks],
        out_specs=row_blocks,
        out_shape=jax.ShapeDtypeStruct((n2, n1, c), F32),
        compiler_params=_params(2),
        name="dft_stage2",
    )(l2, t_nk)
    return out.reshape(n1 * n2, c)


def _dft_matrices(s):
    n1, n2 = _dft_split(s)
    a = np.arange(n1)
    c1, s1 = _angles(a[:, None] * a[None, :], n1)
    f1 = jnp.asarray(np.block([[c1, -s1], [-s1, -c1]]).astype(BF16))
    groups = n1 // SUBLANES
    g = np.arange(groups)[:, None, None, None]
    k2 = np.arange(n2)[None, :, None, None]
    j = np.arange(SUBLANES)[None, None, :, None]
    nn = np.arange(n2)[None, None, None, :]
    c2, s2 = _angles((SUBLANES * g + j + n1 * k2) * nn, s)
    cs = np.concatenate([c2, s2], axis=3).astype(BF16)
    return n1, n2, f1, _expand_twiddles(jnp.asarray(cs.reshape(groups, n2 * SUBLANES, 2 * n2)))


def _expand_kernel(cs_ref, e_ref, o_ref):
    spread = jnp.dot(cs_ref[0], e_ref[...], preferred_element_type=F32)
    row = lax.broadcasted_iota(jnp.int32, spread.shape, 0)
    col = lax.broadcasted_iota(jnp.int32, spread.shape, 1)
    o_ref[0] = jnp.where(row % SUBLANES == col % SUBLANES, spread, 0.0).astype(BF16)


def _expand_twiddles(cs):
    groups, rows, cols = cs.shape
    wide = cols * SUBLANES
    src = jnp.arange(cols, dtype=jnp.int32)[:, None]
    slot = (src % (cols // 2)) * 2 + src // (cols // 2)
    e = jnp.arange(wide, dtype=jnp.int32)[None, :] // SUBLANES == slot
    return pl.pallas_call(
        _expand_kernel,
        grid=(groups,),
        in_specs=[pl.BlockSpec((1, rows, cols), lambda g: (g, 0, 0)),
                  pl.BlockSpec((cols, wide), lambda g: (0, 0))],
        out_specs=pl.BlockSpec((1, rows, wide), lambda g: (g, 0, 0)),
        out_shape=jax.ShapeDtypeStruct((groups, rows, wide), BF16),
        compiler_params=_params(1),
        name="expand_twiddles",
    )(cs, e.astype(BF16))


def kernel(x, c, norm_w, ada_w, ada_b, w_in_ab, w_pool, pool_scale, q_norm_w, k_norm_w, lambda_q1, lambda_k1,
           lambda_q2, lambda_k2, subln_w, w_out_ab, w_in_c, w_fourier, w_out_c):
    batch, s, d = x.shape
    assert batch == 1 and d == D_MODEL and s % ATT_TK == 0
    xs = x.reshape(s, d)
    mod = _ada_mod(c, ada_w, ada_b)
    n1, n2, f1, l2 = _dft_matrices(s)
    row = lambda v: v.reshape(1, -1)

    def mod_of(i):
        return mod[i, :, 0:d], mod[i, :, d:2 * d], mod[i, :, 2 * d:3 * d]

    def finish(ys, w_out, j, xs, i, gate_from=None):
        gate = mod_of(i)[2]
        if i + 1 == DEPTH:
            return _out_proj(ys, w_out, j, xs, gate, gate_from=gate_from), None
        shift, scale, _ = mod_of(i + 1)
        return _out_proj(ys, w_out, j, xs, gate, (row(norm_w[i + 1]), shift, scale), gate_from=gate_from)

    w_in_ab, w_out_ab, w_in_c, w_out_c = (w.astype(BF16) for w in (w_in_ab, w_out_ab, w_in_c, w_out_c))
    h = _modulate(xs, row(norm_w[0]), *mod_of(0)[:2])
    for i in range(DEPTH):
        j = i // 2
        if i % 2 == 0:
            o1, o2, o3, o4 = POOL_WIDTH, POOL_WIDTH + DIFF_WIDTH, POOL_WIDTH + 2 * DIFF_WIDTH, POOL_WIDTH + 3 * DIFF_WIDTH
            lambda_init = 0.8 - 0.6 * math.exp(-0.3 * i)
            ug = _matmul(h, w_in_ab, F32, PROJ_TM, PROJ_TN, name="pool_gate_in_proj",
                         cols=((0, o1), (o4, o4 + AB_WIDTH)), layer=j)
            qt, ka, vt = _qkv_proj(h, w_in_ab, j, o1, o2, o3, row(q_norm_w[j]), row(k_norm_w[j]))
            y_a = _pool_mix(ug, w_pool[j].astype(BF16), row(pool_scale[j]))
            y_b = _diff_attn(qt, ka, vt, row(lambda_q1[j]), row(lambda_k1[j]), row(lambda_q2[j]),
                             row(lambda_k2[j]), subln_w[j], ug, lambda_init)
            xs, h = finish([y_a, y_b], w_out_ab, j, xs, i)
        else:
            ab = _fold_channel_dft(w_fourier[j], s)
            w_pq = _fold_in_proj(w_in_c, j, ab)
            h_t = h.reshape(n1, n2, d).transpose(1, 0, 2).reshape(s, d)
            pq_t = _matmul(h_t, w_pq, BF16, PROJ_TM, PROJ_TN, name="fourier_in_proj")
            t_nk = _dft_stage1(pq_t, f1, n1, n2)
            f = _dft_stage2(t_nk, l2, n1, n2)
            xs, h = finish([f], w_out_c, j, xs, i, gate_from=(h, w_in_c, d))
    return xs.reshape(batch, s, d)
```

```python
import functools
import math

import jax
import jax.numpy as jnp
import numpy as np
from jax import lax
from jax.experimental import pallas as pl
from jax.experimental.pallas import tpu as pltpu

F32 = jnp.float32
BF16 = jnp.bfloat16

D_MODEL = 2048
DEPTH = 4
NORM_EPS = 1e-6

POOL_WINDOWS = (2, 4, 8, 16)
POOL_WIDTH = D_MODEL // 2
POOL_GROUP = POOL_WIDTH // len(POOL_WINDOWS)
POOL_HALO = 8

HEADS = 8
HEAD_DIM = 64
HEAD_V = 2 * HEAD_DIM
DIFF_WIDTH = HEADS * HEAD_V
AB_WIDTH = POOL_WIDTH + DIFF_WIDTH

FOURIER_GROUPS = 4
FOURIER_GROUP = D_MODEL // FOURIER_GROUPS

LANES = 128
SUBLANES = 8
POS_RADIX = 256
COEF_PARTS = 3
LOG2E = math.log2(math.e)

ATT_TQ = 512
ATT_TK = 512
ATT_DIAG = ATT_TQ // ATT_TK
ATT_UNROLL = 8
REF_MARGIN = 60.0
REF_FLOOR = 2.0 ** -60
BOUND_SLACK = 1.001

PROJ_TM, PROJ_TN = 2048, 1024
OUT_SUB = 256
DFT_CB = 2048

VMEM_LIMIT = 56 * 1024 * 1024


def _params(n_axes):
    return pltpu.CompilerParams(dimension_semantics=("arbitrary",) * n_axes,
                                vmem_limit_bytes=VMEM_LIMIT)


def _mod_kernel(c_ref, w_ref, b_ref, o_ref):
    c = c_ref[...]
    o_ref[...] = jnp.sum(jax.nn.silu(c) * w_ref[...], axis=0, keepdims=True) + b_ref[...]


def _ada_mod(c, ada_w, ada_b):
    depth, d, n = ada_w.shape
    tn = 1024
    return pl.pallas_call(
        _mod_kernel,
        grid=(depth, n // tn),
        in_specs=[pl.BlockSpec((d, 1), lambda i, j: (0, 0)),
                  pl.BlockSpec((None, d, tn), lambda i, j: (i, 0, j)),
                  pl.BlockSpec((None, 1, tn), lambda i, j: (i, 0, j))],
        out_specs=pl.BlockSpec((None, 1, tn), lambda i, j: (i, 0, j)),
        out_shape=jax.ShapeDtypeStruct((depth, 1, n), F32),
        compiler_params=_params(2),
        name="ada_mod",
    )(c.reshape(d, 1), ada_w, ada_b.reshape(depth, 1, n))


def _modulate_kernel(x_ref, nw_ref, shift_ref, scale_ref, o_ref):
    xf = x_ref[...]
    r = lax.rsqrt(jnp.mean(xf * xf, axis=-1, keepdims=True) + NORM_EPS)
    y = xf * r * nw_ref[...]
    y = y * (1.0 + scale_ref[...]) + shift_ref[...]
    o_ref[...] = y.astype(o_ref.dtype)


def _modulate(x, nw, shift, scale):
    s, d = x.shape
    tm = min(1024, s)
    vec = pl.BlockSpec((1, d), lambda i: (0, 0))
    return pl.pallas_call(
        _modulate_kernel,
        grid=(s // tm,),
        in_specs=[pl.BlockSpec((tm, d), lambda i: (i, 0)), vec, vec, vec],
        out_specs=pl.BlockSpec((tm, d), lambda i: (i, 0)),
        out_shape=jax.ShapeDtypeStruct((s, d), BF16),
        compiler_params=_params(1),
        name="modulate",
    )(x, nw, shift, scale)


def _mm_kernel(a_ref, w_ref, o_ref, *, precision):
    o_ref[...] = jnp.dot(a_ref[...], w_ref[...], preferred_element_type=F32,
                         precision=precision).astype(o_ref.dtype)


def _matmul(a, w, out_dtype, tm, tn, precision=None, name="matmul", cols=None, layer=None):
    m, k = a.shape
    cols = cols if cols is not None else ((0, w.shape[-1]),)
    n = sum(stop - start for start, stop in cols)
    tm, tn = min(tm, m), min(tn, n)
    assert all(start % tn == 0 and stop % tn == 0 for start, stop in cols)

    def wblock(j):
        jb, first = j, 0
        for start, stop in cols:
            jb = jnp.where(j >= first, j - first + start // tn, jb)
            first += (stop - start) // tn
        return jb

    if layer is None:
        wspec = pl.BlockSpec((k, tn), lambda i, j: (0, wblock(j)))
    else:
        wspec = pl.BlockSpec((None, k, tn), lambda i, j: (layer, 0, wblock(j)))
    return pl.pallas_call(
        functools.partial(_mm_kernel, precision=precision),
        grid=(m // tm, n // tn),
        in_specs=[pl.BlockSpec((tm, k), lambda i, j: (i, 0)), wspec],
        out_specs=pl.BlockSpec((tm, tn), lambda i, j: (i, j)),
        out_shape=jax.ShapeDtypeStruct((m, n), out_dtype),
        compiler_params=_params(2),
        name=name,
    )(a, w)


def _qkv_kernel(h_ref, wq_ref, wk_ref, wv_ref, qnw_ref, knw_ref, qt_ref, ka_ref, vt_ref, *, tm):
    lane = lax.broadcasted_iota(jnp.int32, (ATT_TK, LANES), 1)
    first = lane < HEAD_DIM
    pos = lax.broadcasted_iota(jnp.int32, (ATT_TK, LANES), 0)
    pos_lo = (pos % POS_RADIX).astype(F32)
    pos_hi = (pos // POS_RADIX).astype(F32)
    pos_cols = jnp.where(lane < HEAD_DIM + COEF_PARTS, pos_lo, jnp.where(lane < HEAD_DIM + 2 * COEF_PARTS, pos_hi, 0.0))

    def halves_rms(z, w):
        sq = z * z
        ss1 = jnp.sum(jnp.where(first, sq, 0.0), axis=-1, keepdims=True)
        ss2 = jnp.sum(jnp.where(first, 0.0, sq), axis=-1, keepdims=True)
        r = jnp.where(first, lax.rsqrt(ss1 / HEAD_DIM + NORM_EPS), lax.rsqrt(ss2 / HEAD_DIM + NORM_EPS))
        return z * r * w

    for cc in range(tm // ATT_TK):
        rows = slice(cc * ATT_TK, (cc + 1) * ATT_TK)
        h = h_ref[rows, :]
        zq = jnp.dot(h, wq_ref[...], preferred_element_type=F32)
        zk = jnp.dot(h, wk_ref[...], preferred_element_type=F32)
        zv = jnp.dot(h, wv_ref[...], preferred_element_type=F32)
        for hh in range(2):
            cols = slice(hh * LANES, (hh + 1) * LANES)
            qn = halves_rms(zq[:, cols], qnw_ref[...]) * (HEAD_DIM ** -0.5) * LOG2E
            kn = halves_rms(zk[:, cols], knw_ref[...])
            for m in range(2):
                qm = qn if m == 0 else pltpu.roll(qn, HEAD_DIM, axis=1)
                km = kn if m == 0 else pltpu.roll(kn, HEAD_DIM, axis=1)
                qt_ref[hh, m, :, rows] = jnp.where(first, qm, 0.0).T.astype(BF16)
                ka_ref[hh, m, cc] = jnp.where(first, km, pos_cols).astype(BF16)
            vt_ref[hh, cc] = zv[:, cols].T.astype(BF16)


def _qkv_proj(h, w, layer, col_q, col_k, col_v, qnw, knw):
    s, d = h.shape
    tm = min(2048, s)
    nchunk = s // ATT_TK
    cpt = tm // ATT_TK
    tn = 2 * LANES
    assert col_q % tn == 0 and col_k % tn == 0 and col_v % tn == 0
    wspecs = [pl.BlockSpec((None, d, tn), functools.partial(lambda i, j, jb: (layer, 0, j + jb), jb=c0 // tn))
              for c0 in (col_q, col_k, col_v)]
    vec = pl.BlockSpec((1, LANES), lambda i, j: (0, 0))
    return pl.pallas_call(
        functools.partial(_qkv_kernel, tm=tm),
        grid=(s // tm, HEADS // 2),
        in_specs=[pl.BlockSpec((tm, d), lambda i, j: (i, 0)), *wspecs, vec, vec],
        out_specs=[pl.BlockSpec((2, 2, LANES, tm), lambda i, j: (j, 0, 0, i)),
                   pl.BlockSpec((2, 2, cpt, ATT_TK, LANES), lambda i, j: (j, 0, i, 0, 0)),
                   pl.BlockSpec((2, cpt, HEAD_V, ATT_TK), lambda i, j: (j, i, 0, 0))],
        out_shape=[jax.ShapeDtypeStruct((HEADS, 2, LANES, s), BF16),
                   jax.ShapeDtypeStruct((HEADS, 2, nchunk, ATT_TK, LANES), BF16),
                   jax.ShapeDtypeStruct((HEADS, nchunk, HEAD_V, ATT_TK), BF16)],
        compiler_params=_params(2),
        name="qkv_proj",
    )(h, w, w, w, jnp.tile(qnw, (1, 2)), jnp.tile(knw, (1, 2)))


def _attn_kernel(slopes_ref, qt_ref, ka_ref, vt_ref, lq1_ref, lk1_ref, lq2_ref, lk2_ref, subw_ref, g_ref,
                 o_ref, acc_ref, mu_ref, qv_ref, kmax_ref, p_a, p_b, den_ref, *, lambda_init, nchunk, unroll):
    hd = pl.program_id(0)
    i = pl.program_id(1)
    base = hd * (1 + COEF_PARTS)
    slope = slopes_ref[base]
    q_off = lax.broadcasted_iota(jnp.int32, (1, ATT_TQ), 1).astype(F32)

    row1 = lax.broadcasted_iota(jnp.int32, (LANES - HEAD_DIM, 1), 0)
    coef_col = jnp.zeros((LANES - HEAD_DIM, 1), F32)
    for t in range(COEF_PARTS):
        piece = slopes_ref[base + 1 + t]
        coef_col = jnp.where(row1 == t, piece, coef_col)
        coef_col = jnp.where(row1 == COEF_PARTS + t, piece * POS_RADIX, coef_col)
    coef_rows = jnp.broadcast_to(coef_col, (LANES - HEAD_DIM, ATT_TQ))
    for m in range(2):
        for kind, sign in ((0, -1.0), (1, 0.0), (2, 1.0)):
            qv_ref[kind, m, 0:HEAD_DIM, :] = qt_ref[0, m, 0:HEAD_DIM, :]
            qv_ref[kind, m, HEAD_DIM:LANES, :] = (sign * coef_rows).astype(BF16)

    @pl.when(i == 0)
    def _():
        lane = lax.broadcasted_iota(jnp.int32, (ATT_TK, LANES), 1)
        for m in range(2):
            def widest(c, best):
                kc = ka_ref[0, m, c].astype(F32)
                return jnp.maximum(best, jnp.sum(jnp.where(lane < HEAD_DIM, kc * kc, 0.0), axis=1, keepdims=True))
            kmax_ref[m] = jnp.max(lax.fori_loop(0, nchunk, widest, jnp.zeros((ATT_TK, 1), F32)))

    for m in range(2):
        qf = qt_ref[0, m, 0:HEAD_DIM, :].astype(F32)
        bound = jnp.sqrt(jnp.sum(qf * qf, axis=0, keepdims=True) * kmax_ref[m]) * BOUND_SLACK
        mu_ref[m] = bound - REF_MARGIN

    c_diag = i * ATT_DIAG

    def chunk_of(e):
        if isinstance(e, int) and e < ATT_DIAG:
            return c_diag + e
        rest = e - ATT_DIAG
        return jnp.where(e < ATT_DIAG, c_diag + e, rest + ATT_DIAG * (rest >= c_diag).astype(jnp.int32))

    def side_of(c):
        return (c < c_diag).astype(jnp.int32) - (c >= c_diag + ATT_DIAG).astype(jnp.int32)

    p_bufs = (p_a, p_b)

    def probs(e, p_ref):
        c = chunk_of(e)
        side = side_of(c)
        sidef = side.astype(F32)
        gap = jnp.abs(c * ATT_TK - i * ATT_TQ).astype(F32)
        cvec = -slope * (sidef * sidef * gap + sidef * q_off)
        for m in range(2):
            st = jnp.dot(ka_ref[0, m, c], qv_ref[1 + side, m], preferred_element_type=F32)
            if isinstance(e, int) and e < ATT_DIAG:
                key_off = lax.broadcasted_iota(jnp.int32, (ATT_TK, ATT_TQ), 0) + e * ATT_TK
                qry_off = lax.broadcasted_iota(jnp.int32, (ATT_TK, ATT_TQ), 1)
                st = st - slope * jnp.abs(key_off - qry_off).astype(F32)
            p = jnp.exp2(st - (mu_ref[m] - cvec))
            p_ref[m] = p.astype(BF16)
            den_ref[m] += jnp.sum(p.reshape(ATT_TK // SUBLANES, SUBLANES, ATT_TQ), axis=0)

    def group(e0, count, probs_last):
        sums = [None, None]
        for u in range(count):
            if u < count - 1 or probs_last:
                probs(e0 + u + 1, p_bufs[(u + 1) % 2])
            c = chunk_of(e0 + u)
            for m in range(2):
                pv = jnp.dot(vt_ref[0, c], p_bufs[u % 2][m], preferred_element_type=F32)
                sums[m] = pv if sums[m] is None else sums[m] + pv
        for m in range(2):
            acc_ref[m] += sums[m]

    def loop_body(it, carry):
        group(it * unroll, unroll, True)
        return carry

    def one_pass(_):
        acc_ref[...] = jnp.zeros(acc_ref.shape, F32)
        den_ref[...] = jnp.zeros(den_ref.shape, F32)
        n_groups = nchunk // unroll
        probs(0, p_a)
        group(0, unroll, True)
        lax.fori_loop(1, n_groups - 1, loop_body, 0)
        group((n_groups - 1) * unroll, unroll, False)
        lowest = None
        for m in range(2):
            den = jnp.sum(den_ref[m], axis=0, keepdims=True)
            mu_ref[m] = jnp.where(den < REF_FLOOR, mu_ref[m] - 2.0 * REF_MARGIN, mu_ref[m])
            lowest = jnp.min(den) if lowest is None else jnp.minimum(lowest, jnp.min(den))
        return (lowest < REF_FLOOR).astype(jnp.int32)

    lax.while_loop(lambda retry: retry > 0, one_pass, jnp.int32(1))

    lam = (jnp.exp(jnp.sum(lq1_ref[...] * lk1_ref[...], axis=-1, keepdims=True))
           - jnp.exp(jnp.sum(lq2_ref[...] * lk2_ref[...], axis=-1, keepdims=True)) + lambda_init)
    den1 = jnp.sum(den_ref[0], axis=0, keepdims=True)
    den2 = jnp.sum(den_ref[1], axis=0, keepdims=True)
    o = acc_ref[0] / den1 - lam * (acc_ref[1] / den2)
    r = lax.rsqrt(jnp.mean(o * o, axis=0, keepdims=True) + NORM_EPS)
    y = (o * r * subw_ref[...]) * (1.0 - lambda_init)
    o_ref[...] = (y.T * jax.nn.silu(g_ref[...])).astype(o_ref.dtype)


def _diff_attn(qt, ka, vt, lq1, lk1, lq2, lk2, subw, g, lambda_init):
    s = qt.shape[-1]
    g_col0 = (g.shape[1] - DIFF_WIDTH) // HEAD_V
    nchunk = s // ATT_TK
    unroll = min(ATT_UNROLL, nchunk // 2)
    assert unroll % 2 == 0 and nchunk % unroll == 0 and unroll >= ATT_DIAG and s % ATT_TQ == 0
    whole = jnp.asarray([LOG2E * 2.0 ** (-8.0 * (h + 1) / HEADS) for h in range(HEADS)], F32)
    pieces, rest = [], whole
    for _ in range(COEF_PARTS):
        pieces.append(rest.astype(BF16).astype(F32))
        rest = rest - pieces[-1]
    slopes = jnp.stack([whole] + pieces, axis=1).reshape(-1)
    vec = pl.BlockSpec((1, HEAD_DIM), lambda h, i: (0, 0))
    return pl.pallas_call(
        functools.partial(_attn_kernel, lambda_init=lambda_init, nchunk=nchunk, unroll=unroll),
        grid=(HEADS, s // ATT_TQ),
        in_specs=[pl.BlockSpec(memory_space=pltpu.SMEM),
                  pl.BlockSpec((1, 2, LANES, ATT_TQ), lambda h, i: (h, 0, 0, i)),
                  pl.BlockSpec((1, 2, nchunk, ATT_TK, LANES), lambda h, i: (h, 0, 0, 0, 0)),
                  pl.BlockSpec((1, nchunk, HEAD_V, ATT_TK), lambda h, i: (h, 0, 0, 0)),
                  vec, vec, vec, vec,
                  pl.BlockSpec((HEAD_V, 1), lambda h, i: (0, 0)),
                  pl.BlockSpec((ATT_TQ, HEAD_V), lambda h, i: (i, g_col0 + h))],
        out_specs=pl.BlockSpec((ATT_TQ, HEAD_V), lambda h, i: (i, h)),
        out_shape=jax.ShapeDtypeStruct((s, DIFF_WIDTH), BF16),
        scratch_shapes=[pltpu.VMEM((2, HEAD_V, ATT_TQ), F32), pltpu.VMEM((2, 1, ATT_TQ), F32),
                        pltpu.VMEM((3, 2, LANES, ATT_TQ), BF16), pltpu.SMEM((2,), F32),
                        pltpu.VMEM((2, ATT_TK, ATT_TQ), BF16), pltpu.VMEM((2, ATT_TK, ATT_TQ), BF16),
                        pltpu.VMEM((2, SUBLANES, ATT_TQ), F32)],
        compiler_params=_params(2),
        name="diff_attn",
    )(slopes, qt, ka, vt, lq1, lk1, lq2, lk2, subw.reshape(HEAD_V, 1), g)


def _pool_kernel(prev_ref, cur_ref, next_ref, w_ref, scale_ref, g_ref, o_ref, ext_ref, *, tm, seq):
    i = pl.program_id(0)
    last = pl.num_programs(0) - 1
    zero_halo = jnp.zeros((POOL_HALO, POOL_WIDTH), F32)
    ext_ref[0:POOL_HALO] = jnp.where(i > 0, prev_ref[...], zero_halo)
    ext_ref[POOL_HALO:POOL_HALO + tm] = cur_ref[...]
    ext_ref[POOL_HALO + tm:POOL_HALO + tm + POOL_HALO] = jnp.where(i < last, next_ref[...], zero_halo)
    t = lax.broadcasted_iota(jnp.int32, (tm, 1), 0) + i * tm
    for g, w in enumerate(POOL_WINDOWS):
        cols = pl.ds(g * POOL_GROUP, POOL_GROUP)
        span = tm + 2 * POOL_HALO
        width = 1
        while width < w:
            span -= width
            ext_ref[pl.ds(0, span), cols] = ext_ref[pl.ds(0, span), cols] + ext_ref[pl.ds(width, span), cols]
            width *= 2
        assert width == w
        win = ext_ref[pl.ds(POOL_HALO - w // 2, tm), cols]
        lo = jnp.maximum(t - w // 2, 0)
        hi = jnp.minimum(t + (w - w // 2) - 1, seq - 1)
        cnt = (hi - lo + 1).astype(F32)
        pooled = win * (1.0 / cnt) - cur_ref[:, cols]
        y = jnp.dot(pooled.astype(BF16), w_ref[g], preferred_element_type=F32)
        o_ref[:, cols] = (y * scale_ref[:, cols] * jax.nn.silu(g_ref[:, cols])).astype(o_ref.dtype)


def _pool_mix(ug, w_pool, pool_scale):
    s = ug.shape[0]
    width = POOL_WIDTH
    tm = 512
    hb = tm // POOL_HALO
    nhalo = s // POOL_HALO
    return pl.pallas_call(
        functools.partial(_pool_kernel, tm=tm, seq=s),
        grid=(s // tm,),
        in_specs=[pl.BlockSpec((POOL_HALO, width), lambda i: (jnp.maximum(i * hb - 1, 0), 0)),
                  pl.BlockSpec((tm, width), lambda i: (i, 0)),
                  pl.BlockSpec((POOL_HALO, width), lambda i: (jnp.minimum((i + 1) * hb, nhalo - 1), 0)),
                  pl.BlockSpec(w_pool.shape, lambda i: (0, 0, 0)),
                  pl.BlockSpec((1, width), lambda i: (0, 0)),
                  pl.BlockSpec((tm, width), lambda i: (i, 1))],
        out_specs=pl.BlockSpec((tm, width), lambda i: (i, 0)),
        out_shape=jax.ShapeDtypeStruct((s, width), BF16),
        scratch_shapes=[pltpu.VMEM((tm + 2 * POOL_HALO, width), F32)],
        compiler_params=_params(1),
        name="pool_mix",
    )(ug, ug, ug, w_pool, pool_scale, ug)


def _out_kernel(*refs, widths, tm, modulate_next, gate_proj):
    n = len(widths)
    y_refs = refs[:n]
    refs = refs[n:]
    if gate_proj:
        hin_ref, wg_ref = refs[:2]
        refs = refs[2:]
    w_ref, x_ref, gate_ref = refs[:3]
    if modulate_next:
        nw_ref, shift_ref, scale_ref, o_ref, h_ref = refs[3:]
        mul = nw_ref[...]
        one_plus_scale = 1.0 + scale_ref[...]
    else:
        (o_ref,) = refs[3:]
    sub = min(OUT_SUB, tm // 2)
    for sb in range(tm // sub):
        rows = slice(sb * sub, (sb + 1) * sub)
        acc = None
        off = 0
        for y_ref, wd in zip(y_refs, widths):
            y = y_ref[rows, :]
            if gate_proj:
                y = y * jax.nn.silu(jnp.dot(hin_ref[rows, :], wg_ref[:, off:off + wd], preferred_element_type=F32))
            part = jnp.dot(y.astype(BF16), w_ref[off:off + wd, :], preferred_element_type=F32)
            acc = part if acc is None else acc + part
            off += wd
        xn = x_ref[rows, :] + gate_ref[...] * acc
        o_ref[rows, :] = xn
        if modulate_next:
            r = lax.rsqrt(jnp.mean(xn * xn, axis=-1, keepdims=True) + NORM_EPS)
            h_ref[rows, :] = ((xn * r * mul) * one_plus_scale + shift_ref[...]).astype(BF16)


def _out_proj(ys, w, layer, x, gate, next_mod=None, gate_from=None):
    s, d = x.shape
    k = w.shape[1]
    gate_proj = gate_from is not None
    tm = min(512, s)
    once = pl.Buffered(1) if gate_proj else None
    widths = tuple(y.shape[1] for y in ys)
    vec = pl.BlockSpec((1, d), lambda i: (0, 0))
    rows = pl.BlockSpec((tm, d), lambda i: (i, 0))
    modulate_next = next_mod is not None
    gate_specs, gate_args = [], ()
    if gate_proj:
        hin, wg, col0 = gate_from
        assert col0 % k == 0 and sum(widths) == k
        gate_specs = [pl.BlockSpec((tm, hin.shape[1]), lambda i: (i, 0)),
                      pl.BlockSpec((None, wg.shape[1], k), lambda i: (layer, 0, col0 // k), pipeline_mode=once)]
        gate_args = (hin, wg)
    return pl.pallas_call(
        functools.partial(_out_kernel, widths=widths, tm=tm, modulate_next=modulate_next, gate_proj=gate_proj),
        grid=(s // tm,),
        in_specs=[pl.BlockSpec((tm, wd), lambda i: (i, 0)) for wd in widths] + gate_specs
        + [pl.BlockSpec((None, k, d), lambda i: (layer, 0, 0), pipeline_mode=once), rows, vec]
        + ([vec, vec, vec] if modulate_next else []),
        out_specs=[rows, rows] if modulate_next else rows,
        out_shape=([jax.ShapeDtypeStruct((s, d), F32), jax.ShapeDtypeStruct((s, d), BF16)] if modulate_next
                   else jax.ShapeDtypeStruct((s, d), F32)),
        compiler_params=_params(1),
        name="out_proj",
    )(*ys, *gate_args, w, x, gate, *(next_mod if modulate_next else ()))


def _dft_split(s):
    n1 = 1 << (int(math.log2(s)) // 2)
    n2 = s // n1
    assert n1 * n2 == s and n1 % SUBLANES == 0 and n2 % SUBLANES == 0
    return n1, n2


def _angles(num, den):
    ang = (2.0 * math.pi / den) * (num % den)
    return np.cos(ang), np.sin(ang)


def _fold_cs_kernel(cs_ref, wf_ref, o_ref):
    wf = wf_ref[...]
    o_ref[:, 0:FOURIER_GROUP] = jnp.dot(cs_ref[0], wf, preferred_element_type=F32,
                                        precision=lax.Precision.HIGHEST)
    o_ref[:, FOURIER_GROUP:] = jnp.dot(cs_ref[1], wf, preferred_element_type=F32,
                                       precision=lax.Precision.HIGHEST)


def _fold_channel_dft(w_fourier_j, seq):
    fg = FOURIER_GROUP
    idx = np.arange(fg)
    cc, sc = _angles(idx[:, None] * idx[None, :], fg)
    cs = jnp.asarray(np.stack([cc, sc]) / math.sqrt(seq * fg), F32)
    return pl.pallas_call(
        _fold_cs_kernel,
        grid=(FOURIER_GROUPS,),
        in_specs=[pl.BlockSpec((2, fg, fg), lambda g: (0, 0, 0)),
                  pl.BlockSpec((None, fg, fg), lambda g: (g, 0, 0))],
        out_specs=pl.BlockSpec((None, fg, 2 * fg), lambda g: (g, 0, 0)),
        out_shape=jax.ShapeDtypeStruct((FOURIER_GROUPS, fg, 2 * fg), F32),
        compiler_params=_params(1),
        name="fold_channel_dft",
    )(cs, w_fourier_j)


def _fold_win_kernel(w_ref, ab_ref, o_ref):
    o_ref[...] = jnp.dot(w_ref[...], ab_ref[...].astype(BF16), preferred_element_type=F32).astype(BF16)


def _fold_in_proj(w_in, layer, ab):
    d = w_in.shape[1]
    fg = FOURIER_GROUP
    return pl.pallas_call(
        _fold_win_kernel,
        grid=(FOURIER_GROUPS, 2),
        in_specs=[pl.BlockSpec((None, d, fg), lambda g, t: (layer, 0, g)),
                  pl.BlockSpec((None, fg, fg), lambda g, t: (g, 0, t))],
        out_specs=pl.BlockSpec((d, fg), lambda g, t: (0, g + t * FOURIER_GROUPS)),
        out_shape=jax.ShapeDtypeStruct((d, 2 * D_MODEL), BF16),
        compiler_params=_params(2),
        name="fold_in_proj",
    )(w_in, ab)


HIGH_HALF = -65536


def _bf16_bits(v):
    b = lax.bitcast_convert_type(v, jnp.int32)
    return b + 0x7FFF + (lax.shift_right_logical(b, 16) & 1)


def _dft1_kernel(f_ref, p_ref, q_ref, t_ref, *, nb, n1):
    f = f_ref[...]
    for jj in range(nb):
        rows = slice(jj * n1, (jj + 1) * n1)
        rhs = jnp.concatenate([p_ref[rows, :], q_ref[rows, :]], axis=0)
        t = jnp.dot(f, rhs, preferred_element_type=F32)
        t_ref[jj] = (_bf16_bits(t[0:n1]) & HIGH_HALF) | lax.shift_right_logical(_bf16_bits(t[n1:2 * n1]), 16)


def _dft_stage1(pq_t, f1, n1, n2):
    s, c2 = pq_t.shape
    c = c2 // 2
    nb = SUBLANES
    cb = min(DFT_CB, c)
    ncb = c // cb
    return pl.pallas_call(
        functools.partial(_dft1_kernel, nb=nb, n1=n1),
        grid=(n2 // nb, ncb),
        in_specs=[pl.BlockSpec((2 * n1, 2 * n1), lambda b, j: (0, 0)),
                  pl.BlockSpec((nb * n1, cb), lambda b, j: (b, j)),
                  pl.BlockSpec((nb * n1, cb), lambda b, j: (b, j + ncb))],
        out_specs=pl.BlockSpec((nb, n1, cb), lambda b, j: (b, 0, j)),
        out_shape=jax.ShapeDtypeStruct((n2, n1, c), jnp.int32),
        compiler_params=_params(2),
        name="dft_stage1",
    )(f1, pq_t, pq_t)


def _dft2_kernel(l_ref, t_ref, o_ref, *, n2, cb):
    packed = t_ref[...]
    re = lax.bitcast_convert_type(packed & HIGH_HALF, F32)
    im = lax.bitcast_convert_type(lax.shift_left(packed, 16), F32)
    rhs = jnp.stack([re, im], axis=1).reshape(n2 * 2 * SUBLANES, cb).astype(BF16)
    o_ref[...] = jnp.dot(l_ref[0], rhs, preferred_element_type=F32).reshape(n2, SUBLANES, cb)


def _dft_stage2(t_nk, l2, n1, n2):
    c = t_nk.shape[-1]
    cb = min(DFT_CB, c)
    row_blocks = pl.BlockSpec((n2, SUBLANES, cb), lambda g, j: (0, g, j))
    out = pl.pallas_call(
        functools.partial(_dft2_kernel, n2=n2, cb=cb),
        grid=(n1 // SUBLANES, c // cb),
        in_specs=[pl.BlockSpec((1, SUBLANES * n2, SUBLANES * 2 * n2), lambda g, j: (g, 0, 0)),
                  row_blocks],
        out_specs=row_blocks,
        out_shape=jax.ShapeDtypeStruct((n2, n1, c), F32),
        compiler_params=_params(2),
        name="dft_stage2",
    )(l2, t_nk)
    return out.reshape(n1 * n2, c)


def _dft_matrices(s):
    n1, n2 = _dft_split(s)
    a = np.arange(n1)
    c1, s1 = _angles(a[:, None] * a[None, :], n1)
    f1 = jnp.asarray(np.block([[c1, -s1], [-s1, -c1]]).astype(BF16))
    groups = n1 // SUBLANES
    g = np.arange(groups)[:, None, None, None]
    k2 = np.arange(n2)[None, :, None, None]
    j = np.arange(SUBLANES)[None, None, :, None]
    nn = np.arange(n2)[None, None, None, :]
    c2, s2 = _angles((SUBLANES * g + j + n1 * k2) * nn, s)
    cs = np.concatenate([c2, s2], axis=3).astype(BF16)
    return n1, n2, f1, _expand_twiddles(jnp.asarray(cs.reshape(groups, n2 * SUBLANES, 2 * n2)))


def _expand_kernel(cs_ref, e_ref, o_ref):
    spread = jnp.dot(cs_ref[0], e_ref[...], preferred_element_type=F32)
    row = lax.broadcasted_iota(jnp.int32, spread.shape, 0)
    col = lax.broadcasted_iota(jnp.int32, spread.shape, 1)
    o_ref[0] = jnp.where(row % SUBLANES == col % SUBLANES, spread, 0.0).astype(BF16)


def _expand_twiddles(cs):
    groups, rows, cols = cs.shape
    wide = cols * SUBLANES
    src = jnp.arange(cols, dtype=jnp.int32)[:, None]
    slot = (src % (cols // 2)) * 2 + src // (cols // 2)
    e = jnp.arange(wide, dtype=jnp.int32)[None, :] // SUBLANES == slot
    return pl.pallas_call(
        _expand_kernel,
        grid=(groups,),
        in_specs=[pl.BlockSpec((1, rows, cols), lambda g: (g, 0, 0)),
                  pl.BlockSpec((cols, wide), lambda g: (0, 0))],
        out_specs=pl.BlockSpec((1, rows, wide), lambda g: (g, 0, 0)),
        out_shape=jax.ShapeDtypeStruct((groups, rows, wide), BF16),
        compiler_params=_params(1),
        name="expand_twiddles",
    )(cs, e.astype(BF16))


def kernel(x, c, norm_w, ada_w, ada_b, w_in_ab, w_pool, pool_scale, q_norm_w, k_norm_w, lambda_q1, lambda_k1,
           lambda_q2, lambda_k2, subln_w, w_out_ab, w_in_c, w_fourier, w_out_c):
    batch, s, d = x.shape
    assert batch == 1 and d == D_MODEL and s % ATT_TK == 0
    xs = x.reshape(s, d)
    mod = _ada_mod(c, ada_w, ada_b)
    n1, n2, f1, l2 = _dft_matrices(s)
    row = lambda v: v.reshape(1, -1)

    def mod_of(i):
        return mod[i, :, 0:d], mod[i, :, d:2 * d], mod[i, :, 2 * d:3 * d]

    def finish(ys, w_out, j, xs, i, gate_from=None):
        gate = mod_of(i)[2]
        if i + 1 == DEPTH:
            return _out_proj(ys, w_out, j, xs, gate, gate_from=gate_from), None
        shift, scale, _ = mod_of(i + 1)
        return _out_proj(ys, w_out, j, xs, gate, (row(norm_w[i + 1]), shift, scale), gate_from=gate_from)

    w_in_ab, w_out_ab, w_in_c, w_out_c = (w.astype(BF16) for w in (w_in_ab, w_out_ab, w_in_c, w_out_c))
    h = _modulate(xs, row(norm_w[0]), *mod_of(0)[:2])
    for i in range(DEPTH):
        j = i // 2
        if i % 2 == 0:
            o1, o2, o3, o4 = POOL_WIDTH, POOL_WIDTH + DIFF_WIDTH, POOL_WIDTH + 2 * DIFF_WIDTH, POOL_WIDTH + 3 * DIFF_WIDTH
            lambda_init = 0.8 - 0.6 * math.exp(-0.3 * i)
            ug = _matmul(h, w_in_ab, F32, PROJ_TM, PROJ_TN, name="pool_gate_in_proj",
                         cols=((0, o1), (o4, o4 + AB_WIDTH)), layer=j)
            qt, ka, vt = _qkv_proj(h, w_in_ab, j, o1, o2, o3, row(q_norm_w[j]), row(k_norm_w[j]))
            y_a = _pool_mix(ug, w_pool[j].astype(BF16), row(pool_scale[j]))
            y_b = _diff_attn(qt, ka, vt, row(lambda_q1[j]), row(lambda_k1[j]), row(lambda_q2[j]),
                             row(lambda_k2[j]), subln_w[j], ug, lambda_init)
            xs, h = finish([y_a, y_b], w_out_ab, j, xs, i)
        else:
            ab = _fold_channel_dft(w_fourier[j], s)
            w_pq = _fold_in_proj(w_in_c, j, ab)
            h_t = h.reshape(n1, n2, d).transpose(1, 0, 2).reshape(s, d)
            pq_t = _matmul(h_t, w_pq, BF16, PROJ_TM, PROJ_TN, name="fourier_in_proj")
            t_nk = _dft_stage1(pq_t, f1, n1, n2)
            f = _dft_stage2(t_nk, l2, n1, n2)
            xs, h = finish([f], w_out_c, j, xs, i, gate_from=(h, w_in_c, d))
    return xs.reshape(batch, s, d)
```

```python
import functools
import math

import jax
import jax.numpy as jnp
import numpy as np
from jax import lax
from jax.experimental import pallas as pl
from jax.experimental.pallas import tpu as pltpu

F32 = jnp.float32
BF16 = jnp.bfloat16

D_MODEL = 2048
DEPTH = 4
NORM_EPS = 1e-6

POOL_WINDOWS = (2, 4, 8, 16)
POOL_WIDTH = D_MODEL // 2
POOL_GROUP = POOL_WIDTH // len(POOL_WINDOWS)
POOL_HALO = 8

HEADS = 8
HEAD_DIM = 64
HEAD_V = 2 * HEAD_DIM
DIFF_WIDTH = HEADS * HEAD_V
AB_WIDTH = POOL_WIDTH + DIFF_WIDTH

FOURIER_GROUPS = 4
FOURIER_GROUP = D_MODEL // FOURIER_GROUPS

LANES = 128
SUBLANES = 8
POS_RADIX = 256
COEF_PARTS = 3
LOG2E = math.log2(math.e)

ATT_TQ = 1024
ATT_TK = 512
ATT_DIAG = ATT_TQ // ATT_TK
ATT_UNROLL = 4
REF_MARGIN = 60.0
REF_FLOOR = 2.0 ** -60
BOUND_SLACK = 1.001

PROJ_TM, PROJ_TN = 2048, 1024
OUT_SUB = 256
DFT_CB = 2048

VMEM_LIMIT = 56 * 1024 * 1024


def _params(n_axes):
    return pltpu.CompilerParams(dimension_semantics=("arbitrary",) * n_axes,
                                vmem_limit_bytes=VMEM_LIMIT)


def _mod_kernel(c_ref, w_ref, b_ref, o_ref):
    c = c_ref[...]
    o_ref[...] = jnp.sum(jax.nn.silu(c) * w_ref[...], axis=0, keepdims=True) + b_ref[...]


def _ada_mod(c, ada_w, ada_b):
    depth, d, n = ada_w.shape
    tn = 1024
    return pl.pallas_call(
        _mod_kernel,
        grid=(depth, n // tn),
        in_specs=[pl.BlockSpec((d, 1), lambda i, j: (0, 0)),
                  pl.BlockSpec((None, d, tn), lambda i, j: (i, 0, j)),
                  pl.BlockSpec((None, 1, tn), lambda i, j: (i, 0, j))],
        out_specs=pl.BlockSpec((None, 1, tn), lambda i, j: (i, 0, j)),
        out_shape=jax.ShapeDtypeStruct((depth, 1, n), F32),
        compiler_params=_params(2),
        name="ada_mod",
    )(c.reshape(d, 1), ada_w, ada_b.reshape(depth, 1, n))


def _modulate_kernel(x_ref, nw_ref, shift_ref, scale_ref, o_ref):
    xf = x_ref[...]
    r = lax.rsqrt(jnp.mean(xf * xf, axis=-1, keepdims=True) + NORM_EPS)
    y = xf * r * nw_ref[...]
    y = y * (1.0 + scale_ref[...]) + shift_ref[...]
    o_ref[...] = y.astype(o_ref.dtype)


def _modulate(x, nw, shift, scale):
    s, d = x.shape
    tm = min(1024, s)
    vec = pl.BlockSpec((1, d), lambda i: (0, 0))
    return pl.pallas_call(
        _modulate_kernel,
        grid=(s // tm,),
        in_specs=[pl.BlockSpec((tm, d), lambda i: (i, 0)), vec, vec, vec],
        out_specs=pl.BlockSpec((tm, d), lambda i: (i, 0)),
        out_shape=jax.ShapeDtypeStruct((s, d), BF16),
        compiler_params=_params(1),
        name="modulate",
    )(x, nw, shift, scale)


def _mm_kernel(a_ref, w_ref, o_ref, *, precision):
    o_ref[...] = jnp.dot(a_ref[...], w_ref[...], preferred_element_type=F32,
                         precision=precision).astype(o_ref.dtype)


def _matmul(a, w, out_dtype, tm, tn, precision=None, name="matmul", cols=None, layer=None):
    m, k = a.shape
    cols = cols if cols is not None else ((0, w.shape[-1]),)
    n = sum(stop - start for start, stop in cols)
    tm, tn = min(tm, m), min(tn, n)
    assert all(start % tn == 0 and stop % tn == 0 for start, stop in cols)

    def wblock(j):
        jb, first = j, 0
        for start, stop in cols:
            jb = jnp.where(j >= first, j - first + start // tn, jb)
            first += (stop - start) // tn
        return jb

    if layer is None:
        wspec = pl.BlockSpec((k, tn), lambda i, j: (0, wblock(j)))
    else:
        wspec = pl.BlockSpec((None, k, tn), lambda i, j: (layer, 0, wblock(j)))
    return pl.pallas_call(
        functools.partial(_mm_kernel, precision=precision),
        grid=(m // tm, n // tn),
        in_specs=[pl.BlockSpec((tm, k), lambda i, j: (i, 0)), wspec],
        out_specs=pl.BlockSpec((tm, tn), lambda i, j: (i, j)),
        out_shape=jax.ShapeDtypeStruct((m, n), out_dtype),
        compiler_params=_params(2),
        name=name,
    )(a, w)


def _qkv_kernel(h_ref, wq_ref, wk_ref, wv_ref, qnw_ref, knw_ref, qt_ref, ka_ref, vt_ref, *, tm):
    lane = lax.broadcasted_iota(jnp.int32, (ATT_TK, LANES), 1)
    first = lane < HEAD_DIM
    pos = lax.broadcasted_iota(jnp.int32, (ATT_TK, LANES), 0)
    pos_lo = (pos % POS_RADIX).astype(F32)
    pos_hi = (pos // POS_RADIX).astype(F32)
    pos_cols = jnp.where(lane < HEAD_DIM + COEF_PARTS, pos_lo, jnp.where(lane < HEAD_DIM + 2 * COEF_PARTS, pos_hi, 0.0))

    def halves_rms(z, w):
        sq = z * z
        ss1 = jnp.sum(jnp.where(first, sq, 0.0), axis=-1, keepdims=True)
        ss2 = jnp.sum(jnp.where(first, 0.0, sq), axis=-1, keepdims=True)
        r = jnp.where(first, lax.rsqrt(ss1 / HEAD_DIM + NORM_EPS), lax.rsqrt(ss2 / HEAD_DIM + NORM_EPS))
        return z * r * w

    for cc in range(tm // ATT_TK):
        rows = slice(cc * ATT_TK, (cc + 1) * ATT_TK)
        h = h_ref[rows, :]
        zq = jnp.dot(h, wq_ref[...], preferred_element_type=F32)
        zk = jnp.dot(h, wk_ref[...], preferred_element_type=F32)
        zv = jnp.dot(h, wv_ref[...], preferred_element_type=F32)
        for hh in range(2):
            cols = slice(hh * LANES, (hh + 1) * LANES)
            qn = halves_rms(zq[:, cols], qnw_ref[...]) * (HEAD_DIM ** -0.5) * LOG2E
            kn = halves_rms(zk[:, cols], knw_ref[...])
            for m in range(2):
                qm = qn if m == 0 else pltpu.roll(qn, HEAD_DIM, axis=1)
                km = kn if m == 0 else pltpu.roll(kn, HEAD_DIM, axis=1)
                qt_ref[hh, m, :, rows] = jnp.where(first, qm, 0.0).T.astype(BF16)
                ka_ref[hh, m, cc] = jnp.where(first, km, pos_cols).astype(BF16)
            vt_ref[hh, cc] = zv[:, cols].T.astype(BF16)


def _qkv_proj(h, w, layer, col_q, col_k, col_v, qnw, knw):
    s, d = h.shape
    tm = min(2048, s)
    nchunk = s // ATT_TK
    cpt = tm // ATT_TK
    tn = 2 * LANES
    assert col_q % tn == 0 and col_k % tn == 0 and col_v % tn == 0
    wspecs = [pl.BlockSpec((None, d, tn), functools.partial(lambda i, j, jb: (layer, 0, j + jb), jb=c0 // tn))
              for c0 in (col_q, col_k, col_v)]
    vec = pl.BlockSpec((1, LANES), lambda i, j: (0, 0))
    return pl.pallas_call(
        functools.partial(_qkv_kernel, tm=tm),
        grid=(s // tm, HEADS // 2),
        in_specs=[pl.BlockSpec((tm, d), lambda i, j: (i, 0)), *wspecs, vec, vec],
        out_specs=[pl.BlockSpec((2, 2, LANES, tm), lambda i, j: (j, 0, 0, i)),
                   pl.BlockSpec((2, 2, cpt, ATT_TK, LANES), lambda i, j: (j, 0, i, 0, 0)),
                   pl.BlockSpec((2, cpt, HEAD_V, ATT_TK), lambda i, j: (j, i, 0, 0))],
        out_shape=[jax.ShapeDtypeStruct((HEADS, 2, LANES, s), BF16),
                   jax.ShapeDtypeStruct((HEADS, 2, nchunk, ATT_TK, LANES), BF16),
                   jax.ShapeDtypeStruct((HEADS, nchunk, HEAD_V, ATT_TK), BF16)],
        compiler_params=_params(2),
        name="qkv_proj",
    )(h, w, w, w, jnp.tile(qnw, (1, 2)), jnp.tile(knw, (1, 2)))


def _attn_kernel(slopes_ref, qt_ref, ka_ref, vt_ref, lq1_ref, lk1_ref, lq2_ref, lk2_ref, subw_ref, g_ref,
                 o_ref, acc_ref, mu_ref, qv_ref, kmax_ref, p_a, p_b, den_ref, *, lambda_init, nchunk, unroll):
    hd = pl.program_id(0)
    i = pl.program_id(1)
    base = hd * (1 + COEF_PARTS)
    slope = slopes_ref[base]
    q_off = lax.broadcasted_iota(jnp.int32, (1, ATT_TQ), 1).astype(F32)

    row1 = lax.broadcasted_iota(jnp.int32, (LANES - HEAD_DIM, 1), 0)
    coef_col = jnp.zeros((LANES - HEAD_DIM, 1), F32)
    for t in range(COEF_PARTS):
        piece = slopes_ref[base + 1 + t]
        coef_col = jnp.where(row1 == t, piece, coef_col)
        coef_col = jnp.where(row1 == COEF_PARTS + t, piece * POS_RADIX, coef_col)
    coef_rows = jnp.broadcast_to(coef_col, (LANES - HEAD_DIM, ATT_TQ))
    for m in range(2):
        for kind, sign in ((0, -1.0), (1, 0.0), (2, 1.0)):
            qv_ref[kind, m, 0:HEAD_DIM, :] = qt_ref[0, m, 0:HEAD_DIM, :]
            qv_ref[kind, m, HEAD_DIM:LANES, :] = (sign * coef_rows).astype(BF16)

    @pl.when(i == 0)
    def _():
        lane = lax.broadcasted_iota(jnp.int32, (ATT_TK, LANES), 1)
        for m in range(2):
            def widest(c, best):
                kc = ka_ref[0, m, c].astype(F32)
                return jnp.maximum(best, jnp.sum(jnp.where(lane < HEAD_DIM, kc * kc, 0.0), axis=1, keepdims=True))
            kmax_ref[m] = jnp.max(lax.fori_loop(0, nchunk, widest, jnp.zeros((ATT_TK, 1), F32)))

    for m in range(2):
        qf = qt_ref[0, m, 0:HEAD_DIM, :].astype(F32)
        bound = jnp.sqrt(jnp.sum(qf * qf, axis=0, keepdims=True) * kmax_ref[m]) * BOUND_SLACK
        mu_ref[m] = bound - REF_MARGIN

    c_diag = i * ATT_DIAG

    def chunk_of(e):
        if isinstance(e, int) and e < ATT_DIAG:
            return c_diag + e
        rest = e - ATT_DIAG
        return jnp.where(e < ATT_DIAG, c_diag + e, rest + ATT_DIAG * (rest >= c_diag).astype(jnp.int32))

    def side_of(c):
        return (c < c_diag).astype(jnp.int32) - (c >= c_diag + ATT_DIAG).astype(jnp.int32)

    p_bufs = (p_a, p_b)

    def probs(e, p_ref):
        c = chunk_of(e)
        side = side_of(c)
        sidef = side.astype(F32)
        gap = jnp.abs(c * ATT_TK - i * ATT_TQ).astype(F32)
        cvec = -slope * (sidef * sidef * gap + sidef * q_off)
        for m in range(2):
            st = jnp.dot(ka_ref[0, m, c], qv_ref[1 + side, m], preferred_element_type=F32)
            if isinstance(e, int) and e < ATT_DIAG:
                key_off = lax.broadcasted_iota(jnp.int32, (ATT_TK, ATT_TQ), 0) + e * ATT_TK
                qry_off = lax.broadcasted_iota(jnp.int32, (ATT_TK, ATT_TQ), 1)
                st = st - slope * jnp.abs(key_off - qry_off).astype(F32)
            p = jnp.exp2(st - (mu_ref[m] - cvec))
            p_ref[m] = p.astype(BF16)
            den_ref[m] += jnp.sum(p.reshape(ATT_TK // SUBLANES, SUBLANES, ATT_TQ), axis=0)

    def group(e0, count, probs_last):
        sums = [None, None]
        for u in range(count):
            if u < count - 1 or probs_last:
                probs(e0 + u + 1, p_bufs[(u + 1) % 2])
            c = chunk_of(e0 + u)
            for m in range(2):
                pv = jnp.dot(vt_ref[0, c], p_bufs[u % 2][m], preferred_element_type=F32)
                sums[m] = pv if sums[m] is None else sums[m] + pv
        for m in range(2):
            acc_ref[m] += sums[m]

    def loop_body(it, carry):
        group(it * unroll, unroll, True)
        return carry

    def one_pass(_):
        acc_ref[...] = jnp.zeros(acc_ref.shape, F32)
        den_ref[...] = jnp.zeros(den_ref.shape, F32)
        n_groups = nchunk // unroll
        probs(0, p_a)
        group(0, unroll, True)
        lax.fori_loop(1, n_groups - 1, loop_body, 0)
        group((n_groups - 1) * unroll, unroll, False)
        lowest = None
        for m in range(2):
            den = jnp.sum(den_ref[m], axis=0, keepdims=True)
            mu_ref[m] = jnp.where(den < REF_FLOOR, mu_ref[m] - 2.0 * REF_MARGIN, mu_ref[m])
            lowest = jnp.min(den) if lowest is None else jnp.minimum(lowest, jnp.min(den))
        return (lowest < REF_FLOOR).astype(jnp.int32)

    lax.while_loop(lambda retry: retry > 0, one_pass, jnp.int32(1))

    lam = (jnp.exp(jnp.sum(lq1_ref[...] * lk1_ref[...], axis=-1, keepdims=True))
           - jnp.exp(jnp.sum(lq2_ref[...] * lk2_ref[...], axis=-1, keepdims=True)) + lambda_init)
    den1 = jnp.sum(den_ref[0], axis=0, keepdims=True)
    den2 = jnp.sum(den_ref[1], axis=0, keepdims=True)
    o = acc_ref[0] / den1 - lam * (acc_ref[1] / den2)
    r = lax.rsqrt(jnp.mean(o * o, axis=0, keepdims=True) + NORM_EPS)
    y = (o * r * subw_ref[...]) * (1.0 - lambda_init)
    o_ref[...] = (y.T * jax.nn.silu(g_ref[...])).astype(o_ref.dtype)


def _diff_attn(qt, ka, vt, lq1, lk1, lq2, lk2, subw, g, lambda_init):
    s = qt.shape[-1]
    g_col0 = (g.shape[1] - DIFF_WIDTH) // HEAD_V
    nchunk = s // ATT_TK
    unroll = min(ATT_UNROLL, nchunk // 2)
    assert unroll % 2 == 0 and nchunk % unroll == 0 and unroll >= ATT_DIAG and s % ATT_TQ == 0
    whole = jnp.asarray([LOG2E * 2.0 ** (-8.0 * (h + 1) / HEADS) for h in range(HEADS)], F32)
    pieces, rest = [], whole
    for _ in range(COEF_PARTS):
        pieces.append(rest.astype(BF16).astype(F32))
        rest = rest - pieces[-1]
    slopes = jnp.stack([whole] + pieces, axis=1).reshape(-1)
    vec = pl.BlockSpec((1, HEAD_DIM), lambda h, i: (0, 0))
    return pl.pallas_call(
        functools.partial(_attn_kernel, lambda_init=lambda_init, nchunk=nchunk, unroll=unroll),
        grid=(HEADS, s // ATT_TQ),
        in_specs=[pl.BlockSpec(memory_space=pltpu.SMEM),
                  pl.BlockSpec((1, 2, LANES, ATT_TQ), lambda h, i: (h, 0, 0, i)),
                  pl.BlockSpec((1, 2, nchunk, ATT_TK, LANES), lambda h, i: (h, 0, 0, 0, 0)),
                  pl.BlockSpec((1, nchunk, HEAD_V, ATT_TK), lambda h, i: (h, 0, 0, 0)),
                  vec, vec, vec, vec,
                  pl.BlockSpec((HEAD_V, 1), lambda h, i: (0, 0)),
                  pl.BlockSpec((ATT_TQ, HEAD_V), lambda h, i: (i, g_col0 + h))],
        out_specs=pl.BlockSpec((ATT_TQ, HEAD_V), lambda h, i: (i, h)),
        out_shape=jax.ShapeDtypeStruct((s, DIFF_WIDTH), BF16),
        scratch_shapes=[pltpu.VMEM((2, HEAD_V, ATT_TQ), F32), pltpu.VMEM((2, 1, ATT_TQ), F32),
                        pltpu.VMEM((3, 2, LANES, ATT_TQ), BF16), pltpu.SMEM((2,), F32),
                        pltpu.VMEM((2, ATT_TK, ATT_TQ), BF16), pltpu.VMEM((2, ATT_TK, ATT_TQ), BF16),
                        pltpu.VMEM((2, SUBLANES, ATT_TQ), F32)],
        compiler_params=_params(2),
        name="diff_attn",
    )(slopes, qt, ka, vt, lq1, lk1, lq2, lk2, subw.reshape(HEAD_V, 1), g)


def _pool_kernel(prev_ref, cur_ref, next_ref, w_ref, scale_ref, g_ref, o_ref, ext_ref, *, tm, seq):
    i = pl.program_id(0)
    last = pl.num_programs(0) - 1
    zero_halo = jnp.zeros((POOL_HALO, POOL_WIDTH), F32)
    ext_ref[0:POOL_HALO] = jnp.where(i > 0, prev_ref[...], zero_halo)
    ext_ref[POOL_HALO:POOL_HALO + tm] = cur_ref[...]
    ext_ref[POOL_HALO + tm:POOL_HALO + tm + POOL_HALO] = jnp.where(i < last, next_ref[...], zero_halo)
    t = lax.broadcasted_iota(jnp.int32, (tm, 1), 0) + i * tm
    for g, w in enumerate(POOL_WINDOWS):
        cols = pl.ds(g * POOL_GROUP, POOL_GROUP)
        span = tm + 2 * POOL_HALO
        width = 1
        while width < w:
            span -= width
            ext_ref[pl.ds(0, span), cols] = ext_ref[pl.ds(0, span), cols] + ext_ref[pl.ds(width, span), cols]
            width *= 2
        assert width == w
        win = ext_ref[pl.ds(POOL_HALO - w // 2, tm), cols]
        lo = jnp.maximum(t - w // 2, 0)
        hi = jnp.minimum(t + (w - w // 2) - 1, seq - 1)
        cnt = (hi - lo + 1).astype(F32)
        pooled = win * (1.0 / cnt) - cur_ref[:, cols]
        y = jnp.dot(pooled.astype(BF16), w_ref[g], preferred_element_type=F32)
        o_ref[:, cols] = (y * scale_ref[:, cols] * jax.nn.silu(g_ref[:, cols])).astype(o_ref.dtype)


def _pool_mix(ug, w_pool, pool_scale):
    s = ug.shape[0]
    width = POOL_WIDTH
    tm = 512
    hb = tm // POOL_HALO
    nhalo = s // POOL_HALO
    return pl.pallas_call(
        functools.partial(_pool_kernel, tm=tm, seq=s),
        grid=(s // tm,),
        in_specs=[pl.BlockSpec((POOL_HALO, width), lambda i: (jnp.maximum(i * hb - 1, 0), 0)),
                  pl.BlockSpec((tm, width), lambda i: (i, 0)),
                  pl.BlockSpec((POOL_HALO, width), lambda i: (jnp.minimum((i + 1) * hb, nhalo - 1), 0)),
                  pl.BlockSpec(w_pool.shape, lambda i: (0, 0, 0)),
                  pl.BlockSpec((1, width), lambda i: (0, 0)),
                  pl.BlockSpec((tm, width), lambda i: (i, 1))],
        out_specs=pl.BlockSpec((tm, width), lambda i: (i, 0)),
        out_shape=jax.ShapeDtypeStruct((s, width), BF16),
        scratch_shapes=[pltpu.VMEM((tm + 2 * POOL_HALO, width), F32)],
        compiler_params=_params(1),
        name="pool_mix",
    )(ug, ug, ug, w_pool, pool_scale, ug)


def _out_kernel(*refs, widths, tm, modulate_next, gate_proj):
    n = len(widths)
    y_refs = refs[:n]
    refs = refs[n:]
    if gate_proj:
        hin_ref, wg_ref = refs[:2]
        refs = refs[2:]
    w_ref, x_ref, gate_ref = refs[:3]
    if modulate_next:
        nw_ref, shift_ref, scale_ref, o_ref, h_ref = refs[3:]
        mul = nw_ref[...]
        one_plus_scale = 1.0 + scale_ref[...]
    else:
        (o_ref,) = refs[3:]
    sub = min(OUT_SUB, tm // 2)
    for sb in range(tm // sub):
        rows = slice(sb * sub, (sb + 1) * sub)
        acc = None
        off = 0
        for y_ref, wd in zip(y_refs, widths):
            y = y_ref[rows, :]
            if gate_proj:
                y = y * jax.nn.silu(jnp.dot(hin_ref[rows, :], wg_ref[:, off:off + wd], preferred_element_type=F32))
            part = jnp.dot(y.astype(BF16), w_ref[off:off + wd, :], preferred_element_type=F32)
            acc = part if acc is None else acc + part
            off += wd
        xn = x_ref[rows, :] + gate_ref[...] * acc
        o_ref[rows, :] = xn
        if modulate_next:
            r = lax.rsqrt(jnp.mean(xn * xn, axis=-1, keepdims=True) + NORM_EPS)
            h_ref[rows, :] = ((xn * r * mul) * one_plus_scale + shift_ref[...]).astype(BF16)


def _out_proj(ys, w, layer, x, gate, next_mod=None, gate_from=None):
    s, d = x.shape
    k = w.shape[1]
    gate_proj = gate_from is not None
    tm = min(512, s)
    once = pl.Buffered(1) if gate_proj else None
    widths = tuple(y.shape[1] for y in ys)
    vec = pl.BlockSpec((1, d), lambda i: (0, 0))
    rows = pl.BlockSpec((tm, d), lambda i: (i, 0))
    modulate_next = next_mod is not None
    gate_specs, gate_args = [], ()
    if gate_proj:
        hin, wg, col0 = gate_from
        assert col0 % k == 0 and sum(widths) == k
        gate_specs = [pl.BlockSpec((tm, hin.shape[1]), lambda i: (i, 0)),
                      pl.BlockSpec((None, wg.shape[1], k), lambda i: (layer, 0, col0 // k), pipeline_mode=once)]
        gate_args = (hin, wg)
    return pl.pallas_call(
        functools.partial(_out_kernel, widths=widths, tm=tm, modulate_next=modulate_next, gate_proj=gate_proj),
        grid=(s // tm,),
        in_specs=[pl.BlockSpec((tm, wd), lambda i: (i, 0)) for wd in widths] + gate_specs
        + [pl.BlockSpec((None, k, d), lambda i: (layer, 0, 0), pipeline_mode=once), rows, vec]
        + ([vec, vec, vec] if modulate_next else []),
        out_specs=[rows, rows] if modulate_next else rows,
        out_shape=([jax.ShapeDtypeStruct((s, d), F32), jax.ShapeDtypeStruct((s, d), BF16)] if modulate_next
                   else jax.ShapeDtypeStruct((s, d), F32)),
        compiler_params=_params(1),
        name="out_proj",
    )(*ys, *gate_args, w, x, gate, *(next_mod if modulate_next else ()))


def _dft_split(s):
    n1 = 1 << (int(math.log2(s)) // 2)
    n2 = s // n1
    assert n1 * n2 == s and n1 % SUBLANES == 0 and n2 % SUBLANES == 0
    return n1, n2


def _angles(num, den):
    ang = (2.0 * math.pi / den) * (num % den)
    return np.cos(ang), np.sin(ang)


def _fold_cs_kernel(cs_ref, wf_ref, o_ref):
    wf = wf_ref[...]
    o_ref[:, 0:FOURIER_GROUP] = jnp.dot(cs_ref[0], wf, preferred_element_type=F32,
                                        precision=lax.Precision.HIGHEST)
    o_ref[:, FOURIER_GROUP:] = jnp.dot(cs_ref[1], wf, preferred_element_type=F32,
                                       precision=lax.Precision.HIGHEST)


def _fold_channel_dft(w_fourier_j, seq):
    fg = FOURIER_GROUP
    idx = np.arange(fg)
    cc, sc = _angles(idx[:, None] * idx[None, :], fg)
    cs = jnp.asarray(np.stack([cc, sc]) / math.sqrt(seq * fg), F32)
    return pl.pallas_call(
        _fold_cs_kernel,
        grid=(FOURIER_GROUPS,),
        in_specs=[pl.BlockSpec((2, fg, fg), lambda g: (0, 0, 0)),
                  pl.BlockSpec((None, fg, fg), lambda g: (g, 0, 0))],
        out_specs=pl.BlockSpec((None, fg, 2 * fg), lambda g: (g, 0, 0)),
        out_shape=jax.ShapeDtypeStruct((FOURIER_GROUPS, fg, 2 * fg), F32),
        compiler_params=_params(1),
        name="fold_channel_dft",
    )(cs, w_fourier_j)


def _fold_win_kernel(w_ref, ab_ref, o_ref):
    o_ref[...] = jnp.dot(w_ref[...], ab_ref[...].astype(BF16), preferred_element_type=F32).astype(BF16)


def _fold_in_proj(w_in, layer, ab):
    d = w_in.shape[1]
    fg = FOURIER_GROUP
    return pl.pallas_call(
        _fold_win_kernel,
        grid=(FOURIER_GROUPS, 2),
        in_specs=[pl.BlockSpec((None, d, fg), lambda g, t: (layer, 0, g)),
                  pl.BlockSpec((None, fg, fg), lambda g, t: (g, 0, t))],
        out_specs=pl.BlockSpec((d, fg), lambda g, t: (0, g + t * FOURIER_GROUPS)),
        out_shape=jax.ShapeDtypeStruct((d, 2 * D_MODEL), BF16),
        compiler_params=_params(2),
        name="fold_in_proj",
    )(w_in, ab)


HIGH_HALF = -65536


def _bf16_bits(v):
    b = lax.bitcast_convert_type(v, jnp.int32)
    return b + 0x7FFF + (lax.shift_right_logical(b, 16) & 1)


def _dft1_kernel(f_ref, p_ref, q_ref, t_ref, *, nb, n1):
    f = f_ref[...]
    for jj in range(nb):
        rows = slice(jj * n1, (jj + 1) * n1)
        rhs = jnp.concatenate([p_ref[rows, :], q_ref[rows, :]], axis=0)
        t = jnp.dot(f, rhs, preferred_element_type=F32)
        t_ref[jj] = (_bf16_bits(t[0:n1]) & HIGH_HALF) | lax.shift_right_logical(_bf16_bits(t[n1:2 * n1]), 16)


def _dft_stage1(pq_t, f1, n1, n2):
    s, c2 = pq_t.shape
    c = c2 // 2
    nb = SUBLANES
    cb = min(DFT_CB, c)
    ncb = c // cb
    return pl.pallas_call(
        functools.partial(_dft1_kernel, nb=nb, n1=n1),
        grid=(n2 // nb, ncb),
        in_specs=[pl.BlockSpec((2 * n1, 2 * n1), lambda b, j: (0, 0)),
                  pl.BlockSpec((nb * n1, cb), lambda b, j: (b, j)),
                  pl.BlockSpec((nb * n1, cb), lambda b, j: (b, j + ncb))],
        out_specs=pl.BlockSpec((nb, n1, cb), lambda b, j: (b, 0, j)),
        out_shape=jax.ShapeDtypeStruct((n2, n1, c), jnp.int32),
        compiler_params=_params(2),
        name="dft_stage1",
    )(f1, pq_t, pq_t)


def _dft2_kernel(l_ref, t_ref, o_ref, *, n2, cb):
    packed = t_ref[...]
    re = lax.bitcast_convert_type(packed & HIGH_HALF, F32)
    im = lax.bitcast_convert_type(lax.shift_left(packed, 16), F32)
    rhs = jnp.stack([re, im], axis=1).reshape(n2 * 2 * SUBLANES, cb).astype(BF16)
    o_ref[...] = jnp.dot(l_ref[0], rhs, preferred_element_type=F32).reshape(n2, SUBLANES, cb)


def _dft_stage2(t_nk, l2, n1, n2):
    c = t_nk.shape[-1]
    cb = min(DFT_CB, c)
    row_blocks = pl.BlockSpec((n2, SUBLANES, cb), lambda g, j: (0, g, j))
    out = pl.pallas_call(
        functools.partial(_dft2_kernel, n2=n2, cb=cb),
        grid=(n1 // SUBLANES, c // cb),
        in_specs=[pl.BlockSpec((1, SUBLANES * n2, SUBLANES * 2 * n2), lambda g, j: (g, 0, 0)),
                  row_blocks],
        out_specs=row_blocks,
        out_shape=jax.ShapeDtypeStruct((n2, n1, c), F32),
        compiler_params=_params(2),
        name="dft_stage2",
    )(l2, t_nk)
    return out.reshape(n1 * n2, c)


def _dft_matrices(s):
    n1, n2 = _dft_split(s)
    a = np.arange(n1)
    c1, s1 = _angles(a[:, None] * a[None, :], n1)
    f1 = jnp.asarray(np.block([[c1, -s1], [-s1, -c1]]).astype(BF16))
    groups = n1 // SUBLANES
    g = np.arange(groups)[:, None, None, None]
    k2 = np.arange(n2)[None, :, None, None]
    j = np.arange(SUBLANES)[None, None, :, None]
    nn = np.arange(n2)[None, None, None, :]
    c2, s2 = _angles((SUBLANES * g + j + n1 * k2) * nn, s)
    cs = np.concatenate([c2, s2], axis=3).astype(BF16)
    return n1, n2, f1, _expand_twiddles(jnp.asarray(cs.reshape(groups, n2 * SUBLANES, 2 * n2)))


def _expand_kernel(cs_ref, e_ref, o_ref):
    spread = jnp.dot(cs_ref[0], e_ref[...], preferred_element_type=F32)
    row = lax.broadcasted_iota(jnp.int32, spread.shape, 0)
    col = lax.broadcasted_iota(jnp.int32, spread.shape, 1)
    o_ref[0] = jnp.where(row % SUBLANES == col % SUBLANES, spread, 0.0).astype(BF16)


def _expand_twiddles(cs):
    groups, rows, cols = cs.shape
    wide = cols * SUBLANES
    src = jnp.arange(cols, dtype=jnp.int32)[:, None]
    slot = (src % (cols // 2)) * 2 + src // (cols // 2)
    e = jnp.arange(wide, dtype=jnp.int32)[None, :] // SUBLANES == slot
    return pl.pallas_call(
        _expand_kernel,
        grid=(groups,),
        in_specs=[pl.BlockSpec((1, rows, cols), lambda g: (g, 0, 0)),
                  pl.BlockSpec((cols, wide), lambda g: (0, 0))],
        out_specs=pl.BlockSpec((1, rows, wide), lambda g: (g, 0, 0)),
        out_shape=jax.ShapeDtypeStruct((groups, rows, wide), BF16),
        compiler_params=_params(1),
        name="expand_twiddles",
    )(cs, e.astype(BF16))


def kernel(x, c, norm_w, ada_w, ada_b, w_in_ab, w_pool, pool_scale, q_norm_w, k_norm_w, lambda_q1, lambda_k1,
           lambda_q2, lambda_k2, subln_w, w_out_ab, w_in_c, w_fourier, w_out_c):
    batch, s, d = x.shape
    assert batch == 1 and d == D_MODEL and s % ATT_TK == 0
    xs = x.reshape(s, d)
    mod = _ada_mod(c, ada_w, ada_b)
    n1, n2, f1, l2 = _dft_matrices(s)
    row = lambda v: v.reshape(1, -1)

    def mod_of(i):
        return mod[i, :, 0:d], mod[i, :, d:2 * d], mod[i, :, 2 * d:3 * d]

    def finish(ys, w_out, j, xs, i, gate_from=None):
        gate = mod_of(i)[2]
        if i + 1 == DEPTH:
            return _out_proj(ys, w_out, j, xs, gate, gate_from=gate_from), None
        shift, scale, _ = mod_of(i + 1)
        return _out_proj(ys, w_out, j, xs, gate, (row(norm_w[i + 1]), shift, scale), gate_from=gate_from)

    w_in_ab, w_out_ab, w_in_c, w_out_c = (w.astype(BF16) for w in (w_in_ab, w_out_ab, w_in_c, w_out_c))
    h = _modulate(xs, row(norm_w[0]), *mod_of(0)[:2])
    for i in range(DEPTH):
        j = i // 2
        if i % 2 == 0:
            o1, o2, o3, o4 = POOL_WIDTH, POOL_WIDTH + DIFF_WIDTH, POOL_WIDTH + 2 * DIFF_WIDTH, POOL_WIDTH + 3 * DIFF_WIDTH
            lambda_init = 0.8 - 0.6 * math.exp(-0.3 * i)
            ug = _matmul(h, w_in_ab, F32, PROJ_TM, PROJ_TN, name="pool_gate_in_proj",
                         cols=((0, o1), (o4, o4 + AB_WIDTH)), layer=j)
            qt, ka, vt = _qkv_proj(h, w_in_ab, j, o1, o2, o3, row(q_norm_w[j]), row(k_norm_w[j]))
            y_a = _pool_mix(ug, w_pool[j].astype(BF16), row(pool_scale[j]))
            y_b = _diff_attn(qt, ka, vt, row(lambda_q1[j]), row(lambda_k1[j]), row(lambda_q2[j]),
                             row(lambda_k2[j]), subln_w[j], ug, lambda_init)
            xs, h = finish([y_a, y_b], w_out_ab, j, xs, i)
        else:
            ab = _fold_channel_dft(w_fourier[j], s)
            w_pq = _fold_in_proj(w_in_c, j, ab)
            h_t = h.reshape(n1, n2, d).transpose(1, 0, 2).reshape(s, d)
            pq_t = _matmul(h_t, w_pq, BF16, PROJ_TM, PROJ_TN, name="fourier_in_proj")
            t_nk = _dft_stage1(pq_t, f1, n1, n2)
            f = _dft_stage2(t_nk, l2, n1, n2)
            xs, h = finish([f], w_out_c, j, xs, i, gate_from=(h, w_in_c, d))
    return xs.reshape(batch, s, d)
```

```python
import functools
import math

import jax
import jax.numpy as jnp
import numpy as np
from jax import lax
from jax.experimental import pallas as pl
from jax.experimental.pallas import tpu as pltpu

F32 = jnp.float32
BF16 = jnp.bfloat16

D_MODEL = 2048
DEPTH = 4
NORM_EPS = 1e-6

POOL_WINDOWS = (2, 4, 8, 16)
POOL_WIDTH = D_MODEL // 2
POOL_GROUP = POOL_WIDTH // len(POOL_WINDOWS)
POOL_HALO = 8

HEADS = 8
HEAD_DIM = 64
HEAD_V = 2 * HEAD_DIM
DIFF_WIDTH = HEADS * HEAD_V
AB_WIDTH = POOL_WIDTH + DIFF_WIDTH

FOURIER_GROUPS = 4
FOURIER_GROUP = D_MODEL // FOURIER_GROUPS

LANES = 128
SUBLANES = 8
POS_RADIX = 256
COEF_PARTS = 3
LOG2E = math.log2(math.e)

ATT_TQ = 512
ATT_TK = 512
ATT_DIAG = ATT_TQ // ATT_TK
ATT_UNROLL = 8
REF_MARGIN = 60.0
REF_FLOOR = 2.0 ** -60
BOUND_SLACK = 1.001

PROJ_TM, PROJ_TN = 2048, 1024
OUT_SUB = 256
DFT_CB = 2048

VMEM_LIMIT = 56 * 1024 * 1024


def _params(n_axes):
    return pltpu.CompilerParams(dimension_semantics=("arbitrary",) * n_axes,
                                vmem_limit_bytes=VMEM_LIMIT)


def _mod_kernel(c_ref, w_ref, b_ref, o_ref):
    c = c_ref[...]
    o_ref[...] = jnp.sum(jax.nn.silu(c) * w_ref[...], axis=0, keepdims=True) + b_ref[...]


def _ada_mod(c, ada_w, ada_b):
    depth, d, n = ada_w.shape
    tn = 1024
    return pl.pallas_call(
        _mod_kernel,
        grid=(depth, n // tn),
        in_specs=[pl.BlockSpec((d, 1), lambda i, j: (0, 0)),
                  pl.BlockSpec((None, d, tn), lambda i, j: (i, 0, j)),
                  pl.BlockSpec((None, 1, tn), lambda i, j: (i, 0, j))],
        out_specs=pl.BlockSpec((None, 1, tn), lambda i, j: (i, 0, j)),
        out_shape=jax.ShapeDtypeStruct((depth, 1, n), F32),
        compiler_params=_params(2),
        name="ada_mod",
    )(c.reshape(d, 1), ada_w, ada_b.reshape(depth, 1, n))


def _modulate_kernel(x_ref, nw_ref, shift_ref, scale_ref, o_ref):
    xf = x_ref[...]
    r = lax.rsqrt(jnp.mean(xf * xf, axis=-1, keepdims=True) + NORM_EPS)
    y = xf * r * nw_ref[...]
    y = y * (1.0 + scale_ref[...]) + shift_ref[...]
    o_ref[...] = y.astype(o_ref.dtype)


def _modulate(x, nw, shift, scale):
    s, d = x.shape
    tm = min(1024, s)
    vec = pl.BlockSpec((1, d), lambda i: (0, 0))
    return pl.pallas_call(
        _modulate_kernel,
        grid=(s // tm,),
        in_specs=[pl.BlockSpec((tm, d), lambda i: (i, 0)), vec, vec, vec],
        out_specs=pl.BlockSpec((tm, d), lambda i: (i, 0)),
        out_shape=jax.ShapeDtypeStruct((s, d), BF16),
        compiler_params=_params(1),
        name="modulate",
    )(x, nw, shift, scale)


def _mm_kernel(a_ref, w_ref, o_ref, *, precision):
    o_ref[...] = jnp.dot(a_ref[...], w_ref[...], preferred_element_type=F32,
                         precision=precision).astype(o_ref.dtype)


def _matmul(a, w, out_dtype, tm, tn, precision=None, name="matmul", cols=None, layer=None):
    m, k = a.shape
    cols = cols if cols is not None else ((0, w.shape[-1]),)
    n = sum(stop - start for start, stop in cols)
    tm, tn = min(tm, m), min(tn, n)
    assert all(start % tn == 0 and stop % tn == 0 for start, stop in cols)

    def wblock(j):
        jb, first = j, 0
        for start, stop in cols:
            jb = jnp.where(j >= first, j - first + start // tn, jb)
            first += (stop - start) // tn
        return jb

    if layer is None:
        wspec = pl.BlockSpec((k, tn), lambda i, j: (0, wblock(j)))
    else:
        wspec = pl.BlockSpec((None, k, tn), lambda i, j: (layer, 0, wblock(j)))
    return pl.pallas_call(
        functools.partial(_mm_kernel, precision=precision),
        grid=(m // tm, n // tn),
        in_specs=[pl.BlockSpec((tm, k), lambda i, j: (i, 0)), wspec],
        out_specs=pl.BlockSpec((tm, tn), lambda i, j: (i, j)),
        out_shape=jax.ShapeDtypeStruct((m, n), out_dtype),
        compiler_params=_params(2),
        name=name,
    )(a, w)


def _qkv_kernel(h_ref, wq_ref, wk_ref, wv_ref, qnw_ref, knw_ref, qt_ref, ka_ref, vt_ref, *, tm):
    lane = lax.broadcasted_iota(jnp.int32, (ATT_TK, LANES), 1)
    first = lane < HEAD_DIM
    pos = lax.broadcasted_iota(jnp.int32, (ATT_TK, LANES), 0)
    pos_lo = (pos % POS_RADIX).astype(F32)
    pos_hi = (pos // POS_RADIX).astype(F32)
    pos_cols = jnp.where(lane < HEAD_DIM + COEF_PARTS, pos_lo, jnp.where(lane < HEAD_DIM + 2 * COEF_PARTS, pos_hi, 0.0))

    def halves_rms(z, w):
        sq = z * z
        ss1 = jnp.sum(jnp.where(first, sq, 0.0), axis=-1, keepdims=True)
        ss2 = jnp.sum(jnp.where(first, 0.0, sq), axis=-1, keepdims=True)
        r = jnp.where(first, lax.rsqrt(ss1 / HEAD_DIM + NORM_EPS), lax.rsqrt(ss2 / HEAD_DIM + NORM_EPS))
        return z * r * w

    for cc in range(tm // ATT_TK):
        rows = slice(cc * ATT_TK, (cc + 1) * ATT_TK)
        h = h_ref[rows, :]
        zq = jnp.dot(h, wq_ref[...], preferred_element_type=F32)
        zk = jnp.dot(h, wk_ref[...], preferred_element_type=F32)
        zv = jnp.dot(h, wv_ref[...], preferred_element_type=F32)
        for hh in range(2):
            cols = slice(hh * LANES, (hh + 1) * LANES)
            qn = halves_rms(zq[:, cols], qnw_ref[...]) * (HEAD_DIM ** -0.5) * LOG2E
            kn = halves_rms(zk[:, cols], knw_ref[...])
            for m in range(2):
                qm = qn if m == 0 else pltpu.roll(qn, HEAD_DIM, axis=1)
                km = kn if m == 0 else pltpu.roll(kn, HEAD_DIM, axis=1)
                qt_ref[hh, m, :, rows] = jnp.where(first, qm, 0.0).T.astype(BF16)
                ka_ref[hh, m, cc] = jnp.where(first, km, pos_cols).astype(BF16)
            vt_ref[hh, cc] = zv[:, cols].T.astype(BF16)


def _qkv_proj(h, w, layer, col_q, col_k, col_v, qnw, knw):
    s, d = h.shape
    tm = min(2048, s)
    nchunk = s // ATT_TK
    cpt = tm // ATT_TK
    tn = 2 * LANES
    assert col_q % tn == 0 and col_k % tn == 0 and col_v % tn == 0
    wspecs = [pl.BlockSpec((None, d, tn), functools.partial(lambda i, j, jb: (layer, 0, j + jb), jb=c0 // tn))
              for c0 in (col_q, col_k, col_v)]
    vec = pl.BlockSpec((1, LANES), lambda i, j: (0, 0))
    return pl.pallas_call(
        functools.partial(_qkv_kernel, tm=tm),
        grid=(s // tm, HEADS // 2),
        in_specs=[pl.BlockSpec((tm, d), lambda i, j: (i, 0)), *wspecs, vec, vec],
        out_specs=[pl.BlockSpec((2, 2, LANES, tm), lambda i, j: (j, 0, 0, i)),
                   pl.BlockSpec((2, 2, cpt, ATT_TK, LANES), lambda i, j: (j, 0, i, 0, 0)),
                   pl.BlockSpec((2, cpt, HEAD_V, ATT_TK), lambda i, j: (j, i, 0, 0))],
        out_shape=[jax.ShapeDtypeStruct((HEADS, 2, LANES, s), BF16),
                   jax.ShapeDtypeStruct((HEADS, 2, nchunk, ATT_TK, LANES), BF16),
                   jax.ShapeDtypeStruct((HEADS, nchunk, HEAD_V, ATT_TK), BF16)],
        compiler_params=_params(2),
        name="qkv_proj",
    )(h, w, w, w, jnp.tile(qnw, (1, 2)), jnp.tile(knw, (1, 2)))


def _attn_kernel(slopes_ref, qt_ref, ka_ref, vt_ref, lq1_ref, lk1_ref, lq2_ref, lk2_ref, subw_ref, g_ref,
                 o_ref, acc_ref, mu_ref, qv_ref, kmax_ref, p_a, p_b, den_ref, *, lambda_init, nchunk, unroll):
    hd = pl.program_id(0)
    i = pl.program_id(1)
    base = hd * (1 + COEF_PARTS)
    slope = slopes_ref[base]
    q_off = lax.broadcasted_iota(jnp.int32, (1, ATT_TQ), 1).astype(F32)

    row1 = lax.broadcasted_iota(jnp.int32, (LANES - HEAD_DIM, 1), 0)
    coef_col = jnp.zeros((LANES - HEAD_DIM, 1), F32)
    for t in range(COEF_PARTS):
        piece = slopes_ref[base + 1 + t]
        coef_col = jnp.where(row1 == t, piece, coef_col)
        coef_col = jnp.where(row1 == COEF_PARTS + t, piece * POS_RADIX, coef_col)
    coef_rows = jnp.broadcast_to(coef_col, (LANES - HEAD_DIM, ATT_TQ))
    for m in range(2):
        for kind, sign in ((0, -1.0), (1, 0.0), (2, 1.0)):
            qv_ref[kind, m, 0:HEAD_DIM, :] = qt_ref[0, m, 0:HEAD_DIM, :]
            qv_ref[kind, m, HEAD_DIM:LANES, :] = (sign * coef_rows).astype(BF16)

    @pl.when(i == 0)
    def _():
        lane = lax.broadcasted_iota(jnp.int32, (ATT_TK, LANES), 1)
        for m in range(2):
            def widest(c, best):
                kc = ka_ref[0, m, c].astype(F32)
                return jnp.maximum(best, jnp.sum(jnp.where(lane < HEAD_DIM, kc * kc, 0.0), axis=1, keepdims=True))
            kmax_ref[m] = jnp.max(lax.fori_loop(0, nchunk, widest, jnp.zeros((ATT_TK, 1), F32)))

    for m in range(2):
        qf = qt_ref[0, m, 0:HEAD_DIM, :].astype(F32)
        bound = jnp.sqrt(jnp.sum(qf * qf, axis=0, keepdims=True) * kmax_ref[m]) * BOUND_SLACK
        mu_ref[m] = bound - REF_MARGIN

    c_diag = i * ATT_DIAG

    def chunk_of(e):
        if isinstance(e, int) and e < ATT_DIAG:
            return c_diag + e
        rest = e - ATT_DIAG
        return jnp.where(e < ATT_DIAG, c_diag + e, rest + ATT_DIAG * (rest >= c_diag).astype(jnp.int32))

    def side_of(c):
        return (c < c_diag).astype(jnp.int32) - (c >= c_diag + ATT_DIAG).astype(jnp.int32)

    p_bufs = (p_a, p_b)

    def probs(e, p_ref):
        c = chunk_of(e)
        side = side_of(c)
        sidef = side.astype(F32)
        gap = jnp.abs(c * ATT_TK - i * ATT_TQ).astype(F32)
        cvec = -slope * (sidef * sidef * gap + sidef * q_off)
        for m in range(2):
            st = jnp.dot(ka_ref[0, m, c], qv_ref[1 + side, m], preferred_element_type=F32)
            if isinstance(e, int) and e < ATT_DIAG:
                key_off = lax.broadcasted_iota(jnp.int32, (ATT_TK, ATT_TQ), 0) + e * ATT_TK
                qry_off = lax.broadcasted_iota(jnp.int32, (ATT_TK, ATT_TQ), 1)
                st = st - slope * jnp.abs(key_off - qry_off).astype(F32)
            p = jnp.exp2(st - (mu_ref[m] - cvec))
            p_ref[m] = p.astype(BF16)
            den_ref[m] += jnp.sum(p.reshape(ATT_TK // SUBLANES, SUBLANES, ATT_TQ), axis=0)

    def group(e0, count, probs_last):
        sums = [None, None]
        for u in range(count):
            if u < count - 1 or probs_last:
                probs(e0 + u + 1, p_bufs[(u + 1) % 2])
            c = chunk_of(e0 + u)
            for m in range(2):
                pv = jnp.dot(vt_ref[0, c], p_bufs[u % 2][m], preferred_element_type=F32)
                sums[m] = pv if sums[m] is None else sums[m] + pv
        for m in range(2):
            acc_ref[m] += sums[m]

    def loop_body(it, carry):
        group(it * unroll, unroll, True)
        return carry

    def one_pass(_):
        acc_ref[...] = jnp.zeros(acc_ref.shape, F32)
        den_ref[...] = jnp.zeros(den_ref.shape, F32)
        n_groups = nchunk // unroll
        probs(0, p_a)
        group(0, unroll, True)
        lax.fori_loop(1, n_groups - 1, loop_body, 0)
        group((n_groups - 1) * unroll, unroll, False)
        lowest = None
        for m in range(2):
            den = jnp.sum(den_ref[m], axis=0, keepdims=True)
            mu_ref[m] = jnp.where(den < REF_FLOOR, mu_ref[m] - 2.0 * REF_MARGIN, mu_ref[m])
            lowest = jnp.min(den) if lowest is None else jnp.minimum(lowest, jnp.min(den))
        return (lowest < REF_FLOOR).astype(jnp.int32)

    lax.while_loop(lambda retry: retry > 0, one_pass, jnp.int32(1))

    lam = (jnp.exp(jnp.sum(lq1_ref[...] * lk1_ref[...], axis=-1, keepdims=True))
           - jnp.exp(jnp.sum(lq2_ref[...] * lk2_ref[...], axis=-1, keepdims=True)) + lambda_init)
    den1 = jnp.sum(den_ref[0], axis=0, keepdims=True)
    den2 = jnp.sum(den_ref[1], axis=0, keepdims=True)
    o = acc_ref[0] / den1 - lam * (acc_ref[1] / den2)
    r = lax.rsqrt(jnp.mean(o * o, axis=0, keepdims=True) + NORM_EPS)
    y = (o * r * subw_ref[...]) * (1.0 - lambda_init)
    o_ref[...] = (y.T * jax.nn.silu(g_ref[...])).astype(o_ref.dtype)


def _diff_attn(qt, ka, vt, lq1, lk1, lq2, lk2, subw, g, lambda_init):
    s = qt.shape[-1]
    g_col0 = (g.shape[1] - DIFF_WIDTH) // HEAD_V
    nchunk = s // ATT_TK
    unroll = min(ATT_UNROLL, nchunk // 2)
    assert unroll % 2 == 0 and nchunk % unroll == 0 and unroll >= ATT_DIAG and s % ATT_TQ == 0
    whole = jnp.asarray([LOG2E * 2.0 ** (-8.0 * (h + 1) / HEADS) for h in range(HEADS)], F32)
    pieces, rest = [], whole
    for _ in range(COEF_PARTS):
        pieces.append(rest.astype(BF16).astype(F32))
        rest = rest - pieces[-1]
    slopes = jnp.stack([whole] + pieces, axis=1).reshape(-1)
    vec = pl.BlockSpec((1, HEAD_DIM), lambda h, i: (0, 0))
    return pl.pallas_call(
        functools.partial(_attn_kernel, lambda_init=lambda_init, nchunk=nchunk, unroll=unroll),
        grid=(HEADS, s // ATT_TQ),
        in_specs=[pl.BlockSpec(memory_space=pltpu.SMEM),
                  pl.BlockSpec((1, 2, LANES, ATT_TQ), lambda h, i: (h, 0, 0, i)),
                  pl.BlockSpec((1, 2, nchunk, ATT_TK, LANES), lambda h, i: (h, 0, 0, 0, 0)),
                  pl.BlockSpec((1, nchunk, HEAD_V, ATT_TK), lambda h, i: (h, 0, 0, 0)),
                  vec, vec, vec, vec,
                  pl.BlockSpec((HEAD_V, 1), lambda h, i: (0, 0)),
                  pl.BlockSpec((ATT_TQ, HEAD_V), lambda h, i: (i, g_col0 + h))],
        out_specs=pl.BlockSpec((ATT_TQ, HEAD_V), lambda h, i: (i, h)),
        out_shape=jax.ShapeDtypeStruct((s, DIFF_WIDTH), BF16),
        scratch_shapes=[pltpu.VMEM((2, HEAD_V, ATT_TQ), F32), pltpu.VMEM((2, 1, ATT_TQ), F32),
                        pltpu.VMEM((3, 2, LANES, ATT_TQ), BF16), pltpu.SMEM((2,), F32),
                        pltpu.VMEM((2, ATT_TK, ATT_TQ), BF16), pltpu.VMEM((2, ATT_TK, ATT_TQ), BF16),
                        pltpu.VMEM((2, SUBLANES, ATT_TQ), F32)],
        compiler_params=_params(2),
        name="diff_attn",
    )(slopes, qt, ka, vt, lq1, lk1, lq2, lk2, subw.reshape(HEAD_V, 1), g)


def _pool_kernel(prev_ref, cur_ref, next_ref, w_ref, scale_ref, g_ref, o_ref, ext_ref, *, tm, seq):
    i = pl.program_id(0)
    last = pl.num_programs(0) - 1
    zero_halo = jnp.zeros((POOL_HALO, POOL_WIDTH), F32)
    ext_ref[0:POOL_HALO] = jnp.where(i > 0, prev_ref[...], zero_halo)
    ext_ref[POOL_HALO:POOL_HALO + tm] = cur_ref[...]
    ext_ref[POOL_HALO + tm:POOL_HALO + tm + POOL_HALO] = jnp.where(i < last, next_ref[...], zero_halo)
    t = lax.broadcasted_iota(jnp.int32, (tm, 1), 0) + i * tm
    for g, w in enumerate(POOL_WINDOWS):
        cols = pl.ds(g * POOL_GROUP, POOL_GROUP)
        span = tm + 2 * POOL_HALO
        width = 1
        while width < w:
            span -= width
            ext_ref[pl.ds(0, span), cols] = ext_ref[pl.ds(0, span), cols] + ext_ref[pl.ds(width, span), cols]
            width *= 2
        assert width == w
        win = ext_ref[pl.ds(POOL_HALO - w // 2, tm), cols]
        lo = jnp.maximum(t - w // 2, 0)
        hi = jnp.minimum(t + (w - w // 2) - 1, seq - 1)
        cnt = (hi - lo + 1).astype(F32)
        pooled = win * (1.0 / cnt) - cur_ref[:, cols]
        y = jnp.dot(pooled.astype(BF16), w_ref[g], preferred_element_type=F32)
        o_ref[:, cols] = (y * scale_ref[:, cols] * jax.nn.silu(g_ref[:, cols])).astype(o_ref.dtype)


def _pool_mix(ug, w_pool, pool_scale):
    s = ug.shape[0]
    width = POOL_WIDTH
    tm = 512
    hb = tm // POOL_HALO
    nhalo = s // POOL_HALO
    return pl.pallas_call(
        functools.partial(_pool_kernel, tm=tm, seq=s),
        grid=(s // tm,),
        in_specs=[pl.BlockSpec((POOL_HALO, width), lambda i: (jnp.maximum(i * hb - 1, 0), 0)),
                  pl.BlockSpec((tm, width), lambda i: (i, 0)),
                  pl.BlockSpec((POOL_HALO, width), lambda i: (jnp.minimum((i + 1) * hb, nhalo - 1), 0)),
                  pl.BlockSpec(w_pool.shape, lambda i: (0, 0, 0)),
                  pl.BlockSpec((1, width), lambda i: (0, 0)),
                  pl.BlockSpec((tm, width), lambda i: (i, 1))],
        out_specs=pl.BlockSpec((tm, width), lambda i: (i, 0)),
        out_shape=jax.ShapeDtypeStruct((s, width), BF16),
        scratch_shapes=[pltpu.VMEM((tm + 2 * POOL_HALO, width), F32)],
        compiler_params=_params(1),
        name="pool_mix",
    )(ug, ug, ug, w_pool, pool_scale, ug)


def _out_kernel(*refs, widths, tm, modulate_next, gate_proj):
    n = len(widths)
    y_refs = refs[:n]
    refs = refs[n:]
    if gate_proj:
        hin_ref, wg_ref = refs[:2]
        refs = refs[2:]
    w_ref, x_ref, gate_ref = refs[:3]
    if modulate_next:
        nw_ref, shift_ref, scale_ref, o_ref, h_ref = refs[3:]
        mul = nw_ref[...]
        one_plus_scale = 1.0 + scale_ref[...]
    else:
        (o_ref,) = refs[3:]
    sub = min(OUT_SUB, tm // 2)
    for sb in range(tm // sub):
        rows = slice(sb * sub, (sb + 1) * sub)
        acc = None
        off = 0
        for y_ref, wd in zip(y_refs, widths):
            y = y_ref[rows, :]
            if gate_proj:
                y = y * jax.nn.silu(jnp.dot(hin_ref[rows, :], wg_ref[:, off:off + wd], preferred_element_type=F32))
            part = jnp.dot(y.astype(BF16), w_ref[off:off + wd, :], preferred_element_type=F32)
            acc = part if acc is None else acc + part
            off += wd
        xn = x_ref[rows, :] + gate_ref[...] * acc
        o_ref[rows, :] = xn
        if modulate_next:
            r = lax.rsqrt(jnp.mean(xn * xn, axis=-1, keepdims=True) + NORM_EPS)
            h_ref[rows, :] = ((xn * r * mul) * one_plus_scale + shift_ref[...]).astype(BF16)


def _out_proj(ys, w, layer, x, gate, next_mod=None, gate_from=None):
    s, d = x.shape
    k = w.shape[1]
    gate_proj = gate_from is not None
    tm = min(512, s)
    once = pl.Buffered(1) if gate_proj else None
    widths = tuple(y.shape[1] for y in ys)
    vec = pl.BlockSpec((1, d), lambda i: (0, 0))
    rows = pl.BlockSpec((tm, d), lambda i: (i, 0))
    modulate_next = next_mod is not None
    gate_specs, gate_args = [], ()
    if gate_proj:
        hin, wg, col0 = gate_from
        assert col0 % k == 0 and sum(widths) == k
        gate_specs = [pl.BlockSpec((tm, hin.shape[1]), lambda i: (i, 0)),
                      pl.BlockSpec((None, wg.shape[1], k), lambda i: (layer, 0, col0 // k), pipeline_mode=once)]
        gate_args = (hin, wg)
    return pl.pallas_call(
        functools.partial(_out_kernel, widths=widths, tm=tm, modulate_next=modulate_next, gate_proj=gate_proj),
        grid=(s // tm,),
        in_specs=[pl.BlockSpec((tm, wd), lambda i: (i, 0)) for wd in widths] + gate_specs
        + [pl.BlockSpec((None, k, d), lambda i: (layer, 0, 0), pipeline_mode=once), rows, vec]
        + ([vec, vec, vec] if modulate_next else []),
        out_specs=[rows, rows] if modulate_next else rows,
        out_shape=([jax.ShapeDtypeStruct((s, d), F32), jax.ShapeDtypeStruct((s, d), BF16)] if modulate_next
                   else jax.ShapeDtypeStruct((s, d), F32)),
        compiler_params=_params(1),
        name="out_proj",
    )(*ys, *gate_args, w, x, gate, *(next_mod if modulate_next else ()))


def _dft_split(s):
    n1 = 1 << (int(math.log2(s)) // 2)
    n2 = s // n1
    assert n1 * n2 == s and n1 % SUBLANES == 0 and n2 % SUBLANES == 0
    return n1, n2


def _angles(num, den):
    ang = (2.0 * math.pi / den) * (num % den)
    return np.cos(ang), np.sin(ang)


def _fold_cs_kernel(cs_ref, wf_ref, o_ref):
    wf = wf_ref[...]
    o_ref[:, 0:FOURIER_GROUP] = jnp.dot(cs_ref[0], wf, preferred_element_type=F32,
                                        precision=lax.Precision.HIGHEST)
    o_ref[:, FOURIER_GROUP:] = jnp.dot(cs_ref[1], wf, preferred_element_type=F32,
                                       precision=lax.Precision.HIGHEST)


def _fold_channel_dft(w_fourier_j, seq):
    fg = FOURIER_GROUP
    idx = np.arange(fg)
    cc, sc = _angles(idx[:, None] * idx[None, :], fg)
    cs = jnp.asarray(np.stack([cc, sc]) / math.sqrt(seq * fg), F32)
    return pl.pallas_call(
        _fold_cs_kernel,
        grid=(FOURIER_GROUPS,),
        in_specs=[pl.BlockSpec((2, fg, fg), lambda g: (0, 0, 0)),
                  pl.BlockSpec((None, fg, fg), lambda g: (g, 0, 0))],
        out_specs=pl.BlockSpec((None, fg, 2 * fg), lambda g: (g, 0, 0)),
        out_shape=jax.ShapeDtypeStruct((FOURIER_GROUPS, fg, 2 * fg), F32),
        compiler_params=_params(1),
        name="fold_channel_dft",
    )(cs, w_fourier_j)


def _fold_win_kernel(w_ref, ab_ref, o_ref):
    o_ref[...] = jnp.dot(w_ref[...], ab_ref[...].astype(BF16), preferred_element_type=F32).astype(BF16)


def _fold_in_proj(w_in, layer, ab):
    d = w_in.shape[1]
    fg = FOURIER_GROUP
    return pl.pallas_call(
        _fold_win_kernel,
        grid=(FOURIER_GROUPS, 2),
        in_specs=[pl.BlockSpec((None, d, fg), lambda g, t: (layer, 0, g)),
                  pl.BlockSpec((None, fg, fg), lambda g, t: (g, 0, t))],
        out_specs=pl.BlockSpec((d, fg), lambda g, t: (0, g + t * FOURIER_GROUPS)),
        out_shape=jax.ShapeDtypeStruct((d, 2 * D_MODEL), BF16),
        compiler_params=_params(2),
        name="fold_in_proj",
    )(w_in, ab)


HIGH_HALF = -65536


def _bf16_bits(v):
    b = lax.bitcast_convert_type(v, jnp.int32)
    return b + 0x7FFF + (lax.shift_right_logical(b, 16) & 1)


def _dft1_kernel(h_ref, wp_ref, wq_ref, f_ref, t_ref, *, n1, per_sub, n_sub):
    f = f_ref[...]
    for sb in range(n_sub):
        rows = slice(sb * per_sub * n1, (sb + 1) * per_sub * n1)
        h = h_ref[rows, :]
        p = jnp.dot(h, wp_ref[...], preferred_element_type=F32).astype(BF16)
        q = jnp.dot(h, wq_ref[...], preferred_element_type=F32).astype(BF16)
        for jj in range(per_sub):
            r = slice(jj * n1, (jj + 1) * n1)
            t = jnp.dot(f, jnp.concatenate([p[r], q[r]], axis=0), preferred_element_type=F32)
            t_ref[sb * per_sub + jj] = ((_bf16_bits(t[0:n1]) & HIGH_HALF)
                                        | lax.shift_right_logical(_bf16_bits(t[n1:2 * n1]), 16))


def _dft_stage1(h_t, w_pq, f1, n1, n2):
    s, d = h_t.shape
    c = w_pq.shape[1] // 2
    tm = min(PROJ_TM, s)
    tn = 512
    per_sub = max(1, 256 // n1)
    n_sub = tm // (per_sub * n1)
    ncb = c // tn
    return pl.pallas_call(
        functools.partial(_dft1_kernel, n1=n1, per_sub=per_sub, n_sub=n_sub),
        grid=(s // tm, ncb),
        in_specs=[pl.BlockSpec((tm, d), lambda i, j: (i, 0)),
                  pl.BlockSpec((d, tn), lambda i, j: (0, j)),
                  pl.BlockSpec((d, tn), lambda i, j: (0, j + ncb)),
                  pl.BlockSpec((2 * n1, 2 * n1), lambda i, j: (0, 0))],
        out_specs=pl.BlockSpec((tm // n1, n1, tn), lambda i, j: (i, 0, j)),
        out_shape=jax.ShapeDtypeStruct((n2, n1, c), jnp.int32),
        compiler_params=_params(2),
        name="fourier_in_dft1",
    )(h_t, w_pq, w_pq, f1)


def _dft2_kernel(l_ref, t_ref, o_ref, *, n2, cb):
    packed = t_ref[...]
    re = lax.bitcast_convert_type(packed & HIGH_HALF, F32)
    im = lax.bitcast_convert_type(lax.shift_left(packed, 16), F32)
    rhs = jnp.stack([re, im], axis=1).reshape(n2 * 2 * SUBLANES, cb).astype(BF16)
    o_ref[...] = jnp.dot(l_ref[0], rhs, preferred_element_type=F32).reshape(n2, SUBLANES, cb)


def _dft_stage2(t_nk, l2, n1, n2):
    c = t_nk.shape[-1]
    cb = min(DFT_CB, c)
    row_blocks = pl.BlockSpec((n2, SUBLANES, cb), lambda g, j: (0, g, j))
    out = pl.pallas_call(
        functools.partial(_dft2_kernel, n2=n2, cb=cb),
        grid=(n1 // SUBLANES, c // cb),
        in_specs=[pl.BlockSpec((1, SUBLANES * n2, SUBLANES * 2 * n2), lambda g, j: (g, 0, 0)),
                  row_blocks],
        out_specs=row_blocks,
        out_shape=jax.ShapeDtypeStruct((n2, n1, c), F32),
        compiler_params=_params(2),
        name="dft_stage2",
    )(l2, t_nk)
    return out.reshape(n1 * n2, c)


def _dft_matrices(s):
    n1, n2 = _dft_split(s)
    a = np.arange(n1)
    c1, s1 = _angles(a[:, None] * a[None, :], n1)
    f1 = jnp.asarray(np.block([[c1, -s1], [-s1, -c1]]).astype(BF16))
    groups = n1 // SUBLANES
    g = np.arange(groups)[:, None, None, None]
    k2 = np.arange(n2)[None, :, None, None]
    j = np.arange(SUBLANES)[None, None, :, None]
    nn = np.arange(n2)[None, None, None, :]
    c2, s2 = _angles((SUBLANES * g + j + n1 * k2) * nn, s)
    cs = np.concatenate([c2, s2], axis=3).astype(BF16)
    return n1, n2, f1, _expand_twiddles(jnp.asarray(cs.reshape(groups, n2 * SUBLANES, 2 * n2)))


def _expand_kernel(cs_ref, e_ref, o_ref):
    spread = jnp.dot(cs_ref[0], e_ref[...], preferred_element_type=F32)
    row = lax.broadcasted_iota(jnp.int32, spread.shape, 0)
    col = lax.broadcasted_iota(jnp.int32, spread.shape, 1)
    o_ref[0] = jnp.where(row % SUBLANES == col % SUBLANES, spread, 0.0).astype(BF16)


def _expand_twiddles(cs):
    groups, rows, cols = cs.shape
    wide = cols * SUBLANES
    src = jnp.arange(cols, dtype=jnp.int32)[:, None]
    slot = (src % (cols // 2)) * 2 + src // (cols // 2)
    e = jnp.arange(wide, dtype=jnp.int32)[None, :] // SUBLANES == slot
    return pl.pallas_call(
        _expand_kernel,
        grid=(groups,),
        in_specs=[pl.BlockSpec((1, rows, cols), lambda g: (g, 0, 0)),
                  pl.BlockSpec((cols, wide), lambda g: (0, 0))],
        out_specs=pl.BlockSpec((1, rows, wide), lambda g: (g, 0, 0)),
        out_shape=jax.ShapeDtypeStruct((groups, rows, wide), BF16),
        compiler_params=_params(1),
        name="expand_twiddles",
    )(cs, e.astype(BF16))


def kernel(x, c, norm_w, ada_w, ada_b, w_in_ab, w_pool, pool_scale, q_norm_w, k_norm_w, lambda_q1, lambda_k1,
           lambda_q2, lambda_k2, subln_w, w_out_ab, w_in_c, w_fourier, w_out_c):
    batch, s, d = x.shape
    assert batch == 1 and d == D_MODEL and s % ATT_TK == 0
    xs = x.reshape(s, d)
    mod = _ada_mod(c, ada_w, ada_b)
    n1, n2, f1, l2 = _dft_matrices(s)
    row = lambda v: v.reshape(1, -1)

    def mod_of(i):
        return mod[i, :, 0:d], mod[i, :, d:2 * d], mod[i, :, 2 * d:3 * d]

    def finish(ys, w_out, j, xs, i, gate_from=None):
        gate = mod_of(i)[2]
        if i + 1 == DEPTH:
            return _out_proj(ys, w_out, j, xs, gate, gate_from=gate_from), None
        shift, scale, _ = mod_of(i + 1)
        return _out_proj(ys, w_out, j, xs, gate, (row(norm_w[i + 1]), shift, scale), gate_from=gate_from)

    w_in_ab, w_out_ab, w_in_c, w_out_c = (w.astype(BF16) for w in (w_in_ab, w_out_ab, w_in_c, w_out_c))
    h = _modulate(xs, row(norm_w[0]), *mod_of(0)[:2])
    for i in range(DEPTH):
        j = i // 2
        if i % 2 == 0:
            o1, o2, o3, o4 = POOL_WIDTH, POOL_WIDTH + DIFF_WIDTH, POOL_WIDTH + 2 * DIFF_WIDTH, POOL_WIDTH + 3 * DIFF_WIDTH
            lambda_init = 0.8 - 0.6 * math.exp(-0.3 * i)
            ug = _matmul(h, w_in_ab, F32, PROJ_TM, PROJ_TN, name="pool_gate_in_proj",
                         cols=((0, o1), (o4, o4 + AB_WIDTH)), layer=j)
            qt, ka, vt = _qkv_proj(h, w_in_ab, j, o1, o2, o3, row(q_norm_w[j]), row(k_norm_w[j]))
            y_a = _pool_mix(ug, w_pool[j].astype(BF16), row(pool_scale[j]))
            y_b = _diff_attn(qt, ka, vt, row(lambda_q1[j]), row(lambda_k1[j]), row(lambda_q2[j]),
                             row(lambda_k2[j]), subln_w[j], ug, lambda_init)
            xs, h = finish([y_a, y_b], w_out_ab, j, xs, i)
        else:
            ab = _fold_channel_dft(w_fourier[j], s)
            w_pq = _fold_in_proj(w_in_c, j, ab)
            h_t = h.reshape(n1, n2, d).transpose(1, 0, 2).reshape(s, d)
            t_nk = _dft_stage1(h_t, w_pq, f1, n1, n2)
            f = _dft_stage2(t_nk, l2, n1, n2)
            xs, h = finish([f], w_out_c, j, xs, i, gate_from=(h, w_in_c, d))
    return xs.reshape(batch, s, d)
```

```python
import functools
import math

import jax
import jax.numpy as jnp
import numpy as np
from jax import lax
from jax.experimental import pallas as pl
from jax.experimental.pallas import tpu as pltpu

F32 = jnp.float32
BF16 = jnp.bfloat16

D_MODEL = 2048
DEPTH = 4
NORM_EPS = 1e-6

POOL_WINDOWS = (2, 4, 8, 16)
POOL_WIDTH = D_MODEL // 2
POOL_GROUP = POOL_WIDTH // len(POOL_WINDOWS)
POOL_HALO = 8

HEADS = 8
HEAD_DIM = 64
HEAD_V = 2 * HEAD_DIM
DIFF_WIDTH = HEADS * HEAD_V
AB_WIDTH = POOL_WIDTH + DIFF_WIDTH

FOURIER_GROUPS = 4
FOURIER_GROUP = D_MODEL // FOURIER_GROUPS

LANES = 128
SUBLANES = 8
POS_RADIX = 256
COEF_PARTS = 3
LOG2E = math.log2(math.e)

ATT_TQ = 512
ATT_TK = 512
ATT_DIAG = ATT_TQ // ATT_TK
ATT_UNROLL = 8
REF_MARGIN = 60.0
REF_FLOOR = 2.0 ** -60
BOUND_SLACK = 1.001

PROJ_TM, PROJ_TN = 2048, 1024
OUT_SUB = 256
DFT_CB = 2048

VMEM_LIMIT = 56 * 1024 * 1024


def _params(n_axes):
    return pltpu.CompilerParams(dimension_semantics=("arbitrary",) * n_axes,
                                vmem_limit_bytes=VMEM_LIMIT)


def _mod_kernel(c_ref, w_ref, b_ref, o_ref):
    c = c_ref[...]
    o_ref[...] = jnp.sum(jax.nn.silu(c) * w_ref[...], axis=0, keepdims=True) + b_ref[...]


def _ada_mod(c, ada_w, ada_b):
    depth, d, n = ada_w.shape
    tn = 1024
    return pl.pallas_call(
        _mod_kernel,
        grid=(depth, n // tn),
        in_specs=[pl.BlockSpec((d, 1), lambda i, j: (0, 0)),
                  pl.BlockSpec((None, d, tn), lambda i, j: (i, 0, j)),
                  pl.BlockSpec((None, 1, tn), lambda i, j: (i, 0, j))],
        out_specs=pl.BlockSpec((None, 1, tn), lambda i, j: (i, 0, j)),
        out_shape=jax.ShapeDtypeStruct((depth, 1, n), F32),
        compiler_params=_params(2),
        name="ada_mod",
    )(c.reshape(d, 1), ada_w, ada_b.reshape(depth, 1, n))


def _modulate_kernel(x_ref, nw_ref, shift_ref, scale_ref, o_ref):
    xf = x_ref[...]
    r = lax.rsqrt(jnp.mean(xf * xf, axis=-1, keepdims=True) + NORM_EPS)
    y = xf * r * nw_ref[...]
    y = y * (1.0 + scale_ref[...]) + shift_ref[...]
    o_ref[...] = y.astype(o_ref.dtype)


def _modulate(x, nw, shift, scale):
    s, d = x.shape
    tm = min(1024, s)
    vec = pl.BlockSpec((1, d), lambda i: (0, 0))
    return pl.pallas_call(
        _modulate_kernel,
        grid=(s // tm,),
        in_specs=[pl.BlockSpec((tm, d), lambda i: (i, 0)), vec, vec, vec],
        out_specs=pl.BlockSpec((tm, d), lambda i: (i, 0)),
        out_shape=jax.ShapeDtypeStruct((s, d), BF16),
        compiler_params=_params(1),
        name="modulate",
    )(x, nw, shift, scale)


def _mm_kernel(a_ref, w_ref, o_ref, *, precision):
    o_ref[...] = jnp.dot(a_ref[...], w_ref[...], preferred_element_type=F32,
                         precision=precision).astype(o_ref.dtype)


def _matmul(a, w, out_dtype, tm, tn, precision=None, name="matmul", cols=None, layer=None):
    m, k = a.shape
    cols = cols if cols is not None else ((0, w.shape[-1]),)
    n = sum(stop - start for start, stop in cols)
    tm, tn = min(tm, m), min(tn, n)
    assert all(start % tn == 0 and stop % tn == 0 for start, stop in cols)

    def wblock(j):
        jb, first = j, 0
        for start, stop in cols:
            jb = jnp.where(j >= first, j - first + start // tn, jb)
            first += (stop - start) // tn
        return jb

    if layer is None:
        wspec = pl.BlockSpec((k, tn), lambda i, j: (0, wblock(j)))
    else:
        wspec = pl.BlockSpec((None, k, tn), lambda i, j: (layer, 0, wblock(j)))
    return pl.pallas_call(
        functools.partial(_mm_kernel, precision=precision),
        grid=(m // tm, n // tn),
        in_specs=[pl.BlockSpec((tm, k), lambda i, j: (i, 0)), wspec],
        out_specs=pl.BlockSpec((tm, tn), lambda i, j: (i, j)),
        out_shape=jax.ShapeDtypeStruct((m, n), out_dtype),
        compiler_params=_params(2),
        name=name,
    )(a, w)


def _qkv_kernel(h_ref, wq_ref, wk_ref, wv_ref, qnw_ref, knw_ref, qt_ref, ka_ref, vt_ref, *, tm):
    lane = lax.broadcasted_iota(jnp.int32, (ATT_TK, LANES), 1)
    first = lane < HEAD_DIM
    pos = lax.broadcasted_iota(jnp.int32, (ATT_TK, LANES), 0)
    pos_lo = (pos % POS_RADIX).astype(F32)
    pos_hi = (pos // POS_RADIX).astype(F32)
    pos_cols = jnp.where(lane < HEAD_DIM + COEF_PARTS, pos_lo, jnp.where(lane < HEAD_DIM + 2 * COEF_PARTS, pos_hi, 0.0))

    def halves_rms(z, w):
        sq = z * z
        ss1 = jnp.sum(jnp.where(first, sq, 0.0), axis=-1, keepdims=True)
        ss2 = jnp.sum(jnp.where(first, 0.0, sq), axis=-1, keepdims=True)
        r = jnp.where(first, lax.rsqrt(ss1 / HEAD_DIM + NORM_EPS), lax.rsqrt(ss2 / HEAD_DIM + NORM_EPS))
        return z * r * w

    for cc in range(tm // ATT_TK):
        rows = slice(cc * ATT_TK, (cc + 1) * ATT_TK)
        h = h_ref[rows, :]
        zq = jnp.dot(h, wq_ref[...], preferred_element_type=F32)
        zk = jnp.dot(h, wk_ref[...], preferred_element_type=F32)
        zv = jnp.dot(h, wv_ref[...], preferred_element_type=F32)
        for hh in range(2):
            cols = slice(hh * LANES, (hh + 1) * LANES)
            qn = halves_rms(zq[:, cols], qnw_ref[...]) * (HEAD_DIM ** -0.5) * LOG2E
            kn = halves_rms(zk[:, cols], knw_ref[...])
            for m in range(2):
                qm = qn if m == 0 else pltpu.roll(qn, HEAD_DIM, axis=1)
                km = kn if m == 0 else pltpu.roll(kn, HEAD_DIM, axis=1)
                qt_ref[hh, m, :, rows] = jnp.where(first, qm, 0.0).T.astype(BF16)
                ka_ref[hh, m, cc] = jnp.where(first, km, pos_cols).astype(BF16)
            vt_ref[hh, cc] = zv[:, cols].T.astype(BF16)


def _qkv_proj(h, w, layer, col_q, col_k, col_v, qnw, knw):
    s, d = h.shape
    tm = min(2048, s)
    nchunk = s // ATT_TK
    cpt = tm // ATT_TK
    tn = 2 * LANES
    assert col_q % tn == 0 and col_k % tn == 0 and col_v % tn == 0
    wspecs = [pl.BlockSpec((None, d, tn), functools.partial(lambda i, j, jb: (layer, 0, j + jb), jb=c0 // tn))
              for c0 in (col_q, col_k, col_v)]
    vec = pl.BlockSpec((1, LANES), lambda i, j: (0, 0))
    return pl.pallas_call(
        functools.partial(_qkv_kernel, tm=tm),
        grid=(s // tm, HEADS // 2),
        in_specs=[pl.BlockSpec((tm, d), lambda i, j: (i, 0)), *wspecs, vec, vec],
        out_specs=[pl.BlockSpec((2, 2, LANES, tm), lambda i, j: (j, 0, 0, i)),
                   pl.BlockSpec((2, 2, cpt, ATT_TK, LANES), lambda i, j: (j, 0, i, 0, 0)),
                   pl.BlockSpec((2, cpt, HEAD_V, ATT_TK), lambda i, j: (j, i, 0, 0))],
        out_shape=[jax.ShapeDtypeStruct((HEADS, 2, LANES, s), BF16),
                   jax.ShapeDtypeStruct((HEADS, 2, nchunk, ATT_TK, LANES), BF16),
                   jax.ShapeDtypeStruct((HEADS, nchunk, HEAD_V, ATT_TK), BF16)],
        compiler_params=_params(2),
        name="qkv_proj",
    )(h, w, w, w, jnp.tile(qnw, (1, 2)), jnp.tile(knw, (1, 2)))


def _attn_kernel(slopes_ref, qt_ref, ka_ref, vt_ref, lq1_ref, lk1_ref, lq2_ref, lk2_ref, subw_ref, g_ref,
                 o_ref, acc_ref, mu_ref, qv_ref, kmax_ref, p_a, p_b, den_ref, *, lambda_init, nchunk, unroll):
    hd = pl.program_id(0)
    i = pl.program_id(1)
    base = hd * (1 + COEF_PARTS)
    slope = slopes_ref[base]
    q_off = lax.broadcasted_iota(jnp.int32, (1, ATT_TQ), 1).astype(F32)

    row1 = lax.broadcasted_iota(jnp.int32, (LANES - HEAD_DIM, 1), 0)
    coef_col = jnp.zeros((LANES - HEAD_DIM, 1), F32)
    for t in range(COEF_PARTS):
        piece = slopes_ref[base + 1 + t]
        coef_col = jnp.where(row1 == t, piece, coef_col)
        coef_col = jnp.where(row1 == COEF_PARTS + t, piece * POS_RADIX, coef_col)
    coef_rows = jnp.broadcast_to(coef_col, (LANES - HEAD_DIM, ATT_TQ))
    for m in range(2):
        for kind, sign in ((0, -1.0), (1, 0.0), (2, 1.0)):
            qv_ref[kind, m, 0:HEAD_DIM, :] = qt_ref[0, m, 0:HEAD_DIM, :]
            qv_ref[kind, m, HEAD_DIM:LANES, :] = (sign * coef_rows).astype(BF16)

    @pl.when(i == 0)
    def _():
        lane = lax.broadcasted_iota(jnp.int32, (ATT_TK, LANES), 1)
        for m in range(2):
            def widest(c, best):
                kc = ka_ref[0, m, c].astype(F32)
                return jnp.maximum(best, jnp.sum(jnp.where(lane < HEAD_DIM, kc * kc, 0.0), axis=1, keepdims=True))
            kmax_ref[m] = jnp.max(lax.fori_loop(0, nchunk, widest, jnp.zeros((ATT_TK, 1), F32)))

    for m in range(2):
        qf = qt_ref[0, m, 0:HEAD_DIM, :].astype(F32)
        bound = jnp.sqrt(jnp.sum(qf * qf, axis=0, keepdims=True) * kmax_ref[m]) * BOUND_SLACK
        mu_ref[m] = bound - REF_MARGIN

    c_diag = i * ATT_DIAG

    def chunk_of(e):
        if isinstance(e, int) and e < ATT_DIAG:
            return c_diag + e
        rest = e - ATT_DIAG
        return jnp.where(e < ATT_DIAG, c_diag + e, rest + ATT_DIAG * (rest >= c_diag).astype(jnp.int32))

    def side_of(c):
        return (c < c_diag).astype(jnp.int32) - (c >= c_diag + ATT_DIAG).astype(jnp.int32)

    p_bufs = (p_a, p_b)

    def probs(e, p_ref):
        c = chunk_of(e)
        side = side_of(c)
        sidef = side.astype(F32)
        gap = jnp.abs(c * ATT_TK - i * ATT_TQ).astype(F32)
        cvec = -slope * (sidef * sidef * gap + sidef * q_off)
        for m in range(2):
            st = jnp.dot(ka_ref[0, m, c], qv_ref[1 + side, m], preferred_element_type=F32)
            if isinstance(e, int) and e < ATT_DIAG:
                key_off = lax.broadcasted_iota(jnp.int32, (ATT_TK, ATT_TQ), 0) + e * ATT_TK
                qry_off = lax.broadcasted_iota(jnp.int32, (ATT_TK, ATT_TQ), 1)
                st = st - slope * jnp.abs(key_off - qry_off).astype(F32)
            p = jnp.exp2(st - (mu_ref[m] - cvec))
            p_ref[m] = p.astype(BF16)
            den_ref[m] += jnp.sum(p.reshape(ATT_TK // SUBLANES, SUBLANES, ATT_TQ), axis=0)

    def group(e0, count, probs_last):
        sums = [None, None]
        for u in range(count):
            if u < count - 1 or probs_last:
                probs(e0 + u + 1, p_bufs[(u + 1) % 2])
            c = chunk_of(e0 + u)
            for m in range(2):
                pv = jnp.dot(vt_ref[0, c], p_bufs[u % 2][m], preferred_element_type=F32)
                sums[m] = pv if sums[m] is None else sums[m] + pv
        for m in range(2):
            acc_ref[m] += sums[m]

    def loop_body(it, carry):
        group(it * unroll, unroll, True)
        return carry

    def one_pass(_):
        acc_ref[...] = jnp.zeros(acc_ref.shape, F32)
        den_ref[...] = jnp.zeros(den_ref.shape, F32)
        n_groups = nchunk // unroll
        probs(0, p_a)
        group(0, unroll, True)
        lax.fori_loop(1, n_groups - 1, loop_body, 0)
        group((n_groups - 1) * unroll, unroll, False)
        lowest = None
        for m in range(2):
            den = jnp.sum(den_ref[m], axis=0, keepdims=True)
            mu_ref[m] = jnp.where(den < REF_FLOOR, mu_ref[m] - 2.0 * REF_MARGIN, mu_ref[m])
            lowest = jnp.min(den) if lowest is None else jnp.minimum(lowest, jnp.min(den))
        return (lowest < REF_FLOOR).astype(jnp.int32)

    lax.while_loop(lambda retry: retry > 0, one_pass, jnp.int32(1))

    lam = (jnp.exp(jnp.sum(lq1_ref[...] * lk1_ref[...], axis=-1, keepdims=True))
           - jnp.exp(jnp.sum(lq2_ref[...] * lk2_ref[...], axis=-1, keepdims=True)) + lambda_init)
    den1 = jnp.sum(den_ref[0], axis=0, keepdims=True)
    den2 = jnp.sum(den_ref[1], axis=0, keepdims=True)
    o = acc_ref[0] / den1 - lam * (acc_ref[1] / den2)
    r = lax.rsqrt(jnp.mean(o * o, axis=0, keepdims=True) + NORM_EPS)
    y = (o * r * subw_ref[...]) * (1.0 - lambda_init)
    o_ref[...] = (y.T * jax.nn.silu(g_ref[...])).astype(o_ref.dtype)


def _diff_attn(qt, ka, vt, lq1, lk1, lq2, lk2, subw, g, lambda_init):
    s = qt.shape[-1]
    g_col0 = (g.shape[1] - DIFF_WIDTH) // HEAD_V
    nchunk = s // ATT_TK
    unroll = min(ATT_UNROLL, nchunk // 2)
    assert unroll % 2 == 0 and nchunk % unroll == 0 and unroll >= ATT_DIAG and s % ATT_TQ == 0
    whole = jnp.asarray([LOG2E * 2.0 ** (-8.0 * (h + 1) / HEADS) for h in range(HEADS)], F32)
    pieces, rest = [], whole
    for _ in range(COEF_PARTS):
        pieces.append(rest.astype(BF16).astype(F32))
        rest = rest - pieces[-1]
    slopes = jnp.stack([whole] + pieces, axis=1).reshape(-1)
    vec = pl.BlockSpec((1, HEAD_DIM), lambda h, i: (0, 0))
    return pl.pallas_call(
        functools.partial(_attn_kernel, lambda_init=lambda_init, nchunk=nchunk, unroll=unroll),
        grid=(HEADS, s // ATT_TQ),
        in_specs=[pl.BlockSpec(memory_space=pltpu.SMEM),
                  pl.BlockSpec((1, 2, LANES, ATT_TQ), lambda h, i: (h, 0, 0, i)),
                  pl.BlockSpec((1, 2, nchunk, ATT_TK, LANES), lambda h, i: (h, 0, 0, 0, 0)),
                  pl.BlockSpec((1, nchunk, HEAD_V, ATT_TK), lambda h, i: (h, 0, 0, 0)),
                  vec, vec, vec, vec,
                  pl.BlockSpec((HEAD_V, 1), lambda h, i: (0, 0)),
                  pl.BlockSpec((ATT_TQ, HEAD_V), lambda h, i: (i, g_col0 + h))],
        out_specs=pl.BlockSpec((ATT_TQ, HEAD_V), lambda h, i: (i, h)),
        out_shape=jax.ShapeDtypeStruct((s, DIFF_WIDTH), BF16),
        scratch_shapes=[pltpu.VMEM((2, HEAD_V, ATT_TQ), F32), pltpu.VMEM((2, 1, ATT_TQ), F32),
                        pltpu.VMEM((3, 2, LANES, ATT_TQ), BF16), pltpu.SMEM((2,), F32),
                        pltpu.VMEM((2, ATT_TK, ATT_TQ), BF16), pltpu.VMEM((2, ATT_TK, ATT_TQ), BF16),
                        pltpu.VMEM((2, SUBLANES, ATT_TQ), F32)],
        compiler_params=_params(2),
        name="diff_attn",
    )(slopes, qt, ka, vt, lq1, lk1, lq2, lk2, subw.reshape(HEAD_V, 1), g)


def _pool_kernel(prev_ref, cur_ref, next_ref, w_ref, scale_ref, g_ref, o_ref, ext_ref, *, tm, seq):
    i = pl.program_id(0)
    last = pl.num_programs(0) - 1
    zero_halo = jnp.zeros((POOL_HALO, POOL_WIDTH), F32)
    ext_ref[0:POOL_HALO] = jnp.where(i > 0, prev_ref[...], zero_halo)
    ext_ref[POOL_HALO:POOL_HALO + tm] = cur_ref[...]
    ext_ref[POOL_HALO + tm:POOL_HALO + tm + POOL_HALO] = jnp.where(i < last, next_ref[...], zero_halo)
    t = lax.broadcasted_iota(jnp.int32, (tm, 1), 0) + i * tm
    for g, w in enumerate(POOL_WINDOWS):
        cols = pl.ds(g * POOL_GROUP, POOL_GROUP)
        span = tm + 2 * POOL_HALO
        width = 1
        while width < w:
            span -= width
            ext_ref[pl.ds(0, span), cols] = ext_ref[pl.ds(0, span), cols] + ext_ref[pl.ds(width, span), cols]
            width *= 2
        assert width == w
        win = ext_ref[pl.ds(POOL_HALO - w // 2, tm), cols]
        lo = jnp.maximum(t - w // 2, 0)
        hi = jnp.minimum(t + (w - w // 2) - 1, seq - 1)
        cnt = (hi - lo + 1).astype(F32)
        pooled = win * (1.0 / cnt) - cur_ref[:, cols]
        y = jnp.dot(pooled.astype(BF16), w_ref[g], preferred_element_type=F32)
        o_ref[:, cols] = (y * scale_ref[:, cols] * jax.nn.silu(g_ref[:, cols])).astype(o_ref.dtype)


def _pool_mix(ug, w_pool, pool_scale):
    s = ug.shape[0]
    width = POOL_WIDTH
    tm = 512
    hb = tm // POOL_HALO
    nhalo = s // POOL_HALO
    return pl.pallas_call(
        functools.partial(_pool_kernel, tm=tm, seq=s),
        grid=(s // tm,),
        in_specs=[pl.BlockSpec((POOL_HALO, width), lambda i: (jnp.maximum(i * hb - 1, 0), 0)),
                  pl.BlockSpec((tm, width), lambda i: (i, 0)),
                  pl.BlockSpec((POOL_HALO, width), lambda i: (jnp.minimum((i + 1) * hb, nhalo - 1), 0)),
                  pl.BlockSpec(w_pool.shape, lambda i: (0, 0, 0)),
                  pl.BlockSpec((1, width), lambda i: (0, 0)),
                  pl.BlockSpec((tm, width), lambda i: (i, 1))],
        out_specs=pl.BlockSpec((tm, width), lambda i: (i, 0)),
        out_shape=jax.ShapeDtypeStruct((s, width), BF16),
        scratch_shapes=[pltpu.VMEM((tm + 2 * POOL_HALO, width), F32)],
        compiler_params=_params(1),
        name="pool_mix",
    )(ug, ug, ug, w_pool, pool_scale, ug)


def _out_kernel(*refs, widths, tm, modulate_next, gate_proj):
    n = len(widths)
    y_refs = refs[:n]
    refs = refs[n:]
    if gate_proj:
        hin_ref, wg_ref = refs[:2]
        refs = refs[2:]
    w_ref, x_ref, gate_ref = refs[:3]
    if modulate_next:
        nw_ref, shift_ref, scale_ref, o_ref, h_ref = refs[3:]
        mul = nw_ref[...]
        one_plus_scale = 1.0 + scale_ref[...]
    else:
        (o_ref,) = refs[3:]
    sub = min(OUT_SUB, tm // 2)
    for sb in range(tm // sub):
        rows = slice(sb * sub, (sb + 1) * sub)
        acc = None
        off = 0
        for y_ref, wd in zip(y_refs, widths):
            y = y_ref[rows, :]
            if gate_proj:
                y = y * jax.nn.silu(jnp.dot(hin_ref[rows, :], wg_ref[:, off:off + wd], preferred_element_type=F32))
            part = jnp.dot(y.astype(BF16), w_ref[off:off + wd, :], preferred_element_type=F32)
            acc = part if acc is None else acc + part
            off += wd
        xn = x_ref[rows, :] + gate_ref[...] * acc
        o_ref[rows, :] = xn
        if modulate_next:
            r = lax.rsqrt(jnp.mean(xn * xn, axis=-1, keepdims=True) + NORM_EPS)
            h_ref[rows, :] = ((xn * r * mul) * one_plus_scale + shift_ref[...]).astype(BF16)


def _out_proj(ys, w, layer, x, gate, next_mod=None, gate_from=None):
    s, d = x.shape
    k = w.shape[1]
    gate_proj = gate_from is not None
    tm = min(512, s)
    once = pl.Buffered(1) if gate_proj else None
    widths = tuple(y.shape[1] for y in ys)
    vec = pl.BlockSpec((1, d), lambda i: (0, 0))
    rows = pl.BlockSpec((tm, d), lambda i: (i, 0))
    modulate_next = next_mod is not None
    gate_specs, gate_args = [], ()
    if gate_proj:
        hin, wg, col0 = gate_from
        assert col0 % k == 0 and sum(widths) == k
        gate_specs = [pl.BlockSpec((tm, hin.shape[1]), lambda i: (i, 0)),
                      pl.BlockSpec((None, wg.shape[1], k), lambda i: (layer, 0, col0 // k), pipeline_mode=once)]
        gate_args = (hin, wg)
    return pl.pallas_call(
        functools.partial(_out_kernel, widths=widths, tm=tm, modulate_next=modulate_next, gate_proj=gate_proj),
        grid=(s // tm,),
        in_specs=[pl.BlockSpec((tm, wd), lambda i: (i, 0)) for wd in widths] + gate_specs
        + [pl.BlockSpec((None, k, d), lambda i: (layer, 0, 0), pipeline_mode=once), rows, vec]
        + ([vec, vec, vec] if modulate_next else []),
        out_specs=[rows, rows] if modulate_next else rows,
        out_shape=([jax.ShapeDtypeStruct((s, d), F32), jax.ShapeDtypeStruct((s, d), BF16)] if modulate_next
                   else jax.ShapeDtypeStruct((s, d), F32)),
        compiler_params=_params(1),
        name="out_proj",
    )(*ys, *gate_args, w, x, gate, *(next_mod if modulate_next else ()))


def _dft_split(s):
    n1 = 1 << (int(math.log2(s)) // 2)
    n2 = s // n1
    assert n1 * n2 == s and n1 % SUBLANES == 0 and n2 % SUBLANES == 0
    return n1, n2


def _angles(num, den):
    ang = (2.0 * math.pi / den) * (num % den)
    return np.cos(ang), np.sin(ang)


def _fold_cs_kernel(cs_ref, wf_ref, o_ref):
    wf = wf_ref[...]
    o_ref[:, 0:FOURIER_GROUP] = jnp.dot(cs_ref[0], wf, preferred_element_type=F32,
                                        precision=lax.Precision.HIGHEST)
    o_ref[:, FOURIER_GROUP:] = jnp.dot(cs_ref[1], wf, preferred_element_type=F32,
                                       precision=lax.Precision.HIGHEST)


def _fold_channel_dft(w_fourier_j, seq):
    fg = FOURIER_GROUP
    idx = np.arange(fg)
    cc, sc = _angles(idx[:, None] * idx[None, :], fg)
    cs = jnp.asarray(np.stack([cc, sc]) / math.sqrt(seq * fg), F32)
    return pl.pallas_call(
        _fold_cs_kernel,
        grid=(FOURIER_GROUPS,),
        in_specs=[pl.BlockSpec((2, fg, fg), lambda g: (0, 0, 0)),
                  pl.BlockSpec((None, fg, fg), lambda g: (g, 0, 0))],
        out_specs=pl.BlockSpec((None, fg, 2 * fg), lambda g: (g, 0, 0)),
        out_shape=jax.ShapeDtypeStruct((FOURIER_GROUPS, fg, 2 * fg), F32),
        compiler_params=_params(1),
        name="fold_channel_dft",
    )(cs, w_fourier_j)


def _fold_win_kernel(w_ref, ab_ref, o_ref):
    o_ref[...] = jnp.dot(w_ref[...], ab_ref[...].astype(BF16), preferred_element_type=F32).astype(BF16)


def _fold_in_proj(w_in, layer, ab):
    d = w_in.shape[1]
    fg = FOURIER_GROUP
    return pl.pallas_call(
        _fold_win_kernel,
        grid=(FOURIER_GROUPS, 2),
        in_specs=[pl.BlockSpec((None, d, fg), lambda g, t: (layer, 0, g)),
                  pl.BlockSpec((None, fg, fg), lambda g, t: (g, 0, t))],
        out_specs=pl.BlockSpec((d, fg), lambda g, t: (0, g + t * FOURIER_GROUPS)),
        out_shape=jax.ShapeDtypeStruct((d, 2 * D_MODEL), BF16),
        compiler_params=_params(2),
        name="fold_in_proj",
    )(w_in, ab)


HIGH_HALF = -65536


def _bf16_bits(v):
    b = lax.bitcast_convert_type(v, jnp.int32)
    return b + 0x7FFF + (lax.shift_right_logical(b, 16) & 1)


def _dft1_kernel(h_ref, wp_ref, wq_ref, f_ref, t_ref, *, n1, per_sub, n_sub):
    f = f_ref[...]
    for sb in range(n_sub):
        rows = slice(sb * per_sub * n1, (sb + 1) * per_sub * n1)
        h = h_ref[rows, :]
        p = jnp.dot(h, wp_ref[...], preferred_element_type=F32).astype(BF16)
        q = jnp.dot(h, wq_ref[...], preferred_element_type=F32).astype(BF16)
        for jj in range(per_sub):
            r = slice(jj * n1, (jj + 1) * n1)
            t = jnp.dot(f, jnp.concatenate([p[r], q[r]], axis=0), preferred_element_type=F32)
            t_ref[sb * per_sub + jj] = ((_bf16_bits(t[0:n1]) & HIGH_HALF)
                                        | lax.shift_right_logical(_bf16_bits(t[n1:2 * n1]), 16))


def _dft_stage1(h_t, w_pq, f1, n1, n2):
    s, d = h_t.shape
    c = w_pq.shape[1] // 2
    tm = min(PROJ_TM, s)
    tn = PROJ_TN
    per_sub = max(1, 256 // n1)
    n_sub = tm // (per_sub * n1)
    ncb = c // tn
    return pl.pallas_call(
        functools.partial(_dft1_kernel, n1=n1, per_sub=per_sub, n_sub=n_sub),
        grid=(s // tm, ncb),
        in_specs=[pl.BlockSpec((tm, d), lambda i, j: (i, 0)),
                  pl.BlockSpec((d, tn), lambda i, j: (0, j)),
                  pl.BlockSpec((d, tn), lambda i, j: (0, j + ncb)),
                  pl.BlockSpec((2 * n1, 2 * n1), lambda i, j: (0, 0))],
        out_specs=pl.BlockSpec((tm // n1, n1, tn), lambda i, j: (i, 0, j)),
        out_shape=jax.ShapeDtypeStruct((n2, n1, c), jnp.int32),
        compiler_params=_params(2),
        name="fourier_in_dft1",
    )(h_t, w_pq, w_pq, f1)


def _dft2_kernel(l_ref, t_ref, o_ref, *, n2, cb):
    packed = t_ref[...]
    re = lax.bitcast_convert_type(packed & HIGH_HALF, F32)
    im = lax.bitcast_convert_type(lax.shift_left(packed, 16), F32)
    rhs = jnp.stack([re, im], axis=1).reshape(n2 * 2 * SUBLANES, cb).astype(BF16)
    o_ref[...] = jnp.dot(l_ref[0], rhs, preferred_element_type=F32).reshape(n2, SUBLANES, cb)


def _dft_stage2(t_nk, l2, n1, n2):
    c = t_nk.shape[-1]
    cb = min(DFT_CB, c)
    row_blocks = pl.BlockSpec((n2, SUBLANES, cb), lambda g, j: (0, g, j))
    out = pl.pallas_call(
        functools.partial(_dft2_kernel, n2=n2, cb=cb),
        grid=(n1 // SUBLANES, c // cb),
        in_specs=[pl.BlockSpec((1, SUBLANES * n2, SUBLANES * 2 * n2), lambda g, j: (g, 0, 0)),
                  row_blocks],
        out_specs=row_blocks,
        out_shape=jax.ShapeDtypeStruct((n2, n1, c), F32),
        compiler_params=_params(2),
        name="dft_stage2",
    )(l2, t_nk)
    return out.reshape(n1 * n2, c)


def _dft_matrices(s):
    n1, n2 = _dft_split(s)
    a = np.arange(n1)
    c1, s1 = _angles(a[:, None] * a[None, :], n1)
    f1 = jnp.asarray(np.block([[c1, -s1], [-s1, -c1]]).astype(BF16))
    groups = n1 // SUBLANES
    g = np.arange(groups)[:, None, None, None]
    k2 = np.arange(n2)[None, :, None, None]
    j = np.arange(SUBLANES)[None, None, :, None]
    nn = np.arange(n2)[None, None, None, :]
    c2, s2 = _angles((SUBLANES * g + j + n1 * k2) * nn, s)
    cs = np.concatenate([c2, s2], axis=3).astype(BF16)
    return n1, n2, f1, _expand_twiddles(jnp.asarray(cs.reshape(groups, n2 * SUBLANES, 2 * n2)))


def _expand_kernel(cs_ref, e_ref, o_ref):
    spread = jnp.dot(cs_ref[0], e_ref[...], preferred_element_type=F32)
    row = lax.broadcasted_iota(jnp.int32, spread.shape, 0)
    col = lax.broadcasted_iota(jnp.int32, spread.shape, 1)
    o_ref[0] = jnp.where(row % SUBLANES == col % SUBLANES, spread, 0.0).astype(BF16)


def _expand_twiddles(cs):
    groups, rows, cols = cs.shape
    wide = cols * SUBLANES
    src = jnp.arange(cols, dtype=jnp.int32)[:, None]
    slot = (src % (cols // 2)) * 2 + src // (cols // 2)
    e = jnp.arange(wide, dtype=jnp.int32)[None, :] // SUBLANES == slot
    return pl.pallas_call(
        _expand_kernel,
        grid=(groups,),
        in_specs=[pl.BlockSpec((1, rows, cols), lambda g: (g, 0, 0)),
                  pl.BlockSpec((cols, wide), lambda g: (0, 0))],
        out_specs=pl.BlockSpec((1, rows, wide), lambda g: (g, 0, 0)),
        out_shape=jax.ShapeDtypeStruct((groups, rows, wide), BF16),
        compiler_params=_params(1),
        name="expand_twiddles",
    )(cs, e.astype(BF16))


def kernel(x, c, norm_w, ada_w, ada_b, w_in_ab, w_pool, pool_scale, q_norm_w, k_norm_w, lambda_q1, lambda_k1,
           lambda_q2, lambda_k2, subln_w, w_out_ab, w_in_c, w_fourier, w_out_c):
    batch, s, d = x.shape
    assert batch == 1 and d == D_MODEL and s % ATT_TK == 0
    xs = x.reshape(s, d)
    mod = _ada_mod(c, ada_w, ada_b)
    n1, n2, f1, l2 = _dft_matrices(s)
    row = lambda v: v.reshape(1, -1)

    def mod_of(i):
        return mod[i, :, 0:d], mod[i, :, d:2 * d], mod[i, :, 2 * d:3 * d]

    def finish(ys, w_out, j, xs, i, gate_from=None):
        gate = mod_of(i)[2]
        if i + 1 == DEPTH:
            return _out_proj(ys, w_out, j, xs, gate, gate_from=gate_from), None
        shift, scale, _ = mod_of(i + 1)
        return _out_proj(ys, w_out, j, xs, gate, (row(norm_w[i + 1]), shift, scale), gate_from=gate_from)

    w_in_ab, w_out_ab, w_in_c, w_out_c = (w.astype(BF16) for w in (w_in_ab, w_out_ab, w_in_c, w_out_c))
    h = _modulate(xs, row(norm_w[0]), *mod_of(0)[:2])
    for i in range(DEPTH):
        j = i // 2
        if i % 2 == 0:
            o1, o2, o3, o4 = POOL_WIDTH, POOL_WIDTH + DIFF_WIDTH, POOL_WIDTH + 2 * DIFF_WIDTH, POOL_WIDTH + 3 * DIFF_WIDTH
            lambda_init = 0.8 - 0.6 * math.exp(-0.3 * i)
            ug = _matmul(h, w_in_ab, F32, PROJ_TM, PROJ_TN, name="pool_gate_in_proj",
                         cols=((0, o1), (o4, o4 + AB_WIDTH)), layer=j)
            qt, ka, vt = _qkv_proj(h, w_in_ab, j, o1, o2, o3, row(q_norm_w[j]), row(k_norm_w[j]))
            y_a = _pool_mix(ug, w_pool[j].astype(BF16), row(pool_scale[j]))
            y_b = _diff_attn(qt, ka, vt, row(lambda_q1[j]), row(lambda_k1[j]), row(lambda_q2[j]),
                             row(lambda_k2[j]), subln_w[j], ug, lambda_init)
            xs, h = finish([y_a, y_b], w_out_ab, j, xs, i)
        else:
            ab = _fold_channel_dft(w_fourier[j], s)
            w_pq = _fold_in_proj(w_in_c, j, ab)
            h_t = h.reshape(n1, n2, d).transpose(1, 0, 2).reshape(s, d)
            t_nk = _dft_stage1(h_t, w_pq, f1, n1, n2)
            f = _dft_stage2(t_nk, l2, n1, n2)
            xs, h = finish([f], w_out_c, j, xs, i, gate_from=(h, w_in_c, d))
    return xs.reshape(batch, s, d)
```

```python
import functools
import math

import jax
import jax.numpy as jnp
import numpy as np
from jax import lax
from jax.experimental import pallas as pl
from jax.experimental.pallas import tpu as pltpu

F32 = jnp.float32
BF16 = jnp.bfloat16

D_MODEL = 2048
DEPTH = 4
NORM_EPS = 1e-6

POOL_WINDOWS = (2, 4, 8, 16)
POOL_WIDTH = D_MODEL // 2
POOL_GROUP = POOL_WIDTH // len(POOL_WINDOWS)
POOL_HALO = 8
HALO_BLOCK = 16

HEADS = 8
HEAD_DIM = 64
HEAD_V = 2 * HEAD_DIM
DIFF_WIDTH = HEADS * HEAD_V
AB_WIDTH = POOL_WIDTH + DIFF_WIDTH

FOURIER_GROUPS = 4
FOURIER_GROUP = D_MODEL // FOURIER_GROUPS

LANES = 128
SUBLANES = 8
POS_RADIX = 256
COEF_PARTS = 3
LOG2E = math.log2(math.e)

ATT_TQ = 512
ATT_TK = 512
ATT_DIAG = ATT_TQ // ATT_TK
ATT_UNROLL = 8
REF_MARGIN = 60.0
REF_FLOOR = 2.0 ** -60
BOUND_SLACK = 1.001

PROJ_TM, PROJ_TN = 2048, 1024
OUT_SUB = 256
DFT_CB = 2048

VMEM_LIMIT = 56 * 1024 * 1024


def _params(n_axes):
    return pltpu.CompilerParams(dimension_semantics=("arbitrary",) * n_axes,
                                vmem_limit_bytes=VMEM_LIMIT)


def _mod_kernel(c_ref, w_ref, b_ref, o_ref):
    c = c_ref[...]
    o_ref[...] = jnp.sum(jax.nn.silu(c) * w_ref[...], axis=0, keepdims=True) + b_ref[...]


def _ada_mod(c, ada_w, ada_b):
    depth, d, n = ada_w.shape
    tn = 1024
    return pl.pallas_call(
        _mod_kernel,
        grid=(depth, n // tn),
        in_specs=[pl.BlockSpec((d, 1), lambda i, j: (0, 0)),
                  pl.BlockSpec((None, d, tn), lambda i, j: (i, 0, j)),
                  pl.BlockSpec((None, 1, tn), lambda i, j: (i, 0, j))],
        out_specs=pl.BlockSpec((None, 1, tn), lambda i, j: (i, 0, j)),
        out_shape=jax.ShapeDtypeStruct((depth, 1, n), F32),
        compiler_params=_params(2),
        name="ada_mod",
    )(c.reshape(d, 1), ada_w, ada_b.reshape(depth, 1, n))


def _modulate_kernel(x_ref, nw_ref, shift_ref, scale_ref, o_ref):
    xf = x_ref[...]
    r = lax.rsqrt(jnp.mean(xf * xf, axis=-1, keepdims=True) + NORM_EPS)
    y = xf * r * nw_ref[...]
    y = y * (1.0 + scale_ref[...]) + shift_ref[...]
    o_ref[...] = y.astype(o_ref.dtype)


def _modulate(x, nw, shift, scale):
    s, d = x.shape
    tm = min(1024, s)
    vec = pl.BlockSpec((1, d), lambda i: (0, 0))
    return pl.pallas_call(
        _modulate_kernel,
        grid=(s // tm,),
        in_specs=[pl.BlockSpec((tm, d), lambda i: (i, 0)), vec, vec, vec],
        out_specs=pl.BlockSpec((tm, d), lambda i: (i, 0)),
        out_shape=jax.ShapeDtypeStruct((s, d), BF16),
        compiler_params=_params(1),
        name="modulate",
    )(x, nw, shift, scale)


def _mm_kernel(a_ref, w_ref, o_ref, *, precision):
    o_ref[...] = jnp.dot(a_ref[...], w_ref[...], preferred_element_type=F32,
                         precision=precision).astype(o_ref.dtype)


def _matmul(a, w, out_dtype, tm, tn, precision=None, name="matmul", cols=None, layer=None):
    m, k = a.shape
    cols = cols if cols is not None else ((0, w.shape[-1]),)
    n = sum(stop - start for start, stop in cols)
    tm, tn = min(tm, m), min(tn, n)
    assert all(start % tn == 0 and stop % tn == 0 for start, stop in cols)

    def wblock(j):
        jb, first = j, 0
        for start, stop in cols:
            jb = jnp.where(j >= first, j - first + start // tn, jb)
            first += (stop - start) // tn
        return jb

    if layer is None:
        wspec = pl.BlockSpec((k, tn), lambda i, j: (0, wblock(j)))
    else:
        wspec = pl.BlockSpec((None, k, tn), lambda i, j: (layer, 0, wblock(j)))
    return pl.pallas_call(
        functools.partial(_mm_kernel, precision=precision),
        grid=(m // tm, n // tn),
        in_specs=[pl.BlockSpec((tm, k), lambda i, j: (i, 0)), wspec],
        out_specs=pl.BlockSpec((tm, tn), lambda i, j: (i, j)),
        out_shape=jax.ShapeDtypeStruct((m, n), out_dtype),
        compiler_params=_params(2),
        name=name,
    )(a, w)


def _qkv_kernel(h_ref, wq_ref, wk_ref, wv_ref, qnw_ref, knw_ref, qt_ref, ka_ref, vt_ref, *, tm):
    lane = lax.broadcasted_iota(jnp.int32, (ATT_TK, LANES), 1)
    first = lane < HEAD_DIM
    pos = lax.broadcasted_iota(jnp.int32, (ATT_TK, LANES), 0)
    pos_lo = (pos % POS_RADIX).astype(F32)
    pos_hi = (pos // POS_RADIX).astype(F32)
    pos_cols = jnp.where(lane < HEAD_DIM + COEF_PARTS, pos_lo, jnp.where(lane < HEAD_DIM + 2 * COEF_PARTS, pos_hi, 0.0))

    def halves_rms(z, w):
        sq = z * z
        ss1 = jnp.sum(jnp.where(first, sq, 0.0), axis=-1, keepdims=True)
        ss2 = jnp.sum(jnp.where(first, 0.0, sq), axis=-1, keepdims=True)
        r = jnp.where(first, lax.rsqrt(ss1 / HEAD_DIM + NORM_EPS), lax.rsqrt(ss2 / HEAD_DIM + NORM_EPS))
        return z * r * w

    for cc in range(tm // ATT_TK):
        rows = slice(cc * ATT_TK, (cc + 1) * ATT_TK)
        h = h_ref[rows, :]
        zq = jnp.dot(h, wq_ref[...], preferred_element_type=F32)
        zk = jnp.dot(h, wk_ref[...], preferred_element_type=F32)
        zv = jnp.dot(h, wv_ref[...], preferred_element_type=F32)
        for hh in range(2):
            cols = slice(hh * LANES, (hh + 1) * LANES)
            qn = halves_rms(zq[:, cols], qnw_ref[...]) * (HEAD_DIM ** -0.5) * LOG2E
            kn = halves_rms(zk[:, cols], knw_ref[...])
            for m in range(2):
                qm = qn if m == 0 else pltpu.roll(qn, HEAD_DIM, axis=1)
                km = kn if m == 0 else pltpu.roll(kn, HEAD_DIM, axis=1)
                qt_ref[hh, m, :, rows] = jnp.where(first, qm, 0.0).T.astype(BF16)
                ka_ref[hh, m, cc] = jnp.where(first, km, pos_cols).astype(BF16)
            vt_ref[hh, cc] = zv[:, cols].T.astype(BF16)


def _qkv_proj(h, w, layer, col_q, col_k, col_v, qnw, knw):
    s, d = h.shape
    tm = min(2048, s)
    nchunk = s // ATT_TK
    cpt = tm // ATT_TK
    tn = 2 * LANES
    assert col_q % tn == 0 and col_k % tn == 0 and col_v % tn == 0
    wspecs = [pl.BlockSpec((None, d, tn), functools.partial(lambda i, j, jb: (layer, 0, j + jb), jb=c0 // tn))
              for c0 in (col_q, col_k, col_v)]
    vec = pl.BlockSpec((1, LANES), lambda i, j: (0, 0))
    return pl.pallas_call(
        functools.partial(_qkv_kernel, tm=tm),
        grid=(s // tm, HEADS // 2),
        in_specs=[pl.BlockSpec((tm, d), lambda i, j: (i, 0)), *wspecs, vec, vec],
        out_specs=[pl.BlockSpec((2, 2, LANES, tm), lambda i, j: (j, 0, 0, i)),
                   pl.BlockSpec((2, 2, cpt, ATT_TK, LANES), lambda i, j: (j, 0, i, 0, 0)),
                   pl.BlockSpec((2, cpt, HEAD_V, ATT_TK), lambda i, j: (j, i, 0, 0))],
        out_shape=[jax.ShapeDtypeStruct((HEADS, 2, LANES, s), BF16),
                   jax.ShapeDtypeStruct((HEADS, 2, nchunk, ATT_TK, LANES), BF16),
                   jax.ShapeDtypeStruct((HEADS, nchunk, HEAD_V, ATT_TK), BF16)],
        compiler_params=_params(2),
        name="qkv_proj",
    )(h, w, w, w, jnp.tile(qnw, (1, 2)), jnp.tile(knw, (1, 2)))


def _attn_kernel(slopes_ref, qt_ref, ka_ref, vt_ref, lq1_ref, lk1_ref, lq2_ref, lk2_ref, subw_ref, g_ref,
                 o_ref, acc_ref, mu_ref, qv_ref, kmax_ref, p_a, p_b, den_ref, *, lambda_init, nchunk, unroll):
    hd = pl.program_id(0)
    i = pl.program_id(1)
    base = hd * (1 + COEF_PARTS)
    slope = slopes_ref[base]
    q_off = lax.broadcasted_iota(jnp.int32, (1, ATT_TQ), 1).astype(F32)

    row1 = lax.broadcasted_iota(jnp.int32, (LANES - HEAD_DIM, 1), 0)
    coef_col = jnp.zeros((LANES - HEAD_DIM, 1), F32)
    for t in range(COEF_PARTS):
        piece = slopes_ref[base + 1 + t]
        coef_col = jnp.where(row1 == t, piece, coef_col)
        coef_col = jnp.where(row1 == COEF_PARTS + t, piece * POS_RADIX, coef_col)
    coef_rows = jnp.broadcast_to(coef_col, (LANES - HEAD_DIM, ATT_TQ))
    for m in range(2):
        for kind, sign in ((0, -1.0), (1, 0.0), (2, 1.0)):
            qv_ref[kind, m, 0:HEAD_DIM, :] = qt_ref[0, m, 0:HEAD_DIM, :]
            qv_ref[kind, m, HEAD_DIM:LANES, :] = (sign * coef_rows).astype(BF16)

    @pl.when(i == 0)
    def _():
        lane = lax.broadcasted_iota(jnp.int32, (ATT_TK, LANES), 1)
        for m in range(2):
            def widest(c, best):
                kc = ka_ref[0, m, c].astype(F32)
                return jnp.maximum(best, jnp.sum(jnp.where(lane < HEAD_DIM, kc * kc, 0.0), axis=1, keepdims=True))
            kmax_ref[m] = jnp.max(lax.fori_loop(0, nchunk, widest, jnp.zeros((ATT_TK, 1), F32)))

    for m in range(2):
        qf = qt_ref[0, m, 0:HEAD_DIM, :].astype(F32)
        bound = jnp.sqrt(jnp.sum(qf * qf, axis=0, keepdims=True) * kmax_ref[m]) * BOUND_SLACK
        mu_ref[m] = bound - REF_MARGIN

    c_diag = i * ATT_DIAG

    def chunk_of(e):
        if isinstance(e, int) and e < ATT_DIAG:
            return c_diag + e
        rest = e - ATT_DIAG
        return jnp.where(e < ATT_DIAG, c_diag + e, rest + ATT_DIAG * (rest >= c_diag).astype(jnp.int32))

    def side_of(c):
        return (c < c_diag).astype(jnp.int32) - (c >= c_diag + ATT_DIAG).astype(jnp.int32)

    p_bufs = (p_a, p_b)

    def probs(e, p_ref):
        c = chunk_of(e)
        side = side_of(c)
        sidef = side.astype(F32)
        gap = jnp.abs(c * ATT_TK - i * ATT_TQ).astype(F32)
        cvec = -slope * (sidef * sidef * gap + sidef * q_off)
        for m in range(2):
            st = jnp.dot(ka_ref[0, m, c], qv_ref[1 + side, m], preferred_element_type=F32)
            if isinstance(e, int) and e < ATT_DIAG:
                key_off = lax.broadcasted_iota(jnp.int32, (ATT_TK, ATT_TQ), 0) + e * ATT_TK
                qry_off = lax.broadcasted_iota(jnp.int32, (ATT_TK, ATT_TQ), 1)
                st = st - slope * jnp.abs(key_off - qry_off).astype(F32)
            p = jnp.exp2(st - (mu_ref[m] - cvec))
            p_ref[m] = p.astype(BF16)
            den_ref[m] += jnp.sum(p.reshape(ATT_TK // SUBLANES, SUBLANES, ATT_TQ), axis=0)

    def group(e0, count, probs_last):
        sums = [None, None]
        for u in range(count):
            if u < count - 1 or probs_last:
                probs(e0 + u + 1, p_bufs[(u + 1) % 2])
            c = chunk_of(e0 + u)
            for m in range(2):
                pv = jnp.dot(vt_ref[0, c], p_bufs[u % 2][m], preferred_element_type=F32)
                sums[m] = pv if sums[m] is None else sums[m] + pv
        for m in range(2):
            acc_ref[m] += sums[m]

    def loop_body(it, carry):
        group(it * unroll, unroll, True)
        return carry

    def one_pass(_):
        acc_ref[...] = jnp.zeros(acc_ref.shape, F32)
        den_ref[...] = jnp.zeros(den_ref.shape, F32)
        n_groups = nchunk // unroll
        probs(0, p_a)
        group(0, unroll, True)
        lax.fori_loop(1, n_groups - 1, loop_body, 0)
        group((n_groups - 1) * unroll, unroll, False)
        lowest = None
        for m in range(2):
            den = jnp.sum(den_ref[m], axis=0, keepdims=True)
            mu_ref[m] = jnp.where(den < REF_FLOOR, mu_ref[m] - 2.0 * REF_MARGIN, mu_ref[m])
            lowest = jnp.min(den) if lowest is None else jnp.minimum(lowest, jnp.min(den))
        return (lowest < REF_FLOOR).astype(jnp.int32)

    lax.while_loop(lambda retry: retry > 0, one_pass, jnp.int32(1))

    lam = (jnp.exp(jnp.sum(lq1_ref[...] * lk1_ref[...], axis=-1, keepdims=True))
           - jnp.exp(jnp.sum(lq2_ref[...] * lk2_ref[...], axis=-1, keepdims=True)) + lambda_init)
    den1 = jnp.sum(den_ref[0], axis=0, keepdims=True)
    den2 = jnp.sum(den_ref[1], axis=0, keepdims=True)
    o = acc_ref[0] / den1 - lam * (acc_ref[1] / den2)
    r = lax.rsqrt(jnp.mean(o * o, axis=0, keepdims=True) + NORM_EPS)
    y = (o * r * subw_ref[...]) * (1.0 - lambda_init)
    o_ref[...] = (y.T * jax.nn.silu(g_ref[...])).astype(o_ref.dtype)


def _diff_attn(qt, ka, vt, lq1, lk1, lq2, lk2, subw, g, lambda_init):
    s = qt.shape[-1]
    g_col0 = (g.shape[1] - DIFF_WIDTH) // HEAD_V
    nchunk = s // ATT_TK
    unroll = min(ATT_UNROLL, nchunk // 2)
    assert unroll % 2 == 0 and nchunk % unroll == 0 and unroll >= ATT_DIAG and s % ATT_TQ == 0
    whole = jnp.asarray([LOG2E * 2.0 ** (-8.0 * (h + 1) / HEADS) for h in range(HEADS)], F32)
    pieces, rest = [], whole
    for _ in range(COEF_PARTS):
        pieces.append(rest.astype(BF16).astype(F32))
        rest = rest - pieces[-1]
    slopes = jnp.stack([whole] + pieces, axis=1).reshape(-1)
    vec = pl.BlockSpec((1, HEAD_DIM), lambda h, i: (0, 0))
    return pl.pallas_call(
        functools.partial(_attn_kernel, lambda_init=lambda_init, nchunk=nchunk, unroll=unroll),
        grid=(HEADS, s // ATT_TQ),
        in_specs=[pl.BlockSpec(memory_space=pltpu.SMEM),
                  pl.BlockSpec((1, 2, LANES, ATT_TQ), lambda h, i: (h, 0, 0, i)),
                  pl.BlockSpec((1, 2, nchunk, ATT_TK, LANES), lambda h, i: (h, 0, 0, 0, 0)),
                  pl.BlockSpec((1, nchunk, HEAD_V, ATT_TK), lambda h, i: (h, 0, 0, 0)),
                  vec, vec, vec, vec,
                  pl.BlockSpec((HEAD_V, 1), lambda h, i: (0, 0)),
                  pl.BlockSpec((ATT_TQ, HEAD_V), lambda h, i: (i, g_col0 + h))],
        out_specs=pl.BlockSpec((ATT_TQ, HEAD_V), lambda h, i: (i, h)),
        out_shape=jax.ShapeDtypeStruct((s, DIFF_WIDTH), BF16),
        scratch_shapes=[pltpu.VMEM((2, HEAD_V, ATT_TQ), F32), pltpu.VMEM((2, 1, ATT_TQ), F32),
                        pltpu.VMEM((3, 2, LANES, ATT_TQ), BF16), pltpu.SMEM((2,), F32),
                        pltpu.VMEM((2, ATT_TK, ATT_TQ), BF16), pltpu.VMEM((2, ATT_TK, ATT_TQ), BF16),
                        pltpu.VMEM((2, SUBLANES, ATT_TQ), F32)],
        compiler_params=_params(2),
        name="diff_attn",
    )(slopes, qt, ka, vt, lq1, lk1, lq2, lk2, subw.reshape(HEAD_V, 1), g)


def _pool_kernel(hprev_ref, h_ref, hnext_ref, wu_ref, wg_ref, w_ref, scale_ref, o_ref, ext_ref, cur_ref, gate_ref,
                 *, tm, seq):
    i = pl.program_id(0)
    last = pl.num_programs(0) - 1
    t = lax.broadcasted_iota(jnp.int32, (tm, 1), 0) + i * tm
    pair = 2 * POOL_GROUP
    for g, w in enumerate(POOL_WINDOWS):
        cols = pl.ds(g * POOL_GROUP, POOL_GROUP)
        if g % 2 == 0:
            cols2 = pl.ds(g * POOL_GROUP, pair)
            wu = wu_ref[:, cols2]
            u = jnp.dot(h_ref[...], wu, preferred_element_type=F32)
            u_before = jnp.dot(hprev_ref[...], wu, preferred_element_type=F32)[HALO_BLOCK - POOL_HALO:HALO_BLOCK]
            u_after = jnp.dot(hnext_ref[...], wu, preferred_element_type=F32)[0:POOL_HALO]
            cur_ref[:, cols2] = u
            ext_ref[pl.ds(0, POOL_HALO), cols2] = jnp.where(i > 0, u_before, 0.0)
            ext_ref[pl.ds(POOL_HALO, tm), cols2] = u
            ext_ref[pl.ds(POOL_HALO + tm, POOL_HALO), cols2] = jnp.where(i < last, u_after, 0.0)
            gate_ref[:, cols2] = jnp.dot(h_ref[...], wg_ref[:, cols2], preferred_element_type=F32)
        span = tm + 2 * POOL_HALO
        width = 1
        while width < w:
            span -= width
            ext_ref[pl.ds(0, span), cols] = ext_ref[pl.ds(0, span), cols] + ext_ref[pl.ds(width, span), cols]
            width *= 2
        assert width == w
        win = ext_ref[pl.ds(POOL_HALO - w // 2, tm), cols]
        lo = jnp.maximum(t - w // 2, 0)
        hi = jnp.minimum(t + (w - w // 2) - 1, seq - 1)
        cnt = (hi - lo + 1).astype(F32)
        pooled = win * (1.0 / cnt) - cur_ref[:, cols]
        y = jnp.dot(pooled.astype(BF16), w_ref[g], preferred_element_type=F32)
        o_ref[:, cols] = (y * scale_ref[:, cols] * jax.nn.silu(gate_ref[:, cols])).astype(o_ref.dtype)


def _pool_branch(h, w_in, layer, col_u, col_g, w_pool, pool_scale):
    s, d = h.shape
    width = POOL_WIDTH
    tm = min(1024, s)
    hb = tm // HALO_BLOCK
    nhalo = s // HALO_BLOCK
    assert col_u % width == 0 and col_g % width == 0
    once = pl.Buffered(1)
    return pl.pallas_call(
        functools.partial(_pool_kernel, tm=tm, seq=s),
        grid=(s // tm,),
        in_specs=[pl.BlockSpec((HALO_BLOCK, d), lambda i: (jnp.maximum(i * hb - 1, 0), 0)),
                  pl.BlockSpec((tm, d), lambda i: (i, 0)),
                  pl.BlockSpec((HALO_BLOCK, d), lambda i: (jnp.minimum((i + 1) * hb, nhalo - 1), 0)),
                  pl.BlockSpec((None, d, width), lambda i: (layer, 0, col_u // width), pipeline_mode=once),
                  pl.BlockSpec((None, d, width), lambda i: (layer, 0, col_g // width), pipeline_mode=once),
                  pl.BlockSpec(w_pool.shape, lambda i: (0, 0, 0)),
                  pl.BlockSpec((1, width), lambda i: (0, 0))],
        out_specs=pl.BlockSpec((tm, width), lambda i: (i, 0)),
        out_shape=jax.ShapeDtypeStruct((s, width), BF16),
        scratch_shapes=[pltpu.VMEM((tm + 2 * POOL_HALO, width), F32), pltpu.VMEM((tm, width), F32),
                        pltpu.VMEM((tm, width), F32)],
        compiler_params=_params(1),
        name="pool_branch",
    )(h, h, h, w_in, w_in, w_pool, pool_scale)


def _out_kernel(*refs, widths, tm, modulate_next, gate_proj):
    n = len(widths)
    y_refs = refs[:n]
    refs = refs[n:]
    if gate_proj:
        hin_ref, wg_ref = refs[:2]
        refs = refs[2:]
    w_ref, x_ref, gate_ref = refs[:3]
    if modulate_next:
        nw_ref, shift_ref, scale_ref, o_ref, h_ref = refs[3:]
        mul = nw_ref[...]
        one_plus_scale = 1.0 + scale_ref[...]
    else:
        (o_ref,) = refs[3:]
    sub = min(OUT_SUB, tm // 2)
    for sb in range(tm // sub):
        rows = slice(sb * sub, (sb + 1) * sub)
        acc = None
        off = 0
        for y_ref, wd in zip(y_refs, widths):
            y = y_ref[rows, :]
            if gate_proj:
                y = y * jax.nn.silu(jnp.dot(hin_ref[rows, :], wg_ref[:, off:off + wd], preferred_element_type=F32))
            part = jnp.dot(y.astype(BF16), w_ref[off:off + wd, :], preferred_element_type=F32)
            acc = part if acc is None else acc + part
            off += wd
        xn = x_ref[rows, :] + gate_ref[...] * acc
        o_ref[rows, :] = xn
        if modulate_next:
            r = lax.rsqrt(jnp.mean(xn * xn, axis=-1, keepdims=True) + NORM_EPS)
            h_ref[rows, :] = ((xn * r * mul) * one_plus_scale + shift_ref[...]).astype(BF16)


def _out_proj(ys, w, layer, x, gate, next_mod=None, gate_from=None):
    s, d = x.shape
    k = w.shape[1]
    gate_proj = gate_from is not None
    tm = min(512, s)
    once = pl.Buffered(1) if gate_proj else None
    widths = tuple(y.shape[1] for y in ys)
    vec = pl.BlockSpec((1, d), lambda i: (0, 0))
    rows = pl.BlockSpec((tm, d), lambda i: (i, 0))
    modulate_next = next_mod is not None
    gate_specs, gate_args = [], ()
    if gate_proj:
        hin, wg, col0 = gate_from
        assert col0 % k == 0 and sum(widths) == k
        gate_specs = [pl.BlockSpec((tm, hin.shape[1]), lambda i: (i, 0)),
                      pl.BlockSpec((None, wg.shape[1], k), lambda i: (layer, 0, col0 // k), pipeline_mode=once)]
        gate_args = (hin, wg)
    return pl.pallas_call(
        functools.partial(_out_kernel, widths=widths, tm=tm, modulate_next=modulate_next, gate_proj=gate_proj),
        grid=(s // tm,),
        in_specs=[pl.BlockSpec((tm, wd), lambda i: (i, 0)) for wd in widths] + gate_specs
        + [pl.BlockSpec((None, k, d), lambda i: (layer, 0, 0), pipeline_mode=once), rows, vec]
        + ([vec, vec, vec] if modulate_next else []),
        out_specs=[rows, rows] if modulate_next else rows,
        out_shape=([jax.ShapeDtypeStruct((s, d), F32), jax.ShapeDtypeStruct((s, d), BF16)] if modulate_next
                   else jax.ShapeDtypeStruct((s, d), F32)),
        compiler_params=_params(1),
        name="out_proj",
    )(*ys, *gate_args, w, x, gate, *(next_mod if modulate_next else ()))


def _dft_split(s):
    n1 = 1 << (int(math.log2(s)) // 2)
    n2 = s // n1
    assert n1 * n2 == s and n1 % SUBLANES == 0 and n2 % SUBLANES == 0
    return n1, n2


def _angles(num, den):
    ang = (2.0 * math.pi / den) * (num % den)
    return np.cos(ang), np.sin(ang)


def _fold_cs_kernel(cs_ref, wf_ref, o_ref):
    wf = wf_ref[...]
    o_ref[:, 0:FOURIER_GROUP] = jnp.dot(cs_ref[0], wf, preferred_element_type=F32,
                                        precision=lax.Precision.HIGHEST)
    o_ref[:, FOURIER_GROUP:] = jnp.dot(cs_ref[1], wf, preferred_element_type=F32,
                                       precision=lax.Precision.HIGHEST)


def _fold_channel_dft(w_fourier_j, seq):
    fg = FOURIER_GROUP
    idx = np.arange(fg)
    cc, sc = _angles(idx[:, None] * idx[None, :], fg)
    cs = jnp.asarray(np.stack([cc, sc]) / math.sqrt(seq * fg), F32)
    return pl.pallas_call(
        _fold_cs_kernel,
        grid=(FOURIER_GROUPS,),
        in_specs=[pl.BlockSpec((2, fg, fg), lambda g: (0, 0, 0)),
                  pl.BlockSpec((None, fg, fg), lambda g: (g, 0, 0))],
        out_specs=pl.BlockSpec((None, fg, 2 * fg), lambda g: (g, 0, 0)),
        out_shape=jax.ShapeDtypeStruct((FOURIER_GROUPS, fg, 2 * fg), F32),
        compiler_params=_params(1),
        name="fold_channel_dft",
    )(cs, w_fourier_j)


def _fold_win_kernel(w_ref, ab_ref, o_ref):
    o_ref[...] = jnp.dot(w_ref[...], ab_ref[...].astype(BF16), preferred_element_type=F32).astype(BF16)


def _fold_in_proj(w_in, layer, ab):
    d = w_in.shape[1]
    fg = FOURIER_GROUP
    return pl.pallas_call(
        _fold_win_kernel,
        grid=(FOURIER_GROUPS, 2),
        in_specs=[pl.BlockSpec((None, d, fg), lambda g, t: (layer, 0, g)),
                  pl.BlockSpec((None, fg, fg), lambda g, t: (g, 0, t))],
        out_specs=pl.BlockSpec((d, fg), lambda g, t: (0, g + t * FOURIER_GROUPS)),
        out_shape=jax.ShapeDtypeStruct((d, 2 * D_MODEL), BF16),
        compiler_params=_params(2),
        name="fold_in_proj",
    )(w_in, ab)


HIGH_HALF = -65536


def _bf16_bits(v):
    b = lax.bitcast_convert_type(v, jnp.int32)
    return b + 0x7FFF + (lax.shift_right_logical(b, 16) & 1)


def _dft1_kernel(h_ref, wp_ref, wq_ref, f_ref, t_ref, *, n1, per_sub, n_sub):
    f = f_ref[...]
    for sb in range(n_sub):
        rows = slice(sb * per_sub * n1, (sb + 1) * per_sub * n1)
        h = h_ref[rows, :]
        p = jnp.dot(h, wp_ref[...], preferred_element_type=F32).astype(BF16)
        q = jnp.dot(h, wq_ref[...], preferred_element_type=F32).astype(BF16)
        for jj in range(per_sub):
            r = slice(jj * n1, (jj + 1) * n1)
            t = jnp.dot(f, jnp.concatenate([p[r], q[r]], axis=0), preferred_element_type=F32)
            t_ref[sb * per_sub + jj] = ((_bf16_bits(t[0:n1]) & HIGH_HALF)
                                        | lax.shift_right_logical(_bf16_bits(t[n1:2 * n1]), 16))


def _dft_stage1(h_t, w_pq, f1, n1, n2):
    s, d = h_t.shape
    c = w_pq.shape[1] // 2
    tm = min(PROJ_TM, s)
    tn = PROJ_TN
    per_sub = max(1, 256 // n1)
    n_sub = tm // (per_sub * n1)
    ncb = c // tn
    return pl.pallas_call(
        functools.partial(_dft1_kernel, n1=n1, per_sub=per_sub, n_sub=n_sub),
        grid=(s // tm, ncb),
        in_specs=[pl.BlockSpec((tm, d), lambda i, j: (i, 0)),
                  pl.BlockSpec((d, tn), lambda i, j: (0, j)),
                  pl.BlockSpec((d, tn), lambda i, j: (0, j + ncb)),
                  pl.BlockSpec((2 * n1, 2 * n1), lambda i, j: (0, 0))],
        out_specs=pl.BlockSpec((tm // n1, n1, tn), lambda i, j: (i, 0, j)),
        out_shape=jax.ShapeDtypeStruct((n2, n1, c), jnp.int32),
        compiler_params=_params(2),
        name="fourier_in_dft1",
    )(h_t, w_pq, w_pq, f1)


def _dft2_kernel(l_ref, t_ref, o_ref, *, n2, cb):
    packed = t_ref[...]
    re = lax.bitcast_convert_type(packed & HIGH_HALF, F32)
    im = lax.bitcast_convert_type(lax.shift_left(packed, 16), F32)
    rhs = jnp.stack([re, im], axis=1).reshape(n2 * 2 * SUBLANES, cb).astype(BF16)
    o_ref[...] = jnp.dot(l_ref[0], rhs, preferred_element_type=F32).reshape(n2, SUBLANES, cb)


def _dft_stage2(t_nk, l2, n1, n2):
    c = t_nk.shape[-1]
    cb = min(DFT_CB, c)
    row_blocks = pl.BlockSpec((n2, SUBLANES, cb), lambda g, j: (0, g, j))
    out = pl.pallas_call(
        functools.partial(_dft2_kernel, n2=n2, cb=cb),
        grid=(n1 // SUBLANES, c // cb),
        in_specs=[pl.BlockSpec((1, SUBLANES * n2, SUBLANES * 2 * n2), lambda g, j: (g, 0, 0)),
                  row_blocks],
        out_specs=row_blocks,
        out_shape=jax.ShapeDtypeStruct((n2, n1, c), F32),
        compiler_params=_params(2),
        name="dft_stage2",
    )(l2, t_nk)
    return out.reshape(n1 * n2, c)


def _dft_matrices(s):
    n1, n2 = _dft_split(s)
    a = np.arange(n1)
    c1, s1 = _angles(a[:, None] * a[None, :], n1)
    f1 = jnp.asarray(np.block([[c1, -s1], [-s1, -c1]]).astype(BF16))
    groups = n1 // SUBLANES
    g = np.arange(groups)[:, None, None, None]
    k2 = np.arange(n2)[None, :, None, None]
    j = np.arange(SUBLANES)[None, None, :, None]
    nn = np.arange(n2)[None, None, None, :]
    c2, s2 = _angles((SUBLANES * g + j + n1 * k2) * nn, s)
    cs = np.concatenate([c2, s2], axis=3).astype(BF16)
    return n1, n2, f1, _expand_twiddles(jnp.asarray(cs.reshape(groups, n2 * SUBLANES, 2 * n2)))


def _expand_kernel(cs_ref, e_ref, o_ref):
    spread = jnp.dot(cs_ref[0], e_ref[...], preferred_element_type=F32)
    row = lax.broadcasted_iota(jnp.int32, spread.shape, 0)
    col = lax.broadcasted_iota(jnp.int32, spread.shape, 1)
    o_ref[0] = jnp.where(row % SUBLANES == col % SUBLANES, spread, 0.0).astype(BF16)


def _expand_twiddles(cs):
    groups, rows, cols = cs.shape
    wide = cols * SUBLANES
    src = jnp.arange(cols, dtype=jnp.int32)[:, None]
    slot = (src % (cols // 2)) * 2 + src // (cols // 2)
    e = jnp.arange(wide, dtype=jnp.int32)[None, :] // SUBLANES == slot
    return pl.pallas_call(
        _expand_kernel,
        grid=(groups,),
        in_specs=[pl.BlockSpec((1, rows, cols), lambda g: (g, 0, 0)),
                  pl.BlockSpec((cols, wide), lambda g: (0, 0))],
        out_specs=pl.BlockSpec((1, rows, wide), lambda g: (g, 0, 0)),
        out_shape=jax.ShapeDtypeStruct((groups, rows, wide), BF16),
        compiler_params=_params(1),
        name="expand_twiddles",
    )(cs, e.astype(BF16))


def kernel(x, c, norm_w, ada_w, ada_b, w_in_ab, w_pool, pool_scale, q_norm_w, k_norm_w, lambda_q1, lambda_k1,
           lambda_q2, lambda_k2, subln_w, w_out_ab, w_in_c, w_fourier, w_out_c):
    batch, s, d = x.shape
    assert batch == 1 and d == D_MODEL and s % ATT_TK == 0
    xs = x.reshape(s, d)
    mod = _ada_mod(c, ada_w, ada_b)
    n1, n2, f1, l2 = _dft_matrices(s)
    row = lambda v: v.reshape(1, -1)

    def mod_of(i):
        return mod[i, :, 0:d], mod[i, :, d:2 * d], mod[i, :, 2 * d:3 * d]

    def finish(ys, w_out, j, xs, i, gate_from=None):
        gate = mod_of(i)[2]
        if i + 1 == DEPTH:
            return _out_proj(ys, w_out, j, xs, gate, gate_from=gate_from), None
        shift, scale, _ = mod_of(i + 1)
        return _out_proj(ys, w_out, j, xs, gate, (row(norm_w[i + 1]), shift, scale), gate_from=gate_from)

    w_in_ab, w_out_ab, w_in_c, w_out_c = (w.astype(BF16) for w in (w_in_ab, w_out_ab, w_in_c, w_out_c))
    h = _modulate(xs, row(norm_w[0]), *mod_of(0)[:2])
    for i in range(DEPTH):
        j = i // 2
        if i % 2 == 0:
            o1, o2, o3, o4 = POOL_WIDTH, POOL_WIDTH + DIFF_WIDTH, POOL_WIDTH + 2 * DIFF_WIDTH, POOL_WIDTH + 3 * DIFF_WIDTH
            lambda_init = 0.8 - 0.6 * math.exp(-0.3 * i)
            y_a = _pool_branch(h, w_in_ab, j, 0, o4, w_pool[j].astype(BF16), row(pool_scale[j]))
            g_attn = _matmul(h, w_in_ab, F32, PROJ_TM, PROJ_TN, name="gate_in_proj",
                             cols=((o4 + POOL_WIDTH, o4 + AB_WIDTH),), layer=j)
            qt, ka, vt = _qkv_proj(h, w_in_ab, j, o1, o2, o3, row(q_norm_w[j]), row(k_norm_w[j]))
            y_b = _diff_attn(qt, ka, vt, row(lambda_q1[j]), row(lambda_k1[j]), row(lambda_q2[j]),
                             row(lambda_k2[j]), subln_w[j], g_attn, lambda_init)
            xs, h = finish([y_a, y_b], w_out_ab, j, xs, i)
        else:
            ab = _fold_channel_dft(w_fourier[j], s)
            w_pq = _fold_in_proj(w_in_c, j, ab)
            h_t = h.reshape(n1, n2, d).transpose(1, 0, 2).reshape(s, d)
            t_nk = _dft_stage1(h_t, w_pq, f1, n1, n2)
            f = _dft_stage2(t_nk, l2, n1, n2)
            xs, h = finish([f], w_out_c, j, xs, i, gate_from=(h, w_in_c, d))
    return xs.reshape(batch, s, d)
```

```python
import functools
import math

import jax
import jax.numpy as jnp
import numpy as np
from jax import lax
from jax.experimental import pallas as pl
from jax.experimental.pallas import tpu as pltpu

F32 = jnp.float32
BF16 = jnp.bfloat16

D_MODEL = 2048
DEPTH = 4
NORM_EPS = 1e-6

POOL_WINDOWS = (2, 4, 8, 16)
POOL_WIDTH = D_MODEL // 2
POOL_GROUP = POOL_WIDTH // len(POOL_WINDOWS)
POOL_HALO = 8
HALO_BLOCK = 16

HEADS = 8
HEAD_DIM = 64
HEAD_V = 2 * HEAD_DIM
DIFF_WIDTH = HEADS * HEAD_V
AB_WIDTH = POOL_WIDTH + DIFF_WIDTH

FOURIER_GROUPS = 4
FOURIER_GROUP = D_MODEL // FOURIER_GROUPS

LANES = 128
SUBLANES = 8
POS_RADIX = 256
COEF_PARTS = 3
LOG2E = math.log2(math.e)

ATT_TQ = 512
ATT_TK = 512
ATT_DIAG = ATT_TQ // ATT_TK
ATT_UNROLL = 8
REF_MARGIN = 60.0
REF_FLOOR = 2.0 ** -60
BOUND_SLACK = 1.001

PROJ_TM, PROJ_TN = 2048, 1024
QKV_HEADS = 4
OUT_SUB = 256
DFT_CB = 2048

VMEM_LIMIT = 56 * 1024 * 1024


def _params(n_axes):
    return pltpu.CompilerParams(dimension_semantics=("arbitrary",) * n_axes,
                                vmem_limit_bytes=VMEM_LIMIT)


def _mod_kernel(c_ref, w_ref, b_ref, o_ref):
    c = c_ref[...]
    o_ref[...] = jnp.sum(jax.nn.silu(c) * w_ref[...], axis=0, keepdims=True) + b_ref[...]


def _ada_mod(c, ada_w, ada_b):
    depth, d, n = ada_w.shape
    tn = 1024
    return pl.pallas_call(
        _mod_kernel,
        grid=(depth, n // tn),
        in_specs=[pl.BlockSpec((d, 1), lambda i, j: (0, 0)),
                  pl.BlockSpec((None, d, tn), lambda i, j: (i, 0, j)),
                  pl.BlockSpec((None, 1, tn), lambda i, j: (i, 0, j))],
        out_specs=pl.BlockSpec((None, 1, tn), lambda i, j: (i, 0, j)),
        out_shape=jax.ShapeDtypeStruct((depth, 1, n), F32),
        compiler_params=_params(2),
        name="ada_mod",
    )(c.reshape(d, 1), ada_w, ada_b.reshape(depth, 1, n))


def _modulate_kernel(x_ref, nw_ref, shift_ref, scale_ref, o_ref):
    xf = x_ref[...]
    r = lax.rsqrt(jnp.mean(xf * xf, axis=-1, keepdims=True) + NORM_EPS)
    y = xf * r * nw_ref[...]
    y = y * (1.0 + scale_ref[...]) + shift_ref[...]
    o_ref[...] = y.astype(o_ref.dtype)


def _modulate(x, nw, shift, scale):
    s, d = x.shape
    tm = min(1024, s)
    vec = pl.BlockSpec((1, d), lambda i: (0, 0))
    return pl.pallas_call(
        _modulate_kernel,
        grid=(s // tm,),
        in_specs=[pl.BlockSpec((tm, d), lambda i: (i, 0)), vec, vec, vec],
        out_specs=pl.BlockSpec((tm, d), lambda i: (i, 0)),
        out_shape=jax.ShapeDtypeStruct((s, d), BF16),
        compiler_params=_params(1),
        name="modulate",
    )(x, nw, shift, scale)


def _mm_kernel(a_ref, w_ref, o_ref, *, precision):
    o_ref[...] = jnp.dot(a_ref[...], w_ref[...], preferred_element_type=F32,
                         precision=precision).astype(o_ref.dtype)


def _matmul(a, w, out_dtype, tm, tn, precision=None, name="matmul", cols=None, layer=None):
    m, k = a.shape
    cols = cols if cols is not None else ((0, w.shape[-1]),)
    n = sum(stop - start for start, stop in cols)
    tm, tn = min(tm, m), min(tn, n)
    assert all(start % tn == 0 and stop % tn == 0 for start, stop in cols)

    def wblock(j):
        jb, first = j, 0
        for start, stop in cols:
            jb = jnp.where(j >= first, j - first + start // tn, jb)
            first += (stop - start) // tn
        return jb

    if layer is None:
        wspec = pl.BlockSpec((k, tn), lambda i, j: (0, wblock(j)))
    else:
        wspec = pl.BlockSpec((None, k, tn), lambda i, j: (layer, 0, wblock(j)))
    return pl.pallas_call(
        functools.partial(_mm_kernel, precision=precision),
        grid=(m // tm, n // tn),
        in_specs=[pl.BlockSpec((tm, k), lambda i, j: (i, 0)), wspec],
        out_specs=pl.BlockSpec((tm, tn), lambda i, j: (i, j)),
        out_shape=jax.ShapeDtypeStruct((m, n), out_dtype),
        compiler_params=_params(2),
        name=name,
    )(a, w)


def _qkv_kernel(h_ref, wq_ref, wk_ref, wv_ref, wg_ref, qnw_ref, knw_ref, qt_ref, ka_ref, vt_ref, g_ref, *, tm):
    lane = lax.broadcasted_iota(jnp.int32, (ATT_TK, LANES), 1)
    first = lane < HEAD_DIM
    pos = lax.broadcasted_iota(jnp.int32, (ATT_TK, LANES), 0)
    pos_lo = (pos % POS_RADIX).astype(F32)
    pos_hi = (pos // POS_RADIX).astype(F32)
    pos_cols = jnp.where(lane < HEAD_DIM + COEF_PARTS, pos_lo, jnp.where(lane < HEAD_DIM + 2 * COEF_PARTS, pos_hi, 0.0))

    def halves_rms(z, w):
        sq = z * z
        ss1 = jnp.sum(jnp.where(first, sq, 0.0), axis=-1, keepdims=True)
        ss2 = jnp.sum(jnp.where(first, 0.0, sq), axis=-1, keepdims=True)
        r = jnp.where(first, lax.rsqrt(ss1 / HEAD_DIM + NORM_EPS), lax.rsqrt(ss2 / HEAD_DIM + NORM_EPS))
        return z * r * w

    for cc in range(tm // ATT_TK):
        rows = slice(cc * ATT_TK, (cc + 1) * ATT_TK)
        h = h_ref[rows, :]
        zq = jnp.dot(h, wq_ref[...], preferred_element_type=F32)
        zk = jnp.dot(h, wk_ref[...], preferred_element_type=F32)
        zv = jnp.dot(h, wv_ref[...], preferred_element_type=F32)
        g_ref[rows, :] = jnp.dot(h, wg_ref[...], preferred_element_type=F32)
        for hh in range(QKV_HEADS):
            cols = slice(hh * LANES, (hh + 1) * LANES)
            qn = halves_rms(zq[:, cols], qnw_ref[...]) * (HEAD_DIM ** -0.5) * LOG2E
            kn = halves_rms(zk[:, cols], knw_ref[...])
            for m in range(2):
                qm = qn if m == 0 else pltpu.roll(qn, HEAD_DIM, axis=1)
                km = kn if m == 0 else pltpu.roll(kn, HEAD_DIM, axis=1)
                qt_ref[hh, m, :, rows] = jnp.where(first, qm, 0.0).T.astype(BF16)
                ka_ref[hh, m, cc] = jnp.where(first, km, pos_cols).astype(BF16)
            vt_ref[hh, cc] = zv[:, cols].T.astype(BF16)


def _qkv_proj(h, w, layer, col_q, col_k, col_v, col_g, qnw, knw):
    s, d = h.shape
    tm = min(1024, s)
    nchunk = s // ATT_TK
    cpt = tm // ATT_TK
    tn = QKV_HEADS * LANES
    cols = (col_q, col_k, col_v, col_g)
    assert all(c0 % tn == 0 for c0 in cols)
    wspecs = [pl.BlockSpec((None, d, tn), functools.partial(lambda i, j, jb: (layer, 0, j + jb), jb=c0 // tn))
              for c0 in cols]
    vec = pl.BlockSpec((1, LANES), lambda i, j: (0, 0))
    return pl.pallas_call(
        functools.partial(_qkv_kernel, tm=tm),
        grid=(s // tm, HEADS // QKV_HEADS),
        in_specs=[pl.BlockSpec((tm, d), lambda i, j: (i, 0)), *wspecs, vec, vec],
        out_specs=[pl.BlockSpec((QKV_HEADS, 2, LANES, tm), lambda i, j: (j, 0, 0, i)),
                   pl.BlockSpec((QKV_HEADS, 2, cpt, ATT_TK, LANES), lambda i, j: (j, 0, i, 0, 0)),
                   pl.BlockSpec((QKV_HEADS, cpt, HEAD_V, ATT_TK), lambda i, j: (j, i, 0, 0)),
                   pl.BlockSpec((tm, tn), lambda i, j: (i, j))],
        out_shape=[jax.ShapeDtypeStruct((HEADS, 2, LANES, s), BF16),
                   jax.ShapeDtypeStruct((HEADS, 2, nchunk, ATT_TK, LANES), BF16),
                   jax.ShapeDtypeStruct((HEADS, nchunk, HEAD_V, ATT_TK), BF16),
                   jax.ShapeDtypeStruct((s, DIFF_WIDTH), F32)],
        compiler_params=_params(2),
        name="qkv_proj",
    )(h, w, w, w, w, jnp.tile(qnw, (1, 2)), jnp.tile(knw, (1, 2)))


def _attn_kernel(slopes_ref, qt_ref, ka_ref, vt_ref, lq1_ref, lk1_ref, lq2_ref, lk2_ref, subw_ref, g_ref,
                 o_ref, acc_ref, mu_ref, qv_ref, kmax_ref, p_a, p_b, den_ref, *, lambda_init, nchunk, unroll):
    hd = pl.program_id(0)
    i = pl.program_id(1)
    base = hd * (1 + COEF_PARTS)
    slope = slopes_ref[base]
    q_off = lax.broadcasted_iota(jnp.int32, (1, ATT_TQ), 1).astype(F32)

    row1 = lax.broadcasted_iota(jnp.int32, (LANES - HEAD_DIM, 1), 0)
    coef_col = jnp.zeros((LANES - HEAD_DIM, 1), F32)
    for t in range(COEF_PARTS):
        piece = slopes_ref[base + 1 + t]
        coef_col = jnp.where(row1 == t, piece, coef_col)
        coef_col = jnp.where(row1 == COEF_PARTS + t, piece * POS_RADIX, coef_col)
    coef_rows = jnp.broadcast_to(coef_col, (LANES - HEAD_DIM, ATT_TQ))
    for m in range(2):
        for kind, sign in ((0, -1.0), (1, 0.0), (2, 1.0)):
            qv_ref[kind, m, 0:HEAD_DIM, :] = qt_ref[0, m, 0:HEAD_DIM, :]
            qv_ref[kind, m, HEAD_DIM:LANES, :] = (sign * coef_rows).astype(BF16)

    @pl.when(i == 0)
    def _():
        lane = lax.broadcasted_iota(jnp.int32, (ATT_TK, LANES), 1)
        for m in range(2):
            def widest(c, best):
                kc = ka_ref[0, m, c].astype(F32)
                return jnp.maximum(best, jnp.sum(jnp.where(lane < HEAD_DIM, kc * kc, 0.0), axis=1, keepdims=True))
            kmax_ref[m] = jnp.max(lax.fori_loop(0, nchunk, widest, jnp.zeros((ATT_TK, 1), F32)))

    for m in range(2):
        qf = qt_ref[0, m, 0:HEAD_DIM, :].astype(F32)
        bound = jnp.sqrt(jnp.sum(qf * qf, axis=0, keepdims=True) * kmax_ref[m]) * BOUND_SLACK
        mu_ref[m] = bound - REF_MARGIN

    c_diag = i * ATT_DIAG

    def chunk_of(e):
        if isinstance(e, int) and e < ATT_DIAG:
            return c_diag + e
        rest = e - ATT_DIAG
        return jnp.where(e < ATT_DIAG, c_diag + e, rest + ATT_DIAG * (rest >= c_diag).astype(jnp.int32))

    def side_of(c):
        return (c < c_diag).astype(jnp.int32) - (c >= c_diag + ATT_DIAG).astype(jnp.int32)

    p_bufs = (p_a, p_b)

    def probs(e, p_ref):
        c = chunk_of(e)
        side = side_of(c)
        sidef = side.astype(F32)
        gap = jnp.abs(c * ATT_TK - i * ATT_TQ).astype(F32)
        cvec = -slope * (sidef * sidef * gap + sidef * q_off)
        for m in range(2):
            st = jnp.dot(ka_ref[0, m, c], qv_ref[1 + side, m], preferred_element_type=F32)
            if isinstance(e, int) and e < ATT_DIAG:
                key_off = lax.broadcasted_iota(jnp.int32, (ATT_TK, ATT_TQ), 0) + e * ATT_TK
                qry_off = lax.broadcasted_iota(jnp.int32, (ATT_TK, ATT_TQ), 1)
                st = st - slope * jnp.abs(key_off - qry_off).astype(F32)
            p = jnp.exp2(st - (mu_ref[m] - cvec))
            p_ref[m] = p.astype(BF16)
            den_ref[m] += jnp.sum(p.reshape(ATT_TK // SUBLANES, SUBLANES, ATT_TQ), axis=0)

    def group(e0, count, probs_last):
        sums = [None, None]
        for u in range(count):
            if u < count - 1 or probs_last:
                probs(e0 + u + 1, p_bufs[(u + 1) % 2])
            c = chunk_of(e0 + u)
            for m in range(2):
                pv = jnp.dot(vt_ref[0, c], p_bufs[u % 2][m], preferred_element_type=F32)
                sums[m] = pv if sums[m] is None else sums[m] + pv
        for m in range(2):
            acc_ref[m] += sums[m]

    def loop_body(it, carry):
        group(it * unroll, unroll, True)
        return carry

    def one_pass(_):
        acc_ref[...] = jnp.zeros(acc_ref.shape, F32)
        den_ref[...] = jnp.zeros(den_ref.shape, F32)
        n_groups = nchunk // unroll
        probs(0, p_a)
        group(0, unroll, True)
        lax.fori_loop(1, n_groups - 1, loop_body, 0)
        group((n_groups - 1) * unroll, unroll, False)
        lowest = None
        for m in range(2):
            den = jnp.sum(den_ref[m], axis=0, keepdims=True)
            mu_ref[m] = jnp.where(den < REF_FLOOR, mu_ref[m] - 2.0 * REF_MARGIN, mu_ref[m])
            lowest = jnp.min(den) if lowest is None else jnp.minimum(lowest, jnp.min(den))
        return (lowest < REF_FLOOR).astype(jnp.int32)

    lax.while_loop(lambda retry: retry > 0, one_pass, jnp.int32(1))

    lam = (jnp.exp(jnp.sum(lq1_ref[...] * lk1_ref[...], axis=-1, keepdims=True))
           - jnp.exp(jnp.sum(lq2_ref[...] * lk2_ref[...], axis=-1, keepdims=True)) + lambda_init)
    den1 = jnp.sum(den_ref[0], axis=0, keepdims=True)
    den2 = jnp.sum(den_ref[1], axis=0, keepdims=True)
    o = acc_ref[0] / den1 - lam * (acc_ref[1] / den2)
    r = lax.rsqrt(jnp.mean(o * o, axis=0, keepdims=True) + NORM_EPS)
    y = (o * r * subw_ref[...]) * (1.0 - lambda_init)
    o_ref[...] = (y.T * jax.nn.silu(g_ref[...])).astype(o_ref.dtype)


def _diff_attn(qt, ka, vt, lq1, lk1, lq2, lk2, subw, g, lambda_init):
    s = qt.shape[-1]
    g_col0 = (g.shape[1] - DIFF_WIDTH) // HEAD_V
    nchunk = s // ATT_TK
    unroll = min(ATT_UNROLL, nchunk // 2)
    assert unroll % 2 == 0 and nchunk % unroll == 0 and unroll >= ATT_DIAG and s % ATT_TQ == 0
    whole = jnp.asarray([LOG2E * 2.0 ** (-8.0 * (h + 1) / HEADS) for h in range(HEADS)], F32)
    pieces, rest = [], whole
    for _ in range(COEF_PARTS):
        pieces.append(rest.astype(BF16).astype(F32))
        rest = rest - pieces[-1]
    slopes = jnp.stack([whole] + pieces, axis=1).reshape(-1)
    vec = pl.BlockSpec((1, HEAD_DIM), lambda h, i: (0, 0))
    return pl.pallas_call(
        functools.partial(_attn_kernel, lambda_init=lambda_init, nchunk=nchunk, unroll=unroll),
        grid=(HEADS, s // ATT_TQ),
        in_specs=[pl.BlockSpec(memory_space=pltpu.SMEM),
                  pl.BlockSpec((1, 2, LANES, ATT_TQ), lambda h, i: (h, 0, 0, i)),
                  pl.BlockSpec((1, 2, nchunk, ATT_TK, LANES), lambda h, i: (h, 0, 0, 0, 0)),
                  pl.BlockSpec((1, nchunk, HEAD_V, ATT_TK), lambda h, i: (h, 0, 0, 0)),
                  vec, vec, vec, vec,
                  pl.BlockSpec((HEAD_V, 1), lambda h, i: (0, 0)),
                  pl.BlockSpec((ATT_TQ, HEAD_V), lambda h, i: (i, g_col0 + h))],
        out_specs=pl.BlockSpec((ATT_TQ, HEAD_V), lambda h, i: (i, h)),
        out_shape=jax.ShapeDtypeStruct((s, DIFF_WIDTH), BF16),
        scratch_shapes=[pltpu.VMEM((2, HEAD_V, ATT_TQ), F32), pltpu.VMEM((2, 1, ATT_TQ), F32),
                        pltpu.VMEM((3, 2, LANES, ATT_TQ), BF16), pltpu.SMEM((2,), F32),
                        pltpu.VMEM((2, ATT_TK, ATT_TQ), BF16), pltpu.VMEM((2, ATT_TK, ATT_TQ), BF16),
                        pltpu.VMEM((2, SUBLANES, ATT_TQ), F32)],
        compiler_params=_params(2),
        name="diff_attn",
    )(slopes, qt, ka, vt, lq1, lk1, lq2, lk2, subw.reshape(HEAD_V, 1), g)


def _pool_kernel(hprev_ref, h_ref, hnext_ref, wu_ref, wg_ref, w_ref, scale_ref, o_ref, ext_ref, cur_ref, gate_ref,
                 *, tm, seq):
    i = pl.program_id(0)
    last = pl.num_programs(0) - 1
    t = lax.broadcasted_iota(jnp.int32, (tm, 1), 0) + i * tm
    pair = 2 * POOL_GROUP
    for g, w in enumerate(POOL_WINDOWS):
        cols = pl.ds(g * POOL_GROUP, POOL_GROUP)
        if g % 2 == 0:
            cols2 = pl.ds(g * POOL_GROUP, pair)
            wu = wu_ref[:, cols2]
            u = jnp.dot(h_ref[...], wu, preferred_element_type=F32)
            u_before = jnp.dot(hprev_ref[...], wu, preferred_element_type=F32)[HALO_BLOCK - POOL_HALO:HALO_BLOCK]
            u_after = jnp.dot(hnext_ref[...], wu, preferred_element_type=F32)[0:POOL_HALO]
            cur_ref[:, cols2] = u
            ext_ref[pl.ds(0, POOL_HALO), cols2] = jnp.where(i > 0, u_before, 0.0)
            ext_ref[pl.ds(POOL_HALO, tm), cols2] = u
            ext_ref[pl.ds(POOL_HALO + tm, POOL_HALO), cols2] = jnp.where(i < last, u_after, 0.0)
            gate_ref[:, cols2] = jnp.dot(h_ref[...], wg_ref[:, cols2], preferred_element_type=F32)
        span = tm + 2 * POOL_HALO
        width = 1
        while width < w:
            span -= width
            ext_ref[pl.ds(0, span), cols] = ext_ref[pl.ds(0, span), cols] + ext_ref[pl.ds(width, span), cols]
            width *= 2
        assert width == w
        win = ext_ref[pl.ds(POOL_HALO - w // 2, tm), cols]
        lo = jnp.maximum(t - w // 2, 0)
        hi = jnp.minimum(t + (w - w // 2) - 1, seq - 1)
        cnt = (hi - lo + 1).astype(F32)
        pooled = win * (1.0 / cnt) - cur_ref[:, cols]
        y = jnp.dot(pooled.astype(BF16), w_ref[g], preferred_element_type=F32)
        o_ref[:, cols] = (y * scale_ref[:, cols] * jax.nn.silu(gate_ref[:, cols])).astype(o_ref.dtype)


def _pool_branch(h, w_in, layer, col_u, col_g, w_pool, pool_scale):
    s, d = h.shape
    width = POOL_WIDTH
    tm = min(1024, s)
    hb = tm // HALO_BLOCK
    nhalo = s // HALO_BLOCK
    assert col_u % width == 0 and col_g % width == 0
    once = pl.Buffered(1)
    return pl.pallas_call(
        functools.partial(_pool_kernel, tm=tm, seq=s),
        grid=(s // tm,),
        in_specs=[pl.BlockSpec((HALO_BLOCK, d), lambda i: (jnp.maximum(i * hb - 1, 0), 0)),
                  pl.BlockSpec((tm, d), lambda i: (i, 0)),
                  pl.BlockSpec((HALO_BLOCK, d), lambda i: (jnp.minimum((i + 1) * hb, nhalo - 1), 0)),
                  pl.BlockSpec((None, d, width), lambda i: (layer, 0, col_u // width), pipeline_mode=once),
                  pl.BlockSpec((None, d, width), lambda i: (layer, 0, col_g // width), pipeline_mode=once),
                  pl.BlockSpec(w_pool.shape, lambda i: (0, 0, 0)),
                  pl.BlockSpec((1, width), lambda i: (0, 0))],
        out_specs=pl.BlockSpec((tm, width), lambda i: (i, 0)),
        out_shape=jax.ShapeDtypeStruct((s, width), BF16),
        scratch_shapes=[pltpu.VMEM((tm + 2 * POOL_HALO, width), F32), pltpu.VMEM((tm, width), F32),
                        pltpu.VMEM((tm, width), F32)],
        compiler_params=_params(1),
        name="pool_branch",
    )(h, h, h, w_in, w_in, w_pool, pool_scale)


def _out_kernel(*refs, widths, tm, modulate_next, gate_proj):
    n = len(widths)
    y_refs = refs[:n]
    refs = refs[n:]
    if gate_proj:
        hin_ref, wg_ref = refs[:2]
        refs = refs[2:]
    w_ref, x_ref, gate_ref = refs[:3]
    if modulate_next:
        nw_ref, shift_ref, scale_ref, o_ref, h_ref = refs[3:]
        mul = nw_ref[...]
        one_plus_scale = 1.0 + scale_ref[...]
    else:
        (o_ref,) = refs[3:]
    sub = min(OUT_SUB, tm // 2)
    for sb in range(tm // sub):
        rows = slice(sb * sub, (sb + 1) * sub)
        acc = None
        off = 0
        for y_ref, wd in zip(y_refs, widths):
            y = y_ref[rows, :]
            if gate_proj:
                y = y * jax.nn.silu(jnp.dot(hin_ref[rows, :], wg_ref[:, off:off + wd], preferred_element_type=F32))
            part = jnp.dot(y.astype(BF16), w_ref[off:off + wd, :], preferred_element_type=F32)
            acc = part if acc is None else acc + part
            off += wd
        xn = x_ref[rows, :] + gate_ref[...] * acc
        o_ref[rows, :] = xn
        if modulate_next:
            r = lax.rsqrt(jnp.mean(xn * xn, axis=-1, keepdims=True) + NORM_EPS)
            h_ref[rows, :] = ((xn * r * mul) * one_plus_scale + shift_ref[...]).astype(BF16)


def _out_proj(ys, w, layer, x, gate, next_mod=None, gate_from=None):
    s, d = x.shape
    k = w.shape[1]
    gate_proj = gate_from is not None
    tm = min(512, s)
    once = pl.Buffered(1) if gate_proj else None
    widths = tuple(y.shape[1] for y in ys)
    vec = pl.BlockSpec((1, d), lambda i: (0, 0))
    rows = pl.BlockSpec((tm, d), lambda i: (i, 0))
    modulate_next = next_mod is not None
    gate_specs, gate_args = [], ()
    if gate_proj:
        hin, wg, col0 = gate_from
        assert col0 % k == 0 and sum(widths) == k
        gate_specs = [pl.BlockSpec((tm, hin.shape[1]), lambda i: (i, 0)),
                      pl.BlockSpec((None, wg.shape[1], k), lambda i: (layer, 0, col0 // k), pipeline_mode=once)]
        gate_args = (hin, wg)
    return pl.pallas_call(
        functools.partial(_out_kernel, widths=widths, tm=tm, modulate_next=modulate_next, gate_proj=gate_proj),
        grid=(s // tm,),
        in_specs=[pl.BlockSpec((tm, wd), lambda i: (i, 0)) for wd in widths] + gate_specs
        + [pl.BlockSpec((None, k, d), lambda i: (layer, 0, 0), pipeline_mode=once), rows, vec]
        + ([vec, vec, vec] if modulate_next else []),
        out_specs=[rows, rows] if modulate_next else rows,
        out_shape=([jax.ShapeDtypeStruct((s, d), F32), jax.ShapeDtypeStruct((s, d), BF16)] if modulate_next
                   else jax.ShapeDtypeStruct((s, d), F32)),
        compiler_params=_params(1),
        name="out_proj",
    )(*ys, *gate_args, w, x, gate, *(next_mod if modulate_next else ()))


def _dft_split(s):
    n1 = 1 << (int(math.log2(s)) // 2)
    n2 = s // n1
    assert n1 * n2 == s and n1 % SUBLANES == 0 and n2 % SUBLANES == 0
    return n1, n2


def _angles(num, den):
    ang = (2.0 * math.pi / den) * (num % den)
    return np.cos(ang), np.sin(ang)


def _fold_cs_kernel(cs_ref, wf_ref, o_ref):
    wf = wf_ref[...]
    o_ref[:, 0:FOURIER_GROUP] = jnp.dot(cs_ref[0], wf, preferred_element_type=F32,
                                        precision=lax.Precision.HIGHEST)
    o_ref[:, FOURIER_GROUP:] = jnp.dot(cs_ref[1], wf, preferred_element_type=F32,
                                       precision=lax.Precision.HIGHEST)


def _fold_channel_dft(w_fourier_j, seq):
    fg = FOURIER_GROUP
    idx = np.arange(fg)
    cc, sc = _angles(idx[:, None] * idx[None, :], fg)
    cs = jnp.asarray(np.stack([cc, sc]) / math.sqrt(seq * fg), F32)
    return pl.pallas_call(
        _fold_cs_kernel,
        grid=(FOURIER_GROUPS,),
        in_specs=[pl.BlockSpec((2, fg, fg), lambda g: (0, 0, 0)),
                  pl.BlockSpec((None, fg, fg), lambda g: (g, 0, 0))],
        out_specs=pl.BlockSpec((None, fg, 2 * fg), lambda g: (g, 0, 0)),
        out_shape=jax.ShapeDtypeStruct((FOURIER_GROUPS, fg, 2 * fg), F32),
        compiler_params=_params(1),
        name="fold_channel_dft",
    )(cs, w_fourier_j)


def _fold_win_kernel(w_ref, ab_ref, o_ref):
    o_ref[...] = jnp.dot(w_ref[...], ab_ref[...].astype(BF16), preferred_element_type=F32).astype(BF16)


def _fold_in_proj(w_in, layer, ab):
    d = w_in.shape[1]
    fg = FOURIER_GROUP
    return pl.pallas_call(
        _fold_win_kernel,
        grid=(FOURIER_GROUPS, 2),
        in_specs=[pl.BlockSpec((None, d, fg), lambda g, t: (layer, 0, g)),
                  pl.BlockSpec((None, fg, fg), lambda g, t: (g, 0, t))],
        out_specs=pl.BlockSpec((d, fg), lambda g, t: (0, g + t * FOURIER_GROUPS)),
        out_shape=jax.ShapeDtypeStruct((d, 2 * D_MODEL), BF16),
        compiler_params=_params(2),
        name="fold_in_proj",
    )(w_in, ab)


HIGH_HALF = -65536


def _bf16_bits(v):
    b = lax.bitcast_convert_type(v, jnp.int32)
    return b + 0x7FFF + (lax.shift_right_logical(b, 16) & 1)


def _dft1_kernel(h_ref, wp_ref, wq_ref, f_ref, t_ref, *, n1, per_sub, n_sub):
    f = f_ref[...]
    for sb in range(n_sub):
        rows = slice(sb * per_sub * n1, (sb + 1) * per_sub * n1)
        h = h_ref[rows, :]
        p = jnp.dot(h, wp_ref[...], preferred_element_type=F32).astype(BF16)
        q = jnp.dot(h, wq_ref[...], preferred_element_type=F32).astype(BF16)
        for jj in range(per_sub):
            r = slice(jj * n1, (jj + 1) * n1)
            t = jnp.dot(f, jnp.concatenate([p[r], q[r]], axis=0), preferred_element_type=F32)
            t_ref[sb * per_sub + jj] = ((_bf16_bits(t[0:n1]) & HIGH_HALF)
                                        | lax.shift_right_logical(_bf16_bits(t[n1:2 * n1]), 16))


def _dft_stage1(h_t, w_pq, f1, n1, n2):
    s, d = h_t.shape
    c = w_pq.shape[1] // 2
    tm = min(PROJ_TM, s)
    tn = PROJ_TN
    per_sub = max(1, 256 // n1)
    n_sub = tm // (per_sub * n1)
    ncb = c // tn
    return pl.pallas_call(
        functools.partial(_dft1_kernel, n1=n1, per_sub=per_sub, n_sub=n_sub),
        grid=(s // tm, ncb),
        in_specs=[pl.BlockSpec((tm, d), lambda i, j: (i, 0)),
                  pl.BlockSpec((d, tn), lambda i, j: (0, j)),
                  pl.BlockSpec((d, tn), lambda i, j: (0, j + ncb)),
                  pl.BlockSpec((2 * n1, 2 * n1), lambda i, j: (0, 0))],
        out_specs=pl.BlockSpec((tm // n1, n1, tn), lambda i, j: (i, 0, j)),
        out_shape=jax.ShapeDtypeStruct((n2, n1, c), jnp.int32),
        compiler_params=_params(2),
        name="fourier_in_dft1",
    )(h_t, w_pq, w_pq, f1)


def _dft2_kernel(l_ref, t_ref, o_ref, *, n2, cb):
    packed = t_ref[...]
    re = lax.bitcast_convert_type(packed & HIGH_HALF, F32)
    im = lax.bitcast_convert_type(lax.shift_left(packed, 16), F32)
    rhs = jnp.stack([re, im], axis=1).reshape(n2 * 2 * SUBLANES, cb).astype(BF16)
    o_ref[...] = jnp.dot(l_ref[0], rhs, preferred_element_type=F32).reshape(n2, SUBLANES, cb)


def _dft_stage2(t_nk, l2, n1, n2):
    c = t_nk.shape[-1]
    cb = min(DFT_CB, c)
    row_blocks = pl.BlockSpec((n2, SUBLANES, cb), lambda g, j: (0, g, j))
    out = pl.pallas_call(
        functools.partial(_dft2_kernel, n2=n2, cb=cb),
        grid=(n1 // SUBLANES, c // cb),
        in_specs=[pl.BlockSpec((1, SUBLANES * n2, SUBLANES * 2 * n2), lambda g, j: (g, 0, 0)),
                  row_blocks],
        out_specs=row_blocks,
        out_shape=jax.ShapeDtypeStruct((n2, n1, c), F32),
        compiler_params=_params(2),
        name="dft_stage2",
    )(l2, t_nk)
    return out.reshape(n1 * n2, c)


def _dft_matrices(s):
    n1, n2 = _dft_split(s)
    a = np.arange(n1)
    c1, s1 = _angles(a[:, None] * a[None, :], n1)
    f1 = jnp.asarray(np.block([[c1, -s1], [-s1, -c1]]).astype(BF16))
    groups = n1 // SUBLANES
    g = np.arange(groups)[:, None, None, None]
    k2 = np.arange(n2)[None, :, None, None]
    j = np.arange(SUBLANES)[None, None, :, None]
    nn = np.arange(n2)[None, None, None, :]
    c2, s2 = _angles((SUBLANES * g + j + n1 * k2) * nn, s)
    cs = np.concatenate([c2, s2], axis=3).astype(BF16)
    return n1, n2, f1, _expand_twiddles(jnp.asarray(cs.reshape(groups, n2 * SUBLANES, 2 * n2)))


def _expand_kernel(cs_ref, e_ref, o_ref):
    spread = jnp.dot(cs_ref[0], e_ref[...], preferred_element_type=F32)
    row = lax.broadcasted_iota(jnp.int32, spread.shape, 0)
    col = lax.broadcasted_iota(jnp.int32, spread.shape, 1)
    o_ref[0] = jnp.where(row % SUBLANES == col % SUBLANES, spread, 0.0).astype(BF16)


def _expand_twiddles(cs):
    groups, rows, cols = cs.shape
    wide = cols * SUBLANES
    src = jnp.arange(cols, dtype=jnp.int32)[:, None]
    slot = (src % (cols // 2)) * 2 + src // (cols // 2)
    e = jnp.arange(wide, dtype=jnp.int32)[None, :] // SUBLANES == slot
    return pl.pallas_call(
        _expand_kernel,
        grid=(groups,),
        in_specs=[pl.BlockSpec((1, rows, cols), lambda g: (g, 0, 0)),
                  pl.BlockSpec((cols, wide), lambda g: (0, 0))],
        out_specs=pl.BlockSpec((1, rows, wide), lambda g: (g, 0, 0)),
        out_shape=jax.ShapeDtypeStruct((groups, rows, wide), BF16),
        compiler_params=_params(1),
        name="expand_twiddles",
    )(cs, e.astype(BF16))


def kernel(x, c, norm_w, ada_w, ada_b, w_in_ab, w_pool, pool_scale, q_norm_w, k_norm_w, lambda_q1, lambda_k1,
           lambda_q2, lambda_k2, subln_w, w_out_ab, w_in_c, w_fourier, w_out_c):
    batch, s, d = x.shape
    assert batch == 1 and d == D_MODEL and s % ATT_TK == 0
    xs = x.reshape(s, d)
    mod = _ada_mod(c, ada_w, ada_b)
    n1, n2, f1, l2 = _dft_matrices(s)
    row = lambda v: v.reshape(1, -1)

    def mod_of(i):
        return mod[i, :, 0:d], mod[i, :, d:2 * d], mod[i, :, 2 * d:3 * d]

    def finish(ys, w_out, j, xs, i, gate_from=None):
        gate = mod_of(i)[2]
        if i + 1 == DEPTH:
            return _out_proj(ys, w_out, j, xs, gate, gate_from=gate_from), None
        shift, scale, _ = mod_of(i + 1)
        return _out_proj(ys, w_out, j, xs, gate, (row(norm_w[i + 1]), shift, scale), gate_from=gate_from)

    w_in_ab, w_out_ab, w_in_c, w_out_c = (w.astype(BF16) for w in (w_in_ab, w_out_ab, w_in_c, w_out_c))
    h = _modulate(xs, row(norm_w[0]), *mod_of(0)[:2])
    for i in range(DEPTH):
        j = i // 2
        if i % 2 == 0:
            o1, o2, o3, o4 = POOL_WIDTH, POOL_WIDTH + DIFF_WIDTH, POOL_WIDTH + 2 * DIFF_WIDTH, POOL_WIDTH + 3 * DIFF_WIDTH
            lambda_init = 0.8 - 0.6 * math.exp(-0.3 * i)
            y_a = _pool_branch(h, w_in_ab, j, 0, o4, w_pool[j].astype(BF16), row(pool_scale[j]))
            qt, ka, vt, g_attn = _qkv_proj(h, w_in_ab, j, o1, o2, o3, o4 + POOL_WIDTH,
                                           row(q_norm_w[j]), row(k_norm_w[j]))
            y_b = _diff_attn(qt, ka, vt, row(lambda_q1[j]), row(lambda_k1[j]), row(lambda_q2[j]),
                             row(lambda_k2[j]), subln_w[j], g_attn, lambda_init)
            xs, h = finish([y_a, y_b], w_out_ab, j, xs, i)
        else:
            ab = _fold_channel_dft(w_fourier[j], s)
            w_pq = _fold_in_proj(w_in_c, j, ab)
            h_t = h.reshape(n1, n2, d).transpose(1, 0, 2).reshape(s, d)
            t_nk = _dft_stage1(h_t, w_pq, f1, n1, n2)
            f = _dft_stage2(t_nk, l2, n1, n2)
            xs, h = finish([f], w_out_c, j, xs, i, gate_from=(h, w_in_c, d))
    return xs.reshape(batch, s, d)
```

```python
import functools
import math

import jax
import jax.numpy as jnp
import numpy as np
from jax import lax
from jax.experimental import pallas as pl
from jax.experimental.pallas import tpu as pltpu

F32 = jnp.float32
BF16 = jnp.bfloat16

D_MODEL = 2048
DEPTH = 4
NORM_EPS = 1e-6

POOL_WINDOWS = (2, 4, 8, 16)
POOL_WIDTH = D_MODEL // 2
POOL_GROUP = POOL_WIDTH // len(POOL_WINDOWS)
POOL_HALO = 8
HALO_BLOCK = 16

HEADS = 8
HEAD_DIM = 64
HEAD_V = 2 * HEAD_DIM
DIFF_WIDTH = HEADS * HEAD_V
AB_WIDTH = POOL_WIDTH + DIFF_WIDTH

FOURIER_GROUPS = 4
FOURIER_GROUP = D_MODEL // FOURIER_GROUPS

LANES = 128
SUBLANES = 8
POS_RADIX = 256
COEF_PARTS = 3
LOG2E = math.log2(math.e)

ATT_TQ = 512
ATT_TK = 512
ATT_DIAG = ATT_TQ // ATT_TK
ATT_UNROLL = 8
REF_MARGIN = 60.0
REF_FLOOR = 2.0 ** -60
BOUND_SLACK = 1.001

PROJ_TM, PROJ_TN = 2048, 1024
QKV_HEADS = 4
OUT_SUB = 256
DFT_CB = 2048

VMEM_LIMIT = 56 * 1024 * 1024


def _params(n_axes):
    return pltpu.CompilerParams(dimension_semantics=("arbitrary",) * n_axes,
                                vmem_limit_bytes=VMEM_LIMIT)


def _mod_kernel(c_ref, w_ref, b_ref, o_ref):
    c = c_ref[...]
    o_ref[...] = jnp.sum(jax.nn.silu(c) * w_ref[...], axis=0, keepdims=True) + b_ref[...]


def _ada_mod(c, ada_w, ada_b):
    depth, d, n = ada_w.shape
    tn = 1024
    return pl.pallas_call(
        _mod_kernel,
        grid=(depth, n // tn),
        in_specs=[pl.BlockSpec((d, 1), lambda i, j: (0, 0)),
                  pl.BlockSpec((None, d, tn), lambda i, j: (i, 0, j)),
                  pl.BlockSpec((None, 1, tn), lambda i, j: (i, 0, j))],
        out_specs=pl.BlockSpec((None, 1, tn), lambda i, j: (i, 0, j)),
        out_shape=jax.ShapeDtypeStruct((depth, 1, n), F32),
        compiler_params=_params(2),
        name="ada_mod",
    )(c.reshape(d, 1), ada_w, ada_b.reshape(depth, 1, n))


def _modulate_kernel(x_ref, nw_ref, shift_ref, scale_ref, o_ref):
    xf = x_ref[...]
    r = lax.rsqrt(jnp.mean(xf * xf, axis=-1, keepdims=True) + NORM_EPS)
    y = xf * r * nw_ref[...]
    y = y * (1.0 + scale_ref[...]) + shift_ref[...]
    o_ref[...] = y.astype(o_ref.dtype)


def _modulate(x, nw, shift, scale):
    s, d = x.shape
    tm = min(1024, s)
    vec = pl.BlockSpec((1, d), lambda i: (0, 0))
    return pl.pallas_call(
        _modulate_kernel,
        grid=(s // tm,),
        in_specs=[pl.BlockSpec((tm, d), lambda i: (i, 0)), vec, vec, vec],
        out_specs=pl.BlockSpec((tm, d), lambda i: (i, 0)),
        out_shape=jax.ShapeDtypeStruct((s, d), BF16),
        compiler_params=_params(1),
        name="modulate",
    )(x, nw, shift, scale)


def _mm_kernel(a_ref, w_ref, o_ref, *, precision):
    o_ref[...] = jnp.dot(a_ref[...], w_ref[...], preferred_element_type=F32,
                         precision=precision).astype(o_ref.dtype)


def _matmul(a, w, out_dtype, tm, tn, precision=None, name="matmul", cols=None, layer=None):
    m, k = a.shape
    cols = cols if cols is not None else ((0, w.shape[-1]),)
    n = sum(stop - start for start, stop in cols)
    tm, tn = min(tm, m), min(tn, n)
    assert all(start % tn == 0 and stop % tn == 0 for start, stop in cols)

    def wblock(j):
        jb, first = j, 0
        for start, stop in cols:
            jb = jnp.where(j >= first, j - first + start // tn, jb)
            first += (stop - start) // tn
        return jb

    if layer is None:
        wspec = pl.BlockSpec((k, tn), lambda i, j: (0, wblock(j)))
    else:
        wspec = pl.BlockSpec((None, k, tn), lambda i, j: (layer, 0, wblock(j)))
    return pl.pallas_call(
        functools.partial(_mm_kernel, precision=precision),
        grid=(m // tm, n // tn),
        in_specs=[pl.BlockSpec((tm, k), lambda i, j: (i, 0)), wspec],
        out_specs=pl.BlockSpec((tm, tn), lambda i, j: (i, j)),
        out_shape=jax.ShapeDtypeStruct((m, n), out_dtype),
        compiler_params=_params(2),
        name=name,
    )(a, w)


def _qkv_kernel(h_ref, wq_ref, wk_ref, wv_ref, wg_ref, qnw_ref, knw_ref, qt_ref, ka_ref, vt_ref, g_ref, *, tm):
    lane = lax.broadcasted_iota(jnp.int32, (ATT_TK, LANES), 1)
    first = lane < HEAD_DIM
    pos = lax.broadcasted_iota(jnp.int32, (ATT_TK, LANES), 0)
    pos_lo = (pos % POS_RADIX).astype(F32)
    pos_hi = (pos // POS_RADIX).astype(F32)
    pos_cols = jnp.where(lane < HEAD_DIM + COEF_PARTS, pos_lo, jnp.where(lane < HEAD_DIM + 2 * COEF_PARTS, pos_hi, 0.0))

    def halves_rms(z, w):
        sq = z * z
        ss1 = jnp.sum(jnp.where(first, sq, 0.0), axis=-1, keepdims=True)
        ss2 = jnp.sum(jnp.where(first, 0.0, sq), axis=-1, keepdims=True)
        r = jnp.where(first, lax.rsqrt(ss1 / HEAD_DIM + NORM_EPS), lax.rsqrt(ss2 / HEAD_DIM + NORM_EPS))
        return z * r * w

    for cc in range(tm // ATT_TK):
        rows = slice(cc * ATT_TK, (cc + 1) * ATT_TK)
        h = h_ref[rows, :]
        zq = jnp.dot(h, wq_ref[...], preferred_element_type=F32)
        zk = jnp.dot(h, wk_ref[...], preferred_element_type=F32)
        zv = jnp.dot(h, wv_ref[...], preferred_element_type=F32)
        g_ref[rows, :] = jnp.dot(h, wg_ref[...], preferred_element_type=F32)
        for hh in range(QKV_HEADS):
            cols = slice(hh * LANES, (hh + 1) * LANES)
            qn = halves_rms(zq[:, cols], qnw_ref[...]) * (HEAD_DIM ** -0.5) * LOG2E
            kn = halves_rms(zk[:, cols], knw_ref[...])
            for m in range(2):
                qm = qn if m == 0 else pltpu.roll(qn, HEAD_DIM, axis=1)
                km = kn if m == 0 else pltpu.roll(kn, HEAD_DIM, axis=1)
                qt_ref[hh, m, :, rows] = jnp.where(first, qm, 0.0).T.astype(BF16)
                ka_ref[hh, m, cc] = jnp.where(first, km, pos_cols).astype(BF16)
            vt_ref[hh, cc] = zv[:, cols].T.astype(BF16)


def _qkv_proj(h, w, layer, col_q, col_k, col_v, col_g, qnw, knw):
    s, d = h.shape
    tm = min(1024, s)
    nchunk = s // ATT_TK
    cpt = tm // ATT_TK
    tn = QKV_HEADS * LANES
    cols = (col_q, col_k, col_v, col_g)
    assert all(c0 % tn == 0 for c0 in cols)
    wspecs = [pl.BlockSpec((None, d, tn), functools.partial(lambda i, j, jb: (layer, 0, j + jb), jb=c0 // tn))
              for c0 in cols]
    vec = pl.BlockSpec((1, LANES), lambda i, j: (0, 0))
    return pl.pallas_call(
        functools.partial(_qkv_kernel, tm=tm),
        grid=(s // tm, HEADS // QKV_HEADS),
        in_specs=[pl.BlockSpec((tm, d), lambda i, j: (i, 0)), *wspecs, vec, vec],
        out_specs=[pl.BlockSpec((QKV_HEADS, 2, LANES, tm), lambda i, j: (j, 0, 0, i)),
                   pl.BlockSpec((QKV_HEADS, 2, cpt, ATT_TK, LANES), lambda i, j: (j, 0, i, 0, 0)),
                   pl.BlockSpec((QKV_HEADS, cpt, HEAD_V, ATT_TK), lambda i, j: (j, i, 0, 0)),
                   pl.BlockSpec((tm, tn), lambda i, j: (i, j))],
        out_shape=[jax.ShapeDtypeStruct((HEADS, 2, LANES, s), BF16),
                   jax.ShapeDtypeStruct((HEADS, 2, nchunk, ATT_TK, LANES), BF16),
                   jax.ShapeDtypeStruct((HEADS, nchunk, HEAD_V, ATT_TK), BF16),
                   jax.ShapeDtypeStruct((s, DIFF_WIDTH), F32)],
        compiler_params=_params(2),
        name="qkv_proj",
    )(h, w, w, w, w, jnp.tile(qnw, (1, 2)), jnp.tile(knw, (1, 2)))


def _attn_kernel(slopes_ref, qt_ref, ka_ref, vt_ref, lq1_ref, lk1_ref, lq2_ref, lk2_ref, subw_ref, g_ref,
                 o_ref, acc_ref, mu_ref, qv_ref, kmax_ref, p_a, p_b, den_ref, *, lambda_init, nchunk, unroll):
    hd = pl.program_id(0)
    i = pl.program_id(1)
    base = hd * (1 + COEF_PARTS)
    slope = slopes_ref[base]
    q_off = lax.broadcasted_iota(jnp.int32, (1, ATT_TQ), 1).astype(F32)

    row1 = lax.broadcasted_iota(jnp.int32, (LANES - HEAD_DIM, 1), 0)
    coef_col = jnp.zeros((LANES - HEAD_DIM, 1), F32)
    for t in range(COEF_PARTS):
        piece = slopes_ref[base + 1 + t]
        coef_col = jnp.where(row1 == t, piece, coef_col)
        coef_col = jnp.where(row1 == COEF_PARTS + t, piece * POS_RADIX, coef_col)
    coef_rows = jnp.broadcast_to(coef_col, (LANES - HEAD_DIM, ATT_TQ))
    for m in range(2):
        for kind, sign in ((0, -1.0), (1, 0.0), (2, 1.0)):
            qv_ref[kind, m, 0:HEAD_DIM, :] = qt_ref[0, m, 0:HEAD_DIM, :]
            qv_ref[kind, m, HEAD_DIM:LANES, :] = (sign * coef_rows).astype(BF16)

    @pl.when(i == 0)
    def _():
        lane = lax.broadcasted_iota(jnp.int32, (ATT_TK, LANES), 1)
        for m in range(2):
            def widest(c, best):
                kc = ka_ref[0, m, c].astype(F32)
                return jnp.maximum(best, jnp.sum(jnp.where(lane < HEAD_DIM, kc * kc, 0.0), axis=1, keepdims=True))
            kmax_ref[m] = jnp.max(lax.fori_loop(0, nchunk, widest, jnp.zeros((ATT_TK, 1), F32)))

    for m in range(2):
        qf = qt_ref[0, m, 0:HEAD_DIM, :].astype(F32)
        bound = jnp.sqrt(jnp.sum(qf * qf, axis=0, keepdims=True) * kmax_ref[m]) * BOUND_SLACK
        mu_ref[m] = bound - REF_MARGIN

    c_diag = i * ATT_DIAG

    def chunk_of(e):
        if isinstance(e, int) and e < ATT_DIAG:
            return c_diag + e
        rest = e - ATT_DIAG
        return jnp.where(e < ATT_DIAG, c_diag + e, rest + ATT_DIAG * (rest >= c_diag).astype(jnp.int32))

    def side_of(c):
        return (c < c_diag).astype(jnp.int32) - (c >= c_diag + ATT_DIAG).astype(jnp.int32)

    p_bufs = (p_a, p_b)

    def probs(e, p_ref):
        c = chunk_of(e)
        side = side_of(c)
        sidef = side.astype(F32)
        gap = jnp.abs(c * ATT_TK - i * ATT_TQ).astype(F32)
        cvec = -slope * (sidef * sidef * gap + sidef * q_off)
        for m in range(2):
            st = jnp.dot(ka_ref[0, m, c], qv_ref[1 + side, m], preferred_element_type=F32)
            if isinstance(e, int) and e < ATT_DIAG:
                key_off = lax.broadcasted_iota(jnp.int32, (ATT_TK, ATT_TQ), 0) + e * ATT_TK
                qry_off = lax.broadcasted_iota(jnp.int32, (ATT_TK, ATT_TQ), 1)
                st = st - slope * jnp.abs(key_off - qry_off).astype(F32)
            p = jnp.exp2(st - (mu_ref[m] - cvec))
            p_ref[m] = p.astype(BF16)
            den_ref[m] += jnp.sum(p.reshape(ATT_TK // SUBLANES, SUBLANES, ATT_TQ), axis=0)

    def group(e0, count, probs_last):
        sums = [None, None]
        for u in range(count):
            if u < count - 1 or probs_last:
                probs(e0 + u + 1, p_bufs[(u + 1) % 2])
            c = chunk_of(e0 + u)
            for m in range(2):
                pv = jnp.dot(vt_ref[0, c], p_bufs[u % 2][m], preferred_element_type=F32)
                sums[m] = pv if sums[m] is None else sums[m] + pv
        for m in range(2):
            acc_ref[m] += sums[m]

    def loop_body(it, carry):
        group(it * unroll, unroll, True)
        return carry

    def one_pass(_):
        acc_ref[...] = jnp.zeros(acc_ref.shape, F32)
        den_ref[...] = jnp.zeros(den_ref.shape, F32)
        n_groups = nchunk // unroll
        probs(0, p_a)
        group(0, unroll, True)
        lax.fori_loop(1, n_groups - 1, loop_body, 0)
        group((n_groups - 1) * unroll, unroll, False)
        lowest = None
        for m in range(2):
            den = jnp.sum(den_ref[m], axis=0, keepdims=True)
            mu_ref[m] = jnp.where(den < REF_FLOOR, mu_ref[m] - 2.0 * REF_MARGIN, mu_ref[m])
            lowest = jnp.min(den) if lowest is None else jnp.minimum(lowest, jnp.min(den))
        return (lowest < REF_FLOOR).astype(jnp.int32)

    lax.while_loop(lambda retry: retry > 0, one_pass, jnp.int32(1))

    lam = (jnp.exp(jnp.sum(lq1_ref[...] * lk1_ref[...], axis=-1, keepdims=True))
           - jnp.exp(jnp.sum(lq2_ref[...] * lk2_ref[...], axis=-1, keepdims=True)) + lambda_init)
    den1 = jnp.sum(den_ref[0], axis=0, keepdims=True)
    den2 = jnp.sum(den_ref[1], axis=0, keepdims=True)
    o = acc_ref[0] / den1 - lam * (acc_ref[1] / den2)
    r = lax.rsqrt(jnp.mean(o * o, axis=0, keepdims=True) + NORM_EPS)
    y = (o * r * subw_ref[...]) * (1.0 - lambda_init)
    o_ref[...] = (y.T * jax.nn.silu(g_ref[...])).astype(o_ref.dtype)


def _diff_attn(qt, ka, vt, lq1, lk1, lq2, lk2, subw, g, lambda_init):
    s = qt.shape[-1]
    g_col0 = (g.shape[1] - DIFF_WIDTH) // HEAD_V
    nchunk = s // ATT_TK
    unroll = min(ATT_UNROLL, nchunk // 2)
    assert unroll % 2 == 0 and nchunk % unroll == 0 and unroll >= ATT_DIAG and s % ATT_TQ == 0
    whole = jnp.asarray([LOG2E * 2.0 ** (-8.0 * (h + 1) / HEADS) for h in range(HEADS)], F32)
    pieces, rest = [], whole
    for _ in range(COEF_PARTS):
        pieces.append(rest.astype(BF16).astype(F32))
        rest = rest - pieces[-1]
    slopes = jnp.stack([whole] + pieces, axis=1).reshape(-1)
    vec = pl.BlockSpec((1, HEAD_DIM), lambda h, i: (0, 0))
    return pl.pallas_call(
        functools.partial(_attn_kernel, lambda_init=lambda_init, nchunk=nchunk, unroll=unroll),
        grid=(HEADS, s // ATT_TQ),
        in_specs=[pl.BlockSpec(memory_space=pltpu.SMEM),
                  pl.BlockSpec((1, 2, LANES, ATT_TQ), lambda h, i: (h, 0, 0, i)),
                  pl.BlockSpec((1, 2, nchunk, ATT_TK, LANES), lambda h, i: (h, 0, 0, 0, 0)),
                  pl.BlockSpec((1, nchunk, HEAD_V, ATT_TK), lambda h, i: (h, 0, 0, 0)),
                  vec, vec, vec, vec,
                  pl.BlockSpec((HEAD_V, 1), lambda h, i: (0, 0)),
                  pl.BlockSpec((ATT_TQ, HEAD_V), lambda h, i: (i, g_col0 + h))],
        out_specs=pl.BlockSpec((ATT_TQ, HEAD_V), lambda h, i: (i, h)),
        out_shape=jax.ShapeDtypeStruct((s, DIFF_WIDTH), BF16),
        scratch_shapes=[pltpu.VMEM((2, HEAD_V, ATT_TQ), F32), pltpu.VMEM((2, 1, ATT_TQ), F32),
                        pltpu.VMEM((3, 2, LANES, ATT_TQ), BF16), pltpu.SMEM((2,), F32),
                        pltpu.VMEM((2, ATT_TK, ATT_TQ), BF16), pltpu.VMEM((2, ATT_TK, ATT_TQ), BF16),
                        pltpu.VMEM((2, SUBLANES, ATT_TQ), F32)],
        compiler_params=_params(2),
        name="diff_attn",
    )(slopes, qt, ka, vt, lq1, lk1, lq2, lk2, subw.reshape(HEAD_V, 1), g)


def _pool_kernel(hprev_ref, h_ref, hnext_ref, wu_ref, wg_ref, w_ref, scale_ref, o_ref, ext_ref, cur_ref, gate_ref,
                 hx_ref, *, tm, seq):
    i = pl.program_id(0)
    last = pl.num_programs(0) - 1
    t = lax.broadcasted_iota(jnp.int32, (tm, 1), 0) + i * tm
    pair = 2 * POOL_GROUP

    hx_ref[0:HALO_BLOCK] = hprev_ref[...]
    hx_ref[HALO_BLOCK:HALO_BLOCK + tm] = h_ref[...]
    hx_ref[HALO_BLOCK + tm:2 * HALO_BLOCK + tm] = hnext_ref[...]
    first_u = HALO_BLOCK - POOL_HALO

    def project_u(p):
        cols2 = pl.ds(p * pair, pair)
        u = jnp.dot(hx_ref[...], wu_ref[:, cols2], preferred_element_type=F32)
        cur_ref[:, cols2] = u[HALO_BLOCK:HALO_BLOCK + tm]
        ext_ref[pl.ds(0, POOL_HALO), cols2] = jnp.where(i > 0, u[first_u:HALO_BLOCK], 0.0)
        ext_ref[pl.ds(POOL_HALO, tm), cols2] = u[HALO_BLOCK:HALO_BLOCK + tm]
        ext_ref[pl.ds(POOL_HALO + tm, POOL_HALO), cols2] = jnp.where(
            i < last, u[HALO_BLOCK + tm:HALO_BLOCK + tm + POOL_HALO], 0.0)

    def project_gate(p):
        cols2 = pl.ds(p * pair, pair)
        gate_ref[:, cols2] = jnp.dot(h_ref[...], wg_ref[:, cols2], preferred_element_type=F32)

    project_u(0)
    project_u(1)
    project_gate(0)
    for g, w in enumerate(POOL_WINDOWS):
        cols = pl.ds(g * POOL_GROUP, POOL_GROUP)
        if g == 2:
            project_gate(1)
        span = tm + 2 * POOL_HALO
        width = 1
        while width < w:
            span -= width
            ext_ref[pl.ds(0, span), cols] = ext_ref[pl.ds(0, span), cols] + ext_ref[pl.ds(width, span), cols]
            width *= 2
        assert width == w
        win = ext_ref[pl.ds(POOL_HALO - w // 2, tm), cols]
        lo = jnp.maximum(t - w // 2, 0)
        hi = jnp.minimum(t + (w - w // 2) - 1, seq - 1)
        cnt = (hi - lo + 1).astype(F32)
        pooled = win * (1.0 / cnt) - cur_ref[:, cols]
        y = jnp.dot(pooled.astype(BF16), w_ref[g], preferred_element_type=F32)
        o_ref[:, cols] = (y * scale_ref[:, cols] * jax.nn.silu(gate_ref[:, cols])).astype(o_ref.dtype)


def _pool_branch(h, w_in, layer, col_u, col_g, w_pool, pool_scale):
    s, d = h.shape
    width = POOL_WIDTH
    tm = min(1024, s)
    hb = tm // HALO_BLOCK
    nhalo = s // HALO_BLOCK
    assert col_u % width == 0 and col_g % width == 0
    once = pl.Buffered(1)
    return pl.pallas_call(
        functools.partial(_pool_kernel, tm=tm, seq=s),
        grid=(s // tm,),
        in_specs=[pl.BlockSpec((HALO_BLOCK, d), lambda i: (jnp.maximum(i * hb - 1, 0), 0)),
                  pl.BlockSpec((tm, d), lambda i: (i, 0)),
                  pl.BlockSpec((HALO_BLOCK, d), lambda i: (jnp.minimum((i + 1) * hb, nhalo - 1), 0)),
                  pl.BlockSpec((None, d, width), lambda i: (layer, 0, col_u // width), pipeline_mode=once),
                  pl.BlockSpec((None, d, width), lambda i: (layer, 0, col_g // width), pipeline_mode=once),
                  pl.BlockSpec(w_pool.shape, lambda i: (0, 0, 0)),
                  pl.BlockSpec((1, width), lambda i: (0, 0))],
        out_specs=pl.BlockSpec((tm, width), lambda i: (i, 0)),
        out_shape=jax.ShapeDtypeStruct((s, width), BF16),
        scratch_shapes=[pltpu.VMEM((tm + 2 * POOL_HALO, width), F32), pltpu.VMEM((tm, width), F32),
                        pltpu.VMEM((tm, width), F32), pltpu.VMEM((tm + 2 * HALO_BLOCK, d), BF16)],
        compiler_params=_params(1),
        name="pool_branch",
    )(h, h, h, w_in, w_in, w_pool, pool_scale)


def _out_kernel(*refs, widths, tm, modulate_next, gate_proj):
    n = len(widths)
    y_refs = refs[:n]
    refs = refs[n:]
    if gate_proj:
        hin_ref, wg_ref = refs[:2]
        refs = refs[2:]
    w_ref, x_ref, gate_ref = refs[:3]
    if modulate_next:
        nw_ref, shift_ref, scale_ref, o_ref, h_ref = refs[3:]
        mul = nw_ref[...]
        one_plus_scale = 1.0 + scale_ref[...]
    else:
        (o_ref,) = refs[3:]
    sub = min(OUT_SUB, tm // 2)
    for sb in range(tm // sub):
        rows = slice(sb * sub, (sb + 1) * sub)
        acc = None
        off = 0
        for y_ref, wd in zip(y_refs, widths):
            y = y_ref[rows, :]
            if gate_proj:
                y = y * jax.nn.silu(jnp.dot(hin_ref[rows, :], wg_ref[:, off:off + wd], preferred_element_type=F32))
            part = jnp.dot(y.astype(BF16), w_ref[off:off + wd, :], preferred_element_type=F32)
            acc = part if acc is None else acc + part
            off += wd
        xn = x_ref[rows, :] + gate_ref[...] * acc
        o_ref[rows, :] = xn
        if modulate_next:
            r = lax.rsqrt(jnp.mean(xn * xn, axis=-1, keepdims=True) + NORM_EPS)
            h_ref[rows, :] = ((xn * r * mul) * one_plus_scale + shift_ref[...]).astype(BF16)


def _out_proj(ys, w, layer, x, gate, next_mod=None, gate_from=None):
    s, d = x.shape
    k = w.shape[1]
    gate_proj = gate_from is not None
    tm = min(512, s)
    once = pl.Buffered(1) if gate_proj else None
    widths = tuple(y.shape[1] for y in ys)
    vec = pl.BlockSpec((1, d), lambda i: (0, 0))
    rows = pl.BlockSpec((tm, d), lambda i: (i, 0))
    modulate_next = next_mod is not None
    gate_specs, gate_args = [], ()
    if gate_proj:
        hin, wg, col0 = gate_from
        assert col0 % k == 0 and sum(widths) == k
        gate_specs = [pl.BlockSpec((tm, hin.shape[1]), lambda i: (i, 0)),
                      pl.BlockSpec((None, wg.shape[1], k), lambda i: (layer, 0, col0 // k), pipeline_mode=once)]
        gate_args = (hin, wg)
    return pl.pallas_call(
        functools.partial(_out_kernel, widths=widths, tm=tm, modulate_next=modulate_next, gate_proj=gate_proj),
        grid=(s // tm,),
        in_specs=[pl.BlockSpec((tm, wd), lambda i: (i, 0)) for wd in widths] + gate_specs
        + [pl.BlockSpec((None, k, d), lambda i: (layer, 0, 0), pipeline_mode=once), rows, vec]
        + ([vec, vec, vec] if modulate_next else []),
        out_specs=[rows, rows] if modulate_next else rows,
        out_shape=([jax.ShapeDtypeStruct((s, d), F32), jax.ShapeDtypeStruct((s, d), BF16)] if modulate_next
                   else jax.ShapeDtypeStruct((s, d), F32)),
        compiler_params=_params(1),
        name="out_proj",
    )(*ys, *gate_args, w, x, gate, *(next_mod if modulate_next else ()))


def _dft_split(s):
    n1 = 1 << (int(math.log2(s)) // 2)
    n2 = s // n1
    assert n1 * n2 == s and n1 % SUBLANES == 0 and n2 % SUBLANES == 0
    return n1, n2


def _angles(num, den):
    ang = (2.0 * math.pi / den) * (num % den)
    return np.cos(ang), np.sin(ang)


def _fold_cs_kernel(cs_ref, wf_ref, o_ref):
    wf = wf_ref[...]
    o_ref[:, 0:FOURIER_GROUP] = jnp.dot(cs_ref[0], wf, preferred_element_type=F32,
                                        precision=lax.Precision.HIGHEST)
    o_ref[:, FOURIER_GROUP:] = jnp.dot(cs_ref[1], wf, preferred_element_type=F32,
                                       precision=lax.Precision.HIGHEST)


def _fold_channel_dft(w_fourier_j, seq):
    fg = FOURIER_GROUP
    idx = np.arange(fg)
    cc, sc = _angles(idx[:, None] * idx[None, :], fg)
    cs = jnp.asarray(np.stack([cc, sc]) / math.sqrt(seq * fg), F32)
    return pl.pallas_call(
        _fold_cs_kernel,
        grid=(FOURIER_GROUPS,),
        in_specs=[pl.BlockSpec((2, fg, fg), lambda g: (0, 0, 0)),
                  pl.BlockSpec((None, fg, fg), lambda g: (g, 0, 0))],
        out_specs=pl.BlockSpec((None, fg, 2 * fg), lambda g: (g, 0, 0)),
        out_shape=jax.ShapeDtypeStruct((FOURIER_GROUPS, fg, 2 * fg), F32),
        compiler_params=_params(1),
        name="fold_channel_dft",
    )(cs, w_fourier_j)


def _fold_win_kernel(w_ref, ab_ref, o_ref):
    o_ref[...] = jnp.dot(w_ref[...], ab_ref[...].astype(BF16), preferred_element_type=F32).astype(BF16)


def _fold_in_proj(w_in, layer, ab):
    d = w_in.shape[1]
    fg = FOURIER_GROUP
    return pl.pallas_call(
        _fold_win_kernel,
        grid=(FOURIER_GROUPS, 2),
        in_specs=[pl.BlockSpec((None, d, fg), lambda g, t: (layer, 0, g)),
                  pl.BlockSpec((None, fg, fg), lambda g, t: (g, 0, t))],
        out_specs=pl.BlockSpec((d, fg), lambda g, t: (0, g + t * FOURIER_GROUPS)),
        out_shape=jax.ShapeDtypeStruct((d, 2 * D_MODEL), BF16),
        compiler_params=_params(2),
        name="fold_in_proj",
    )(w_in, ab)


HIGH_HALF = -65536


def _bf16_bits(v):
    b = lax.bitcast_convert_type(v, jnp.int32)
    return b + 0x7FFF + (lax.shift_right_logical(b, 16) & 1)


def _dft1_kernel(h_ref, wp_ref, wq_ref, f_ref, t_ref, *, n1, per_sub, n_sub):
    f = f_ref[...]
    for sb in range(n_sub):
        rows = slice(sb * per_sub * n1, (sb + 1) * per_sub * n1)
        h = h_ref[rows, :]
        p = jnp.dot(h, wp_ref[...], preferred_element_type=F32).astype(BF16)
        q = jnp.dot(h, wq_ref[...], preferred_element_type=F32).astype(BF16)
        for jj in range(per_sub):
            r = slice(jj * n1, (jj + 1) * n1)
            t = jnp.dot(f, jnp.concatenate([p[r], q[r]], axis=0), preferred_element_type=F32)
            t_ref[sb * per_sub + jj] = ((_bf16_bits(t[0:n1]) & HIGH_HALF)
                                        | lax.shift_right_logical(_bf16_bits(t[n1:2 * n1]), 16))


def _dft_stage1(h_t, w_pq, f1, n1, n2):
    s, d = h_t.shape
    c = w_pq.shape[1] // 2
    tm = min(PROJ_TM, s)
    tn = PROJ_TN
    per_sub = max(1, 256 // n1)
    n_sub = tm // (per_sub * n1)
    ncb = c // tn
    return pl.pallas_call(
        functools.partial(_dft1_kernel, n1=n1, per_sub=per_sub, n_sub=n_sub),
        grid=(s // tm, ncb),
        in_specs=[pl.BlockSpec((tm, d), lambda i, j: (i, 0)),
                  pl.BlockSpec((d, tn), lambda i, j: (0, j)),
                  pl.BlockSpec((d, tn), lambda i, j: (0, j + ncb)),
                  pl.BlockSpec((2 * n1, 2 * n1), lambda i, j: (0, 0))],
        out_specs=pl.BlockSpec((tm // n1, n1, tn), lambda i, j: (i, 0, j)),
        out_shape=jax.ShapeDtypeStruct((n2, n1, c), jnp.int32),
        compiler_params=_params(2),
        name="fourier_in_dft1",
    )(h_t, w_pq, w_pq, f1)


def _dft2_kernel(l_ref, t_ref, o_ref, *, n2, cb):
    packed = t_ref[...]
    re = lax.bitcast_convert_type(packed & HIGH_HALF, F32)
    im = lax.bitcast_convert_type(lax.shift_left(packed, 16), F32)
    rhs = jnp.stack([re, im], axis=1).reshape(n2 * 2 * SUBLANES, cb).astype(BF16)
    o_ref[...] = jnp.dot(l_ref[0], rhs, preferred_element_type=F32).reshape(n2, SUBLANES, cb)


def _dft_stage2(t_nk, l2, n1, n2):
    c = t_nk.shape[-1]
    cb = min(DFT_CB, c)
    row_blocks = pl.BlockSpec((n2, SUBLANES, cb), lambda g, j: (0, g, j))
    out = pl.pallas_call(
        functools.partial(_dft2_kernel, n2=n2, cb=cb),
        grid=(n1 // SUBLANES, c // cb),
        in_specs=[pl.BlockSpec((1, SUBLANES * n2, SUBLANES * 2 * n2), lambda g, j: (g, 0, 0)),
                  row_blocks],
        out_specs=row_blocks,
        out_shape=jax.ShapeDtypeStruct((n2, n1, c), F32),
        compiler_params=_params(2),
        name="dft_stage2",
    )(l2, t_nk)
    return out.reshape(n1 * n2, c)


def _dft_matrices(s):
    n1, n2 = _dft_split(s)
    a = np.arange(n1)
    c1, s1 = _angles(a[:, None] * a[None, :], n1)
    f1 = jnp.asarray(np.block([[c1, -s1], [-s1, -c1]]).astype(BF16))
    groups = n1 // SUBLANES
    g = np.arange(groups)[:, None, None, None]
    k2 = np.arange(n2)[None, :, None, None]
    j = np.arange(SUBLANES)[None, None, :, None]
    nn = np.arange(n2)[None, None, None, :]
    c2, s2 = _angles((SUBLANES * g + j + n1 * k2) * nn, s)
    cs = np.concatenate([c2, s2], axis=3).astype(BF16)
    return n1, n2, f1, _expand_twiddles(jnp.asarray(cs.reshape(groups, n2 * SUBLANES, 2 * n2)))


def _expand_kernel(cs_ref, e_ref, o_ref):
    spread = jnp.dot(cs_ref[0], e_ref[...], preferred_element_type=F32)
    row = lax.broadcasted_iota(jnp.int32, spread.shape, 0)
    col = lax.broadcasted_iota(jnp.int32, spread.shape, 1)
    o_ref[0] = jnp.where(row % SUBLANES == col % SUBLANES, spread, 0.0).astype(BF16)


def _expand_twiddles(cs):
    groups, rows, cols = cs.shape
    wide = cols * SUBLANES
    src = jnp.arange(cols, dtype=jnp.int32)[:, None]
    slot = (src % (cols // 2)) * 2 + src // (cols // 2)
    e = jnp.arange(wide, dtype=jnp.int32)[None, :] // SUBLANES == slot
    return pl.pallas_call(
        _expand_kernel,
        grid=(groups,),
        in_specs=[pl.BlockSpec((1, rows, cols), lambda g: (g, 0, 0)),
                  pl.BlockSpec((cols, wide), lambda g: (0, 0))],
        out_specs=pl.BlockSpec((1, rows, wide), lambda g: (g, 0, 0)),
        out_shape=jax.ShapeDtypeStruct((groups, rows, wide), BF16),
        compiler_params=_params(1),
        name="expand_twiddles",
    )(cs, e.astype(BF16))


def kernel(x, c, norm_w, ada_w, ada_b, w_in_ab, w_pool, pool_scale, q_norm_w, k_norm_w, lambda_q1, lambda_k1,
           lambda_q2, lambda_k2, subln_w, w_out_ab, w_in_c, w_fourier, w_out_c):
    batch, s, d = x.shape
    assert batch == 1 and d == D_MODEL and s % ATT_TK == 0
    xs = x.reshape(s, d)
    mod = _ada_mod(c, ada_w, ada_b)
    n1, n2, f1, l2 = _dft_matrices(s)
    row = lambda v: v.reshape(1, -1)

    def mod_of(i):
        return mod[i, :, 0:d], mod[i, :, d:2 * d], mod[i, :, 2 * d:3 * d]

    def finish(ys, w_out, j, xs, i, gate_from=None):
        gate = mod_of(i)[2]
        if i + 1 == DEPTH:
            return _out_proj(ys, w_out, j, xs, gate, gate_from=gate_from), None
        shift, scale, _ = mod_of(i + 1)
        return _out_proj(ys, w_out, j, xs, gate, (row(norm_w[i + 1]), shift, scale), gate_from=gate_from)

    w_in_ab, w_out_ab, w_in_c, w_out_c = (w.astype(BF16) for w in (w_in_ab, w_out_ab, w_in_c, w_out_c))
    h = _modulate(xs, row(norm_w[0]), *mod_of(0)[:2])
    for i in range(DEPTH):
        j = i // 2
        if i % 2 == 0:
            o1, o2, o3, o4 = POOL_WIDTH, POOL_WIDTH + DIFF_WIDTH, POOL_WIDTH + 2 * DIFF_WIDTH, POOL_WIDTH + 3 * DIFF_WIDTH
            lambda_init = 0.8 - 0.6 * math.exp(-0.3 * i)
            y_a = _pool_branch(h, w_in_ab, j, 0, o4, w_pool[j].astype(BF16), row(pool_scale[j]))
            qt, ka, vt, g_attn = _qkv_proj(h, w_in_ab, j, o1, o2, o3, o4 + POOL_WIDTH,
                                           row(q_norm_w[j]), row(k_norm_w[j]))
            y_b = _diff_attn(qt, ka, vt, row(lambda_q1[j]), row(lambda_k1[j]), row(lambda_q2[j]),
                             row(lambda_k2[j]), subln_w[j], g_attn, lambda_init)
            xs, h = finish([y_a, y_b], w_out_ab, j, xs, i)
        else:
            ab = _fold_channel_dft(w_fourier[j], s)
            w_pq = _fold_in_proj(w_in_c, j, ab)
            h_t = h.reshape(n1, n2, d).transpose(1, 0, 2).reshape(s, d)
            t_nk = _dft_stage1(h_t, w_pq, f1, n1, n2)
            f = _dft_stage2(t_nk, l2, n1, n2)
            xs, h = finish([f], w_out_c, j, xs, i, gate_from=(h, w_in_c, d))
    return xs.reshape(batch, s, d)
```

```python
import functools
import math

import jax
import jax.numpy as jnp
import numpy as np
from jax import lax
from jax.experimental import pallas as pl
from jax.experimental.pallas import tpu as pltpu

F32 = jnp.float32
BF16 = jnp.bfloat16

D_MODEL = 2048
DEPTH = 4
NORM_EPS = 1e-6

POOL_WINDOWS = (2, 4, 8, 16)
POOL_WIDTH = D_MODEL // 2
POOL_GROUP = POOL_WIDTH // len(POOL_WINDOWS)
POOL_HALO = 8
HALO_BLOCK = 16

HEADS = 8
HEAD_DIM = 64
HEAD_V = 2 * HEAD_DIM
DIFF_WIDTH = HEADS * HEAD_V
AB_WIDTH = POOL_WIDTH + DIFF_WIDTH

FOURIER_GROUPS = 4
FOURIER_GROUP = D_MODEL // FOURIER_GROUPS

LANES = 128
SUBLANES = 8
POS_RADIX = 256
COEF_PARTS = 3
LOG2E = math.log2(math.e)

ATT_TQ = 512
ATT_TK = 512
ATT_DIAG = ATT_TQ // ATT_TK
ATT_UNROLL = 8
REF_MARGIN = 60.0
REF_FLOOR = 2.0 ** -60
BOUND_SLACK = 1.001

PROJ_TM, PROJ_TN = 2048, 1024
QKV_HEADS = 8
OUT_SUB = 256
DFT_CB = 2048

VMEM_LIMIT = 56 * 1024 * 1024


def _params(n_axes):
    return pltpu.CompilerParams(dimension_semantics=("arbitrary",) * n_axes,
                                vmem_limit_bytes=VMEM_LIMIT)


def _mod_kernel(c_ref, w_ref, b_ref, o_ref):
    c = c_ref[...]
    o_ref[...] = jnp.sum(jax.nn.silu(c) * w_ref[...], axis=0, keepdims=True) + b_ref[...]


def _ada_mod(c, ada_w, ada_b):
    depth, d, n = ada_w.shape
    tn = 1024
    return pl.pallas_call(
        _mod_kernel,
        grid=(depth, n // tn),
        in_specs=[pl.BlockSpec((d, 1), lambda i, j: (0, 0)),
                  pl.BlockSpec((None, d, tn), lambda i, j: (i, 0, j)),
                  pl.BlockSpec((None, 1, tn), lambda i, j: (i, 0, j))],
        out_specs=pl.BlockSpec((None, 1, tn), lambda i, j: (i, 0, j)),
        out_shape=jax.ShapeDtypeStruct((depth, 1, n), F32),
        compiler_params=_params(2),
        name="ada_mod",
    )(c.reshape(d, 1), ada_w, ada_b.reshape(depth, 1, n))


def _modulate_kernel(x_ref, nw_ref, shift_ref, scale_ref, o_ref):
    xf = x_ref[...]
    r = lax.rsqrt(jnp.mean(xf * xf, axis=-1, keepdims=True) + NORM_EPS)
    y = xf * r * nw_ref[...]
    y = y * (1.0 + scale_ref[...]) + shift_ref[...]
    o_ref[...] = y.astype(o_ref.dtype)


def _modulate(x, nw, shift, scale):
    s, d = x.shape
    tm = min(1024, s)
    vec = pl.BlockSpec((1, d), lambda i: (0, 0))
    return pl.pallas_call(
        _modulate_kernel,
        grid=(s // tm,),
        in_specs=[pl.BlockSpec((tm, d), lambda i: (i, 0)), vec, vec, vec],
        out_specs=pl.BlockSpec((tm, d), lambda i: (i, 0)),
        out_shape=jax.ShapeDtypeStruct((s, d), BF16),
        compiler_params=_params(1),
        name="modulate",
    )(x, nw, shift, scale)


def _mm_kernel(a_ref, w_ref, o_ref, *, precision):
    o_ref[...] = jnp.dot(a_ref[...], w_ref[...], preferred_element_type=F32,
                         precision=precision).astype(o_ref.dtype)


def _matmul(a, w, out_dtype, tm, tn, precision=None, name="matmul", cols=None, layer=None):
    m, k = a.shape
    cols = cols if cols is not None else ((0, w.shape[-1]),)
    n = sum(stop - start for start, stop in cols)
    tm, tn = min(tm, m), min(tn, n)
    assert all(start % tn == 0 and stop % tn == 0 for start, stop in cols)

    def wblock(j):
        jb, first = j, 0
        for start, stop in cols:
            jb = jnp.where(j >= first, j - first + start // tn, jb)
            first += (stop - start) // tn
        return jb

    if layer is None:
        wspec = pl.BlockSpec((k, tn), lambda i, j: (0, wblock(j)))
    else:
        wspec = pl.BlockSpec((None, k, tn), lambda i, j: (layer, 0, wblock(j)))
    return pl.pallas_call(
        functools.partial(_mm_kernel, precision=precision),
        grid=(m // tm, n // tn),
        in_specs=[pl.BlockSpec((tm, k), lambda i, j: (i, 0)), wspec],
        out_specs=pl.BlockSpec((tm, tn), lambda i, j: (i, j)),
        out_shape=jax.ShapeDtypeStruct((m, n), out_dtype),
        compiler_params=_params(2),
        name=name,
    )(a, w)


def _qkv_kernel(h_ref, wq_ref, wk_ref, wv_ref, wg_ref, qnw_ref, knw_ref, qt_ref, ka_ref, vt_ref, g_ref, *, tm):
    lane = lax.broadcasted_iota(jnp.int32, (ATT_TK, LANES), 1)
    first = lane < HEAD_DIM
    pos = lax.broadcasted_iota(jnp.int32, (ATT_TK, LANES), 0)
    pos_lo = (pos % POS_RADIX).astype(F32)
    pos_hi = (pos // POS_RADIX).astype(F32)
    pos_cols = jnp.where(lane < HEAD_DIM + COEF_PARTS, pos_lo, jnp.where(lane < HEAD_DIM + 2 * COEF_PARTS, pos_hi, 0.0))

    def halves_rms(z, w):
        sq = z * z
        ss1 = jnp.sum(jnp.where(first, sq, 0.0), axis=-1, keepdims=True)
        ss2 = jnp.sum(jnp.where(first, 0.0, sq), axis=-1, keepdims=True)
        r = jnp.where(first, lax.rsqrt(ss1 / HEAD_DIM + NORM_EPS), lax.rsqrt(ss2 / HEAD_DIM + NORM_EPS))
        return z * r * w

    for cc in range(tm // ATT_TK):
        rows = slice(cc * ATT_TK, (cc + 1) * ATT_TK)
        h = h_ref[rows, :]
        zq = jnp.dot(h, wq_ref[...], preferred_element_type=F32)
        zk = jnp.dot(h, wk_ref[...], preferred_element_type=F32)
        zv = jnp.dot(h, wv_ref[...], preferred_element_type=F32)
        g_ref[rows, :] = jnp.dot(h, wg_ref[...], preferred_element_type=F32)
        for hh in range(QKV_HEADS):
            cols = slice(hh * LANES, (hh + 1) * LANES)
            qn = halves_rms(zq[:, cols], qnw_ref[...]) * (HEAD_DIM ** -0.5) * LOG2E
            kn = halves_rms(zk[:, cols], knw_ref[...])
            for m in range(2):
                qm = qn if m == 0 else pltpu.roll(qn, HEAD_DIM, axis=1)
                km = kn if m == 0 else pltpu.roll(kn, HEAD_DIM, axis=1)
                qt_ref[hh, m, :, rows] = jnp.where(first, qm, 0.0).T.astype(BF16)
                ka_ref[hh, m, cc] = jnp.where(first, km, pos_cols).astype(BF16)
            vt_ref[hh, cc] = zv[:, cols].T.astype(BF16)


def _qkv_proj(h, w, layer, col_q, col_k, col_v, col_g, qnw, knw):
    s, d = h.shape
    tm = min(512, s)
    nchunk = s // ATT_TK
    cpt = tm // ATT_TK
    tn = QKV_HEADS * LANES
    cols = (col_q, col_k, col_v, col_g)
    assert all(c0 % tn == 0 for c0 in cols)
    once = pl.Buffered(1) if QKV_HEADS == HEADS else None
    wspecs = [pl.BlockSpec((None, d, tn), functools.partial(lambda i, j, jb: (layer, 0, j + jb), jb=c0 // tn),
                           pipeline_mode=once)
              for c0 in cols]
    vec = pl.BlockSpec((1, LANES), lambda i, j: (0, 0))
    return pl.pallas_call(
        functools.partial(_qkv_kernel, tm=tm),
        grid=(s // tm, HEADS // QKV_HEADS),
        in_specs=[pl.BlockSpec((tm, d), lambda i, j: (i, 0)), *wspecs, vec, vec],
        out_specs=[pl.BlockSpec((QKV_HEADS, 2, LANES, tm), lambda i, j: (j, 0, 0, i)),
                   pl.BlockSpec((QKV_HEADS, 2, cpt, ATT_TK, LANES), lambda i, j: (j, 0, i, 0, 0)),
                   pl.BlockSpec((QKV_HEADS, cpt, HEAD_V, ATT_TK), lambda i, j: (j, i, 0, 0)),
                   pl.BlockSpec((tm, tn), lambda i, j: (i, j))],
        out_shape=[jax.ShapeDtypeStruct((HEADS, 2, LANES, s), BF16),
                   jax.ShapeDtypeStruct((HEADS, 2, nchunk, ATT_TK, LANES), BF16),
                   jax.ShapeDtypeStruct((HEADS, nchunk, HEAD_V, ATT_TK), BF16),
                   jax.ShapeDtypeStruct((s, DIFF_WIDTH), F32)],
        compiler_params=_params(2),
        name="qkv_proj",
    )(h, w, w, w, w, jnp.tile(qnw, (1, 2)), jnp.tile(knw, (1, 2)))


def _attn_kernel(slopes_ref, qt_ref, ka_ref, vt_ref, lq1_ref, lk1_ref, lq2_ref, lk2_ref, subw_ref, g_ref,
                 o_ref, acc_ref, mu_ref, qv_ref, kmax_ref, p_a, p_b, den_ref, *, lambda_init, nchunk, unroll):
    hd = pl.program_id(0)
    i = pl.program_id(1)
    base = hd * (1 + COEF_PARTS)
    slope = slopes_ref[base]
    q_off = lax.broadcasted_iota(jnp.int32, (1, ATT_TQ), 1).astype(F32)

    row1 = lax.broadcasted_iota(jnp.int32, (LANES - HEAD_DIM, 1), 0)
    coef_col = jnp.zeros((LANES - HEAD_DIM, 1), F32)
    for t in range(COEF_PARTS):
        piece = slopes_ref[base + 1 + t]
        coef_col = jnp.where(row1 == t, piece, coef_col)
        coef_col = jnp.where(row1 == COEF_PARTS + t, piece * POS_RADIX, coef_col)
    coef_rows = jnp.broadcast_to(coef_col, (LANES - HEAD_DIM, ATT_TQ))
    for m in range(2):
        for kind, sign in ((0, -1.0), (1, 0.0), (2, 1.0)):
            qv_ref[kind, m, 0:HEAD_DIM, :] = qt_ref[0, m, 0:HEAD_DIM, :]
            qv_ref[kind, m, HEAD_DIM:LANES, :] = (sign * coef_rows).astype(BF16)

    @pl.when(i == 0)
    def _():
        lane = lax.broadcasted_iota(jnp.int32, (ATT_TK, LANES), 1)
        for m in range(2):
            def widest(c, best):
                kc = ka_ref[0, m, c].astype(F32)
                return jnp.maximum(best, jnp.sum(jnp.where(lane < HEAD_DIM, kc * kc, 0.0), axis=1, keepdims=True))
            kmax_ref[m] = jnp.max(lax.fori_loop(0, nchunk, widest, jnp.zeros((ATT_TK, 1), F32)))

    for m in range(2):
        qf = qt_ref[0, m, 0:HEAD_DIM, :].astype(F32)
        bound = jnp.sqrt(jnp.sum(qf * qf, axis=0, keepdims=True) * kmax_ref[m]) * BOUND_SLACK
        mu_ref[m] = bound - REF_MARGIN

    c_diag = i * ATT_DIAG

    def chunk_of(e):
        if isinstance(e, int) and e < ATT_DIAG:
            return c_diag + e
        rest = e - ATT_DIAG
        return jnp.where(e < ATT_DIAG, c_diag + e, rest + ATT_DIAG * (rest >= c_diag).astype(jnp.int32))

    def side_of(c):
        return (c < c_diag).astype(jnp.int32) - (c >= c_diag + ATT_DIAG).astype(jnp.int32)

    p_bufs = (p_a, p_b)

    def probs(e, p_ref):
        c = chunk_of(e)
        side = side_of(c)
        sidef = side.astype(F32)
        gap = jnp.abs(c * ATT_TK - i * ATT_TQ).astype(F32)
        cvec = -slope * (sidef * sidef * gap + sidef * q_off)
        for m in range(2):
            st = jnp.dot(ka_ref[0, m, c], qv_ref[1 + side, m], preferred_element_type=F32)
            if isinstance(e, int) and e < ATT_DIAG:
                key_off = lax.broadcasted_iota(jnp.int32, (ATT_TK, ATT_TQ), 0) + e * ATT_TK
                qry_off = lax.broadcasted_iota(jnp.int32, (ATT_TK, ATT_TQ), 1)
                st = st - slope * jnp.abs(key_off - qry_off).astype(F32)
            p = jnp.exp2(st - (mu_ref[m] - cvec))
            p_ref[m] = p.astype(BF16)
            den_ref[m] += jnp.sum(p.reshape(ATT_TK // SUBLANES, SUBLANES, ATT_TQ), axis=0)

    def group(e0, count, probs_last):
        sums = [None, None]
        for u in range(count):
            if u < count - 1 or probs_last:
                probs(e0 + u + 1, p_bufs[(u + 1) % 2])
            c = chunk_of(e0 + u)
            for m in range(2):
                pv = jnp.dot(vt_ref[0, c], p_bufs[u % 2][m], preferred_element_type=F32)
                sums[m] = pv if sums[m] is None else sums[m] + pv
        for m in range(2):
            acc_ref[m] += sums[m]

    def loop_body(it, carry):
        group(it * unroll, unroll, True)
        return carry

    def one_pass(_):
        acc_ref[...] = jnp.zeros(acc_ref.shape, F32)
        den_ref[...] = jnp.zeros(den_ref.shape, F32)
        n_groups = nchunk // unroll
        probs(0, p_a)
        group(0, unroll, True)
        lax.fori_loop(1, n_groups - 1, loop_body, 0)
        group((n_groups - 1) * unroll, unroll, False)
        lowest = None
        for m in range(2):
            den = jnp.sum(den_ref[m], axis=0, keepdims=True)
            mu_ref[m] = jnp.where(den < REF_FLOOR, mu_ref[m] - 2.0 * REF_MARGIN, mu_ref[m])
            lowest = jnp.min(den) if lowest is None else jnp.minimum(lowest, jnp.min(den))
        return (lowest < REF_FLOOR).astype(jnp.int32)

    lax.while_loop(lambda retry: retry > 0, one_pass, jnp.int32(1))

    lam = (jnp.exp(jnp.sum(lq1_ref[...] * lk1_ref[...], axis=-1, keepdims=True))
           - jnp.exp(jnp.sum(lq2_ref[...] * lk2_ref[...], axis=-1, keepdims=True)) + lambda_init)
    den1 = jnp.sum(den_ref[0], axis=0, keepdims=True)
    den2 = jnp.sum(den_ref[1], axis=0, keepdims=True)
    o = acc_ref[0] / den1 - lam * (acc_ref[1] / den2)
    r = lax.rsqrt(jnp.mean(o * o, axis=0, keepdims=True) + NORM_EPS)
    y = (o * r * subw_ref[...]) * (1.0 - lambda_init)
    o_ref[...] = (y.T * jax.nn.silu(g_ref[...])).astype(o_ref.dtype)


def _diff_attn(qt, ka, vt, lq1, lk1, lq2, lk2, subw, g, lambda_init):
    s = qt.shape[-1]
    g_col0 = (g.shape[1] - DIFF_WIDTH) // HEAD_V
    nchunk = s // ATT_TK
    unroll = min(ATT_UNROLL, nchunk // 2)
    assert unroll % 2 == 0 and nchunk % unroll == 0 and unroll >= ATT_DIAG and s % ATT_TQ == 0
    whole = jnp.asarray([LOG2E * 2.0 ** (-8.0 * (h + 1) / HEADS) for h in range(HEADS)], F32)
    pieces, rest = [], whole
    for _ in range(COEF_PARTS):
        pieces.append(rest.astype(BF16).astype(F32))
        rest = rest - pieces[-1]
    slopes = jnp.stack([whole] + pieces, axis=1).reshape(-1)
    vec = pl.BlockSpec((1, HEAD_DIM), lambda h, i: (0, 0))
    return pl.pallas_call(
        functools.partial(_attn_kernel, lambda_init=lambda_init, nchunk=nchunk, unroll=unroll),
        grid=(HEADS, s // ATT_TQ),
        in_specs=[pl.BlockSpec(memory_space=pltpu.SMEM),
                  pl.BlockSpec((1, 2, LANES, ATT_TQ), lambda h, i: (h, 0, 0, i)),
                  pl.BlockSpec((1, 2, nchunk, ATT_TK, LANES), lambda h, i: (h, 0, 0, 0, 0)),
                  pl.BlockSpec((1, nchunk, HEAD_V, ATT_TK), lambda h, i: (h, 0, 0, 0)),
                  vec, vec, vec, vec,
                  pl.BlockSpec((HEAD_V, 1), lambda h, i: (0, 0)),
                  pl.BlockSpec((ATT_TQ, HEAD_V), lambda h, i: (i, g_col0 + h))],
        out_specs=pl.BlockSpec((ATT_TQ, HEAD_V), lambda h, i: (i, h)),
        out_shape=jax.ShapeDtypeStruct((s, DIFF_WIDTH), BF16),
        scratch_shapes=[pltpu.VMEM((2, HEAD_V, ATT_TQ), F32), pltpu.VMEM((2, 1, ATT_TQ), F32),
                        pltpu.VMEM((3, 2, LANES, ATT_TQ), BF16), pltpu.SMEM((2,), F32),
                        pltpu.VMEM((2, ATT_TK, ATT_TQ), BF16), pltpu.VMEM((2, ATT_TK, ATT_TQ), BF16),
                        pltpu.VMEM((2, SUBLANES, ATT_TQ), F32)],
        compiler_params=_params(2),
        name="diff_attn",
    )(slopes, qt, ka, vt, lq1, lk1, lq2, lk2, subw.reshape(HEAD_V, 1), g)


def _pool_kernel(hprev_ref, h_ref, hnext_ref, wu_ref, wg_ref, w_ref, scale_ref, o_ref, ext_ref, cur_ref, gate_ref,
                 hx_ref, *, tm, seq):
    i = pl.program_id(0)
    last = pl.num_programs(0) - 1
    t = lax.broadcasted_iota(jnp.int32, (tm, 1), 0) + i * tm
    pair = 2 * POOL_GROUP

    hx_ref[0:HALO_BLOCK] = hprev_ref[...]
    hx_ref[HALO_BLOCK:HALO_BLOCK + tm] = h_ref[...]
    hx_ref[HALO_BLOCK + tm:2 * HALO_BLOCK + tm] = hnext_ref[...]
    first_u = HALO_BLOCK - POOL_HALO

    def project_u(p):
        cols2 = pl.ds(p * pair, pair)
        u = jnp.dot(hx_ref[...], wu_ref[:, cols2], preferred_element_type=F32)
        cur_ref[:, cols2] = u[HALO_BLOCK:HALO_BLOCK + tm]
        ext_ref[pl.ds(0, POOL_HALO), cols2] = jnp.where(i > 0, u[first_u:HALO_BLOCK], 0.0)
        ext_ref[pl.ds(POOL_HALO, tm), cols2] = u[HALO_BLOCK:HALO_BLOCK + tm]
        ext_ref[pl.ds(POOL_HALO + tm, POOL_HALO), cols2] = jnp.where(
            i < last, u[HALO_BLOCK + tm:HALO_BLOCK + tm + POOL_HALO], 0.0)

    def project_gate(p):
        cols2 = pl.ds(p * pair, pair)
        gate_ref[:, cols2] = jnp.dot(h_ref[...], wg_ref[:, cols2], preferred_element_type=F32)

    project_u(0)
    project_u(1)
    project_gate(0)
    for g, w in enumerate(POOL_WINDOWS):
        cols = pl.ds(g * POOL_GROUP, POOL_GROUP)
        if g == 2:
            project_gate(1)
        span = tm + 2 * POOL_HALO
        width = 1
        while width < w:
            span -= width
            ext_ref[pl.ds(0, span), cols] = ext_ref[pl.ds(0, span), cols] + ext_ref[pl.ds(width, span), cols]
            width *= 2
        assert width == w
        win = ext_ref[pl.ds(POOL_HALO - w // 2, tm), cols]
        lo = jnp.maximum(t - w // 2, 0)
        hi = jnp.minimum(t + (w - w // 2) - 1, seq - 1)
        cnt = (hi - lo + 1).astype(F32)
        pooled = win * (1.0 / cnt) - cur_ref[:, cols]
        y = jnp.dot(pooled.astype(BF16), w_ref[g], preferred_element_type=F32)
        o_ref[:, cols] = (y * scale_ref[:, cols] * jax.nn.silu(gate_ref[:, cols])).astype(o_ref.dtype)


def _pool_branch(h, w_in, layer, col_u, col_g, w_pool, pool_scale):
    s, d = h.shape
    width = POOL_WIDTH
    tm = min(1024, s)
    hb = tm // HALO_BLOCK
    nhalo = s // HALO_BLOCK
    assert col_u % width == 0 and col_g % width == 0
    once = pl.Buffered(1)
    return pl.pallas_call(
        functools.partial(_pool_kernel, tm=tm, seq=s),
        grid=(s // tm,),
        in_specs=[pl.BlockSpec((HALO_BLOCK, d), lambda i: (jnp.maximum(i * hb - 1, 0), 0)),
                  pl.BlockSpec((tm, d), lambda i: (i, 0)),
                  pl.BlockSpec((HALO_BLOCK, d), lambda i: (jnp.minimum((i + 1) * hb, nhalo - 1), 0)),
                  pl.BlockSpec((None, d, width), lambda i: (layer, 0, col_u // width), pipeline_mode=once),
                  pl.BlockSpec((None, d, width), lambda i: (layer, 0, col_g // width), pipeline_mode=once),
                  pl.BlockSpec(w_pool.shape, lambda i: (0, 0, 0)),
                  pl.BlockSpec((1, width), lambda i: (0, 0))],
        out_specs=pl.BlockSpec((tm, width), lambda i: (i, 0)),
        out_shape=jax.ShapeDtypeStruct((s, width), BF16),
        scratch_shapes=[pltpu.VMEM((tm + 2 * POOL_HALO, width), F32), pltpu.VMEM((tm, width), F32),
                        pltpu.VMEM((tm, width), F32), pltpu.VMEM((tm + 2 * HALO_BLOCK, d), BF16)],
        compiler_params=_params(1),
        name="pool_branch",
    )(h, h, h, w_in, w_in, w_pool, pool_scale)


def _out_kernel(*refs, widths, tm, modulate_next, gate_proj):
    n = len(widths)
    y_refs = refs[:n]
    refs = refs[n:]
    if gate_proj:
        hin_ref, wg_ref = refs[:2]
        refs = refs[2:]
    w_ref, x_ref, gate_ref = refs[:3]
    if modulate_next:
        nw_ref, shift_ref, scale_ref, o_ref, h_ref = refs[3:]
        mul = nw_ref[...]
        one_plus_scale = 1.0 + scale_ref[...]
    else:
        (o_ref,) = refs[3:]
    sub = min(OUT_SUB, tm // 2)
    for sb in range(tm // sub):
        rows = slice(sb * sub, (sb + 1) * sub)
        acc = None
        off = 0
        for y_ref, wd in zip(y_refs, widths):
            y = y_ref[rows, :]
            if gate_proj:
                y = y * jax.nn.silu(jnp.dot(hin_ref[rows, :], wg_ref[:, off:off + wd], preferred_element_type=F32))
            part = jnp.dot(y.astype(BF16), w_ref[off:off + wd, :], preferred_element_type=F32)
            acc = part if acc is None else acc + part
            off += wd
        xn = x_ref[rows, :] + gate_ref[...] * acc
        o_ref[rows, :] = xn
        if modulate_next:
            r = lax.rsqrt(jnp.mean(xn * xn, axis=-1, keepdims=True) + NORM_EPS)
            h_ref[rows, :] = ((xn * r * mul) * one_plus_scale + shift_ref[...]).astype(BF16)


def _out_proj(ys, w, layer, x, gate, next_mod=None, gate_from=None):
    s, d = x.shape
    k = w.shape[1]
    gate_proj = gate_from is not None
    tm = min(512, s)
    once = pl.Buffered(1) if gate_proj else None
    widths = tuple(y.shape[1] for y in ys)
    vec = pl.BlockSpec((1, d), lambda i: (0, 0))
    rows = pl.BlockSpec((tm, d), lambda i: (i, 0))
    modulate_next = next_mod is not None
    gate_specs, gate_args = [], ()
    if gate_proj:
        hin, wg, col0 = gate_from
        assert col0 % k == 0 and sum(widths) == k
        gate_specs = [pl.BlockSpec((tm, hin.shape[1]), lambda i: (i, 0)),
                      pl.BlockSpec((None, wg.shape[1], k), lambda i: (layer, 0, col0 // k), pipeline_mode=once)]
        gate_args = (hin, wg)
    return pl.pallas_call(
        functools.partial(_out_kernel, widths=widths, tm=tm, modulate_next=modulate_next, gate_proj=gate_proj),
        grid=(s // tm,),
        in_specs=[pl.BlockSpec((tm, wd), lambda i: (i, 0)) for wd in widths] + gate_specs
        + [pl.BlockSpec((None, k, d), lambda i: (layer, 0, 0), pipeline_mode=once), rows, vec]
        + ([vec, vec, vec] if modulate_next else []),
        out_specs=[rows, rows] if modulate_next else rows,
        out_shape=([jax.ShapeDtypeStruct((s, d), F32), jax.ShapeDtypeStruct((s, d), BF16)] if modulate_next
                   else jax.ShapeDtypeStruct((s, d), F32)),
        compiler_params=_params(1),
        name="out_proj",
    )(*ys, *gate_args, w, x, gate, *(next_mod if modulate_next else ()))


def _dft_split(s):
    n1 = 1 << (int(math.log2(s)) // 2)
    n2 = s // n1
    assert n1 * n2 == s and n1 % SUBLANES == 0 and n2 % SUBLANES == 0
    return n1, n2


def _angles(num, den):
    ang = (2.0 * math.pi / den) * (num % den)
    return np.cos(ang), np.sin(ang)


def _fold_cs_kernel(cs_ref, wf_ref, o_ref):
    wf = wf_ref[...]
    o_ref[:, 0:FOURIER_GROUP] = jnp.dot(cs_ref[0], wf, preferred_element_type=F32,
                                        precision=lax.Precision.HIGHEST)
    o_ref[:, FOURIER_GROUP:] = jnp.dot(cs_ref[1], wf, preferred_element_type=F32,
                                       precision=lax.Precision.HIGHEST)


def _fold_channel_dft(w_fourier_j, seq):
    fg = FOURIER_GROUP
    idx = np.arange(fg)
    cc, sc = _angles(idx[:, None] * idx[None, :], fg)
    cs = jnp.asarray(np.stack([cc, sc]) / math.sqrt(seq * fg), F32)
    return pl.pallas_call(
        _fold_cs_kernel,
        grid=(FOURIER_GROUPS,),
        in_specs=[pl.BlockSpec((2, fg, fg), lambda g: (0, 0, 0)),
                  pl.BlockSpec((None, fg, fg), lambda g: (g, 0, 0))],
        out_specs=pl.BlockSpec((None, fg, 2 * fg), lambda g: (g, 0, 0)),
        out_shape=jax.ShapeDtypeStruct((FOURIER_GROUPS, fg, 2 * fg), F32),
        compiler_params=_params(1),
        name="fold_channel_dft",
    )(cs, w_fourier_j)


def _fold_win_kernel(w_ref, ab_ref, o_ref):
    o_ref[...] = jnp.dot(w_ref[...], ab_ref[...].astype(BF16), preferred_element_type=F32).astype(BF16)


def _fold_in_proj(w_in, layer, ab):
    d = w_in.shape[1]
    fg = FOURIER_GROUP
    return pl.pallas_call(
        _fold_win_kernel,
        grid=(FOURIER_GROUPS, 2),
        in_specs=[pl.BlockSpec((None, d, fg), lambda g, t: (layer, 0, g)),
                  pl.BlockSpec((None, fg, fg), lambda g, t: (g, 0, t))],
        out_specs=pl.BlockSpec((d, fg), lambda g, t: (0, g + t * FOURIER_GROUPS)),
        out_shape=jax.ShapeDtypeStruct((d, 2 * D_MODEL), BF16),
        compiler_params=_params(2),
        name="fold_in_proj",
    )(w_in, ab)


HIGH_HALF = -65536


def _bf16_bits(v):
    b = lax.bitcast_convert_type(v, jnp.int32)
    return b + 0x7FFF + (lax.shift_right_logical(b, 16) & 1)


def _dft1_kernel(h_ref, wp_ref, wq_ref, f_ref, t_ref, *, n1, per_sub, n_sub):
    f = f_ref[...]
    for sb in range(n_sub):
        rows = slice(sb * per_sub * n1, (sb + 1) * per_sub * n1)
        h = h_ref[rows, :]
        p = jnp.dot(h, wp_ref[...], preferred_element_type=F32).astype(BF16)
        q = jnp.dot(h, wq_ref[...], preferred_element_type=F32).astype(BF16)
        for jj in range(per_sub):
            r = slice(jj * n1, (jj + 1) * n1)
            t = jnp.dot(f, jnp.concatenate([p[r], q[r]], axis=0), preferred_element_type=F32)
            t_ref[sb * per_sub + jj] = ((_bf16_bits(t[0:n1]) & HIGH_HALF)
                                        | lax.shift_right_logical(_bf16_bits(t[n1:2 * n1]), 16))


def _dft_stage1(h_t, w_pq, f1, n1, n2):
    s, d = h_t.shape
    c = w_pq.shape[1] // 2
    tm = min(PROJ_TM, s)
    tn = PROJ_TN
    per_sub = max(1, 256 // n1)
    n_sub = tm // (per_sub * n1)
    ncb = c // tn
    return pl.pallas_call(
        functools.partial(_dft1_kernel, n1=n1, per_sub=per_sub, n_sub=n_sub),
        grid=(s // tm, ncb),
        in_specs=[pl.BlockSpec((tm, d), lambda i, j: (i, 0)),
                  pl.BlockSpec((d, tn), lambda i, j: (0, j)),
                  pl.BlockSpec((d, tn), lambda i, j: (0, j + ncb)),
                  pl.BlockSpec((2 * n1, 2 * n1), lambda i, j: (0, 0))],
        out_specs=pl.BlockSpec((tm // n1, n1, tn), lambda i, j: (i, 0, j)),
        out_shape=jax.ShapeDtypeStruct((n2, n1, c), jnp.int32),
        compiler_params=_params(2),
        name="fourier_in_dft1",
    )(h_t, w_pq, w_pq, f1)


def _dft2_kernel(l_ref, t_ref, o_ref, *, n2, cb):
    packed = t_ref[...]
    re = lax.bitcast_convert_type(packed & HIGH_HALF, F32)
    im = lax.bitcast_convert_type(lax.shift_left(packed, 16), F32)
    rhs = jnp.stack([re, im], axis=1).reshape(n2 * 2 * SUBLANES, cb).astype(BF16)
    o_ref[...] = jnp.dot(l_ref[0], rhs, preferred_element_type=F32).reshape(n2, SUBLANES, cb)


def _dft_stage2(t_nk, l2, n1, n2):
    c = t_nk.shape[-1]
    cb = min(DFT_CB, c)
    row_blocks = pl.BlockSpec((n2, SUBLANES, cb), lambda g, j: (0, g, j))
    out = pl.pallas_call(
        functools.partial(_dft2_kernel, n2=n2, cb=cb),
        grid=(n1 // SUBLANES, c // cb),
        in_specs=[pl.BlockSpec((1, SUBLANES * n2, SUBLANES * 2 * n2), lambda g, j: (g, 0, 0)),
                  row_blocks],
        out_specs=row_blocks,
        out_shape=jax.ShapeDtypeStruct((n2, n1, c), F32),
        compiler_params=_params(2),
        name="dft_stage2",
    )(l2, t_nk)
    return out.reshape(n1 * n2, c)


def _dft_matrices(s):
    n1, n2 = _dft_split(s)
    a = np.arange(n1)
    c1, s1 = _angles(a[:, None] * a[None, :], n1)
    f1 = jnp.asarray(np.block([[c1, -s1], [-s1, -c1]]).astype(BF16))
    groups = n1 // SUBLANES
    g = np.arange(groups)[:, None, None, None]
    k2 = np.arange(n2)[None, :, None, None]
    j = np.arange(SUBLANES)[None, None, :, None]
    nn = np.arange(n2)[None, None, None, :]
    c2, s2 = _angles((SUBLANES * g + j + n1 * k2) * nn, s)
    cs = np.concatenate([c2, s2], axis=3).astype(BF16)
    return n1, n2, f1, _expand_twiddles(jnp.asarray(cs.reshape(groups, n2 * SUBLANES, 2 * n2)))


def _expand_kernel(cs_ref, e_ref, o_ref):
    spread = jnp.dot(cs_ref[0], e_ref[...], preferred_element_type=F32)
    row = lax.broadcasted_iota(jnp.int32, spread.shape, 0)
    col = lax.broadcasted_iota(jnp.int32, spread.shape, 1)
    o_ref[0] = jnp.where(row % SUBLANES == col % SUBLANES, spread, 0.0).astype(BF16)


def _expand_twiddles(cs):
    groups, rows, cols = cs.shape
    wide = cols * SUBLANES
    src = jnp.arange(cols, dtype=jnp.int32)[:, None]
    slot = (src % (cols // 2)) * 2 + src // (cols // 2)
    e = jnp.arange(wide, dtype=jnp.int32)[None, :] // SUBLANES == slot
    return pl.pallas_call(
        _expand_kernel,
        grid=(groups,),
        in_specs=[pl.BlockSpec((1, rows, cols), lambda g: (g, 0, 0)),
                  pl.BlockSpec((cols, wide), lambda g: (0, 0))],
        out_specs=pl.BlockSpec((1, rows, wide), lambda g: (g, 0, 0)),
        out_shape=jax.ShapeDtypeStruct((groups, rows, wide), BF16),
        compiler_params=_params(1),
        name="expand_twiddles",
    )(cs, e.astype(BF16))


def kernel(x, c, norm_w, ada_w, ada_b, w_in_ab, w_pool, pool_scale, q_norm_w, k_norm_w, lambda_q1, lambda_k1,
           lambda_q2, lambda_k2, subln_w, w_out_ab, w_in_c, w_fourier, w_out_c):
    batch, s, d = x.shape
    assert batch == 1 and d == D_MODEL and s % ATT_TK == 0
    xs = x.reshape(s, d)
    mod = _ada_mod(c, ada_w, ada_b)
    n1, n2, f1, l2 = _dft_matrices(s)
    row = lambda v: v.reshape(1, -1)

    def mod_of(i):
        return mod[i, :, 0:d], mod[i, :, d:2 * d], mod[i, :, 2 * d:3 * d]

    def finish(ys, w_out, j, xs, i, gate_from=None):
        gate = mod_of(i)[2]
        if i + 1 == DEPTH:
            return _out_proj(ys, w_out, j, xs, gate, gate_from=gate_from), None
        shift, scale, _ = mod_of(i + 1)
        return _out_proj(ys, w_out, j, xs, gate, (row(norm_w[i + 1]), shift, scale), gate_from=gate_from)

    w_in_ab, w_out_ab, w_in_c, w_out_c = (w.astype(BF16) for w in (w_in_ab, w_out_ab, w_in_c, w_out_c))
    h = _modulate(xs, row(norm_w[0]), *mod_of(0)[:2])
    for i in range(DEPTH):
        j = i // 2
        if i % 2 == 0:
            o1, o2, o3, o4 = POOL_WIDTH, POOL_WIDTH + DIFF_WIDTH, POOL_WIDTH + 2 * DIFF_WIDTH, POOL_WIDTH + 3 * DIFF_WIDTH
            lambda_init = 0.8 - 0.6 * math.exp(-0.3 * i)
            y_a = _pool_branch(h, w_in_ab, j, 0, o4, w_pool[j].astype(BF16), row(pool_scale[j]))
            qt, ka, vt, g_attn = _qkv_proj(h, w_in_ab, j, o1, o2, o3, o4 + POOL_WIDTH,
                                           row(q_norm_w[j]), row(k_norm_w[j]))
            y_b = _diff_attn(qt, ka, vt, row(lambda_q1[j]), row(lambda_k1[j]), row(lambda_q2[j]),
                             row(lambda_k2[j]), subln_w[j], g_attn, lambda_init)
            xs, h = finish([y_a, y_b], w_out_ab, j, xs, i)
        else:
            ab = _fold_channel_dft(w_fourier[j], s)
            w_pq = _fold_in_proj(w_in_c, j, ab)
            h_t = h.reshape(n1, n2, d).transpose(1, 0, 2).reshape(s, d)
            t_nk = _dft_stage1(h_t, w_pq, f1, n1, n2)
            f = _dft_stage2(t_nk, l2, n1, n2)
            xs, h = finish([f], w_out_c, j, xs, i, gate_from=(h, w_in_c, d))
    return xs.reshape(batch, s, d)
```

```python
import functools
import math

import jax
import jax.numpy as jnp
import numpy as np
from jax import lax
from jax.experimental import pallas as pl
from jax.experimental.pallas import tpu as pltpu

F32 = jnp.float32
BF16 = jnp.bfloat16

D_MODEL = 2048
DEPTH = 4
NORM_EPS = 1e-6

POOL_WINDOWS = (2, 4, 8, 16)
POOL_WIDTH = D_MODEL // 2
POOL_GROUP = POOL_WIDTH // len(POOL_WINDOWS)
POOL_HALO = 8
HALO_BLOCK = 16

HEADS = 8
HEAD_DIM = 64
HEAD_V = 2 * HEAD_DIM
DIFF_WIDTH = HEADS * HEAD_V

FOURIER_GROUPS = 4
FOURIER_GROUP = D_MODEL // FOURIER_GROUPS

LANES = 128
SUBLANES = 8
POS_RADIX = 256
COEF_PARTS = 3
LOG2E = math.log2(math.e)

ATT_TQ = 512
ATT_TK = 512
ATT_DIAG = ATT_TQ // ATT_TK
ATT_UNROLL = 8
REF_MARGIN = 60.0
REF_FLOOR = 2.0 ** -60
BOUND_SLACK = 1.001

PROJ_TM, PROJ_TN = 2048, 1024
QKV_HEADS = 4
OUT_SUB = 256
DFT_CB = 2048

VMEM_LIMIT = 56 * 1024 * 1024


def _params(n_axes):
    return pltpu.CompilerParams(dimension_semantics=("arbitrary",) * n_axes,
                                vmem_limit_bytes=VMEM_LIMIT)


def _mod_kernel(c_ref, w_ref, b_ref, o_ref):
    c = c_ref[...]
    o_ref[...] = jnp.sum(jax.nn.silu(c) * w_ref[...], axis=0, keepdims=True) + b_ref[...]


def _ada_mod(c, ada_w, ada_b):
    depth, d, n = ada_w.shape
    tn = 1024
    return pl.pallas_call(
        _mod_kernel,
        grid=(depth, n // tn),
        in_specs=[pl.BlockSpec((d, 1), lambda i, j: (0, 0)),
                  pl.BlockSpec((None, d, tn), lambda i, j: (i, 0, j)),
                  pl.BlockSpec((None, 1, tn), lambda i, j: (i, 0, j))],
        out_specs=pl.BlockSpec((None, 1, tn), lambda i, j: (i, 0, j)),
        out_shape=jax.ShapeDtypeStruct((depth, 1, n), F32),
        compiler_params=_params(2),
        name="ada_mod",
    )(c.reshape(d, 1), ada_w, ada_b.reshape(depth, 1, n))


def _modulate_kernel(x_ref, nw_ref, shift_ref, scale_ref, o_ref):
    xf = x_ref[...]
    r = lax.rsqrt(jnp.mean(xf * xf, axis=-1, keepdims=True) + NORM_EPS)
    y = xf * r * nw_ref[...]
    y = y * (1.0 + scale_ref[...]) + shift_ref[...]
    o_ref[...] = y.astype(o_ref.dtype)


def _modulate(x, nw, shift, scale):
    s, d = x.shape
    tm = min(1024, s)
    vec = pl.BlockSpec((1, d), lambda i: (0, 0))
    return pl.pallas_call(
        _modulate_kernel,
        grid=(s // tm,),
        in_specs=[pl.BlockSpec((tm, d), lambda i: (i, 0)), vec, vec, vec],
        out_specs=pl.BlockSpec((tm, d), lambda i: (i, 0)),
        out_shape=jax.ShapeDtypeStruct((s, d), BF16),
        compiler_params=_params(1),
        name="modulate",
    )(x, nw, shift, scale)


def _qkv_kernel(h_ref, wq_ref, wk_ref, wv_ref, wg_ref, qnw_ref, knw_ref, qt_ref, ka_ref, vt_ref, g_ref, *, tm):
    lane = lax.broadcasted_iota(jnp.int32, (ATT_TK, LANES), 1)
    first = lane < HEAD_DIM
    pos = lax.broadcasted_iota(jnp.int32, (ATT_TK, LANES), 0)
    pos_lo = (pos % POS_RADIX).astype(F32)
    pos_hi = (pos // POS_RADIX).astype(F32)
    pos_cols = jnp.where(lane < HEAD_DIM + COEF_PARTS, pos_lo, jnp.where(lane < HEAD_DIM + 2 * COEF_PARTS, pos_hi, 0.0))

    def halves_rms(z, w):
        sq = z * z
        ss1 = jnp.sum(jnp.where(first, sq, 0.0), axis=-1, keepdims=True)
        ss2 = jnp.sum(jnp.where(first, 0.0, sq), axis=-1, keepdims=True)
        r = jnp.where(first, lax.rsqrt(ss1 / HEAD_DIM + NORM_EPS), lax.rsqrt(ss2 / HEAD_DIM + NORM_EPS))
        return z * r * w

    for cc in range(tm // ATT_TK):
        rows = slice(cc * ATT_TK, (cc + 1) * ATT_TK)
        h = h_ref[rows, :]
        zq = jnp.dot(h, wq_ref[...], preferred_element_type=F32)
        zk = jnp.dot(h, wk_ref[...], preferred_element_type=F32)
        zv = jnp.dot(h, wv_ref[...], preferred_element_type=F32)
        g_ref[rows, :] = jnp.dot(h, wg_ref[...], preferred_element_type=F32)
        for hh in range(QKV_HEADS):
            cols = slice(hh * LANES, (hh + 1) * LANES)
            qn = halves_rms(zq[:, cols], qnw_ref[...]) * (HEAD_DIM ** -0.5) * LOG2E
            kn = halves_rms(zk[:, cols], knw_ref[...])
            for m in range(2):
                qm = qn if m == 0 else pltpu.roll(qn, HEAD_DIM, axis=1)
                km = kn if m == 0 else pltpu.roll(kn, HEAD_DIM, axis=1)
                qt_ref[hh, m, :, rows] = jnp.where(first, qm, 0.0).T.astype(BF16)
                ka_ref[hh, m, cc] = jnp.where(first, km, pos_cols).astype(BF16)
            vt_ref[hh, cc] = zv[:, cols].T.astype(BF16)


def _qkv_proj(h, w, layer, col_q, col_k, col_v, col_g, qnw, knw):
    s, d = h.shape
    tm = min(1024, s)
    nchunk = s // ATT_TK
    cpt = tm // ATT_TK
    tn = QKV_HEADS * LANES
    cols = (col_q, col_k, col_v, col_g)
    assert all(c0 % tn == 0 for c0 in cols)
    wspecs = [pl.BlockSpec((None, d, tn), functools.partial(lambda i, j, jb: (layer, 0, j + jb), jb=c0 // tn))
              for c0 in cols]
    vec = pl.BlockSpec((1, LANES), lambda i, j: (0, 0))
    return pl.pallas_call(
        functools.partial(_qkv_kernel, tm=tm),
        grid=(s // tm, HEADS // QKV_HEADS),
        in_specs=[pl.BlockSpec((tm, d), lambda i, j: (i, 0)), *wspecs, vec, vec],
        out_specs=[pl.BlockSpec((QKV_HEADS, 2, LANES, tm), lambda i, j: (j, 0, 0, i)),
                   pl.BlockSpec((QKV_HEADS, 2, cpt, ATT_TK, LANES), lambda i, j: (j, 0, i, 0, 0)),
                   pl.BlockSpec((QKV_HEADS, cpt, HEAD_V, ATT_TK), lambda i, j: (j, i, 0, 0)),
                   pl.BlockSpec((tm, tn), lambda i, j: (i, j))],
        out_shape=[jax.ShapeDtypeStruct((HEADS, 2, LANES, s), BF16),
                   jax.ShapeDtypeStruct((HEADS, 2, nchunk, ATT_TK, LANES), BF16),
                   jax.ShapeDtypeStruct((HEADS, nchunk, HEAD_V, ATT_TK), BF16),
                   jax.ShapeDtypeStruct((s, DIFF_WIDTH), F32)],
        compiler_params=_params(2),
        name="qkv_proj",
    )(h, w, w, w, w, jnp.tile(qnw, (1, 2)), jnp.tile(knw, (1, 2)))


def _attn_kernel(slopes_ref, qt_ref, ka_ref, vt_ref, lq1_ref, lk1_ref, lq2_ref, lk2_ref, subw_ref, g_ref,
                 o_ref, acc_ref, mu_ref, qv_ref, kmax_ref, p_a, p_b, den_ref, *, lambda_init, nchunk, unroll):
    hd = pl.program_id(0)
    i = pl.program_id(1)
    base = hd * (1 + COEF_PARTS)
    slope = slopes_ref[base]
    q_off = lax.broadcasted_iota(jnp.int32, (1, ATT_TQ), 1).astype(F32)

    row1 = lax.broadcasted_iota(jnp.int32, (LANES - HEAD_DIM, 1), 0)
    coef_col = jnp.zeros((LANES - HEAD_DIM, 1), F32)
    for t in range(COEF_PARTS):
        piece = slopes_ref[base + 1 + t]
        coef_col = jnp.where(row1 == t, piece, coef_col)
        coef_col = jnp.where(row1 == COEF_PARTS + t, piece * POS_RADIX, coef_col)
    coef_rows = jnp.broadcast_to(coef_col, (LANES - HEAD_DIM, ATT_TQ))
    for m in range(2):
        for kind, sign in ((0, -1.0), (1, 0.0), (2, 1.0)):
            qv_ref[kind, m, 0:HEAD_DIM, :] = qt_ref[0, m, 0:HEAD_DIM, :]
            qv_ref[kind, m, HEAD_DIM:LANES, :] = (sign * coef_rows).astype(BF16)

    @pl.when(i == 0)
    def _():
        lane = lax.broadcasted_iota(jnp.int32, (ATT_TK, LANES), 1)
        for m in range(2):
            def widest(c, best):
                kc = ka_ref[0, m, c].astype(F32)
                return jnp.maximum(best, jnp.sum(jnp.where(lane < HEAD_DIM, kc * kc, 0.0), axis=1, keepdims=True))
            kmax_ref[m] = jnp.max(lax.fori_loop(0, nchunk, widest, jnp.zeros((ATT_TK, 1), F32)))

    for m in range(2):
        qf = qt_ref[0, m, 0:HEAD_DIM, :].astype(F32)
        bound = jnp.sqrt(jnp.sum(qf * qf, axis=0, keepdims=True) * kmax_ref[m]) * BOUND_SLACK
        mu_ref[m] = bound - REF_MARGIN

    c_diag = i * ATT_DIAG

    def chunk_of(e):
        if isinstance(e, int) and e < ATT_DIAG:
            return c_diag + e
        rest = e - ATT_DIAG
        return jnp.where(e < ATT_DIAG, c_diag + e, rest + ATT_DIAG * (rest >= c_diag).astype(jnp.int32))

    def side_of(c):
        return (c < c_diag).astype(jnp.int32) - (c >= c_diag + ATT_DIAG).astype(jnp.int32)

    p_bufs = (p_a, p_b)

    def probs(e, p_ref):
        c = chunk_of(e)
        side = side_of(c)
        sidef = side.astype(F32)
        gap = jnp.abs(c * ATT_TK - i * ATT_TQ).astype(F32)
        cvec = -slope * (sidef * sidef * gap + sidef * q_off)
        for m in range(2):
            st = jnp.dot(ka_ref[0, m, c], qv_ref[1 + side, m], preferred_element_type=F32)
            if isinstance(e, int) and e < ATT_DIAG:
                key_off = lax.broadcasted_iota(jnp.int32, (ATT_TK, ATT_TQ), 0) + e * ATT_TK
                qry_off = lax.broadcasted_iota(jnp.int32, (ATT_TK, ATT_TQ), 1)
                st = st - slope * jnp.abs(key_off - qry_off).astype(F32)
            p = jnp.exp2(st - (mu_ref[m] - cvec))
            p_ref[m] = p.astype(BF16)
            den_ref[m] += jnp.sum(p.reshape(ATT_TK // SUBLANES, SUBLANES, ATT_TQ), axis=0)

    def group(e0, count, probs_last):
        sums = [None, None]
        for u in range(count):
            if u < count - 1 or probs_last:
                probs(e0 + u + 1, p_bufs[(u + 1) % 2])
            c = chunk_of(e0 + u)
            for m in range(2):
                pv = jnp.dot(vt_ref[0, c], p_bufs[u % 2][m], preferred_element_type=F32)
                sums[m] = pv if sums[m] is None else sums[m] + pv
        for m in range(2):
            acc_ref[m] += sums[m]

    def loop_body(it, carry):
        group(it * unroll, unroll, True)
        return carry

    def one_pass(_):
        acc_ref[...] = jnp.zeros(acc_ref.shape, F32)
        den_ref[...] = jnp.zeros(den_ref.shape, F32)
        n_groups = nchunk // unroll
        probs(0, p_a)
        group(0, unroll, True)
        lax.fori_loop(1, n_groups - 1, loop_body, 0)
        group((n_groups - 1) * unroll, unroll, False)
        lowest = None
        for m in range(2):
            den = jnp.sum(den_ref[m], axis=0, keepdims=True)
            mu_ref[m] = jnp.where(den < REF_FLOOR, mu_ref[m] - 2.0 * REF_MARGIN, mu_ref[m])
            lowest = jnp.min(den) if lowest is None else jnp.minimum(lowest, jnp.min(den))
        return (lowest < REF_FLOOR).astype(jnp.int32)

    lax.while_loop(lambda retry: retry > 0, one_pass, jnp.int32(1))

    lam = (jnp.exp(jnp.sum(lq1_ref[...] * lk1_ref[...], axis=-1, keepdims=True))
           - jnp.exp(jnp.sum(lq2_ref[...] * lk2_ref[...], axis=-1, keepdims=True)) + lambda_init)
    den1 = jnp.sum(den_ref[0], axis=0, keepdims=True)
    den2 = jnp.sum(den_ref[1], axis=0, keepdims=True)
    o = acc_ref[0] / den1 - lam * (acc_ref[1] / den2)
    r = lax.rsqrt(jnp.mean(o * o, axis=0, keepdims=True) + NORM_EPS)
    y = (o * r * subw_ref[...]) * (1.0 - lambda_init)
    o_ref[...] = (y.T * jax.nn.silu(g_ref[...])).astype(o_ref.dtype)


def _diff_attn(qt, ka, vt, lq1, lk1, lq2, lk2, subw, g, lambda_init):
    s = qt.shape[-1]
    g_col0 = (g.shape[1] - DIFF_WIDTH) // HEAD_V
    nchunk = s // ATT_TK
    unroll = min(ATT_UNROLL, nchunk // 2)
    assert unroll % 2 == 0 and nchunk % unroll == 0 and unroll >= ATT_DIAG and s % ATT_TQ == 0
    whole = jnp.asarray([LOG2E * 2.0 ** (-8.0 * (h + 1) / HEADS) for h in range(HEADS)], F32)
    pieces, rest = [], whole
    for _ in range(COEF_PARTS):
        pieces.append(rest.astype(BF16).astype(F32))
        rest = rest - pieces[-1]
    slopes = jnp.stack([whole] + pieces, axis=1).reshape(-1)
    vec = pl.BlockSpec((1, HEAD_DIM), lambda h, i: (0, 0))
    return pl.pallas_call(
        functools.partial(_attn_kernel, lambda_init=lambda_init, nchunk=nchunk, unroll=unroll),
        grid=(HEADS, s // ATT_TQ),
        in_specs=[pl.BlockSpec(memory_space=pltpu.SMEM),
                  pl.BlockSpec((1, 2, LANES, ATT_TQ), lambda h, i: (h, 0, 0, i)),
                  pl.BlockSpec((1, 2, nchunk, ATT_TK, LANES), lambda h, i: (h, 0, 0, 0, 0)),
                  pl.BlockSpec((1, nchunk, HEAD_V, ATT_TK), lambda h, i: (h, 0, 0, 0)),
                  vec, vec, vec, vec,
                  pl.BlockSpec((HEAD_V, 1), lambda h, i: (0, 0)),
                  pl.BlockSpec((ATT_TQ, HEAD_V), lambda h, i: (i, g_col0 + h))],
        out_specs=pl.BlockSpec((ATT_TQ, HEAD_V), lambda h, i: (i, h)),
        out_shape=jax.ShapeDtypeStruct((s, DIFF_WIDTH), BF16),
        scratch_shapes=[pltpu.VMEM((2, HEAD_V, ATT_TQ), F32), pltpu.VMEM((2, 1, ATT_TQ), F32),
                        pltpu.VMEM((3, 2, LANES, ATT_TQ), BF16), pltpu.SMEM((2,), F32),
                        pltpu.VMEM((2, ATT_TK, ATT_TQ), BF16), pltpu.VMEM((2, ATT_TK, ATT_TQ), BF16),
                        pltpu.VMEM((2, SUBLANES, ATT_TQ), F32)],
        compiler_params=_params(2),
        name="diff_attn",
    )(slopes, qt, ka, vt, lq1, lk1, lq2, lk2, subw.reshape(HEAD_V, 1), g)


def _pool_kernel(hprev_ref, h_ref, hnext_ref, wu_ref, wg_ref, w_ref, scale_ref, o_ref, ext_ref, cur_ref, gate_ref,
                 hx_ref, *, tm, seq):
    i = pl.program_id(0)
    last = pl.num_programs(0) - 1
    t = lax.broadcasted_iota(jnp.int32, (tm, 1), 0) + i * tm
    pair = 2 * POOL_GROUP

    hx_ref[0:HALO_BLOCK] = hprev_ref[...]
    hx_ref[HALO_BLOCK:HALO_BLOCK + tm] = h_ref[...]
    hx_ref[HALO_BLOCK + tm:2 * HALO_BLOCK + tm] = hnext_ref[...]
    first_u = HALO_BLOCK - POOL_HALO

    def project_u(p):
        cols2 = pl.ds(p * pair, pair)
        u = jnp.dot(hx_ref[...], wu_ref[:, cols2], preferred_element_type=F32)
        cur_ref[:, cols2] = u[HALO_BLOCK:HALO_BLOCK + tm]
        ext_ref[pl.ds(0, POOL_HALO), cols2] = jnp.where(i > 0, u[first_u:HALO_BLOCK], 0.0)
        ext_ref[pl.ds(POOL_HALO, tm), cols2] = u[HALO_BLOCK:HALO_BLOCK + tm]
        ext_ref[pl.ds(POOL_HALO + tm, POOL_HALO), cols2] = jnp.where(
            i < last, u[HALO_BLOCK + tm:HALO_BLOCK + tm + POOL_HALO], 0.0)

    def project_gate(p):
        cols2 = pl.ds(p * pair, pair)
        gate_ref[:, cols2] = jnp.dot(h_ref[...], wg_ref[:, cols2], preferred_element_type=F32)

    project_u(0)
    project_u(1)
    project_gate(0)
    for g, w in enumerate(POOL_WINDOWS):
        cols = pl.ds(g * POOL_GROUP, POOL_GROUP)
        if g == 2:
            project_gate(1)
        span = tm + 2 * POOL_HALO
        width = 1
        while width < w:
            span -= width
            ext_ref[pl.ds(0, span), cols] = ext_ref[pl.ds(0, span), cols] + ext_ref[pl.ds(width, span), cols]
            width *= 2
        assert width == w
        win = ext_ref[pl.ds(POOL_HALO - w // 2, tm), cols]
        lo = jnp.maximum(t - w // 2, 0)
        hi = jnp.minimum(t + (w - w // 2) - 1, seq - 1)
        cnt = (hi - lo + 1).astype(F32)
        pooled = win * (1.0 / cnt) - cur_ref[:, cols]
        y = jnp.dot(pooled.astype(BF16), w_ref[g], preferred_element_type=F32)
        o_ref[:, cols] = (y * scale_ref[:, cols] * jax.nn.silu(gate_ref[:, cols])).astype(o_ref.dtype)


def _pool_branch(h, w_in, layer, col_u, col_g, w_pool, pool_scale):
    s, d = h.shape
    width = POOL_WIDTH
    tm = min(1024, s)
    hb = tm // HALO_BLOCK
    nhalo = s // HALO_BLOCK
    assert col_u % width == 0 and col_g % width == 0
    once = pl.Buffered(1)
    return pl.pallas_call(
        functools.partial(_pool_kernel, tm=tm, seq=s),
        grid=(s // tm,),
        in_specs=[pl.BlockSpec((HALO_BLOCK, d), lambda i: (jnp.maximum(i * hb - 1, 0), 0)),
                  pl.BlockSpec((tm, d), lambda i: (i, 0)),
                  pl.BlockSpec((HALO_BLOCK, d), lambda i: (jnp.minimum((i + 1) * hb, nhalo - 1), 0)),
                  pl.BlockSpec((None, d, width), lambda i: (layer, 0, col_u // width), pipeline_mode=once),
                  pl.BlockSpec((None, d, width), lambda i: (layer, 0, col_g // width), pipeline_mode=once),
                  pl.BlockSpec(w_pool.shape, lambda i: (0, 0, 0)),
                  pl.BlockSpec((1, width), lambda i: (0, 0))],
        out_specs=pl.BlockSpec((tm, width), lambda i: (i, 0)),
        out_shape=jax.ShapeDtypeStruct((s, width), BF16),
        scratch_shapes=[pltpu.VMEM((tm + 2 * POOL_HALO, width), F32), pltpu.VMEM((tm, width), F32),
                        pltpu.VMEM((tm, width), F32), pltpu.VMEM((tm + 2 * HALO_BLOCK, d), BF16)],
        compiler_params=_params(1),
        name="pool_branch",
    )(h, h, h, w_in, w_in, w_pool, pool_scale)


def _out_kernel(*refs, widths, tm, modulate_next, gate_proj):
    n = len(widths)
    y_refs = refs[:n]
    refs = refs[n:]
    if gate_proj:
        hin_ref, wg_ref = refs[:2]
        refs = refs[2:]
    w_ref, x_ref, gate_ref = refs[:3]
    if modulate_next:
        nw_ref, shift_ref, scale_ref, o_ref, h_ref = refs[3:]
        mul = nw_ref[...]
        one_plus_scale = 1.0 + scale_ref[...]
    else:
        (o_ref,) = refs[3:]
    sub = min(OUT_SUB, tm // 2)
    for sb in range(tm // sub):
        rows = slice(sb * sub, (sb + 1) * sub)
        acc = None
        off = 0
        for y_ref, wd in zip(y_refs, widths):
            y = y_ref[rows, :]
            if gate_proj:
                y = y * jax.nn.silu(jnp.dot(hin_ref[rows, :], wg_ref[:, off:off + wd], preferred_element_type=F32))
            part = jnp.dot(y.astype(BF16), w_ref[off:off + wd, :], preferred_element_type=F32)
            acc = part if acc is None else acc + part
            off += wd
        xn = x_ref[rows, :] + gate_ref[...] * acc
        o_ref[rows, :] = xn
        if modulate_next:
            r = lax.rsqrt(jnp.mean(xn * xn, axis=-1, keepdims=True) + NORM_EPS)
            h_ref[rows, :] = ((xn * r * mul) * one_plus_scale + shift_ref[...]).astype(BF16)


def _out_proj(ys, w, layer, x, gate, next_mod=None, gate_from=None):
    s, d = x.shape
    k = w.shape[1]
    gate_proj = gate_from is not None
    tm = min(512, s)
    once = pl.Buffered(1) if gate_proj else None
    widths = tuple(y.shape[1] for y in ys)
    vec = pl.BlockSpec((1, d), lambda i: (0, 0))
    rows = pl.BlockSpec((tm, d), lambda i: (i, 0))
    modulate_next = next_mod is not None
    gate_specs, gate_args = [], ()
    if gate_proj:
        hin, wg, col0 = gate_from
        assert col0 % k == 0 and sum(widths) == k
        gate_specs = [pl.BlockSpec((tm, hin.shape[1]), lambda i: (i, 0)),
                      pl.BlockSpec((None, wg.shape[1], k), lambda i: (layer, 0, col0 // k), pipeline_mode=once)]
        gate_args = (hin, wg)
    return pl.pallas_call(
        functools.partial(_out_kernel, widths=widths, tm=tm, modulate_next=modulate_next, gate_proj=gate_proj),
        grid=(s // tm,),
        in_specs=[pl.BlockSpec((tm, wd), lambda i: (i, 0)) for wd in widths] + gate_specs
        + [pl.BlockSpec((None, k, d), lambda i: (layer, 0, 0), pipeline_mode=once), rows, vec]
        + ([vec, vec, vec] if modulate_next else []),
        out_specs=[rows, rows] if modulate_next else rows,
        out_shape=([jax.ShapeDtypeStruct((s, d), F32), jax.ShapeDtypeStruct((s, d), BF16)] if modulate_next
                   else jax.ShapeDtypeStruct((s, d), F32)),
        compiler_params=_params(1),
        name="out_proj",
    )(*ys, *gate_args, w, x, gate, *(next_mod if modulate_next else ()))


def _dft_split(s):
    n1 = 1 << (int(math.log2(s)) // 2)
    n2 = s // n1
    assert n1 * n2 == s and n1 % SUBLANES == 0 and n2 % SUBLANES == 0
    return n1, n2


def _angles(num, den):
    ang = (2.0 * math.pi / den) * (num % den)
    return np.cos(ang), np.sin(ang)


def _fold_cs_kernel(cs_ref, wf_ref, o_ref):
    wf = wf_ref[...]
    o_ref[:, 0:FOURIER_GROUP] = jnp.dot(cs_ref[0], wf, preferred_element_type=F32,
                                        precision=lax.Precision.HIGHEST)
    o_ref[:, FOURIER_GROUP:] = jnp.dot(cs_ref[1], wf, preferred_element_type=F32,
                                       precision=lax.Precision.HIGHEST)


def _fold_channel_dft(w_fourier_j, seq):
    fg = FOURIER_GROUP
    idx = np.arange(fg)
    cc, sc = _angles(idx[:, None] * idx[None, :], fg)
    cs = jnp.asarray(np.stack([cc, sc]) / math.sqrt(seq * fg), F32)
    return pl.pallas_call(
        _fold_cs_kernel,
        grid=(FOURIER_GROUPS,),
        in_specs=[pl.BlockSpec((2, fg, fg), lambda g: (0, 0, 0)),
                  pl.BlockSpec((None, fg, fg), lambda g: (g, 0, 0))],
        out_specs=pl.BlockSpec((None, fg, 2 * fg), lambda g: (g, 0, 0)),
        out_shape=jax.ShapeDtypeStruct((FOURIER_GROUPS, fg, 2 * fg), F32),
        compiler_params=_params(1),
        name="fold_channel_dft",
    )(cs, w_fourier_j)


def _fold_win_kernel(w_ref, ab_ref, o_ref):
    o_ref[...] = jnp.dot(w_ref[...], ab_ref[...].astype(BF16), preferred_element_type=F32).astype(BF16)


def _fold_in_proj(w_in, layer, ab):
    d = w_in.shape[1]
    fg = FOURIER_GROUP
    return pl.pallas_call(
        _fold_win_kernel,
        grid=(FOURIER_GROUPS, 2),
        in_specs=[pl.BlockSpec((None, d, fg), lambda g, t: (layer, 0, g)),
                  pl.BlockSpec((None, fg, fg), lambda g, t: (g, 0, t))],
        out_specs=pl.BlockSpec((d, fg), lambda g, t: (0, g + t * FOURIER_GROUPS)),
        out_shape=jax.ShapeDtypeStruct((d, 2 * D_MODEL), BF16),
        compiler_params=_params(2),
        name="fold_in_proj",
    )(w_in, ab)


HIGH_HALF = -65536


def _bf16_bits(v):
    b = lax.bitcast_convert_type(v, jnp.int32)
    return b + 0x7FFF + (lax.shift_right_logical(b, 16) & 1)


def _dft1_kernel(h_ref, wp_ref, wq_ref, f_ref, t_ref, *, n1, per_sub, n_sub):
    f = f_ref[...]
    for sb in range(n_sub):
        rows = slice(sb * per_sub * n1, (sb + 1) * per_sub * n1)
        h = h_ref[rows, :]
        p = jnp.dot(h, wp_ref[...], preferred_element_type=F32).astype(BF16)
        q = jnp.dot(h, wq_ref[...], preferred_element_type=F32).astype(BF16)
        for jj in range(per_sub):
            r = slice(jj * n1, (jj + 1) * n1)
            t = jnp.dot(f, jnp.concatenate([p[r], q[r]], axis=0), preferred_element_type=F32)
            t_ref[sb * per_sub + jj] = ((_bf16_bits(t[0:n1]) & HIGH_HALF)
                                        | lax.shift_right_logical(_bf16_bits(t[n1:2 * n1]), 16))


def _dft_stage1(h_t, w_pq, f1, n1, n2):
    s, d = h_t.shape
    c = w_pq.shape[1] // 2
    tm = min(PROJ_TM, s)
    tn = PROJ_TN
    per_sub = max(1, 256 // n1)
    n_sub = tm // (per_sub * n1)
    ncb = c // tn
    return pl.pallas_call(
        functools.partial(_dft1_kernel, n1=n1, per_sub=per_sub, n_sub=n_sub),
        grid=(s // tm, ncb),
        in_specs=[pl.BlockSpec((tm, d), lambda i, j: (i, 0)),
                  pl.BlockSpec((d, tn), lambda i, j: (0, j)),
                  pl.BlockSpec((d, tn), lambda i, j: (0, j + ncb)),
                  pl.BlockSpec((2 * n1, 2 * n1), lambda i, j: (0, 0))],
        out_specs=pl.BlockSpec((tm // n1, n1, tn), lambda i, j: (i, 0, j)),
        out_shape=jax.ShapeDtypeStruct((n2, n1, c), jnp.int32),
        compiler_params=_params(2),
        name="fourier_in_dft1",
    )(h_t, w_pq, w_pq, f1)


def _dft2_kernel(l_ref, t_ref, o_ref, *, n2, cb):
    packed = t_ref[...]
    re = lax.bitcast_convert_type(packed & HIGH_HALF, F32)
    im = lax.bitcast_convert_type(lax.shift_left(packed, 16), F32)
    rhs = jnp.stack([re, im], axis=1).reshape(n2 * 2 * SUBLANES, cb).astype(BF16)
    o_ref[...] = jnp.dot(l_ref[0], rhs, preferred_element_type=F32).reshape(n2, SUBLANES, cb)


def _dft_stage2(t_nk, l2, n1, n2):
    c = t_nk.shape[-1]
    cb = min(DFT_CB, c)
    row_blocks = pl.BlockSpec((n2, SUBLANES, cb), lambda g, j: (0, g, j))
    out = pl.pallas_call(
        functools.partial(_dft2_kernel, n2=n2, cb=cb),
        grid=(n1 // SUBLANES, c // cb),
        in_specs=[pl.BlockSpec((1, SUBLANES * n2, SUBLANES * 2 * n2), lambda g, j: (g, 0, 0)),
                  row_blocks],
        out_specs=row_blocks,
        out_shape=jax.ShapeDtypeStruct((n2, n1, c), F32),
        compiler_params=_params(2),
        name="dft_stage2",
    )(l2, t_nk)
    return out.reshape(n1 * n2, c)


def _dft_matrices(s):
    n1, n2 = _dft_split(s)
    a = np.arange(n1)
    c1, s1 = _angles(a[:, None] * a[None, :], n1)
    f1 = jnp.asarray(np.block([[c1, -s1], [-s1, -c1]]).astype(BF16))
    groups = n1 // SUBLANES
    g = np.arange(groups)[:, None, None, None]
    k2 = np.arange(n2)[None, :, None, None]
    j = np.arange(SUBLANES)[None, None, :, None]
    nn = np.arange(n2)[None, None, None, :]
    c2, s2 = _angles((SUBLANES * g + j + n1 * k2) * nn, s)
    cs = np.concatenate([c2, s2], axis=3).astype(BF16)
    return n1, n2, f1, _expand_twiddles(jnp.asarray(cs.reshape(groups, n2 * SUBLANES, 2 * n2)))


def _expand_kernel(cs_ref, e_ref, o_ref):
    spread = jnp.dot(cs_ref[0], e_ref[...], preferred_element_type=F32)
    row = lax.broadcasted_iota(jnp.int32, spread.shape, 0)
    col = lax.broadcasted_iota(jnp.int32, spread.shape, 1)
    o_ref[0] = jnp.where(row % SUBLANES == col % SUBLANES, spread, 0.0).astype(BF16)


def _expand_twiddles(cs):
    groups, rows, cols = cs.shape
    wide = cols * SUBLANES
    src = jnp.arange(cols, dtype=jnp.int32)[:, None]
    slot = (src % (cols // 2)) * 2 + src // (cols // 2)
    e = jnp.arange(wide, dtype=jnp.int32)[None, :] // SUBLANES == slot
    return pl.pallas_call(
        _expand_kernel,
        grid=(groups,),
        in_specs=[pl.BlockSpec((1, rows, cols), lambda g: (g, 0, 0)),
                  pl.BlockSpec((cols, wide), lambda g: (0, 0))],
        out_specs=pl.BlockSpec((1, rows, wide), lambda g: (g, 0, 0)),
        out_shape=jax.ShapeDtypeStruct((groups, rows, wide), BF16),
        compiler_params=_params(1),
        name="expand_twiddles",
    )(cs, e.astype(BF16))


def kernel(x, c, norm_w, ada_w, ada_b, w_in_ab, w_pool, pool_scale, q_norm_w, k_norm_w, lambda_q1, lambda_k1,
           lambda_q2, lambda_k2, subln_w, w_out_ab, w_in_c, w_fourier, w_out_c):
    batch, s, d = x.shape
    assert batch == 1 and d == D_MODEL and s % ATT_TK == 0
    xs = x.reshape(s, d)
    mod = _ada_mod(c, ada_w, ada_b)
    n1, n2, f1, l2 = _dft_matrices(s)
    row = lambda v: v.reshape(1, -1)

    def mod_of(i):
        return mod[i, :, 0:d], mod[i, :, d:2 * d], mod[i, :, 2 * d:3 * d]

    def finish(ys, w_out, j, xs, i, gate_from=None):
        gate = mod_of(i)[2]
        if i + 1 == DEPTH:
            return _out_proj(ys, w_out, j, xs, gate, gate_from=gate_from), None
        shift, scale, _ = mod_of(i + 1)
        return _out_proj(ys, w_out, j, xs, gate, (row(norm_w[i + 1]), shift, scale), gate_from=gate_from)

    w_in_ab, w_out_ab, w_in_c, w_out_c = (w.astype(BF16) for w in (w_in_ab, w_out_ab, w_in_c, w_out_c))
    h = _modulate(xs, row(norm_w[0]), *mod_of(0)[:2])
    for i in range(DEPTH):
        j = i // 2
        if i % 2 == 0:
            o1, o2, o3, o4 = POOL_WIDTH, POOL_WIDTH + DIFF_WIDTH, POOL_WIDTH + 2 * DIFF_WIDTH, POOL_WIDTH + 3 * DIFF_WIDTH
            lambda_init = 0.8 - 0.6 * math.exp(-0.3 * i)
            y_a = _pool_branch(h, w_in_ab, j, 0, o4, w_pool[j].astype(BF16), row(pool_scale[j]))
            qt, ka, vt, g_attn = _qkv_proj(h, w_in_ab, j, o1, o2, o3, o4 + POOL_WIDTH,
                                           row(q_norm_w[j]), row(k_norm_w[j]))
            y_b = _diff_attn(qt, ka, vt, row(lambda_q1[j]), row(lambda_k1[j]), row(lambda_q2[j]),
                             row(lambda_k2[j]), subln_w[j], g_attn, lambda_init)
            xs, h = finish([y_a, y_b], w_out_ab, j, xs, i)
        else:
            ab = _fold_channel_dft(w_fourier[j], s)
            w_pq = _fold_in_proj(w_in_c, j, ab)
            h_t = h.reshape(n1, n2, d).transpose(1, 0, 2).reshape(s, d)
            t_nk = _dft_stage1(h_t, w_pq, f1, n1, n2)
            f = _dft_stage2(t_nk, l2, n1, n2)
            xs, h = finish([f], w_out_c, j, xs, i, gate_from=(h, w_in_c, d))
    return xs.reshape(batch, s, d)
```
